```python
import math
import jax, jax.numpy as jnp
from jax import lax
import numpy as np

D_MODEL = 1024
BATCH = 16
SEQ = 4096
DEPTH = 2

N_HEADS = 16
HEAD_DIM = D_MODEL // N_HEADS
D_FF = ((8 * D_MODEL // 3) + 255) // 256 * 256
CONV_WIDTH = 3
DILATED_BRANCHES = ((128, 1), (512, 4), (2048, 16))
BLOCK = 128
REL_BUCKETS = 32
REL_MAX_DISTANCE = 2048
N_A_LAYERS = DEPTH // 2
N_B_LAYERS = DEPTH - N_A_LAYERS
RMS_EPS = 1e-6

kernel_name = "yoco_shortconv_dilated_attention_trunk"


def rmsnorm(x, g):
    xf = x.astype(jnp.float32)
    y = xf * lax.rsqrt(jnp.mean(xf * xf, axis=-1, keepdims=True) + RMS_EPS)
    return (y * g.astype(jnp.float32)).astype(x.dtype)


def causal_dwconv(x, w, b=None):
    S = x.shape[1]
    xp = jnp.pad(x, ((0, 0), (CONV_WIDTH - 1, 0), (0, 0)))
    y = xp[:, 0:S] * w[0]
    for tap in range(1, CONV_WIDTH):
        y = y + xp[:, tap:tap + S] * w[tap]
    if b is not None:
        y = y + b
    return y


def t5_bucket(dist):
    max_exact = REL_BUCKETS // 2
    n = jnp.maximum(dist, 0)
    nf = jnp.maximum(n, max_exact).astype(jnp.float32)
    large = max_exact + (jnp.log(nf / max_exact) / math.log(REL_MAX_DISTANCE / max_exact)
                         * (REL_BUCKETS - max_exact)).astype(jnp.int32)
    large = jnp.minimum(large, REL_BUCKETS - 1)
    return jnp.where(n < max_exact, n, large)


def short_conv_mixer(xn, w_in, conv_w, w_out):
    b_gate, c_gate, h = jnp.split(xn @ w_in, 3, axis=-1)
    return (b_gate * causal_dwconv(c_gate * h, conv_w)) @ w_out


def conv_ffn(xn, w_up, conv_w, conv_b, w_down):
    u = causal_dwconv(xn @ w_up, conv_w, conv_b)
    g, up = jnp.split(u, 2, axis=-1)
    return (jax.nn.silu(g) * up) @ w_down


def dilated_branch(q, k, v, rel_bias, window, dilation):
    B, S, H, Dh = q.shape
    P = BLOCK
    W = window // dilation
    L = S // dilation
    Lp = -(-L // P) * P
    nb = Lp // P

    def to_sub(t):
        return t.reshape(B, L, dilation, H, Dh).transpose(2, 0, 1, 3, 4)

    def to_blocks(t):
        return (t.reshape(dilation, B, nb, P, H, Dh).transpose(0, 2, 1, 3, 4, 5)
                .reshape(dilation * nb, B, P, H, Dh))

    qs = jnp.pad(to_sub(q), ((0, 0), (0, 0), (0, Lp - L), (0, 0), (0, 0)))
    ks = jnp.pad(to_sub(k), ((0, 0), (0, 0), (P, Lp - L), (0, 0), (0, 0)))
    vs = jnp.pad(to_sub(v), ((0, 0), (0, 0), (P, Lp - L), (0, 0), (0, 0)))
    q_blk = to_blocks(qs)
    k_prev, k_cur = to_blocks(ks[:, :, :Lp]), to_blocks(ks[:, :, P:])
    v_prev, v_cur = to_blocks(vs[:, :, :Lp]), to_blocks(vs[:, :, P:])
    blk_idx = jnp.tile(jnp.arange(nb, dtype=jnp.int32), dilation)

    qi = jnp.arange(P, dtype=jnp.int32)[:, None]
    kc = jnp.arange(2 * P, dtype=jnp.int32)[None, :]
    delta = qi + P - kc
    band = (delta >= 0) & (delta <= W)
    bias = rel_bias[t5_bucket(delta * dilation)].astype(jnp.float32).transpose(2, 0, 1)
    scale = HEAD_DIM ** -0.5

    def block_fn(args):
        qb, kp, kcur, vp, vcur, j = args
        kw = jnp.concatenate([kp, kcur], axis=1).astype(jnp.float32)
        vw = jnp.concatenate([vp, vcur], axis=1).astype(jnp.float32)
        s = jnp.einsum('bqhd,bkhd->bhqk', qb.astype(jnp.float32), kw) * scale + bias
        valid = band & ((j * P + kc - P) >= 0)
        s = jnp.where(valid, s, -jnp.inf)
        m = jnp.max(s, axis=-1)
        p = jnp.exp(s - m[..., None])
        den = jnp.sum(p, axis=-1)
        num = jnp.einsum('bhqk,bkhd->bqhd', p, vw)
        return num, den.transpose(0, 2, 1), m.transpose(0, 2, 1)

    num, den, mx = lax.map(block_fn, (q_blk, k_prev, k_cur, v_prev, v_cur, blk_idx))

    def from_blocks(t):
        t = t.reshape((dilation, nb, B, P) + t.shape[3:])
        t = jnp.moveaxis(jnp.moveaxis(t, 0, 3), 0, 1)
        t = t.reshape((B, Lp, dilation) + t.shape[4:])[:, :L]
        return t.reshape((B, S) + t.shape[3:])

    return from_blocks(num), from_blocks(den), from_blocks(mx)


def dilated_attention(xn, w_q, w_o, k, v, rel_bias):
    B, S, _ = xn.shape
    q = (xn @ w_q).reshape(B, S, N_HEADS, HEAD_DIM)
    branches = [dilated_branch(q, k, v, rel_bias, w, d) for (w, d) in DILATED_BRANCHES]
    m_all = jnp.max(jnp.stack([br[2] for br in branches]), axis=0)
    num_tot = jnp.zeros(q.shape, jnp.float32)
    den_tot = jnp.zeros(m_all.shape, jnp.float32)
    for num, den, mx in branches:
        wgt = jnp.exp(mx - m_all)
        num_tot = num_tot + wgt[..., None] * num
        den_tot = den_tot + wgt * den
    out = (num_tot / den_tot[..., None]).astype(xn.dtype).reshape(B, S, D_MODEL)
    return out @ w_o


def _fwd_setup_inputs(seed: int = 0) -> dict:
    key = jax.random.key(seed)
    ks = jax.random.split(key, 17)
    f32 = jnp.float32
    D, F = D_MODEL, D_FF

    def nrm(k, shape, scale):
        return jax.random.normal(k, shape, f32) * scale

    def gain(k, shape):
        return 1.0 + 0.02 * jax.random.normal(k, shape, f32)

    return {
        "x": nrm(ks[0], (BATCH, SEQ, D), 1.0),
        "a_norm": gain(ks[1], (N_A_LAYERS, D)),
        "a_w_in": nrm(ks[2], (N_A_LAYERS, D, 3 * D), D ** -0.5),
        "a_conv": nrm(ks[3], (N_A_LAYERS, CONV_WIDTH, D), CONV_WIDTH ** -0.5),
        "a_w_out": nrm(ks[4], (N_A_LAYERS, D, D), D ** -0.5),
        "kv_norm": gain(ks[5], (D,)),
        "w_kv": nrm(ks[6], (D, 2 * D), D ** -0.5),
        "b_norm": gain(ks[7], (N_B_LAYERS, D)),
        "b_w_q": nrm(ks[8], (N_B_LAYERS, D, D), D ** -0.5),
        "b_w_o": nrm(ks[9], (N_B_LAYERS, D, D), D ** -0.5),
        "rel_bias": nrm(ks[10], (REL_BUCKETS, N_HEADS), 0.5),
        "ffn_norm": gain(ks[11], (DEPTH, D)),
        "ffn_w_up": nrm(ks[12], (DEPTH, D, 2 * F), D ** -0.5),
        "ffn_conv": nrm(ks[13], (DEPTH, CONV_WIDTH, 2 * F), CONV_WIDTH ** -0.5),
        "ffn_conv_b": nrm(ks[14], (DEPTH, 2 * F), 0.02),
        "ffn_w_down": nrm(ks[15], (DEPTH, F, D), F ** -0.5),
        "final_norm": gain(ks[16], (D,)),
    }


def _fwd_reference(x, a_norm, a_w_in, a_conv, a_w_out, kv_norm, w_kv, b_norm, b_w_q, b_w_o, rel_bias,
              ffn_norm, ffn_w_up, ffn_conv, ffn_conv_b, ffn_w_down, final_norm):
    B, S, _ = x.shape
    h = x
    k = v = None
    for l in range(DEPTH):
        if l < N_A_LAYERS:
            h = h + short_conv_mixer(rmsnorm(h, a_norm[l]), a_w_in[l], a_conv[l], a_w_out[l])
        else:
            j = l - N_A_LAYERS
            h = h + dilated_attention(rmsnorm(h, b_norm[j]), b_w_q[j], b_w_o[j], k, v, rel_bias)
        h = h + conv_ffn(rmsnorm(h, ffn_norm[l]), ffn_w_up[l], ffn_conv[l], ffn_conv_b[l], ffn_w_down[l])
        if l == N_A_LAYERS - 1:
            k_flat, v_flat = jnp.split(rmsnorm(h, kv_norm) @ w_kv, 2, axis=-1)
            k = k_flat.reshape(B, S, N_HEADS, HEAD_DIM)
            v = v_flat.reshape(B, S, N_HEADS, HEAD_DIM)
    return rmsnorm(h, final_norm)


import jax as _jax
import jax.numpy as _jnp

TWIN_FORMAT = 'train_step'
FWD_PARAMS = ['x', 'a_norm', 'a_w_in', 'a_conv', 'a_w_out', 'kv_norm', 'w_kv', 'b_norm', 'b_w_q', 'b_w_o', 'rel_bias', 'ffn_norm', 'ffn_w_up', 'ffn_conv', 'ffn_conv_b', 'ffn_w_down', 'final_norm']
TWIN_WEIGHTS = ['a_norm', 'a_w_in', 'a_conv', 'a_w_out', 'kv_norm', 'w_kv', 'b_norm', 'b_w_q', 'b_w_o', 'rel_bias', 'ffn_norm', 'ffn_w_up', 'ffn_conv', 'ffn_conv_b', 'ffn_w_down', 'final_norm']
TWIN_DIFF_INPUT = 'x'
TWIN_INPUTS = ['x', 'a_norm', 'a_w_in', 'a_conv', 'a_w_out', 'kv_norm', 'w_kv', 'b_norm', 'b_w_q', 'b_w_o', 'rel_bias', 'ffn_norm', 'ffn_w_up', 'ffn_conv', 'ffn_conv_b', 'ffn_w_down', 'final_norm', 'loss_target', 'm_a_norm', 'm_a_w_in', 'm_a_conv', 'm_a_w_out', 'm_kv_norm', 'm_w_kv', 'm_b_norm', 'm_b_w_q', 'm_b_w_o', 'm_rel_bias', 'm_ffn_norm', 'm_ffn_w_up', 'm_ffn_conv', 'm_ffn_conv_b', 'm_ffn_w_down', 'm_final_norm', 'v_a_norm', 'v_a_w_in', 'v_a_conv', 'v_a_w_out', 'v_kv_norm', 'v_w_kv', 'v_b_norm', 'v_b_w_q', 'v_b_w_o', 'v_rel_bias', 'v_ffn_norm', 'v_ffn_w_up', 'v_ffn_conv', 'v_ffn_conv_b', 'v_ffn_w_down', 'v_final_norm']
TWIN_OUTPUTS = ['loss', 'grad_x', 'grad_a_norm', 'grad_a_w_in', 'grad_a_conv', 'grad_a_w_out', 'grad_kv_norm', 'grad_w_kv', 'grad_b_norm', 'grad_b_w_q', 'grad_b_w_o', 'grad_rel_bias', 'grad_ffn_norm', 'grad_ffn_w_up', 'grad_ffn_conv', 'grad_ffn_conv_b', 'grad_ffn_w_down', 'grad_final_norm', 'delta_a_norm', 'delta_a_w_in', 'delta_a_conv', 'delta_a_w_out', 'delta_kv_norm', 'delta_w_kv', 'delta_b_norm', 'delta_b_w_q', 'delta_b_w_o', 'delta_rel_bias', 'delta_ffn_norm', 'delta_ffn_w_up', 'delta_ffn_conv', 'delta_ffn_conv_b', 'delta_ffn_w_down', 'delta_final_norm', 'new_m_a_norm', 'new_m_a_w_in', 'new_m_a_conv', 'new_m_a_w_out', 'new_m_kv_norm', 'new_m_w_kv', 'new_m_b_norm', 'new_m_b_w_q', 'new_m_b_w_o', 'new_m_rel_bias', 'new_m_ffn_norm', 'new_m_ffn_w_up', 'new_m_ffn_conv', 'new_m_ffn_conv_b', 'new_m_ffn_w_down', 'new_m_final_norm', 'new_v_a_norm', 'new_v_a_w_in', 'new_v_a_conv', 'new_v_a_w_out', 'new_v_kv_norm', 'new_v_w_kv', 'new_v_b_norm', 'new_v_b_w_q', 'new_v_b_w_o', 'new_v_rel_bias', 'new_v_ffn_norm', 'new_v_ffn_w_up', 'new_v_ffn_conv', 'new_v_ffn_conv_b', 'new_v_ffn_w_down', 'new_v_final_norm']
TWIN_LEAF_KINDS = {'loss': 'loss', 'grad_x': 'grad_x', 'grad_a_norm': 'grad_w', 'grad_a_w_in': 'grad_w', 'grad_a_conv': 'grad_w', 'grad_a_w_out': 'grad_w', 'grad_kv_norm': 'grad_w', 'grad_w_kv': 'grad_w', 'grad_b_norm': 'grad_w', 'grad_b_w_q': 'grad_w', 'grad_b_w_o': 'grad_w', 'grad_rel_bias': 'grad_w', 'grad_ffn_norm': 'grad_w', 'grad_ffn_w_up': 'grad_w', 'grad_ffn_conv': 'grad_w', 'grad_ffn_conv_b': 'grad_w', 'grad_ffn_w_down': 'grad_w', 'grad_final_norm': 'grad_w', 'delta_a_norm': 'delta_w', 'delta_a_w_in': 'delta_w', 'delta_a_conv': 'delta_w', 'delta_a_w_out': 'delta_w', 'delta_kv_norm': 'delta_w', 'delta_w_kv': 'delta_w', 'delta_b_norm': 'delta_w', 'delta_b_w_q': 'delta_w', 'delta_b_w_o': 'delta_w', 'delta_rel_bias': 'delta_w', 'delta_ffn_norm': 'delta_w', 'delta_ffn_w_up': 'delta_w', 'delta_ffn_conv': 'delta_w', 'delta_ffn_conv_b': 'delta_w', 'delta_ffn_w_down': 'delta_w', 'delta_final_norm': 'delta_w', 'new_m_a_norm': 'new_m', 'new_m_a_w_in': 'new_m', 'new_m_a_conv': 'new_m', 'new_m_a_w_out': 'new_m', 'new_m_kv_norm': 'new_m', 'new_m_w_kv': 'new_m', 'new_m_b_norm': 'new_m', 'new_m_b_w_q': 'new_m', 'new_m_b_w_o': 'new_m', 'new_m_rel_bias': 'new_m', 'new_m_ffn_norm': 'new_m', 'new_m_ffn_w_up': 'new_m', 'new_m_ffn_conv': 'new_m', 'new_m_ffn_conv_b': 'new_m', 'new_m_ffn_w_down': 'new_m', 'new_m_final_norm': 'new_m', 'new_v_a_norm': 'new_v', 'new_v_a_w_in': 'new_v', 'new_v_a_conv': 'new_v', 'new_v_a_w_out': 'new_v', 'new_v_kv_norm': 'new_v', 'new_v_w_kv': 'new_v', 'new_v_b_norm': 'new_v', 'new_v_b_w_q': 'new_v', 'new_v_b_w_o': 'new_v', 'new_v_rel_bias': 'new_v', 'new_v_ffn_norm': 'new_v', 'new_v_ffn_w_up': 'new_v', 'new_v_ffn_conv': 'new_v', 'new_v_ffn_conv_b': 'new_v', 'new_v_ffn_w_down': 'new_v', 'new_v_final_norm': 'new_v'}


def _forward(args):
    return _fwd_reference(*[args[k] for k in FWD_PARAMS])


def _output_shape():
    out = _jax.eval_shape(lambda: _forward(_fwd_setup_inputs(0)))
    return out.shape, out.dtype

N_MICROBATCH = 1
ADAM_LR = 0.001
ADAM_B1 = 0.9
ADAM_B2 = 0.999
ADAM_EPS = 1e-08
ADAM_WD = 0.01
ADAM_STEP = 10
PER_EXAMPLE_BATCH_AXIS = {'x': 0, 'loss_target': 0}
SHARED_INPUTS = []
_WEIGHT_DTYPES = {'a_norm': _jnp.float32, 'a_w_in': _jnp.float32, 'a_conv': _jnp.float32, 'a_w_out': _jnp.float32, 'kv_norm': _jnp.float32, 'w_kv': _jnp.float32, 'b_norm': _jnp.float32, 'b_w_q': _jnp.float32, 'b_w_o': _jnp.float32, 'rel_bias': _jnp.float32, 'ffn_norm': _jnp.float32, 'ffn_w_up': _jnp.float32, 'ffn_conv': _jnp.float32, 'ffn_conv_b': _jnp.float32, 'ffn_w_down': _jnp.float32, 'final_norm': _jnp.float32}
MOMENT_SCALE = {'a_norm': 3.585491e-01, 'a_w_in': 2.072651e-01, 'a_conv': 2.147408e-01, 'a_w_out': 2.060846e-01, 'kv_norm': 5.365747e-02, 'w_kv': 3.847725e-02, 'b_norm': 3.352777e-02, 'b_w_q': 3.385321e-02, 'b_w_o': 4.221276e-02, 'rel_bias': 4.910118e-02, 'ffn_norm': 1.459793e-01, 'ffn_w_up': 6.281901e-02, 'ffn_conv': 6.370323e-02, 'ffn_conv_b': 6.137096e-02, 'ffn_w_down': 1.023815e-01, 'final_norm': 6.395694e+01}


def _to_microbatches(a, axis):
    t = _jnp.moveaxis(a, axis, 0)
    t = t.reshape((N_MICROBATCH, t.shape[0] // N_MICROBATCH) + t.shape[1:])
    return _jnp.moveaxis(t, 1, axis + 1)


def setup_inputs(seed: int = 0) -> dict:
    inp = _fwd_setup_inputs(seed)
    key = _jax.random.fold_in(_jax.random.key(seed), 7919)
    shape, _ = _output_shape()
    out = dict(inp)
    out["loss_target"] = _jax.random.normal(_jax.random.fold_in(key, 0), shape, _jnp.float32)
    for i, name in enumerate(TWIN_WEIGHTS):
        w = inp[name].astype(_jnp.float32)
        if MOMENT_SCALE is None:
            s = _jnp.sqrt(_jnp.mean(_jnp.square(w)) + 1e-30)
        else:
            s = MOMENT_SCALE[name]
        km, kv = _jax.random.split(_jax.random.fold_in(key, i + 1))
        out[name] = w
        out["m_" + name] = s * _jax.random.normal(km, w.shape, _jnp.float32)
        out["v_" + name] = (s * s) * _jax.random.uniform(kv, w.shape, _jnp.float32, 0.5, 1.5)
    if N_MICROBATCH > 1:
        for name, axis in PER_EXAMPLE_BATCH_AXIS.items():
            out[name] = _to_microbatches(out[name], axis)
    return {'x': out['x'], 'a_norm': out['a_norm'], 'a_w_in': out['a_w_in'], 'a_conv': out['a_conv'], 'a_w_out': out['a_w_out'], 'kv_norm': out['kv_norm'], 'w_kv': out['w_kv'], 'b_norm': out['b_norm'], 'b_w_q': out['b_w_q'], 'b_w_o': out['b_w_o'], 'rel_bias': out['rel_bias'], 'ffn_norm': out['ffn_norm'], 'ffn_w_up': out['ffn_w_up'], 'ffn_conv': out['ffn_conv'], 'ffn_conv_b': out['ffn_conv_b'], 'ffn_w_down': out['ffn_w_down'], 'final_norm': out['final_norm'], 'loss_target': out['loss_target'], 'm_a_norm': out['m_a_norm'], 'm_a_w_in': out['m_a_w_in'], 'm_a_conv': out['m_a_conv'], 'm_a_w_out': out['m_a_w_out'], 'm_kv_norm': out['m_kv_norm'], 'm_w_kv': out['m_w_kv'], 'm_b_norm': out['m_b_norm'], 'm_b_w_q': out['m_b_w_q'], 'm_b_w_o': out['m_b_w_o'], 'm_rel_bias': out['m_rel_bias'], 'm_ffn_norm': out['m_ffn_norm'], 'm_ffn_w_up': out['m_ffn_w_up'], 'm_ffn_conv': out['m_ffn_conv'], 'm_ffn_conv_b': out['m_ffn_conv_b'], 'm_ffn_w_down': out['m_ffn_w_down'], 'm_final_norm': out['m_final_norm'], 'v_a_norm': out['v_a_norm'], 'v_a_w_in': out['v_a_w_in'], 'v_a_conv': out['v_a_conv'], 'v_a_w_out': out['v_a_w_out'], 'v_kv_norm': out['v_kv_norm'], 'v_w_kv': out['v_w_kv'], 'v_b_norm': out['v_b_norm'], 'v_b_w_q': out['v_b_w_q'], 'v_b_w_o': out['v_b_w_o'], 'v_rel_bias': out['v_rel_bias'], 'v_ffn_norm': out['v_ffn_norm'], 'v_ffn_w_up': out['v_ffn_w_up'], 'v_ffn_conv': out['v_ffn_conv'], 'v_ffn_conv_b': out['v_ffn_conv_b'], 'v_ffn_w_down': out['v_ffn_w_down'], 'v_final_norm': out['v_final_norm']}


def _loss(weights, diff, rest, loss_target):
    with _jax.named_scope("forward"):
        args = {**rest, TWIN_DIFF_INPUT: diff, **{k: w.astype(_WEIGHT_DTYPES[k]) for k, w in weights.items()}}
        y = _forward(args)
    with _jax.named_scope("loss_head"):
        err = _jnp.square(y.astype(_jnp.float32) - loss_target)
        return 0.5 * _jnp.sum(_jnp.mean(err, axis=-1)) if err.ndim else 0.5 * err


def _adamw(w, g, m, v):
    m = ADAM_B1 * m + (1.0 - ADAM_B1) * g
    v = ADAM_B2 * v + (1.0 - ADAM_B2) * _jnp.square(g)
    m_hat = m / (1.0 - ADAM_B1 ** ADAM_STEP)
    v_hat = v / (1.0 - ADAM_B2 ** ADAM_STEP)
    delta = -ADAM_LR * (m_hat / (_jnp.sqrt(v_hat) + ADAM_EPS) + ADAM_WD * w)
    return delta, m, v


def reference(x, a_norm, a_w_in, a_conv, a_w_out, kv_norm, w_kv, b_norm, b_w_q, b_w_o, rel_bias, ffn_norm, ffn_w_up, ffn_conv, ffn_conv_b, ffn_w_down, final_norm, loss_target, m_a_norm, m_a_w_in, m_a_conv, m_a_w_out, m_kv_norm, m_w_kv, m_b_norm, m_b_w_q, m_b_w_o, m_rel_bias, m_ffn_norm, m_ffn_w_up, m_ffn_conv, m_ffn_conv_b, m_ffn_w_down, m_final_norm, v_a_norm, v_a_w_in, v_a_conv, v_a_w_out, v_kv_norm, v_w_kv, v_b_norm, v_b_w_q, v_b_w_o, v_rel_bias, v_ffn_norm, v_ffn_w_up, v_ffn_conv, v_ffn_conv_b, v_ffn_w_down, v_final_norm):
    given = dict(x=x, a_norm=a_norm, a_w_in=a_w_in, a_conv=a_conv, a_w_out=a_w_out, kv_norm=kv_norm, w_kv=w_kv, b_norm=b_norm, b_w_q=b_w_q, b_w_o=b_w_o, rel_bias=rel_bias, ffn_norm=ffn_norm, ffn_w_up=ffn_w_up, ffn_conv=ffn_conv, ffn_conv_b=ffn_conv_b, ffn_w_down=ffn_w_down, final_norm=final_norm, loss_target=loss_target, m_a_norm=m_a_norm, m_a_w_in=m_a_w_in, m_a_conv=m_a_conv, m_a_w_out=m_a_w_out, m_kv_norm=m_kv_norm, m_w_kv=m_w_kv, m_b_norm=m_b_norm, m_b_w_q=m_b_w_q, m_b_w_o=m_b_w_o, m_rel_bias=m_rel_bias, m_ffn_norm=m_ffn_norm, m_ffn_w_up=m_ffn_w_up, m_ffn_conv=m_ffn_conv, m_ffn_conv_b=m_ffn_conv_b, m_ffn_w_down=m_ffn_w_down, m_final_norm=m_final_norm, v_a_norm=v_a_norm, v_a_w_in=v_a_w_in, v_a_conv=v_a_conv, v_a_w_out=v_a_w_out, v_kv_norm=v_kv_norm, v_w_kv=v_w_kv, v_b_norm=v_b_norm, v_b_w_q=v_b_w_q, v_b_w_o=v_b_w_o, v_rel_bias=v_rel_bias, v_ffn_norm=v_ffn_norm, v_ffn_w_up=v_ffn_w_up, v_ffn_conv=v_ffn_conv, v_ffn_conv_b=v_ffn_conv_b, v_ffn_w_down=v_ffn_w_down, v_final_norm=v_final_norm)
    weights = {n: given[n] for n in TWIN_WEIGHTS}
    shared = {n: given[n] for n in SHARED_INPUTS}
    per_example = {n: given[n] for n in ['x']}
    grad_fn = _jax.value_and_grad(_loss, argnums=(0, 1))

    def one_microbatch(ex, loss_target):
        ex = dict(ex)
        diff = ex.pop(TWIN_DIFF_INPUT)
        return grad_fn(weights, diff, {**shared, **ex}, loss_target)

    if N_MICROBATCH == 1:
        loss, (grad_w, grad_x) = one_microbatch(per_example, given["loss_target"])
    else:
        def body(carry, xs):
            loss_sum, grad_sum = carry
            l_k, (gw_k, gx_k) = one_microbatch(xs[0], xs[1])
            with _jax.named_scope("update"):
                return (loss_sum + l_k, _jax.tree.map(_jnp.add, grad_sum, gw_k)), gx_k

        init = (_jnp.zeros((), _jnp.float32), _jax.tree.map(_jnp.zeros_like, weights))
        (loss, grad_w), grad_x = _jax.lax.scan(body, init, (per_example, given["loss_target"]))
    with _jax.named_scope("update"):
        delta_w, new_m, new_v = {}, {}, {}
        for n in TWIN_WEIGHTS:
            delta_w[n], new_m[n], new_v[n] = _adamw(weights[n], grad_w[n], given["m_" + n], given["v_" + n])
    return (loss, grad_x, *[grad_w[n] for n in TWIN_WEIGHTS], *[delta_w[n] for n in TWIN_WEIGHTS],
            *[new_m[n] for n in TWIN_WEIGHTS], *[new_v[n] for n in TWIN_WEIGHTS])
```

```python
import functools
import math

import numpy as np

import jax
import jax.numpy as jnp
from jax import lax
from jax.experimental import pallas as pl
from jax.experimental.pallas import tpu as pltpu

F32 = jnp.float32
BF16 = jnp.bfloat16

RMS_EPS = 1e-6
HEAD_DIM = 64
ATT_BLOCK = 128
DILATED_BRANCHES = ((128, 1), (512, 4), (2048, 16))
REL_BUCKETS = 32
REL_MAX_DISTANCE = 2048
MASKED_LOGIT = -1e30

ADAM_LR = 0.001
ADAM_B1 = 0.9
ADAM_B2 = 0.999
ADAM_EPS = 1e-08
ADAM_WD = 0.01
ADAM_STEP = 10

LANES = 128
SUBLANES_F32 = 8
SUBLANES_BF16 = 16
VMEM_LIMIT_BYTES = 56 * 1024 * 1024

MESH_AXES = ("x", "y", "c")
N_SHARDS = 4
N_DEVICES = 8
ANY = pl.BlockSpec(memory_space=pl.ANY)


def _tile(n, pref, mult):
    best = None
    for t in range(mult, min(n, pref) + 1, mult):
        if n % t == 0:
            best = t
    if best is None:
        raise ValueError(f"no tile for {n} (multiple of {mult}, at most {pref})")
    return best


def _params(*sem):
    return pltpu.CompilerParams(dimension_semantics=sem, vmem_limit_bytes=VMEM_LIMIT_BYTES)


def _rmsnorm_fwd(x, g, name):
    T, D = x.shape
    tm = _tile(T, 512, SUBLANES_BF16)

    def body(x_ref, g_ref, o_ref):
        xf = x_ref[...]
        r = lax.rsqrt(jnp.mean(xf * xf, axis=-1, keepdims=True) + RMS_EPS)
        o_ref[...] = ((xf * r) * g_ref[...]).astype(o_ref.dtype)

    return pl.pallas_call(
        body, name=name, out_shape=jax.ShapeDtypeStruct((T, D), BF16), grid=(T // tm,),
        in_specs=[pl.BlockSpec((tm, D), lambda i: (i, 0)), pl.BlockSpec((1, D), lambda i: (0, 0))],
        out_specs=pl.BlockSpec((tm, D), lambda i: (i, 0)),
        compiler_params=_params("parallel"),
    )(x, g.reshape(1, D))


def _rmsnorm_bwd(x, g, dxn, dres, name):
    T, D = x.shape
    tm = _tile(T, 512, SUBLANES_BF16)

    def body(x_ref, g_ref, dxn_ref, dres_ref, dx_ref, dg_ref):
        @pl.when(pl.program_id(0) == 0)
        def _():
            dg_ref[...] = jnp.zeros_like(dg_ref)

        xf = x_ref[...]
        r = lax.rsqrt(jnp.mean(xf * xf, axis=-1, keepdims=True) + RMS_EPS)
        xhat = xf * r
        dy = dxn_ref[...].astype(F32)
        dg_ref[0:1, :] += jnp.sum(dy * xhat, axis=0, keepdims=True)
        t = dy * g_ref[...]
        dx_ref[...] = dres_ref[...] + r * (t - xhat * jnp.mean(t * xhat, axis=-1, keepdims=True))

    row = pl.BlockSpec((tm, D), lambda i: (i, 0))
    dx, dg = pl.pallas_call(
        body, name=name,
        out_shape=(jax.ShapeDtypeStruct((T, D), F32), jax.ShapeDtypeStruct((SUBLANES_F32, D), F32)),
        grid=(T // tm,),
        in_specs=[row, pl.BlockSpec((1, D), lambda i: (0, 0)), row, row],
        out_specs=(row, pl.BlockSpec((SUBLANES_F32, D), lambda i: (0, 0))),
        compiler_params=_params("arbitrary"),
    )(x, g.reshape(1, D), dxn, dres)
    return dx, dg[0]


def _loss_head(h, g, target, name):
    T, D = h.shape
    tm = _tile(T, 512, SUBLANES_F32)

    def body(h_ref, g_ref, t_ref, dh_ref, acc_ref):
        @pl.when(pl.program_id(0) == 0)
        def _():
            acc_ref[...] = jnp.zeros_like(acc_ref)

        xf = h_ref[...]
        r = lax.rsqrt(jnp.mean(xf * xf, axis=-1, keepdims=True) + RMS_EPS)
        xhat = xf * r
        err = xhat * g_ref[...] - t_ref[...]
        dy = err * (1.0 / D)
        acc_ref[0:1, :] += jnp.sum(dy * xhat, axis=0, keepdims=True)
        acc_ref[1:2, :] += jnp.sum(err * err, axis=0, keepdims=True)
        t = dy * g_ref[...]
        dh_ref[...] = r * (t - xhat * jnp.mean(t * xhat, axis=-1, keepdims=True))

    row = pl.BlockSpec((tm, D), lambda i: (i, 0))
    dh, acc = pl.pallas_call(
        body, name=name,
        out_shape=(jax.ShapeDtypeStruct((T, D), F32), jax.ShapeDtypeStruct((SUBLANES_F32, D), F32)),
        grid=(T // tm,),
        in_specs=[row, pl.BlockSpec((1, D), lambda i: (0, 0)), row],
        out_specs=(row, pl.BlockSpec((SUBLANES_F32, D), lambda i: (0, 0))),
        compiler_params=_params("arbitrary"),
    )(h, g.reshape(1, D), target)
    return acc[1], dh, acc[0]


def _mm_nn(a, b, res, out_dtype, name):
    T, K = a.shape
    N = b.shape[1]
    tm = _tile(T, 512, SUBLANES_BF16)
    tn = _tile(N, 1536, LANES)

    def body(a_ref, b_ref, *rest):
        o_ref = rest[-1]
        acc = jnp.dot(a_ref[...].astype(BF16), b_ref[...], preferred_element_type=F32)
        if res is not None:
            acc = acc + rest[0][...]
        o_ref[...] = acc.astype(o_ref.dtype)

    in_specs = [pl.BlockSpec((tm, K), lambda j, i: (i, 0)), pl.BlockSpec((K, tn), lambda j, i: (0, j))]
    args = [a, b]
    if res is not None:
        in_specs.append(pl.BlockSpec((tm, tn), lambda j, i: (i, j)))
        args.append(res)
    return pl.pallas_call(
        body, name=name, out_shape=jax.ShapeDtypeStruct((T, N), out_dtype), grid=(N // tn, T // tm),
        in_specs=in_specs, out_specs=pl.BlockSpec((tm, tn), lambda j, i: (i, j)),
        compiler_params=_params("parallel", "parallel"),
    )(*args)


def _mm_nt(dy, b, out_dtype, name):
    T, N = dy.shape
    K = b.shape[0]
    tm = _tile(T, 1024, SUBLANES_BF16)
    tk = _tile(K, 1536, LANES)
    tn = _tile(N, 1536, LANES)
    n_steps = N // tn

    def body(dy_ref, b_ref, o_ref, acc_ref):
        n = pl.program_id(2)

        @pl.when(n == 0)
        def _():
            acc_ref[...] = jnp.zeros_like(acc_ref)

        acc_ref[...] += lax.dot_general(dy_ref[...].astype(BF16), b_ref[...], (((1,), (1,)), ((), ())),
                                        preferred_element_type=F32)

        @pl.when(n == n_steps - 1)
        def _():
            o_ref[...] = acc_ref[...].astype(o_ref.dtype)

    return pl.pallas_call(
        body, name=name, out_shape=jax.ShapeDtypeStruct((T, K), out_dtype), grid=(T // tm, K // tk, n_steps),
        in_specs=[pl.BlockSpec((tm, tn), lambda i, k, n: (i, n)), pl.BlockSpec((tk, tn), lambda i, k, n: (k, n))],
        out_specs=pl.BlockSpec((tm, tk), lambda i, k, n: (i, k)),
        scratch_shapes=[pltpu.VMEM((tm, tk), F32)],
        compiler_params=_params("parallel", "parallel", "arbitrary"),
    )(dy, b)


def _mm_tn(a, dy, name):
    T, K = a.shape
    N = dy.shape[1]
    tt = _tile(T, 512, SUBLANES_BF16)
    tk = _tile(K, 1536, LANES)
    tn = _tile(N, 1536, LANES)
    t_steps = T // tt

    def body(a_ref, dy_ref, o_ref, acc_ref):
        t = pl.program_id(2)

        @pl.when(t == 0)
        def _():
            acc_ref[...] = jnp.zeros_like(acc_ref)

        acc_ref[...] += lax.dot_general(a_ref[...].astype(BF16), dy_ref[...].astype(BF16),
                                        (((0,), (0,)), ((), ())), preferred_element_type=F32)

        @pl.when(t == t_steps - 1)
        def _():
            o_ref[...] = acc_ref[...].astype(o_ref.dtype)

    return pl.pallas_call(
        body, name=name, out_shape=jax.ShapeDtypeStruct((K, N), BF16), grid=(K // tk, N // tn, t_steps),
        in_specs=[pl.BlockSpec((tt, tk), lambda k, n, t: (t, k)), pl.BlockSpec((tt, tn), lambda k, n, t: (t, n))],
        out_specs=pl.BlockSpec((tk, tn), lambda k, n, t: (k, n)),
        scratch_shapes=[pltpu.VMEM((tk, tn), F32)],
        compiler_params=_params("parallel", "parallel", "arbitrary"),
    )(a, dy)


def _rows_before(halo, cur, k):
    h = halo.shape[0]
    return pltpu.roll(jnp.concatenate([halo, cur], axis=0), k, 0)[h:]


def _rows_after(cur, halo, k):
    n = cur.shape[0]
    total = n + halo.shape[0]
    return pltpu.roll(jnp.concatenate([cur, halo], axis=0), total - k, 0)[:n]


def _halo_specs(tm, width, n_rows):
    per = tm // SUBLANES_BF16
    last = n_rows // SUBLANES_BF16 - 1
    prev = pl.BlockSpec((SUBLANES_BF16, width), lambda i: (jnp.maximum(i * per - 1, 0), 0))
    nxt = pl.BlockSpec((SUBLANES_BF16, width), lambda i: (jnp.minimum((i + 1) * per, last), 0))
    return prev, nxt


def _gate_a_fwd(p, cw, seq, name):
    T, D3 = p.shape
    D = D3 // 3
    tm = _tile(seq, 512, SUBLANES_BF16)
    cc = _tile(D, 256, LANES)
    prev, _ = _halo_specs(tm, D3, T)

    def body(p_ref, ph_ref, cw_ref, z_ref):
        at_start = (pl.program_id(0) * tm) % seq == 0
        for c0 in range(0, D, cc):
            b = p_ref[:, c0:c0 + cc].astype(F32)
            u = p_ref[:, D + c0:D + c0 + cc].astype(F32) * p_ref[:, 2 * D + c0:2 * D + c0 + cc].astype(F32)
            uh = ph_ref[:, D + c0:D + c0 + cc].astype(F32) * ph_ref[:, 2 * D + c0:2 * D + c0 + cc].astype(F32)
            uh = jnp.where(at_start, 0.0, uh)
            w = cw_ref[:, c0:c0 + cc]
            cv = _rows_before(uh, u, 2) * w[0:1] + _rows_before(uh, u, 1) * w[1:2] + u * w[2:3]
            z_ref[:, c0:c0 + cc] = (b * cv).astype(z_ref.dtype)

    return pl.pallas_call(
        body, name=name, out_shape=jax.ShapeDtypeStruct((T, D), BF16), grid=(T // tm,),
        in_specs=[pl.BlockSpec((tm, D3), lambda i: (i, 0)), prev, pl.BlockSpec((3, D), lambda i: (0, 0))],
        out_specs=pl.BlockSpec((tm, D), lambda i: (i, 0)),
        compiler_params=_params("parallel"),
    )(p, p, cw)


def _gate_a_bwd(p, cw, dz, seq, name):
    T, D3 = p.shape
    D = D3 // 3
    tm = _tile(seq, 512, SUBLANES_BF16)
    cc = _tile(D, 256, LANES)
    p_prev, p_next = _halo_specs(tm, D3, T)
    _, dz_next = _halo_specs(tm, D, T)

    def body(p_ref, pp_ref, pn_ref, dz_ref, dzn_ref, cw_ref, dp_ref, dcw_ref):
        i = pl.program_id(0)

        @pl.when(i == 0)
        def _():
            dcw_ref[...] = jnp.zeros_like(dcw_ref)

        at_start = (i * tm) % seq == 0
        at_end = ((i + 1) * tm) % seq == 0
        for c0 in range(0, D, cc):
            cb, cc_, ch = slice(c0, c0 + cc), slice(D + c0, D + c0 + cc), slice(2 * D + c0, 2 * D + c0 + cc)
            b = p_ref[:, cb].astype(F32)
            c = p_ref[:, cc_].astype(F32)
            hh = p_ref[:, ch].astype(F32)
            u = c * hh
            uh = jnp.where(at_start, 0.0, pp_ref[:, cc_].astype(F32) * pp_ref[:, ch].astype(F32))
            w = cw_ref[:, cb]
            u1 = _rows_before(uh, u, 1)
            u2 = _rows_before(uh, u, 2)
            cv = u2 * w[0:1] + u1 * w[1:2] + u * w[2:3]
            dz_t = dz_ref[:, cb].astype(F32)
            dcv = dz_t * b
            dcvn = jnp.where(at_end, 0.0, dzn_ref[:, cb].astype(F32) * pn_ref[:, cb].astype(F32))
            du = dcv * w[2:3] + _rows_after(dcv, dcvn, 1) * w[1:2] + _rows_after(dcv, dcvn, 2) * w[0:1]
            dp_ref[:, cb] = (dz_t * cv).astype(dp_ref.dtype)
            dp_ref[:, cc_] = (du * hh).astype(dp_ref.dtype)
            dp_ref[:, ch] = (du * c).astype(dp_ref.dtype)
            dcw_ref[0:1, cb] += jnp.sum(dcv * u2, axis=0, keepdims=True)
            dcw_ref[1:2, cb] += jnp.sum(dcv * u1, axis=0, keepdims=True)
            dcw_ref[2:3, cb] += jnp.sum(dcv * u, axis=0, keepdims=True)

    dp, dcw = pl.pallas_call(
        body, name=name,
        out_shape=(jax.ShapeDtypeStruct((T, D3), BF16), jax.ShapeDtypeStruct((SUBLANES_F32, D), F32)),
        grid=(T // tm,),
        in_specs=[pl.BlockSpec((tm, D3), lambda i: (i, 0)), p_prev, p_next,
                  pl.BlockSpec((tm, D), lambda i: (i, 0)), dz_next, pl.BlockSpec((3, D), lambda i: (0, 0))],
        out_specs=(pl.BlockSpec((tm, D3), lambda i: (i, 0)), pl.BlockSpec((SUBLANES_F32, D), lambda i: (0, 0))),
        compiler_params=_params("arbitrary"),
    )(p, p, p, dz, dz, cw)
    return dp, dcw[0:3]


def _ffn_gate_fwd(u, cw, cb, seq, name):
    T, F2 = u.shape
    F = F2 // 2
    tm = _tile(seq, 256, SUBLANES_BF16)
    cc = _tile(F, 256, LANES)
    prev, _ = _halo_specs(tm, F2, T)

    def body(u_ref, uh_ref, cw_ref, cb_ref, a_ref):
        at_start = (pl.program_id(0) * tm) % seq == 0

        def conv(c0):
            cols = slice(c0, c0 + cc)
            cur = u_ref[:, cols].astype(F32)
            halo = jnp.where(at_start, 0.0, uh_ref[:, cols].astype(F32))
            w = cw_ref[:, cols]
            return (_rows_before(halo, cur, 2) * w[0:1] + _rows_before(halo, cur, 1) * w[1:2] + cur * w[2:3]
                    + cb_ref[:, cols])

        for c0 in range(0, F, cc):
            g = conv(c0)
            up = conv(F + c0)
            a_ref[:, c0:c0 + cc] = ((g * jax.nn.sigmoid(g)) * up).astype(a_ref.dtype)

    return pl.pallas_call(
        body, name=name, out_shape=jax.ShapeDtypeStruct((T, F), BF16), grid=(T // tm,),
        in_specs=[pl.BlockSpec((tm, F2), lambda i: (i, 0)), prev,
                  pl.BlockSpec((3, F2), lambda i: (0, 0)), pl.BlockSpec((1, F2), lambda i: (0, 0))],
        out_specs=pl.BlockSpec((tm, F), lambda i: (i, 0)),
        compiler_params=_params("parallel"),
    )(u, u, cw, cb.reshape(1, F2))


def _ffn_gate_bwd(u, cw, cb, da, seq, name):
    T, F2 = u.shape
    F = F2 // 2
    H = SUBLANES_BF16
    tm = _tile(seq, 256, H)
    cc = _tile(F, 256, LANES)
    u_prev, u_next = _halo_specs(tm, F2, T)
    _, da_next = _halo_specs(tm, F, T)

    def body(u_ref, up_ref, un_ref, da_ref, dan_ref, cw_ref, cb_ref, du_ref, acc_ref):
        i = pl.program_id(0)

        @pl.when(i == 0)
        def _():
            acc_ref[...] = jnp.zeros_like(acc_ref)

        at_start = (i * tm) % seq == 0
        at_end = ((i + 1) * tm) % seq == 0

        def conv_ext(cols):
            ext = jnp.concatenate([jnp.where(at_start, 0.0, up_ref[:, cols].astype(F32)),
                                   u_ref[:, cols].astype(F32), un_ref[:, cols].astype(F32)], axis=0)
            w = cw_ref[:, cols]
            e1 = pltpu.roll(ext, 1, 0)
            e2 = pltpu.roll(ext, 2, 0)
            out = (e2 * w[0:1] + e1 * w[1:2] + ext * w[2:3] + cb_ref[:, cols])[H:]
            return out, ext, e1, e2

        def back(d2, cols, ext, e1, e2):
            w = cw_ref[:, cols]
            n = tm + H
            d1n = pltpu.roll(d2, n - 1, 0)[:tm]
            d2n = pltpu.roll(d2, n - 2, 0)[:tm]
            d_t = d2[:tm]
            du_ref[:, cols] = (d_t * w[2:3] + d1n * w[1:2] + d2n * w[0:1]).astype(du_ref.dtype)
            acc_ref[0:1, cols] += jnp.sum(d_t * e2[H:H + tm], axis=0, keepdims=True)
            acc_ref[1:2, cols] += jnp.sum(d_t * e1[H:H + tm], axis=0, keepdims=True)
            acc_ref[2:3, cols] += jnp.sum(d_t * ext[H:H + tm], axis=0, keepdims=True)
            acc_ref[3:4, cols] += jnp.sum(d_t, axis=0, keepdims=True)

        for c0 in range(0, F, cc):
            gc, uc = slice(c0, c0 + cc), slice(F + c0, F + c0 + cc)
            g, g_ext, g_e1, g_e2 = conv_ext(gc)
            up, up_ext, up_e1, up_e2 = conv_ext(uc)
            da_ext = jnp.concatenate([da_ref[:, gc].astype(F32),
                                      jnp.where(at_end, 0.0, dan_ref[:, gc].astype(F32))], axis=0)
            sg = jax.nn.sigmoid(g)
            d_up = da_ext * (g * sg)
            d_g = da_ext * up * (sg * (1.0 + g * (1.0 - sg)))
            back(d_g, gc, g_ext, g_e1, g_e2)
            back(d_up, uc, up_ext, up_e1, up_e2)

    du, acc = pl.pallas_call(
        body, name=name,
        out_shape=(jax.ShapeDtypeStruct((T, F2), BF16), jax.ShapeDtypeStruct((SUBLANES_F32, F2), F32)),
        grid=(T // tm,),
        in_specs=[pl.BlockSpec((tm, F2), lambda i: (i, 0)), u_prev, u_next,
                  pl.BlockSpec((tm, F), lambda i: (i, 0)), da_next,
                  pl.BlockSpec((3, F2), lambda i: (0, 0)), pl.BlockSpec((1, F2), lambda i: (0, 0))],
        out_specs=(pl.BlockSpec((tm, F2), lambda i: (i, 0)), pl.BlockSpec((SUBLANES_F32, F2), lambda i: (0, 0))),
        compiler_params=_params("arbitrary"),
    )(u, u, u, da, da, cw, cb.reshape(1, F2))
    return du, acc[0:3], acc[3]


def _bucket_map():
    P = ATT_BLOCK
    qi = np.arange(P, dtype=np.int64)[:, None]
    kc = np.arange(2 * P, dtype=np.int64)[None, :]
    delta = qi + P - kc
    maps = []
    max_exact = REL_BUCKETS // 2
    for window, dilation in DILATED_BRANCHES:
        band = (delta >= 0) & (delta <= window // dilation)
        n = np.maximum(delta * dilation, 0)
        nf = np.maximum(n, max_exact).astype(np.float32)
        large = max_exact + (np.log(nf / np.float32(max_exact)) / np.float32(math.log(REL_MAX_DISTANCE / max_exact))
                             * np.float32(REL_BUCKETS - max_exact)).astype(np.int32)
        large = np.minimum(large, REL_BUCKETS - 1)
        bucket = np.where(n < max_exact, n, large)
        maps.append(np.where(band, bucket, -1).astype(np.int32))
    return np.stack(maps)


def _bias_tables(rel_bias, bmap, name):
    n_heads = rel_bias.shape[1]
    nbr, P, P2 = bmap.shape

    def body(rb_ref, bm_ref, o_ref):
        h = pl.program_id(0)
        for br in range(nbr):
            bm = bm_ref[br]
            acc = jnp.full((P, P2), MASKED_LOGIT, F32)
            for b in range(REL_BUCKETS):
                acc = jnp.where(bm == b, rb_ref[b, h], acc)
            o_ref[br, 0] = acc

    return pl.pallas_call(
        body, name=name, out_shape=jax.ShapeDtypeStruct((nbr, n_heads, P, P2), F32), grid=(n_heads,),
        in_specs=[pl.BlockSpec(memory_space=pltpu.SMEM), pl.BlockSpec((nbr, P, P2), lambda h: (0, 0, 0))],
        out_specs=pl.BlockSpec((nbr, 1, P, P2), lambda h: (0, h, 0, 0)),
        compiler_params=_params("parallel"),
    )(rel_bias, bmap)


def _bias_grad(dbias, bmap, name):
    nbr, n_heads, P, P2 = dbias.shape

    def body(db_ref, bm_ref, o_ref):
        lane = lax.broadcasted_iota(jnp.int32, (1, LANES), 1)
        row = jnp.zeros((1, LANES), F32)
        for br in range(nbr):
            bm = bm_ref[br]
            d = db_ref[br, 0]
            for b in range(REL_BUCKETS):
                hit = jnp.sum(jnp.where(bm == b, d, 0.0), axis=1, keepdims=True)
                row = row + jnp.where(lane == b, jnp.sum(hit, axis=0, keepdims=True), 0.0)
        o_ref[0] = row

    return pl.pallas_call(
        body, name=name, out_shape=jax.ShapeDtypeStruct((n_heads, 1, LANES), F32), grid=(n_heads,),
        in_specs=[pl.BlockSpec((nbr, 1, P, P2), lambda h: (0, h, 0, 0)), pl.BlockSpec((nbr, P, P2), lambda h: (0, 0, 0))],
        out_specs=pl.BlockSpec((1, 1, LANES), lambda h: (h, 0, 0)),
        compiler_params=_params("parallel"),
    )(dbias, bmap)[:, 0, :]


def _rows(start, n, dilation):
    return pl.ds(start, n) if dilation == 1 else pl.ds(start, n, stride=dilation)


def _for_each_block(seq, fn):
    P = ATT_BLOCK
    for br, (_, d) in enumerate(DILATED_BRANCHES):
        nb = seq // d // P

        def one_class(r, carry, br=br, d=d, nb=nb):
            fn(br, d, r, True)

            def later(j, c2):
                start = r + j * (d * P)
                if d == 1:
                    start = pl.multiple_of(start, P)
                fn(br, d, start, False)
                return c2

            lax.fori_loop(1, nb, later, 0)
            return carry

        if d == 1:
            one_class(0, 0)
        else:
            lax.fori_loop(0, d, one_class, 0)


def _attn_fwd(q, kv, bias, name):
    B, S, D = q.shape
    P = ATT_BLOCK
    n_pairs = D // LANES
    scale = HEAD_DIM ** -0.5

    def body(q_ref, k_ref, v_ref, bias_ref, o_ref, lse_ref, m_s, l_s, acc_s):
        lane = lax.broadcasted_iota(jnp.int32, (P, LANES), 1)
        head0 = lane < HEAD_DIM

        def block(br, d, start, first):
            rows = _rows(start, P, d)
            nk = P if first else 2 * P
            krows = _rows(start if first else start - d * P, nk, d)
            qb = q_ref[rows, :] * scale
            kb = k_ref[krows, :].astype(BF16)
            vb = v_ref[krows, :].astype(BF16)
            stats = []
            for hh in range(2):
                mine = head0 if hh == 0 else jnp.logical_not(head0)
                qh = jnp.where(mine, qb, 0.0).astype(BF16)
                s = lax.dot_general(qh, kb, (((1,), (1,)), ((), ())), preferred_element_type=F32)
                bb = bias_ref[br, hh]
                s = s + (bb[:, P:] if first else bb)
                m = jnp.max(s, axis=-1, keepdims=True)
                p = jnp.exp(s - m)
                l = jnp.sum(p, axis=-1, keepdims=True)
                pv = jnp.dot(p.astype(BF16), vb, preferred_element_type=F32)
                stats.append((m, l, pv))
            m_b = jnp.where(head0, stats[0][0], stats[1][0])
            l_b = jnp.where(head0, stats[0][1], stats[1][1])
            pv_b = jnp.where(head0, stats[0][2], stats[1][2])
            if br == 0:
                m_s[rows, :] = m_b
                l_s[rows, :] = l_b
                acc_s[rows, :] = pv_b
            else:
                m_old = m_s[rows, :]
                m_new = jnp.maximum(m_old, m_b)
                a_old = jnp.exp(m_old - m_new)
                a_new = jnp.exp(m_b - m_new)
                m_s[rows, :] = m_new
                l_s[rows, :] = a_old * l_s[rows, :] + a_new * l_b
                acc_s[rows, :] = a_old * acc_s[rows, :] + a_new * pv_b

        _for_each_block(S, block)

        chunk = _tile(S, 512, SUBLANES_F32)

        def finish(i, carry):
            rows = pl.ds(pl.multiple_of(i * chunk, chunk), chunk)
            l = l_s[rows, :]
            o_ref[rows, :] = acc_s[rows, :] / l
            lse_ref[rows, :] = m_s[rows, :] + jnp.log(l)
            return carry

        lax.fori_loop(0, S // chunk, finish, 0)

    slab = lambda col0: pl.BlockSpec((None, S, LANES), lambda b, h: (b, 0, col0 + h))
    nbr = len(DILATED_BRANCHES)
    return pl.pallas_call(
        body, name=name,
        out_shape=(jax.ShapeDtypeStruct((B, S, D), F32), jax.ShapeDtypeStruct((B, S, D), F32)),
        grid=(B, n_pairs),
        in_specs=[slab(0), slab(0), slab(n_pairs),
                  pl.BlockSpec((nbr, 2, P, 2 * P), lambda b, h: (0, h, 0, 0))],
        out_specs=(slab(0), slab(0)),
        scratch_shapes=[pltpu.VMEM((S, LANES), F32)] * 3,
        compiler_params=_params("parallel", "parallel"),
    )(q, kv, kv, bias)


def _attn_bwd(q, kv, o, lse, do, bias, name):
    B, S, D = q.shape
    P = ATT_BLOCK
    n_pairs = D // LANES
    n_heads = D // HEAD_DIM
    nbr = len(DILATED_BRANCHES)
    scale = HEAD_DIM ** -0.5

    def body(q_ref, k_ref, v_ref, o_ref, lse_ref, do_ref, bias_ref, dq_ref, dk_ref, dv_ref, dbias_ref, delta_s):
        lane = lax.broadcasted_iota(jnp.int32, (P, LANES), 1)
        head0 = lane < HEAD_DIM

        @pl.when(pl.program_id(1) == 0)
        def _():
            dbias_ref[...] = jnp.zeros_like(dbias_ref)

        chunk = _tile(S, 512, SUBLANES_F32)

        def prepare(i, carry):
            rows = pl.ds(pl.multiple_of(i * chunk, chunk), chunk)
            x = do_ref[rows, :] * o_ref[rows, :]
            h0 = lax.broadcasted_iota(jnp.int32, (chunk, LANES), 1) < HEAD_DIM
            d0 = jnp.sum(jnp.where(h0, x, 0.0), axis=-1, keepdims=True)
            d1 = jnp.sum(jnp.where(h0, 0.0, x), axis=-1, keepdims=True)
            delta_s[rows, :] = jnp.where(h0, d0, d1)
            zero = jnp.zeros((chunk, LANES), F32)
            dq_ref[rows, :] = zero
            dk_ref[rows, :] = zero
            dv_ref[rows, :] = zero
            return carry

        lax.fori_loop(0, S // chunk, prepare, 0)

        def block(br, d, start, first):
            rows = _rows(start, P, d)
            nk = P if first else 2 * P
            krows = _rows(start if first else start - d * P, nk, d)
            qb = q_ref[rows, :] * scale
            kb = k_ref[krows, :].astype(BF16)
            vb = v_ref[krows, :].astype(BF16)
            dob = do_ref[rows, :]
            lse_b = lse_ref[rows, :]
            delta_b = delta_s[rows, :]
            dk_b = jnp.zeros((nk, LANES), F32)
            dv_b = jnp.zeros((nk, LANES), F32)
            dq_h = []
            for hh in range(2):
                mine = head0 if hh == 0 else jnp.logical_not(head0)
                col = hh * HEAD_DIM
                qh = jnp.where(mine, qb, 0.0).astype(BF16)
                doh = jnp.where(mine, dob, 0.0).astype(BF16)
                s = lax.dot_general(qh, kb, (((1,), (1,)), ((), ())), preferred_element_type=F32)
                bb = bias_ref[br, hh]
                s = s + (bb[:, P:] if first else bb)
                p = jnp.exp(s - lse_b[:, col:col + 1])
                dp = lax.dot_general(doh, vb, (((1,), (1,)), ((), ())), preferred_element_type=F32)
                ds = p * (dp - delta_b[:, col:col + 1])
                if first:
                    dbias_ref[br, hh, :, P:] += ds
                else:
                    dbias_ref[br, hh] += ds
                ds16 = ds.astype(BF16)
                dq_h.append(jnp.dot(ds16, kb, preferred_element_type=F32))
                dk_b = dk_b + lax.dot_general(ds16, qh, (((0,), (0,)), ((), ())), preferred_element_type=F32)
                dv_b = dv_b + lax.dot_general(p.astype(BF16), doh, (((0,), (0,)), ((), ())),
                                              preferred_element_type=F32)
            dq_ref[rows, :] += jnp.where(head0, dq_h[0], dq_h[1]) * scale
            dk_ref[krows, :] += dk_b
            dv_ref[krows, :] += dv_b

        _for_each_block(S, block)

    slab = lambda col0: pl.BlockSpec((None, S, LANES), lambda h, b: (b, 0, col0 + h))
    tab = pl.BlockSpec((nbr, 2, P, 2 * P), lambda h, b: (0, h, 0, 0))
    shp = jax.ShapeDtypeStruct((B, S, D), F32)
    return pl.pallas_call(
        body, name=name,
        out_shape=(shp, shp, shp, jax.ShapeDtypeStruct((nbr, n_heads, P, 2 * P), F32)),
        grid=(n_pairs, B),
        in_specs=[slab(0), slab(0), slab(n_pairs), slab(0), slab(0), slab(0), tab],
        out_specs=(slab(0), slab(0), slab(0), tab),
        scratch_shapes=[pltpu.VMEM((S, LANES), F32)],
        compiler_params=_params("parallel", "arbitrary"),
    )(q, kv, kv, o, lse, do, bias)


def _adamw(w, g, m, v, name):
    R, C = w.shape
    tr = _tile(R, 256, SUBLANES_F32) if R % SUBLANES_F32 == 0 else R
    tc = _tile(C, 2048, LANES) if C % LANES == 0 else C

    def body(w_ref, g_ref, m_ref, v_ref, d_ref, nm_ref, nv_ref):
        g_ = g_ref[...]
        m2 = ADAM_B1 * m_ref[...] + (1.0 - ADAM_B1) * g_
        v2 = ADAM_B2 * v_ref[...] + (1.0 - ADAM_B2) * (g_ * g_)
        m_hat = m2 / (1.0 - ADAM_B1 ** ADAM_STEP)
        v_hat = v2 / (1.0 - ADAM_B2 ** ADAM_STEP)
        d_ref[...] = -ADAM_LR * (m_hat / (jnp.sqrt(v_hat) + ADAM_EPS) + ADAM_WD * w_ref[...])
        nm_ref[...] = m2
        nv_ref[...] = v2

    blk = pl.BlockSpec((tr, tc), lambda i, j: (i, j))
    shp = jax.ShapeDtypeStruct((R, C), F32)
    return pl.pallas_call(
        body, name=name, out_shape=(shp, shp, shp), grid=(R // tr, C // tc),
        in_specs=[blk] * 4, out_specs=(blk,) * 3, compiler_params=_params("parallel", "parallel"),
    )(w, g, m, v)


def _sum_slots(slots, name):
    n, R, C = slots.shape
    tr = _tile(R, 256, SUBLANES_BF16) if R % SUBLANES_BF16 == 0 else R
    tc = _tile(C, 2048, LANES) if C % LANES == 0 else C

    def body(s_ref, o_ref):
        acc = s_ref[0].astype(F32)
        for k in range(1, n):
            acc = acc + s_ref[k].astype(F32)
        o_ref[...] = acc

    return pl.pallas_call(
        body, name=name, out_shape=jax.ShapeDtypeStruct((R, C), F32), grid=(R // tr, C // tc),
        in_specs=[pl.BlockSpec((n, tr, tc), lambda i, j: (0, i, j))],
        out_specs=pl.BlockSpec((tr, tc), lambda i, j: (i, j)),
        compiler_params=_params("parallel", "parallel"),
    )(slots)


def _my_place():
    return lax.axis_index("x"), lax.axis_index("y"), lax.axis_index("c")


def _other_chips(x, y):
    return [(1 - x, y), (x, 1 - y), (1 - x, 1 - y)]


def _piece(ref, blk, axis, shard, half):
    h0 = blk[0] // 2
    idx = []
    for dim, n in enumerate(blk):
        if dim == 0:
            start = half * h0 + (shard * n if axis == 0 else 0)
            idx.append(pl.ds(start, h0))
        elif dim == axis:
            idx.append(pl.ds(shard * n, n))
        else:
            idx.append(slice(None))
    return ref.at[tuple(idx)]


def _shard_of(ref, blk, axis, shard):
    idx = [pl.ds(shard * n, n) if dim == axis else slice(None) for dim, n in enumerate(blk)]
    return ref.at[tuple(idx)]


def _half_of(ref, blk, half):
    return ref.at[pl.ds(half * (blk[0] // 2), blk[0] // 2)]


def _gather_weights(shards, axes, small):
    n = len(shards)
    blks = [s.shape for s in shards]
    fulls = [tuple(d * N_SHARDS if i == ax else d for i, d in enumerate(b)) for b, ax in zip(blks, axes)]

    def stage1(*refs):
        ins, small_in = refs[:n], refs[n]
        outs, small_out = refs[n + 1:2 * n + 1], refs[2 * n + 1]
        send_sems, recv_sems, local_sems = refs[2 * n + 2:]
        x, y, c = _my_place()
        me = 2 * x + y
        chips = _other_chips(x, y)
        local = [pltpu.make_async_copy(ins[a], _shard_of(outs[a], blks[a], axes[a], me), local_sems.at[a])
                 for a in range(n)]
        local.append(pltpu.make_async_copy(small_in, small_out.at[me], local_sems.at[n]))
        for cp in local:
            cp.start()
        sends, recvs = [], []
        for a in range(n + 1):
            for k, (px, py) in enumerate(chips):
                peer = 2 * px + py
                if a < n:
                    src = _half_of(ins[a], blks[a], c)
                    dst = _piece(outs[a], blks[a], axes[a], me, c)
                    land = _piece(outs[a], blks[a], axes[a], peer, c)
                else:
                    src, dst, land = small_in, small_out.at[me], small_out.at[peer]
                sends.append(pltpu.make_async_remote_copy(
                    src_ref=src, dst_ref=dst, send_sem=send_sems.at[a, k], recv_sem=recv_sems.at[a, k],
                    device_id=(px, py, c), device_id_type=pl.DeviceIdType.MESH))
                recvs.append(pltpu.make_async_remote_copy(
                    src_ref=src, dst_ref=land, send_sem=send_sems.at[a, k], recv_sem=recv_sems.at[a, k],
                    device_id=(px, py, c), device_id_type=pl.DeviceIdType.MESH))
        for cp in sends:
            cp.start()
        for cp in recvs:
            cp.wait_recv()
        for cp in sends:
            cp.wait_send()
        for cp in local:
            cp.wait()

    out_shape = [jax.ShapeDtypeStruct(f, s.dtype) for f, s in zip(fulls, shards)]
    out_shape.append(jax.ShapeDtypeStruct((N_SHARDS,) + small.shape, small.dtype))
    res = pl.pallas_call(
        stage1, name="gather_weights_ici", out_shape=out_shape,
        in_specs=[ANY] * (n + 1), out_specs=[ANY] * (n + 1),
        scratch_shapes=[pltpu.SemaphoreType.DMA((n + 1, 3)), pltpu.SemaphoreType.DMA((n + 1, 3)),
                        pltpu.SemaphoreType.DMA((n + 1,))],
    )(*shards, small)
    partial, small_full = res[:n], res[n]

    def stage2(*refs):
        outs = refs[n:2 * n]
        send_sems, recv_sems = refs[2 * n:]
        x, y, c = _my_place()
        sends, recvs = [], []
        for a in range(n):
            for k, (px, py) in enumerate(_other_chips(x, y)):
                peer = 2 * px + py
                mine = _piece(outs[a], blks[a], axes[a], peer, c)
                theirs = _piece(outs[a], blks[a], axes[a], peer, 1 - c)
                sends.append(pltpu.make_async_remote_copy(
                    src_ref=mine, dst_ref=mine, send_sem=send_sems.at[a, k], recv_sem=recv_sems.at[a, k],
                    device_id=(x, y, 1 - c), device_id_type=pl.DeviceIdType.MESH))
                recvs.append(pltpu.make_async_remote_copy(
                    src_ref=mine, dst_ref=theirs, send_sem=send_sems.at[a, k], recv_sem=recv_sems.at[a, k],
                    device_id=(x, y, 1 - c), device_id_type=pl.DeviceIdType.MESH))
        for cp in sends:
            cp.start()
        for cp in recvs:
            cp.wait_recv()
        for cp in sends:
            cp.wait_send()

    full = pl.pallas_call(
        stage2, name="gather_weights_d2d", out_shape=out_shape[:n],
        in_specs=[ANY] * n, out_specs=[ANY] * n, input_output_aliases={a: a for a in range(n)},
        scratch_shapes=[pltpu.SemaphoreType.DMA((n, 3)), pltpu.SemaphoreType.DMA((n, 3))],
    )(*partial)
    return list(full), small_full


def _scatter_grads(grads, blks, axes, small):
    n = len(grads)
    halves = [(b[0] // 2,) + tuple(b[1:]) for b in blks]

    def body(*refs):
        ins, small_in = refs[:n], refs[n]
        outs, small_out = refs[n + 1:2 * n + 1], refs[2 * n + 1]
        send_sems, recv_sems, local_sems = refs[2 * n + 2:]
        x, y, c = _my_place()
        me = 4 * x + 2 * y + c
        local = [pltpu.make_async_copy(_piece(ins[a], blks[a], axes[a], 2 * x + y, c), outs[a].at[me],
                                       local_sems.at[a]) for a in range(n)]
        local.append(pltpu.make_async_copy(small_in, small_out.at[me], local_sems.at[n]))
        for cp in local:
            cp.start()
        sends, recvs = [], []
        for rel in range(1, N_DEVICES):
            px = x ^ ((rel >> 2) & 1)
            py = y ^ ((rel >> 1) & 1)
            pc = c ^ (rel & 1)
            peer = 4 * px + 2 * py + pc
            for a in range(n + 1):
                if a < n:
                    src = _piece(ins[a], blks[a], axes[a], 2 * px + py, pc)
                    dst, land = outs[a].at[me], outs[a].at[peer]
                else:
                    src, dst, land = small_in, small_out.at[me], small_out.at[peer]
                sends.append(pltpu.make_async_remote_copy(
                    src_ref=src, dst_ref=dst, send_sem=send_sems.at[a, rel - 1], recv_sem=recv_sems.at[a, rel - 1],
                    device_id=(px, py, pc), device_id_type=pl.DeviceIdType.MESH))
                recvs.append(pltpu.make_async_remote_copy(
                    src_ref=src, dst_ref=land, send_sem=send_sems.at[a, rel - 1], recv_sem=recv_sems.at[a, rel - 1],
                    device_id=(px, py, pc), device_id_type=pl.DeviceIdType.MESH))
        for cp in sends:
            cp.start()
        for cp in recvs:
            cp.wait_recv()
        for cp in sends:
            cp.wait_send()
        for cp in local:
            cp.wait()

    out_shape = [jax.ShapeDtypeStruct((N_DEVICES,) + h, g.dtype) for h, g in zip(halves, grads)]
    out_shape.append(jax.ShapeDtypeStruct((N_DEVICES,) + small.shape, small.dtype))
    res = pl.pallas_call(
        body, name="scatter_grads", out_shape=out_shape,
        in_specs=[ANY] * (n + 1), out_specs=[ANY] * (n + 1),
        scratch_shapes=[pltpu.SemaphoreType.DMA((n + 1, N_DEVICES - 1)), pltpu.SemaphoreType.DMA((n + 1, N_DEVICES - 1)),
                        pltpu.SemaphoreType.DMA((n + 1,))],
    )(*grads, small)
    return list(res[:n]), res[n]


def _swap_halves(halves, blks):
    n = len(halves)

    def body(*refs):
        ins, outs = refs[:n], refs[n:2 * n]
        send_sems, recv_sems, local_sems = refs[2 * n:]
        x, y, c = _my_place()
        local = [pltpu.make_async_copy(ins[a], _half_of(outs[a], blks[a], c), local_sems.at[a]) for a in range(n)]
        for cp in local:
            cp.start()
        sends, recvs = [], []
        for a in range(n):
            sends.append(pltpu.make_async_remote_copy(
                src_ref=ins[a], dst_ref=_half_of(outs[a], blks[a], c), send_sem=send_sems.at[a],
                recv_sem=recv_sems.at[a], device_id=(x, y, 1 - c), device_id_type=pl.DeviceIdType.MESH))
            recvs.append(pltpu.make_async_remote_copy(
                src_ref=ins[a], dst_ref=_half_of(outs[a], blks[a], 1 - c), send_sem=send_sems.at[a],
                recv_sem=recv_sems.at[a], device_id=(x, y, 1 - c), device_id_type=pl.DeviceIdType.MESH))
        for cp in sends:
            cp.start()
        for cp in recvs:
            cp.wait_recv()
        for cp in sends:
            cp.wait_send()
        for cp in local:
            cp.wait()

    return list(pl.pallas_call(
        body, name="swap_grad_halves", out_shape=[jax.ShapeDtypeStruct(b, F32) for b in blks],
        in_specs=[ANY] * n, out_specs=[ANY] * n,
        scratch_shapes=[pltpu.SemaphoreType.DMA((n,)), pltpu.SemaphoreType.DMA((n,)), pltpu.SemaphoreType.DMA((n,))],
    )(*halves))


def _pack(arrays):
    flat = jnp.concatenate([a.reshape(-1).astype(F32) for a in arrays])
    pad = (-flat.shape[0]) % (SUBLANES_F32 * LANES)
    return jnp.pad(flat, (0, pad)).reshape(-1, LANES)


def _unpack(packed, shapes):
    flat = packed.reshape(-1)
    out, off = [], 0
    for s in shapes:
        n = int(np.prod(s))
        out.append(flat[off:off + n].reshape(s))
        off += n
    return out


def _ffn_fwd(h, norm, w_up, cw, cb, w_down, seq, tag):
    xn = _rmsnorm_fwd(h, norm, f"ffn{tag}_norm")
    u = _mm_nn(xn, w_up, None, BF16, f"ffn{tag}_up")
    a = _ffn_gate_fwd(u, cw, cb, seq, f"ffn{tag}_gate")
    out = _mm_nn(a, w_down, h, F32, f"ffn{tag}_down")
    return out, (xn, u, a)


def _ffn_bwd(dout, h, norm, w_up, cw, cb, w_down, saved, seq, tag):
    xn, u, a = saved
    da = _mm_nt(dout, w_down, BF16, f"ffn{tag}_down_dx")
    g_down = _mm_tn(a, dout, f"ffn{tag}_down_dw")
    du, g_cw, g_cb = _ffn_gate_bwd(u, cw, cb, da, seq, f"ffn{tag}_gate_bwd")
    dxn = _mm_nt(du, w_up, BF16, f"ffn{tag}_up_dx")
    g_up = _mm_tn(xn, du, f"ffn{tag}_up_dw")
    dh, g_norm = _rmsnorm_bwd(h, norm, dxn, dout, f"ffn{tag}_norm_bwd")
    return dh, g_up, g_down, g_cw, g_cb, g_norm


def _local_step(x, target, W, small):
    B, S, D = x.shape
    T = B * S
    x2 = x.reshape(T, D)
    tgt = target.reshape(T, D)
    bmap = jnp.asarray(_bucket_map())

    xn0 = _rmsnorm_fwd(x2, small["a_norm"][0], "a_norm")
    p = _mm_nn(xn0, W["w_in"], None, BF16, "a_in")
    z = _gate_a_fwd(p, small["a_conv"][0], S, "a_gate")
    h1 = _mm_nn(z, W["w_out"], x2, F32, "a_out")
    h2, ffn0 = _ffn_fwd(h1, small["ffn_norm"][0], W["w_up"][0], small["ffn_conv"][0], small["ffn_conv_b"][0],
                        W["w_down"][0], S, 0)
    kvn = _rmsnorm_fwd(h2, small["kv_norm"], "kv_norm")
    kv = _mm_nn(kvn, W["w_kv"], None, F32, "kv_proj")
    xn3 = _rmsnorm_fwd(h2, small["b_norm"][0], "b_norm")
    q = _mm_nn(xn3, W["w_q"], None, F32, "q_proj")
    bias = _bias_tables(small["rel_bias"], bmap, "rel_bias_tables")
    q3, kv3 = q.reshape(B, S, D), kv.reshape(B, S, 2 * D)
    o3, lse3 = _attn_fwd(q3, kv3, bias, "attn_fwd")
    o = o3.reshape(T, D)
    h3 = _mm_nn(o, W["w_o"], h2, F32, "o_proj")
    h4, ffn1 = _ffn_fwd(h3, small["ffn_norm"][1], W["w_up"][1], small["ffn_conv"][1], small["ffn_conv_b"][1],
                        W["w_down"][1], S, 1)
    sq_err, dh4, g_final = _loss_head(h4, small["final_norm"], tgt, "loss_head")
    loss = 0.5 * jnp.sum(sq_err) / D

    dh3, g_up1, g_down1, g_cw1, g_cb1, g_fn1 = _ffn_bwd(
        dh4, h3, small["ffn_norm"][1], W["w_up"][1], small["ffn_conv"][1], small["ffn_conv_b"][1], W["w_down"][1],
        ffn1, S, 1)
    do = _mm_nt(dh3, W["w_o"], F32, "o_proj_dx")
    g_o = _mm_tn(o, dh3, "o_proj_dw")
    dq3, dk3, dv3, dbias = _attn_bwd(q3, kv3, o3, lse3, do.reshape(B, S, D), bias, "attn_bwd")
    g_rel = _bias_grad(dbias, bmap, "rel_bias_grad")[:, :REL_BUCKETS].T
    dq = dq3.reshape(T, D)
    dkv = jnp.concatenate([dk3, dv3], axis=-1).reshape(T, 2 * D)
    dxn3 = _mm_nt(dq, W["w_q"], BF16, "q_proj_dx")
    g_q = _mm_tn(xn3, dq, "q_proj_dw")
    dh2, g_bn = _rmsnorm_bwd(h2, small["b_norm"][0], dxn3, dh3, "b_norm_bwd")
    dkvn = _mm_nt(dkv, W["w_kv"], BF16, "kv_proj_dx")
    g_kv = _mm_tn(kvn, dkv, "kv_proj_dw")
    dh2, g_kvn = _rmsnorm_bwd(h2, small["kv_norm"], dkvn, dh2, "kv_norm_bwd")
    dh1, g_up0, g_down0, g_cw0, g_cb0, g_fn0 = _ffn_bwd(
        dh2, h1, small["ffn_norm"][0], W["w_up"][0], small["ffn_conv"][0], small["ffn_conv_b"][0], W["w_down"][0],
        ffn0, S, 0)
    dz = _mm_nt(dh1, W["w_out"], BF16, "a_out_dx")
    g_out = _mm_tn(z, dh1, "a_out_dw")
    dp, g_aconv = _gate_a_bwd(p, small["a_conv"][0], dz, S, "a_gate_bwd")
    dxn0 = _mm_nt(dp, W["w_in"], BF16, "a_in_dx")
    g_in = _mm_tn(xn0, dp, "a_in_dw")
    dx, g_an = _rmsnorm_bwd(x2, small["a_norm"][0], dxn0, dh1, "a_norm_bwd")

    big = {"w_in": g_in, "w_out": g_out, "w_kv": g_kv, "w_q": g_q, "w_o": g_o,
           "w_up": jnp.stack([g_up0, g_up1]), "w_down": jnp.stack([g_down0, g_down1])}
    small_g = {"a_norm": g_an[None], "a_conv": g_aconv[None], "kv_norm": g_kvn, "b_norm": g_bn[None],
               "rel_bias": g_rel, "ffn_norm": jnp.stack([g_fn0, g_fn1]), "ffn_conv": jnp.stack([g_cw0, g_cw1]),
               "ffn_conv_b": jnp.stack([g_cb0, g_cb1]), "final_norm": g_final}
    return loss, dx.reshape(B, S, D), big, small_g


BIG = ("w_in", "w_out", "w_kv", "w_q", "w_o", "w_up", "w_down")
SMALL = ("a_norm", "a_conv", "kv_norm", "b_norm", "rel_bias", "ffn_norm", "ffn_conv", "ffn_conv_b", "final_norm")
SMALL_SHARDED = ("a_norm", "a_conv", "ffn_conv")
WEIGHT_ORDER = ("a_norm", "a_w_in", "a_conv", "a_w_out", "kv_norm", "w_kv", "b_norm", "b_w_q", "b_w_o", "rel_bias",
                "ffn_norm", "ffn_w_up", "ffn_conv", "ffn_conv_b", "ffn_w_down", "final_norm")
BIG_OF = {"a_w_in": "w_in", "a_w_out": "w_out", "w_kv": "w_kv", "b_w_q": "w_q", "b_w_o": "w_o", "ffn_w_up": "w_up",
          "ffn_w_down": "w_down"}


def _as2d(a):
    return a.reshape(-1, a.shape[-1])


def kernel(x, a_norm, a_w_in, a_conv, a_w_out, kv_norm, w_kv, b_norm, b_w_q, b_w_o, rel_bias, ffn_norm, ffn_w_up, ffn_conv, ffn_conv_b, ffn_w_down, final_norm, loss_target, m_a_norm, m_a_w_in, m_a_conv, m_a_w_out, m_kv_norm, m_w_kv, m_b_norm, m_b_w_q, m_b_w_o, m_rel_bias, m_ffn_norm, m_ffn_w_up, m_ffn_conv, m_ffn_conv_b, m_ffn_w_down, m_final_norm, v_a_norm, v_a_w_in, v_a_conv, v_a_w_out, v_kv_norm, v_w_kv, v_b_norm, v_b_w_q, v_b_w_o, v_rel_bias, v_ffn_norm, v_ffn_w_up, v_ffn_conv, v_ffn_conv_b, v_ffn_w_down, v_final_norm):
    given = dict(a_norm=a_norm, a_w_in=a_w_in, a_conv=a_conv, a_w_out=a_w_out, kv_norm=kv_norm, w_kv=w_kv, b_norm=b_norm,
                 b_w_q=b_w_q, b_w_o=b_w_o, rel_bias=rel_bias, ffn_norm=ffn_norm, ffn_w_up=ffn_w_up, ffn_conv=ffn_conv,
                 ffn_conv_b=ffn_conv_b, ffn_w_down=ffn_w_down, final_norm=final_norm)
    mom_m = dict(a_norm=m_a_norm, a_w_in=m_a_w_in, a_conv=m_a_conv, a_w_out=m_a_w_out, kv_norm=m_kv_norm, w_kv=m_w_kv,
                 b_norm=m_b_norm, b_w_q=m_b_w_q, b_w_o=m_b_w_o, rel_bias=m_rel_bias, ffn_norm=m_ffn_norm,
                 ffn_w_up=m_ffn_w_up, ffn_conv=m_ffn_conv, ffn_conv_b=m_ffn_conv_b, ffn_w_down=m_ffn_w_down,
                 final_norm=m_final_norm)
    mom_v = dict(a_norm=v_a_norm, a_w_in=v_a_w_in, a_conv=v_a_conv, a_w_out=v_a_w_out, kv_norm=v_kv_norm, w_kv=v_w_kv,
                 b_norm=v_b_norm, b_w_q=v_b_w_q, b_w_o=v_b_w_o, rel_bias=v_rel_bias, ffn_norm=v_ffn_norm,
                 ffn_w_up=v_ffn_w_up, ffn_conv=v_ffn_conv, ffn_conv_b=v_ffn_conv_b, ffn_w_down=v_ffn_w_down,
                 final_norm=v_final_norm)

    shard = {"w_in": (a_w_in[0], 1), "w_out": (a_w_out[0], 0), "w_kv": (w_kv, 1), "w_q": (b_w_q[0], 0),
             "w_o": (b_w_o[0], 0), "w_up": (ffn_w_up, 2), "w_down": (ffn_w_down, 1)}
    blks = [shard[k][0].shape for k in BIG]
    axes = [shard[k][1] for k in BIG]

    small_sharded = [given[k] for k in SMALL_SHARDED]
    packed = _pack(small_sharded)
    fulls, packed_all = _gather_weights([shard[k][0].astype(BF16) for k in BIG], axes, packed)
    W = dict(zip(BIG, fulls))
    small = {k: given[k] for k in SMALL}
    per_shard = [_unpack(packed_all[j], [a.shape for a in small_sharded]) for j in range(N_SHARDS)]
    for i, k in enumerate(SMALL_SHARDED):
        small[k] = jnp.concatenate([per_shard[j][i] for j in range(N_SHARDS)], axis=-1)

    loss, grad_x, big_g, small_g = _local_step(x, loss_target, W, small)
    loss = lax.psum(loss, MESH_AXES)

    small_shapes = [small_g[k].shape for k in SMALL]
    slots, small_slots = _scatter_grads([big_g[k] for k in BIG], blks, axes, _pack([small_g[k] for k in SMALL]))
    halves = []
    for k, s, b in zip(BIG, slots, blks):
        summed = _sum_slots(s.reshape(N_DEVICES, -1, b[-1]), f"sum_{k}")
        halves.append(summed.reshape((b[0] // 2,) + tuple(b[1:])))
    reduced = dict(zip(BIG, _swap_halves(halves, blks)))
    small_sum = _sum_slots(small_slots, "sum_small")
    small_red = dict(zip(SMALL, _unpack(small_sum, small_shapes)))
    j = 2 * lax.axis_index("x") + lax.axis_index("y")
    for k in SMALL_SHARDED:
        w = given[k].shape[-1]
        small_red[k] = lax.dynamic_slice_in_dim(small_red[k], j * w, w, axis=small_red[k].ndim - 1)

    grads, deltas, new_m, new_v = {}, {}, {}, {}
    for name in WEIGHT_ORDER:
        if name in BIG_OF:
            g = reduced[BIG_OF[name]].reshape(given[name].shape)
            d, nm, nv = _adamw(_as2d(given[name]), _as2d(g), _as2d(mom_m[name]), _as2d(mom_v[name]), f"adamw_{name}")
            grads[name] = g
            deltas[name], new_m[name], new_v[name] = (t.reshape(given[name].shape) for t in (d, nm, nv))
    small_names = [n for n in WEIGHT_ORDER if n not in BIG_OF]
    for n in small_names:
        grads[n] = small_red[n].reshape(given[n].shape)
    sw, sg, sm, sv = (_pack([d[n] for n in small_names]) for d in (given, grads, mom_m, mom_v))
    d, nm, nv = _adamw(sw, sg, sm, sv, "adamw_small")
    shapes = [given[n].shape for n in small_names]
    for n, a, b_, c_ in zip(small_names, _unpack(d, shapes), _unpack(nm, shapes), _unpack(nv, shapes)):
        deltas[n], new_m[n], new_v[n] = a, b_, c_

    return (loss, grad_x, *[grads[n] for n in WEIGHT_ORDER], *[deltas[n] for n in WEIGHT_ORDER],
            *[new_m[n] for n in WEIGHT_ORDER], *[new_v[n] for n in WEIGHT_ORDER])
```

```python
import functools
import math

import numpy as np

import jax
import jax.numpy as jnp
from jax import lax
from jax.experimental import pallas as pl
from jax.experimental.pallas import tpu as pltpu

F32 = jnp.float32
BF16 = jnp.bfloat16

RMS_EPS = 1e-6
HEAD_DIM = 64
ATT_BLOCK = 128
DILATED_BRANCHES = ((128, 1), (512, 4), (2048, 16))
REL_BUCKETS = 32
REL_MAX_DISTANCE = 2048
MASKED_LOGIT = -1e30
ATTN_FWD_UNROLL = 4
ATTN_BWD_UNROLL = 4

ADAM_LR = 0.001
ADAM_B1 = 0.9
ADAM_B2 = 0.999
ADAM_EPS = 1e-08
ADAM_WD = 0.01
ADAM_STEP = 10

LANES = 128
SUBLANES_F32 = 8
SUBLANES_BF16 = 16
VMEM_LIMIT_BYTES = 56 * 1024 * 1024

MESH_AXES = ("x", "y", "c")
N_SHARDS = 4
N_DEVICES = 8
ANY = pl.BlockSpec(memory_space=pl.ANY)


def _tile(n, pref, mult):
    best = None
    for t in range(mult, min(n, pref) + 1, mult):
        if n % t == 0:
            best = t
    if best is None:
        raise ValueError(f"no tile for {n} (multiple of {mult}, at most {pref})")
    return best


def _params(*sem):
    return pltpu.CompilerParams(dimension_semantics=sem, vmem_limit_bytes=VMEM_LIMIT_BYTES)


def _rmsnorm_fwd(x, g, name):
    T, D = x.shape
    tm = _tile(T, 512, SUBLANES_BF16)

    def body(x_ref, g_ref, o_ref):
        xf = x_ref[...]
        r = lax.rsqrt(jnp.mean(xf * xf, axis=-1, keepdims=True) + RMS_EPS)
        o_ref[...] = ((xf * r) * g_ref[...]).astype(o_ref.dtype)

    return pl.pallas_call(
        body, name=name, out_shape=jax.ShapeDtypeStruct((T, D), BF16), grid=(T // tm,),
        in_specs=[pl.BlockSpec((tm, D), lambda i: (i, 0)), pl.BlockSpec((1, D), lambda i: (0, 0))],
        out_specs=pl.BlockSpec((tm, D), lambda i: (i, 0)),
        compiler_params=_params("parallel"),
    )(x, g.reshape(1, D))


def _rmsnorm_bwd(x, g, dxn, dres, name):
    T, D = x.shape
    tm = _tile(T, 512, SUBLANES_BF16)

    def body(x_ref, g_ref, dxn_ref, dres_ref, dx_ref, dg_ref):
        @pl.when(pl.program_id(0) == 0)
        def _():
            dg_ref[...] = jnp.zeros_like(dg_ref)

        xf = x_ref[...]
        r = lax.rsqrt(jnp.mean(xf * xf, axis=-1, keepdims=True) + RMS_EPS)
        xhat = xf * r
        dy = dxn_ref[...].astype(F32)
        dg_ref[0:1, :] += jnp.sum(dy * xhat, axis=0, keepdims=True)
        t = dy * g_ref[...]
        dx_ref[...] = dres_ref[...] + r * (t - xhat * jnp.mean(t * xhat, axis=-1, keepdims=True))

    row = pl.BlockSpec((tm, D), lambda i: (i, 0))
    dx, dg = pl.pallas_call(
        body, name=name,
        out_shape=(jax.ShapeDtypeStruct((T, D), F32), jax.ShapeDtypeStruct((SUBLANES_F32, D), F32)),
        grid=(T // tm,),
        in_specs=[row, pl.BlockSpec((1, D), lambda i: (0, 0)), row, row],
        out_specs=(row, pl.BlockSpec((SUBLANES_F32, D), lambda i: (0, 0))),
        compiler_params=_params("arbitrary"),
    )(x, g.reshape(1, D), dxn, dres)
    return dx, dg[0]


def _loss_head(h, g, target, name):
    T, D = h.shape
    tm = _tile(T, 512, SUBLANES_F32)

    def body(h_ref, g_ref, t_ref, dh_ref, acc_ref):
        @pl.when(pl.program_id(0) == 0)
        def _():
            acc_ref[...] = jnp.zeros_like(acc_ref)

        xf = h_ref[...]
        r = lax.rsqrt(jnp.mean(xf * xf, axis=-1, keepdims=True) + RMS_EPS)
        xhat = xf * r
        err = xhat * g_ref[...] - t_ref[...]
        dy = err * (1.0 / D)
        acc_ref[0:1, :] += jnp.sum(dy * xhat, axis=0, keepdims=True)
        acc_ref[1:2, :] += jnp.sum(err * err, axis=0, keepdims=True)
        t = dy * g_ref[...]
        dh_ref[...] = r * (t - xhat * jnp.mean(t * xhat, axis=-1, keepdims=True))

    row = pl.BlockSpec((tm, D), lambda i: (i, 0))
    dh, acc = pl.pallas_call(
        body, name=name,
        out_shape=(jax.ShapeDtypeStruct((T, D), F32), jax.ShapeDtypeStruct((SUBLANES_F32, D), F32)),
        grid=(T // tm,),
        in_specs=[row, pl.BlockSpec((1, D), lambda i: (0, 0)), row],
        out_specs=(row, pl.BlockSpec((SUBLANES_F32, D), lambda i: (0, 0))),
        compiler_params=_params("arbitrary"),
    )(h, g.reshape(1, D), target)
    return acc[1], dh, acc[0]


def _mm_nn(a, b, res, out_dtype, name):
    T, K = a.shape
    N = b.shape[1]
    tm = _tile(T, 512, SUBLANES_BF16)
    tn = _tile(N, 1536, LANES)

    def body(a_ref, b_ref, *rest):
        o_ref = rest[-1]
        acc = jnp.dot(a_ref[...].astype(BF16), b_ref[...], preferred_element_type=F32)
        if res is not None:
            acc = acc + rest[0][...]
        o_ref[...] = acc.astype(o_ref.dtype)

    in_specs = [pl.BlockSpec((tm, K), lambda j, i: (i, 0)), pl.BlockSpec((K, tn), lambda j, i: (0, j))]
    args = [a, b]
    if res is not None:
        in_specs.append(pl.BlockSpec((tm, tn), lambda j, i: (i, j)))
        args.append(res)
    return pl.pallas_call(
        body, name=name, out_shape=jax.ShapeDtypeStruct((T, N), out_dtype), grid=(N // tn, T // tm),
        in_specs=in_specs, out_specs=pl.BlockSpec((tm, tn), lambda j, i: (i, j)),
        compiler_params=_params("parallel", "parallel"),
    )(*args)


def _mm_nt(dy, b, out_dtype, name):
    T, N = dy.shape
    K = b.shape[0]
    tm = _tile(T, 1024, SUBLANES_BF16)
    tk = _tile(K, 1536, LANES)
    tn = _tile(N, 1536, LANES)
    n_steps = N // tn

    def body(dy_ref, b_ref, o_ref, acc_ref):
        n = pl.program_id(2)

        @pl.when(n == 0)
        def _():
            acc_ref[...] = jnp.zeros_like(acc_ref)

        acc_ref[...] += lax.dot_general(dy_ref[...].astype(BF16), b_ref[...], (((1,), (1,)), ((), ())),
                                        preferred_element_type=F32)

        @pl.when(n == n_steps - 1)
        def _():
            o_ref[...] = acc_ref[...].astype(o_ref.dtype)

    return pl.pallas_call(
        body, name=name, out_shape=jax.ShapeDtypeStruct((T, K), out_dtype), grid=(T // tm, K // tk, n_steps),
        in_specs=[pl.BlockSpec((tm, tn), lambda i, k, n: (i, n)), pl.BlockSpec((tk, tn), lambda i, k, n: (k, n))],
        out_specs=pl.BlockSpec((tm, tk), lambda i, k, n: (i, k)),
        scratch_shapes=[pltpu.VMEM((tm, tk), F32)],
        compiler_params=_params("parallel", "parallel", "arbitrary"),
    )(dy, b)


def _mm_tn(a, dy, name):
    T, K = a.shape
    N = dy.shape[1]
    tt = _tile(T, 512, SUBLANES_BF16)
    tk = _tile(K, 1536, LANES)
    tn = _tile(N, 1536, LANES)
    t_steps = T // tt

    def body(a_ref, dy_ref, o_ref, acc_ref):
        t = pl.program_id(2)

        @pl.when(t == 0)
        def _():
            acc_ref[...] = jnp.zeros_like(acc_ref)

        acc_ref[...] += lax.dot_general(a_ref[...].astype(BF16), dy_ref[...].astype(BF16),
                                        (((0,), (0,)), ((), ())), preferred_element_type=F32)

        @pl.when(t == t_steps - 1)
        def _():
            o_ref[...] = acc_ref[...].astype(o_ref.dtype)

    return pl.pallas_call(
        body, name=name, out_shape=jax.ShapeDtypeStruct((K, N), BF16), grid=(K // tk, N // tn, t_steps),
        in_specs=[pl.BlockSpec((tt, tk), lambda k, n, t: (t, k)), pl.BlockSpec((tt, tn), lambda k, n, t: (t, n))],
        out_specs=pl.BlockSpec((tk, tn), lambda k, n, t: (k, n)),
        scratch_shapes=[pltpu.VMEM((tk, tn), F32)],
        compiler_params=_params("parallel", "parallel", "arbitrary"),
    )(a, dy)


def _rows_before(halo, cur, k):
    h = halo.shape[0]
    return pltpu.roll(jnp.concatenate([halo, cur], axis=0), k, 0)[h:]


def _rows_after(cur, halo, k):
    n = cur.shape[0]
    total = n + halo.shape[0]
    return pltpu.roll(jnp.concatenate([cur, halo], axis=0), total - k, 0)[:n]


def _halo_specs(tm, width, n_rows):
    per = tm // SUBLANES_BF16
    last = n_rows // SUBLANES_BF16 - 1
    prev = pl.BlockSpec((SUBLANES_BF16, width), lambda i: (jnp.maximum(i * per - 1, 0), 0))
    nxt = pl.BlockSpec((SUBLANES_BF16, width), lambda i: (jnp.minimum((i + 1) * per, last), 0))
    return prev, nxt


def _gate_a_fwd(p, cw, seq, name):
    T, D3 = p.shape
    D = D3 // 3
    tm = _tile(seq, 512, SUBLANES_BF16)
    cc = _tile(D, 256, LANES)
    prev, _ = _halo_specs(tm, D3, T)

    def body(p_ref, ph_ref, cw_ref, z_ref):
        at_start = (pl.program_id(0) * tm) % seq == 0
        for c0 in range(0, D, cc):
            b = p_ref[:, c0:c0 + cc].astype(F32)
            u = p_ref[:, D + c0:D + c0 + cc].astype(F32) * p_ref[:, 2 * D + c0:2 * D + c0 + cc].astype(F32)
            uh = ph_ref[:, D + c0:D + c0 + cc].astype(F32) * ph_ref[:, 2 * D + c0:2 * D + c0 + cc].astype(F32)
            uh = jnp.where(at_start, 0.0, uh)
            w = cw_ref[:, c0:c0 + cc]
            cv = _rows_before(uh, u, 2) * w[0:1] + _rows_before(uh, u, 1) * w[1:2] + u * w[2:3]
            z_ref[:, c0:c0 + cc] = (b * cv).astype(z_ref.dtype)

    return pl.pallas_call(
        body, name=name, out_shape=jax.ShapeDtypeStruct((T, D), BF16), grid=(T // tm,),
        in_specs=[pl.BlockSpec((tm, D3), lambda i: (i, 0)), prev, pl.BlockSpec((3, D), lambda i: (0, 0))],
        out_specs=pl.BlockSpec((tm, D), lambda i: (i, 0)),
        compiler_params=_params("parallel"),
    )(p, p, cw)


def _gate_a_bwd(p, cw, dz, seq, name):
    T, D3 = p.shape
    D = D3 // 3
    tm = _tile(seq, 512, SUBLANES_BF16)
    cc = _tile(D, 256, LANES)
    p_prev, p_next = _halo_specs(tm, D3, T)
    _, dz_next = _halo_specs(tm, D, T)

    def body(p_ref, pp_ref, pn_ref, dz_ref, dzn_ref, cw_ref, dp_ref, dcw_ref):
        i = pl.program_id(0)

        @pl.when(i == 0)
        def _():
            dcw_ref[...] = jnp.zeros_like(dcw_ref)

        at_start = (i * tm) % seq == 0
        at_end = ((i + 1) * tm) % seq == 0
        for c0 in range(0, D, cc):
            cb, cc_, ch = slice(c0, c0 + cc), slice(D + c0, D + c0 + cc), slice(2 * D + c0, 2 * D + c0 + cc)
            b = p_ref[:, cb].astype(F32)
            c = p_ref[:, cc_].astype(F32)
            hh = p_ref[:, ch].astype(F32)
            u = c * hh
            uh = jnp.where(at_start, 0.0, pp_ref[:, cc_].astype(F32) * pp_ref[:, ch].astype(F32))
            w = cw_ref[:, cb]
            u1 = _rows_before(uh, u, 1)
            u2 = _rows_before(uh, u, 2)
            cv = u2 * w[0:1] + u1 * w[1:2] + u * w[2:3]
            dz_t = dz_ref[:, cb].astype(F32)
            dcv = dz_t * b
            dcvn = jnp.where(at_end, 0.0, dzn_ref[:, cb].astype(F32) * pn_ref[:, cb].astype(F32))
            du = dcv * w[2:3] + _rows_after(dcv, dcvn, 1) * w[1:2] + _rows_after(dcv, dcvn, 2) * w[0:1]
            dp_ref[:, cb] = (dz_t * cv).astype(dp_ref.dtype)
            dp_ref[:, cc_] = (du * hh).astype(dp_ref.dtype)
            dp_ref[:, ch] = (du * c).astype(dp_ref.dtype)
            dcw_ref[0:1, cb] += jnp.sum(dcv * u2, axis=0, keepdims=True)
            dcw_ref[1:2, cb] += jnp.sum(dcv * u1, axis=0, keepdims=True)
            dcw_ref[2:3, cb] += jnp.sum(dcv * u, axis=0, keepdims=True)

    dp, dcw = pl.pallas_call(
        body, name=name,
        out_shape=(jax.ShapeDtypeStruct((T, D3), BF16), jax.ShapeDtypeStruct((SUBLANES_F32, D), F32)),
        grid=(T // tm,),
        in_specs=[pl.BlockSpec((tm, D3), lambda i: (i, 0)), p_prev, p_next,
                  pl.BlockSpec((tm, D), lambda i: (i, 0)), dz_next, pl.BlockSpec((3, D), lambda i: (0, 0))],
        out_specs=(pl.BlockSpec((tm, D3), lambda i: (i, 0)), pl.BlockSpec((SUBLANES_F32, D), lambda i: (0, 0))),
        compiler_params=_params("arbitrary"),
    )(p, p, p, dz, dz, cw)
    return dp, dcw[0:3]


def _ffn_gate_fwd(u, cw, cb, seq, name):
    T, F2 = u.shape
    F = F2 // 2
    tm = _tile(seq, 256, SUBLANES_BF16)
    cc = _tile(F, 256, LANES)
    prev, _ = _halo_specs(tm, F2, T)

    def body(u_ref, uh_ref, cw_ref, cb_ref, a_ref):
        at_start = (pl.program_id(0) * tm) % seq == 0

        def conv(c0):
            cols = slice(c0, c0 + cc)
            cur = u_ref[:, cols].astype(F32)
            halo = jnp.where(at_start, 0.0, uh_ref[:, cols].astype(F32))
            w = cw_ref[:, cols]
            return (_rows_before(halo, cur, 2) * w[0:1] + _rows_before(halo, cur, 1) * w[1:2] + cur * w[2:3]
                    + cb_ref[:, cols])

        for c0 in range(0, F, cc):
            g = conv(c0)
            up = conv(F + c0)
            a_ref[:, c0:c0 + cc] = ((g * jax.nn.sigmoid(g)) * up).astype(a_ref.dtype)

    return pl.pallas_call(
        body, name=name, out_shape=jax.ShapeDtypeStruct((T, F), BF16), grid=(T // tm,),
        in_specs=[pl.BlockSpec((tm, F2), lambda i: (i, 0)), prev,
                  pl.BlockSpec((3, F2), lambda i: (0, 0)), pl.BlockSpec((1, F2), lambda i: (0, 0))],
        out_specs=pl.BlockSpec((tm, F), lambda i: (i, 0)),
        compiler_params=_params("parallel"),
    )(u, u, cw, cb.reshape(1, F2))


def _ffn_gate_bwd(u, cw, cb, da, seq, name):
    T, F2 = u.shape
    F = F2 // 2
    H = SUBLANES_BF16
    tm = _tile(seq, 256, H)
    cc = _tile(F, 256, LANES)
    u_prev, u_next = _halo_specs(tm, F2, T)
    _, da_next = _halo_specs(tm, F, T)

    def body(u_ref, up_ref, un_ref, da_ref, dan_ref, cw_ref, cb_ref, du_ref, acc_ref):
        i = pl.program_id(0)

        @pl.when(i == 0)
        def _():
            acc_ref[...] = jnp.zeros_like(acc_ref)

        at_start = (i * tm) % seq == 0
        at_end = ((i + 1) * tm) % seq == 0

        def conv_ext(cols):
            ext = jnp.concatenate([jnp.where(at_start, 0.0, up_ref[:, cols].astype(F32)),
                                   u_ref[:, cols].astype(F32), un_ref[:, cols].astype(F32)], axis=0)
            w = cw_ref[:, cols]
            e1 = pltpu.roll(ext, 1, 0)
            e2 = pltpu.roll(ext, 2, 0)
            out = (e2 * w[0:1] + e1 * w[1:2] + ext * w[2:3] + cb_ref[:, cols])[H:]
            return out, ext, e1, e2

        def back(d2, cols, ext, e1, e2):
            w = cw_ref[:, cols]
            n = tm + H
            d1n = pltpu.roll(d2, n - 1, 0)[:tm]
            d2n = pltpu.roll(d2, n - 2, 0)[:tm]
            d_t = d2[:tm]
            du_ref[:, cols] = (d_t * w[2:3] + d1n * w[1:2] + d2n * w[0:1]).astype(du_ref.dtype)
            acc_ref[0:1, cols] += jnp.sum(d_t * e2[H:H + tm], axis=0, keepdims=True)
            acc_ref[1:2, cols] += jnp.sum(d_t * e1[H:H + tm], axis=0, keepdims=True)
            acc_ref[2:3, cols] += jnp.sum(d_t * ext[H:H + tm], axis=0, keepdims=True)
            acc_ref[3:4, cols] += jnp.sum(d_t, axis=0, keepdims=True)

        for c0 in range(0, F, cc):
            gc, uc = slice(c0, c0 + cc), slice(F + c0, F + c0 + cc)
            g, g_ext, g_e1, g_e2 = conv_ext(gc)
            up, up_ext, up_e1, up_e2 = conv_ext(uc)
            da_ext = jnp.concatenate([da_ref[:, gc].astype(F32),
                                      jnp.where(at_end, 0.0, dan_ref[:, gc].astype(F32))], axis=0)
            sg = jax.nn.sigmoid(g)
            d_up = da_ext * (g * sg)
            d_g = da_ext * up * (sg * (1.0 + g * (1.0 - sg)))
            back(d_g, gc, g_ext, g_e1, g_e2)
            back(d_up, uc, up_ext, up_e1, up_e2)

    du, acc = pl.pallas_call(
        body, name=name,
        out_shape=(jax.ShapeDtypeStruct((T, F2), BF16), jax.ShapeDtypeStruct((SUBLANES_F32, F2), F32)),
        grid=(T // tm,),
        in_specs=[pl.BlockSpec((tm, F2), lambda i: (i, 0)), u_prev, u_next,
                  pl.BlockSpec((tm, F), lambda i: (i, 0)), da_next,
                  pl.BlockSpec((3, F2), lambda i: (0, 0)), pl.BlockSpec((1, F2), lambda i: (0, 0))],
        out_specs=(pl.BlockSpec((tm, F2), lambda i: (i, 0)), pl.BlockSpec((SUBLANES_F32, F2), lambda i: (0, 0))),
        compiler_params=_params("arbitrary"),
    )(u, u, u, da, da, cw, cb.reshape(1, F2))
    return du, acc[0:3], acc[3]


def _bucket_map():
    P = ATT_BLOCK
    qi = np.arange(P, dtype=np.int64)[:, None]
    kc = np.arange(2 * P, dtype=np.int64)[None, :]
    delta = qi + P - kc
    maps = []
    max_exact = REL_BUCKETS // 2
    for window, dilation in DILATED_BRANCHES:
        band = (delta >= 0) & (delta <= window // dilation)
        n = np.maximum(delta * dilation, 0)
        nf = np.maximum(n, max_exact).astype(np.float32)
        large = max_exact + (np.log(nf / np.float32(max_exact)) / np.float32(math.log(REL_MAX_DISTANCE / max_exact))
                             * np.float32(REL_BUCKETS - max_exact)).astype(np.int32)
        large = np.minimum(large, REL_BUCKETS - 1)
        bucket = np.where(n < max_exact, n, large)
        maps.append(np.where(band, bucket, -1).astype(np.int32))
    return np.stack(maps)


def _bias_tables(rel_bias, bmap, name):
    n_pairs = rel_bias.shape[1] // 2
    nbr, P, P2 = bmap.shape

    def body(rb_ref, bm_ref, o_ref):
        pair = pl.program_id(0)
        in_seq = lax.broadcasted_iota(jnp.int32, (P, P2), 1) >= P
        for br in range(nbr):
            bm = bm_ref[br]
            for hh in range(2):
                acc = jnp.full((P, P2), MASKED_LOGIT, F32)
                for b in range(REL_BUCKETS):
                    acc = jnp.where(bm == b, rb_ref[b, 2 * pair + hh], acc)
                o_ref[br, 0, 0, hh * P:(hh + 1) * P, :] = acc
                o_ref[br, 0, 1, hh * P:(hh + 1) * P, :] = jnp.where(in_seq, acc, MASKED_LOGIT)

    return pl.pallas_call(
        body, name=name, out_shape=jax.ShapeDtypeStruct((nbr, n_pairs, 2, 2 * P, P2), F32), grid=(n_pairs,),
        in_specs=[pl.BlockSpec(memory_space=pltpu.SMEM), pl.BlockSpec((nbr, P, P2), lambda h: (0, 0, 0))],
        out_specs=pl.BlockSpec((nbr, 1, 2, 2 * P, P2), lambda h: (0, h, 0, 0, 0)),
        compiler_params=_params("parallel"),
    )(rel_bias, bmap)


def _bias_grad(dbias, bmap, name):
    nbr, n_pairs, _, P2 = dbias.shape
    P = P2 // 2

    def body(db_ref, bm_ref, o_ref):
        lane = lax.broadcasted_iota(jnp.int32, (1, LANES), 1)
        for hh in range(2):
            row = jnp.zeros((1, LANES), F32)
            for br in range(nbr):
                bm = bm_ref[br]
                d = db_ref[br, 0, hh * P:(hh + 1) * P, :]
                for b in range(REL_BUCKETS):
                    hit = jnp.sum(jnp.where(bm == b, d, 0.0), axis=1, keepdims=True)
                    row = row + jnp.where(lane == b, jnp.sum(hit, axis=0, keepdims=True), 0.0)
            o_ref[hh] = row

    return pl.pallas_call(
        body, name=name, out_shape=jax.ShapeDtypeStruct((2 * n_pairs, 1, LANES), F32), grid=(n_pairs,),
        in_specs=[pl.BlockSpec((nbr, 1, P2, P2), lambda h: (0, h, 0, 0)), pl.BlockSpec((nbr, P, P2), lambda h: (0, 0, 0))],
        out_specs=pl.BlockSpec((2, 1, LANES), lambda h: (h, 0, 0)),
        compiler_params=_params("parallel"),
    )(dbias, bmap)[:, 0, :]


def _rows(start, dilation):
    if dilation == 1:
        return pl.ds(pl.multiple_of(start, ATT_BLOCK), ATT_BLOCK)
    return pl.ds(start, ATT_BLOCK, stride=dilation)


def _for_each_block(seq, unroll, fn):
    P = ATT_BLOCK
    for br, (_, d) in enumerate(DILATED_BRANCHES):
        nb = seq // d // P
        total = nb * d
        u = unroll if total % unroll == 0 else 1

        def some(i, carry, br=br, d=d, nb=nb, u=u):
            blocks = []
            for k in range(u):
                idx = i * u + k
                r, j = idx // nb, idx % nb
                blocks.append((r + j * (d * P), r + jnp.maximum(j - 1, 0) * (d * P), jnp.where(j == 0, 1, 0)))
            fn(br, d, blocks)
            return carry

        lax.fori_loop(0, total // u, some, 0)


def _stack_heads(x, head0):
    return jnp.concatenate([jnp.where(head0, x, 0.0), jnp.where(head0, 0.0, x)], axis=0).astype(BF16)


def _window(ref, start, prev, dilation):
    return jnp.concatenate([ref[_rows(prev, dilation), :], ref[_rows(start, dilation), :]], axis=0).astype(BF16)


def _attn_fwd(q, kv, bias, name):
    B, S, D = q.shape
    P = ATT_BLOCK
    n_pairs = D // LANES
    nbr = len(DILATED_BRANCHES)
    scale = HEAD_DIM ** -0.5

    def body(q_ref, k_ref, v_ref, bias_ref, o_ref, lse_ref, *stats):
        m_s, l_s, acc_s = stats[0:nbr], stats[nbr:2 * nbr], stats[2 * nbr:3 * nbr]
        head0 = lax.broadcasted_iota(jnp.int32, (P, LANES), 1) < HEAD_DIM

        def block(br, d, blocks):
            s = [lax.dot_general(_stack_heads(q_ref[_rows(start, d), :] * scale, head0),
                                 _window(k_ref, start, prev, d), (((1,), (1,)), ((), ())),
                                 preferred_element_type=F32) + bias_ref[br, 0, first]
                 for start, prev, first in blocks]
            m = [jnp.max(x, axis=-1, keepdims=True) for x in s]
            p = [jnp.exp(x - y) for x, y in zip(s, m)]
            l = [jnp.sum(x, axis=-1, keepdims=True) for x in p]
            pv = [jnp.dot(x.astype(BF16), _window(v_ref, start, prev, d), preferred_element_type=F32)
                  for x, (start, prev, _) in zip(p, blocks)]
            for k, (start, _, _) in enumerate(blocks):
                rows = _rows(start, d)
                m_s[br][rows, :] = jnp.where(head0, m[k][:P], m[k][P:])
                l_s[br][rows, :] = jnp.where(head0, l[k][:P], l[k][P:])
                acc_s[br][rows, :] = jnp.where(head0, pv[k][:P], pv[k][P:])

        _for_each_block(S, ATTN_FWD_UNROLL, block)

        chunk = _tile(S, 256, SUBLANES_F32)

        def merge(i, carry):
            rows = pl.ds(pl.multiple_of(i * chunk, chunk), chunk)
            ms = [m_s[br][rows, :] for br in range(nbr)]
            m = functools.reduce(jnp.maximum, ms)
            l = jnp.zeros((chunk, LANES), F32)
            acc = jnp.zeros((chunk, LANES), F32)
            for br in range(nbr):
                w = jnp.exp(ms[br] - m)
                l = l + w * l_s[br][rows, :]
                acc = acc + w * acc_s[br][rows, :]
            o_ref[rows, :] = acc / l
            lse_ref[rows, :] = m + jnp.log(l)
            return carry

        lax.fori_loop(0, S // chunk, merge, 0)

    slab = lambda col0: pl.BlockSpec((None, S, LANES), lambda b, h: (b, 0, col0 + h))
    return pl.pallas_call(
        body, name=name,
        out_shape=(jax.ShapeDtypeStruct((B, S, D), F32), jax.ShapeDtypeStruct((B, S, D), F32)),
        grid=(B, n_pairs),
        in_specs=[slab(0), slab(0), slab(n_pairs),
                  pl.BlockSpec((nbr, 1, 2, 2 * P, 2 * P), lambda b, h: (0, h, 0, 0, 0))],
        out_specs=(slab(0), slab(0)),
        scratch_shapes=[pltpu.VMEM((S, LANES), F32)] * (3 * nbr),
        compiler_params=_params("parallel", "parallel"),
    )(q, kv, kv, bias)


def _attn_bwd(q, kv, o, lse, do, bias, name):
    B, S, D = q.shape
    P = ATT_BLOCK
    n_pairs = D // LANES
    nbr = len(DILATED_BRANCHES)
    scale = HEAD_DIM ** -0.5

    def body(q_ref, k_ref, v_ref, o_ref, lse_ref, do_ref, bias_ref, dq_ref, dk_ref, dv_ref, dbias_ref, delta_s):
        head0 = lax.broadcasted_iota(jnp.int32, (P, LANES), 1) < HEAD_DIM

        @pl.when(pl.program_id(1) == 0)
        def _():
            dbias_ref[...] = jnp.zeros_like(dbias_ref)

        chunk = _tile(S, 512, SUBLANES_F32)

        def prepare(i, carry):
            rows = pl.ds(pl.multiple_of(i * chunk, chunk), chunk)
            x = do_ref[rows, :] * o_ref[rows, :]
            h0 = lax.broadcasted_iota(jnp.int32, (chunk, LANES), 1) < HEAD_DIM
            d0 = jnp.sum(jnp.where(h0, x, 0.0), axis=-1, keepdims=True)
            d1 = jnp.sum(jnp.where(h0, 0.0, x), axis=-1, keepdims=True)
            delta_s[rows, :] = jnp.where(h0, d0, d1)
            zero = jnp.zeros((chunk, LANES), F32)
            dq_ref[rows, :] = zero
            dk_ref[rows, :] = zero
            dv_ref[rows, :] = zero
            return carry

        lax.fori_loop(0, S // chunk, prepare, 0)

        def per_head(x):
            return jnp.concatenate([x[:, 0:1], x[:, HEAD_DIM:HEAD_DIM + 1]], axis=0)

        nt = (((1,), (1,)), ((), ()))
        tn = (((0,), (0,)), ((), ()))

        def block(br, d, blocks):
            q2 = [_stack_heads(q_ref[_rows(start, d), :] * scale, head0) for start, _, _ in blocks]
            do2 = [_stack_heads(do_ref[_rows(start, d), :], head0) for start, _, _ in blocks]
            kb = [_window(k_ref, start, prev, d) for start, prev, _ in blocks]
            vb = [_window(v_ref, start, prev, d) for start, prev, _ in blocks]
            s = [lax.dot_general(a, b, nt, preferred_element_type=F32) + bias_ref[br, 0, first]
                 for a, b, (_, _, first) in zip(q2, kb, blocks)]
            dp = [lax.dot_general(a, b, nt, preferred_element_type=F32) for a, b in zip(do2, vb)]
            p = [jnp.exp(x - per_head(lse_ref[_rows(start, d), :])) for x, (start, _, _) in zip(s, blocks)]
            ds = [x * (y - per_head(delta_s[_rows(start, d), :])) for x, y, (start, _, _) in zip(p, dp, blocks)]
            for x in ds:
                dbias_ref[br, 0] += x
            ds16 = [x.astype(BF16) for x in ds]
            dq2 = [jnp.dot(a, b, preferred_element_type=F32) for a, b in zip(ds16, kb)]
            dk = [lax.dot_general(a, b, tn, preferred_element_type=F32) for a, b in zip(ds16, q2)]
            dv = [lax.dot_general(a.astype(BF16), b, tn, preferred_element_type=F32) for a, b in zip(p, do2)]
            for k, (start, prev, _) in enumerate(blocks):
                rows, prows = _rows(start, d), _rows(prev, d)
                dq_ref[rows, :] += jnp.where(head0, dq2[k][:P], dq2[k][P:]) * scale
                dk_ref[prows, :] += dk[k][:P]
                dk_ref[rows, :] += dk[k][P:]
                dv_ref[prows, :] += dv[k][:P]
                dv_ref[rows, :] += dv[k][P:]

        _for_each_block(S, ATTN_BWD_UNROLL, block)

    slab = lambda col0: pl.BlockSpec((None, S, LANES), lambda h, b: (b, 0, col0 + h))
    tab = pl.BlockSpec((nbr, 1, 2, 2 * P, 2 * P), lambda h, b: (0, h, 0, 0, 0))
    dtab = pl.BlockSpec((nbr, 1, 2 * P, 2 * P), lambda h, b: (0, h, 0, 0))
    shp = jax.ShapeDtypeStruct((B, S, D), F32)
    return pl.pallas_call(
        body, name=name,
        out_shape=(shp, shp, shp, jax.ShapeDtypeStruct((nbr, n_pairs, 2 * P, 2 * P), F32)),
        grid=(n_pairs, B),
        in_specs=[slab(0), slab(0), slab(n_pairs), slab(0), slab(0), slab(0), tab],
        out_specs=(slab(0), slab(0), slab(0), dtab),
        scratch_shapes=[pltpu.VMEM((S, LANES), F32)],
        compiler_params=_params("parallel", "arbitrary"),
    )(q, kv, kv, o, lse, do, bias)


def _adamw(w, g, m, v, name):
    R, C = w.shape
    tr = _tile(R, 256, SUBLANES_F32) if R % SUBLANES_F32 == 0 else R
    tc = _tile(C, 2048, LANES) if C % LANES == 0 else C

    def body(w_ref, g_ref, m_ref, v_ref, d_ref, nm_ref, nv_ref):
        g_ = g_ref[...]
        m2 = ADAM_B1 * m_ref[...] + (1.0 - ADAM_B1) * g_
        v2 = ADAM_B2 * v_ref[...] + (1.0 - ADAM_B2) * (g_ * g_)
        m_hat = m2 / (1.0 - ADAM_B1 ** ADAM_STEP)
        v_hat = v2 / (1.0 - ADAM_B2 ** ADAM_STEP)
        d_ref[...] = -ADAM_LR * (m_hat / (jnp.sqrt(v_hat) + ADAM_EPS) + ADAM_WD * w_ref[...])
        nm_ref[...] = m2
        nv_ref[...] = v2

    blk = pl.BlockSpec((tr, tc), lambda i, j: (i, j))
    shp = jax.ShapeDtypeStruct((R, C), F32)
    return pl.pallas_call(
        body, name=name, out_shape=(shp, shp, shp), grid=(R // tr, C // tc),
        in_specs=[blk] * 4, out_specs=(blk,) * 3, compiler_params=_params("parallel", "parallel"),
    )(w, g, m, v)


def _sum_slots(slots, name):
    n, R, C = slots.shape
    tr = _tile(R, 256, SUBLANES_BF16) if R % SUBLANES_BF16 == 0 else R
    tc = _tile(C, 2048, LANES) if C % LANES == 0 else C

    def body(s_ref, o_ref):
        acc = s_ref[0].astype(F32)
        for k in range(1, n):
            acc = acc + s_ref[k].astype(F32)
        o_ref[...] = acc

    return pl.pallas_call(
        body, name=name, out_shape=jax.ShapeDtypeStruct((R, C), F32), grid=(R // tr, C // tc),
        in_specs=[pl.BlockSpec((n, tr, tc), lambda i, j: (0, i, j))],
        out_specs=pl.BlockSpec((tr, tc), lambda i, j: (i, j)),
        compiler_params=_params("parallel", "parallel"),
    )(slots)


def _my_place():
    return lax.axis_index("x"), lax.axis_index("y"), lax.axis_index("c")


def _other_chips(x, y):
    return [(1 - x, y), (x, 1 - y), (1 - x, 1 - y)]


def _piece(ref, blk, axis, shard, half):
    h0 = blk[0] // 2
    idx = []
    for dim, n in enumerate(blk):
        if dim == 0:
            start = half * h0 + (shard * n if axis == 0 else 0)
            idx.append(pl.ds(start, h0))
        elif dim == axis:
            idx.append(pl.ds(shard * n, n))
        else:
            idx.append(slice(None))
    return ref.at[tuple(idx)]


def _shard_of(ref, blk, axis, shard):
    idx = [pl.ds(shard * n, n) if dim == axis else slice(None) for dim, n in enumerate(blk)]
    return ref.at[tuple(idx)]


def _half_of(ref, blk, half):
    return ref.at[pl.ds(half * (blk[0] // 2), blk[0] // 2)]


def _gather_weights(shards, axes, small):
    n = len(shards)
    blks = [s.shape for s in shards]
    fulls = [tuple(d * N_SHARDS if i == ax else d for i, d in enumerate(b)) for b, ax in zip(blks, axes)]

    def stage1(*refs):
        ins, small_in = refs[:n], refs[n]
        outs, small_out = refs[n + 1:2 * n + 1], refs[2 * n + 1]
        send_sems, recv_sems, local_sems = refs[2 * n + 2:]
        x, y, c = _my_place()
        me = 2 * x + y
        chips = _other_chips(x, y)
        local = [pltpu.make_async_copy(ins[a], _shard_of(outs[a], blks[a], axes[a], me), local_sems.at[a])
                 for a in range(n)]
        local.append(pltpu.make_async_copy(small_in, small_out.at[me], local_sems.at[n]))
        for cp in local:
            cp.start()
        sends, recvs = [], []
        for a in range(n + 1):
            for k, (px, py) in enumerate(chips):
                peer = 2 * px + py
                if a < n:
                    src = _half_of(ins[a], blks[a], c)
                    dst = _piece(outs[a], blks[a], axes[a], me, c)
                    land = _piece(outs[a], blks[a], axes[a], peer, c)
                else:
                    src, dst, land = small_in, small_out.at[me], small_out.at[peer]
                sends.append(pltpu.make_async_remote_copy(
                    src_ref=src, dst_ref=dst, send_sem=send_sems.at[a, k], recv_sem=recv_sems.at[a, k],
                    device_id=(px, py, c), device_id_type=pl.DeviceIdType.MESH))
                recvs.append(pltpu.make_async_remote_copy(
                    src_ref=src, dst_ref=land, send_sem=send_sems.at[a, k], recv_sem=recv_sems.at[a, k],
                    device_id=(px, py, c), device_id_type=pl.DeviceIdType.MESH))
        for cp in sends:
            cp.start()
        for cp in recvs:
            cp.wait_recv()
        for cp in sends:
            cp.wait_send()
        for cp in local:
            cp.wait()

    out_shape = [jax.ShapeDtypeStruct(f, s.dtype) for f, s in zip(fulls, shards)]
    out_shape.append(jax.ShapeDtypeStruct((N_SHARDS,) + small.shape, small.dtype))
    res = pl.pallas_call(
        stage1, name="gather_weights_ici", out_shape=out_shape,
        in_specs=[ANY] * (n + 1), out_specs=[ANY] * (n + 1),
        scratch_shapes=[pltpu.SemaphoreType.DMA((n + 1, 3)), pltpu.SemaphoreType.DMA((n + 1, 3)),
                        pltpu.SemaphoreType.DMA((n + 1,))],
    )(*shards, small)
    partial, small_full = res[:n], res[n]

    def stage2(*refs):
        outs = refs[n:2 * n]
        send_sems, recv_sems = refs[2 * n:]
        x, y, c = _my_place()
        sends, recvs = [], []
        for a in range(n):
            for k, (px, py) in enumerate(_other_chips(x, y)):
                peer = 2 * px + py
                mine = _piece(outs[a], blks[a], axes[a], peer, c)
                theirs = _piece(outs[a], blks[a], axes[a], peer, 1 - c)
                sends.append(pltpu.make_async_remote_copy(
                    src_ref=mine, dst_ref=mine, send_sem=send_sems.at[a, k], recv_sem=recv_sems.at[a, k],
                    device_id=(x, y, 1 - c), device_id_type=pl.DeviceIdType.MESH))
                recvs.append(pltpu.make_async_remote_copy(
                    src_ref=mine, dst_ref=theirs, send_sem=send_sems.at[a, k], recv_sem=recv_sems.at[a, k],
                    device_id=(x, y, 1 - c), device_id_type=pl.DeviceIdType.MESH))
        for cp in sends:
            cp.start()
        for cp in recvs:
            cp.wait_recv()
        for cp in sends:
            cp.wait_send()

    full = pl.pallas_call(
        stage2, name="gather_weights_d2d", out_shape=out_shape[:n],
        in_specs=[ANY] * n, out_specs=[ANY] * n, input_output_aliases={a: a for a in range(n)},
        scratch_shapes=[pltpu.SemaphoreType.DMA((n, 3)), pltpu.SemaphoreType.DMA((n, 3))],
    )(*partial)
    return list(full), small_full


def _scatter_grads(grads, blks, axes, small):
    n = len(grads)
    halves = [(b[0] // 2,) + tuple(b[1:]) for b in blks]

    def body(*refs):
        ins, small_in = refs[:n], refs[n]
        outs, small_out = refs[n + 1:2 * n + 1], refs[2 * n + 1]
        send_sems, recv_sems, local_sems = refs[2 * n + 2:]
        x, y, c = _my_place()
        me = 4 * x + 2 * y + c
        local = [pltpu.make_async_copy(_piece(ins[a], blks[a], axes[a], 2 * x + y, c), outs[a].at[me],
                                       local_sems.at[a]) for a in range(n)]
        local.append(pltpu.make_async_copy(small_in, small_out.at[me], local_sems.at[n]))
        for cp in local:
            cp.start()
        sends, recvs = [], []
        for rel in range(1, N_DEVICES):
            px = x ^ ((rel >> 2) & 1)
            py = y ^ ((rel >> 1) & 1)
            pc = c ^ (rel & 1)
            peer = 4 * px + 2 * py + pc
            for a in range(n + 1):
                if a < n:
                    src = _piece(ins[a], blks[a], axes[a], 2 * px + py, pc)
                    dst, land = outs[a].at[me], outs[a].at[peer]
                else:
                    src, dst, land = small_in, small_out.at[me], small_out.at[peer]
                sends.append(pltpu.make_async_remote_copy(
                    src_ref=src, dst_ref=dst, send_sem=send_sems.at[a, rel - 1], recv_sem=recv_sems.at[a, rel - 1],
                    device_id=(px, py, pc), device_id_type=pl.DeviceIdType.MESH))
                recvs.append(pltpu.make_async_remote_copy(
                    src_ref=src, dst_ref=land, send_sem=send_sems.at[a, rel - 1], recv_sem=recv_sems.at[a, rel - 1],
                    device_id=(px, py, pc), device_id_type=pl.DeviceIdType.MESH))
        for cp in sends:
            cp.start()
        for cp in recvs:
            cp.wait_recv()
        for cp in sends:
            cp.wait_send()
        for cp in local:
            cp.wait()

    out_shape = [jax.ShapeDtypeStruct((N_DEVICES,) + h, g.dtype) for h, g in zip(halves, grads)]
    out_shape.append(jax.ShapeDtypeStruct((N_DEVICES,) + small.shape, small.dtype))
    res = pl.pallas_call(
        body, name="scatter_grads", out_shape=out_shape,
        in_specs=[ANY] * (n + 1), out_specs=[ANY] * (n + 1),
        scratch_shapes=[pltpu.SemaphoreType.DMA((n + 1, N_DEVICES - 1)), pltpu.SemaphoreType.DMA((n + 1, N_DEVICES - 1)),
                        pltpu.SemaphoreType.DMA((n + 1,))],
    )(*grads, small)
    return list(res[:n]), res[n]


def _swap_halves(halves, blks):
    n = len(halves)

    def body(*refs):
        ins, outs = refs[:n], refs[n:2 * n]
        send_sems, recv_sems, local_sems = refs[2 * n:]
        x, y, c = _my_place()
        local = [pltpu.make_async_copy(ins[a], _half_of(outs[a], blks[a], c), local_sems.at[a]) for a in range(n)]
        for cp in local:
            cp.start()
        sends, recvs = [], []
        for a in range(n):
            sends.append(pltpu.make_async_remote_copy(
                src_ref=ins[a], dst_ref=_half_of(outs[a], blks[a], c), send_sem=send_sems.at[a],
                recv_sem=recv_sems.at[a], device_id=(x, y, 1 - c), device_id_type=pl.DeviceIdType.MESH))
            recvs.append(pltpu.make_async_remote_copy(
                src_ref=ins[a], dst_ref=_half_of(outs[a], blks[a], 1 - c), send_sem=send_sems.at[a],
                recv_sem=recv_sems.at[a], device_id=(x, y, 1 - c), device_id_type=pl.DeviceIdType.MESH))
        for cp in sends:
            cp.start()
        for cp in recvs:
            cp.wait_recv()
        for cp in sends:
            cp.wait_send()
        for cp in local:
            cp.wait()

    return list(pl.pallas_call(
        body, name="swap_grad_halves", out_shape=[jax.ShapeDtypeStruct(b, F32) for b in blks],
        in_specs=[ANY] * n, out_specs=[ANY] * n,
        scratch_shapes=[pltpu.SemaphoreType.DMA((n,)), pltpu.SemaphoreType.DMA((n,)), pltpu.SemaphoreType.DMA((n,))],
    )(*halves))


def _pack(arrays):
    flat = jnp.concatenate([a.reshape(-1).astype(F32) for a in arrays])
    pad = (-flat.shape[0]) % (SUBLANES_F32 * LANES)
    return jnp.pad(flat, (0, pad)).reshape(-1, LANES)


def _unpack(packed, shapes):
    flat = packed.reshape(-1)
    out, off = [], 0
    for s in shapes:
        n = int(np.prod(s))
        out.append(flat[off:off + n].reshape(s))
        off += n
    return out


def _ffn_fwd(h, norm, w_up, cw, cb, w_down, seq, tag):
    xn = _rmsnorm_fwd(h, norm, f"ffn{tag}_norm")
    u = _mm_nn(xn, w_up, None, BF16, f"ffn{tag}_up")
    a = _ffn_gate_fwd(u, cw, cb, seq, f"ffn{tag}_gate")
    out = _mm_nn(a, w_down, h, F32, f"ffn{tag}_down")
    return out, (xn, u, a)


def _ffn_bwd(dout, h, norm, w_up, cw, cb, w_down, saved, seq, tag):
    xn, u, a = saved
    da = _mm_nt(dout, w_down, BF16, f"ffn{tag}_down_dx")
    g_down = _mm_tn(a, dout, f"ffn{tag}_down_dw")
    du, g_cw, g_cb = _ffn_gate_bwd(u, cw, cb, da, seq, f"ffn{tag}_gate_bwd")
    dxn = _mm_nt(du, w_up, BF16, f"ffn{tag}_up_dx")
    g_up = _mm_tn(xn, du, f"ffn{tag}_up_dw")
    dh, g_norm = _rmsnorm_bwd(h, norm, dxn, dout, f"ffn{tag}_norm_bwd")
    return dh, g_up, g_down, g_cw, g_cb, g_norm


def _local_step(x, target, W, small):
    B, S, D = x.shape
    T = B * S
    x2 = x.reshape(T, D)
    tgt = target.reshape(T, D)
    bmap = jnp.asarray(_bucket_map())

    xn0 = _rmsnorm_fwd(x2, small["a_norm"][0], "a_norm")
    p = _mm_nn(xn0, W["w_in"], None, BF16, "a_in")
    z = _gate_a_fwd(p, small["a_conv"][0], S, "a_gate")
    h1 = _mm_nn(z, W["w_out"], x2, F32, "a_out")
    h2, ffn0 = _ffn_fwd(h1, small["ffn_norm"][0], W["w_up"][0], small["ffn_conv"][0], small["ffn_conv_b"][0],
                        W["w_down"][0], S, 0)
    kvn = _rmsnorm_fwd(h2, small["kv_norm"], "kv_norm")
    kv = _mm_nn(kvn, W["w_kv"], None, F32, "kv_proj")
    xn3 = _rmsnorm_fwd(h2, small["b_norm"][0], "b_norm")
    q = _mm_nn(xn3, W["w_q"], None, F32, "q_proj")
    bias = _bias_tables(small["rel_bias"], bmap, "rel_bias_tables")
    q3, kv3 = q.reshape(B, S, D), kv.reshape(B, S, 2 * D)
    o3, lse3 = _attn_fwd(q3, kv3, bias, "attn_fwd")
    o = o3.reshape(T, D)
    h3 = _mm_nn(o, W["w_o"], h2, F32, "o_proj")
    h4, ffn1 = _ffn_fwd(h3, small["ffn_norm"][1], W["w_up"][1], small["ffn_conv"][1], small["ffn_conv_b"][1],
                        W["w_down"][1], S, 1)
    sq_err, dh4, g_final = _loss_head(h4, small["final_norm"], tgt, "loss_head")
    loss = 0.5 * jnp.sum(sq_err) / D

    dh3, g_up1, g_down1, g_cw1, g_cb1, g_fn1 = _ffn_bwd(
        dh4, h3, small["ffn_norm"][1], W["w_up"][1], small["ffn_conv"][1], small["ffn_conv_b"][1], W["w_down"][1],
        ffn1, S, 1)
    do = _mm_nt(dh3, W["w_o"], F32, "o_proj_dx")
    g_o = _mm_tn(o, dh3, "o_proj_dw")
    dq3, dk3, dv3, dbias = _attn_bwd(q3, kv3, o3, lse3, do.reshape(B, S, D), bias, "attn_bwd")
    g_rel = _bias_grad(dbias, bmap, "rel_bias_grad")[:, :REL_BUCKETS].T
    dq = dq3.reshape(T, D)
    dkv = jnp.concatenate([dk3, dv3], axis=-1).reshape(T, 2 * D)
    dxn3 = _mm_nt(dq, W["w_q"], BF16, "q_proj_dx")
    g_q = _mm_tn(xn3, dq, "q_proj_dw")
    dh2, g_bn = _rmsnorm_bwd(h2, small["b_norm"][0], dxn3, dh3, "b_norm_bwd")
    dkvn = _mm_nt(dkv, W["w_kv"], BF16, "kv_proj_dx")
    g_kv = _mm_tn(kvn, dkv, "kv_proj_dw")
    dh2, g_kvn = _rmsnorm_bwd(h2, small["kv_norm"], dkvn, dh2, "kv_norm_bwd")
    dh1, g_up0, g_down0, g_cw0, g_cb0, g_fn0 = _ffn_bwd(
        dh2, h1, small["ffn_norm"][0], W["w_up"][0], small["ffn_conv"][0], small["ffn_conv_b"][0], W["w_down"][0],
        ffn0, S, 0)
    dz = _mm_nt(dh1, W["w_out"], BF16, "a_out_dx")
    g_out = _mm_tn(z, dh1, "a_out_dw")
    dp, g_aconv = _gate_a_bwd(p, small["a_conv"][0], dz, S, "a_gate_bwd")
    dxn0 = _mm_nt(dp, W["w_in"], BF16, "a_in_dx")
    g_in = _mm_tn(xn0, dp, "a_in_dw")
    dx, g_an = _rmsnorm_bwd(x2, small["a_norm"][0], dxn0, dh1, "a_norm_bwd")

    big = {"w_in": g_in, "w_out": g_out, "w_kv": g_kv, "w_q": g_q, "w_o": g_o,
           "w_up": jnp.stack([g_up0, g_up1]), "w_down": jnp.stack([g_down0, g_down1])}
    small_g = {"a_norm": g_an[None], "a_conv": g_aconv[None], "kv_norm": g_kvn, "b_norm": g_bn[None],
               "rel_bias": g_rel, "ffn_norm": jnp.stack([g_fn0, g_fn1]), "ffn_conv": jnp.stack([g_cw0, g_cw1]),
               "ffn_conv_b": jnp.stack([g_cb0, g_cb1]), "final_norm": g_final}
    return loss, dx.reshape(B, S, D), big, small_g


BIG = ("w_in", "w_out", "w_kv", "w_q", "w_o", "w_up", "w_down")
SMALL = ("a_norm", "a_conv", "kv_norm", "b_norm", "rel_bias", "ffn_norm", "ffn_conv", "ffn_conv_b", "final_norm")
SMALL_SHARDED = ("a_norm", "a_conv", "ffn_conv")
WEIGHT_ORDER = ("a_norm", "a_w_in", "a_conv", "a_w_out", "kv_norm", "w_kv", "b_norm", "b_w_q", "b_w_o", "rel_bias",
                "ffn_norm", "ffn_w_up", "ffn_conv", "ffn_conv_b", "ffn_w_down", "final_norm")
BIG_OF = {"a_w_in": "w_in", "a_w_out": "w_out", "w_kv": "w_kv", "b_w_q": "w_q", "b_w_o": "w_o", "ffn_w_up": "w_up",
          "ffn_w_down": "w_down"}


def _as2d(a):
    return a.reshape(-1, a.shape[-1])


def kernel(x, a_norm, a_w_in, a_conv, a_w_out, kv_norm, w_kv, b_norm, b_w_q, b_w_o, rel_bias, ffn_norm, ffn_w_up, ffn_conv, ffn_conv_b, ffn_w_down, final_norm, loss_target, m_a_norm, m_a_w_in, m_a_conv, m_a_w_out, m_kv_norm, m_w_kv, m_b_norm, m_b_w_q, m_b_w_o, m_rel_bias, m_ffn_norm, m_ffn_w_up, m_ffn_conv, m_ffn_conv_b, m_ffn_w_down, m_final_norm, v_a_norm, v_a_w_in, v_a_conv, v_a_w_out, v_kv_norm, v_w_kv, v_b_norm, v_b_w_q, v_b_w_o, v_rel_bias, v_ffn_norm, v_ffn_w_up, v_ffn_conv, v_ffn_conv_b, v_ffn_w_down, v_final_norm):
    given = dict(a_norm=a_norm, a_w_in=a_w_in, a_conv=a_conv, a_w_out=a_w_out, kv_norm=kv_norm, w_kv=w_kv, b_norm=b_norm,
                 b_w_q=b_w_q, b_w_o=b_w_o, rel_bias=rel_bias, ffn_norm=ffn_norm, ffn_w_up=ffn_w_up, ffn_conv=ffn_conv,
                 ffn_conv_b=ffn_conv_b, ffn_w_down=ffn_w_down, final_norm=final_norm)
    mom_m = dict(a_norm=m_a_norm, a_w_in=m_a_w_in, a_conv=m_a_conv, a_w_out=m_a_w_out, kv_norm=m_kv_norm, w_kv=m_w_kv,
                 b_norm=m_b_norm, b_w_q=m_b_w_q, b_w_o=m_b_w_o, rel_bias=m_rel_bias, ffn_norm=m_ffn_norm,
                 ffn_w_up=m_ffn_w_up, ffn_conv=m_ffn_conv, ffn_conv_b=m_ffn_conv_b, ffn_w_down=m_ffn_w_down,
                 final_norm=m_final_norm)
    mom_v = dict(a_norm=v_a_norm, a_w_in=v_a_w_in, a_conv=v_a_conv, a_w_out=v_a_w_out, kv_norm=v_kv_norm, w_kv=v_w_kv,
                 b_norm=v_b_norm, b_w_q=v_b_w_q, b_w_o=v_b_w_o, rel_bias=v_rel_bias, ffn_norm=v_ffn_norm,
                 ffn_w_up=v_ffn_w_up, ffn_conv=v_ffn_conv, ffn_conv_b=v_ffn_conv_b, ffn_w_down=v_ffn_w_down,
                 final_norm=v_final_norm)

    shard = {"w_in": (a_w_in[0], 1), "w_out": (a_w_out[0], 0), "w_kv": (w_kv, 1), "w_q": (b_w_q[0], 0),
             "w_o": (b_w_o[0], 0), "w_up": (ffn_w_up, 2), "w_down": (ffn_w_down, 1)}
    blks = [shard[k][0].shape for k in BIG]
    axes = [shard[k][1] for k in BIG]

    small_sharded = [given[k] for k in SMALL_SHARDED]
    packed = _pack(small_sharded)
    fulls, packed_all = _gather_weights([shard[k][0].astype(BF16) for k in BIG], axes, packed)
    W = dict(zip(BIG, fulls))
    small = {k: given[k] for k in SMALL}
    per_shard = [_unpack(packed_all[j], [a.shape for a in small_sharded]) for j in range(N_SHARDS)]
    for i, k in enumerate(SMALL_SHARDED):
        small[k] = jnp.concatenate([per_shard[j][i] for j in range(N_SHARDS)], axis=-1)

    loss, grad_x, big_g, small_g = _local_step(x, loss_target, W, small)
    loss = lax.psum(loss, MESH_AXES)

    small_shapes = [small_g[k].shape for k in SMALL]
    slots, small_slots = _scatter_grads([big_g[k] for k in BIG], blks, axes, _pack([small_g[k] for k in SMALL]))
    halves = []
    for k, s, b in zip(BIG, slots, blks):
        summed = _sum_slots(s.reshape(N_DEVICES, -1, b[-1]), f"sum_{k}")
        halves.append(summed.reshape((b[0] // 2,) + tuple(b[1:])))
    reduced = dict(zip(BIG, _swap_halves(halves, blks)))
    small_sum = _sum_slots(small_slots, "sum_small")
    small_red = dict(zip(SMALL, _unpack(small_sum, small_shapes)))
    j = 2 * lax.axis_index("x") + lax.axis_index("y")
    for k in SMALL_SHARDED:
        w = given[k].shape[-1]
        small_red[k] = lax.dynamic_slice_in_dim(small_red[k], j * w, w, axis=small_red[k].ndim - 1)

    grads, deltas, new_m, new_v = {}, {}, {}, {}
    for name in WEIGHT_ORDER:
        if name in BIG_OF:
            g = reduced[BIG_OF[name]].reshape(given[name].shape)
            d, nm, nv = _adamw(_as2d(given[name]), _as2d(g), _as2d(mom_m[name]), _as2d(mom_v[name]), f"adamw_{name}")
            grads[name] = g
            deltas[name], new_m[name], new_v[name] = (t.reshape(given[name].shape) for t in (d, nm, nv))
    small_names = [n for n in WEIGHT_ORDER if n not in BIG_OF]
    for n in small_names:
        grads[n] = small_red[n].reshape(given[n].shape)
    sw, sg, sm, sv = (_pack([d[n] for n in small_names]) for d in (given, grads, mom_m, mom_v))
    d, nm, nv = _adamw(sw, sg, sm, sv, "adamw_small")
    shapes = [given[n].shape for n in small_names]
    for n, a, b_, c_ in zip(small_names, _unpack(d, shapes), _unpack(nm, shapes), _unpack(nv, shapes)):
        deltas[n], new_m[n], new_v[n] = a, b_, c_

    return (loss, grad_x, *[grads[n] for n in WEIGHT_ORDER], *[deltas[n] for n in WEIGHT_ORDER],
            *[new_m[n] for n in WEIGHT_ORDER], *[new_v[n] for n in WEIGHT_ORDER])
```

```python
import functools
import math

import numpy as np

import jax
import jax.numpy as jnp
from jax import lax
from jax.experimental import pallas as pl
from jax.experimental.pallas import tpu as pltpu

F32 = jnp.float32
BF16 = jnp.bfloat16

RMS_EPS = 1e-6
HEAD_DIM = 64
ATT_BLOCK = 128
DILATED_BRANCHES = ((128, 1), (512, 4), (2048, 16))
REL_BUCKETS = 32
REL_MAX_DISTANCE = 2048
MASKED_LOGIT = -1e30
ATTN_FWD_UNROLL = 4
ATTN_BWD_UNROLL = 4

ADAM_LR = 0.001
ADAM_B1 = 0.9
ADAM_B2 = 0.999
ADAM_EPS = 1e-08
ADAM_WD = 0.01
ADAM_STEP = 10

LANES = 128
SUBLANES_F32 = 8
SUBLANES_BF16 = 16
VMEM_LIMIT_BYTES = 56 * 1024 * 1024

MESH_AXES = ("x", "y", "c")
N_SHARDS = 4
N_DEVICES = 8
ANY = pl.BlockSpec(memory_space=pl.ANY)


def _tile(n, pref, mult):
    best = None
    for t in range(mult, min(n, pref) + 1, mult):
        if n % t == 0:
            best = t
    if best is None:
        raise ValueError(f"no tile for {n} (multiple of {mult}, at most {pref})")
    return best


def _params(*sem):
    return pltpu.CompilerParams(dimension_semantics=sem, vmem_limit_bytes=VMEM_LIMIT_BYTES)


def _rmsnorm_fwd(x, g, name):
    T, D = x.shape
    tm = _tile(T, 512, SUBLANES_BF16)

    def body(x_ref, g_ref, o_ref):
        xf = x_ref[...]
        r = lax.rsqrt(jnp.mean(xf * xf, axis=-1, keepdims=True) + RMS_EPS)
        o_ref[...] = ((xf * r) * g_ref[...]).astype(o_ref.dtype)

    return pl.pallas_call(
        body, name=name, out_shape=jax.ShapeDtypeStruct((T, D), BF16), grid=(T // tm,),
        in_specs=[pl.BlockSpec((tm, D), lambda i: (i, 0)), pl.BlockSpec((1, D), lambda i: (0, 0))],
        out_specs=pl.BlockSpec((tm, D), lambda i: (i, 0)),
        compiler_params=_params("parallel"),
    )(x, g.reshape(1, D))


def _rmsnorm_bwd(x, g, dxn, dres, name):
    T, D = x.shape
    tm = _tile(T, 512, SUBLANES_BF16)

    def body(x_ref, g_ref, dxn_ref, dres_ref, dx_ref, dg_ref):
        @pl.when(pl.program_id(0) == 0)
        def _():
            dg_ref[...] = jnp.zeros_like(dg_ref)

        xf = x_ref[...]
        r = lax.rsqrt(jnp.mean(xf * xf, axis=-1, keepdims=True) + RMS_EPS)
        xhat = xf * r
        dy = dxn_ref[...].astype(F32)
        dg_ref[0:1, :] += jnp.sum(dy * xhat, axis=0, keepdims=True)
        t = dy * g_ref[...]
        dx_ref[...] = dres_ref[...] + r * (t - xhat * jnp.mean(t * xhat, axis=-1, keepdims=True))

    row = pl.BlockSpec((tm, D), lambda i: (i, 0))
    dx, dg = pl.pallas_call(
        body, name=name,
        out_shape=(jax.ShapeDtypeStruct((T, D), F32), jax.ShapeDtypeStruct((SUBLANES_F32, D), F32)),
        grid=(T // tm,),
        in_specs=[row, pl.BlockSpec((1, D), lambda i: (0, 0)), row, row],
        out_specs=(row, pl.BlockSpec((SUBLANES_F32, D), lambda i: (0, 0))),
        compiler_params=_params("arbitrary"),
    )(x, g.reshape(1, D), dxn, dres)
    return dx, dg[0]


def _loss_head(h, g, target, name):
    T, D = h.shape
    tm = _tile(T, 512, SUBLANES_F32)

    def body(h_ref, g_ref, t_ref, dh_ref, acc_ref):
        @pl.when(pl.program_id(0) == 0)
        def _():
            acc_ref[...] = jnp.zeros_like(acc_ref)

        xf = h_ref[...]
        r = lax.rsqrt(jnp.mean(xf * xf, axis=-1, keepdims=True) + RMS_EPS)
        xhat = xf * r
        err = xhat * g_ref[...] - t_ref[...]
        dy = err * (1.0 / D)
        acc_ref[0:1, :] += jnp.sum(dy * xhat, axis=0, keepdims=True)
        acc_ref[1:2, :] += jnp.sum(err * err, axis=0, keepdims=True)
        t = dy * g_ref[...]
        dh_ref[...] = r * (t - xhat * jnp.mean(t * xhat, axis=-1, keepdims=True))

    row = pl.BlockSpec((tm, D), lambda i: (i, 0))
    dh, acc = pl.pallas_call(
        body, name=name,
        out_shape=(jax.ShapeDtypeStruct((T, D), F32), jax.ShapeDtypeStruct((SUBLANES_F32, D), F32)),
        grid=(T // tm,),
        in_specs=[row, pl.BlockSpec((1, D), lambda i: (0, 0)), row],
        out_specs=(row, pl.BlockSpec((SUBLANES_F32, D), lambda i: (0, 0))),
        compiler_params=_params("arbitrary"),
    )(h, g.reshape(1, D), target)
    return acc[1], dh, acc[0]


def _mm_nn(a, b, res, out_dtype, name):
    T, K = a.shape
    N = b.shape[1]
    tm = _tile(T, 512, SUBLANES_BF16)
    tn = _tile(N, 1536, LANES)

    def body(a_ref, b_ref, *rest):
        o_ref = rest[-1]
        acc = jnp.dot(a_ref[...].astype(BF16), b_ref[...], preferred_element_type=F32)
        if res is not None:
            acc = acc + rest[0][...]
        o_ref[...] = acc.astype(o_ref.dtype)

    in_specs = [pl.BlockSpec((tm, K), lambda j, i: (i, 0)), pl.BlockSpec((K, tn), lambda j, i: (0, j))]
    args = [a, b]
    if res is not None:
        in_specs.append(pl.BlockSpec((tm, tn), lambda j, i: (i, j)))
        args.append(res)
    return pl.pallas_call(
        body, name=name, out_shape=jax.ShapeDtypeStruct((T, N), out_dtype), grid=(N // tn, T // tm),
        in_specs=in_specs, out_specs=pl.BlockSpec((tm, tn), lambda j, i: (i, j)),
        compiler_params=_params("parallel", "parallel"),
    )(*args)


def _mm_nt(dy, b, out_dtype, name):
    T, N = dy.shape
    K = b.shape[0]
    tm = _tile(T, 1024, SUBLANES_BF16)
    tk = _tile(K, 1536, LANES)
    tn = _tile(N, 1536, LANES)
    n_steps = N // tn

    def body(dy_ref, b_ref, o_ref, acc_ref):
        n = pl.program_id(2)

        @pl.when(n == 0)
        def _():
            acc_ref[...] = jnp.zeros_like(acc_ref)

        acc_ref[...] += lax.dot_general(dy_ref[...].astype(BF16), b_ref[...], (((1,), (1,)), ((), ())),
                                        preferred_element_type=F32)

        @pl.when(n == n_steps - 1)
        def _():
            o_ref[...] = acc_ref[...].astype(o_ref.dtype)

    return pl.pallas_call(
        body, name=name, out_shape=jax.ShapeDtypeStruct((T, K), out_dtype), grid=(T // tm, K // tk, n_steps),
        in_specs=[pl.BlockSpec((tm, tn), lambda i, k, n: (i, n)), pl.BlockSpec((tk, tn), lambda i, k, n: (k, n))],
        out_specs=pl.BlockSpec((tm, tk), lambda i, k, n: (i, k)),
        scratch_shapes=[pltpu.VMEM((tm, tk), F32)],
        compiler_params=_params("parallel", "parallel", "arbitrary"),
    )(dy, b)


def _mm_tn(a, dy, name):
    T, K = a.shape
    N = dy.shape[1]
    tt = _tile(T, 512, SUBLANES_BF16)
    tk = _tile(K, 1536, LANES)
    tn = _tile(N, 1536, LANES)
    t_steps = T // tt

    def body(a_ref, dy_ref, o_ref, acc_ref):
        t = pl.program_id(2)

        @pl.when(t == 0)
        def _():
            acc_ref[...] = jnp.zeros_like(acc_ref)

        acc_ref[...] += lax.dot_general(a_ref[...].astype(BF16), dy_ref[...].astype(BF16),
                                        (((0,), (0,)), ((), ())), preferred_element_type=F32)

        @pl.when(t == t_steps - 1)
        def _():
            o_ref[...] = acc_ref[...].astype(o_ref.dtype)

    return pl.pallas_call(
        body, name=name, out_shape=jax.ShapeDtypeStruct((K, N), BF16), grid=(K // tk, N // tn, t_steps),
        in_specs=[pl.BlockSpec((tt, tk), lambda k, n, t: (t, k)), pl.BlockSpec((tt, tn), lambda k, n, t: (t, n))],
        out_specs=pl.BlockSpec((tk, tn), lambda k, n, t: (k, n)),
        scratch_shapes=[pltpu.VMEM((tk, tn), F32)],
        compiler_params=_params("parallel", "parallel", "arbitrary"),
    )(a, dy)


def _rows_before(halo, cur, k):
    h = halo.shape[0]
    return pltpu.roll(jnp.concatenate([halo, cur], axis=0), k, 0)[h:]


def _rows_after(cur, halo, k):
    n = cur.shape[0]
    total = n + halo.shape[0]
    return pltpu.roll(jnp.concatenate([cur, halo], axis=0), total - k, 0)[:n]


def _halo_specs(tm, width, n_rows):
    per = tm // SUBLANES_BF16
    last = n_rows // SUBLANES_BF16 - 1
    prev = pl.BlockSpec((SUBLANES_BF16, width), lambda i: (jnp.maximum(i * per - 1, 0), 0))
    nxt = pl.BlockSpec((SUBLANES_BF16, width), lambda i: (jnp.minimum((i + 1) * per, last), 0))
    return prev, nxt


def _gate_a_fwd(p, cw, seq, name):
    T, D3 = p.shape
    D = D3 // 3
    tm = _tile(seq, 512, SUBLANES_BF16)
    cc = _tile(D, 256, LANES)
    prev, _ = _halo_specs(tm, D3, T)

    def body(p_ref, ph_ref, cw_ref, z_ref):
        at_start = (pl.program_id(0) * tm) % seq == 0
        for c0 in range(0, D, cc):
            b = p_ref[:, c0:c0 + cc].astype(F32)
            u = p_ref[:, D + c0:D + c0 + cc].astype(F32) * p_ref[:, 2 * D + c0:2 * D + c0 + cc].astype(F32)
            uh = ph_ref[:, D + c0:D + c0 + cc].astype(F32) * ph_ref[:, 2 * D + c0:2 * D + c0 + cc].astype(F32)
            uh = jnp.where(at_start, 0.0, uh)
            w = cw_ref[:, c0:c0 + cc]
            cv = _rows_before(uh, u, 2) * w[0:1] + _rows_before(uh, u, 1) * w[1:2] + u * w[2:3]
            z_ref[:, c0:c0 + cc] = (b * cv).astype(z_ref.dtype)

    return pl.pallas_call(
        body, name=name, out_shape=jax.ShapeDtypeStruct((T, D), BF16), grid=(T // tm,),
        in_specs=[pl.BlockSpec((tm, D3), lambda i: (i, 0)), prev, pl.BlockSpec((3, D), lambda i: (0, 0))],
        out_specs=pl.BlockSpec((tm, D), lambda i: (i, 0)),
        compiler_params=_params("parallel"),
    )(p, p, cw)


def _gate_a_bwd(p, cw, dz, seq, name):
    T, D3 = p.shape
    D = D3 // 3
    tm = _tile(seq, 512, SUBLANES_BF16)
    cc = _tile(D, 256, LANES)
    p_prev, p_next = _halo_specs(tm, D3, T)
    _, dz_next = _halo_specs(tm, D, T)

    def body(p_ref, pp_ref, pn_ref, dz_ref, dzn_ref, cw_ref, dp_ref, dcw_ref):
        i = pl.program_id(0)

        @pl.when(i == 0)
        def _():
            dcw_ref[...] = jnp.zeros_like(dcw_ref)

        at_start = (i * tm) % seq == 0
        at_end = ((i + 1) * tm) % seq == 0
        for c0 in range(0, D, cc):
            cb, cc_, ch = slice(c0, c0 + cc), slice(D + c0, D + c0 + cc), slice(2 * D + c0, 2 * D + c0 + cc)
            b = p_ref[:, cb].astype(F32)
            c = p_ref[:, cc_].astype(F32)
            hh = p_ref[:, ch].astype(F32)
            u = c * hh
            uh = jnp.where(at_start, 0.0, pp_ref[:, cc_].astype(F32) * pp_ref[:, ch].astype(F32))
            w = cw_ref[:, cb]
            u1 = _rows_before(uh, u, 1)
            u2 = _rows_before(uh, u, 2)
            cv = u2 * w[0:1] + u1 * w[1:2] + u * w[2:3]
            dz_t = dz_ref[:, cb].astype(F32)
            dcv = dz_t * b
            dcvn = jnp.where(at_end, 0.0, dzn_ref[:, cb].astype(F32) * pn_ref[:, cb].astype(F32))
            du = dcv * w[2:3] + _rows_after(dcv, dcvn, 1) * w[1:2] + _rows_after(dcv, dcvn, 2) * w[0:1]
            dp_ref[:, cb] = (dz_t * cv).astype(dp_ref.dtype)
            dp_ref[:, cc_] = (du * hh).astype(dp_ref.dtype)
            dp_ref[:, ch] = (du * c).astype(dp_ref.dtype)
            dcw_ref[0:1, cb] += jnp.sum(dcv * u2, axis=0, keepdims=True)
            dcw_ref[1:2, cb] += jnp.sum(dcv * u1, axis=0, keepdims=True)
            dcw_ref[2:3, cb] += jnp.sum(dcv * u, axis=0, keepdims=True)

    dp, dcw = pl.pallas_call(
        body, name=name,
        out_shape=(jax.ShapeDtypeStruct((T, D3), BF16), jax.ShapeDtypeStruct((SUBLANES_F32, D), F32)),
        grid=(T // tm,),
        in_specs=[pl.BlockSpec((tm, D3), lambda i: (i, 0)), p_prev, p_next,
                  pl.BlockSpec((tm, D), lambda i: (i, 0)), dz_next, pl.BlockSpec((3, D), lambda i: (0, 0))],
        out_specs=(pl.BlockSpec((tm, D3), lambda i: (i, 0)), pl.BlockSpec((SUBLANES_F32, D), lambda i: (0, 0))),
        compiler_params=_params("arbitrary"),
    )(p, p, p, dz, dz, cw)
    return dp, dcw[0:3]


def _ffn_gate_fwd(u, cw, cb, seq, name):
    T, F2 = u.shape
    F = F2 // 2
    tm = _tile(seq, 256, SUBLANES_BF16)
    cc = _tile(F, 256, LANES)
    prev, _ = _halo_specs(tm, F2, T)

    def body(u_ref, uh_ref, cw_ref, cb_ref, a_ref):
        at_start = (pl.program_id(0) * tm) % seq == 0

        def conv(c0):
            cols = slice(c0, c0 + cc)
            cur = u_ref[:, cols].astype(F32)
            halo = jnp.where(at_start, 0.0, uh_ref[:, cols].astype(F32))
            w = cw_ref[:, cols]
            return (_rows_before(halo, cur, 2) * w[0:1] + _rows_before(halo, cur, 1) * w[1:2] + cur * w[2:3]
                    + cb_ref[:, cols])

        for c0 in range(0, F, cc):
            g = conv(c0)
            up = conv(F + c0)
            a_ref[:, c0:c0 + cc] = ((g * jax.nn.sigmoid(g)) * up).astype(a_ref.dtype)

    return pl.pallas_call(
        body, name=name, out_shape=jax.ShapeDtypeStruct((T, F), BF16), grid=(T // tm,),
        in_specs=[pl.BlockSpec((tm, F2), lambda i: (i, 0)), prev,
                  pl.BlockSpec((3, F2), lambda i: (0, 0)), pl.BlockSpec((1, F2), lambda i: (0, 0))],
        out_specs=pl.BlockSpec((tm, F), lambda i: (i, 0)),
        compiler_params=_params("parallel"),
    )(u, u, cw, cb.reshape(1, F2))


def _ffn_gate_bwd(u, cw, cb, da, seq, name):
    T, F2 = u.shape
    F = F2 // 2
    H = SUBLANES_BF16
    tm = _tile(seq, 256, H)
    cc = _tile(F, 256, LANES)
    u_prev, u_next = _halo_specs(tm, F2, T)
    _, da_next = _halo_specs(tm, F, T)

    def body(u_ref, up_ref, un_ref, da_ref, dan_ref, cw_ref, cb_ref, du_ref, acc_ref):
        i = pl.program_id(0)

        @pl.when(i == 0)
        def _():
            acc_ref[...] = jnp.zeros_like(acc_ref)

        at_start = (i * tm) % seq == 0
        at_end = ((i + 1) * tm) % seq == 0

        def conv_ext(cols):
            ext = jnp.concatenate([jnp.where(at_start, 0.0, up_ref[:, cols].astype(F32)),
                                   u_ref[:, cols].astype(F32), un_ref[:, cols].astype(F32)], axis=0)
            w = cw_ref[:, cols]
            e1 = pltpu.roll(ext, 1, 0)
            e2 = pltpu.roll(ext, 2, 0)
            out = (e2 * w[0:1] + e1 * w[1:2] + ext * w[2:3] + cb_ref[:, cols])[H:]
            return out, ext, e1, e2

        def back(d2, cols, ext, e1, e2):
            w = cw_ref[:, cols]
            n = tm + H
            d1n = pltpu.roll(d2, n - 1, 0)[:tm]
            d2n = pltpu.roll(d2, n - 2, 0)[:tm]
            d_t = d2[:tm]
            du_ref[:, cols] = (d_t * w[2:3] + d1n * w[1:2] + d2n * w[0:1]).astype(du_ref.dtype)
            acc_ref[0:1, cols] += jnp.sum(d_t * e2[H:H + tm], axis=0, keepdims=True)
            acc_ref[1:2, cols] += jnp.sum(d_t * e1[H:H + tm], axis=0, keepdims=True)
            acc_ref[2:3, cols] += jnp.sum(d_t * ext[H:H + tm], axis=0, keepdims=True)
            acc_ref[3:4, cols] += jnp.sum(d_t, axis=0, keepdims=True)

        for c0 in range(0, F, cc):
            gc, uc = slice(c0, c0 + cc), slice(F + c0, F + c0 + cc)
            g, g_ext, g_e1, g_e2 = conv_ext(gc)
            up, up_ext, up_e1, up_e2 = conv_ext(uc)
            da_ext = jnp.concatenate([da_ref[:, gc].astype(F32),
                                      jnp.where(at_end, 0.0, dan_ref[:, gc].astype(F32))], axis=0)
            sg = jax.nn.sigmoid(g)
            d_up = da_ext * (g * sg)
            d_g = da_ext * up * (sg * (1.0 + g * (1.0 - sg)))
            back(d_g, gc, g_ext, g_e1, g_e2)
            back(d_up, uc, up_ext, up_e1, up_e2)

    du, acc = pl.pallas_call(
        body, name=name,
        out_shape=(jax.ShapeDtypeStruct((T, F2), BF16), jax.ShapeDtypeStruct((SUBLANES_F32, F2), F32)),
        grid=(T // tm,),
        in_specs=[pl.BlockSpec((tm, F2), lambda i: (i, 0)), u_prev, u_next,
                  pl.BlockSpec((tm, F), lambda i: (i, 0)), da_next,
                  pl.BlockSpec((3, F2), lambda i: (0, 0)), pl.BlockSpec((1, F2), lambda i: (0, 0))],
        out_specs=(pl.BlockSpec((tm, F2), lambda i: (i, 0)), pl.BlockSpec((SUBLANES_F32, F2), lambda i: (0, 0))),
        compiler_params=_params("arbitrary"),
    )(u, u, u, da, da, cw, cb.reshape(1, F2))
    return du, acc[0:3], acc[3]


def _bucket_map():
    P = ATT_BLOCK
    qi = np.arange(P, dtype=np.int64)[:, None]
    kc = np.arange(2 * P, dtype=np.int64)[None, :]
    delta = qi + P - kc
    maps = []
    max_exact = REL_BUCKETS // 2
    for window, dilation in DILATED_BRANCHES:
        band = (delta >= 0) & (delta <= window // dilation)
        n = np.maximum(delta * dilation, 0)
        nf = np.maximum(n, max_exact).astype(np.float32)
        large = max_exact + (np.log(nf / np.float32(max_exact)) / np.float32(math.log(REL_MAX_DISTANCE / max_exact))
                             * np.float32(REL_BUCKETS - max_exact)).astype(np.int32)
        large = np.minimum(large, REL_BUCKETS - 1)
        bucket = np.where(n < max_exact, n, large)
        maps.append(np.where(band, bucket, -1).astype(np.int32))
    return np.stack(maps)


def _bias_tables(rel_bias, bmap, name):
    n_pairs = rel_bias.shape[1] // 2
    nbr, P, P2 = bmap.shape

    def body(rb_ref, bm_ref, o_ref):
        pair = pl.program_id(0)
        in_seq = lax.broadcasted_iota(jnp.int32, (P, P2), 1) >= P
        for br in range(nbr):
            bm = bm_ref[br]
            for hh in range(2):
                acc = jnp.full((P, P2), MASKED_LOGIT, F32)
                for b in range(REL_BUCKETS):
                    acc = jnp.where(bm == b, rb_ref[b, 2 * pair + hh], acc)
                o_ref[br, 0, 0, hh * P:(hh + 1) * P, :] = acc
                o_ref[br, 0, 1, hh * P:(hh + 1) * P, :] = jnp.where(in_seq, acc, MASKED_LOGIT)

    return pl.pallas_call(
        body, name=name, out_shape=jax.ShapeDtypeStruct((nbr, n_pairs, 2, 2 * P, P2), F32), grid=(n_pairs,),
        in_specs=[pl.BlockSpec(memory_space=pltpu.SMEM), pl.BlockSpec((nbr, P, P2), lambda h: (0, 0, 0))],
        out_specs=pl.BlockSpec((nbr, 1, 2, 2 * P, P2), lambda h: (0, h, 0, 0, 0)),
        compiler_params=_params("parallel"),
    )(rel_bias, bmap)


def _bias_grad(dbias, bmap, name):
    nbr, n_pairs, _, P2 = dbias.shape
    P = P2 // 2

    def body(db_ref, bm_ref, o_ref):
        lane = lax.broadcasted_iota(jnp.int32, (1, LANES), 1)
        for hh in range(2):
            row = jnp.zeros((1, LANES), F32)
            for br in range(nbr):
                bm = bm_ref[br]
                d = db_ref[br, 0, hh * P:(hh + 1) * P, :]
                for b in range(REL_BUCKETS):
                    hit = jnp.sum(jnp.where(bm == b, d, 0.0), axis=1, keepdims=True)
                    row = row + jnp.where(lane == b, jnp.sum(hit, axis=0, keepdims=True), 0.0)
            o_ref[hh] = row

    return pl.pallas_call(
        body, name=name, out_shape=jax.ShapeDtypeStruct((2 * n_pairs, 1, LANES), F32), grid=(n_pairs,),
        in_specs=[pl.BlockSpec((nbr, 1, P2, P2), lambda h: (0, h, 0, 0)), pl.BlockSpec((nbr, P, P2), lambda h: (0, 0, 0))],
        out_specs=pl.BlockSpec((2, 1, LANES), lambda h: (h, 0, 0)),
        compiler_params=_params("parallel"),
    )(dbias, bmap)[:, 0, :]


def _rows(start, dilation):
    if dilation == 1:
        return pl.ds(pl.multiple_of(start, ATT_BLOCK), ATT_BLOCK)
    return pl.ds(start, ATT_BLOCK, stride=dilation)


def _for_each_block(seq, unroll, fn):
    P = ATT_BLOCK
    for br, (_, d) in enumerate(DILATED_BRANCHES):
        nb = seq // d // P
        total = nb * d
        u = unroll if total % unroll == 0 else 1

        def some(i, carry, br=br, d=d, nb=nb, u=u):
            blocks = []
            for k in range(u):
                idx = i * u + k
                r, j = idx // nb, idx % nb
                blocks.append((r + j * (d * P), r + jnp.maximum(j - 1, 0) * (d * P), jnp.where(j == 0, 1, 0)))
            fn(br, d, blocks)
            return carry

        lax.fori_loop(0, total // u, some, 0)


def _stack_heads(x, head0):
    return jnp.concatenate([jnp.where(head0, x, 0.0), jnp.where(head0, 0.0, x)], axis=0).astype(BF16)


def _window(ref, start, prev, dilation):
    return jnp.concatenate([ref[_rows(prev, dilation), :], ref[_rows(start, dilation), :]], axis=0).astype(BF16)


def _attn_fwd(q, kv, bias, name):
    B, S, D = q.shape
    P = ATT_BLOCK
    n_pairs = D // LANES
    nbr = len(DILATED_BRANCHES)
    scale = HEAD_DIM ** -0.5

    def body(q_ref, k_ref, v_ref, bias_ref, o_ref, lse_ref, *stats):
        m_s, l_s, acc_s = stats[0:nbr], stats[nbr:2 * nbr], stats[2 * nbr:3 * nbr]
        head0 = lax.broadcasted_iota(jnp.int32, (P, LANES), 1) < HEAD_DIM

        def block(br, d, blocks):
            s = [lax.dot_general(_stack_heads(q_ref[_rows(start, d), :] * scale, head0),
                                 _window(k_ref, start, prev, d), (((1,), (1,)), ((), ())),
                                 preferred_element_type=F32) + bias_ref[br, 0, first]
                 for start, prev, first in blocks]
            m = [jnp.max(x, axis=-1, keepdims=True) for x in s]
            p = [jnp.exp(x - y) for x, y in zip(s, m)]
            l = [jnp.sum(x, axis=-1, keepdims=True) for x in p]
            pv = [jnp.dot(x.astype(BF16), _window(v_ref, start, prev, d), preferred_element_type=F32)
                  for x, (start, prev, _) in zip(p, blocks)]
            for k, (start, _, _) in enumerate(blocks):
                rows = _rows(start, d)
                m_s[br][rows, :] = jnp.where(head0, m[k][:P], m[k][P:])
                l_s[br][rows, :] = jnp.where(head0, l[k][:P], l[k][P:])
                acc_s[br][rows, :] = jnp.where(head0, pv[k][:P], pv[k][P:])

        _for_each_block(S, ATTN_FWD_UNROLL, block)

        chunk = _tile(S, 256, SUBLANES_F32)

        def merge(i, carry):
            rows = pl.ds(pl.multiple_of(i * chunk, chunk), chunk)
            ms = [m_s[br][rows, :] for br in range(nbr)]
            m = functools.reduce(jnp.maximum, ms)
            l = jnp.zeros((chunk, LANES), F32)
            acc = jnp.zeros((chunk, LANES), F32)
            for br in range(nbr):
                w = jnp.exp(ms[br] - m)
                l = l + w * l_s[br][rows, :]
                acc = acc + w * acc_s[br][rows, :]
            o_ref[rows, :] = acc / l
            lse_ref[rows, :] = m + jnp.log(l)
            return carry

        lax.fori_loop(0, S // chunk, merge, 0)

    slab = lambda col0: pl.BlockSpec((None, S, LANES), lambda b, h: (b, 0, col0 + h))
    return pl.pallas_call(
        body, name=name,
        out_shape=(jax.ShapeDtypeStruct((B, S, D), F32), jax.ShapeDtypeStruct((B, S, D), F32)),
        grid=(B, n_pairs),
        in_specs=[slab(0), slab(0), slab(n_pairs),
                  pl.BlockSpec((nbr, 1, 2, 2 * P, 2 * P), lambda b, h: (0, h, 0, 0, 0))],
        out_specs=(slab(0), slab(0)),
        scratch_shapes=[pltpu.VMEM((S, LANES), F32)] * (3 * nbr),
        compiler_params=_params("parallel", "parallel"),
    )(q, kv, kv, bias)


def _attn_bwd(q, kv, o, lse, do, bias, name):
    B, S, D = q.shape
    P = ATT_BLOCK
    n_pairs = D // LANES
    nbr = len(DILATED_BRANCHES)
    scale = HEAD_DIM ** -0.5

    def body(q_ref, k_ref, v_ref, o_ref, lse_ref, do_ref, bias_ref, dq_ref, dk_ref, dv_ref, dbias_ref, delta_s):
        head0 = lax.broadcasted_iota(jnp.int32, (P, LANES), 1) < HEAD_DIM

        @pl.when(pl.program_id(1) == 0)
        def _():
            dbias_ref[...] = jnp.zeros_like(dbias_ref)

        chunk = _tile(S, 512, SUBLANES_F32)

        def prepare(i, carry):
            rows = pl.ds(pl.multiple_of(i * chunk, chunk), chunk)
            x = do_ref[rows, :] * o_ref[rows, :]
            h0 = lax.broadcasted_iota(jnp.int32, (chunk, LANES), 1) < HEAD_DIM
            d0 = jnp.sum(jnp.where(h0, x, 0.0), axis=-1, keepdims=True)
            d1 = jnp.sum(jnp.where(h0, 0.0, x), axis=-1, keepdims=True)
            delta_s[rows, :] = jnp.where(h0, d0, d1)
            zero = jnp.zeros((chunk, LANES), F32)
            dq_ref[rows, :] = zero
            dk_ref[rows, :] = zero
            dv_ref[rows, :] = zero
            return carry

        lax.fori_loop(0, S // chunk, prepare, 0)

        def per_head(x):
            return jnp.concatenate([x[:, 0:1], x[:, HEAD_DIM:HEAD_DIM + 1]], axis=0)

        nt = (((1,), (1,)), ((), ()))
        tn = (((0,), (0,)), ((), ()))

        def block(br, d, blocks):
            q2 = [_stack_heads(q_ref[_rows(start, d), :] * scale, head0) for start, _, _ in blocks]
            do2 = [_stack_heads(do_ref[_rows(start, d), :], head0) for start, _, _ in blocks]
            kb = [_window(k_ref, start, prev, d) for start, prev, _ in blocks]
            vb = [_window(v_ref, start, prev, d) for start, prev, _ in blocks]
            s = [lax.dot_general(a, b, nt, preferred_element_type=F32) + bias_ref[br, 0, first]
                 for a, b, (_, _, first) in zip(q2, kb, blocks)]
            dp = [lax.dot_general(a, b, nt, preferred_element_type=F32) for a, b in zip(do2, vb)]
            p = [jnp.exp(x - per_head(lse_ref[_rows(start, d), :])) for x, (start, _, _) in zip(s, blocks)]
            ds = [x * (y - per_head(delta_s[_rows(start, d), :])) for x, y, (start, _, _) in zip(p, dp, blocks)]
            for x in ds:
                dbias_ref[br, 0] += x
            ds16 = [x.astype(BF16) for x in ds]
            dq2 = [jnp.dot(a, b, preferred_element_type=F32) for a, b in zip(ds16, kb)]
            dk = [lax.dot_general(a, b, tn, preferred_element_type=F32) for a, b in zip(ds16, q2)]
            dv = [lax.dot_general(a.astype(BF16), b, tn, preferred_element_type=F32) for a, b in zip(p, do2)]
            for k, (start, prev, _) in enumerate(blocks):
                rows, prows = _rows(start, d), _rows(prev, d)
                dq_ref[rows, :] += jnp.where(head0, dq2[k][:P], dq2[k][P:]) * scale
                dk_ref[prows, :] += dk[k][:P]
                dk_ref[rows, :] += dk[k][P:]
                dv_ref[prows, :] += dv[k][:P]
                dv_ref[rows, :] += dv[k][P:]

        _for_each_block(S, ATTN_BWD_UNROLL, block)

    slab = lambda col0: pl.BlockSpec((None, S, LANES), lambda h, b: (b, 0, col0 + h))
    tab = pl.BlockSpec((nbr, 1, 2, 2 * P, 2 * P), lambda h, b: (0, h, 0, 0, 0))
    dtab = pl.BlockSpec((nbr, 1, 2 * P, 2 * P), lambda h, b: (0, h, 0, 0))
    shp = jax.ShapeDtypeStruct((B, S, D), F32)
    return pl.pallas_call(
        body, name=name,
        out_shape=(shp, shp, shp, jax.ShapeDtypeStruct((nbr, n_pairs, 2 * P, 2 * P), F32)),
        grid=(n_pairs, B),
        in_specs=[slab(0), slab(0), slab(n_pairs), slab(0), slab(0), slab(0), tab],
        out_specs=(slab(0), slab(0), slab(0), dtab),
        scratch_shapes=[pltpu.VMEM((S, LANES), F32)],
        compiler_params=_params("parallel", "arbitrary"),
    )(q, kv, kv, o, lse, do, bias)


def _adamw(w, g, m, v, name):
    R, C = w.shape
    tr = _tile(R, 256, SUBLANES_F32) if R % SUBLANES_F32 == 0 else R
    tc = _tile(C, 2048, LANES) if C % LANES == 0 else C

    def body(w_ref, g_ref, m_ref, v_ref, d_ref, nm_ref, nv_ref):
        g_ = g_ref[...]
        m2 = ADAM_B1 * m_ref[...] + (1.0 - ADAM_B1) * g_
        v2 = ADAM_B2 * v_ref[...] + (1.0 - ADAM_B2) * (g_ * g_)
        m_hat = m2 / (1.0 - ADAM_B1 ** ADAM_STEP)
        v_hat = v2 / (1.0 - ADAM_B2 ** ADAM_STEP)
        d_ref[...] = -ADAM_LR * (m_hat / (jnp.sqrt(v_hat) + ADAM_EPS) + ADAM_WD * w_ref[...])
        nm_ref[...] = m2
        nv_ref[...] = v2

    blk = pl.BlockSpec((tr, tc), lambda i, j: (i, j))
    shp = jax.ShapeDtypeStruct((R, C), F32)
    return pl.pallas_call(
        body, name=name, out_shape=(shp, shp, shp), grid=(R // tr, C // tc),
        in_specs=[blk] * 4, out_specs=(blk,) * 3, compiler_params=_params("parallel", "parallel"),
    )(w, g, m, v)


def _sum_slots(slots, own, name):
    n, R, C = slots.shape
    tr = _tile(R, 256, SUBLANES_BF16) if R % SUBLANES_BF16 == 0 else R
    tc = _tile(C, 2048, LANES) if C % LANES == 0 else C

    def body(s_ref, *rest):
        o_ref = rest[-1]
        acc = s_ref[0].astype(F32)
        if own is not None:
            acc = rest[0][...].astype(F32) + acc
        for k in range(1, n):
            acc = acc + s_ref[k].astype(F32)
        o_ref[...] = acc

    in_specs = [pl.BlockSpec((n, tr, tc), lambda i, j: (0, i, j))]
    args = [slots]
    if own is not None:
        in_specs.append(pl.BlockSpec((tr, tc), lambda i, j: (i, j)))
        args.append(own)
    return pl.pallas_call(
        body, name=name, out_shape=jax.ShapeDtypeStruct((R, C), F32), grid=(R // tr, C // tc),
        in_specs=in_specs, out_specs=pl.BlockSpec((tr, tc), lambda i, j: (i, j)),
        compiler_params=_params("parallel", "parallel"),
    )(*args)


def _my_place():
    return lax.axis_index("x"), lax.axis_index("y"), lax.axis_index("c")


def _other_chips(x, y):
    return [(1 - x, y), (x, 1 - y), (1 - x, 1 - y)]


def _piece(ref, blk, axis, shard, half):
    h0 = blk[0] // 2
    idx = []
    for dim, n in enumerate(blk):
        if dim == 0:
            start = half * h0 + (shard * n if axis == 0 else 0)
            idx.append(pl.ds(start, h0))
        elif dim == axis:
            idx.append(pl.ds(shard * n, n))
        else:
            idx.append(slice(None))
    return ref.at[tuple(idx)]


def _half_of(ref, blk, half):
    return ref.at[pl.ds(half * (blk[0] // 2), blk[0] // 2)]


def _piece_start(blk, axis, shard, half):
    return tuple(half * (blk[0] // 2) + (shard * n if axis == 0 else 0) if dim == 0 else (shard * n if dim == axis else 0)
                 for dim, n in enumerate(blk))


def _gather_weights(shards, axes, small):
    n = len(shards)
    blks = [s.shape for s in shards]
    fulls = [tuple(d * N_SHARDS if i == ax else d for i, d in enumerate(b)) for b, ax in zip(blks, axes)]
    me2 = 2 * lax.axis_index("x") + lax.axis_index("y")
    bases = [lax.dynamic_update_slice(jnp.zeros(f, s.dtype), s, tuple(me2 * b[i] if i == ax else 0 for i in range(len(b))))
             for f, s, b, ax in zip(fulls, shards, blks, axes)]

    def stage1(*refs):
        ins, small_in = refs[:n], refs[2 * n]
        outs, small_out = refs[2 * n + 1:3 * n + 1], refs[3 * n + 1]
        send_sems, recv_sems, local_sems = refs[3 * n + 2:]
        x, y, c = _my_place()
        me = 2 * x + y
        chips = _other_chips(x, y)
        local = [pltpu.make_async_copy(small_in, small_out.at[me], local_sems.at[0])]
        for cp in local:
            cp.start()
        sends, recvs = [], []
        for a in range(n + 1):
            for k, (px, py) in enumerate(chips):
                peer = 2 * px + py
                if a < n:
                    src = _half_of(ins[a], blks[a], c)
                    dst = _piece(outs[a], blks[a], axes[a], me, c)
                    land = _piece(outs[a], blks[a], axes[a], peer, c)
                else:
                    src, dst, land = small_in, small_out.at[me], small_out.at[peer]
                sends.append(pltpu.make_async_remote_copy(
                    src_ref=src, dst_ref=dst, send_sem=send_sems.at[a, k], recv_sem=recv_sems.at[a, k],
                    device_id=(px, py, c), device_id_type=pl.DeviceIdType.MESH))
                recvs.append(pltpu.make_async_remote_copy(
                    src_ref=src, dst_ref=land, send_sem=send_sems.at[a, k], recv_sem=recv_sems.at[a, k],
                    device_id=(px, py, c), device_id_type=pl.DeviceIdType.MESH))
        for cp in sends:
            cp.start()
        for cp in recvs:
            cp.wait_recv()
        for cp in sends:
            cp.wait_send()
        for cp in local:
            cp.wait()

    out_shape = [jax.ShapeDtypeStruct(f, s.dtype) for f, s in zip(fulls, shards)]
    out_shape.append(jax.ShapeDtypeStruct((N_SHARDS,) + small.shape, small.dtype))
    res = pl.pallas_call(
        stage1, name="gather_weights_ici", out_shape=out_shape,
        in_specs=[ANY] * (2 * n + 1), out_specs=[ANY] * (n + 1),
        input_output_aliases={n + a: a for a in range(n)},
        scratch_shapes=[pltpu.SemaphoreType.DMA((n + 1, 3)), pltpu.SemaphoreType.DMA((n + 1, 3)),
                        pltpu.SemaphoreType.DMA((1,))],
    )(*shards, *bases, small)
    partial, small_full = res[:n], res[n]

    def stage2(*refs):
        outs = refs[n:2 * n]
        send_sems, recv_sems = refs[2 * n:]
        x, y, c = _my_place()
        sends, recvs = [], []
        for a in range(n):
            for k, (px, py) in enumerate(_other_chips(x, y)):
                peer = 2 * px + py
                mine = _piece(outs[a], blks[a], axes[a], peer, c)
                theirs = _piece(outs[a], blks[a], axes[a], peer, 1 - c)
                sends.append(pltpu.make_async_remote_copy(
                    src_ref=mine, dst_ref=mine, send_sem=send_sems.at[a, k], recv_sem=recv_sems.at[a, k],
                    device_id=(x, y, 1 - c), device_id_type=pl.DeviceIdType.MESH))
                recvs.append(pltpu.make_async_remote_copy(
                    src_ref=mine, dst_ref=theirs, send_sem=send_sems.at[a, k], recv_sem=recv_sems.at[a, k],
                    device_id=(x, y, 1 - c), device_id_type=pl.DeviceIdType.MESH))
        for cp in sends:
            cp.start()
        for cp in recvs:
            cp.wait_recv()
        for cp in sends:
            cp.wait_send()

    full = pl.pallas_call(
        stage2, name="gather_weights_d2d", out_shape=out_shape[:n],
        in_specs=[ANY] * n, out_specs=[ANY] * n, input_output_aliases={a: a for a in range(n)},
        scratch_shapes=[pltpu.SemaphoreType.DMA((n, 3)), pltpu.SemaphoreType.DMA((n, 3))],
    )(*partial)
    return list(full), small_full


def _scatter_grads(grads, blks, axes, small):
    n = len(grads)
    halves = [(b[0] // 2,) + tuple(b[1:]) for b in blks]

    def body(*refs):
        ins, small_in = refs[:n], refs[n]
        outs, small_out = refs[n + 1:2 * n + 1], refs[2 * n + 1]
        send_sems, recv_sems, local_sems = refs[2 * n + 2:]
        x, y, c = _my_place()
        me = 4 * x + 2 * y + c
        local = [pltpu.make_async_copy(small_in, small_out.at[me], local_sems.at[0])]
        for cp in local:
            cp.start()
        sends, recvs = [], []
        for rel in range(1, N_DEVICES):
            px = x ^ ((rel >> 2) & 1)
            py = y ^ ((rel >> 1) & 1)
            pc = c ^ (rel & 1)
            peer = 4 * px + 2 * py + pc
            for a in range(n + 1):
                if a < n:
                    src = _piece(ins[a], blks[a], axes[a], 2 * px + py, pc)
                    dst = land = outs[a].at[rel - 1]
                else:
                    src, dst, land = small_in, small_out.at[me], small_out.at[peer]
                sends.append(pltpu.make_async_remote_copy(
                    src_ref=src, dst_ref=dst, send_sem=send_sems.at[a, rel - 1], recv_sem=recv_sems.at[a, rel - 1],
                    device_id=(px, py, pc), device_id_type=pl.DeviceIdType.MESH))
                recvs.append(pltpu.make_async_remote_copy(
                    src_ref=src, dst_ref=land, send_sem=send_sems.at[a, rel - 1], recv_sem=recv_sems.at[a, rel - 1],
                    device_id=(px, py, pc), device_id_type=pl.DeviceIdType.MESH))
        for cp in sends:
            cp.start()
        for cp in recvs:
            cp.wait_recv()
        for cp in sends:
            cp.wait_send()
        for cp in local:
            cp.wait()

    out_shape = [jax.ShapeDtypeStruct((N_DEVICES - 1,) + h, g.dtype) for h, g in zip(halves, grads)]
    out_shape.append(jax.ShapeDtypeStruct((N_DEVICES,) + small.shape, small.dtype))
    res = pl.pallas_call(
        body, name="scatter_grads", out_shape=out_shape,
        in_specs=[ANY] * (n + 1), out_specs=[ANY] * (n + 1),
        scratch_shapes=[pltpu.SemaphoreType.DMA((n + 1, N_DEVICES - 1)), pltpu.SemaphoreType.DMA((n + 1, N_DEVICES - 1)),
                        pltpu.SemaphoreType.DMA((1,))],
    )(*grads, small)
    return list(res[:n]), res[n]


def _swap_halves(halves, blks):
    n = len(halves)
    c_ = lax.axis_index("c")
    bases = [lax.dynamic_update_slice(jnp.zeros(b, F32), h, (c_ * (b[0] // 2),) + (0,) * (len(b) - 1))
             for h, b in zip(halves, blks)]

    def body(*refs):
        ins, outs = refs[:n], refs[2 * n:3 * n]
        send_sems, recv_sems = refs[3 * n:]
        x, y, c = _my_place()
        sends, recvs = [], []
        for a in range(n):
            sends.append(pltpu.make_async_remote_copy(
                src_ref=ins[a], dst_ref=_half_of(outs[a], blks[a], c), send_sem=send_sems.at[a],
                recv_sem=recv_sems.at[a], device_id=(x, y, 1 - c), device_id_type=pl.DeviceIdType.MESH))
            recvs.append(pltpu.make_async_remote_copy(
                src_ref=ins[a], dst_ref=_half_of(outs[a], blks[a], 1 - c), send_sem=send_sems.at[a],
                recv_sem=recv_sems.at[a], device_id=(x, y, 1 - c), device_id_type=pl.DeviceIdType.MESH))
        for cp in sends:
            cp.start()
        for cp in recvs:
            cp.wait_recv()
        for cp in sends:
            cp.wait_send()

    return list(pl.pallas_call(
        body, name="swap_grad_halves", out_shape=[jax.ShapeDtypeStruct(b, F32) for b in blks],
        in_specs=[ANY] * (2 * n), out_specs=[ANY] * n, input_output_aliases={n + a: a for a in range(n)},
        scratch_shapes=[pltpu.SemaphoreType.DMA((n,)), pltpu.SemaphoreType.DMA((n,))],
    )(*halves, *bases))


def _pack(arrays):
    flat = jnp.concatenate([a.reshape(-1).astype(F32) for a in arrays])
    pad = (-flat.shape[0]) % (SUBLANES_F32 * LANES)
    return jnp.pad(flat, (0, pad)).reshape(-1, LANES)


def _unpack(packed, shapes):
    flat = packed.reshape(-1)
    out, off = [], 0
    for s in shapes:
        n = int(np.prod(s))
        out.append(flat[off:off + n].reshape(s))
        off += n
    return out


def _ffn_fwd(h, norm, w_up, cw, cb, w_down, seq, tag):
    xn = _rmsnorm_fwd(h, norm, f"ffn{tag}_norm")
    u = _mm_nn(xn, w_up, None, BF16, f"ffn{tag}_up")
    a = _ffn_gate_fwd(u, cw, cb, seq, f"ffn{tag}_gate")
    out = _mm_nn(a, w_down, h, F32, f"ffn{tag}_down")
    return out, (xn, u, a)


def _ffn_bwd(dout, h, norm, w_up, cw, cb, w_down, saved, seq, tag):
    xn, u, a = saved
    da = _mm_nt(dout, w_down, BF16, f"ffn{tag}_down_dx")
    g_down = _mm_tn(a, dout, f"ffn{tag}_down_dw")
    du, g_cw, g_cb = _ffn_gate_bwd(u, cw, cb, da, seq, f"ffn{tag}_gate_bwd")
    dxn = _mm_nt(du, w_up, BF16, f"ffn{tag}_up_dx")
    g_up = _mm_tn(xn, du, f"ffn{tag}_up_dw")
    dh, g_norm = _rmsnorm_bwd(h, norm, dxn, dout, f"ffn{tag}_norm_bwd")
    return dh, g_up, g_down, g_cw, g_cb, g_norm


def _local_step(x, target, W, small):
    B, S, D = x.shape
    T = B * S
    x2 = x.reshape(T, D)
    tgt = target.reshape(T, D)
    bmap = jnp.asarray(_bucket_map())

    xn0 = _rmsnorm_fwd(x2, small["a_norm"][0], "a_norm")
    p = _mm_nn(xn0, W["w_in"], None, BF16, "a_in")
    z = _gate_a_fwd(p, small["a_conv"][0], S, "a_gate")
    h1 = _mm_nn(z, W["w_out"], x2, F32, "a_out")
    h2, ffn0 = _ffn_fwd(h1, small["ffn_norm"][0], W["w_up0"], small["ffn_conv"][0], small["ffn_conv_b"][0],
                        W["w_down0"], S, 0)
    kvn = _rmsnorm_fwd(h2, small["kv_norm"], "kv_norm")
    kv = _mm_nn(kvn, W["w_kv"], None, F32, "kv_proj")
    xn3 = _rmsnorm_fwd(h2, small["b_norm"][0], "b_norm")
    q = _mm_nn(xn3, W["w_q"], None, F32, "q_proj")
    bias = _bias_tables(small["rel_bias"], bmap, "rel_bias_tables")
    q3, kv3 = q.reshape(B, S, D), kv.reshape(B, S, 2 * D)
    o3, lse3 = _attn_fwd(q3, kv3, bias, "attn_fwd")
    o = o3.reshape(T, D)
    h3 = _mm_nn(o, W["w_o"], h2, F32, "o_proj")
    h4, ffn1 = _ffn_fwd(h3, small["ffn_norm"][1], W["w_up1"], small["ffn_conv"][1], small["ffn_conv_b"][1],
                        W["w_down1"], S, 1)
    sq_err, dh4, g_final = _loss_head(h4, small["final_norm"], tgt, "loss_head")
    loss = 0.5 * jnp.sum(sq_err) / D

    dh3, g_up1, g_down1, g_cw1, g_cb1, g_fn1 = _ffn_bwd(
        dh4, h3, small["ffn_norm"][1], W["w_up1"], small["ffn_conv"][1], small["ffn_conv_b"][1], W["w_down1"],
        ffn1, S, 1)
    do = _mm_nt(dh3, W["w_o"], F32, "o_proj_dx")
    g_o = _mm_tn(o, dh3, "o_proj_dw")
    dq3, dk3, dv3, dbias = _attn_bwd(q3, kv3, o3, lse3, do.reshape(B, S, D), bias, "attn_bwd")
    g_rel = _bias_grad(dbias, bmap, "rel_bias_grad")[:, :REL_BUCKETS].T
    dq = dq3.reshape(T, D)
    dkv = jnp.concatenate([dk3, dv3], axis=-1).reshape(T, 2 * D)
    dxn3 = _mm_nt(dq, W["w_q"], BF16, "q_proj_dx")
    g_q = _mm_tn(xn3, dq, "q_proj_dw")
    dh2, g_bn = _rmsnorm_bwd(h2, small["b_norm"][0], dxn3, dh3, "b_norm_bwd")
    dkvn = _mm_nt(dkv, W["w_kv"], BF16, "kv_proj_dx")
    g_kv = _mm_tn(kvn, dkv, "kv_proj_dw")
    dh2, g_kvn = _rmsnorm_bwd(h2, small["kv_norm"], dkvn, dh2, "kv_norm_bwd")
    dh1, g_up0, g_down0, g_cw0, g_cb0, g_fn0 = _ffn_bwd(
        dh2, h1, small["ffn_norm"][0], W["w_up0"], small["ffn_conv"][0], small["ffn_conv_b"][0], W["w_down0"],
        ffn0, S, 0)
    dz = _mm_nt(dh1, W["w_out"], BF16, "a_out_dx")
    g_out = _mm_tn(z, dh1, "a_out_dw")
    dp, g_aconv = _gate_a_bwd(p, small["a_conv"][0], dz, S, "a_gate_bwd")
    dxn0 = _mm_nt(dp, W["w_in"], BF16, "a_in_dx")
    g_in = _mm_tn(xn0, dp, "a_in_dw")
    dx, g_an = _rmsnorm_bwd(x2, small["a_norm"][0], dxn0, dh1, "a_norm_bwd")

    big = {"w_in": g_in, "w_out": g_out, "w_kv": g_kv, "w_q": g_q, "w_o": g_o,
           "w_up0": g_up0, "w_up1": g_up1, "w_down0": g_down0, "w_down1": g_down1}
    small_g = {"a_norm": g_an[None], "a_conv": g_aconv[None], "kv_norm": g_kvn, "b_norm": g_bn[None],
               "rel_bias": g_rel, "ffn_norm": jnp.stack([g_fn0, g_fn1]), "ffn_conv": jnp.stack([g_cw0, g_cw1]),
               "ffn_conv_b": jnp.stack([g_cb0, g_cb1]), "final_norm": g_final}
    return loss, dx.reshape(B, S, D), big, small_g


BIG = ("w_in", "w_out", "w_kv", "w_q", "w_o", "w_up0", "w_up1", "w_down0", "w_down1")
SMALL = ("a_norm", "a_conv", "kv_norm", "b_norm", "rel_bias", "ffn_norm", "ffn_conv", "ffn_conv_b", "final_norm")
SMALL_SHARDED = ("a_norm", "a_conv", "ffn_conv")
WEIGHT_ORDER = ("a_norm", "a_w_in", "a_conv", "a_w_out", "kv_norm", "w_kv", "b_norm", "b_w_q", "b_w_o", "rel_bias",
                "ffn_norm", "ffn_w_up", "ffn_conv", "ffn_conv_b", "ffn_w_down", "final_norm")
BIG_OF = {"a_w_in": "w_in", "a_w_out": "w_out", "w_kv": "w_kv", "b_w_q": "w_q", "b_w_o": "w_o", "ffn_w_up": "w_up",
          "ffn_w_down": "w_down"}


def _as2d(a):
    return a.reshape(-1, a.shape[-1])


def kernel(x, a_norm, a_w_in, a_conv, a_w_out, kv_norm, w_kv, b_norm, b_w_q, b_w_o, rel_bias, ffn_norm, ffn_w_up, ffn_conv, ffn_conv_b, ffn_w_down, final_norm, loss_target, m_a_norm, m_a_w_in, m_a_conv, m_a_w_out, m_kv_norm, m_w_kv, m_b_norm, m_b_w_q, m_b_w_o, m_rel_bias, m_ffn_norm, m_ffn_w_up, m_ffn_conv, m_ffn_conv_b, m_ffn_w_down, m_final_norm, v_a_norm, v_a_w_in, v_a_conv, v_a_w_out, v_kv_norm, v_w_kv, v_b_norm, v_b_w_q, v_b_w_o, v_rel_bias, v_ffn_norm, v_ffn_w_up, v_ffn_conv, v_ffn_conv_b, v_ffn_w_down, v_final_norm):
    given = dict(a_norm=a_norm, a_w_in=a_w_in, a_conv=a_conv, a_w_out=a_w_out, kv_norm=kv_norm, w_kv=w_kv, b_norm=b_norm,
                 b_w_q=b_w_q, b_w_o=b_w_o, rel_bias=rel_bias, ffn_norm=ffn_norm, ffn_w_up=ffn_w_up, ffn_conv=ffn_conv,
                 ffn_conv_b=ffn_conv_b, ffn_w_down=ffn_w_down, final_norm=final_norm)
    mom_m = dict(a_norm=m_a_norm, a_w_in=m_a_w_in, a_conv=m_a_conv, a_w_out=m_a_w_out, kv_norm=m_kv_norm, w_kv=m_w_kv,
                 b_norm=m_b_norm, b_w_q=m_b_w_q, b_w_o=m_b_w_o, rel_bias=m_rel_bias, ffn_norm=m_ffn_norm,
                 ffn_w_up=m_ffn_w_up, ffn_conv=m_ffn_conv, ffn_conv_b=m_ffn_conv_b, ffn_w_down=m_ffn_w_down,
                 final_norm=m_final_norm)
    mom_v = dict(a_norm=v_a_norm, a_w_in=v_a_w_in, a_conv=v_a_conv, a_w_out=v_a_w_out, kv_norm=v_kv_norm, w_kv=v_w_kv,
                 b_norm=v_b_norm, b_w_q=v_b_w_q, b_w_o=v_b_w_o, rel_bias=v_rel_bias, ffn_norm=v_ffn_norm,
                 ffn_w_up=v_ffn_w_up, ffn_conv=v_ffn_conv, ffn_conv_b=v_ffn_conv_b, ffn_w_down=v_ffn_w_down,
                 final_norm=v_final_norm)

    shard = {"w_in": (a_w_in[0], 1), "w_out": (a_w_out[0], 0), "w_kv": (w_kv, 1), "w_q": (b_w_q[0], 0),
             "w_o": (b_w_o[0], 0), "w_up0": (ffn_w_up[0], 1), "w_up1": (ffn_w_up[1], 1),
             "w_down0": (ffn_w_down[0], 0), "w_down1": (ffn_w_down[1], 0)}
    blks = [shard[k][0].shape for k in BIG]
    axes = [shard[k][1] for k in BIG]

    small_sharded = [given[k] for k in SMALL_SHARDED]
    packed = _pack(small_sharded)
    fulls, packed_all = _gather_weights([shard[k][0].astype(BF16) for k in BIG], axes, packed)
    W = dict(zip(BIG, fulls))
    small = {k: given[k] for k in SMALL}
    per_shard = [_unpack(packed_all[j], [a.shape for a in small_sharded]) for j in range(N_SHARDS)]
    for i, k in enumerate(SMALL_SHARDED):
        small[k] = jnp.concatenate([per_shard[j][i] for j in range(N_SHARDS)], axis=-1)

    loss, grad_x, big_g, small_g = _local_step(x, loss_target, W, small)
    loss = lax.psum(loss, MESH_AXES)

    small_shapes = [small_g[k].shape for k in SMALL]
    slots, small_slots = _scatter_grads([big_g[k] for k in BIG], blks, axes, _pack([small_g[k] for k in SMALL]))
    me2, c_ = 2 * lax.axis_index("x") + lax.axis_index("y"), lax.axis_index("c")
    halves = []
    for k, s, b, ax in zip(BIG, slots, blks, axes):
        own = lax.dynamic_slice(big_g[k], _piece_start(b, ax, me2, c_), (b[0] // 2, b[1]))
        halves.append(_sum_slots(s, own, f"sum_{k}"))
    reduced = dict(zip(BIG, _swap_halves(halves, blks)))
    reduced["w_up"] = jnp.stack([reduced["w_up0"], reduced["w_up1"]])
    reduced["w_down"] = jnp.stack([reduced["w_down0"], reduced["w_down1"]])
    small_sum = _sum_slots(small_slots, None, "sum_small")
    small_red = dict(zip(SMALL, _unpack(small_sum, small_shapes)))
    j = 2 * lax.axis_index("x") + lax.axis_index("y")
    for k in SMALL_SHARDED:
        w = given[k].shape[-1]
        small_red[k] = lax.dynamic_slice_in_dim(small_red[k], j * w, w, axis=small_red[k].ndim - 1)

    grads, deltas, new_m, new_v = {}, {}, {}, {}
    for name in WEIGHT_ORDER:
        if name in BIG_OF:
            g = reduced[BIG_OF[name]].reshape(given[name].shape)
            d, nm, nv = _adamw(_as2d(given[name]), _as2d(g), _as2d(mom_m[name]), _as2d(mom_v[name]), f"adamw_{name}")
            grads[name] = g
            deltas[name], new_m[name], new_v[name] = (t.reshape(given[name].shape) for t in (d, nm, nv))
    small_names = [n for n in WEIGHT_ORDER if n not in BIG_OF]
    for n in small_names:
        grads[n] = small_red[n].reshape(given[n].shape)
    sw, sg, sm, sv = (_pack([d[n] for n in small_names]) for d in (given, grads, mom_m, mom_v))
    d, nm, nv = _adamw(sw, sg, sm, sv, "adamw_small")
    shapes = [given[n].shape for n in small_names]
    for n, a, b_, c_ in zip(small_names, _unpack(d, shapes), _unpack(nm, shapes), _unpack(nv, shapes)):
        deltas[n], new_m[n], new_v[n] = a, b_, c_

    return (loss, grad_x, *[grads[n] for n in WEIGHT_ORDER], *[deltas[n] for n in WEIGHT_ORDER],
            *[new_m[n] for n in WEIGHT_ORDER], *[new_v[n] for n in WEIGHT_ORDER])
```

```python
import functools
import math

import numpy as np

import jax
import jax.numpy as jnp
from jax import lax
from jax.experimental import pallas as pl
from jax.experimental.pallas import tpu as pltpu

F32 = jnp.float32
BF16 = jnp.bfloat16

RMS_EPS = 1e-6
HEAD_DIM = 64
ATT_BLOCK = 128
DILATED_BRANCHES = ((128, 1), (512, 4), (2048, 16))
REL_BUCKETS = 32
REL_MAX_DISTANCE = 2048
MASKED_LOGIT = -1e30
ATTN_FWD_UNROLL = 4
ATTN_BWD_UNROLL = 4

ADAM_LR = 0.001
ADAM_B1 = 0.9
ADAM_B2 = 0.999
ADAM_EPS = 1e-08
ADAM_WD = 0.01
ADAM_STEP = 10

LANES = 128
SUBLANES_F32 = 8
SUBLANES_BF16 = 16
VMEM_LIMIT_BYTES = 56 * 1024 * 1024

MESH_AXES = ("x", "y", "c")
N_SHARDS = 4
N_DEVICES = 8
ANY = pl.BlockSpec(memory_space=pl.ANY)


def _tile(n, pref, mult):
    best = None
    for t in range(mult, min(n, pref) + 1, mult):
        if n % t == 0:
            best = t
    if best is None:
        raise ValueError(f"no tile for {n} (multiple of {mult}, at most {pref})")
    return best


def _params(*sem):
    return pltpu.CompilerParams(dimension_semantics=sem, vmem_limit_bytes=VMEM_LIMIT_BYTES)


class _Comm:
    def __init__(self, ins, outs, aliases, n_sems, copies):
        self.ins, self.outs, self.aliases, self.n_sems, self.copies = list(ins), list(outs), dict(aliases), n_sems, copies

    @staticmethod
    def join(parts):
        parts = [p for p in parts if p is not None]
        ins, outs, aliases, offs, n_sems = [], [], {}, [], 0
        for p in parts:
            offs.append((len(ins), len(outs), n_sems))
            aliases.update({len(ins) + i: len(outs) + o for i, o in p.aliases.items()})
            ins += p.ins
            outs += p.outs
            n_sems += p.n_sems

        def copies(in_refs, out_refs, sem):
            sends, recvs = [], []
            for p, (i0, o0, s0) in zip(parts, offs):
                s, r = p.copies(in_refs[i0:i0 + len(p.ins)], out_refs[o0:o0 + len(p.outs)],
                                lambda k, s0=s0: sem(s0 + k))
                sends += s
                recvs += r
            return sends, recvs

        return _Comm(ins, outs, aliases, n_sems, copies)


def _pallas(body, *, comm=None, name, out_shape, grid=(), in_specs=(), out_specs=(), scratch_shapes=(),
            compiler_params=None):
    if comm is None:
        return pl.pallas_call(body, name=name, out_shape=out_shape, grid=grid, in_specs=in_specs, out_specs=out_specs,
                              scratch_shapes=scratch_shapes, compiler_params=compiler_params)
    single = not isinstance(out_shape, (tuple, list))
    outs = (out_shape,) if single else tuple(out_shape)
    o_specs = (out_specs,) if single else tuple(out_specs)
    n_in, n_cin, n_out, n_cout, n_scr = len(in_specs), len(comm.ins), len(outs), len(comm.outs), len(scratch_shapes)

    def carried(*refs):
        base_in, c_in = refs[:n_in], refs[n_in:n_in + n_cin]
        o0 = n_in + n_cin
        base_out, c_out = refs[o0:o0 + n_out], refs[o0 + n_out:o0 + n_out + n_cout]
        s0 = o0 + n_out + n_cout
        base_scr, (send_sems, recv_sems) = refs[s0:s0 + n_scr], refs[s0 + n_scr:]
        sem = lambda k: (send_sems.at[k], recv_sems.at[k])
        first = functools.reduce(jnp.logical_and, [pl.program_id(a) == 0 for a in range(len(grid))])
        last = functools.reduce(jnp.logical_and, [pl.program_id(a) == n - 1 for a, n in enumerate(grid)])

        @pl.when(first)
        def _():
            for cp in comm.copies(c_in, c_out, sem)[0]:
                cp.start()

        body(*base_in, *base_out, *base_scr)

        @pl.when(last)
        def _():
            sends, recvs = comm.copies(c_in, c_out, sem)
            for cp in recvs:
                cp.wait_recv()
            for cp in sends:
                cp.wait_send()

    call = pl.pallas_call(
        carried, name=name, out_shape=outs + tuple(comm.outs), grid=grid,
        in_specs=list(in_specs) + [ANY] * n_cin, out_specs=o_specs + (ANY,) * n_cout,
        scratch_shapes=list(scratch_shapes) + [pltpu.SemaphoreType.DMA((comm.n_sems,))] * 2,
        input_output_aliases={n_in + i: n_out + o for i, o in comm.aliases.items()},
        compiler_params=_params(*(["arbitrary"] * len(grid))))

    def run(*args):
        res = call(*args, *comm.ins)
        base = res[0] if single else tuple(res[:n_out])
        return base, list(res[n_out:])

    return run


def _rmsnorm_fwd(x, g, name, comm=None):
    T, D = x.shape
    tm = _tile(T, 512, SUBLANES_BF16)

    def body(x_ref, g_ref, o_ref):
        xf = x_ref[...]
        r = lax.rsqrt(jnp.mean(xf * xf, axis=-1, keepdims=True) + RMS_EPS)
        o_ref[...] = ((xf * r) * g_ref[...]).astype(o_ref.dtype)

    return _pallas(
        body, comm=comm, name=name, out_shape=jax.ShapeDtypeStruct((T, D), BF16), grid=(T // tm,),
        in_specs=[pl.BlockSpec((tm, D), lambda i: (i, 0)), pl.BlockSpec((1, D), lambda i: (0, 0))],
        out_specs=pl.BlockSpec((tm, D), lambda i: (i, 0)),
        compiler_params=_params("parallel"),
    )(x, g.reshape(1, D))


def _rmsnorm_bwd(x, g, dxn, dres, name):
    T, D = x.shape
    tm = _tile(T, 512, SUBLANES_BF16)

    def body(x_ref, g_ref, dxn_ref, dres_ref, dx_ref, dg_ref):
        @pl.when(pl.program_id(0) == 0)
        def _():
            dg_ref[...] = jnp.zeros_like(dg_ref)

        xf = x_ref[...]
        r = lax.rsqrt(jnp.mean(xf * xf, axis=-1, keepdims=True) + RMS_EPS)
        xhat = xf * r
        dy = dxn_ref[...].astype(F32)
        dg_ref[0:1, :] += jnp.sum(dy * xhat, axis=0, keepdims=True)
        t = dy * g_ref[...]
        dx_ref[...] = dres_ref[...] + r * (t - xhat * jnp.mean(t * xhat, axis=-1, keepdims=True))

    row = pl.BlockSpec((tm, D), lambda i: (i, 0))
    dx, dg = pl.pallas_call(
        body, name=name,
        out_shape=(jax.ShapeDtypeStruct((T, D), F32), jax.ShapeDtypeStruct((SUBLANES_F32, D), F32)),
        grid=(T // tm,),
        in_specs=[row, pl.BlockSpec((1, D), lambda i: (0, 0)), row, row],
        out_specs=(row, pl.BlockSpec((SUBLANES_F32, D), lambda i: (0, 0))),
        compiler_params=_params("arbitrary"),
    )(x, g.reshape(1, D), dxn, dres)
    return dx, dg[0]


def _loss_head(h, g, target, name):
    T, D = h.shape
    tm = _tile(T, 512, SUBLANES_F32)

    def body(h_ref, g_ref, t_ref, dh_ref, acc_ref):
        @pl.when(pl.program_id(0) == 0)
        def _():
            acc_ref[...] = jnp.zeros_like(acc_ref)

        xf = h_ref[...]
        r = lax.rsqrt(jnp.mean(xf * xf, axis=-1, keepdims=True) + RMS_EPS)
        xhat = xf * r
        err = xhat * g_ref[...] - t_ref[...]
        dy = err * (1.0 / D)
        acc_ref[0:1, :] += jnp.sum(dy * xhat, axis=0, keepdims=True)
        acc_ref[1:2, :] += jnp.sum(err * err, axis=0, keepdims=True)
        t = dy * g_ref[...]
        dh_ref[...] = r * (t - xhat * jnp.mean(t * xhat, axis=-1, keepdims=True))

    row = pl.BlockSpec((tm, D), lambda i: (i, 0))
    dh, acc = pl.pallas_call(
        body, name=name,
        out_shape=(jax.ShapeDtypeStruct((T, D), F32), jax.ShapeDtypeStruct((SUBLANES_F32, D), F32)),
        grid=(T // tm,),
        in_specs=[row, pl.BlockSpec((1, D), lambda i: (0, 0)), row],
        out_specs=(row, pl.BlockSpec((SUBLANES_F32, D), lambda i: (0, 0))),
        compiler_params=_params("arbitrary"),
    )(h, g.reshape(1, D), target)
    return acc[1], dh, acc[0]


def _mm_nn(a, b, res, out_dtype, name, comm=None):
    T, K = a.shape
    N = b.shape[1]
    tm = _tile(T, 512, SUBLANES_BF16)
    tn = _tile(N, 1536, LANES)

    def body(a_ref, b_ref, *rest):
        o_ref = rest[-1]
        acc = jnp.dot(a_ref[...].astype(BF16), b_ref[...], preferred_element_type=F32)
        if res is not None:
            acc = acc + rest[0][...]
        o_ref[...] = acc.astype(o_ref.dtype)

    in_specs = [pl.BlockSpec((tm, K), lambda j, i: (i, 0)), pl.BlockSpec((K, tn), lambda j, i: (0, j))]
    args = [a, b]
    if res is not None:
        in_specs.append(pl.BlockSpec((tm, tn), lambda j, i: (i, j)))
        args.append(res)
    return _pallas(
        body, comm=comm, name=name, out_shape=jax.ShapeDtypeStruct((T, N), out_dtype), grid=(N // tn, T // tm),
        in_specs=in_specs, out_specs=pl.BlockSpec((tm, tn), lambda j, i: (i, j)),
        compiler_params=_params("parallel", "parallel"),
    )(*args)


def _mm_nt(dy, b, out_dtype, name, comm=None):
    T, N = dy.shape
    K = b.shape[0]
    tm = _tile(T, 1024, SUBLANES_BF16)
    tk = _tile(K, 1536, LANES)
    tn = _tile(N, 1536, LANES)
    n_steps = N // tn

    def body(dy_ref, b_ref, o_ref, acc_ref):
        n = pl.program_id(2)

        @pl.when(n == 0)
        def _():
            acc_ref[...] = jnp.zeros_like(acc_ref)

        acc_ref[...] += lax.dot_general(dy_ref[...].astype(BF16), b_ref[...], (((1,), (1,)), ((), ())),
                                        preferred_element_type=F32)

        @pl.when(n == n_steps - 1)
        def _():
            o_ref[...] = acc_ref[...].astype(o_ref.dtype)

    return _pallas(
        body, comm=comm, name=name, out_shape=jax.ShapeDtypeStruct((T, K), out_dtype), grid=(T // tm, K // tk, n_steps),
        in_specs=[pl.BlockSpec((tm, tn), lambda i, k, n: (i, n)), pl.BlockSpec((tk, tn), lambda i, k, n: (k, n))],
        out_specs=pl.BlockSpec((tm, tk), lambda i, k, n: (i, k)),
        scratch_shapes=[pltpu.VMEM((tm, tk), F32)],
        compiler_params=_params("parallel", "parallel", "arbitrary"),
    )(dy, b)


def _mm_tn(a, dy, name):
    T, K = a.shape
    N = dy.shape[1]
    tt = _tile(T, 512, SUBLANES_BF16)
    tk = _tile(K, 1536, LANES)
    tn = _tile(N, 1536, LANES)
    t_steps = T // tt

    def body(a_ref, dy_ref, o_ref, acc_ref):
        t = pl.program_id(2)

        @pl.when(t == 0)
        def _():
            acc_ref[...] = jnp.zeros_like(acc_ref)

        acc_ref[...] += lax.dot_general(a_ref[...].astype(BF16), dy_ref[...].astype(BF16),
                                        (((0,), (0,)), ((), ())), preferred_element_type=F32)

        @pl.when(t == t_steps - 1)
        def _():
            o_ref[...] = acc_ref[...].astype(o_ref.dtype)

    return pl.pallas_call(
        body, name=name, out_shape=jax.ShapeDtypeStruct((K, N), BF16), grid=(K // tk, N // tn, t_steps),
        in_specs=[pl.BlockSpec((tt, tk), lambda k, n, t: (t, k)), pl.BlockSpec((tt, tn), lambda k, n, t: (t, n))],
        out_specs=pl.BlockSpec((tk, tn), lambda k, n, t: (k, n)),
        scratch_shapes=[pltpu.VMEM((tk, tn), F32)],
        compiler_params=_params("parallel", "parallel", "arbitrary"),
    )(a, dy)


def _rows_before(halo, cur, k):
    h = halo.shape[0]
    return pltpu.roll(jnp.concatenate([halo, cur], axis=0), k, 0)[h:]


def _rows_after(cur, halo, k):
    n = cur.shape[0]
    total = n + halo.shape[0]
    return pltpu.roll(jnp.concatenate([cur, halo], axis=0), total - k, 0)[:n]


def _halo_specs(tm, width, n_rows):
    per = tm // SUBLANES_BF16
    last = n_rows // SUBLANES_BF16 - 1
    prev = pl.BlockSpec((SUBLANES_BF16, width), lambda i: (jnp.maximum(i * per - 1, 0), 0))
    nxt = pl.BlockSpec((SUBLANES_BF16, width), lambda i: (jnp.minimum((i + 1) * per, last), 0))
    return prev, nxt


def _gate_a_fwd(p, cw, seq, name, comm=None):
    T, D3 = p.shape
    D = D3 // 3
    tm = _tile(seq, 512, SUBLANES_BF16)
    cc = _tile(D, 256, LANES)
    prev, _ = _halo_specs(tm, D3, T)

    def body(p_ref, ph_ref, cw_ref, z_ref):
        at_start = (pl.program_id(0) * tm) % seq == 0
        for c0 in range(0, D, cc):
            b = p_ref[:, c0:c0 + cc].astype(F32)
            u = p_ref[:, D + c0:D + c0 + cc].astype(F32) * p_ref[:, 2 * D + c0:2 * D + c0 + cc].astype(F32)
            uh = ph_ref[:, D + c0:D + c0 + cc].astype(F32) * ph_ref[:, 2 * D + c0:2 * D + c0 + cc].astype(F32)
            uh = jnp.where(at_start, 0.0, uh)
            w = cw_ref[:, c0:c0 + cc]
            cv = _rows_before(uh, u, 2) * w[0:1] + _rows_before(uh, u, 1) * w[1:2] + u * w[2:3]
            z_ref[:, c0:c0 + cc] = (b * cv).astype(z_ref.dtype)

    return _pallas(
        body, comm=comm, name=name, out_shape=jax.ShapeDtypeStruct((T, D), BF16), grid=(T // tm,),
        in_specs=[pl.BlockSpec((tm, D3), lambda i: (i, 0)), prev, pl.BlockSpec((3, D), lambda i: (0, 0))],
        out_specs=pl.BlockSpec((tm, D), lambda i: (i, 0)),
        compiler_params=_params("parallel"),
    )(p, p, cw)


def _gate_a_bwd(p, cw, dz, seq, name, comm=None):
    T, D3 = p.shape
    D = D3 // 3
    tm = _tile(seq, 512, SUBLANES_BF16)
    cc = _tile(D, 256, LANES)
    p_prev, p_next = _halo_specs(tm, D3, T)
    _, dz_next = _halo_specs(tm, D, T)

    def body(p_ref, pp_ref, pn_ref, dz_ref, dzn_ref, cw_ref, dp_ref, dcw_ref):
        i = pl.program_id(0)

        @pl.when(i == 0)
        def _():
            dcw_ref[...] = jnp.zeros_like(dcw_ref)

        at_start = (i * tm) % seq == 0
        at_end = ((i + 1) * tm) % seq == 0
        for c0 in range(0, D, cc):
            cb, cc_, ch = slice(c0, c0 + cc), slice(D + c0, D + c0 + cc), slice(2 * D + c0, 2 * D + c0 + cc)
            b = p_ref[:, cb].astype(F32)
            c = p_ref[:, cc_].astype(F32)
            hh = p_ref[:, ch].astype(F32)
            u = c * hh
            uh = jnp.where(at_start, 0.0, pp_ref[:, cc_].astype(F32) * pp_ref[:, ch].astype(F32))
            w = cw_ref[:, cb]
            u1 = _rows_before(uh, u, 1)
            u2 = _rows_before(uh, u, 2)
            cv = u2 * w[0:1] + u1 * w[1:2] + u * w[2:3]
            dz_t = dz_ref[:, cb].astype(F32)
            dcv = dz_t * b
            dcvn = jnp.where(at_end, 0.0, dzn_ref[:, cb].astype(F32) * pn_ref[:, cb].astype(F32))
            du = dcv * w[2:3] + _rows_after(dcv, dcvn, 1) * w[1:2] + _rows_after(dcv, dcvn, 2) * w[0:1]
            dp_ref[:, cb] = (dz_t * cv).astype(dp_ref.dtype)
            dp_ref[:, cc_] = (du * hh).astype(dp_ref.dtype)
            dp_ref[:, ch] = (du * c).astype(dp_ref.dtype)
            dcw_ref[0:1, cb] += jnp.sum(dcv * u2, axis=0, keepdims=True)
            dcw_ref[1:2, cb] += jnp.sum(dcv * u1, axis=0, keepdims=True)
            dcw_ref[2:3, cb] += jnp.sum(dcv * u, axis=0, keepdims=True)

    res = _pallas(
        body, comm=comm, name=name,
        out_shape=(jax.ShapeDtypeStruct((T, D3), BF16), jax.ShapeDtypeStruct((SUBLANES_F32, D), F32)),
        grid=(T // tm,),
        in_specs=[pl.BlockSpec((tm, D3), lambda i: (i, 0)), p_prev, p_next,
                  pl.BlockSpec((tm, D), lambda i: (i, 0)), dz_next, pl.BlockSpec((3, D), lambda i: (0, 0))],
        out_specs=(pl.BlockSpec((tm, D3), lambda i: (i, 0)), pl.BlockSpec((SUBLANES_F32, D), lambda i: (0, 0))),
        compiler_params=_params("arbitrary"),
    )(p, p, p, dz, dz, cw)
    if comm is None:
        return res[0], res[1][0:3]
    return res[0][0], res[0][1][0:3], res[1]


def _ffn_gate_fwd(u, cw, cb, seq, name, comm=None):
    T, F2 = u.shape
    F = F2 // 2
    tm = _tile(seq, 256, SUBLANES_BF16)
    cc = _tile(F, 256, LANES)
    prev, _ = _halo_specs(tm, F2, T)

    def body(u_ref, uh_ref, cw_ref, cb_ref, a_ref):
        at_start = (pl.program_id(0) * tm) % seq == 0

        def conv(c0):
            cols = slice(c0, c0 + cc)
            cur = u_ref[:, cols].astype(F32)
            halo = jnp.where(at_start, 0.0, uh_ref[:, cols].astype(F32))
            w = cw_ref[:, cols]
            return (_rows_before(halo, cur, 2) * w[0:1] + _rows_before(halo, cur, 1) * w[1:2] + cur * w[2:3]
                    + cb_ref[:, cols])

        for c0 in range(0, F, cc):
            g = conv(c0)
            up = conv(F + c0)
            a_ref[:, c0:c0 + cc] = ((g * jax.nn.sigmoid(g)) * up).astype(a_ref.dtype)

    return _pallas(
        body, comm=comm, name=name, out_shape=jax.ShapeDtypeStruct((T, F), BF16), grid=(T // tm,),
        in_specs=[pl.BlockSpec((tm, F2), lambda i: (i, 0)), prev,
                  pl.BlockSpec((3, F2), lambda i: (0, 0)), pl.BlockSpec((1, F2), lambda i: (0, 0))],
        out_specs=pl.BlockSpec((tm, F), lambda i: (i, 0)),
        compiler_params=_params("parallel"),
    )(u, u, cw, cb.reshape(1, F2))


def _ffn_gate_bwd(u, cw, cb, da, seq, name, comm=None):
    T, F2 = u.shape
    F = F2 // 2
    H = SUBLANES_BF16
    tm = _tile(seq, 256, H)
    cc = _tile(F, 256, LANES)
    u_prev, u_next = _halo_specs(tm, F2, T)
    _, da_next = _halo_specs(tm, F, T)

    def body(u_ref, up_ref, un_ref, da_ref, dan_ref, cw_ref, cb_ref, du_ref, acc_ref):
        i = pl.program_id(0)

        @pl.when(i == 0)
        def _():
            acc_ref[...] = jnp.zeros_like(acc_ref)

        at_start = (i * tm) % seq == 0
        at_end = ((i + 1) * tm) % seq == 0

        def conv_ext(cols):
            ext = jnp.concatenate([jnp.where(at_start, 0.0, up_ref[:, cols].astype(F32)),
                                   u_ref[:, cols].astype(F32), un_ref[:, cols].astype(F32)], axis=0)
            w = cw_ref[:, cols]
            e1 = pltpu.roll(ext, 1, 0)
            e2 = pltpu.roll(ext, 2, 0)
            out = (e2 * w[0:1] + e1 * w[1:2] + ext * w[2:3] + cb_ref[:, cols])[H:]
            return out, ext, e1, e2

        def back(d2, cols, ext, e1, e2):
            w = cw_ref[:, cols]
            n = tm + H
            d1n = pltpu.roll(d2, n - 1, 0)[:tm]
            d2n = pltpu.roll(d2, n - 2, 0)[:tm]
            d_t = d2[:tm]
            du_ref[:, cols] = (d_t * w[2:3] + d1n * w[1:2] + d2n * w[0:1]).astype(du_ref.dtype)
            acc_ref[0:1, cols] += jnp.sum(d_t * e2[H:H + tm], axis=0, keepdims=True)
            acc_ref[1:2, cols] += jnp.sum(d_t * e1[H:H + tm], axis=0, keepdims=True)
            acc_ref[2:3, cols] += jnp.sum(d_t * ext[H:H + tm], axis=0, keepdims=True)
            acc_ref[3:4, cols] += jnp.sum(d_t, axis=0, keepdims=True)

        for c0 in range(0, F, cc):
            gc, uc = slice(c0, c0 + cc), slice(F + c0, F + c0 + cc)
            g, g_ext, g_e1, g_e2 = conv_ext(gc)
            up, up_ext, up_e1, up_e2 = conv_ext(uc)
            da_ext = jnp.concatenate([da_ref[:, gc].astype(F32),
                                      jnp.where(at_end, 0.0, dan_ref[:, gc].astype(F32))], axis=0)
            sg = jax.nn.sigmoid(g)
            d_up = da_ext * (g * sg)
            d_g = da_ext * up * (sg * (1.0 + g * (1.0 - sg)))
            back(d_g, gc, g_ext, g_e1, g_e2)
            back(d_up, uc, up_ext, up_e1, up_e2)

    res = _pallas(
        body, comm=comm, name=name,
        out_shape=(jax.ShapeDtypeStruct((T, F2), BF16), jax.ShapeDtypeStruct((SUBLANES_F32, F2), F32)),
        grid=(T // tm,),
        in_specs=[pl.BlockSpec((tm, F2), lambda i: (i, 0)), u_prev, u_next,
                  pl.BlockSpec((tm, F), lambda i: (i, 0)), da_next,
                  pl.BlockSpec((3, F2), lambda i: (0, 0)), pl.BlockSpec((1, F2), lambda i: (0, 0))],
        out_specs=(pl.BlockSpec((tm, F2), lambda i: (i, 0)), pl.BlockSpec((SUBLANES_F32, F2), lambda i: (0, 0))),
        compiler_params=_params("arbitrary"),
    )(u, u, u, da, da, cw, cb.reshape(1, F2))
    (du, acc), landed = res if comm is not None else (res, None)
    return (du, acc[0:3], acc[3]) if comm is None else (du, acc[0:3], acc[3], landed)


def _bucket_map():
    P = ATT_BLOCK
    qi = np.arange(P, dtype=np.int64)[:, None]
    kc = np.arange(2 * P, dtype=np.int64)[None, :]
    delta = qi + P - kc
    maps = []
    max_exact = REL_BUCKETS // 2
    for window, dilation in DILATED_BRANCHES:
        band = (delta >= 0) & (delta <= window // dilation)
        n = np.maximum(delta * dilation, 0)
        nf = np.maximum(n, max_exact).astype(np.float32)
        large = max_exact + (np.log(nf / np.float32(max_exact)) / np.float32(math.log(REL_MAX_DISTANCE / max_exact))
                             * np.float32(REL_BUCKETS - max_exact)).astype(np.int32)
        large = np.minimum(large, REL_BUCKETS - 1)
        bucket = np.where(n < max_exact, n, large)
        maps.append(np.where(band, bucket, -1).astype(np.int32))
    return np.stack(maps)


def _bias_tables(rel_bias, bmap, name):
    n_pairs = rel_bias.shape[1] // 2
    nbr, P, P2 = bmap.shape

    def body(rb_ref, bm_ref, o_ref):
        pair = pl.program_id(0)
        in_seq = lax.broadcasted_iota(jnp.int32, (P, P2), 1) >= P
        for br in range(nbr):
            bm = bm_ref[br]
            for hh in range(2):
                acc = jnp.full((P, P2), MASKED_LOGIT, F32)
                for b in range(REL_BUCKETS):
                    acc = jnp.where(bm == b, rb_ref[b, 2 * pair + hh], acc)
                o_ref[br, 0, 0, hh * P:(hh + 1) * P, :] = acc
                o_ref[br, 0, 1, hh * P:(hh + 1) * P, :] = jnp.where(in_seq, acc, MASKED_LOGIT)

    return pl.pallas_call(
        body, name=name, out_shape=jax.ShapeDtypeStruct((nbr, n_pairs, 2, 2 * P, P2), F32), grid=(n_pairs,),
        in_specs=[pl.BlockSpec(memory_space=pltpu.SMEM), pl.BlockSpec((nbr, P, P2), lambda h: (0, 0, 0))],
        out_specs=pl.BlockSpec((nbr, 1, 2, 2 * P, P2), lambda h: (0, h, 0, 0, 0)),
        compiler_params=_params("parallel"),
    )(rel_bias, bmap)


def _bias_grad(dbias, bmap, name):
    nbr, n_pairs, _, P2 = dbias.shape
    P = P2 // 2

    def body(db_ref, bm_ref, o_ref):
        lane = lax.broadcasted_iota(jnp.int32, (1, LANES), 1)
        for hh in range(2):
            row = jnp.zeros((1, LANES), F32)
            for br in range(nbr):
                bm = bm_ref[br]
                d = db_ref[br, 0, hh * P:(hh + 1) * P, :]
                for b in range(REL_BUCKETS):
                    hit = jnp.sum(jnp.where(bm == b, d, 0.0), axis=1, keepdims=True)
                    row = row + jnp.where(lane == b, jnp.sum(hit, axis=0, keepdims=True), 0.0)
            o_ref[hh] = row

    return pl.pallas_call(
        body, name=name, out_shape=jax.ShapeDtypeStruct((2 * n_pairs, 1, LANES), F32), grid=(n_pairs,),
        in_specs=[pl.BlockSpec((nbr, 1, P2, P2), lambda h: (0, h, 0, 0)), pl.BlockSpec((nbr, P, P2), lambda h: (0, 0, 0))],
        out_specs=pl.BlockSpec((2, 1, LANES), lambda h: (h, 0, 0)),
        compiler_params=_params("parallel"),
    )(dbias, bmap)[:, 0, :]


def _rows(start, dilation):
    if dilation == 1:
        return pl.ds(pl.multiple_of(start, ATT_BLOCK), ATT_BLOCK)
    return pl.ds(start, ATT_BLOCK, stride=dilation)


def _for_each_block(seq, unroll, fn):
    P = ATT_BLOCK
    for br, (_, d) in enumerate(DILATED_BRANCHES):
        nb = seq // d // P
        total = nb * d
        u = unroll if total % unroll == 0 else 1

        def some(i, carry, br=br, d=d, nb=nb, u=u):
            blocks = []
            for k in range(u):
                idx = i * u + k
                r, j = idx // nb, idx % nb
                blocks.append((r + j * (d * P), r + jnp.maximum(j - 1, 0) * (d * P), jnp.where(j == 0, 1, 0)))
            fn(br, d, blocks)
            return carry

        lax.fori_loop(0, total // u, some, 0)


def _stack_heads(x, head0):
    return jnp.concatenate([jnp.where(head0, x, 0.0), jnp.where(head0, 0.0, x)], axis=0).astype(BF16)


def _window(ref, start, prev, dilation):
    return jnp.concatenate([ref[_rows(prev, dilation), :], ref[_rows(start, dilation), :]], axis=0).astype(BF16)


def _attn_fwd(q, kv, bias, name):
    B, S, D = q.shape
    P = ATT_BLOCK
    n_pairs = D // LANES
    nbr = len(DILATED_BRANCHES)
    scale = HEAD_DIM ** -0.5

    def body(q_ref, k_ref, v_ref, bias_ref, o_ref, lse_ref, *stats):
        m_s, l_s, acc_s = stats[0:nbr], stats[nbr:2 * nbr], stats[2 * nbr:3 * nbr]
        head0 = lax.broadcasted_iota(jnp.int32, (P, LANES), 1) < HEAD_DIM

        def block(br, d, blocks):
            s = [lax.dot_general(_stack_heads(q_ref[_rows(start, d), :] * scale, head0),
                                 _window(k_ref, start, prev, d), (((1,), (1,)), ((), ())),
                                 preferred_element_type=F32) + bias_ref[br, 0, first]
                 for start, prev, first in blocks]
            m = [jnp.max(x, axis=-1, keepdims=True) for x in s]
            p = [jnp.exp(x - y) for x, y in zip(s, m)]
            l = [jnp.sum(x, axis=-1, keepdims=True) for x in p]
            pv = [jnp.dot(x.astype(BF16), _window(v_ref, start, prev, d), preferred_element_type=F32)
                  for x, (start, prev, _) in zip(p, blocks)]
            for k, (start, _, _) in enumerate(blocks):
                rows = _rows(start, d)
                m_s[br][rows, :] = jnp.where(head0, m[k][:P], m[k][P:])
                l_s[br][rows, :] = jnp.where(head0, l[k][:P], l[k][P:])
                acc_s[br][rows, :] = jnp.where(head0, pv[k][:P], pv[k][P:])

        _for_each_block(S, ATTN_FWD_UNROLL, block)

        chunk = _tile(S, 256, SUBLANES_F32)

        def merge(i, carry):
            rows = pl.ds(pl.multiple_of(i * chunk, chunk), chunk)
            ms = [m_s[br][rows, :] for br in range(nbr)]
            m = functools.reduce(jnp.maximum, ms)
            l = jnp.zeros((chunk, LANES), F32)
            acc = jnp.zeros((chunk, LANES), F32)
            for br in range(nbr):
                w = jnp.exp(ms[br] - m)
                l = l + w * l_s[br][rows, :]
                acc = acc + w * acc_s[br][rows, :]
            o_ref[rows, :] = acc / l
            lse_ref[rows, :] = m + jnp.log(l)
            return carry

        lax.fori_loop(0, S // chunk, merge, 0)

    slab = lambda col0: pl.BlockSpec((None, S, LANES), lambda b, h: (b, 0, col0 + h))
    return pl.pallas_call(
        body, name=name,
        out_shape=(jax.ShapeDtypeStruct((B, S, D), F32), jax.ShapeDtypeStruct((B, S, D), F32)),
        grid=(B, n_pairs),
        in_specs=[slab(0), slab(0), slab(n_pairs),
                  pl.BlockSpec((nbr, 1, 2, 2 * P, 2 * P), lambda b, h: (0, h, 0, 0, 0))],
        out_specs=(slab(0), slab(0)),
        scratch_shapes=[pltpu.VMEM((S, LANES), F32)] * (3 * nbr),
        compiler_params=_params("parallel", "parallel"),
    )(q, kv, kv, bias)


def _attn_bwd(q, kv, o, lse, do, bias, name, comm=None):
    B, S, D = q.shape
    P = ATT_BLOCK
    n_pairs = D // LANES
    nbr = len(DILATED_BRANCHES)
    scale = HEAD_DIM ** -0.5

    def body(q_ref, k_ref, v_ref, o_ref, lse_ref, do_ref, bias_ref, dq_ref, dk_ref, dv_ref, dbias_ref, delta_s):
        head0 = lax.broadcasted_iota(jnp.int32, (P, LANES), 1) < HEAD_DIM

        @pl.when(pl.program_id(1) == 0)
        def _():
            dbias_ref[...] = jnp.zeros_like(dbias_ref)

        chunk = _tile(S, 512, SUBLANES_F32)

        def prepare(i, carry):
            rows = pl.ds(pl.multiple_of(i * chunk, chunk), chunk)
            x = do_ref[rows, :] * o_ref[rows, :]
            h0 = lax.broadcasted_iota(jnp.int32, (chunk, LANES), 1) < HEAD_DIM
            d0 = jnp.sum(jnp.where(h0, x, 0.0), axis=-1, keepdims=True)
            d1 = jnp.sum(jnp.where(h0, 0.0, x), axis=-1, keepdims=True)
            delta_s[rows, :] = jnp.where(h0, d0, d1)
            zero = jnp.zeros((chunk, LANES), F32)
            dq_ref[rows, :] = zero
            dk_ref[rows, :] = zero
            dv_ref[rows, :] = zero
            return carry

        lax.fori_loop(0, S // chunk, prepare, 0)

        def per_head(x):
            return jnp.concatenate([x[:, 0:1], x[:, HEAD_DIM:HEAD_DIM + 1]], axis=0)

        nt = (((1,), (1,)), ((), ()))
        tn = (((0,), (0,)), ((), ()))

        def block(br, d, blocks):
            q2 = [_stack_heads(q_ref[_rows(start, d), :] * scale, head0) for start, _, _ in blocks]
            do2 = [_stack_heads(do_ref[_rows(start, d), :], head0) for start, _, _ in blocks]
            kb = [_window(k_ref, start, prev, d) for start, prev, _ in blocks]
            vb = [_window(v_ref, start, prev, d) for start, prev, _ in blocks]
            s = [lax.dot_general(a, b, nt, preferred_element_type=F32) + bias_ref[br, 0, first]
                 for a, b, (_, _, first) in zip(q2, kb, blocks)]
            dp = [lax.dot_general(a, b, nt, preferred_element_type=F32) for a, b in zip(do2, vb)]
            p = [jnp.exp(x - per_head(lse_ref[_rows(start, d), :])) for x, (start, _, _) in zip(s, blocks)]
            ds = [x * (y - per_head(delta_s[_rows(start, d), :])) for x, y, (start, _, _) in zip(p, dp, blocks)]
            for x in ds:
                dbias_ref[br, 0] += x
            ds16 = [x.astype(BF16) for x in ds]
            dq2 = [jnp.dot(a, b, preferred_element_type=F32) for a, b in zip(ds16, kb)]
            dk = [lax.dot_general(a, b, tn, preferred_element_type=F32) for a, b in zip(ds16, q2)]
            dv = [lax.dot_general(a.astype(BF16), b, tn, preferred_element_type=F32) for a, b in zip(p, do2)]
            for k, (start, prev, _) in enumerate(blocks):
                rows, prows = _rows(start, d), _rows(prev, d)
                dq_ref[rows, :] += jnp.where(head0, dq2[k][:P], dq2[k][P:]) * scale
                dk_ref[prows, :] += dk[k][:P]
                dk_ref[rows, :] += dk[k][P:]
                dv_ref[prows, :] += dv[k][:P]
                dv_ref[rows, :] += dv[k][P:]

        _for_each_block(S, ATTN_BWD_UNROLL, block)

    slab = lambda col0: pl.BlockSpec((None, S, LANES), lambda h, b: (b, 0, col0 + h))
    tab = pl.BlockSpec((nbr, 1, 2, 2 * P, 2 * P), lambda h, b: (0, h, 0, 0, 0))
    dtab = pl.BlockSpec((nbr, 1, 2 * P, 2 * P), lambda h, b: (0, h, 0, 0))
    shp = jax.ShapeDtypeStruct((B, S, D), F32)
    return _pallas(
        body, comm=comm, name=name,
        out_shape=(shp, shp, shp, jax.ShapeDtypeStruct((nbr, n_pairs, 2 * P, 2 * P), F32)),
        grid=(n_pairs, B),
        in_specs=[slab(0), slab(0), slab(n_pairs), slab(0), slab(0), slab(0), tab],
        out_specs=(slab(0), slab(0), slab(0), dtab),
        scratch_shapes=[pltpu.VMEM((S, LANES), F32)],
        compiler_params=_params("parallel", "arbitrary"),
    )(q, kv, kv, o, lse, do, bias)


def _adamw(w, g, m, v, name):
    R, C = w.shape
    tr = _tile(R, 256, SUBLANES_F32) if R % SUBLANES_F32 == 0 else R
    tc = _tile(C, 2048, LANES) if C % LANES == 0 else C

    def body(w_ref, g_ref, m_ref, v_ref, d_ref, nm_ref, nv_ref):
        g_ = g_ref[...]
        m2 = ADAM_B1 * m_ref[...] + (1.0 - ADAM_B1) * g_
        v2 = ADAM_B2 * v_ref[...] + (1.0 - ADAM_B2) * (g_ * g_)
        m_hat = m2 / (1.0 - ADAM_B1 ** ADAM_STEP)
        v_hat = v2 / (1.0 - ADAM_B2 ** ADAM_STEP)
        d_ref[...] = -ADAM_LR * (m_hat / (jnp.sqrt(v_hat) + ADAM_EPS) + ADAM_WD * w_ref[...])
        nm_ref[...] = m2
        nv_ref[...] = v2

    blk = pl.BlockSpec((tr, tc), lambda i, j: (i, j))
    shp = jax.ShapeDtypeStruct((R, C), F32)
    return pl.pallas_call(
        body, name=name, out_shape=(shp, shp, shp), grid=(R // tr, C // tc),
        in_specs=[blk] * 4, out_specs=(blk,) * 3, compiler_params=_params("parallel", "parallel"),
    )(w, g, m, v)


def _sum_slots(slots, own, name):
    n, R, C = slots.shape
    tr = _tile(R, 256, SUBLANES_BF16) if R % SUBLANES_BF16 == 0 else R
    tc = _tile(C, 2048, LANES) if C % LANES == 0 else C

    def body(s_ref, *rest):
        o_ref = rest[-1]
        acc = s_ref[0].astype(F32)
        if own is not None:
            acc = rest[0][...].astype(F32) + acc
        for k in range(1, n):
            acc = acc + s_ref[k].astype(F32)
        o_ref[...] = acc

    in_specs = [pl.BlockSpec((n, tr, tc), lambda i, j: (0, i, j))]
    args = [slots]
    if own is not None:
        in_specs.append(pl.BlockSpec((tr, tc), lambda i, j: (i, j)))
        args.append(own)
    return pl.pallas_call(
        body, name=name, out_shape=jax.ShapeDtypeStruct((R, C), F32), grid=(R // tr, C // tc),
        in_specs=in_specs, out_specs=pl.BlockSpec((tr, tc), lambda i, j: (i, j)),
        compiler_params=_params("parallel", "parallel"),
    )(*args)


def _my_place():
    return lax.axis_index("x"), lax.axis_index("y"), lax.axis_index("c")


def _other_chips(x, y):
    return [(1 - x, y), (x, 1 - y), (1 - x, 1 - y)]


def _piece(ref, blk, axis, shard, half):
    h0 = blk[0] // 2
    idx = []
    for dim, n in enumerate(blk):
        if dim == 0:
            start = half * h0 + (shard * n if axis == 0 else 0)
            idx.append(pl.ds(start, h0))
        elif dim == axis:
            idx.append(pl.ds(shard * n, n))
        else:
            idx.append(slice(None))
    return ref.at[tuple(idx)]


def _half_of(ref, blk, half):
    return ref.at[pl.ds(half * (blk[0] // 2), blk[0] // 2)]


def _piece_start(blk, axis, shard, half):
    return tuple(half * (blk[0] // 2) + (shard * n if axis == 0 else 0) if dim == 0 else (shard * n if dim == axis else 0)
                 for dim, n in enumerate(blk))


def _gather_weights(shards, axes, small):
    n = len(shards)
    blks = [s.shape for s in shards]
    fulls = [tuple(d * N_SHARDS if i == ax else d for i, d in enumerate(b)) for b, ax in zip(blks, axes)]
    me2 = 2 * lax.axis_index("x") + lax.axis_index("y")
    bases = [lax.dynamic_update_slice(jnp.zeros(f, s.dtype), s, tuple(me2 * b[i] if i == ax else 0 for i in range(len(b))))
             for f, s, b, ax in zip(fulls, shards, blks, axes)]

    def stage1(*refs):
        ins, small_in = refs[:n], refs[2 * n]
        outs, small_out = refs[2 * n + 1:3 * n + 1], refs[3 * n + 1]
        send_sems, recv_sems, local_sems = refs[3 * n + 2:]
        x, y, c = _my_place()
        me = 2 * x + y
        chips = _other_chips(x, y)
        local = [pltpu.make_async_copy(small_in, small_out.at[me], local_sems.at[0])]
        for cp in local:
            cp.start()
        sends, recvs = [], []
        for a in range(n + 1):
            for k, (px, py) in enumerate(chips):
                peer = 2 * px + py
                if a < n:
                    src = _half_of(ins[a], blks[a], c)
                    dst = _piece(outs[a], blks[a], axes[a], me, c)
                    land = _piece(outs[a], blks[a], axes[a], peer, c)
                else:
                    src, dst, land = small_in, small_out.at[me], small_out.at[peer]
                sends.append(pltpu.make_async_remote_copy(
                    src_ref=src, dst_ref=dst, send_sem=send_sems.at[a, k], recv_sem=recv_sems.at[a, k],
                    device_id=(px, py, c), device_id_type=pl.DeviceIdType.MESH))
                recvs.append(pltpu.make_async_remote_copy(
                    src_ref=src, dst_ref=land, send_sem=send_sems.at[a, k], recv_sem=recv_sems.at[a, k],
                    device_id=(px, py, c), device_id_type=pl.DeviceIdType.MESH))
        for cp in sends:
            cp.start()
        for cp in recvs:
            cp.wait_recv()
        for cp in sends:
            cp.wait_send()
        for cp in local:
            cp.wait()

    out_shape = [jax.ShapeDtypeStruct(f, s.dtype) for f, s in zip(fulls, shards)]
    out_shape.append(jax.ShapeDtypeStruct((N_SHARDS,) + small.shape, small.dtype))
    res = pl.pallas_call(
        stage1, name="gather_weights_ici", out_shape=out_shape,
        in_specs=[ANY] * (2 * n + 1), out_specs=[ANY] * (n + 1),
        input_output_aliases={n + a: a for a in range(n)},
        scratch_shapes=[pltpu.SemaphoreType.DMA((n + 1, 3)), pltpu.SemaphoreType.DMA((n + 1, 3)),
                        pltpu.SemaphoreType.DMA((1,))],
    )(*shards, *bases, small)
    partial, small_full = res[:n], res[n]

    def stage2(*refs):
        outs = refs[n:2 * n]
        send_sems, recv_sems = refs[2 * n:]
        x, y, c = _my_place()
        sends, recvs = [], []
        for a in range(n):
            for k, (px, py) in enumerate(_other_chips(x, y)):
                peer = 2 * px + py
                mine = _piece(outs[a], blks[a], axes[a], peer, c)
                theirs = _piece(outs[a], blks[a], axes[a], peer, 1 - c)
                sends.append(pltpu.make_async_remote_copy(
                    src_ref=mine, dst_ref=mine, send_sem=send_sems.at[a, k], recv_sem=recv_sems.at[a, k],
                    device_id=(x, y, 1 - c), device_id_type=pl.DeviceIdType.MESH))
                recvs.append(pltpu.make_async_remote_copy(
                    src_ref=mine, dst_ref=theirs, send_sem=send_sems.at[a, k], recv_sem=recv_sems.at[a, k],
                    device_id=(x, y, 1 - c), device_id_type=pl.DeviceIdType.MESH))
        for cp in sends:
            cp.start()
        for cp in recvs:
            cp.wait_recv()
        for cp in sends:
            cp.wait_send()

    full = pl.pallas_call(
        stage2, name="gather_weights_d2d", out_shape=out_shape[:n],
        in_specs=[ANY] * n, out_specs=[ANY] * n, input_output_aliases={a: a for a in range(n)},
        scratch_shapes=[pltpu.SemaphoreType.DMA((n, 3)), pltpu.SemaphoreType.DMA((n, 3))],
    )(*partial)
    return list(full), small_full


def _scatter_grads(grads, blks, axes, small):
    n = len(grads)
    halves = [(b[0] // 2,) + tuple(b[1:]) for b in blks]

    def body(*refs):
        ins, small_in = refs[:n], refs[n]
        outs, small_out = refs[n + 1:2 * n + 1], refs[2 * n + 1]
        send_sems, recv_sems, local_sems = refs[2 * n + 2:]
        x, y, c = _my_place()
        me = 4 * x + 2 * y + c
        local = [pltpu.make_async_copy(small_in, small_out.at[me], local_sems.at[0])]
        for cp in local:
            cp.start()
        sends, recvs = [], []
        for rel in range(1, N_DEVICES):
            px = x ^ ((rel >> 2) & 1)
            py = y ^ ((rel >> 1) & 1)
            pc = c ^ (rel & 1)
            peer = 4 * px + 2 * py + pc
            for a in range(n + 1):
                if a < n:
                    src = _piece(ins[a], blks[a], axes[a], 2 * px + py, pc)
                    dst = land = outs[a].at[rel - 1]
                else:
                    src, dst, land = small_in, small_out.at[me], small_out.at[peer]
                sends.append(pltpu.make_async_remote_copy(
                    src_ref=src, dst_ref=dst, send_sem=send_sems.at[a, rel - 1], recv_sem=recv_sems.at[a, rel - 1],
                    device_id=(px, py, pc), device_id_type=pl.DeviceIdType.MESH))
                recvs.append(pltpu.make_async_remote_copy(
                    src_ref=src, dst_ref=land, send_sem=send_sems.at[a, rel - 1], recv_sem=recv_sems.at[a, rel - 1],
                    device_id=(px, py, pc), device_id_type=pl.DeviceIdType.MESH))
        for cp in sends:
            cp.start()
        for cp in recvs:
            cp.wait_recv()
        for cp in sends:
            cp.wait_send()
        for cp in local:
            cp.wait()

    out_shape = [jax.ShapeDtypeStruct((N_DEVICES - 1,) + h, g.dtype) for h, g in zip(halves, grads)]
    out_shape.append(jax.ShapeDtypeStruct((N_DEVICES,) + small.shape, small.dtype))
    res = pl.pallas_call(
        body, name="scatter_grads", out_shape=out_shape,
        in_specs=[ANY] * (n + 1), out_specs=[ANY] * (n + 1),
        scratch_shapes=[pltpu.SemaphoreType.DMA((n + 1, N_DEVICES - 1)), pltpu.SemaphoreType.DMA((n + 1, N_DEVICES - 1)),
                        pltpu.SemaphoreType.DMA((1,))],
    )(*grads, small)
    return list(res[:n]), res[n]


def _swap_halves(halves, blks):
    n = len(halves)
    c_ = lax.axis_index("c")
    bases = [lax.dynamic_update_slice(jnp.zeros(b, F32), h, (c_ * (b[0] // 2),) + (0,) * (len(b) - 1))
             for h, b in zip(halves, blks)]

    def body(*refs):
        ins, outs = refs[:n], refs[2 * n:3 * n]
        send_sems, recv_sems = refs[3 * n:]
        x, y, c = _my_place()
        sends, recvs = [], []
        for a in range(n):
            sends.append(pltpu.make_async_remote_copy(
                src_ref=ins[a], dst_ref=_half_of(outs[a], blks[a], c), send_sem=send_sems.at[a],
                recv_sem=recv_sems.at[a], device_id=(x, y, 1 - c), device_id_type=pl.DeviceIdType.MESH))
            recvs.append(pltpu.make_async_remote_copy(
                src_ref=ins[a], dst_ref=_half_of(outs[a], blks[a], 1 - c), send_sem=send_sems.at[a],
                recv_sem=recv_sems.at[a], device_id=(x, y, 1 - c), device_id_type=pl.DeviceIdType.MESH))
        for cp in sends:
            cp.start()
        for cp in recvs:
            cp.wait_recv()
        for cp in sends:
            cp.wait_send()

    return list(pl.pallas_call(
        body, name="swap_grad_halves", out_shape=[jax.ShapeDtypeStruct(b, F32) for b in blks],
        in_specs=[ANY] * (2 * n), out_specs=[ANY] * n, input_output_aliases={n + a: a for a in range(n)},
        scratch_shapes=[pltpu.SemaphoreType.DMA((n,)), pltpu.SemaphoreType.DMA((n,))],
    )(*halves, *bases))


def _remote(src, dst, sems, device):
    return pltpu.make_async_remote_copy(src_ref=src, dst_ref=dst, send_sem=sems[0], recv_sem=sems[1],
                                        device_id=device, device_id_type=pl.DeviceIdType.MESH)


def _place_shard(shard, axis):
    blk = shard.shape
    full = tuple(d * N_SHARDS if i == axis else d for i, d in enumerate(blk))
    me2 = 2 * lax.axis_index("x") + lax.axis_index("y")
    return lax.dynamic_update_slice(jnp.zeros(full, shard.dtype), shard,
                                    tuple(me2 * blk[i] if i == axis else 0 for i in range(len(blk))))


def _gather_ici_task(shards, axes):
    n = len(shards)
    blks = [s.shape for s in shards]
    bases = [_place_shard(s, ax) for s, ax in zip(shards, axes)]

    def copies(in_refs, out_refs, sem):
        x, y, c = _my_place()
        me = 2 * x + y
        sends, recvs = [], []
        for a in range(n):
            for k, (px, py) in enumerate(_other_chips(x, y)):
                src = _half_of(in_refs[a], blks[a], c)
                sends.append(_remote(src, _piece(out_refs[a], blks[a], axes[a], me, c), sem(3 * a + k), (px, py, c)))
                recvs.append(_remote(src, _piece(out_refs[a], blks[a], axes[a], 2 * px + py, c), sem(3 * a + k),
                                     (px, py, c)))
        return sends, recvs

    return _Comm(list(shards) + bases, [jax.ShapeDtypeStruct(b.shape, b.dtype) for b in bases],
                 {n + a: a for a in range(n)}, 3 * n, copies)


def _gather_d2d_task(partials, blks, axes):
    n = len(partials)

    def copies(in_refs, out_refs, sem):
        x, y, c = _my_place()
        sends, recvs = [], []
        for a in range(n):
            for k, (px, py) in enumerate(_other_chips(x, y)):
                mine = _piece(out_refs[a], blks[a], axes[a], 2 * px + py, c)
                theirs = _piece(out_refs[a], blks[a], axes[a], 2 * px + py, 1 - c)
                sends.append(_remote(mine, mine, sem(3 * a + k), (x, y, 1 - c)))
                recvs.append(_remote(mine, theirs, sem(3 * a + k), (x, y, 1 - c)))
        return sends, recvs

    return _Comm(partials, [jax.ShapeDtypeStruct(p.shape, p.dtype) for p in partials], {a: a for a in range(n)},
                 3 * n, copies)


def _scatter_task(grads, blks, axes):
    n = len(grads)

    def copies(in_refs, out_refs, sem):
        x, y, c = _my_place()
        sends, recvs = [], []
        for rel in range(1, N_DEVICES):
            px, py, pc = x ^ ((rel >> 2) & 1), y ^ ((rel >> 1) & 1), c ^ (rel & 1)
            for a in range(n):
                src = _piece(in_refs[a], blks[a], axes[a], 2 * px + py, pc)
                k = (N_DEVICES - 1) * a + rel - 1
                sends.append(_remote(src, out_refs[a].at[rel - 1], sem(k), (px, py, pc)))
                recvs.append(_remote(src, out_refs[a].at[rel - 1], sem(k), (px, py, pc)))
        return sends, recvs

    outs = [jax.ShapeDtypeStruct((N_DEVICES - 1, b[0] // 2) + tuple(b[1:]), g.dtype) for b, g in zip(blks, grads)]
    return _Comm(grads, outs, {}, (N_DEVICES - 1) * n, copies)


def _pack(arrays):
    flat = jnp.concatenate([a.reshape(-1).astype(F32) for a in arrays])
    pad = (-flat.shape[0]) % (SUBLANES_F32 * LANES)
    return jnp.pad(flat, (0, pad)).reshape(-1, LANES)


def _unpack(packed, shapes):
    flat = packed.reshape(-1)
    out, off = [], 0
    for s in shapes:
        n = int(np.prod(s))
        out.append(flat[off:off + n].reshape(s))
        off += n
    return out


def _local_step(x, target, W, shards, geom, small):
    W = dict(W)
    B, S, D = x.shape
    T = B * S
    x2 = x.reshape(T, D)
    tgt = target.reshape(T, D)
    bmap = jnp.asarray(_bucket_map())
    blk = lambda names: [geom[k][0] for k in names]
    axs = lambda names: [geom[k][1] for k in names]
    ici = lambda names: _gather_ici_task([shards[k] for k in names], axs(names))
    d2d = lambda names, partials: _gather_d2d_task(list(partials), blk(names), axs(names))
    big, slots = {}, {}

    def scatter(names):
        return _scatter_task([big[k] for k in names], blk(names), axs(names))

    def ffn_bwd(l, dout, h, saved, first, second):
        xn, u, a = saved
        cw, cb = small["ffn_conv"][l], small["ffn_conv_b"][l]
        da = _mm_nt(dout, W[f"w_down{l}"], BF16, f"ffn{l}_down_dx")
        big[f"w_down{l}"] = _mm_tn(a, dout, f"ffn{l}_down_dw")
        names = first + [f"w_down{l}"]
        du, g_cw, g_cb, landed = _ffn_gate_bwd(u, cw, cb, da, S, f"ffn{l}_gate_bwd", comm=scatter(names))
        slots.update(zip(names, landed))
        big[f"w_up{l}"] = _mm_tn(xn, du, f"ffn{l}_up_dw")
        if second:
            dxn, (slots[f"w_up{l}"],) = _mm_nt(du, W[f"w_up{l}"], BF16, f"ffn{l}_up_dx", comm=scatter([f"w_up{l}"]))
        else:
            dxn = _mm_nt(du, W[f"w_up{l}"], BF16, f"ffn{l}_up_dx")
        dh, g_norm = _rmsnorm_bwd(h, small["ffn_norm"][l], dxn, dout, f"ffn{l}_norm_bwd")
        return dh, g_cw, g_cb, g_norm

    xn0 = _rmsnorm_fwd(x2, small["a_norm"][0], "a_norm")
    p, part = _mm_nn(xn0, W["w_in"], None, BF16, "a_in", comm=ici(["w_up0"]))
    z, (W["w_up0"],) = _gate_a_fwd(p, small["a_conv"][0], S, "a_gate", comm=d2d(["w_up0"], part))
    h1, part = _mm_nn(z, W["w_out"], x2, F32, "a_out", comm=ici(["w_down0"]))
    xn1, (W["w_down0"],) = _rmsnorm_fwd(h1, small["ffn_norm"][0], "ffn0_norm", comm=d2d(["w_down0"], part))
    later = ["w_kv", "w_q", "w_o", "w_up1"]
    u0, part = _mm_nn(xn1, W["w_up0"], None, BF16, "ffn0_up", comm=ici(later))
    a0, landed = _ffn_gate_fwd(u0, small["ffn_conv"][0], small["ffn_conv_b"][0], S, "ffn0_gate",
                               comm=_Comm.join([d2d(later, part), ici(["w_down1"])]))
    W.update(zip(later, landed[:len(later)]))
    h2, (W["w_down1"],) = _mm_nn(a0, W["w_down0"], h1, F32, "ffn0_down", comm=d2d(["w_down1"], landed[len(later):]))
    kvn = _rmsnorm_fwd(h2, small["kv_norm"], "kv_norm")
    kv = _mm_nn(kvn, W["w_kv"], None, F32, "kv_proj")
    xn3 = _rmsnorm_fwd(h2, small["b_norm"][0], "b_norm")
    q = _mm_nn(xn3, W["w_q"], None, F32, "q_proj")
    bias = _bias_tables(small["rel_bias"], bmap, "rel_bias_tables")
    q3, kv3 = q.reshape(B, S, D), kv.reshape(B, S, 2 * D)
    o3, lse3 = _attn_fwd(q3, kv3, bias, "attn_fwd")
    o = o3.reshape(T, D)
    h3 = _mm_nn(o, W["w_o"], h2, F32, "o_proj")
    xn4 = _rmsnorm_fwd(h3, small["ffn_norm"][1], "ffn1_norm")
    u1 = _mm_nn(xn4, W["w_up1"], None, BF16, "ffn1_up")
    a1 = _ffn_gate_fwd(u1, small["ffn_conv"][1], small["ffn_conv_b"][1], S, "ffn1_gate")
    h4 = _mm_nn(a1, W["w_down1"], h3, F32, "ffn1_down")
    sq_err, dh4, g_final = _loss_head(h4, small["final_norm"], tgt, "loss_head")
    loss = 0.5 * jnp.sum(sq_err) / D

    dh3, g_cw1, g_cb1, g_fn1 = ffn_bwd(1, dh4, h3, (xn4, u1, a1), [], False)
    do = _mm_nt(dh3, W["w_o"], F32, "o_proj_dx")
    big["w_o"] = _mm_tn(o, dh3, "o_proj_dw")
    (dq3, dk3, dv3, dbias), landed = _attn_bwd(q3, kv3, o3, lse3, do.reshape(B, S, D), bias, "attn_bwd",
                                               comm=scatter(["w_up1", "w_o"]))
    slots.update(zip(["w_up1", "w_o"], landed))
    g_rel = _bias_grad(dbias, bmap, "rel_bias_grad")[:, :REL_BUCKETS].T
    dq = dq3.reshape(T, D)
    dkv = jnp.concatenate([dk3, dv3], axis=-1).reshape(T, 2 * D)
    dxn3 = _mm_nt(dq, W["w_q"], BF16, "q_proj_dx")
    big["w_q"] = _mm_tn(xn3, dq, "q_proj_dw")
    dh2, g_bn = _rmsnorm_bwd(h2, small["b_norm"][0], dxn3, dh3, "b_norm_bwd")
    dkvn = _mm_nt(dkv, W["w_kv"], BF16, "kv_proj_dx")
    big["w_kv"] = _mm_tn(kvn, dkv, "kv_proj_dw")
    dh2, g_kvn = _rmsnorm_bwd(h2, small["kv_norm"], dkvn, dh2, "kv_norm_bwd")
    dh1, g_cw0, g_cb0, g_fn0 = ffn_bwd(0, dh2, h1, (xn1, u0, a0), ["w_q", "w_kv"], True)
    dz = _mm_nt(dh1, W["w_out"], BF16, "a_out_dx")
    big["w_out"] = _mm_tn(z, dh1, "a_out_dw")
    dp, g_aconv, (slots["w_out"],) = _gate_a_bwd(p, small["a_conv"][0], dz, S, "a_gate_bwd", comm=scatter(["w_out"]))
    big["w_in"] = _mm_tn(xn0, dp, "a_in_dw")
    dxn0, (slots["w_in"],) = _mm_nt(dp, W["w_in"], BF16, "a_in_dx", comm=scatter(["w_in"]))
    dx, g_an = _rmsnorm_bwd(x2, small["a_norm"][0], dxn0, dh1, "a_norm_bwd")

    small_g = {"a_norm": g_an[None], "a_conv": g_aconv[None], "kv_norm": g_kvn, "b_norm": g_bn[None],
               "rel_bias": g_rel, "ffn_norm": jnp.stack([g_fn0, g_fn1]), "ffn_conv": jnp.stack([g_cw0, g_cw1]),
               "ffn_conv_b": jnp.stack([g_cb0, g_cb1]), "final_norm": g_final}
    return loss, dx.reshape(B, S, D), big, slots, small_g


BIG = ("w_in", "w_out", "w_kv", "w_q", "w_o", "w_up0", "w_up1", "w_down0", "w_down1")
SMALL = ("a_norm", "a_conv", "kv_norm", "b_norm", "rel_bias", "ffn_norm", "ffn_conv", "ffn_conv_b", "final_norm")
SMALL_SHARDED = ("a_norm", "a_conv", "ffn_conv")
WEIGHT_ORDER = ("a_norm", "a_w_in", "a_conv", "a_w_out", "kv_norm", "w_kv", "b_norm", "b_w_q", "b_w_o", "rel_bias",
                "ffn_norm", "ffn_w_up", "ffn_conv", "ffn_conv_b", "ffn_w_down", "final_norm")
BIG_OF = {"a_w_in": "w_in", "a_w_out": "w_out", "w_kv": "w_kv", "b_w_q": "w_q", "b_w_o": "w_o", "ffn_w_up": "w_up",
          "ffn_w_down": "w_down"}


def _as2d(a):
    return a.reshape(-1, a.shape[-1])


def kernel(x, a_norm, a_w_in, a_conv, a_w_out, kv_norm, w_kv, b_norm, b_w_q, b_w_o, rel_bias, ffn_norm, ffn_w_up, ffn_conv, ffn_conv_b, ffn_w_down, final_norm, loss_target, m_a_norm, m_a_w_in, m_a_conv, m_a_w_out, m_kv_norm, m_w_kv, m_b_norm, m_b_w_q, m_b_w_o, m_rel_bias, m_ffn_norm, m_ffn_w_up, m_ffn_conv, m_ffn_conv_b, m_ffn_w_down, m_final_norm, v_a_norm, v_a_w_in, v_a_conv, v_a_w_out, v_kv_norm, v_w_kv, v_b_norm, v_b_w_q, v_b_w_o, v_rel_bias, v_ffn_norm, v_ffn_w_up, v_ffn_conv, v_ffn_conv_b, v_ffn_w_down, v_final_norm):
    given = dict(a_norm=a_norm, a_w_in=a_w_in, a_conv=a_conv, a_w_out=a_w_out, kv_norm=kv_norm, w_kv=w_kv, b_norm=b_norm,
                 b_w_q=b_w_q, b_w_o=b_w_o, rel_bias=rel_bias, ffn_norm=ffn_norm, ffn_w_up=ffn_w_up, ffn_conv=ffn_conv,
                 ffn_conv_b=ffn_conv_b, ffn_w_down=ffn_w_down, final_norm=final_norm)
    mom_m = dict(a_norm=m_a_norm, a_w_in=m_a_w_in, a_conv=m_a_conv, a_w_out=m_a_w_out, kv_norm=m_kv_norm, w_kv=m_w_kv,
                 b_norm=m_b_norm, b_w_q=m_b_w_q, b_w_o=m_b_w_o, rel_bias=m_rel_bias, ffn_norm=m_ffn_norm,
                 ffn_w_up=m_ffn_w_up, ffn_conv=m_ffn_conv, ffn_conv_b=m_ffn_conv_b, ffn_w_down=m_ffn_w_down,
                 final_norm=m_final_norm)
    mom_v = dict(a_norm=v_a_norm, a_w_in=v_a_w_in, a_conv=v_a_conv, a_w_out=v_a_w_out, kv_norm=v_kv_norm, w_kv=v_w_kv,
                 b_norm=v_b_norm, b_w_q=v_b_w_q, b_w_o=v_b_w_o, rel_bias=v_rel_bias, ffn_norm=v_ffn_norm,
                 ffn_w_up=v_ffn_w_up, ffn_conv=v_ffn_conv, ffn_conv_b=v_ffn_conv_b, ffn_w_down=v_ffn_w_down,
                 final_norm=v_final_norm)

    shard = {"w_in": (a_w_in[0], 1), "w_out": (a_w_out[0], 0), "w_kv": (w_kv, 1), "w_q": (b_w_q[0], 0),
             "w_o": (b_w_o[0], 0), "w_up0": (ffn_w_up[0], 1), "w_up1": (ffn_w_up[1], 1),
             "w_down0": (ffn_w_down[0], 0), "w_down1": (ffn_w_down[1], 0)}
    blks = [shard[k][0].shape for k in BIG]
    axes = [shard[k][1] for k in BIG]

    small_sharded = [given[k] for k in SMALL_SHARDED]
    packed = _pack(small_sharded)
    first = ("w_in", "w_out")
    fulls, packed_all = _gather_weights([shard[k][0].astype(BF16) for k in first], [shard[k][1] for k in first], packed)
    W = dict(zip(first, fulls))
    later = {k: shard[k][0].astype(BF16) for k in BIG if k not in first}
    geom = {k: (shard[k][0].shape, shard[k][1]) for k in BIG}
    small = {k: given[k] for k in SMALL}
    per_shard = [_unpack(packed_all[j], [a.shape for a in small_sharded]) for j in range(N_SHARDS)]
    for i, k in enumerate(SMALL_SHARDED):
        small[k] = jnp.concatenate([per_shard[j][i] for j in range(N_SHARDS)], axis=-1)

    loss, grad_x, big_g, slots, small_g = _local_step(x, loss_target, W, later, geom, small)
    loss = lax.psum(loss, MESH_AXES)

    small_shapes = [small_g[k].shape for k in SMALL]
    _, small_slots = _scatter_grads([], [], [], _pack([small_g[k] for k in SMALL]))
    me2, c_ = 2 * lax.axis_index("x") + lax.axis_index("y"), lax.axis_index("c")
    halves = []
    for k, b, ax in zip(BIG, blks, axes):
        own = lax.dynamic_slice(big_g[k], _piece_start(b, ax, me2, c_), (b[0] // 2, b[1]))
        halves.append(_sum_slots(slots[k], own, f"sum_{k}"))
    reduced = dict(zip(BIG, _swap_halves(halves, blks)))
    reduced["w_up"] = jnp.stack([reduced["w_up0"], reduced["w_up1"]])
    reduced["w_down"] = jnp.stack([reduced["w_down0"], reduced["w_down1"]])
    small_sum = _sum_slots(small_slots, None, "sum_small")
    small_red = dict(zip(SMALL, _unpack(small_sum, small_shapes)))
    j = 2 * lax.axis_index("x") + lax.axis_index("y")
    for k in SMALL_SHARDED:
        w = given[k].shape[-1]
        small_red[k] = lax.dynamic_slice_in_dim(small_red[k], j * w, w, axis=small_red[k].ndim - 1)

    grads, deltas, new_m, new_v = {}, {}, {}, {}
    for name in WEIGHT_ORDER:
        if name in BIG_OF:
            g = reduced[BIG_OF[name]].reshape(given[name].shape)
            d, nm, nv = _adamw(_as2d(given[name]), _as2d(g), _as2d(mom_m[name]), _as2d(mom_v[name]), f"adamw_{name}")
            grads[name] = g
            deltas[name], new_m[name], new_v[name] = (t.reshape(given[name].shape) for t in (d, nm, nv))
    small_names = [n for n in WEIGHT_ORDER if n not in BIG_OF]
    for n in small_names:
        grads[n] = small_red[n].reshape(given[n].shape)
    sw, sg, sm, sv = (_pack([d[n] for n in small_names]) for d in (given, grads, mom_m, mom_v))
    d, nm, nv = _adamw(sw, sg, sm, sv, "adamw_small")
    shapes = [given[n].shape for n in small_names]
    for n, a, b_, c_ in zip(small_names, _unpack(d, shapes), _unpack(nm, shapes), _unpack(nv, shapes)):
        deltas[n], new_m[n], new_v[n] = a, b_, c_

    return (loss, grad_x, *[grads[n] for n in WEIGHT_ORDER], *[deltas[n] for n in WEIGHT_ORDER],
            *[new_m[n] for n in WEIGHT_ORDER], *[new_v[n] for n in WEIGHT_ORDER])
```

```python
import functools
import math

import numpy as np

import jax
import jax.numpy as jnp
from jax import lax
from jax.experimental import pallas as pl
from jax.experimental.pallas import tpu as pltpu

F32 = jnp.float32
BF16 = jnp.bfloat16

RMS_EPS = 1e-6
HEAD_DIM = 64
ATT_BLOCK = 128
DILATED_BRANCHES = ((128, 1), (512, 4), (2048, 16))
REL_BUCKETS = 32
REL_MAX_DISTANCE = 2048
MASKED_LOGIT = -1e30
ATTN_FWD_UNROLL = 4
ATTN_BWD_UNROLL = 4

ADAM_LR = 0.001
ADAM_B1 = 0.9
ADAM_B2 = 0.999
ADAM_EPS = 1e-08
ADAM_WD = 0.01
ADAM_STEP = 10

LANES = 128
SUBLANES_F32 = 8
SUBLANES_BF16 = 16
VMEM_LIMIT_BYTES = 56 * 1024 * 1024

MESH_AXES = ("x", "y", "c")
N_SHARDS = 4
N_DEVICES = 8
ANY = pl.BlockSpec(memory_space=pl.ANY)


def _tile(n, pref, mult):
    best = None
    for t in range(mult, min(n, pref) + 1, mult):
        if n % t == 0:
            best = t
    if best is None:
        raise ValueError(f"no tile for {n} (multiple of {mult}, at most {pref})")
    return best


def _params(*sem):
    return pltpu.CompilerParams(dimension_semantics=sem, vmem_limit_bytes=VMEM_LIMIT_BYTES)


class _Comm:
    def __init__(self, ins, outs, aliases, n_sems, copies):
        self.ins, self.outs, self.aliases, self.n_sems, self.copies = list(ins), list(outs), dict(aliases), n_sems, copies

    @staticmethod
    def join(parts):
        parts = [p for p in parts if p is not None]
        ins, outs, aliases, offs, n_sems = [], [], {}, [], 0
        for p in parts:
            offs.append((len(ins), len(outs), n_sems))
            aliases.update({len(ins) + i: len(outs) + o for i, o in p.aliases.items()})
            ins += p.ins
            outs += p.outs
            n_sems += p.n_sems

        def copies(in_refs, out_refs, sem):
            sends, recvs = [], []
            for p, (i0, o0, s0) in zip(parts, offs):
                s, r = p.copies(in_refs[i0:i0 + len(p.ins)], out_refs[o0:o0 + len(p.outs)],
                                lambda k, s0=s0: sem(s0 + k))
                sends += s
                recvs += r
            return sends, recvs

        return _Comm(ins, outs, aliases, n_sems, copies)


def _pallas(body, *, comm=None, name, out_shape, grid=(), in_specs=(), out_specs=(), scratch_shapes=(),
            compiler_params=None):
    if comm is None:
        return pl.pallas_call(body, name=name, out_shape=out_shape, grid=grid, in_specs=in_specs, out_specs=out_specs,
                              scratch_shapes=scratch_shapes, compiler_params=compiler_params)
    single = not isinstance(out_shape, (tuple, list))
    outs = (out_shape,) if single else tuple(out_shape)
    o_specs = (out_specs,) if single else tuple(out_specs)
    n_in, n_cin, n_out, n_cout, n_scr = len(in_specs), len(comm.ins), len(outs), len(comm.outs), len(scratch_shapes)

    def carried(*refs):
        base_in, c_in = refs[:n_in], refs[n_in:n_in + n_cin]
        o0 = n_in + n_cin
        base_out, c_out = refs[o0:o0 + n_out], refs[o0 + n_out:o0 + n_out + n_cout]
        s0 = o0 + n_out + n_cout
        base_scr, (send_sems, recv_sems) = refs[s0:s0 + n_scr], refs[s0 + n_scr:]
        sem = lambda k: (send_sems.at[k], recv_sems.at[k])
        first = functools.reduce(jnp.logical_and, [pl.program_id(a) == 0 for a in range(len(grid))])
        last = functools.reduce(jnp.logical_and, [pl.program_id(a) == n - 1 for a, n in enumerate(grid)])

        @pl.when(first)
        def _():
            for cp in comm.copies(c_in, c_out, sem)[0]:
                cp.start()

        body(*base_in, *base_out, *base_scr)

        @pl.when(last)
        def _():
            sends, recvs = comm.copies(c_in, c_out, sem)
            for cp in recvs:
                cp.wait_recv()
            for cp in sends:
                cp.wait_send()

    call = pl.pallas_call(
        carried, name=name, out_shape=outs + tuple(comm.outs), grid=grid,
        in_specs=list(in_specs) + [ANY] * n_cin, out_specs=o_specs + (ANY,) * n_cout,
        scratch_shapes=list(scratch_shapes) + [pltpu.SemaphoreType.DMA((comm.n_sems,))] * 2,
        input_output_aliases={n_in + i: n_out + o for i, o in comm.aliases.items()},
        compiler_params=_params(*(["arbitrary"] * len(grid))))

    def run(*args):
        res = call(*args, *comm.ins)
        base = res[0] if single else tuple(res[:n_out])
        return base, list(res[n_out:])

    return run


def _rmsnorm_fwd(x, g, name, comm=None):
    T, D = x.shape
    tm = _tile(T, 512, SUBLANES_BF16)

    def body(x_ref, g_ref, o_ref):
        xf = x_ref[...]
        r = lax.rsqrt(jnp.mean(xf * xf, axis=-1, keepdims=True) + RMS_EPS)
        o_ref[...] = ((xf * r) * g_ref[...]).astype(o_ref.dtype)

    return _pallas(
        body, comm=comm, name=name, out_shape=jax.ShapeDtypeStruct((T, D), BF16), grid=(T // tm,),
        in_specs=[pl.BlockSpec((tm, D), lambda i: (i, 0)), pl.BlockSpec((1, D), lambda i: (0, 0))],
        out_specs=pl.BlockSpec((tm, D), lambda i: (i, 0)),
        compiler_params=_params("parallel"),
    )(x, g.reshape(1, D))


def _rmsnorm_bwd(x, g, dxn, dres, name):
    T, D = x.shape
    tm = _tile(T, 512, SUBLANES_BF16)

    def body(x_ref, g_ref, dxn_ref, dres_ref, dx_ref, dg_ref):
        @pl.when(pl.program_id(0) == 0)
        def _():
            dg_ref[...] = jnp.zeros_like(dg_ref)

        xf = x_ref[...]
        r = lax.rsqrt(jnp.mean(xf * xf, axis=-1, keepdims=True) + RMS_EPS)
        xhat = xf * r
        dy = dxn_ref[...].astype(F32)
        dg_ref[0:1, :] += jnp.sum(dy * xhat, axis=0, keepdims=True)
        t = dy * g_ref[...]
        dx_ref[...] = dres_ref[...] + r * (t - xhat * jnp.mean(t * xhat, axis=-1, keepdims=True))

    row = pl.BlockSpec((tm, D), lambda i: (i, 0))
    dx, dg = pl.pallas_call(
        body, name=name,
        out_shape=(jax.ShapeDtypeStruct((T, D), F32), jax.ShapeDtypeStruct((SUBLANES_F32, D), F32)),
        grid=(T // tm,),
        in_specs=[row, pl.BlockSpec((1, D), lambda i: (0, 0)), row, row],
        out_specs=(row, pl.BlockSpec((SUBLANES_F32, D), lambda i: (0, 0))),
        compiler_params=_params("arbitrary"),
    )(x, g.reshape(1, D), dxn, dres)
    return dx, dg[0]


def _loss_head(h, g, target, name):
    T, D = h.shape
    tm = _tile(T, 512, SUBLANES_F32)

    def body(h_ref, g_ref, t_ref, dh_ref, acc_ref):
        @pl.when(pl.program_id(0) == 0)
        def _():
            acc_ref[...] = jnp.zeros_like(acc_ref)

        xf = h_ref[...]
        r = lax.rsqrt(jnp.mean(xf * xf, axis=-1, keepdims=True) + RMS_EPS)
        xhat = xf * r
        err = xhat * g_ref[...] - t_ref[...]
        dy = err * (1.0 / D)
        acc_ref[0:1, :] += jnp.sum(dy * xhat, axis=0, keepdims=True)
        acc_ref[1:2, :] += jnp.sum(err * err, axis=0, keepdims=True)
        t = dy * g_ref[...]
        dh_ref[...] = r * (t - xhat * jnp.mean(t * xhat, axis=-1, keepdims=True))

    row = pl.BlockSpec((tm, D), lambda i: (i, 0))
    dh, acc = pl.pallas_call(
        body, name=name,
        out_shape=(jax.ShapeDtypeStruct((T, D), F32), jax.ShapeDtypeStruct((SUBLANES_F32, D), F32)),
        grid=(T // tm,),
        in_specs=[row, pl.BlockSpec((1, D), lambda i: (0, 0)), row],
        out_specs=(row, pl.BlockSpec((SUBLANES_F32, D), lambda i: (0, 0))),
        compiler_params=_params("arbitrary"),
    )(h, g.reshape(1, D), target)
    return acc[1], dh, acc[0]


def _mm_nn(a, b, res, out_dtype, name, comm=None):
    T, K = a.shape
    N = b.shape[1]
    tm = _tile(T, 512, SUBLANES_BF16)
    tn = _tile(N, 3072, LANES)

    def body(a_ref, b_ref, *rest):
        o_ref = rest[-1]
        acc = jnp.dot(a_ref[...].astype(BF16), b_ref[...], preferred_element_type=F32)
        if res is not None:
            acc = acc + rest[0][...]
        o_ref[...] = acc.astype(o_ref.dtype)

    in_specs = [pl.BlockSpec((tm, K), lambda j, i: (i, 0)), pl.BlockSpec((K, tn), lambda j, i: (0, j))]
    args = [a, b]
    if res is not None:
        in_specs.append(pl.BlockSpec((tm, tn), lambda j, i: (i, j)))
        args.append(res)
    return _pallas(
        body, comm=comm, name=name, out_shape=jax.ShapeDtypeStruct((T, N), out_dtype), grid=(N // tn, T // tm),
        in_specs=in_specs, out_specs=pl.BlockSpec((tm, tn), lambda j, i: (i, j)),
        compiler_params=_params("parallel", "parallel"),
    )(*args)


def _mm_nt(dy, b, out_dtype, name, comm=None):
    dys = list(dy) if isinstance(dy, (list, tuple)) else [dy]
    T, n_each = dys[0].shape
    K = b.shape[0]
    tm = _tile(T, 1024, SUBLANES_BF16)
    tk = _tile(K, 1536, LANES)
    tn = _tile(n_each, 2816, LANES)
    per = n_each // tn
    n_steps = per * len(dys)

    def body(*refs):
        dy_refs, (b_ref, o_ref, acc_ref) = refs[:len(dys)], refs[len(dys):]
        n = pl.program_id(2)

        @pl.when(n == 0)
        def _():
            acc_ref[...] = jnp.zeros_like(acc_ref)

        for p, dy_ref in enumerate(dy_refs):
            @pl.when(jnp.logical_and(n >= p * per, n < (p + 1) * per))
            def _(dy_ref=dy_ref):
                acc_ref[...] += lax.dot_general(dy_ref[...].astype(BF16), b_ref[...], (((1,), (1,)), ((), ())),
                                                preferred_element_type=F32)

        @pl.when(n == n_steps - 1)
        def _():
            o_ref[...] = acc_ref[...].astype(o_ref.dtype)

    dy_specs = [pl.BlockSpec((tm, tn), lambda i, k, n, p=p: (i, jnp.clip(n - p * per, 0, per - 1))) for p in range(len(dys))]
    return _pallas(
        body, comm=comm, name=name, out_shape=jax.ShapeDtypeStruct((T, K), out_dtype), grid=(T // tm, K // tk, n_steps),
        in_specs=dy_specs + [pl.BlockSpec((tk, tn), lambda i, k, n: (k, n))],
        out_specs=pl.BlockSpec((tm, tk), lambda i, k, n: (i, k)),
        scratch_shapes=[pltpu.VMEM((tm, tk), F32)],
        compiler_params=_params("parallel", "parallel", "arbitrary"),
    )(*dys, b)


def _mm_tn(a, dy, name):
    T, K = a.shape
    N = dy.shape[1]
    tt = _tile(T, 2048, SUBLANES_BF16)
    tk = _tile(K, 1536, LANES)
    tn = _tile(N, 1536, LANES)
    t_steps = T // tt

    def body(a_ref, dy_ref, o_ref, acc_ref):
        t = pl.program_id(2)

        @pl.when(t == 0)
        def _():
            acc_ref[...] = jnp.zeros_like(acc_ref)

        acc_ref[...] += lax.dot_general(a_ref[...].astype(BF16), dy_ref[...].astype(BF16),
                                        (((0,), (0,)), ((), ())), preferred_element_type=F32)

        @pl.when(t == t_steps - 1)
        def _():
            o_ref[...] = acc_ref[...].astype(o_ref.dtype)

    return pl.pallas_call(
        body, name=name, out_shape=jax.ShapeDtypeStruct((K, N), BF16), grid=(K // tk, N // tn, t_steps),
        in_specs=[pl.BlockSpec((tt, tk), lambda k, n, t: (t, k)), pl.BlockSpec((tt, tn), lambda k, n, t: (t, n))],
        out_specs=pl.BlockSpec((tk, tn), lambda k, n, t: (k, n)),
        scratch_shapes=[pltpu.VMEM((tk, tn), F32)],
        compiler_params=_params("parallel", "parallel", "arbitrary"),
    )(a, dy)


def _rows_before(halo, cur, k):
    h = halo.shape[0]
    return pltpu.roll(jnp.concatenate([halo, cur], axis=0), k, 0)[h:]


def _rows_after(cur, halo, k):
    n = cur.shape[0]
    total = n + halo.shape[0]
    return pltpu.roll(jnp.concatenate([cur, halo], axis=0), total - k, 0)[:n]


def _halo_specs(tm, width, n_rows):
    per = tm // SUBLANES_BF16
    last = n_rows // SUBLANES_BF16 - 1
    prev = pl.BlockSpec((SUBLANES_BF16, width), lambda i: (jnp.maximum(i * per - 1, 0), 0))
    nxt = pl.BlockSpec((SUBLANES_BF16, width), lambda i: (jnp.minimum((i + 1) * per, last), 0))
    return prev, nxt


def _gate_a_fwd(p, cw, seq, name, comm=None):
    T, D3 = p.shape
    D = D3 // 3
    tm = _tile(seq, 512, SUBLANES_BF16)
    cc = _tile(D, 256, LANES)
    prev, _ = _halo_specs(tm, D3, T)

    def body(p_ref, ph_ref, cw_ref, z_ref):
        at_start = (pl.program_id(0) * tm) % seq == 0
        for c0 in range(0, D, cc):
            b = p_ref[:, c0:c0 + cc].astype(F32)
            u = p_ref[:, D + c0:D + c0 + cc].astype(F32) * p_ref[:, 2 * D + c0:2 * D + c0 + cc].astype(F32)
            uh = ph_ref[:, D + c0:D + c0 + cc].astype(F32) * ph_ref[:, 2 * D + c0:2 * D + c0 + cc].astype(F32)
            uh = jnp.where(at_start, 0.0, uh)
            w = cw_ref[:, c0:c0 + cc]
            cv = _rows_before(uh, u, 2) * w[0:1] + _rows_before(uh, u, 1) * w[1:2] + u * w[2:3]
            z_ref[:, c0:c0 + cc] = (b * cv).astype(z_ref.dtype)

    return _pallas(
        body, comm=comm, name=name, out_shape=jax.ShapeDtypeStruct((T, D), BF16), grid=(T // tm,),
        in_specs=[pl.BlockSpec((tm, D3), lambda i: (i, 0)), prev, pl.BlockSpec((3, D), lambda i: (0, 0))],
        out_specs=pl.BlockSpec((tm, D), lambda i: (i, 0)),
        compiler_params=_params("parallel"),
    )(p, p, cw)


def _gate_a_bwd(p, cw, dz, seq, name, comm=None):
    T, D3 = p.shape
    D = D3 // 3
    tm = _tile(seq, 512, SUBLANES_BF16)
    cc = _tile(D, 256, LANES)
    p_prev, p_next = _halo_specs(tm, D3, T)
    _, dz_next = _halo_specs(tm, D, T)

    def body(p_ref, pp_ref, pn_ref, dz_ref, dzn_ref, cw_ref, dp_ref, dcw_ref):
        i = pl.program_id(0)

        @pl.when(i == 0)
        def _():
            dcw_ref[...] = jnp.zeros_like(dcw_ref)

        at_start = (i * tm) % seq == 0
        at_end = ((i + 1) * tm) % seq == 0
        for c0 in range(0, D, cc):
            cb, cc_, ch = slice(c0, c0 + cc), slice(D + c0, D + c0 + cc), slice(2 * D + c0, 2 * D + c0 + cc)
            b = p_ref[:, cb].astype(F32)
            c = p_ref[:, cc_].astype(F32)
            hh = p_ref[:, ch].astype(F32)
            u = c * hh
            uh = jnp.where(at_start, 0.0, pp_ref[:, cc_].astype(F32) * pp_ref[:, ch].astype(F32))
            w = cw_ref[:, cb]
            u1 = _rows_before(uh, u, 1)
            u2 = _rows_before(uh, u, 2)
            cv = u2 * w[0:1] + u1 * w[1:2] + u * w[2:3]
            dz_t = dz_ref[:, cb].astype(F32)
            dcv = dz_t * b
            dcvn = jnp.where(at_end, 0.0, dzn_ref[:, cb].astype(F32) * pn_ref[:, cb].astype(F32))
            du = dcv * w[2:3] + _rows_after(dcv, dcvn, 1) * w[1:2] + _rows_after(dcv, dcvn, 2) * w[0:1]
            dp_ref[:, cb] = (dz_t * cv).astype(dp_ref.dtype)
            dp_ref[:, cc_] = (du * hh).astype(dp_ref.dtype)
            dp_ref[:, ch] = (du * c).astype(dp_ref.dtype)
            dcw_ref[0:1, cb] += jnp.sum(dcv * u2, axis=0, keepdims=True)
            dcw_ref[1:2, cb] += jnp.sum(dcv * u1, axis=0, keepdims=True)
            dcw_ref[2:3, cb] += jnp.sum(dcv * u, axis=0, keepdims=True)

    res = _pallas(
        body, comm=comm, name=name,
        out_shape=(jax.ShapeDtypeStruct((T, D3), BF16), jax.ShapeDtypeStruct((SUBLANES_F32, D), F32)),
        grid=(T // tm,),
        in_specs=[pl.BlockSpec((tm, D3), lambda i: (i, 0)), p_prev, p_next,
                  pl.BlockSpec((tm, D), lambda i: (i, 0)), dz_next, pl.BlockSpec((3, D), lambda i: (0, 0))],
        out_specs=(pl.BlockSpec((tm, D3), lambda i: (i, 0)), pl.BlockSpec((SUBLANES_F32, D), lambda i: (0, 0))),
        compiler_params=_params("arbitrary"),
    )(p, p, p, dz, dz, cw)
    if comm is None:
        return res[0], res[1][0:3]
    return res[0][0], res[0][1][0:3], res[1]


def _ffn_gate_fwd(u, cw, cb, seq, name, comm=None):
    T, F2 = u.shape
    F = F2 // 2
    tm = _tile(seq, 256, SUBLANES_BF16)
    cc = _tile(F, 256, LANES)
    prev, _ = _halo_specs(tm, F2, T)

    def body(u_ref, uh_ref, cw_ref, cb_ref, a_ref):
        at_start = (pl.program_id(0) * tm) % seq == 0

        def conv(c0):
            cols = slice(c0, c0 + cc)
            cur = u_ref[:, cols].astype(F32)
            halo = jnp.where(at_start, 0.0, uh_ref[:, cols].astype(F32))
            w = cw_ref[:, cols]
            return (_rows_before(halo, cur, 2) * w[0:1] + _rows_before(halo, cur, 1) * w[1:2] + cur * w[2:3]
                    + cb_ref[:, cols])

        for c0 in range(0, F, cc):
            g = conv(c0)
            up = conv(F + c0)
            a_ref[:, c0:c0 + cc] = ((g * jax.nn.sigmoid(g)) * up).astype(a_ref.dtype)

    return _pallas(
        body, comm=comm, name=name, out_shape=jax.ShapeDtypeStruct((T, F), BF16), grid=(T // tm,),
        in_specs=[pl.BlockSpec((tm, F2), lambda i: (i, 0)), prev,
                  pl.BlockSpec((3, F2), lambda i: (0, 0)), pl.BlockSpec((1, F2), lambda i: (0, 0))],
        out_specs=pl.BlockSpec((tm, F), lambda i: (i, 0)),
        compiler_params=_params("parallel"),
    )(u, u, cw, cb.reshape(1, F2))


def _ffn_gate_bwd(u, cw, cb, da, seq, name, comm=None):
    T, F2 = u.shape
    F = F2 // 2
    H = SUBLANES_BF16
    tm = _tile(seq, 256, H)
    cc = _tile(F, 256, LANES)
    u_prev, u_next = _halo_specs(tm, F2, T)
    _, da_next = _halo_specs(tm, F, T)

    def body(u_ref, up_ref, un_ref, da_ref, dan_ref, cw_ref, cb_ref, du_ref, acc_ref):
        i = pl.program_id(0)

        @pl.when(i == 0)
        def _():
            acc_ref[...] = jnp.zeros_like(acc_ref)

        at_start = (i * tm) % seq == 0
        at_end = ((i + 1) * tm) % seq == 0

        def conv_ext(cols):
            ext = jnp.concatenate([jnp.where(at_start, 0.0, up_ref[:, cols].astype(F32)),
                                   u_ref[:, cols].astype(F32), un_ref[:, cols].astype(F32)], axis=0)
            w = cw_ref[:, cols]
            e1 = pltpu.roll(ext, 1, 0)
            e2 = pltpu.roll(ext, 2, 0)
            out = (e2 * w[0:1] + e1 * w[1:2] + ext * w[2:3] + cb_ref[:, cols])[H:]
            return out, ext, e1, e2

        def back(d2, cols, ext, e1, e2):
            w = cw_ref[:, cols]
            n = tm + H
            d1n = pltpu.roll(d2, n - 1, 0)[:tm]
            d2n = pltpu.roll(d2, n - 2, 0)[:tm]
            d_t = d2[:tm]
            du_ref[:, cols] = (d_t * w[2:3] + d1n * w[1:2] + d2n * w[0:1]).astype(du_ref.dtype)
            acc_ref[0:1, cols] += jnp.sum(d_t * e2[H:H + tm], axis=0, keepdims=True)
            acc_ref[1:2, cols] += jnp.sum(d_t * e1[H:H + tm], axis=0, keepdims=True)
            acc_ref[2:3, cols] += jnp.sum(d_t * ext[H:H + tm], axis=0, keepdims=True)
            acc_ref[3:4, cols] += jnp.sum(d_t, axis=0, keepdims=True)

        for c0 in range(0, F, cc):
            gc, uc = slice(c0, c0 + cc), slice(F + c0, F + c0 + cc)
            g, g_ext, g_e1, g_e2 = conv_ext(gc)
            up, up_ext, up_e1, up_e2 = conv_ext(uc)
            da_ext = jnp.concatenate([da_ref[:, gc].astype(F32),
                                      jnp.where(at_end, 0.0, dan_ref[:, gc].astype(F32))], axis=0)
            sg = jax.nn.sigmoid(g)
            d_up = da_ext * (g * sg)
            d_g = da_ext * up * (sg * (1.0 + g * (1.0 - sg)))
            back(d_g, gc, g_ext, g_e1, g_e2)
            back(d_up, uc, up_ext, up_e1, up_e2)

    res = _pallas(
        body, comm=comm, name=name,
        out_shape=(jax.ShapeDtypeStruct((T, F2), BF16), jax.ShapeDtypeStruct((SUBLANES_F32, F2), F32)),
        grid=(T // tm,),
        in_specs=[pl.BlockSpec((tm, F2), lambda i: (i, 0)), u_prev, u_next,
                  pl.BlockSpec((tm, F), lambda i: (i, 0)), da_next,
                  pl.BlockSpec((3, F2), lambda i: (0, 0)), pl.BlockSpec((1, F2), lambda i: (0, 0))],
        out_specs=(pl.BlockSpec((tm, F2), lambda i: (i, 0)), pl.BlockSpec((SUBLANES_F32, F2), lambda i: (0, 0))),
        compiler_params=_params("arbitrary"),
    )(u, u, u, da, da, cw, cb.reshape(1, F2))
    (du, acc), landed = res if comm is not None else (res, None)
    return (du, acc[0:3], acc[3]) if comm is None else (du, acc[0:3], acc[3], landed)


def _bucket_map():
    P = ATT_BLOCK
    qi = np.arange(P, dtype=np.int64)[:, None]
    kc = np.arange(2 * P, dtype=np.int64)[None, :]
    delta = qi + P - kc
    maps = []
    max_exact = REL_BUCKETS // 2
    for window, dilation in DILATED_BRANCHES:
        band = (delta >= 0) & (delta <= window // dilation)
        n = np.maximum(delta * dilation, 0)
        nf = np.maximum(n, max_exact).astype(np.float32)
        large = max_exact + (np.log(nf / np.float32(max_exact)) / np.float32(math.log(REL_MAX_DISTANCE / max_exact))
                             * np.float32(REL_BUCKETS - max_exact)).astype(np.int32)
        large = np.minimum(large, REL_BUCKETS - 1)
        bucket = np.where(n < max_exact, n, large)
        maps.append(np.where(band, bucket, -1).astype(np.int32))
    return np.stack(maps)


def _bias_tables(rel_bias, bmap, name):
    n_pairs = rel_bias.shape[1] // 2
    nbr, P, P2 = bmap.shape

    def body(rb_ref, bm_ref, o_ref):
        pair = pl.program_id(0)
        in_seq = lax.broadcasted_iota(jnp.int32, (P, P2), 1) >= P
        for br in range(nbr):
            bm = bm_ref[br]
            for hh in range(2):
                acc = jnp.full((P, P2), MASKED_LOGIT, F32)
                for b in range(REL_BUCKETS):
                    acc = jnp.where(bm == b, rb_ref[b, 2 * pair + hh], acc)
                o_ref[br, 0, 0, hh * P:(hh + 1) * P, :] = acc
                o_ref[br, 0, 1, hh * P:(hh + 1) * P, :] = jnp.where(in_seq, acc, MASKED_LOGIT)

    return pl.pallas_call(
        body, name=name, out_shape=jax.ShapeDtypeStruct((nbr, n_pairs, 2, 2 * P, P2), F32), grid=(n_pairs,),
        in_specs=[pl.BlockSpec(memory_space=pltpu.SMEM), pl.BlockSpec((nbr, P, P2), lambda h: (0, 0, 0))],
        out_specs=pl.BlockSpec((nbr, 1, 2, 2 * P, P2), lambda h: (0, h, 0, 0, 0)),
        compiler_params=_params("parallel"),
    )(rel_bias, bmap)


def _bias_grad(dbias, bmap, name):
    nbr, n_pairs, _, P2 = dbias.shape
    P = P2 // 2

    def body(db_ref, bm_ref, o_ref):
        lane = lax.broadcasted_iota(jnp.int32, (1, LANES), 1)
        for hh in range(2):
            row = jnp.zeros((1, LANES), F32)
            for br in range(nbr):
                bm = bm_ref[br]
                d = db_ref[br, 0, hh * P:(hh + 1) * P, :]
                for b in range(REL_BUCKETS):
                    hit = jnp.sum(jnp.where(bm == b, d, 0.0), axis=1, keepdims=True)
                    row = row + jnp.where(lane == b, jnp.sum(hit, axis=0, keepdims=True), 0.0)
            o_ref[hh] = row

    return pl.pallas_call(
        body, name=name, out_shape=jax.ShapeDtypeStruct((2 * n_pairs, 1, LANES), F32), grid=(n_pairs,),
        in_specs=[pl.BlockSpec((nbr, 1, P2, P2), lambda h: (0, h, 0, 0)), pl.BlockSpec((nbr, P, P2), lambda h: (0, 0, 0))],
        out_specs=pl.BlockSpec((2, 1, LANES), lambda h: (h, 0, 0)),
        compiler_params=_params("parallel"),
    )(dbias, bmap)[:, 0, :]


def _rows(start, dilation):
    if dilation == 1:
        return pl.ds(pl.multiple_of(start, ATT_BLOCK), ATT_BLOCK)
    return pl.ds(start, ATT_BLOCK, stride=dilation)


def _for_each_block(seq, unroll, fn):
    P = ATT_BLOCK
    for br, (_, d) in enumerate(DILATED_BRANCHES):
        nb = seq // d // P
        total = nb * d
        u = unroll if total % unroll == 0 else 1

        def some(i, carry, br=br, d=d, nb=nb, u=u):
            blocks = []
            for k in range(u):
                idx = i * u + k
                r, j = idx // nb, idx % nb
                blocks.append((r + j * (d * P), r + jnp.maximum(j - 1, 0) * (d * P), jnp.where(j == 0, 1, 0)))
            fn(br, d, blocks)
            return carry

        lax.fori_loop(0, total // u, some, 0)


def _stack_heads(x, head0):
    return jnp.concatenate([jnp.where(head0, x, 0.0), jnp.where(head0, 0.0, x)], axis=0).astype(BF16)


def _window(ref, start, prev, dilation):
    return jnp.concatenate([ref[_rows(prev, dilation), :], ref[_rows(start, dilation), :]], axis=0).astype(BF16)


def _attn_fwd(q, kv, bias, name):
    B, S, D = q.shape
    P = ATT_BLOCK
    n_pairs = D // LANES
    nbr = len(DILATED_BRANCHES)
    scale = HEAD_DIM ** -0.5

    def body(q_ref, k_ref, v_ref, bias_ref, o_ref, lse_ref, *stats):
        m_s, l_s, acc_s = stats[0:nbr], stats[nbr:2 * nbr], stats[2 * nbr:3 * nbr]
        head0 = lax.broadcasted_iota(jnp.int32, (P, LANES), 1) < HEAD_DIM

        def block(br, d, blocks):
            s = [lax.dot_general(_stack_heads(q_ref[_rows(start, d), :] * scale, head0),
                                 _window(k_ref, start, prev, d), (((1,), (1,)), ((), ())),
                                 preferred_element_type=F32) + bias_ref[br, 0, first]
                 for start, prev, first in blocks]
            m = [jnp.max(x, axis=-1, keepdims=True) for x in s]
            p = [jnp.exp(x - y) for x, y in zip(s, m)]
            l = [jnp.sum(x, axis=-1, keepdims=True) for x in p]
            pv = [jnp.dot(x.astype(BF16), _window(v_ref, start, prev, d), preferred_element_type=F32)
                  for x, (start, prev, _) in zip(p, blocks)]
            for k, (start, _, _) in enumerate(blocks):
                rows = _rows(start, d)
                m_s[br][rows, :] = jnp.where(head0, m[k][:P], m[k][P:])
                l_s[br][rows, :] = jnp.where(head0, l[k][:P], l[k][P:])
                acc_s[br][rows, :] = jnp.where(head0, pv[k][:P], pv[k][P:])

        _for_each_block(S, ATTN_FWD_UNROLL, block)

        chunk = _tile(S, 256, SUBLANES_F32)

        def merge(i, carry):
            rows = pl.ds(pl.multiple_of(i * chunk, chunk), chunk)
            ms = [m_s[br][rows, :] for br in range(nbr)]
            m = functools.reduce(jnp.maximum, ms)
            l = jnp.zeros((chunk, LANES), F32)
            acc = jnp.zeros((chunk, LANES), F32)
            for br in range(nbr):
                w = jnp.exp(ms[br] - m)
                l = l + w * l_s[br][rows, :]
                acc = acc + w * acc_s[br][rows, :]
            o_ref[rows, :] = acc / l
            lse_ref[rows, :] = m + jnp.log(l)
            return carry

        lax.fori_loop(0, S // chunk, merge, 0)

    slab = lambda col0: pl.BlockSpec((None, S, LANES), lambda b, h: (b, 0, col0 + h))
    return pl.pallas_call(
        body, name=name,
        out_shape=(jax.ShapeDtypeStruct((B, S, D), F32), jax.ShapeDtypeStruct((B, S, D), F32)),
        grid=(B, n_pairs),
        in_specs=[slab(0), slab(0), slab(n_pairs),
                  pl.BlockSpec((nbr, 1, 2, 2 * P, 2 * P), lambda b, h: (0, h, 0, 0, 0))],
        out_specs=(slab(0), slab(0)),
        scratch_shapes=[pltpu.VMEM((S, LANES), F32)] * (3 * nbr),
        compiler_params=_params("parallel", "parallel"),
    )(q, kv, kv, bias)


def _attn_bwd(q, kv, o, lse, do, bias, name, comm=None):
    B, S, D = q.shape
    P = ATT_BLOCK
    n_pairs = D // LANES
    nbr = len(DILATED_BRANCHES)
    scale = HEAD_DIM ** -0.5

    def body(q_ref, k_ref, v_ref, o_ref, lse_ref, do_ref, bias_ref, dq_ref, dk_ref, dv_ref, dbias_ref, delta_s):
        head0 = lax.broadcasted_iota(jnp.int32, (P, LANES), 1) < HEAD_DIM

        @pl.when(pl.program_id(1) == 0)
        def _():
            dbias_ref[...] = jnp.zeros_like(dbias_ref)

        chunk = _tile(S, 512, SUBLANES_F32)

        def prepare(i, carry):
            rows = pl.ds(pl.multiple_of(i * chunk, chunk), chunk)
            x = do_ref[rows, :] * o_ref[rows, :]
            h0 = lax.broadcasted_iota(jnp.int32, (chunk, LANES), 1) < HEAD_DIM
            d0 = jnp.sum(jnp.where(h0, x, 0.0), axis=-1, keepdims=True)
            d1 = jnp.sum(jnp.where(h0, 0.0, x), axis=-1, keepdims=True)
            delta_s[rows, :] = jnp.where(h0, d0, d1)
            zero = jnp.zeros((chunk, LANES), F32)
            dq_ref[rows, :] = zero
            dk_ref[rows, :] = zero
            dv_ref[rows, :] = zero
            return carry

        lax.fori_loop(0, S // chunk, prepare, 0)

        def per_head(x):
            return jnp.concatenate([x[:, 0:1], x[:, HEAD_DIM:HEAD_DIM + 1]], axis=0)

        nt = (((1,), (1,)), ((), ()))
        tn = (((0,), (0,)), ((), ()))

        def block(br, d, blocks):
            q2 = [_stack_heads(q_ref[_rows(start, d), :] * scale, head0) for start, _, _ in blocks]
            do2 = [_stack_heads(do_ref[_rows(start, d), :], head0) for start, _, _ in blocks]
            kb = [_window(k_ref, start, prev, d) for start, prev, _ in blocks]
            vb = [_window(v_ref, start, prev, d) for start, prev, _ in blocks]
            s = [lax.dot_general(a, b, nt, preferred_element_type=F32) + bias_ref[br, 0, first]
                 for a, b, (_, _, first) in zip(q2, kb, blocks)]
            dp = [lax.dot_general(a, b, nt, preferred_element_type=F32) for a, b in zip(do2, vb)]
            p = [jnp.exp(x - per_head(lse_ref[_rows(start, d), :])) for x, (start, _, _) in zip(s, blocks)]
            ds = [x * (y - per_head(delta_s[_rows(start, d), :])) for x, y, (start, _, _) in zip(p, dp, blocks)]
            for x in ds:
                dbias_ref[br, 0] += x
            ds16 = [x.astype(BF16) for x in ds]
            dq2 = [jnp.dot(a, b, preferred_element_type=F32) for a, b in zip(ds16, kb)]
            dk = [lax.dot_general(a, b, tn, preferred_element_type=F32) for a, b in zip(ds16, q2)]
            dv = [lax.dot_general(a.astype(BF16), b, tn, preferred_element_type=F32) for a, b in zip(p, do2)]
            for k, (start, prev, _) in enumerate(blocks):
                rows, prows = _rows(start, d), _rows(prev, d)
                dq_ref[rows, :] += jnp.where(head0, dq2[k][:P], dq2[k][P:]) * scale
                dk_ref[prows, :] += dk[k][:P]
                dk_ref[rows, :] += dk[k][P:]
                dv_ref[prows, :] += dv[k][:P]
                dv_ref[rows, :] += dv[k][P:]

        _for_each_block(S, ATTN_BWD_UNROLL, block)

    slab = lambda col0: pl.BlockSpec((None, S, LANES), lambda h, b: (b, 0, col0 + h))
    tab = pl.BlockSpec((nbr, 1, 2, 2 * P, 2 * P), lambda h, b: (0, h, 0, 0, 0))
    dtab = pl.BlockSpec((nbr, 1, 2 * P, 2 * P), lambda h, b: (0, h, 0, 0))
    shp = jax.ShapeDtypeStruct((B, S, D), F32)
    return _pallas(
        body, comm=comm, name=name,
        out_shape=(shp, shp, shp, jax.ShapeDtypeStruct((nbr, n_pairs, 2 * P, 2 * P), F32)),
        grid=(n_pairs, B),
        in_specs=[slab(0), slab(0), slab(n_pairs), slab(0), slab(0), slab(0), tab],
        out_specs=(slab(0), slab(0), slab(0), dtab),
        scratch_shapes=[pltpu.VMEM((S, LANES), F32)],
        compiler_params=_params("parallel", "arbitrary"),
    )(q, kv, kv, o, lse, do, bias)


def _adamw(w, g, m, v, name):
    R, C = w.shape
    tr = _tile(R, 256, SUBLANES_F32) if R % SUBLANES_F32 == 0 else R
    tc = _tile(C, 2048, LANES) if C % LANES == 0 else C

    def body(w_ref, g_ref, m_ref, v_ref, d_ref, nm_ref, nv_ref):
        g_ = g_ref[...]
        m2 = ADAM_B1 * m_ref[...] + (1.0 - ADAM_B1) * g_
        v2 = ADAM_B2 * v_ref[...] + (1.0 - ADAM_B2) * (g_ * g_)
        m_hat = m2 / (1.0 - ADAM_B1 ** ADAM_STEP)
        v_hat = v2 / (1.0 - ADAM_B2 ** ADAM_STEP)
        d_ref[...] = -ADAM_LR * (m_hat / (jnp.sqrt(v_hat) + ADAM_EPS) + ADAM_WD * w_ref[...])
        nm_ref[...] = m2
        nv_ref[...] = v2

    blk = pl.BlockSpec((tr, tc), lambda i, j: (i, j))
    shp = jax.ShapeDtypeStruct((R, C), F32)
    return pl.pallas_call(
        body, name=name, out_shape=(shp, shp, shp), grid=(R // tr, C // tc),
        in_specs=[blk] * 4, out_specs=(blk,) * 3, compiler_params=_params("parallel", "parallel"),
    )(w, g, m, v)


def _sum_slots(slots, own, name):
    n, R, C = slots.shape
    tr = _tile(R, 256, SUBLANES_BF16) if R % SUBLANES_BF16 == 0 else R
    tc = _tile(C, 2048, LANES) if C % LANES == 0 else C

    def body(s_ref, *rest):
        o_ref = rest[-1]
        acc = s_ref[0].astype(F32)
        if own is not None:
            acc = rest[0][...].astype(F32) + acc
        for k in range(1, n):
            acc = acc + s_ref[k].astype(F32)
        o_ref[...] = acc

    in_specs = [pl.BlockSpec((n, tr, tc), lambda i, j: (0, i, j))]
    args = [slots]
    if own is not None:
        in_specs.append(pl.BlockSpec((tr, tc), lambda i, j: (i, j)))
        args.append(own)
    return pl.pallas_call(
        body, name=name, out_shape=jax.ShapeDtypeStruct((R, C), F32), grid=(R // tr, C // tc),
        in_specs=in_specs, out_specs=pl.BlockSpec((tr, tc), lambda i, j: (i, j)),
        compiler_params=_params("parallel", "parallel"),
    )(*args)


def _my_place():
    return lax.axis_index("x"), lax.axis_index("y"), lax.axis_index("c")


def _other_chips(x, y):
    return [(1 - x, y), (x, 1 - y), (1 - x, 1 - y)]


def _piece(ref, blk, axis, shard, half):
    h0 = blk[0] // 2
    idx = []
    for dim, n in enumerate(blk):
        if dim == 0:
            start = half * h0 + (shard * n if axis == 0 else 0)
            idx.append(pl.ds(start, h0))
        elif dim == axis:
            idx.append(pl.ds(shard * n, n))
        else:
            idx.append(slice(None))
    return ref.at[tuple(idx)]


def _half_of(ref, blk, half):
    return ref.at[pl.ds(half * (blk[0] // 2), blk[0] // 2)]


def _piece_start(blk, axis, shard, half):
    return tuple(half * (blk[0] // 2) + (shard * n if axis == 0 else 0) if dim == 0 else (shard * n if dim == axis else 0)
                 for dim, n in enumerate(blk))


def _gather_weights(shards, axes, names, small):
    n = len(shards)
    blks = [s.shape for s in shards]
    task = _gather_ici_task(shards, axes, names)

    def stage1(*refs):
        small_in, outs, small_out = refs[n], refs[n + 1:2 * n + 1], refs[2 * n + 1]
        send_sems, recv_sems, local_sems = refs[2 * n + 2:]
        sem = lambda k: (send_sems.at[k], recv_sems.at[k])
        x, y, c = _my_place()
        me = 2 * x + y
        local = pltpu.make_async_copy(small_in, small_out.at[me], local_sems.at[0])
        local.start()
        sends, recvs = task.copies(None, outs, sem)
        for k, (px, py) in enumerate(_other_chips(x, y)):
            sends.append(_remote(small_in, small_out.at[me], sem(task.n_sems + k), (px, py, c)))
            recvs.append(_remote(small_in, small_out.at[2 * px + py], sem(task.n_sems + k), (px, py, c)))
        for cp in sends:
            cp.start()
        for cp in recvs:
            cp.wait_recv()
        for cp in sends:
            cp.wait_send()
        local.wait()

    res = pl.pallas_call(
        stage1, name="gather_weights_ici",
        out_shape=task.outs + [jax.ShapeDtypeStruct((N_SHARDS,) + small.shape, small.dtype)],
        in_specs=[ANY] * (n + 1), out_specs=[ANY] * (n + 1), input_output_aliases={a: a for a in range(n)},
        scratch_shapes=[pltpu.SemaphoreType.DMA((task.n_sems + 3,)), pltpu.SemaphoreType.DMA((task.n_sems + 3,)),
                        pltpu.SemaphoreType.DMA((1,))],
    )(*task.ins, small)
    full = _run_comm(_gather_d2d_task(list(res[:n]), blks, axes), "gather_weights_d2d")
    return full, res[n]


def _run_comm(comm, name):
    n_in, n_out = len(comm.ins), len(comm.outs)

    def body(*refs):
        send_sems, recv_sems = refs[n_in + n_out:]
        sends, recvs = comm.copies(refs[:n_in], refs[n_in:n_in + n_out], lambda k: (send_sems.at[k], recv_sems.at[k]))
        for cp in sends:
            cp.start()
        for cp in recvs:
            cp.wait_recv()
        for cp in sends:
            cp.wait_send()

    return list(pl.pallas_call(
        body, name=name, out_shape=comm.outs, in_specs=[ANY] * n_in, out_specs=[ANY] * n_out,
        input_output_aliases=comm.aliases, scratch_shapes=[pltpu.SemaphoreType.DMA((comm.n_sems,))] * 2,
    )(*comm.ins))


def _scatter_grads(grads, blks, axes, small):
    n = len(grads)
    halves = [(b[0] // 2,) + tuple(b[1:]) for b in blks]

    def body(*refs):
        ins, small_in = refs[:n], refs[n]
        outs, small_out = refs[n + 1:2 * n + 1], refs[2 * n + 1]
        send_sems, recv_sems, local_sems = refs[2 * n + 2:]
        x, y, c = _my_place()
        me = 4 * x + 2 * y + c
        local = [pltpu.make_async_copy(small_in, small_out.at[me], local_sems.at[0])]
        for cp in local:
            cp.start()
        sends, recvs = [], []
        for rel in range(1, N_DEVICES):
            px = x ^ ((rel >> 2) & 1)
            py = y ^ ((rel >> 1) & 1)
            pc = c ^ (rel & 1)
            peer = 4 * px + 2 * py + pc
            for a in range(n + 1):
                if a < n:
                    src = _piece(ins[a], blks[a], axes[a], 2 * px + py, pc)
                    dst = land = outs[a].at[rel - 1]
                else:
                    src, dst, land = small_in, small_out.at[me], small_out.at[peer]
                sends.append(pltpu.make_async_remote_copy(
                    src_ref=src, dst_ref=dst, send_sem=send_sems.at[a, rel - 1], recv_sem=recv_sems.at[a, rel - 1],
                    device_id=(px, py, pc), device_id_type=pl.DeviceIdType.MESH))
                recvs.append(pltpu.make_async_remote_copy(
                    src_ref=src, dst_ref=land, send_sem=send_sems.at[a, rel - 1], recv_sem=recv_sems.at[a, rel - 1],
                    device_id=(px, py, pc), device_id_type=pl.DeviceIdType.MESH))
        for cp in sends:
            cp.start()
        for cp in recvs:
            cp.wait_recv()
        for cp in sends:
            cp.wait_send()
        for cp in local:
            cp.wait()

    out_shape = [jax.ShapeDtypeStruct((N_DEVICES - 1,) + h, g.dtype) for h, g in zip(halves, grads)]
    out_shape.append(jax.ShapeDtypeStruct((N_DEVICES,) + small.shape, small.dtype))
    res = pl.pallas_call(
        body, name="scatter_grads", out_shape=out_shape,
        in_specs=[ANY] * (n + 1), out_specs=[ANY] * (n + 1),
        scratch_shapes=[pltpu.SemaphoreType.DMA((n + 1, N_DEVICES - 1)), pltpu.SemaphoreType.DMA((n + 1, N_DEVICES - 1)),
                        pltpu.SemaphoreType.DMA((1,))],
    )(*grads, small)
    return list(res[:n]), res[n]


def _remote(src, dst, sems, device):
    return pltpu.make_async_remote_copy(src_ref=src, dst_ref=dst, send_sem=sems[0], recv_sem=sems[1],
                                        device_id=device, device_id_type=pl.DeviceIdType.MESH)


def _sum_piece(dest, slots, grad, blk, axis, layer, n_layers, name):
    r, c = blk
    h0 = r // 2
    tr = _tile(h0, 256, SUBLANES_BF16)
    tc = _tile(c, 2048, LANES)
    place = jnp.stack([2 * lax.axis_index("x") + lax.axis_index("y"), lax.axis_index("c")]).astype(jnp.int32)

    def body(p_ref, s_ref, g_ref, *rest):
        acc = g_ref[...].astype(F32) + s_ref[0].astype(F32)
        for k in range(1, N_DEVICES - 1):
            acc = acc + s_ref[k].astype(F32)
        rest[-1][...] = acc

    def g_map(i, j, p):
        return (p[1] * (h0 // tr) + (p[0] * (r // tr) if axis == 0 else 0) + i, (p[0] * (c // tc) if axis == 1 else 0) + j)

    in_specs = [pl.BlockSpec((N_DEVICES - 1, tr, tc), lambda i, j, p: (0, i, j)), pl.BlockSpec((tr, tc), g_map)]
    args = [place, slots, grad]
    if dest is not None:
        in_specs.append(ANY)
        args.append(dest)
    return pl.pallas_call(
        body, name=name, out_shape=jax.ShapeDtypeStruct((n_layers, r, c), F32),
        grid_spec=pltpu.PrefetchScalarGridSpec(
            num_scalar_prefetch=1, grid=(h0 // tr, c // tc), in_specs=in_specs,
            out_specs=pl.BlockSpec((None, tr, tc), lambda i, j, p: (layer, p[1] * (h0 // tr) + i, j))),
        input_output_aliases={3: 0} if dest is not None else {},
        compiler_params=_params("parallel", "parallel"),
    )(*args)


def _swap_halves(blocks):
    n = len(blocks)
    layers = [(a, l) for a, b in enumerate(blocks) for l in range(b.shape[0])]

    def copies(in_refs, out_refs, sem):
        x, y, c = _my_place()
        sends, recvs = [], []
        for k, (a, l) in enumerate(layers):
            blk = blocks[a].shape[1:]
            mine = _half_of(out_refs[a].at[l], blk, c)
            sends.append(_remote(mine, mine, sem(k), (x, y, 1 - c)))
            recvs.append(_remote(mine, _half_of(out_refs[a].at[l], blk, 1 - c), sem(k), (x, y, 1 - c)))
        return sends, recvs

    task = _Comm(blocks, [jax.ShapeDtypeStruct(b.shape, b.dtype) for b in blocks], {a: a for a in range(n)},
                 len(layers), copies)
    return _run_comm(task, "swap_grad_halves")


def _place_shard(shard, axis, name):
    r, c = shard.shape
    tr = _tile(r, 512, SUBLANES_BF16)
    full = (r * N_SHARDS, c) if axis == 0 else (r, c * N_SHARDS)
    me2 = (2 * lax.axis_index("x") + lax.axis_index("y")).astype(jnp.int32).reshape(1)

    def body(me_ref, s_ref, o_ref):
        o_ref[...] = s_ref[...].astype(o_ref.dtype)

    if axis == 0:
        out_map = lambda i, me: (me[0] * (r // tr) + i, 0)
    else:
        out_map = lambda i, me: (i, me[0])
    return pl.pallas_call(
        body, name=name, out_shape=jax.ShapeDtypeStruct(full, BF16),
        grid_spec=pltpu.PrefetchScalarGridSpec(
            num_scalar_prefetch=1, grid=(r // tr,),
            in_specs=[pl.BlockSpec((tr, c), lambda i, me: (i, 0))], out_specs=pl.BlockSpec((tr, c), out_map)),
        compiler_params=_params("parallel"),
    )(me2, shard)


def _gather_ici_task(shards, axes, names):
    n = len(shards)
    blks = [s.shape for s in shards]
    bases = [_place_shard(s, ax, f"place_{nm}") for s, ax, nm in zip(shards, axes, names)]

    def copies(in_refs, out_refs, sem):
        x, y, c = _my_place()
        me = 2 * x + y
        sends, recvs = [], []
        for a in range(n):
            mine = _piece(out_refs[a], blks[a], axes[a], me, c)
            for k, (px, py) in enumerate(_other_chips(x, y)):
                sends.append(_remote(mine, mine, sem(3 * a + k), (px, py, c)))
                recvs.append(_remote(mine, _piece(out_refs[a], blks[a], axes[a], 2 * px + py, c), sem(3 * a + k),
                                     (px, py, c)))
        return sends, recvs

    return _Comm(bases, [jax.ShapeDtypeStruct(b.shape, b.dtype) for b in bases], {a: a for a in range(n)}, 3 * n, copies)


def _gather_d2d_task(partials, blks, axes):
    n = len(partials)

    def copies(in_refs, out_refs, sem):
        x, y, c = _my_place()
        sends, recvs = [], []
        for a in range(n):
            for k, (px, py) in enumerate(_other_chips(x, y)):
                mine = _piece(out_refs[a], blks[a], axes[a], 2 * px + py, c)
                theirs = _piece(out_refs[a], blks[a], axes[a], 2 * px + py, 1 - c)
                sends.append(_remote(mine, mine, sem(3 * a + k), (x, y, 1 - c)))
                recvs.append(_remote(mine, theirs, sem(3 * a + k), (x, y, 1 - c)))
        return sends, recvs

    return _Comm(partials, [jax.ShapeDtypeStruct(p.shape, p.dtype) for p in partials], {a: a for a in range(n)},
                 3 * n, copies)


def _scatter_task(grads, blks, axes):
    n = len(grads)

    def copies(in_refs, out_refs, sem):
        x, y, c = _my_place()
        sends, recvs = [], []
        for rel in range(1, N_DEVICES):
            px, py, pc = x ^ ((rel >> 2) & 1), y ^ ((rel >> 1) & 1), c ^ (rel & 1)
            for a in range(n):
                src = _piece(in_refs[a], blks[a], axes[a], 2 * px + py, pc)
                k = (N_DEVICES - 1) * a + rel - 1
                sends.append(_remote(src, out_refs[a].at[rel - 1], sem(k), (px, py, pc)))
                recvs.append(_remote(src, out_refs[a].at[rel - 1], sem(k), (px, py, pc)))
        return sends, recvs

    outs = [jax.ShapeDtypeStruct((N_DEVICES - 1, b[0] // 2) + tuple(b[1:]), g.dtype) for b, g in zip(blks, grads)]
    return _Comm(grads, outs, {}, (N_DEVICES - 1) * n, copies)


def _pack(arrays):
    flat = jnp.concatenate([a.reshape(-1).astype(F32) for a in arrays])
    pad = (-flat.shape[0]) % (SUBLANES_F32 * LANES)
    return jnp.pad(flat, (0, pad)).reshape(-1, LANES)


def _unpack(packed, shapes):
    flat = packed.reshape(-1)
    out, off = [], 0
    for s in shapes:
        n = int(np.prod(s))
        out.append(flat[off:off + n].reshape(s))
        off += n
    return out


def _local_step(x, target, W, shards, geom, small):
    W = dict(W)
    B, S, D = x.shape
    T = B * S
    x2 = x.reshape(T, D)
    tgt = target.reshape(T, D)
    bmap = jnp.asarray(_bucket_map())
    blk = lambda names: [geom[k][0] for k in names]
    axs = lambda names: [geom[k][1] for k in names]
    ici = lambda names: _gather_ici_task([shards[k] for k in names], axs(names), names)
    d2d = lambda names, partials: _gather_d2d_task(list(partials), blk(names), axs(names))
    big, slots = {}, {}

    def scatter(names):
        return _scatter_task([big[k] for k in names], blk(names), axs(names))

    def ffn_bwd(l, dout, h, saved, first, second):
        xn, u, a = saved
        cw, cb = small["ffn_conv"][l], small["ffn_conv_b"][l]
        da = _mm_nt(dout, W[f"w_down{l}"], BF16, f"ffn{l}_down_dx")
        big[f"w_down{l}"] = _mm_tn(a, dout, f"ffn{l}_down_dw")
        names = first + [f"w_down{l}"]
        du, g_cw, g_cb, landed = _ffn_gate_bwd(u, cw, cb, da, S, f"ffn{l}_gate_bwd", comm=scatter(names))
        slots.update(zip(names, landed))
        big[f"w_up{l}"] = _mm_tn(xn, du, f"ffn{l}_up_dw")
        if second:
            dxn, (slots[f"w_up{l}"],) = _mm_nt(du, W[f"w_up{l}"], BF16, f"ffn{l}_up_dx", comm=scatter([f"w_up{l}"]))
        else:
            dxn = _mm_nt(du, W[f"w_up{l}"], BF16, f"ffn{l}_up_dx")
        dh, g_norm = _rmsnorm_bwd(h, small["ffn_norm"][l], dxn, dout, f"ffn{l}_norm_bwd")
        return dh, g_cw, g_cb, g_norm

    xn0 = _rmsnorm_fwd(x2, small["a_norm"][0], "a_norm")
    p, part = _mm_nn(xn0, W["w_in"], None, BF16, "a_in", comm=ici(["w_up0"]))
    z, (W["w_up0"],) = _gate_a_fwd(p, small["a_conv"][0], S, "a_gate", comm=d2d(["w_up0"], part))
    h1, part = _mm_nn(z, W["w_out"], x2, F32, "a_out", comm=ici(["w_down0"]))
    xn1, (W["w_down0"],) = _rmsnorm_fwd(h1, small["ffn_norm"][0], "ffn0_norm", comm=d2d(["w_down0"], part))
    later = ["w_kv", "w_q", "w_o", "w_up1"]
    u0, part = _mm_nn(xn1, W["w_up0"], None, BF16, "ffn0_up", comm=ici(later))
    a0, landed = _ffn_gate_fwd(u0, small["ffn_conv"][0], small["ffn_conv_b"][0], S, "ffn0_gate",
                               comm=_Comm.join([d2d(later, part), ici(["w_down1"])]))
    W.update(zip(later, landed[:len(later)]))
    h2, (W["w_down1"],) = _mm_nn(a0, W["w_down0"], h1, F32, "ffn0_down", comm=d2d(["w_down1"], landed[len(later):]))
    kvn = _rmsnorm_fwd(h2, small["kv_norm"], "kv_norm")
    kv = _mm_nn(kvn, W["w_kv"], None, F32, "kv_proj")
    xn3 = _rmsnorm_fwd(h2, small["b_norm"][0], "b_norm")
    q = _mm_nn(xn3, W["w_q"], None, F32, "q_proj")
    bias = _bias_tables(small["rel_bias"], bmap, "rel_bias_tables")
    q3, kv3 = q.reshape(B, S, D), kv.reshape(B, S, 2 * D)
    o3, lse3 = _attn_fwd(q3, kv3, bias, "attn_fwd")
    o = o3.reshape(T, D)
    h3 = _mm_nn(o, W["w_o"], h2, F32, "o_proj")
    xn4 = _rmsnorm_fwd(h3, small["ffn_norm"][1], "ffn1_norm")
    u1 = _mm_nn(xn4, W["w_up1"], None, BF16, "ffn1_up")
    a1 = _ffn_gate_fwd(u1, small["ffn_conv"][1], small["ffn_conv_b"][1], S, "ffn1_gate")
    h4 = _mm_nn(a1, W["w_down1"], h3, F32, "ffn1_down")
    sq_err, dh4, g_final = _loss_head(h4, small["final_norm"], tgt, "loss_head")
    loss = 0.5 * jnp.sum(sq_err) / D

    dh3, g_cw1, g_cb1, g_fn1 = ffn_bwd(1, dh4, h3, (xn4, u1, a1), [], False)
    do = _mm_nt(dh3, W["w_o"], F32, "o_proj_dx")
    big["w_o"] = _mm_tn(o, dh3, "o_proj_dw")
    (dq3, dk3, dv3, dbias), landed = _attn_bwd(q3, kv3, o3, lse3, do.reshape(B, S, D), bias, "attn_bwd",
                                               comm=scatter(["w_up1", "w_o"]))
    slots.update(zip(["w_up1", "w_o"], landed))
    g_rel = _bias_grad(dbias, bmap, "rel_bias_grad")[:, :REL_BUCKETS].T
    dq, dk, dv = dq3.reshape(T, D), dk3.reshape(T, D), dv3.reshape(T, D)
    dxn3 = _mm_nt(dq, W["w_q"], BF16, "q_proj_dx")
    big["w_q"] = _mm_tn(xn3, dq, "q_proj_dw")
    dh2, g_bn = _rmsnorm_bwd(h2, small["b_norm"][0], dxn3, dh3, "b_norm_bwd")
    dkvn = _mm_nt([dk, dv], W["w_kv"], BF16, "kv_proj_dx")
    big["w_kv"] = jnp.concatenate([_mm_tn(kvn, dk, "k_proj_dw"), _mm_tn(kvn, dv, "v_proj_dw")], axis=1)
    dh2, g_kvn = _rmsnorm_bwd(h2, small["kv_norm"], dkvn, dh2, "kv_norm_bwd")
    dh1, g_cw0, g_cb0, g_fn0 = ffn_bwd(0, dh2, h1, (xn1, u0, a0), ["w_q", "w_kv"], True)
    dz = _mm_nt(dh1, W["w_out"], BF16, "a_out_dx")
    big["w_out"] = _mm_tn(z, dh1, "a_out_dw")
    dp, g_aconv, (slots["w_out"],) = _gate_a_bwd(p, small["a_conv"][0], dz, S, "a_gate_bwd", comm=scatter(["w_out"]))
    big["w_in"] = _mm_tn(xn0, dp, "a_in_dw")
    dxn0, (slots["w_in"],) = _mm_nt(dp, W["w_in"], BF16, "a_in_dx", comm=scatter(["w_in"]))
    dx, g_an = _rmsnorm_bwd(x2, small["a_norm"][0], dxn0, dh1, "a_norm_bwd")

    small_g = {"a_norm": g_an[None], "a_conv": g_aconv[None], "kv_norm": g_kvn, "b_norm": g_bn[None],
               "rel_bias": g_rel, "ffn_norm": jnp.stack([g_fn0, g_fn1]), "ffn_conv": jnp.stack([g_cw0, g_cw1]),
               "ffn_conv_b": jnp.stack([g_cb0, g_cb1]), "final_norm": g_final}
    return loss, dx.reshape(B, S, D), big, slots, small_g


BIG = ("w_in", "w_out", "w_kv", "w_q", "w_o", "w_up0", "w_up1", "w_down0", "w_down1")
SMALL = ("a_norm", "a_conv", "kv_norm", "b_norm", "rel_bias", "ffn_norm", "ffn_conv", "ffn_conv_b", "final_norm")
SMALL_SHARDED = ("a_norm", "a_conv", "ffn_conv")
WEIGHT_ORDER = ("a_norm", "a_w_in", "a_conv", "a_w_out", "kv_norm", "w_kv", "b_norm", "b_w_q", "b_w_o", "rel_bias",
                "ffn_norm", "ffn_w_up", "ffn_conv", "ffn_conv_b", "ffn_w_down", "final_norm")
GRAD_OF = {"w_in": ("a_w_in", 0), "w_out": ("a_w_out", 0), "w_kv": ("w_kv", 0), "w_q": ("b_w_q", 0), "w_o": ("b_w_o", 0),
           "w_up0": ("ffn_w_up", 0), "w_up1": ("ffn_w_up", 1), "w_down0": ("ffn_w_down", 0), "w_down1": ("ffn_w_down", 1)}


def _as2d(a):
    return a.reshape(-1, a.shape[-1])


def kernel(x, a_norm, a_w_in, a_conv, a_w_out, kv_norm, w_kv, b_norm, b_w_q, b_w_o, rel_bias, ffn_norm, ffn_w_up, ffn_conv, ffn_conv_b, ffn_w_down, final_norm, loss_target, m_a_norm, m_a_w_in, m_a_conv, m_a_w_out, m_kv_norm, m_w_kv, m_b_norm, m_b_w_q, m_b_w_o, m_rel_bias, m_ffn_norm, m_ffn_w_up, m_ffn_conv, m_ffn_conv_b, m_ffn_w_down, m_final_norm, v_a_norm, v_a_w_in, v_a_conv, v_a_w_out, v_kv_norm, v_w_kv, v_b_norm, v_b_w_q, v_b_w_o, v_rel_bias, v_ffn_norm, v_ffn_w_up, v_ffn_conv, v_ffn_conv_b, v_ffn_w_down, v_final_norm):
    given = dict(a_norm=a_norm, a_w_in=a_w_in, a_conv=a_conv, a_w_out=a_w_out, kv_norm=kv_norm, w_kv=w_kv, b_norm=b_norm,
                 b_w_q=b_w_q, b_w_o=b_w_o, rel_bias=rel_bias, ffn_norm=ffn_norm, ffn_w_up=ffn_w_up, ffn_conv=ffn_conv,
                 ffn_conv_b=ffn_conv_b, ffn_w_down=ffn_w_down, final_norm=final_norm)
    mom_m = dict(a_norm=m_a_norm, a_w_in=m_a_w_in, a_conv=m_a_conv, a_w_out=m_a_w_out, kv_norm=m_kv_norm, w_kv=m_w_kv,
                 b_norm=m_b_norm, b_w_q=m_b_w_q, b_w_o=m_b_w_o, rel_bias=m_rel_bias, ffn_norm=m_ffn_norm,
                 ffn_w_up=m_ffn_w_up, ffn_conv=m_ffn_conv, ffn_conv_b=m_ffn_conv_b, ffn_w_down=m_ffn_w_down,
                 final_norm=m_final_norm)
    mom_v = dict(a_norm=v_a_norm, a_w_in=v_a_w_in, a_conv=v_a_conv, a_w_out=v_a_w_out, kv_norm=v_kv_norm, w_kv=v_w_kv,
                 b_norm=v_b_norm, b_w_q=v_b_w_q, b_w_o=v_b_w_o, rel_bias=v_rel_bias, ffn_norm=v_ffn_norm,
                 ffn_w_up=v_ffn_w_up, ffn_conv=v_ffn_conv, ffn_conv_b=v_ffn_conv_b, ffn_w_down=v_ffn_w_down,
                 final_norm=v_final_norm)

    shard = {"w_in": (a_w_in[0], 1), "w_out": (a_w_out[0], 0), "w_kv": (w_kv, 1), "w_q": (b_w_q[0], 0),
             "w_o": (b_w_o[0], 0), "w_up0": (ffn_w_up[0], 1), "w_up1": (ffn_w_up[1], 1),
             "w_down0": (ffn_w_down[0], 0), "w_down1": (ffn_w_down[1], 0)}
    blks = [shard[k][0].shape for k in BIG]
    axes = [shard[k][1] for k in BIG]

    small_sharded = [given[k] for k in SMALL_SHARDED]
    packed = _pack(small_sharded)
    first = ("w_in", "w_out")
    fulls, packed_all = _gather_weights([shard[k][0] for k in first], [shard[k][1] for k in first], first, packed)
    W = dict(zip(first, fulls))
    later = {k: shard[k][0] for k in BIG if k not in first}
    geom = {k: (shard[k][0].shape, shard[k][1]) for k in BIG}
    small = {k: given[k] for k in SMALL}
    per_shard = [_unpack(packed_all[j], [a.shape for a in small_sharded]) for j in range(N_SHARDS)]
    for i, k in enumerate(SMALL_SHARDED):
        small[k] = jnp.concatenate([per_shard[j][i] for j in range(N_SHARDS)], axis=-1)

    loss, grad_x, big_g, slots, small_g = _local_step(x, loss_target, W, later, geom, small)
    loss = lax.psum(loss, MESH_AXES)

    small_shapes = [small_g[k].shape for k in SMALL]
    _, small_slots = _scatter_grads([], [], [], _pack([small_g[k] for k in SMALL]))
    layers_of = {}
    for k in BIG:
        layers_of.setdefault(GRAD_OF[k][0], []).append(k)
    blocks = {}
    for name, members in layers_of.items():
        dest = None
        for k in members:
            dest = _sum_piece(dest, slots[k], big_g[k], geom[k][0], geom[k][1], GRAD_OF[k][1], len(members), f"sum_{k}")
        blocks[name] = dest
    reduced = dict(zip(blocks, _swap_halves(list(blocks.values()))))
    small_sum = _sum_slots(small_slots, None, "sum_small")
    small_red = dict(zip(SMALL, _unpack(small_sum, small_shapes)))
    j = 2 * lax.axis_index("x") + lax.axis_index("y")
    for k in SMALL_SHARDED:
        w = given[k].shape[-1]
        small_red[k] = lax.dynamic_slice_in_dim(small_red[k], j * w, w, axis=small_red[k].ndim - 1)

    grads, deltas, new_m, new_v = {}, {}, {}, {}
    for name in WEIGHT_ORDER:
        if name in reduced:
            g = reduced[name].reshape(given[name].shape)
            d, nm, nv = _adamw(_as2d(given[name]), _as2d(g), _as2d(mom_m[name]), _as2d(mom_v[name]), f"adamw_{name}")
            grads[name] = g
            deltas[name], new_m[name], new_v[name] = (t.reshape(given[name].shape) for t in (d, nm, nv))
    small_names = [n for n in WEIGHT_ORDER if n not in reduced]
    for n in small_names:
        grads[n] = small_red[n].reshape(given[n].shape)
    sw, sg, sm, sv = (_pack([d[n] for n in small_names]) for d in (given, grads, mom_m, mom_v))
    d, nm, nv = _adamw(sw, sg, sm, sv, "adamw_small")
    shapes = [given[n].shape for n in small_names]
    for n, a, b_, c_ in zip(small_names, _unpack(d, shapes), _unpack(nm, shapes), _unpack(nv, shapes)):
        deltas[n], new_m[n], new_v[n] = a, b_, c_

    return (loss, grad_x, *[grads[n] for n in WEIGHT_ORDER], *[deltas[n] for n in WEIGHT_ORDER],
            *[new_m[n] for n in WEIGHT_ORDER], *[new_v[n] for n in WEIGHT_ORDER])
```

```python
import functools
import math

import numpy as np

import jax
import jax.numpy as jnp
from jax import lax
from jax.experimental import pallas as pl
from jax.experimental.pallas import tpu as pltpu

F32 = jnp.float32
BF16 = jnp.bfloat16

RMS_EPS = 1e-6
HEAD_DIM = 64
ATT_BLOCK = 128
DILATED_BRANCHES = ((128, 1), (512, 4), (2048, 16))
REL_BUCKETS = 32
REL_MAX_DISTANCE = 2048
MASKED_LOGIT = -1e30
ATTN_FWD_UNROLL = 4
ATTN_BWD_UNROLL = 4

ADAM_LR = 0.001
ADAM_B1 = 0.9
ADAM_B2 = 0.999
ADAM_EPS = 1e-08
ADAM_WD = 0.01
ADAM_STEP = 10

LANES = 128
SUBLANES_F32 = 8
SUBLANES_BF16 = 16
VMEM_LIMIT_BYTES = 56 * 1024 * 1024

MESH_AXES = ("x", "y", "c")
N_SHARDS = 4
N_DEVICES = 8
ANY = pl.BlockSpec(memory_space=pl.ANY)


def _tile(n, pref, mult):
    best = None
    for t in range(mult, min(n, pref) + 1, mult):
        if n % t == 0:
            best = t
    if best is None:
        raise ValueError(f"no tile for {n} (multiple of {mult}, at most {pref})")
    return best


def _params(*sem):
    return pltpu.CompilerParams(dimension_semantics=sem, vmem_limit_bytes=VMEM_LIMIT_BYTES)


class _Comm:
    def __init__(self, ins, outs, aliases, n_sems, copies):
        self.ins, self.outs, self.aliases, self.n_sems, self.copies = list(ins), list(outs), dict(aliases), n_sems, copies

    @staticmethod
    def join(parts):
        parts = [p for p in parts if p is not None]
        ins, outs, aliases, offs, n_sems = [], [], {}, [], 0
        for p in parts:
            offs.append((len(ins), len(outs), n_sems))
            aliases.update({len(ins) + i: len(outs) + o for i, o in p.aliases.items()})
            ins += p.ins
            outs += p.outs
            n_sems += p.n_sems

        def copies(in_refs, out_refs, sem):
            sends, recvs = [], []
            for p, (i0, o0, s0) in zip(parts, offs):
                s, r = p.copies(in_refs[i0:i0 + len(p.ins)], out_refs[o0:o0 + len(p.outs)],
                                lambda k, s0=s0: sem(s0 + k))
                sends += s
                recvs += r
            return sends, recvs

        return _Comm(ins, outs, aliases, n_sems, copies)


def _pallas(body, *, comm=None, name, out_shape, grid=(), in_specs=(), out_specs=(), scratch_shapes=(),
            compiler_params=None):
    if comm is None:
        return pl.pallas_call(body, name=name, out_shape=out_shape, grid=grid, in_specs=in_specs, out_specs=out_specs,
                              scratch_shapes=scratch_shapes, compiler_params=compiler_params)
    single = not isinstance(out_shape, (tuple, list))
    outs = (out_shape,) if single else tuple(out_shape)
    o_specs = (out_specs,) if single else tuple(out_specs)
    n_in, n_cin, n_out, n_cout, n_scr = len(in_specs), len(comm.ins), len(outs), len(comm.outs), len(scratch_shapes)

    def carried(*refs):
        base_in, c_in = refs[:n_in], refs[n_in:n_in + n_cin]
        o0 = n_in + n_cin
        base_out, c_out = refs[o0:o0 + n_out], refs[o0 + n_out:o0 + n_out + n_cout]
        s0 = o0 + n_out + n_cout
        base_scr, (send_sems, recv_sems) = refs[s0:s0 + n_scr], refs[s0 + n_scr:]
        sem = lambda k: (send_sems.at[k], recv_sems.at[k])
        first = functools.reduce(jnp.logical_and, [pl.program_id(a) == 0 for a in range(len(grid))])
        last = functools.reduce(jnp.logical_and, [pl.program_id(a) == n - 1 for a, n in enumerate(grid)])

        @pl.when(first)
        def _():
            for cp in comm.copies(c_in, c_out, sem)[0]:
                cp.start()

        body(*base_in, *base_out, *base_scr)

        @pl.when(last)
        def _():
            sends, recvs = comm.copies(c_in, c_out, sem)
            for cp in recvs:
                cp.wait_recv()
            for cp in sends:
                cp.wait_send()

    call = pl.pallas_call(
        carried, name=name, out_shape=outs + tuple(comm.outs), grid=grid,
        in_specs=list(in_specs) + [ANY] * n_cin, out_specs=o_specs + (ANY,) * n_cout,
        scratch_shapes=list(scratch_shapes) + [pltpu.SemaphoreType.DMA((comm.n_sems,))] * 2,
        input_output_aliases={n_in + i: n_out + o for i, o in comm.aliases.items()},
        compiler_params=_params(*(["arbitrary"] * len(grid))))

    def run(*args):
        res = call(*args, *comm.ins)
        base = res[0] if single else tuple(res[:n_out])
        return base, list(res[n_out:])

    return run


def _rmsnorm_fwd(x, g, name, comm=None):
    T, D = x.shape
    tm = _tile(T, 512, SUBLANES_BF16)

    def body(x_ref, g_ref, o_ref):
        xf = x_ref[...]
        r = lax.rsqrt(jnp.mean(xf * xf, axis=-1, keepdims=True) + RMS_EPS)
        o_ref[...] = ((xf * r) * g_ref[...]).astype(o_ref.dtype)

    return _pallas(
        body, comm=comm, name=name, out_shape=jax.ShapeDtypeStruct((T, D), BF16), grid=(T // tm,),
        in_specs=[pl.BlockSpec((tm, D), lambda i: (i, 0)), pl.BlockSpec((1, D), lambda i: (0, 0))],
        out_specs=pl.BlockSpec((tm, D), lambda i: (i, 0)),
        compiler_params=_params("parallel"),
    )(x, g.reshape(1, D))


def _rmsnorm_bwd(x, g, dxn, dres, name):
    T, D = x.shape
    tm = _tile(T, 512, SUBLANES_BF16)

    def body(x_ref, g_ref, dxn_ref, dres_ref, dx_ref, dg_ref):
        @pl.when(pl.program_id(0) == 0)
        def _():
            dg_ref[...] = jnp.zeros_like(dg_ref)

        xf = x_ref[...]
        r = lax.rsqrt(jnp.mean(xf * xf, axis=-1, keepdims=True) + RMS_EPS)
        xhat = xf * r
        dy = dxn_ref[...].astype(F32)
        dg_ref[0:1, :] += jnp.sum(dy * xhat, axis=0, keepdims=True)
        t = dy * g_ref[...]
        dx_ref[...] = dres_ref[...] + r * (t - xhat * jnp.mean(t * xhat, axis=-1, keepdims=True))

    row = pl.BlockSpec((tm, D), lambda i: (i, 0))
    dx, dg = pl.pallas_call(
        body, name=name,
        out_shape=(jax.ShapeDtypeStruct((T, D), F32), jax.ShapeDtypeStruct((SUBLANES_F32, D), F32)),
        grid=(T // tm,),
        in_specs=[row, pl.BlockSpec((1, D), lambda i: (0, 0)), row, row],
        out_specs=(row, pl.BlockSpec((SUBLANES_F32, D), lambda i: (0, 0))),
        compiler_params=_params("arbitrary"),
    )(x, g.reshape(1, D), dxn, dres)
    return dx, dg[0]


def _loss_head(h, g, target, name):
    T, D = h.shape
    tm = _tile(T, 512, SUBLANES_F32)

    def body(h_ref, g_ref, t_ref, dh_ref, acc_ref):
        @pl.when(pl.program_id(0) == 0)
        def _():
            acc_ref[...] = jnp.zeros_like(acc_ref)

        xf = h_ref[...]
        r = lax.rsqrt(jnp.mean(xf * xf, axis=-1, keepdims=True) + RMS_EPS)
        xhat = xf * r
        err = xhat * g_ref[...] - t_ref[...]
        dy = err * (1.0 / D)
        acc_ref[0:1, :] += jnp.sum(dy * xhat, axis=0, keepdims=True)
        acc_ref[1:2, :] += jnp.sum(err * err, axis=0, keepdims=True)
        t = dy * g_ref[...]
        dh_ref[...] = r * (t - xhat * jnp.mean(t * xhat, axis=-1, keepdims=True))

    row = pl.BlockSpec((tm, D), lambda i: (i, 0))
    dh, acc = pl.pallas_call(
        body, name=name,
        out_shape=(jax.ShapeDtypeStruct((T, D), F32), jax.ShapeDtypeStruct((SUBLANES_F32, D), F32)),
        grid=(T // tm,),
        in_specs=[row, pl.BlockSpec((1, D), lambda i: (0, 0)), row],
        out_specs=(row, pl.BlockSpec((SUBLANES_F32, D), lambda i: (0, 0))),
        compiler_params=_params("arbitrary"),
    )(h, g.reshape(1, D), target)
    return acc[1], dh, acc[0]


def _mm_nn(a, b, res, out_dtype, name, comm=None, norm_gains=()):
    T, K = a.shape
    N = b.shape[1]
    tm = _tile(T, 512, SUBLANES_BF16)
    tn = _tile(N, 3072, LANES)
    n_g = len(norm_gains)
    assert not n_g or tn == N, "the fused rmsnorm needs whole rows in one tile"

    def body(a_ref, b_ref, *rest):
        ins, outs = rest[:len(rest) - 1 - n_g], rest[len(rest) - 1 - n_g:]
        acc = jnp.dot(a_ref[...].astype(BF16), b_ref[...], preferred_element_type=F32)
        if res is not None:
            acc = acc + ins[0][...]
        outs[0][...] = acc.astype(outs[0].dtype)
        if n_g:
            y = acc * lax.rsqrt(jnp.mean(acc * acc, axis=-1, keepdims=True) + RMS_EPS)
            for g_ref, xn_ref in zip(ins[len(ins) - n_g:], outs[1:]):
                xn_ref[...] = (y * g_ref[...]).astype(xn_ref.dtype)

    tile = pl.BlockSpec((tm, tn), lambda j, i: (i, j))
    in_specs = [pl.BlockSpec((tm, K), lambda j, i: (i, 0)), pl.BlockSpec((K, tn), lambda j, i: (0, j))]
    args = [a, b]
    if res is not None:
        in_specs.append(tile)
        args.append(res)
    for g in norm_gains:
        in_specs.append(pl.BlockSpec((1, N), lambda j, i: (0, 0)))
        args.append(g.reshape(1, N))
    out_shape = jax.ShapeDtypeStruct((T, N), out_dtype)
    if n_g:
        out_shape = (out_shape,) + (jax.ShapeDtypeStruct((T, N), BF16),) * n_g
    return _pallas(
        body, comm=comm, name=name, out_shape=out_shape, grid=(N // tn, T // tm),
        in_specs=in_specs, out_specs=(tile,) * (1 + n_g) if n_g else tile,
        compiler_params=_params("parallel", "parallel"),
    )(*args)


def _mm_nt(dy, b, out_dtype, name, comm=None, norm=None):
    dys = list(dy) if isinstance(dy, (list, tuple)) else [dy]
    T, n_each = dys[0].shape
    K = b.shape[0]
    tm = _tile(T, 1024 if norm is None else 512, SUBLANES_BF16)
    tk = _tile(K, 1536, LANES)
    tn = _tile(n_each, 2816, LANES)
    per = n_each // tn
    n_steps = per * len(dys)
    assert norm is None or tk == K, "the fused rmsnorm backward needs whole rows in one tile"

    def body(*refs):
        dy_refs, b_ref, acc_ref = refs[:len(dys)], refs[len(dys)], refs[-1]
        i, n = pl.program_id(0), pl.program_id(2)

        @pl.when(n == 0)
        def _():
            acc_ref[...] = jnp.zeros_like(acc_ref)

        for p, dy_ref in enumerate(dy_refs):
            @pl.when(jnp.logical_and(n >= p * per, n < (p + 1) * per))
            def _(dy_ref=dy_ref):
                acc_ref[...] += lax.dot_general(dy_ref[...].astype(BF16), b_ref[...], (((1,), (1,)), ((), ())),
                                                preferred_element_type=F32)

        if norm is None:
            @pl.when(n == n_steps - 1)
            def _():
                refs[-2][...] = acc_ref[...].astype(refs[-2].dtype)
        else:
            x_ref, g_ref, dres_ref, dx_ref, dg_ref = refs[len(dys) + 1:-1]

            @pl.when(jnp.logical_and(i == 0, n == 0))
            def _():
                dg_ref[...] = jnp.zeros_like(dg_ref)

            @pl.when(n == n_steps - 1)
            def _():
                xf = x_ref[...]
                r = lax.rsqrt(jnp.mean(xf * xf, axis=-1, keepdims=True) + RMS_EPS)
                xhat = xf * r
                d = acc_ref[...]
                dg_ref[0:1, :] += jnp.sum(d * xhat, axis=0, keepdims=True)
                t = d * g_ref[...]
                dx_ref[...] = dres_ref[...] + r * (t - xhat * jnp.mean(t * xhat, axis=-1, keepdims=True))

    in_specs = [pl.BlockSpec((tm, tn), lambda i, k, n, p=p: (i, jnp.clip(n - p * per, 0, per - 1))) for p in range(len(dys))]
    in_specs.append(pl.BlockSpec((tk, tn), lambda i, k, n: (k, n)))
    args = dys + [b]
    tile = pl.BlockSpec((tm, tk), lambda i, k, n: (i, k))
    if norm is None:
        out_shape, out_specs = jax.ShapeDtypeStruct((T, K), out_dtype), tile
    else:
        x, g, dres = norm
        in_specs += [tile, pl.BlockSpec((1, K), lambda i, k, n: (0, 0)), tile]
        args += [x, g.reshape(1, K), dres]
        out_shape = (jax.ShapeDtypeStruct((T, K), F32), jax.ShapeDtypeStruct((SUBLANES_F32, K), F32))
        out_specs = (tile, pl.BlockSpec((SUBLANES_F32, K), lambda i, k, n: (0, 0)))
    return _pallas(
        body, comm=comm, name=name, out_shape=out_shape, grid=(T // tm, K // tk, n_steps),
        in_specs=in_specs, out_specs=out_specs, scratch_shapes=[pltpu.VMEM((tm, tk), F32)],
        compiler_params=_params("parallel", "parallel", "arbitrary") if norm is None else _params(*["arbitrary"] * 3),
    )(*args)


def _mm_tn(a, dy, name):
    T, K = a.shape
    N = dy.shape[1]
    tt = _tile(T, 2048, SUBLANES_BF16)
    tk = _tile(K, 1536, LANES)
    tn = _tile(N, 1536, LANES)
    t_steps = T // tt

    def body(a_ref, dy_ref, o_ref, acc_ref):
        t = pl.program_id(2)

        @pl.when(t == 0)
        def _():
            acc_ref[...] = jnp.zeros_like(acc_ref)

        acc_ref[...] += lax.dot_general(a_ref[...].astype(BF16), dy_ref[...].astype(BF16),
                                        (((0,), (0,)), ((), ())), preferred_element_type=F32)

        @pl.when(t == t_steps - 1)
        def _():
            o_ref[...] = acc_ref[...].astype(o_ref.dtype)

    return pl.pallas_call(
        body, name=name, out_shape=jax.ShapeDtypeStruct((K, N), BF16), grid=(K // tk, N // tn, t_steps),
        in_specs=[pl.BlockSpec((tt, tk), lambda k, n, t: (t, k)), pl.BlockSpec((tt, tn), lambda k, n, t: (t, n))],
        out_specs=pl.BlockSpec((tk, tn), lambda k, n, t: (k, n)),
        scratch_shapes=[pltpu.VMEM((tk, tn), F32)],
        compiler_params=_params("parallel", "parallel", "arbitrary"),
    )(a, dy)


def _rows_before(halo, cur, k):
    h = halo.shape[0]
    return pltpu.roll(jnp.concatenate([halo, cur], axis=0), k, 0)[h:]


def _rows_after(cur, halo, k):
    n = cur.shape[0]
    total = n + halo.shape[0]
    return pltpu.roll(jnp.concatenate([cur, halo], axis=0), total - k, 0)[:n]


def _halo_specs(tm, width, n_rows):
    per = tm // SUBLANES_BF16
    last = n_rows // SUBLANES_BF16 - 1
    prev = pl.BlockSpec((SUBLANES_BF16, width), lambda i: (jnp.maximum(i * per - 1, 0), 0))
    nxt = pl.BlockSpec((SUBLANES_BF16, width), lambda i: (jnp.minimum((i + 1) * per, last), 0))
    return prev, nxt


def _gate_a_fwd(p, cw, seq, name, comm=None):
    T, D3 = p.shape
    D = D3 // 3
    tm = _tile(seq, 512, SUBLANES_BF16)
    cc = _tile(D, 256, LANES)
    prev, _ = _halo_specs(tm, D3, T)

    def body(p_ref, ph_ref, cw_ref, z_ref):
        at_start = (pl.program_id(0) * tm) % seq == 0
        for c0 in range(0, D, cc):
            b = p_ref[:, c0:c0 + cc].astype(F32)
            u = p_ref[:, D + c0:D + c0 + cc].astype(F32) * p_ref[:, 2 * D + c0:2 * D + c0 + cc].astype(F32)
            uh = ph_ref[:, D + c0:D + c0 + cc].astype(F32) * ph_ref[:, 2 * D + c0:2 * D + c0 + cc].astype(F32)
            uh = jnp.where(at_start, 0.0, uh)
            w = cw_ref[:, c0:c0 + cc]
            cv = _rows_before(uh, u, 2) * w[0:1] + _rows_before(uh, u, 1) * w[1:2] + u * w[2:3]
            z_ref[:, c0:c0 + cc] = (b * cv).astype(z_ref.dtype)

    return _pallas(
        body, comm=comm, name=name, out_shape=jax.ShapeDtypeStruct((T, D), BF16), grid=(T // tm,),
        in_specs=[pl.BlockSpec((tm, D3), lambda i: (i, 0)), prev, pl.BlockSpec((3, D), lambda i: (0, 0))],
        out_specs=pl.BlockSpec((tm, D), lambda i: (i, 0)),
        compiler_params=_params("parallel"),
    )(p, p, cw)


def _gate_a_bwd(p, cw, dz, seq, name, comm=None):
    T, D3 = p.shape
    D = D3 // 3
    tm = _tile(seq, 512, SUBLANES_BF16)
    cc = _tile(D, 256, LANES)
    p_prev, p_next = _halo_specs(tm, D3, T)
    _, dz_next = _halo_specs(tm, D, T)

    def body(p_ref, pp_ref, pn_ref, dz_ref, dzn_ref, cw_ref, dp_ref, dcw_ref):
        i = pl.program_id(0)

        @pl.when(i == 0)
        def _():
            dcw_ref[...] = jnp.zeros_like(dcw_ref)

        at_start = (i * tm) % seq == 0
        at_end = ((i + 1) * tm) % seq == 0
        for c0 in range(0, D, cc):
            cb, cc_, ch = slice(c0, c0 + cc), slice(D + c0, D + c0 + cc), slice(2 * D + c0, 2 * D + c0 + cc)
            b = p_ref[:, cb].astype(F32)
            c = p_ref[:, cc_].astype(F32)
            hh = p_ref[:, ch].astype(F32)
            u = c * hh
            uh = jnp.where(at_start, 0.0, pp_ref[:, cc_].astype(F32) * pp_ref[:, ch].astype(F32))
            w = cw_ref[:, cb]
            u1 = _rows_before(uh, u, 1)
            u2 = _rows_before(uh, u, 2)
            cv = u2 * w[0:1] + u1 * w[1:2] + u * w[2:3]
            dz_t = dz_ref[:, cb].astype(F32)
            dcv = dz_t * b
            dcvn = jnp.where(at_end, 0.0, dzn_ref[:, cb].astype(F32) * pn_ref[:, cb].astype(F32))
            du = dcv * w[2:3] + _rows_after(dcv, dcvn, 1) * w[1:2] + _rows_after(dcv, dcvn, 2) * w[0:1]
            dp_ref[:, cb] = (dz_t * cv).astype(dp_ref.dtype)
            dp_ref[:, cc_] = (du * hh).astype(dp_ref.dtype)
            dp_ref[:, ch] = (du * c).astype(dp_ref.dtype)
            dcw_ref[0:1, cb] += jnp.sum(dcv * u2, axis=0, keepdims=True)
            dcw_ref[1:2, cb] += jnp.sum(dcv * u1, axis=0, keepdims=True)
            dcw_ref[2:3, cb] += jnp.sum(dcv * u, axis=0, keepdims=True)

    res = _pallas(
        body, comm=comm, name=name,
        out_shape=(jax.ShapeDtypeStruct((T, D3), BF16), jax.ShapeDtypeStruct((SUBLANES_F32, D), F32)),
        grid=(T // tm,),
        in_specs=[pl.BlockSpec((tm, D3), lambda i: (i, 0)), p_prev, p_next,
                  pl.BlockSpec((tm, D), lambda i: (i, 0)), dz_next, pl.BlockSpec((3, D), lambda i: (0, 0))],
        out_specs=(pl.BlockSpec((tm, D3), lambda i: (i, 0)), pl.BlockSpec((SUBLANES_F32, D), lambda i: (0, 0))),
        compiler_params=_params("arbitrary"),
    )(p, p, p, dz, dz, cw)
    if comm is None:
        return res[0], res[1][0:3]
    return res[0][0], res[0][1][0:3], res[1]


def _ffn_gate_fwd(u, cw, cb, seq, name, comm=None):
    T, F2 = u.shape
    F = F2 // 2
    tm = _tile(seq, 256, SUBLANES_BF16)
    cc = _tile(F, 256, LANES)
    prev, _ = _halo_specs(tm, F2, T)

    def body(u_ref, uh_ref, cw_ref, cb_ref, a_ref):
        at_start = (pl.program_id(0) * tm) % seq == 0

        def conv(c0):
            cols = slice(c0, c0 + cc)
            cur = u_ref[:, cols].astype(F32)
            halo = jnp.where(at_start, 0.0, uh_ref[:, cols].astype(F32))
            w = cw_ref[:, cols]
            return (_rows_before(halo, cur, 2) * w[0:1] + _rows_before(halo, cur, 1) * w[1:2] + cur * w[2:3]
                    + cb_ref[:, cols])

        for c0 in range(0, F, cc):
            g = conv(c0)
            up = conv(F + c0)
            a_ref[:, c0:c0 + cc] = ((g * jax.nn.sigmoid(g)) * up).astype(a_ref.dtype)

    return _pallas(
        body, comm=comm, name=name, out_shape=jax.ShapeDtypeStruct((T, F), BF16), grid=(T // tm,),
        in_specs=[pl.BlockSpec((tm, F2), lambda i: (i, 0)), prev,
                  pl.BlockSpec((3, F2), lambda i: (0, 0)), pl.BlockSpec((1, F2), lambda i: (0, 0))],
        out_specs=pl.BlockSpec((tm, F), lambda i: (i, 0)),
        compiler_params=_params("parallel"),
    )(u, u, cw, cb.reshape(1, F2))


def _ffn_gate_bwd(u, cw, cb, da, seq, name, comm=None):
    T, F2 = u.shape
    F = F2 // 2
    H = SUBLANES_BF16
    tm = _tile(seq, 256, H)
    cc = _tile(F, 256, LANES)
    u_prev, u_next = _halo_specs(tm, F2, T)
    _, da_next = _halo_specs(tm, F, T)

    def body(u_ref, up_ref, un_ref, da_ref, dan_ref, cw_ref, cb_ref, du_ref, acc_ref):
        i = pl.program_id(0)

        @pl.when(i == 0)
        def _():
            acc_ref[...] = jnp.zeros_like(acc_ref)

        at_start = (i * tm) % seq == 0
        at_end = ((i + 1) * tm) % seq == 0

        def conv_ext(cols):
            ext = jnp.concatenate([jnp.where(at_start, 0.0, up_ref[:, cols].astype(F32)),
                                   u_ref[:, cols].astype(F32), un_ref[:, cols].astype(F32)], axis=0)
            w = cw_ref[:, cols]
            e1 = pltpu.roll(ext, 1, 0)
            e2 = pltpu.roll(ext, 2, 0)
            out = (e2 * w[0:1] + e1 * w[1:2] + ext * w[2:3] + cb_ref[:, cols])[H:]
            return out, ext, e1, e2

        def back(d2, cols, ext, e1, e2):
            w = cw_ref[:, cols]
            n = tm + H
            d1n = pltpu.roll(d2, n - 1, 0)[:tm]
            d2n = pltpu.roll(d2, n - 2, 0)[:tm]
            d_t = d2[:tm]
            du_ref[:, cols] = (d_t * w[2:3] + d1n * w[1:2] + d2n * w[0:1]).astype(du_ref.dtype)
            acc_ref[0:1, cols] += jnp.sum(d_t * e2[H:H + tm], axis=0, keepdims=True)
            acc_ref[1:2, cols] += jnp.sum(d_t * e1[H:H + tm], axis=0, keepdims=True)
            acc_ref[2:3, cols] += jnp.sum(d_t * ext[H:H + tm], axis=0, keepdims=True)
            acc_ref[3:4, cols] += jnp.sum(d_t, axis=0, keepdims=True)

        for c0 in range(0, F, cc):
            gc, uc = slice(c0, c0 + cc), slice(F + c0, F + c0 + cc)
            g, g_ext, g_e1, g_e2 = conv_ext(gc)
            up, up_ext, up_e1, up_e2 = conv_ext(uc)
            da_ext = jnp.concatenate([da_ref[:, gc].astype(F32),
                                      jnp.where(at_end, 0.0, dan_ref[:, gc].astype(F32))], axis=0)
            sg = jax.nn.sigmoid(g)
            d_up = da_ext * (g * sg)
            d_g = da_ext * up * (sg * (1.0 + g * (1.0 - sg)))
            back(d_g, gc, g_ext, g_e1, g_e2)
            back(d_up, uc, up_ext, up_e1, up_e2)

    res = _pallas(
        body, comm=comm, name=name,
        out_shape=(jax.ShapeDtypeStruct((T, F2), BF16), jax.ShapeDtypeStruct((SUBLANES_F32, F2), F32)),
        grid=(T // tm,),
        in_specs=[pl.BlockSpec((tm, F2), lambda i: (i, 0)), u_prev, u_next,
                  pl.BlockSpec((tm, F), lambda i: (i, 0)), da_next,
                  pl.BlockSpec((3, F2), lambda i: (0, 0)), pl.BlockSpec((1, F2), lambda i: (0, 0))],
        out_specs=(pl.BlockSpec((tm, F2), lambda i: (i, 0)), pl.BlockSpec((SUBLANES_F32, F2), lambda i: (0, 0))),
        compiler_params=_params("arbitrary"),
    )(u, u, u, da, da, cw, cb.reshape(1, F2))
    (du, acc), landed = res if comm is not None else (res, None)
    return (du, acc[0:3], acc[3]) if comm is None else (du, acc[0:3], acc[3], landed)


def _bucket_map():
    P = ATT_BLOCK
    qi = np.arange(P, dtype=np.int64)[:, None]
    kc = np.arange(2 * P, dtype=np.int64)[None, :]
    delta = qi + P - kc
    maps = []
    max_exact = REL_BUCKETS // 2
    for window, dilation in DILATED_BRANCHES:
        band = (delta >= 0) & (delta <= window // dilation)
        n = np.maximum(delta * dilation, 0)
        nf = np.maximum(n, max_exact).astype(np.float32)
        large = max_exact + (np.log(nf / np.float32(max_exact)) / np.float32(math.log(REL_MAX_DISTANCE / max_exact))
                             * np.float32(REL_BUCKETS - max_exact)).astype(np.int32)
        large = np.minimum(large, REL_BUCKETS - 1)
        bucket = np.where(n < max_exact, n, large)
        maps.append(np.where(band, bucket, -1).astype(np.int32))
    return np.stack(maps)


def _bias_tables(rel_bias, bmap, name):
    n_pairs = rel_bias.shape[1] // 2
    nbr, P, P2 = bmap.shape

    def body(rb_ref, bm_ref, o_ref):
        pair = pl.program_id(0)
        in_seq = lax.broadcasted_iota(jnp.int32, (P, P2), 1) >= P
        for br in range(nbr):
            bm = bm_ref[br]
            for hh in range(2):
                acc = jnp.full((P, P2), MASKED_LOGIT, F32)
                for b in range(REL_BUCKETS):
                    acc = jnp.where(bm == b, rb_ref[b, 2 * pair + hh], acc)
                o_ref[br, 0, 0, hh * P:(hh + 1) * P, :] = acc
                o_ref[br, 0, 1, hh * P:(hh + 1) * P, :] = jnp.where(in_seq, acc, MASKED_LOGIT)

    return pl.pallas_call(
        body, name=name, out_shape=jax.ShapeDtypeStruct((nbr, n_pairs, 2, 2 * P, P2), F32), grid=(n_pairs,),
        in_specs=[pl.BlockSpec(memory_space=pltpu.SMEM), pl.BlockSpec((nbr, P, P2), lambda h: (0, 0, 0))],
        out_specs=pl.BlockSpec((nbr, 1, 2, 2 * P, P2), lambda h: (0, h, 0, 0, 0)),
        compiler_params=_params("parallel"),
    )(rel_bias, bmap)


def _bias_grad(dbias, bmap, name):
    nbr, n_pairs, _, P2 = dbias.shape
    P = P2 // 2

    def body(db_ref, bm_ref, o_ref):
        lane = lax.broadcasted_iota(jnp.int32, (1, LANES), 1)
        for hh in range(2):
            row = jnp.zeros((1, LANES), F32)
            for br in range(nbr):
                bm = bm_ref[br]
                d = db_ref[br, 0, hh * P:(hh + 1) * P, :]
                for b in range(REL_BUCKETS):
                    hit = jnp.sum(jnp.where(bm == b, d, 0.0), axis=1, keepdims=True)
                    row = row + jnp.where(lane == b, jnp.sum(hit, axis=0, keepdims=True), 0.0)
            o_ref[hh] = row

    return pl.pallas_call(
        body, name=name, out_shape=jax.ShapeDtypeStruct((2 * n_pairs, 1, LANES), F32), grid=(n_pairs,),
        in_specs=[pl.BlockSpec((nbr, 1, P2, P2), lambda h: (0, h, 0, 0)), pl.BlockSpec((nbr, P, P2), lambda h: (0, 0, 0))],
        out_specs=pl.BlockSpec((2, 1, LANES), lambda h: (h, 0, 0)),
        compiler_params=_params("parallel"),
    )(dbias, bmap)[:, 0, :]


def _rows(start, dilation):
    if dilation == 1:
        return pl.ds(pl.multiple_of(start, ATT_BLOCK), ATT_BLOCK)
    return pl.ds(start, ATT_BLOCK, stride=dilation)


def _for_each_block(seq, unroll, fn):
    P = ATT_BLOCK
    for br, (_, d) in enumerate(DILATED_BRANCHES):
        nb = seq // d // P
        total = nb * d
        u = unroll if total % unroll == 0 else 1

        def some(i, carry, br=br, d=d, nb=nb, u=u):
            blocks = []
            for k in range(u):
                idx = i * u + k
                r, j = idx // nb, idx % nb
                blocks.append((r + j * (d * P), r + jnp.maximum(j - 1, 0) * (d * P), jnp.where(j == 0, 1, 0)))
            fn(br, d, blocks)
            return carry

        lax.fori_loop(0, total // u, some, 0)


def _stack_heads(x, head0):
    return jnp.concatenate([jnp.where(head0, x, 0.0), jnp.where(head0, 0.0, x)], axis=0).astype(BF16)


def _window(ref, start, prev, dilation):
    return jnp.concatenate([ref[_rows(prev, dilation), :], ref[_rows(start, dilation), :]], axis=0).astype(BF16)


def _attn_fwd(q, kv, bias, name):
    B, S, D = q.shape
    P = ATT_BLOCK
    n_pairs = D // LANES
    nbr = len(DILATED_BRANCHES)
    scale = HEAD_DIM ** -0.5

    def body(q_ref, k_ref, v_ref, bias_ref, o_ref, lse_ref, *stats):
        m_s, l_s, acc_s = stats[0:nbr], stats[nbr:2 * nbr], stats[2 * nbr:3 * nbr]
        head0 = lax.broadcasted_iota(jnp.int32, (P, LANES), 1) < HEAD_DIM

        def block(br, d, blocks):
            s = [lax.dot_general(_stack_heads(q_ref[_rows(start, d), :] * scale, head0),
                                 _window(k_ref, start, prev, d), (((1,), (1,)), ((), ())),
                                 preferred_element_type=F32) + bias_ref[br, 0, first]
                 for start, prev, first in blocks]
            m = [jnp.max(x, axis=-1, keepdims=True) for x in s]
            p = [jnp.exp(x - y) for x, y in zip(s, m)]
            l = [jnp.sum(x, axis=-1, keepdims=True) for x in p]
            pv = [jnp.dot(x.astype(BF16), _window(v_ref, start, prev, d), preferred_element_type=F32)
                  for x, (start, prev, _) in zip(p, blocks)]
            for k, (start, _, _) in enumerate(blocks):
                rows = _rows(start, d)
                m_s[br][rows, :] = jnp.where(head0, m[k][:P], m[k][P:])
                l_s[br][rows, :] = jnp.where(head0, l[k][:P], l[k][P:])
                acc_s[br][rows, :] = jnp.where(head0, pv[k][:P], pv[k][P:])

        _for_each_block(S, ATTN_FWD_UNROLL, block)

        chunk = _tile(S, 256, SUBLANES_F32)

        def merge(i, carry):
            rows = pl.ds(pl.multiple_of(i * chunk, chunk), chunk)
            ms = [m_s[br][rows, :] for br in range(nbr)]
            m = functools.reduce(jnp.maximum, ms)
            l = jnp.zeros((chunk, LANES), F32)
            acc = jnp.zeros((chunk, LANES), F32)
            for br in range(nbr):
                w = jnp.exp(ms[br] - m)
                l = l + w * l_s[br][rows, :]
                acc = acc + w * acc_s[br][rows, :]
            o_ref[rows, :] = acc / l
            lse_ref[rows, :] = m + jnp.log(l)
            return carry

        lax.fori_loop(0, S // chunk, merge, 0)

    slab = lambda col0: pl.BlockSpec((None, S, LANES), lambda b, h: (b, 0, col0 + h))
    return pl.pallas_call(
        body, name=name,
        out_shape=(jax.ShapeDtypeStruct((B, S, D), F32), jax.ShapeDtypeStruct((B, S, D), F32)),
        grid=(B, n_pairs),
        in_specs=[slab(0), slab(0), slab(n_pairs),
                  pl.BlockSpec((nbr, 1, 2, 2 * P, 2 * P), lambda b, h: (0, h, 0, 0, 0))],
        out_specs=(slab(0), slab(0)),
        scratch_shapes=[pltpu.VMEM((S, LANES), F32)] * (3 * nbr),
        compiler_params=_params("parallel", "parallel"),
    )(q, kv, kv, bias)


def _attn_bwd(q, kv, o, lse, do, bias, name, comm=None):
    B, S, D = q.shape
    P = ATT_BLOCK
    n_pairs = D // LANES
    nbr = len(DILATED_BRANCHES)
    scale = HEAD_DIM ** -0.5

    def body(q_ref, k_ref, v_ref, o_ref, lse_ref, do_ref, bias_ref, dq_ref, dk_ref, dv_ref, dbias_ref, delta_s):
        head0 = lax.broadcasted_iota(jnp.int32, (P, LANES), 1) < HEAD_DIM

        @pl.when(pl.program_id(1) == 0)
        def _():
            dbias_ref[...] = jnp.zeros_like(dbias_ref)

        chunk = _tile(S, 512, SUBLANES_F32)

        def prepare(i, carry):
            rows = pl.ds(pl.multiple_of(i * chunk, chunk), chunk)
            x = do_ref[rows, :] * o_ref[rows, :]
            h0 = lax.broadcasted_iota(jnp.int32, (chunk, LANES), 1) < HEAD_DIM
            d0 = jnp.sum(jnp.where(h0, x, 0.0), axis=-1, keepdims=True)
            d1 = jnp.sum(jnp.where(h0, 0.0, x), axis=-1, keepdims=True)
            delta_s[rows, :] = jnp.where(h0, d0, d1)
            zero = jnp.zeros((chunk, LANES), F32)
            dq_ref[rows, :] = zero
            dk_ref[rows, :] = zero
            dv_ref[rows, :] = zero
            return carry

        lax.fori_loop(0, S // chunk, prepare, 0)

        def per_head(x):
            return jnp.concatenate([x[:, 0:1], x[:, HEAD_DIM:HEAD_DIM + 1]], axis=0)

        nt = (((1,), (1,)), ((), ()))
        tn = (((0,), (0,)), ((), ()))

        def block(br, d, blocks):
            q2 = [_stack_heads(q_ref[_rows(start, d), :] * scale, head0) for start, _, _ in blocks]
            do2 = [_stack_heads(do_ref[_rows(start, d), :], head0) for start, _, _ in blocks]
            kb = [_window(k_ref, start, prev, d) for start, prev, _ in blocks]
            vb = [_window(v_ref, start, prev, d) for start, prev, _ in blocks]
            s = [lax.dot_general(a, b, nt, preferred_element_type=F32) + bias_ref[br, 0, first]
                 for a, b, (_, _, first) in zip(q2, kb, blocks)]
            dp = [lax.dot_general(a, b, nt, preferred_element_type=F32) for a, b in zip(do2, vb)]
            p = [jnp.exp(x - per_head(lse_ref[_rows(start, d), :])) for x, (start, _, _) in zip(s, blocks)]
            ds = [x * (y - per_head(delta_s[_rows(start, d), :])) for x, y, (start, _, _) in zip(p, dp, blocks)]
            for x in ds:
                dbias_ref[br, 0] += x
            ds16 = [x.astype(BF16) for x in ds]
            dq2 = [jnp.dot(a, b, preferred_element_type=F32) for a, b in zip(ds16, kb)]
            dk = [lax.dot_general(a, b, tn, preferred_element_type=F32) for a, b in zip(ds16, q2)]
            dv = [lax.dot_general(a.astype(BF16), b, tn, preferred_element_type=F32) for a, b in zip(p, do2)]
            for k, (start, prev, _) in enumerate(blocks):
                rows, prows = _rows(start, d), _rows(prev, d)
                dq_ref[rows, :] += jnp.where(head0, dq2[k][:P], dq2[k][P:]) * scale
                dk_ref[prows, :] += dk[k][:P]
                dk_ref[rows, :] += dk[k][P:]
                dv_ref[prows, :] += dv[k][:P]
                dv_ref[rows, :] += dv[k][P:]

        _for_each_block(S, ATTN_BWD_UNROLL, block)

    slab = lambda col0: pl.BlockSpec((None, S, LANES), lambda h, b: (b, 0, col0 + h))
    tab = pl.BlockSpec((nbr, 1, 2, 2 * P, 2 * P), lambda h, b: (0, h, 0, 0, 0))
    dtab = pl.BlockSpec((nbr, 1, 2 * P, 2 * P), lambda h, b: (0, h, 0, 0))
    shp = jax.ShapeDtypeStruct((B, S, D), F32)
    return _pallas(
        body, comm=comm, name=name,
        out_shape=(shp, shp, shp, jax.ShapeDtypeStruct((nbr, n_pairs, 2 * P, 2 * P), F32)),
        grid=(n_pairs, B),
        in_specs=[slab(0), slab(0), slab(n_pairs), slab(0), slab(0), slab(0), tab],
        out_specs=(slab(0), slab(0), slab(0), dtab),
        scratch_shapes=[pltpu.VMEM((S, LANES), F32)],
        compiler_params=_params("parallel", "arbitrary"),
    )(q, kv, kv, o, lse, do, bias)


def _adamw(w, g, m, v, name):
    R, C = w.shape
    tr = _tile(R, 256, SUBLANES_F32) if R % SUBLANES_F32 == 0 else R
    tc = _tile(C, 2048, LANES) if C % LANES == 0 else C

    def body(w_ref, g_ref, m_ref, v_ref, d_ref, nm_ref, nv_ref):
        g_ = g_ref[...]
        m2 = ADAM_B1 * m_ref[...] + (1.0 - ADAM_B1) * g_
        v2 = ADAM_B2 * v_ref[...] + (1.0 - ADAM_B2) * (g_ * g_)
        m_hat = m2 / (1.0 - ADAM_B1 ** ADAM_STEP)
        v_hat = v2 / (1.0 - ADAM_B2 ** ADAM_STEP)
        d_ref[...] = -ADAM_LR * (m_hat / (jnp.sqrt(v_hat) + ADAM_EPS) + ADAM_WD * w_ref[...])
        nm_ref[...] = m2
        nv_ref[...] = v2

    blk = pl.BlockSpec((tr, tc), lambda i, j: (i, j))
    shp = jax.ShapeDtypeStruct((R, C), F32)
    return pl.pallas_call(
        body, name=name, out_shape=(shp, shp, shp), grid=(R // tr, C // tc),
        in_specs=[blk] * 4, out_specs=(blk,) * 3, compiler_params=_params("parallel", "parallel"),
    )(w, g, m, v)


def _sum_slots(slots, own, name):
    n, R, C = slots.shape
    tr = _tile(R, 256, SUBLANES_BF16) if R % SUBLANES_BF16 == 0 else R
    tc = _tile(C, 2048, LANES) if C % LANES == 0 else C

    def body(s_ref, *rest):
        o_ref = rest[-1]
        acc = s_ref[0].astype(F32)
        if own is not None:
            acc = rest[0][...].astype(F32) + acc
        for k in range(1, n):
            acc = acc + s_ref[k].astype(F32)
        o_ref[...] = acc

    in_specs = [pl.BlockSpec((n, tr, tc), lambda i, j: (0, i, j))]
    args = [slots]
    if own is not None:
        in_specs.append(pl.BlockSpec((tr, tc), lambda i, j: (i, j)))
        args.append(own)
    return pl.pallas_call(
        body, name=name, out_shape=jax.ShapeDtypeStruct((R, C), F32), grid=(R // tr, C // tc),
        in_specs=in_specs, out_specs=pl.BlockSpec((tr, tc), lambda i, j: (i, j)),
        compiler_params=_params("parallel", "parallel"),
    )(*args)


def _my_place():
    return lax.axis_index("x"), lax.axis_index("y"), lax.axis_index("c")


def _other_chips(x, y):
    return [(1 - x, y), (x, 1 - y), (1 - x, 1 - y)]


def _piece(ref, blk, axis, shard, half):
    h0 = blk[0] // 2
    idx = []
    for dim, n in enumerate(blk):
        if dim == 0:
            start = half * h0 + (shard * n if axis == 0 else 0)
            idx.append(pl.ds(start, h0))
        elif dim == axis:
            idx.append(pl.ds(shard * n, n))
        else:
            idx.append(slice(None))
    return ref.at[tuple(idx)]


def _half_of(ref, blk, half):
    return ref.at[pl.ds(half * (blk[0] // 2), blk[0] // 2)]


def _piece_start(blk, axis, shard, half):
    return tuple(half * (blk[0] // 2) + (shard * n if axis == 0 else 0) if dim == 0 else (shard * n if dim == axis else 0)
                 for dim, n in enumerate(blk))


def _gather_weights(shards, axes, names, small):
    n = len(shards)
    blks = [s.shape for s in shards]
    task = _gather_ici_task(shards, axes, names)

    def stage1(*refs):
        small_in, outs, small_out = refs[n], refs[n + 1:2 * n + 1], refs[2 * n + 1]
        send_sems, recv_sems, local_sems = refs[2 * n + 2:]
        sem = lambda k: (send_sems.at[k], recv_sems.at[k])
        x, y, c = _my_place()
        me = 2 * x + y
        local = pltpu.make_async_copy(small_in, small_out.at[me], local_sems.at[0])
        local.start()
        sends, recvs = task.copies(None, outs, sem)
        for k, (px, py) in enumerate(_other_chips(x, y)):
            sends.append(_remote(small_in, small_out.at[me], sem(task.n_sems + k), (px, py, c)))
            recvs.append(_remote(small_in, small_out.at[2 * px + py], sem(task.n_sems + k), (px, py, c)))
        for cp in sends:
            cp.start()
        for cp in recvs:
            cp.wait_recv()
        for cp in sends:
            cp.wait_send()
        local.wait()

    res = pl.pallas_call(
        stage1, name="gather_weights_ici",
        out_shape=task.outs + [jax.ShapeDtypeStruct((N_SHARDS,) + small.shape, small.dtype)],
        in_specs=[ANY] * (n + 1), out_specs=[ANY] * (n + 1), input_output_aliases={a: a for a in range(n)},
        scratch_shapes=[pltpu.SemaphoreType.DMA((task.n_sems + 3,)), pltpu.SemaphoreType.DMA((task.n_sems + 3,)),
                        pltpu.SemaphoreType.DMA((1,))],
    )(*task.ins, small)
    full = _run_comm(_gather_d2d_task(list(res[:n]), blks, axes), "gather_weights_d2d")
    return full, res[n]


def _run_comm(comm, name):
    n_in, n_out = len(comm.ins), len(comm.outs)

    def body(*refs):
        send_sems, recv_sems = refs[n_in + n_out:]
        sends, recvs = comm.copies(refs[:n_in], refs[n_in:n_in + n_out], lambda k: (send_sems.at[k], recv_sems.at[k]))
        for cp in sends:
            cp.start()
        for cp in recvs:
            cp.wait_recv()
        for cp in sends:
            cp.wait_send()

    return list(pl.pallas_call(
        body, name=name, out_shape=comm.outs, in_specs=[ANY] * n_in, out_specs=[ANY] * n_out,
        input_output_aliases=comm.aliases, scratch_shapes=[pltpu.SemaphoreType.DMA((comm.n_sems,))] * 2,
    )(*comm.ins))


def _scatter_grads(grads, blks, axes, small):
    n = len(grads)
    halves = [(b[0] // 2,) + tuple(b[1:]) for b in blks]

    def body(*refs):
        ins, small_in = refs[:n], refs[n]
        outs, small_out = refs[n + 1:2 * n + 1], refs[2 * n + 1]
        send_sems, recv_sems, local_sems = refs[2 * n + 2:]
        x, y, c = _my_place()
        me = 4 * x + 2 * y + c
        local = [pltpu.make_async_copy(small_in, small_out.at[me], local_sems.at[0])]
        for cp in local:
            cp.start()
        sends, recvs = [], []
        for rel in range(1, N_DEVICES):
            px = x ^ ((rel >> 2) & 1)
            py = y ^ ((rel >> 1) & 1)
            pc = c ^ (rel & 1)
            peer = 4 * px + 2 * py + pc
            for a in range(n + 1):
                if a < n:
                    src = _piece(ins[a], blks[a], axes[a], 2 * px + py, pc)
                    dst = land = outs[a].at[rel - 1]
                else:
                    src, dst, land = small_in, small_out.at[me], small_out.at[peer]
                sends.append(pltpu.make_async_remote_copy(
                    src_ref=src, dst_ref=dst, send_sem=send_sems.at[a, rel - 1], recv_sem=recv_sems.at[a, rel - 1],
                    device_id=(px, py, pc), device_id_type=pl.DeviceIdType.MESH))
                recvs.append(pltpu.make_async_remote_copy(
                    src_ref=src, dst_ref=land, send_sem=send_sems.at[a, rel - 1], recv_sem=recv_sems.at[a, rel - 1],
                    device_id=(px, py, pc), device_id_type=pl.DeviceIdType.MESH))
        for cp in sends:
            cp.start()
        for cp in recvs:
            cp.wait_recv()
        for cp in sends:
            cp.wait_send()
        for cp in local:
            cp.wait()

    out_shape = [jax.ShapeDtypeStruct((N_DEVICES - 1,) + h, g.dtype) for h, g in zip(halves, grads)]
    out_shape.append(jax.ShapeDtypeStruct((N_DEVICES,) + small.shape, small.dtype))
    res = pl.pallas_call(
        body, name="scatter_grads", out_shape=out_shape,
        in_specs=[ANY] * (n + 1), out_specs=[ANY] * (n + 1),
        scratch_shapes=[pltpu.SemaphoreType.DMA((n + 1, N_DEVICES - 1)), pltpu.SemaphoreType.DMA((n + 1, N_DEVICES - 1)),
                        pltpu.SemaphoreType.DMA((1,))],
    )(*grads, small)
    return list(res[:n]), res[n]


def _remote(src, dst, sems, device):
    return pltpu.make_async_remote_copy(src_ref=src, dst_ref=dst, send_sem=sems[0], recv_sem=sems[1],
                                        device_id=device, device_id_type=pl.DeviceIdType.MESH)


def _sum_piece(dest, slots, grad, blk, axis, layer, n_layers, name):
    r, c = blk
    h0 = r // 2
    tr = _tile(h0, 256, SUBLANES_BF16)
    tc = _tile(c, 2048, LANES)
    place = jnp.stack([2 * lax.axis_index("x") + lax.axis_index("y"), lax.axis_index("c")]).astype(jnp.int32)

    def body(p_ref, s_ref, g_ref, *rest):
        acc = g_ref[...].astype(F32) + s_ref[0].astype(F32)
        for k in range(1, N_DEVICES - 1):
            acc = acc + s_ref[k].astype(F32)
        rest[-1][...] = acc

    def g_map(i, j, p):
        return (p[1] * (h0 // tr) + (p[0] * (r // tr) if axis == 0 else 0) + i, (p[0] * (c // tc) if axis == 1 else 0) + j)

    in_specs = [pl.BlockSpec((N_DEVICES - 1, tr, tc), lambda i, j, p: (0, i, j)), pl.BlockSpec((tr, tc), g_map)]
    args = [place, slots, grad]
    if dest is not None:
        in_specs.append(ANY)
        args.append(dest)
    return pl.pallas_call(
        body, name=name, out_shape=jax.ShapeDtypeStruct((n_layers, r, c), F32),
        grid_spec=pltpu.PrefetchScalarGridSpec(
            num_scalar_prefetch=1, grid=(h0 // tr, c // tc), in_specs=in_specs,
            out_specs=pl.BlockSpec((None, tr, tc), lambda i, j, p: (layer, p[1] * (h0 // tr) + i, j))),
        input_output_aliases={3: 0} if dest is not None else {},
        compiler_params=_params("parallel", "parallel"),
    )(*args)


def _swap_halves(blocks):
    n = len(blocks)
    layers = [(a, l) for a, b in enumerate(blocks) for l in range(b.shape[0])]

    def copies(in_refs, out_refs, sem):
        x, y, c = _my_place()
        sends, recvs = [], []
        for k, (a, l) in enumerate(layers):
            blk = blocks[a].shape[1:]
            mine = _half_of(out_refs[a].at[l], blk, c)
            sends.append(_remote(mine, mine, sem(k), (x, y, 1 - c)))
            recvs.append(_remote(mine, _half_of(out_refs[a].at[l], blk, 1 - c), sem(k), (x, y, 1 - c)))
        return sends, recvs

    task = _Comm(blocks, [jax.ShapeDtypeStruct(b.shape, b.dtype) for b in blocks], {a: a for a in range(n)},
                 len(layers), copies)
    return _run_comm(task, "swap_grad_halves")


def _place_shard(shard, axis, name):
    r, c = shard.shape
    tr = _tile(r, 512, SUBLANES_BF16)
    full = (r * N_SHARDS, c) if axis == 0 else (r, c * N_SHARDS)
    me2 = (2 * lax.axis_index("x") + lax.axis_index("y")).astype(jnp.int32).reshape(1)

    def body(me_ref, s_ref, o_ref):
        o_ref[...] = s_ref[...].astype(o_ref.dtype)

    if axis == 0:
        out_map = lambda i, me: (me[0] * (r // tr) + i, 0)
    else:
        out_map = lambda i, me: (i, me[0])
    return pl.pallas_call(
        body, name=name, out_shape=jax.ShapeDtypeStruct(full, BF16),
        grid_spec=pltpu.PrefetchScalarGridSpec(
            num_scalar_prefetch=1, grid=(r // tr,),
            in_specs=[pl.BlockSpec((tr, c), lambda i, me: (i, 0))], out_specs=pl.BlockSpec((tr, c), out_map)),
        compiler_params=_params("parallel"),
    )(me2, shard)


def _gather_ici_task(shards, axes, names):
    n = len(shards)
    blks = [s.shape for s in shards]
    bases = [_place_shard(s, ax, f"place_{nm}") for s, ax, nm in zip(shards, axes, names)]

    def copies(in_refs, out_refs, sem):
        x, y, c = _my_place()
        me = 2 * x + y
        sends, recvs = [], []
        for a in range(n):
            mine = _piece(out_refs[a], blks[a], axes[a], me, c)
            for k, (px, py) in enumerate(_other_chips(x, y)):
                sends.append(_remote(mine, mine, sem(3 * a + k), (px, py, c)))
                recvs.append(_remote(mine, _piece(out_refs[a], blks[a], axes[a], 2 * px + py, c), sem(3 * a + k),
                                     (px, py, c)))
        return sends, recvs

    return _Comm(bases, [jax.ShapeDtypeStruct(b.shape, b.dtype) for b in bases], {a: a for a in range(n)}, 3 * n, copies)


def _gather_d2d_task(partials, blks, axes):
    n = len(partials)

    def copies(in_refs, out_refs, sem):
        x, y, c = _my_place()
        sends, recvs = [], []
        for a in range(n):
            for k, (px, py) in enumerate(_other_chips(x, y)):
                mine = _piece(out_refs[a], blks[a], axes[a], 2 * px + py, c)
                theirs = _piece(out_refs[a], blks[a], axes[a], 2 * px + py, 1 - c)
                sends.append(_remote(mine, mine, sem(3 * a + k), (x, y, 1 - c)))
                recvs.append(_remote(mine, theirs, sem(3 * a + k), (x, y, 1 - c)))
        return sends, recvs

    return _Comm(partials, [jax.ShapeDtypeStruct(p.shape, p.dtype) for p in partials], {a: a for a in range(n)},
                 3 * n, copies)


def _scatter_task(grads, blks, axes):
    n = len(grads)

    def copies(in_refs, out_refs, sem):
        x, y, c = _my_place()
        sends, recvs = [], []
        for rel in range(1, N_DEVICES):
            px, py, pc = x ^ ((rel >> 2) & 1), y ^ ((rel >> 1) & 1), c ^ (rel & 1)
            for a in range(n):
                src = _piece(in_refs[a], blks[a], axes[a], 2 * px + py, pc)
                k = (N_DEVICES - 1) * a + rel - 1
                sends.append(_remote(src, out_refs[a].at[rel - 1], sem(k), (px, py, pc)))
                recvs.append(_remote(src, out_refs[a].at[rel - 1], sem(k), (px, py, pc)))
        return sends, recvs

    outs = [jax.ShapeDtypeStruct((N_DEVICES - 1, b[0] // 2) + tuple(b[1:]), g.dtype) for b, g in zip(blks, grads)]
    return _Comm(grads, outs, {}, (N_DEVICES - 1) * n, copies)


def _pack(arrays):
    flat = jnp.concatenate([a.reshape(-1).astype(F32) for a in arrays])
    pad = (-flat.shape[0]) % (SUBLANES_F32 * LANES)
    return jnp.pad(flat, (0, pad)).reshape(-1, LANES)


def _unpack(packed, shapes):
    flat = packed.reshape(-1)
    out, off = [], 0
    for s in shapes:
        n = int(np.prod(s))
        out.append(flat[off:off + n].reshape(s))
        off += n
    return out


def _local_step(x, target, W, shards, geom, small):
    W = dict(W)
    B, S, D = x.shape
    T = B * S
    x2 = x.reshape(T, D)
    tgt = target.reshape(T, D)
    bmap = jnp.asarray(_bucket_map())
    blk = lambda names: [geom[k][0] for k in names]
    axs = lambda names: [geom[k][1] for k in names]
    ici = lambda names: _gather_ici_task([shards[k] for k in names], axs(names), names)
    d2d = lambda names, partials: _gather_d2d_task(list(partials), blk(names), axs(names))
    big, slots = {}, {}

    def scatter(names):
        return _scatter_task([big[k] for k in names], blk(names), axs(names))

    def ffn_bwd(l, dout, h, saved, first, second):
        xn, u, a = saved
        cw, cb = small["ffn_conv"][l], small["ffn_conv_b"][l]
        da = _mm_nt(dout, W[f"w_down{l}"], BF16, f"ffn{l}_down_dx")
        big[f"w_down{l}"] = _mm_tn(a, dout, f"ffn{l}_down_dw")
        names = first + [f"w_down{l}"]
        du, g_cw, g_cb, landed = _ffn_gate_bwd(u, cw, cb, da, S, f"ffn{l}_gate_bwd", comm=scatter(names))
        slots.update(zip(names, landed))
        big[f"w_up{l}"] = _mm_tn(xn, du, f"ffn{l}_up_dw")
        norm = (h, small["ffn_norm"][l], dout)
        if second:
            (dh, g_norm), (slots[f"w_up{l}"],) = _mm_nt(du, W[f"w_up{l}"], None, f"ffn{l}_up_dx",
                                                        comm=scatter([f"w_up{l}"]), norm=norm)
        else:
            dh, g_norm = _mm_nt(du, W[f"w_up{l}"], None, f"ffn{l}_up_dx", norm=norm)
        return dh, g_cw, g_cb, g_norm[0]

    xn0 = _rmsnorm_fwd(x2, small["a_norm"][0], "a_norm")
    p, part = _mm_nn(xn0, W["w_in"], None, BF16, "a_in", comm=ici(["w_up0"]))
    z, (W["w_up0"],) = _gate_a_fwd(p, small["a_conv"][0], S, "a_gate", comm=d2d(["w_up0"], part))
    (h1, xn1), part = _mm_nn(z, W["w_out"], x2, F32, "a_out", comm=ici(["w_down0"]), norm_gains=[small["ffn_norm"][0]])
    later = ["w_kv", "w_q", "w_o", "w_up1"]
    u0, landed = _mm_nn(xn1, W["w_up0"], None, BF16, "ffn0_up", comm=_Comm.join([d2d(["w_down0"], part), ici(later)]))
    W["w_down0"] = landed[0]
    a0, landed = _ffn_gate_fwd(u0, small["ffn_conv"][0], small["ffn_conv_b"][0], S, "ffn0_gate",
                               comm=_Comm.join([d2d(later, landed[1:]), ici(["w_down1"])]))
    W.update(zip(later, landed[:len(later)]))
    (h2, kvn, xn3), (W["w_down1"],) = _mm_nn(a0, W["w_down0"], h1, F32, "ffn0_down",
                                            comm=d2d(["w_down1"], landed[len(later):]),
                                            norm_gains=[small["kv_norm"], small["b_norm"][0]])
    kv = _mm_nn(kvn, W["w_kv"], None, F32, "kv_proj")
    q = _mm_nn(xn3, W["w_q"], None, F32, "q_proj")
    bias = _bias_tables(small["rel_bias"], bmap, "rel_bias_tables")
    q3, kv3 = q.reshape(B, S, D), kv.reshape(B, S, 2 * D)
    o3, lse3 = _attn_fwd(q3, kv3, bias, "attn_fwd")
    o = o3.reshape(T, D)
    h3, xn4 = _mm_nn(o, W["w_o"], h2, F32, "o_proj", norm_gains=[small["ffn_norm"][1]])
    u1 = _mm_nn(xn4, W["w_up1"], None, BF16, "ffn1_up")
    a1 = _ffn_gate_fwd(u1, small["ffn_conv"][1], small["ffn_conv_b"][1], S, "ffn1_gate")
    h4 = _mm_nn(a1, W["w_down1"], h3, F32, "ffn1_down")
    sq_err, dh4, g_final = _loss_head(h4, small["final_norm"], tgt, "loss_head")
    loss = 0.5 * jnp.sum(sq_err) / D

    dh3, g_cw1, g_cb1, g_fn1 = ffn_bwd(1, dh4, h3, (xn4, u1, a1), [], False)
    do = _mm_nt(dh3, W["w_o"], F32, "o_proj_dx")
    big["w_o"] = _mm_tn(o, dh3, "o_proj_dw")
    (dq3, dk3, dv3, dbias), landed = _attn_bwd(q3, kv3, o3, lse3, do.reshape(B, S, D), bias, "attn_bwd",
                                               comm=scatter(["w_up1", "w_o"]))
    slots.update(zip(["w_up1", "w_o"], landed))
    g_rel = _bias_grad(dbias, bmap, "rel_bias_grad")[:, :REL_BUCKETS].T
    dq, dk, dv = dq3.reshape(T, D), dk3.reshape(T, D), dv3.reshape(T, D)
    dh2, g_bn = _mm_nt(dq, W["w_q"], None, "q_proj_dx", norm=(h2, small["b_norm"][0], dh3))
    big["w_q"] = _mm_tn(xn3, dq, "q_proj_dw")
    dh2, g_kvn = _mm_nt([dk, dv], W["w_kv"], None, "kv_proj_dx", norm=(h2, small["kv_norm"], dh2))
    big["w_kv"] = jnp.concatenate([_mm_tn(kvn, dk, "k_proj_dw"), _mm_tn(kvn, dv, "v_proj_dw")], axis=1)
    dh1, g_cw0, g_cb0, g_fn0 = ffn_bwd(0, dh2, h1, (xn1, u0, a0), ["w_q", "w_kv"], True)
    dz = _mm_nt(dh1, W["w_out"], BF16, "a_out_dx")
    big["w_out"] = _mm_tn(z, dh1, "a_out_dw")
    dp, g_aconv, (slots["w_out"],) = _gate_a_bwd(p, small["a_conv"][0], dz, S, "a_gate_bwd", comm=scatter(["w_out"]))
    big["w_in"] = _mm_tn(xn0, dp, "a_in_dw")
    (dx, g_an), (slots["w_in"],) = _mm_nt(dp, W["w_in"], None, "a_in_dx", comm=scatter(["w_in"]),
                                          norm=(x2, small["a_norm"][0], dh1))
    g_bn, g_kvn, g_an = g_bn[0], g_kvn[0], g_an[0]

    small_g = {"a_norm": g_an[None], "a_conv": g_aconv[None], "kv_norm": g_kvn, "b_norm": g_bn[None],
               "rel_bias": g_rel, "ffn_norm": jnp.stack([g_fn0, g_fn1]), "ffn_conv": jnp.stack([g_cw0, g_cw1]),
               "ffn_conv_b": jnp.stack([g_cb0, g_cb1]), "final_norm": g_final}
    return loss, dx.reshape(B, S, D), big, slots, small_g


BIG = ("w_in", "w_out", "w_kv", "w_q", "w_o", "w_up0", "w_up1", "w_down0", "w_down1")
SMALL = ("a_norm", "a_conv", "kv_norm", "b_norm", "rel_bias", "ffn_norm", "ffn_conv", "ffn_conv_b", "final_norm")
SMALL_SHARDED = ("a_norm", "a_conv", "ffn_conv")
WEIGHT_ORDER = ("a_norm", "a_w_in", "a_conv", "a_w_out", "kv_norm", "w_kv", "b_norm", "b_w_q", "b_w_o", "rel_bias",
                "ffn_norm", "ffn_w_up", "ffn_conv", "ffn_conv_b", "ffn_w_down", "final_norm")
GRAD_OF = {"w_in": ("a_w_in", 0), "w_out": ("a_w_out", 0), "w_kv": ("w_kv", 0), "w_q": ("b_w_q", 0), "w_o": ("b_w_o", 0),
           "w_up0": ("ffn_w_up", 0), "w_up1": ("ffn_w_up", 1), "w_down0": ("ffn_w_down", 0), "w_down1": ("ffn_w_down", 1)}


def _as2d(a):
    return a.reshape(-1, a.shape[-1])


def kernel(x, a_norm, a_w_in, a_conv, a_w_out, kv_norm, w_kv, b_norm, b_w_q, b_w_o, rel_bias, ffn_norm, ffn_w_up, ffn_conv, ffn_conv_b, ffn_w_down, final_norm, loss_target, m_a_norm, m_a_w_in, m_a_conv, m_a_w_out, m_kv_norm, m_w_kv, m_b_norm, m_b_w_q, m_b_w_o, m_rel_bias, m_ffn_norm, m_ffn_w_up, m_ffn_conv, m_ffn_conv_b, m_ffn_w_down, m_final_norm, v_a_norm, v_a_w_in, v_a_conv, v_a_w_out, v_kv_norm, v_w_kv, v_b_norm, v_b_w_q, v_b_w_o, v_rel_bias, v_ffn_norm, v_ffn_w_up, v_ffn_conv, v_ffn_conv_b, v_ffn_w_down, v_final_norm):
    given = dict(a_norm=a_norm, a_w_in=a_w_in, a_conv=a_conv, a_w_out=a_w_out, kv_norm=kv_norm, w_kv=w_kv, b_norm=b_norm,
                 b_w_q=b_w_q, b_w_o=b_w_o, rel_bias=rel_bias, ffn_norm=ffn_norm, ffn_w_up=ffn_w_up, ffn_conv=ffn_conv,
                 ffn_conv_b=ffn_conv_b, ffn_w_down=ffn_w_down, final_norm=final_norm)
    mom_m = dict(a_norm=m_a_norm, a_w_in=m_a_w_in, a_conv=m_a_conv, a_w_out=m_a_w_out, kv_norm=m_kv_norm, w_kv=m_w_kv,
                 b_norm=m_b_norm, b_w_q=m_b_w_q, b_w_o=m_b_w_o, rel_bias=m_rel_bias, ffn_norm=m_ffn_norm,
                 ffn_w_up=m_ffn_w_up, ffn_conv=m_ffn_conv, ffn_conv_b=m_ffn_conv_b, ffn_w_down=m_ffn_w_down,
                 final_norm=m_final_norm)
    mom_v = dict(a_norm=v_a_norm, a_w_in=v_a_w_in, a_conv=v_a_conv, a_w_out=v_a_w_out, kv_norm=v_kv_norm, w_kv=v_w_kv,
                 b_norm=v_b_norm, b_w_q=v_b_w_q, b_w_o=v_b_w_o, rel_bias=v_rel_bias, ffn_norm=v_ffn_norm,
                 ffn_w_up=v_ffn_w_up, ffn_conv=v_ffn_conv, ffn_conv_b=v_ffn_conv_b, ffn_w_down=v_ffn_w_down,
                 final_norm=v_final_norm)

    shard = {"w_in": (a_w_in[0], 1), "w_out": (a_w_out[0], 0), "w_kv": (w_kv, 1), "w_q": (b_w_q[0], 0),
             "w_o": (b_w_o[0], 0), "w_up0": (ffn_w_up[0], 1), "w_up1": (ffn_w_up[1], 1),
             "w_down0": (ffn_w_down[0], 0), "w_down1": (ffn_w_down[1], 0)}
    blks = [shard[k][0].shape for k in BIG]
    axes = [shard[k][1] for k in BIG]

    small_sharded = [given[k] for k in SMALL_SHARDED]
    packed = _pack(small_sharded)
    first = ("w_in", "w_out")
    fulls, packed_all = _gather_weights([shard[k][0] for k in first], [shard[k][1] for k in first], first, packed)
    W = dict(zip(first, fulls))
    later = {k: shard[k][0] for k in BIG if k not in first}
    geom = {k: (shard[k][0].shape, shard[k][1]) for k in BIG}
    small = {k: given[k] for k in SMALL}
    per_shard = [_unpack(packed_all[j], [a.shape for a in small_sharded]) for j in range(N_SHARDS)]
    for i, k in enumerate(SMALL_SHARDED):
        small[k] = jnp.concatenate([per_shard[j][i] for j in range(N_SHARDS)], axis=-1)

    loss, grad_x, big_g, slots, small_g = _local_step(x, loss_target, W, later, geom, small)
    loss = lax.psum(loss, MESH_AXES)

    small_shapes = [small_g[k].shape for k in SMALL]
    _, small_slots = _scatter_grads([], [], [], _pack([small_g[k] for k in SMALL]))
    layers_of = {}
    for k in BIG:
        layers_of.setdefault(GRAD_OF[k][0], []).append(k)
    blocks = {}
    for name, members in layers_of.items():
        dest = None
        for k in members:
            dest = _sum_piece(dest, slots[k], big_g[k], geom[k][0], geom[k][1], GRAD_OF[k][1], len(members), f"sum_{k}")
        blocks[name] = dest
    reduced = dict(zip(blocks, _swap_halves(list(blocks.values()))))
    small_sum = _sum_slots(small_slots, None, "sum_small")
    small_red = dict(zip(SMALL, _unpack(small_sum, small_shapes)))
    j = 2 * lax.axis_index("x") + lax.axis_index("y")
    for k in SMALL_SHARDED:
        w = given[k].shape[-1]
        small_red[k] = lax.dynamic_slice_in_dim(small_red[k], j * w, w, axis=small_red[k].ndim - 1)

    grads, deltas, new_m, new_v = {}, {}, {}, {}
    for name in WEIGHT_ORDER:
        if name in reduced:
            g = reduced[name].reshape(given[name].shape)
            d, nm, nv = _adamw(_as2d(given[name]), _as2d(g), _as2d(mom_m[name]), _as2d(mom_v[name]), f"adamw_{name}")
            grads[name] = g
            deltas[name], new_m[name], new_v[name] = (t.reshape(given[name].shape) for t in (d, nm, nv))
    small_names = [n for n in WEIGHT_ORDER if n not in reduced]
    for n in small_names:
        grads[n] = small_red[n].reshape(given[n].shape)
    sw, sg, sm, sv = (_pack([d[n] for n in small_names]) for d in (given, grads, mom_m, mom_v))
    d, nm, nv = _adamw(sw, sg, sm, sv, "adamw_small")
    shapes = [given[n].shape for n in small_names]
    for n, a, b_, c_ in zip(small_names, _unpack(d, shapes), _unpack(nm, shapes), _unpack(nv, shapes)):
        deltas[n], new_m[n], new_v[n] = a, b_, c_

    return (loss, grad_x, *[grads[n] for n in WEIGHT_ORDER], *[deltas[n] for n in WEIGHT_ORDER],
            *[new_m[n] for n in WEIGHT_ORDER], *[new_v[n] for n in WEIGHT_ORDER])
```

```python
import functools
import math

import numpy as np

import jax
import jax.numpy as jnp
from jax import lax
from jax.experimental import pallas as pl
from jax.experimental.pallas import tpu as pltpu

F32 = jnp.float32
BF16 = jnp.bfloat16

RMS_EPS = 1e-6
HEAD_DIM = 64
ATT_BLOCK = 128
DILATED_BRANCHES = ((128, 1), (512, 4), (2048, 16))
REL_BUCKETS = 32
REL_MAX_DISTANCE = 2048
MASKED_LOGIT = -1e30
ATTN_FWD_UNROLL = 4
ATTN_BWD_UNROLL = 4

ADAM_LR = 0.001
ADAM_B1 = 0.9
ADAM_B2 = 0.999
ADAM_EPS = 1e-08
ADAM_WD = 0.01
ADAM_STEP = 10

LANES = 128
SUBLANES_F32 = 8
SUBLANES_BF16 = 16
VMEM_LIMIT_BYTES = 56 * 1024 * 1024

MESH_AXES = ("x", "y", "c")
N_SHARDS = 4
N_DEVICES = 8
ANY = pl.BlockSpec(memory_space=pl.ANY)


def _tile(n, pref, mult):
    best = None
    for t in range(mult, min(n, pref) + 1, mult):
        if n % t == 0:
            best = t
    if best is None:
        raise ValueError(f"no tile for {n} (multiple of {mult}, at most {pref})")
    return best


def _params(*sem):
    return pltpu.CompilerParams(dimension_semantics=sem, vmem_limit_bytes=VMEM_LIMIT_BYTES)


class _Comm:
    def __init__(self, ins, outs, aliases, n_sems, copies):
        self.ins, self.outs, self.aliases, self.n_sems, self.copies = list(ins), list(outs), dict(aliases), n_sems, copies

    @staticmethod
    def join(parts):
        parts = [p for p in parts if p is not None]
        ins, outs, aliases, offs, n_sems = [], [], {}, [], 0
        for p in parts:
            offs.append((len(ins), len(outs), n_sems))
            aliases.update({len(ins) + i: len(outs) + o for i, o in p.aliases.items()})
            ins += p.ins
            outs += p.outs
            n_sems += p.n_sems

        def copies(in_refs, out_refs, sem):
            sends, recvs = [], []
            for p, (i0, o0, s0) in zip(parts, offs):
                s, r = p.copies(in_refs[i0:i0 + len(p.ins)], out_refs[o0:o0 + len(p.outs)],
                                lambda k, s0=s0: sem(s0 + k))
                sends += s
                recvs += r
            return sends, recvs

        return _Comm(ins, outs, aliases, n_sems, copies)


def _pallas(body, *, comm=None, name, out_shape, grid=(), in_specs=(), out_specs=(), scratch_shapes=(),
            compiler_params=None):
    if comm is None:
        return pl.pallas_call(body, name=name, out_shape=out_shape, grid=grid, in_specs=in_specs, out_specs=out_specs,
                              scratch_shapes=scratch_shapes, compiler_params=compiler_params)
    single = not isinstance(out_shape, (tuple, list))
    outs = (out_shape,) if single else tuple(out_shape)
    o_specs = (out_specs,) if single else tuple(out_specs)
    n_in, n_cin, n_out, n_cout, n_scr = len(in_specs), len(comm.ins), len(outs), len(comm.outs), len(scratch_shapes)

    def carried(*refs):
        base_in, c_in = refs[:n_in], refs[n_in:n_in + n_cin]
        o0 = n_in + n_cin
        base_out, c_out = refs[o0:o0 + n_out], refs[o0 + n_out:o0 + n_out + n_cout]
        s0 = o0 + n_out + n_cout
        base_scr, (send_sems, recv_sems) = refs[s0:s0 + n_scr], refs[s0 + n_scr:]
        sem = lambda k: (send_sems.at[k], recv_sems.at[k])
        first = functools.reduce(jnp.logical_and, [pl.program_id(a) == 0 for a in range(len(grid))])
        last = functools.reduce(jnp.logical_and, [pl.program_id(a) == n - 1 for a, n in enumerate(grid)])

        @pl.when(first)
        def _():
            for cp in comm.copies(c_in, c_out, sem)[0]:
                cp.start()

        body(*base_in, *base_out, *base_scr)

        @pl.when(last)
        def _():
            sends, recvs = comm.copies(c_in, c_out, sem)
            for cp in recvs:
                cp.wait_recv()
            for cp in sends:
                cp.wait_send()

    call = pl.pallas_call(
        carried, name=name, out_shape=outs + tuple(comm.outs), grid=grid,
        in_specs=list(in_specs) + [ANY] * n_cin, out_specs=o_specs + (ANY,) * n_cout,
        scratch_shapes=list(scratch_shapes) + [pltpu.SemaphoreType.DMA((comm.n_sems,))] * 2,
        input_output_aliases={n_in + i: n_out + o for i, o in comm.aliases.items()},
        compiler_params=_params(*(["arbitrary"] * len(grid))))

    def run(*args):
        res = call(*args, *comm.ins)
        base = res[0] if single else tuple(res[:n_out])
        return base, list(res[n_out:])

    return run


def _rmsnorm_fwd(x, g, name, comm=None):
    T, D = x.shape
    tm = _tile(T, 512, SUBLANES_BF16)

    def body(x_ref, g_ref, o_ref):
        xf = x_ref[...]
        r = lax.rsqrt(jnp.mean(xf * xf, axis=-1, keepdims=True) + RMS_EPS)
        o_ref[...] = ((xf * r) * g_ref[...]).astype(o_ref.dtype)

    return _pallas(
        body, comm=comm, name=name, out_shape=jax.ShapeDtypeStruct((T, D), BF16), grid=(T // tm,),
        in_specs=[pl.BlockSpec((tm, D), lambda i: (i, 0)), pl.BlockSpec((1, D), lambda i: (0, 0))],
        out_specs=pl.BlockSpec((tm, D), lambda i: (i, 0)),
        compiler_params=_params("parallel"),
    )(x, g.reshape(1, D))


def _loss_head(h, g, target, name):
    T, D = h.shape
    tm = _tile(T, 512, SUBLANES_F32)

    def body(h_ref, g_ref, t_ref, dh_ref, acc_ref):
        @pl.when(pl.program_id(0) == 0)
        def _():
            acc_ref[...] = jnp.zeros_like(acc_ref)

        xf = h_ref[...]
        r = lax.rsqrt(jnp.mean(xf * xf, axis=-1, keepdims=True) + RMS_EPS)
        xhat = xf * r
        err = xhat * g_ref[...] - t_ref[...]
        dy = err * (1.0 / D)
        acc_ref[0:1, :] += jnp.sum(dy * xhat, axis=0, keepdims=True)
        acc_ref[1:2, :] += jnp.sum(err * err, axis=0, keepdims=True)
        t = dy * g_ref[...]
        dh_ref[...] = r * (t - xhat * jnp.mean(t * xhat, axis=-1, keepdims=True))

    row = pl.BlockSpec((tm, D), lambda i: (i, 0))
    dh, acc = pl.pallas_call(
        body, name=name,
        out_shape=(jax.ShapeDtypeStruct((T, D), F32), jax.ShapeDtypeStruct((SUBLANES_F32, D), F32)),
        grid=(T // tm,),
        in_specs=[row, pl.BlockSpec((1, D), lambda i: (0, 0)), row],
        out_specs=(row, pl.BlockSpec((SUBLANES_F32, D), lambda i: (0, 0))),
        compiler_params=_params("arbitrary"),
    )(h, g.reshape(1, D), target)
    return acc[1], dh, acc[0]


def _mm_nn(a, b, res, out_dtype, name, comm=None, norm_gains=()):
    T, K = a.shape
    N = b.shape[1]
    tm = _tile(T, 512, SUBLANES_BF16)
    tn = _tile(N, 3072, LANES)
    n_g = len(norm_gains)
    assert not n_g or tn == N, "the fused rmsnorm needs whole rows in one tile"

    def body(a_ref, b_ref, *rest):
        ins, outs = rest[:len(rest) - 1 - n_g], rest[len(rest) - 1 - n_g:]
        acc = jnp.dot(a_ref[...].astype(BF16), b_ref[...], preferred_element_type=F32)
        if res is not None:
            acc = acc + ins[0][...]
        outs[0][...] = acc.astype(outs[0].dtype)
        if n_g:
            y = acc * lax.rsqrt(jnp.mean(acc * acc, axis=-1, keepdims=True) + RMS_EPS)
            for g_ref, xn_ref in zip(ins[len(ins) - n_g:], outs[1:]):
                xn_ref[...] = (y * g_ref[...]).astype(xn_ref.dtype)

    tile = pl.BlockSpec((tm, tn), lambda j, i: (i, j))
    in_specs = [pl.BlockSpec((tm, K), lambda j, i: (i, 0)), pl.BlockSpec((K, tn), lambda j, i: (0, j))]
    args = [a, b]
    if res is not None:
        in_specs.append(tile)
        args.append(res)
    for g in norm_gains:
        in_specs.append(pl.BlockSpec((1, N), lambda j, i: (0, 0)))
        args.append(g.reshape(1, N))
    out_shape = jax.ShapeDtypeStruct((T, N), out_dtype)
    if n_g:
        out_shape = (out_shape,) + (jax.ShapeDtypeStruct((T, N), BF16),) * n_g
    return _pallas(
        body, comm=comm, name=name, out_shape=out_shape, grid=(N // tn, T // tm),
        in_specs=in_specs, out_specs=(tile,) * (1 + n_g) if n_g else tile,
        compiler_params=_params("parallel", "parallel"),
    )(*args)


def _mm_nt(dy, b, out_dtype, name, comm=None, norm=None):
    dys = list(dy) if isinstance(dy, (list, tuple)) else [dy]
    T, n_each = dys[0].shape
    K = b.shape[0]
    tm = _tile(T, 1024 if norm is None else 512, SUBLANES_BF16)
    tk = _tile(K, 1536, LANES)
    tn = _tile(n_each, 2816, LANES)
    per = n_each // tn
    n_steps = per * len(dys)
    assert norm is None or tk == K, "the fused rmsnorm backward needs whole rows in one tile"

    def body(*refs):
        dy_refs, b_ref, acc_ref = refs[:len(dys)], refs[len(dys)], refs[-1]
        i, n = pl.program_id(0), pl.program_id(2)

        @pl.when(n == 0)
        def _():
            acc_ref[...] = jnp.zeros_like(acc_ref)

        for p, dy_ref in enumerate(dy_refs):
            @pl.when(jnp.logical_and(n >= p * per, n < (p + 1) * per))
            def _(dy_ref=dy_ref):
                acc_ref[...] += lax.dot_general(dy_ref[...].astype(BF16), b_ref[...], (((1,), (1,)), ((), ())),
                                                preferred_element_type=F32)

        if norm is None:
            @pl.when(n == n_steps - 1)
            def _():
                refs[-2][...] = acc_ref[...].astype(refs[-2].dtype)
        else:
            x_ref, g_ref, dres_ref, dx_ref, dg_ref = refs[len(dys) + 1:-1]

            @pl.when(jnp.logical_and(i == 0, n == 0))
            def _():
                dg_ref[...] = jnp.zeros_like(dg_ref)

            @pl.when(n == n_steps - 1)
            def _():
                xf = x_ref[...]
                r = lax.rsqrt(jnp.mean(xf * xf, axis=-1, keepdims=True) + RMS_EPS)
                xhat = xf * r
                d = acc_ref[...]
                dg_ref[0:1, :] += jnp.sum(d * xhat, axis=0, keepdims=True)
                t = d * g_ref[...]
                dx_ref[...] = dres_ref[...] + r * (t - xhat * jnp.mean(t * xhat, axis=-1, keepdims=True))

    in_specs = [pl.BlockSpec((tm, tn), lambda i, k, n, p=p: (i, jnp.clip(n - p * per, 0, per - 1))) for p in range(len(dys))]
    in_specs.append(pl.BlockSpec((tk, tn), lambda i, k, n: (k, n)))
    args = dys + [b]
    tile = pl.BlockSpec((tm, tk), lambda i, k, n: (i, k))
    if norm is None:
        out_shape, out_specs = jax.ShapeDtypeStruct((T, K), out_dtype), tile
    else:
        x, g, dres = norm
        in_specs += [tile, pl.BlockSpec((1, K), lambda i, k, n: (0, 0)), tile]
        args += [x, g.reshape(1, K), dres]
        out_shape = (jax.ShapeDtypeStruct((T, K), F32), jax.ShapeDtypeStruct((SUBLANES_F32, K), F32))
        out_specs = (tile, pl.BlockSpec((SUBLANES_F32, K), lambda i, k, n: (0, 0)))
    return _pallas(
        body, comm=comm, name=name, out_shape=out_shape, grid=(T // tm, K // tk, n_steps),
        in_specs=in_specs, out_specs=out_specs, scratch_shapes=[pltpu.VMEM((tm, tk), F32)],
        compiler_params=_params("parallel", "parallel", "arbitrary") if norm is None else _params(*["arbitrary"] * 3),
    )(*args)


def _mm_tn(a, dy, name):
    T, K = a.shape
    N = dy.shape[1]
    tt = _tile(T, 2048, SUBLANES_BF16)
    tk = _tile(K, 1536, LANES)
    tn = _tile(N, 1536, LANES)
    t_steps = T // tt

    def body(a_ref, dy_ref, o_ref, acc_ref):
        t = pl.program_id(2)

        @pl.when(t == 0)
        def _():
            acc_ref[...] = jnp.zeros_like(acc_ref)

        acc_ref[...] += lax.dot_general(a_ref[...].astype(BF16), dy_ref[...].astype(BF16),
                                        (((0,), (0,)), ((), ())), preferred_element_type=F32)

        @pl.when(t == t_steps - 1)
        def _():
            o_ref[...] = acc_ref[...].astype(o_ref.dtype)

    return pl.pallas_call(
        body, name=name, out_shape=jax.ShapeDtypeStruct((K, N), BF16), grid=(K // tk, N // tn, t_steps),
        in_specs=[pl.BlockSpec((tt, tk), lambda k, n, t: (t, k)), pl.BlockSpec((tt, tn), lambda k, n, t: (t, n))],
        out_specs=pl.BlockSpec((tk, tn), lambda k, n, t: (k, n)),
        scratch_shapes=[pltpu.VMEM((tk, tn), F32)],
        compiler_params=_params("parallel", "parallel", "arbitrary"),
    )(a, dy)


def _rows_before(halo, cur, k):
    h = halo.shape[0]
    return pltpu.roll(jnp.concatenate([halo, cur], axis=0), k, 0)[h:]


def _rows_after(cur, halo, k):
    n = cur.shape[0]
    total = n + halo.shape[0]
    return pltpu.roll(jnp.concatenate([cur, halo], axis=0), total - k, 0)[:n]


def _halo_specs(tm, width, n_rows):
    per = tm // SUBLANES_BF16
    last = n_rows // SUBLANES_BF16 - 1
    prev = pl.BlockSpec((SUBLANES_BF16, width), lambda i: (jnp.maximum(i * per - 1, 0), 0))
    nxt = pl.BlockSpec((SUBLANES_BF16, width), lambda i: (jnp.minimum((i + 1) * per, last), 0))
    return prev, nxt


def _gate_a_fwd(p, cw, seq, name, comm=None):
    T, D3 = p.shape
    D = D3 // 3
    tm = _tile(seq, 512, SUBLANES_BF16)
    cc = _tile(D, 256, LANES)
    prev, _ = _halo_specs(tm, D3, T)

    def body(p_ref, ph_ref, cw_ref, z_ref):
        at_start = (pl.program_id(0) * tm) % seq == 0
        for c0 in range(0, D, cc):
            b = p_ref[:, c0:c0 + cc].astype(F32)
            u = p_ref[:, D + c0:D + c0 + cc].astype(F32) * p_ref[:, 2 * D + c0:2 * D + c0 + cc].astype(F32)
            uh = ph_ref[:, D + c0:D + c0 + cc].astype(F32) * ph_ref[:, 2 * D + c0:2 * D + c0 + cc].astype(F32)
            uh = jnp.where(at_start, 0.0, uh)
            w = cw_ref[:, c0:c0 + cc]
            cv = _rows_before(uh, u, 2) * w[0:1] + _rows_before(uh, u, 1) * w[1:2] + u * w[2:3]
            z_ref[:, c0:c0 + cc] = (b * cv).astype(z_ref.dtype)

    return _pallas(
        body, comm=comm, name=name, out_shape=jax.ShapeDtypeStruct((T, D), BF16), grid=(T // tm,),
        in_specs=[pl.BlockSpec((tm, D3), lambda i: (i, 0)), prev, pl.BlockSpec((3, D), lambda i: (0, 0))],
        out_specs=pl.BlockSpec((tm, D), lambda i: (i, 0)),
        compiler_params=_params("parallel"),
    )(p, p, cw)


def _gate_a_bwd(p, cw, dz, seq, name, comm=None):
    T, D3 = p.shape
    D = D3 // 3
    tm = _tile(seq, 512, SUBLANES_BF16)
    cc = _tile(D, 256, LANES)
    p_prev, p_next = _halo_specs(tm, D3, T)
    _, dz_next = _halo_specs(tm, D, T)

    def body(p_ref, pp_ref, pn_ref, dz_ref, dzn_ref, cw_ref, dp_ref, dcw_ref):
        i = pl.program_id(0)

        @pl.when(i == 0)
        def _():
            dcw_ref[...] = jnp.zeros_like(dcw_ref)

        at_start = (i * tm) % seq == 0
        at_end = ((i + 1) * tm) % seq == 0
        for c0 in range(0, D, cc):
            cb, cc_, ch = slice(c0, c0 + cc), slice(D + c0, D + c0 + cc), slice(2 * D + c0, 2 * D + c0 + cc)
            b = p_ref[:, cb].astype(F32)
            c = p_ref[:, cc_].astype(F32)
            hh = p_ref[:, ch].astype(F32)
            u = c * hh
            uh = jnp.where(at_start, 0.0, pp_ref[:, cc_].astype(F32) * pp_ref[:, ch].astype(F32))
            w = cw_ref[:, cb]
            u1 = _rows_before(uh, u, 1)
            u2 = _rows_before(uh, u, 2)
            cv = u2 * w[0:1] + u1 * w[1:2] + u * w[2:3]
            dz_t = dz_ref[:, cb].astype(F32)
            dcv = dz_t * b
            dcvn = jnp.where(at_end, 0.0, dzn_ref[:, cb].astype(F32) * pn_ref[:, cb].astype(F32))
            du = dcv * w[2:3] + _rows_after(dcv, dcvn, 1) * w[1:2] + _rows_after(dcv, dcvn, 2) * w[0:1]
            dp_ref[:, cb] = (dz_t * cv).astype(dp_ref.dtype)
            dp_ref[:, cc_] = (du * hh).astype(dp_ref.dtype)
            dp_ref[:, ch] = (du * c).astype(dp_ref.dtype)
            dcw_ref[0:1, cb] += jnp.sum(dcv * u2, axis=0, keepdims=True)
            dcw_ref[1:2, cb] += jnp.sum(dcv * u1, axis=0, keepdims=True)
            dcw_ref[2:3, cb] += jnp.sum(dcv * u, axis=0, keepdims=True)

    res = _pallas(
        body, comm=comm, name=name,
        out_shape=(jax.ShapeDtypeStruct((T, D3), BF16), jax.ShapeDtypeStruct((SUBLANES_F32, D), F32)),
        grid=(T // tm,),
        in_specs=[pl.BlockSpec((tm, D3), lambda i: (i, 0)), p_prev, p_next,
                  pl.BlockSpec((tm, D), lambda i: (i, 0)), dz_next, pl.BlockSpec((3, D), lambda i: (0, 0))],
        out_specs=(pl.BlockSpec((tm, D3), lambda i: (i, 0)), pl.BlockSpec((SUBLANES_F32, D), lambda i: (0, 0))),
        compiler_params=_params("arbitrary"),
    )(p, p, p, dz, dz, cw)
    if comm is None:
        return res[0], res[1][0:3]
    return res[0][0], res[0][1][0:3], res[1]


def _ffn_gate_fwd(u, cw, cb, seq, name, comm=None):
    T, F2 = u.shape
    F = F2 // 2
    tm = _tile(seq, 256, SUBLANES_BF16)
    cc = _tile(F, 256, LANES)
    prev, _ = _halo_specs(tm, F2, T)

    def body(u_ref, uh_ref, cw_ref, cb_ref, a_ref, v_ref):
        at_start = (pl.program_id(0) * tm) % seq == 0

        def conv(c0):
            cols = slice(c0, c0 + cc)
            cur = u_ref[:, cols].astype(F32)
            halo = jnp.where(at_start, 0.0, uh_ref[:, cols].astype(F32))
            w = cw_ref[:, cols]
            return (_rows_before(halo, cur, 2) * w[0:1] + _rows_before(halo, cur, 1) * w[1:2] + cur * w[2:3]
                    + cb_ref[:, cols])

        for c0 in range(0, F, cc):
            g = conv(c0)
            up = conv(F + c0)
            a_ref[:, c0:c0 + cc] = ((g * jax.nn.sigmoid(g)) * up).astype(a_ref.dtype)
            v_ref[:, c0:c0 + cc] = g.astype(v_ref.dtype)
            v_ref[:, F + c0:F + c0 + cc] = up.astype(v_ref.dtype)

    return _pallas(
        body, comm=comm, name=name,
        out_shape=(jax.ShapeDtypeStruct((T, F), BF16), jax.ShapeDtypeStruct((T, F2), BF16)), grid=(T // tm,),
        in_specs=[pl.BlockSpec((tm, F2), lambda i: (i, 0)), prev,
                  pl.BlockSpec((3, F2), lambda i: (0, 0)), pl.BlockSpec((1, F2), lambda i: (0, 0))],
        out_specs=(pl.BlockSpec((tm, F), lambda i: (i, 0)), pl.BlockSpec((tm, F2), lambda i: (i, 0))),
        compiler_params=_params("parallel"),
    )(u, u, cw, cb.reshape(1, F2))


def _ffn_gate_bwd(u, v, cw, da, seq, name, comm=None):
    T, F2 = u.shape
    F = F2 // 2
    H = SUBLANES_BF16
    tm = _tile(seq, 256, H)
    cc = _tile(F, 256, LANES)
    _, v_next = _halo_specs(tm, F2, T)
    _, da_next = _halo_specs(tm, F, T)

    def body(u_ref, v_ref, vn_ref, da_ref, dan_ref, cw_ref, du_ref, acc_ref):
        i = pl.program_id(0)

        @pl.when(i == 0)
        def _():
            acc_ref[...] = jnp.zeros_like(acc_ref)

        at_end = ((i + 1) * tm) % seq == 0
        n = tm + H

        def rows_and_next(ref, nxt, cols):
            return jnp.concatenate([ref[:, cols].astype(F32), nxt[:, cols].astype(F32)], axis=0)

        def back(d, cols):
            w = cw_ref[:, cols]
            d0 = d[:tm]
            d1 = pltpu.roll(d, n - 1, 0)[:tm]
            d2 = pltpu.roll(d, n - 2, 0)[:tm]
            du_ref[:, cols] = (d0 * w[2:3] + d1 * w[1:2] + d2 * w[0:1]).astype(du_ref.dtype)
            ut = u_ref[:, cols].astype(F32)
            acc_ref[0:1, cols] += jnp.sum(d2 * ut, axis=0, keepdims=True)
            acc_ref[1:2, cols] += jnp.sum(d1 * ut, axis=0, keepdims=True)
            acc_ref[2:3, cols] += jnp.sum(d0 * ut, axis=0, keepdims=True)
            acc_ref[3:4, cols] += jnp.sum(d0, axis=0, keepdims=True)

        for c0 in range(0, F, cc):
            gc, uc = slice(c0, c0 + cc), slice(F + c0, F + c0 + cc)
            g = rows_and_next(v_ref, vn_ref, gc)
            up = rows_and_next(v_ref, vn_ref, uc)
            da_ext = jnp.concatenate([da_ref[:, gc].astype(F32),
                                      jnp.where(at_end, 0.0, dan_ref[:, gc].astype(F32))], axis=0)
            sg = jax.nn.sigmoid(g)
            back(da_ext * up * (sg * (1.0 + g * (1.0 - sg))), gc)
            back(da_ext * (g * sg), uc)

    res = _pallas(
        body, comm=comm, name=name,
        out_shape=(jax.ShapeDtypeStruct((T, F2), BF16), jax.ShapeDtypeStruct((SUBLANES_F32, F2), F32)),
        grid=(T // tm,),
        in_specs=[pl.BlockSpec((tm, F2), lambda i: (i, 0)), pl.BlockSpec((tm, F2), lambda i: (i, 0)), v_next,
                  pl.BlockSpec((tm, F), lambda i: (i, 0)), da_next, pl.BlockSpec((3, F2), lambda i: (0, 0))],
        out_specs=(pl.BlockSpec((tm, F2), lambda i: (i, 0)), pl.BlockSpec((SUBLANES_F32, F2), lambda i: (0, 0))),
        compiler_params=_params("arbitrary"),
    )(u, v, v, da, da, cw)
    (du, acc), landed = res if comm is not None else (res, None)
    return (du, acc[0:3], acc[3]) if comm is None else (du, acc[0:3], acc[3], landed)


def _bucket_map():
    P = ATT_BLOCK
    qi = np.arange(P, dtype=np.int64)[:, None]
    kc = np.arange(2 * P, dtype=np.int64)[None, :]
    delta = qi + P - kc
    maps = []
    max_exact = REL_BUCKETS // 2
    for window, dilation in DILATED_BRANCHES:
        band = (delta >= 0) & (delta <= window // dilation)
        n = np.maximum(delta * dilation, 0)
        nf = np.maximum(n, max_exact).astype(np.float32)
        large = max_exact + (np.log(nf / np.float32(max_exact)) / np.float32(math.log(REL_MAX_DISTANCE / max_exact))
                             * np.float32(REL_BUCKETS - max_exact)).astype(np.int32)
        large = np.minimum(large, REL_BUCKETS - 1)
        bucket = np.where(n < max_exact, n, large)
        maps.append(np.where(band, bucket, -1).astype(np.int32))
    return np.stack(maps)


def _bias_tables(rel_bias, bmap, name):
    n_pairs = rel_bias.shape[1] // 2
    nbr, P, P2 = bmap.shape

    def body(rb_ref, bm_ref, o_ref):
        pair = pl.program_id(0)
        in_seq = lax.broadcasted_iota(jnp.int32, (P, P2), 1) >= P
        for br in range(nbr):
            bm = bm_ref[br]
            for hh in range(2):
                acc = jnp.full((P, P2), MASKED_LOGIT, F32)
                for b in range(REL_BUCKETS):
                    acc = jnp.where(bm == b, rb_ref[b, 2 * pair + hh], acc)
                o_ref[br, 0, 0, hh * P:(hh + 1) * P, :] = acc
                o_ref[br, 0, 1, hh * P:(hh + 1) * P, :] = jnp.where(in_seq, acc, MASKED_LOGIT)

    return pl.pallas_call(
        body, name=name, out_shape=jax.ShapeDtypeStruct((nbr, n_pairs, 2, 2 * P, P2), F32), grid=(n_pairs,),
        in_specs=[pl.BlockSpec(memory_space=pltpu.SMEM), pl.BlockSpec((nbr, P, P2), lambda h: (0, 0, 0))],
        out_specs=pl.BlockSpec((nbr, 1, 2, 2 * P, P2), lambda h: (0, h, 0, 0, 0)),
        compiler_params=_params("parallel"),
    )(rel_bias, bmap)


def _bias_grad(dbias, bmap, name):
    nbr, n_pairs, _, P2 = dbias.shape
    P = P2 // 2

    def body(db_ref, bm_ref, o_ref):
        lane = lax.broadcasted_iota(jnp.int32, (1, LANES), 1)
        for hh in range(2):
            row = jnp.zeros((1, LANES), F32)
            for br in range(nbr):
                bm = bm_ref[br]
                d = db_ref[br, 0, hh * P:(hh + 1) * P, :]
                for b in range(REL_BUCKETS):
                    hit = jnp.sum(jnp.where(bm == b, d, 0.0), axis=1, keepdims=True)
                    row = row + jnp.where(lane == b, jnp.sum(hit, axis=0, keepdims=True), 0.0)
            o_ref[hh] = row

    return pl.pallas_call(
        body, name=name, out_shape=jax.ShapeDtypeStruct((2 * n_pairs, 1, LANES), F32), grid=(n_pairs,),
        in_specs=[pl.BlockSpec((nbr, 1, P2, P2), lambda h: (0, h, 0, 0)), pl.BlockSpec((nbr, P, P2), lambda h: (0, 0, 0))],
        out_specs=pl.BlockSpec((2, 1, LANES), lambda h: (h, 0, 0)),
        compiler_params=_params("parallel"),
    )(dbias, bmap)[:, 0, :]


def _rows(start, dilation):
    if dilation == 1:
        return pl.ds(pl.multiple_of(start, ATT_BLOCK), ATT_BLOCK)
    return pl.ds(start, ATT_BLOCK, stride=dilation)


def _for_each_block(seq, unroll, fn):
    P = ATT_BLOCK
    for br, (_, d) in enumerate(DILATED_BRANCHES):
        nb = seq // d // P
        total = nb * d
        u = unroll if total % unroll == 0 else 1

        def some(i, carry, br=br, d=d, nb=nb, u=u):
            blocks = []
            for k in range(u):
                idx = i * u + k
                r, j = idx // nb, idx % nb
                blocks.append((r + j * (d * P), r + jnp.maximum(j - 1, 0) * (d * P), jnp.where(j == 0, 1, 0)))
            fn(br, d, blocks)
            return carry

        lax.fori_loop(0, total // u, some, 0)


def _stack_heads(x, head0):
    return jnp.concatenate([jnp.where(head0, x, 0.0), jnp.where(head0, 0.0, x)], axis=0).astype(BF16)


def _window(ref, start, prev, dilation):
    return jnp.concatenate([ref[_rows(prev, dilation), :], ref[_rows(start, dilation), :]], axis=0).astype(BF16)


def _attn_fwd(q, kv, bias, name):
    B, S, D = q.shape
    P = ATT_BLOCK
    n_pairs = D // LANES
    nbr = len(DILATED_BRANCHES)
    scale = HEAD_DIM ** -0.5

    def body(q_ref, k_ref, v_ref, bias_ref, o_ref, lse_ref, *stats):
        m_s, l_s, acc_s = stats[0:nbr], stats[nbr:2 * nbr], stats[2 * nbr:3 * nbr]
        head0 = lax.broadcasted_iota(jnp.int32, (P, LANES), 1) < HEAD_DIM

        def block(br, d, blocks):
            s = [lax.dot_general(_stack_heads(q_ref[_rows(start, d), :] * scale, head0),
                                 _window(k_ref, start, prev, d), (((1,), (1,)), ((), ())),
                                 preferred_element_type=F32) + bias_ref[br, 0, first]
                 for start, prev, first in blocks]
            m = [jnp.max(x, axis=-1, keepdims=True) for x in s]
            p = [jnp.exp(x - y) for x, y in zip(s, m)]
            l = [jnp.sum(x, axis=-1, keepdims=True) for x in p]
            pv = [jnp.dot(x.astype(BF16), _window(v_ref, start, prev, d), preferred_element_type=F32)
                  for x, (start, prev, _) in zip(p, blocks)]
            for k, (start, _, _) in enumerate(blocks):
                rows = _rows(start, d)
                m_s[br][rows, :] = jnp.where(head0, m[k][:P], m[k][P:])
                l_s[br][rows, :] = jnp.where(head0, l[k][:P], l[k][P:])
                acc_s[br][rows, :] = jnp.where(head0, pv[k][:P], pv[k][P:])

        _for_each_block(S, ATTN_FWD_UNROLL, block)

        chunk = _tile(S, 256, SUBLANES_F32)

        def merge(i, carry):
            rows = pl.ds(pl.multiple_of(i * chunk, chunk), chunk)
            ms = [m_s[br][rows, :] for br in range(nbr)]
            m = functools.reduce(jnp.maximum, ms)
            l = jnp.zeros((chunk, LANES), F32)
            acc = jnp.zeros((chunk, LANES), F32)
            for br in range(nbr):
                w = jnp.exp(ms[br] - m)
                l = l + w * l_s[br][rows, :]
                acc = acc + w * acc_s[br][rows, :]
            o_ref[rows, :] = acc / l
            lse_ref[rows, :] = m + jnp.log(l)
            return carry

        lax.fori_loop(0, S // chunk, merge, 0)

    slab = lambda col0: pl.BlockSpec((None, S, LANES), lambda b, h: (b, 0, col0 + h))
    return pl.pallas_call(
        body, name=name,
        out_shape=(jax.ShapeDtypeStruct((B, S, D), F32), jax.ShapeDtypeStruct((B, S, D), F32)),
        grid=(B, n_pairs),
        in_specs=[slab(0), slab(0), slab(n_pairs),
                  pl.BlockSpec((nbr, 1, 2, 2 * P, 2 * P), lambda b, h: (0, h, 0, 0, 0))],
        out_specs=(slab(0), slab(0)),
        scratch_shapes=[pltpu.VMEM((S, LANES), F32)] * (3 * nbr),
        compiler_params=_params("parallel", "parallel"),
    )(q, kv, kv, bias)


def _attn_bwd(q, kv, o, lse, do, bias, name, comm=None):
    B, S, D = q.shape
    P = ATT_BLOCK
    n_pairs = D // LANES
    nbr = len(DILATED_BRANCHES)
    scale = HEAD_DIM ** -0.5

    def body(q_ref, k_ref, v_ref, o_ref, lse_ref, do_ref, bias_ref, dq_ref, dk_ref, dv_ref, dbias_ref, delta_s):
        head0 = lax.broadcasted_iota(jnp.int32, (P, LANES), 1) < HEAD_DIM

        @pl.when(pl.program_id(1) == 0)
        def _():
            dbias_ref[...] = jnp.zeros_like(dbias_ref)

        chunk = _tile(S, 512, SUBLANES_F32)

        def prepare(i, carry):
            rows = pl.ds(pl.multiple_of(i * chunk, chunk), chunk)
            x = do_ref[rows, :] * o_ref[rows, :]
            h0 = lax.broadcasted_iota(jnp.int32, (chunk, LANES), 1) < HEAD_DIM
            d0 = jnp.sum(jnp.where(h0, x, 0.0), axis=-1, keepdims=True)
            d1 = jnp.sum(jnp.where(h0, 0.0, x), axis=-1, keepdims=True)
            delta_s[rows, :] = jnp.where(h0, d0, d1)
            zero = jnp.zeros((chunk, LANES), F32)
            dq_ref[rows, :] = zero
            dk_ref[rows, :] = zero
            dv_ref[rows, :] = zero
            return carry

        lax.fori_loop(0, S // chunk, prepare, 0)

        def per_head(x):
            return jnp.concatenate([x[:, 0:1], x[:, HEAD_DIM:HEAD_DIM + 1]], axis=0)

        nt = (((1,), (1,)), ((), ()))
        tn = (((0,), (0,)), ((), ()))

        def block(br, d, blocks):
            q2 = [_stack_heads(q_ref[_rows(start, d), :] * scale, head0) for start, _, _ in blocks]
            do2 = [_stack_heads(do_ref[_rows(start, d), :], head0) for start, _, _ in blocks]
            kb = [_window(k_ref, start, prev, d) for start, prev, _ in blocks]
            vb = [_window(v_ref, start, prev, d) for start, prev, _ in blocks]
            s = [lax.dot_general(a, b, nt, preferred_element_type=F32) + bias_ref[br, 0, first]
                 for a, b, (_, _, first) in zip(q2, kb, blocks)]
            dp = [lax.dot_general(a, b, nt, preferred_element_type=F32) for a, b in zip(do2, vb)]
            p = [jnp.exp(x - per_head(lse_ref[_rows(start, d), :])) for x, (start, _, _) in zip(s, blocks)]
            ds = [x * (y - per_head(delta_s[_rows(start, d), :])) for x, y, (start, _, _) in zip(p, dp, blocks)]
            for x in ds:
                dbias_ref[br, 0] += x
            ds16 = [x.astype(BF16) for x in ds]
            dq2 = [jnp.dot(a, b, preferred_element_type=F32) for a, b in zip(ds16, kb)]
            dk = [lax.dot_general(a, b, tn, preferred_element_type=F32) for a, b in zip(ds16, q2)]
            dv = [lax.dot_general(a.astype(BF16), b, tn, preferred_element_type=F32) for a, b in zip(p, do2)]
            for k, (start, prev, _) in enumerate(blocks):
                rows, prows = _rows(start, d), _rows(prev, d)
                dq_ref[rows, :] += jnp.where(head0, dq2[k][:P], dq2[k][P:]) * scale
                dk_ref[prows, :] += dk[k][:P]
                dk_ref[rows, :] += dk[k][P:]
                dv_ref[prows, :] += dv[k][:P]
                dv_ref[rows, :] += dv[k][P:]

        _for_each_block(S, ATTN_BWD_UNROLL, block)

    slab = lambda col0: pl.BlockSpec((None, S, LANES), lambda h, b: (b, 0, col0 + h))
    tab = pl.BlockSpec((nbr, 1, 2, 2 * P, 2 * P), lambda h, b: (0, h, 0, 0, 0))
    dtab = pl.BlockSpec((nbr, 1, 2 * P, 2 * P), lambda h, b: (0, h, 0, 0))
    shp = jax.ShapeDtypeStruct((B, S, D), F32)
    return _pallas(
        body, comm=comm, name=name,
        out_shape=(shp, shp, shp, jax.ShapeDtypeStruct((nbr, n_pairs, 2 * P, 2 * P), F32)),
        grid=(n_pairs, B),
        in_specs=[slab(0), slab(0), slab(n_pairs), slab(0), slab(0), slab(0), tab],
        out_specs=(slab(0), slab(0), slab(0), dtab),
        scratch_shapes=[pltpu.VMEM((S, LANES), F32)],
        compiler_params=_params("parallel", "arbitrary"),
    )(q, kv, kv, o, lse, do, bias)


def _adamw(w, g, m, v, name):
    R, C = w.shape
    tr = _tile(R, 256, SUBLANES_F32) if R % SUBLANES_F32 == 0 else R
    tc = _tile(C, 2048, LANES) if C % LANES == 0 else C

    def body(w_ref, g_ref, m_ref, v_ref, d_ref, nm_ref, nv_ref):
        g_ = g_ref[...]
        m2 = ADAM_B1 * m_ref[...] + (1.0 - ADAM_B1) * g_
        v2 = ADAM_B2 * v_ref[...] + (1.0 - ADAM_B2) * (g_ * g_)
        m_hat = m2 / (1.0 - ADAM_B1 ** ADAM_STEP)
        v_hat = v2 / (1.0 - ADAM_B2 ** ADAM_STEP)
        d_ref[...] = -ADAM_LR * (m_hat / (jnp.sqrt(v_hat) + ADAM_EPS) + ADAM_WD * w_ref[...])
        nm_ref[...] = m2
        nv_ref[...] = v2

    blk = pl.BlockSpec((tr, tc), lambda i, j: (i, j))
    shp = jax.ShapeDtypeStruct((R, C), F32)
    return pl.pallas_call(
        body, name=name, out_shape=(shp, shp, shp), grid=(R // tr, C // tc),
        in_specs=[blk] * 4, out_specs=(blk,) * 3, compiler_params=_params("parallel", "parallel"),
    )(w, g, m, v)


def _sum_slots(slots, own, name):
    n, R, C = slots.shape
    tr = _tile(R, 256, SUBLANES_BF16) if R % SUBLANES_BF16 == 0 else R
    tc = _tile(C, 2048, LANES) if C % LANES == 0 else C

    def body(s_ref, *rest):
        o_ref = rest[-1]
        acc = s_ref[0].astype(F32)
        if own is not None:
            acc = rest[0][...].astype(F32) + acc
        for k in range(1, n):
            acc = acc + s_ref[k].astype(F32)
        o_ref[...] = acc

    in_specs = [pl.BlockSpec((n, tr, tc), lambda i, j: (0, i, j))]
    args = [slots]
    if own is not None:
        in_specs.append(pl.BlockSpec((tr, tc), lambda i, j: (i, j)))
        args.append(own)
    return pl.pallas_call(
        body, name=name, out_shape=jax.ShapeDtypeStruct((R, C), F32), grid=(R // tr, C // tc),
        in_specs=in_specs, out_specs=pl.BlockSpec((tr, tc), lambda i, j: (i, j)),
        compiler_params=_params("parallel", "parallel"),
    )(*args)


def _my_place():
    return lax.axis_index("x"), lax.axis_index("y"), lax.axis_index("c")


def _other_chips(x, y):
    return [(1 - x, y), (x, 1 - y), (1 - x, 1 - y)]


def _piece(ref, blk, axis, shard, half):
    h0 = blk[0] // 2
    idx = []
    for dim, n in enumerate(blk):
        if dim == 0:
            start = half * h0 + (shard * n if axis == 0 else 0)
            idx.append(pl.ds(start, h0))
        elif dim == axis:
            idx.append(pl.ds(shard * n, n))
        else:
            idx.append(slice(None))
    return ref.at[tuple(idx)]


def _half_of(ref, blk, half):
    return ref.at[pl.ds(half * (blk[0] // 2), blk[0] // 2)]


def _gather_weights(shards, axes, names, small):
    n = len(shards)
    blks = [s.shape for s in shards]
    task = _gather_ici_task(shards, axes, names)

    def stage1(*refs):
        small_in, outs, small_out = refs[n], refs[n + 1:2 * n + 1], refs[2 * n + 1]
        send_sems, recv_sems, local_sems = refs[2 * n + 2:]
        sem = lambda k: (send_sems.at[k], recv_sems.at[k])
        x, y, c = _my_place()
        me = 2 * x + y
        local = pltpu.make_async_copy(small_in, small_out.at[me], local_sems.at[0])
        local.start()
        sends, recvs = task.copies(None, outs, sem)
        for k, (px, py) in enumerate(_other_chips(x, y)):
            sends.append(_remote(small_in, small_out.at[me], sem(task.n_sems + k), (px, py, c)))
            recvs.append(_remote(small_in, small_out.at[2 * px + py], sem(task.n_sems + k), (px, py, c)))
        for cp in sends:
            cp.start()
        for cp in recvs:
            cp.wait_recv()
        for cp in sends:
            cp.wait_send()
        local.wait()

    res = pl.pallas_call(
        stage1, name="gather_weights_ici",
        out_shape=task.outs + [jax.ShapeDtypeStruct((N_SHARDS,) + small.shape, small.dtype)],
        in_specs=[ANY] * (n + 1), out_specs=[ANY] * (n + 1), input_output_aliases={a: a for a in range(n)},
        scratch_shapes=[pltpu.SemaphoreType.DMA((task.n_sems + 3,)), pltpu.SemaphoreType.DMA((task.n_sems + 3,)),
                        pltpu.SemaphoreType.DMA((1,))],
    )(*task.ins, small)
    full = _run_comm(_gather_d2d_task(list(res[:n]), blks, axes), "gather_weights_d2d")
    return full, res[n]


def _run_comm(comm, name):
    n_in, n_out = len(comm.ins), len(comm.outs)

    def body(*refs):
        send_sems, recv_sems = refs[n_in + n_out:]
        sends, recvs = comm.copies(refs[:n_in], refs[n_in:n_in + n_out], lambda k: (send_sems.at[k], recv_sems.at[k]))
        for cp in sends:
            cp.start()
        for cp in recvs:
            cp.wait_recv()
        for cp in sends:
            cp.wait_send()

    return list(pl.pallas_call(
        body, name=name, out_shape=comm.outs, in_specs=[ANY] * n_in, out_specs=[ANY] * n_out,
        input_output_aliases=comm.aliases, scratch_shapes=[pltpu.SemaphoreType.DMA((comm.n_sems,))] * 2,
    )(*comm.ins))


def _scatter_grads(grads, blks, axes, small):
    n = len(grads)
    halves = [(b[0] // 2,) + tuple(b[1:]) for b in blks]

    def body(*refs):
        ins, small_in = refs[:n], refs[n]
        outs, small_out = refs[n + 1:2 * n + 1], refs[2 * n + 1]
        send_sems, recv_sems, local_sems = refs[2 * n + 2:]
        x, y, c = _my_place()
        me = 4 * x + 2 * y + c
        local = [pltpu.make_async_copy(small_in, small_out.at[me], local_sems.at[0])]
        for cp in local:
            cp.start()
        sends, recvs = [], []
        for rel in range(1, N_DEVICES):
            px = x ^ ((rel >> 2) & 1)
            py = y ^ ((rel >> 1) & 1)
            pc = c ^ (rel & 1)
            peer = 4 * px + 2 * py + pc
            for a in range(n + 1):
                if a < n:
                    src = _piece(ins[a], blks[a], axes[a], 2 * px + py, pc)
                    dst = land = outs[a].at[rel - 1]
                else:
                    src, dst, land = small_in, small_out.at[me], small_out.at[peer]
                sends.append(pltpu.make_async_remote_copy(
                    src_ref=src, dst_ref=dst, send_sem=send_sems.at[a, rel - 1], recv_sem=recv_sems.at[a, rel - 1],
                    device_id=(px, py, pc), device_id_type=pl.DeviceIdType.MESH))
                recvs.append(pltpu.make_async_remote_copy(
                    src_ref=src, dst_ref=land, send_sem=send_sems.at[a, rel - 1], recv_sem=recv_sems.at[a, rel - 1],
                    device_id=(px, py, pc), device_id_type=pl.DeviceIdType.MESH))
        for cp in sends:
            cp.start()
        for cp in recvs:
            cp.wait_recv()
        for cp in sends:
            cp.wait_send()
        for cp in local:
            cp.wait()

    out_shape = [jax.ShapeDtypeStruct((N_DEVICES - 1,) + h, g.dtype) for h, g in zip(halves, grads)]
    out_shape.append(jax.ShapeDtypeStruct((N_DEVICES,) + small.shape, small.dtype))
    res = pl.pallas_call(
        body, name="scatter_grads", out_shape=out_shape,
        in_specs=[ANY] * (n + 1), out_specs=[ANY] * (n + 1),
        scratch_shapes=[pltpu.SemaphoreType.DMA((n + 1, N_DEVICES - 1)), pltpu.SemaphoreType.DMA((n + 1, N_DEVICES - 1)),
                        pltpu.SemaphoreType.DMA((1,))],
    )(*grads, small)
    return list(res[:n]), res[n]


def _remote(src, dst, sems, device):
    return pltpu.make_async_remote_copy(src_ref=src, dst_ref=dst, send_sem=sems[0], recv_sem=sems[1],
                                        device_id=device, device_id_type=pl.DeviceIdType.MESH)


def _sum_piece(dest, slots, grad, blk, axis, layer, n_layers, name):
    r, c = blk
    h0 = r // 2
    tr = _tile(h0, 256, SUBLANES_BF16)
    tc = _tile(c, 2048, LANES)
    place = jnp.stack([2 * lax.axis_index("x") + lax.axis_index("y"), lax.axis_index("c")]).astype(jnp.int32)

    def body(p_ref, s_ref, g_ref, *rest):
        acc = g_ref[...].astype(F32) + s_ref[0].astype(F32)
        for k in range(1, N_DEVICES - 1):
            acc = acc + s_ref[k].astype(F32)
        rest[-1][...] = acc

    def g_map(i, j, p):
        return (p[1] * (h0 // tr) + (p[0] * (r // tr) if axis == 0 else 0) + i, (p[0] * (c // tc) if axis == 1 else 0) + j)

    in_specs = [pl.BlockSpec((N_DEVICES - 1, tr, tc), lambda i, j, p: (0, i, j)), pl.BlockSpec((tr, tc), g_map)]
    args = [place, slots, grad]
    if dest is not None:
        in_specs.append(ANY)
        args.append(dest)
    return pl.pallas_call(
        body, name=name, out_shape=jax.ShapeDtypeStruct((n_layers, r, c), F32),
        grid_spec=pltpu.PrefetchScalarGridSpec(
            num_scalar_prefetch=1, grid=(h0 // tr, c // tc), in_specs=in_specs,
            out_specs=pl.BlockSpec((None, tr, tc), lambda i, j, p: (layer, p[1] * (h0 // tr) + i, j))),
        input_output_aliases={3: 0} if dest is not None else {},
        compiler_params=_params("parallel", "parallel"),
    )(*args)


def _swap_halves(blocks):
    n = len(blocks)
    layers = [(a, l) for a, b in enumerate(blocks) for l in range(b.shape[0])]

    def copies(in_refs, out_refs, sem):
        x, y, c = _my_place()
        sends, recvs = [], []
        for k, (a, l) in enumerate(layers):
            blk = blocks[a].shape[1:]
            mine = _half_of(out_refs[a].at[l], blk, c)
            sends.append(_remote(mine, mine, sem(k), (x, y, 1 - c)))
            recvs.append(_remote(mine, _half_of(out_refs[a].at[l], blk, 1 - c), sem(k), (x, y, 1 - c)))
        return sends, recvs

    task = _Comm(blocks, [jax.ShapeDtypeStruct(b.shape, b.dtype) for b in blocks], {a: a for a in range(n)},
                 len(layers), copies)
    return _run_comm(task, "swap_grad_halves")


def _place_shard(shard, axis, name):
    r, c = shard.shape
    tr = _tile(r, 512, SUBLANES_BF16)
    full = (r * N_SHARDS, c) if axis == 0 else (r, c * N_SHARDS)
    me2 = (2 * lax.axis_index("x") + lax.axis_index("y")).astype(jnp.int32).reshape(1)

    def body(me_ref, s_ref, o_ref):
        o_ref[...] = s_ref[...].astype(o_ref.dtype)

    if axis == 0:
        out_map = lambda i, me: (me[0] * (r // tr) + i, 0)
    else:
        out_map = lambda i, me: (i, me[0])
    return pl.pallas_call(
        body, name=name, out_shape=jax.ShapeDtypeStruct(full, BF16),
        grid_spec=pltpu.PrefetchScalarGridSpec(
            num_scalar_prefetch=1, grid=(r // tr,),
            in_specs=[pl.BlockSpec((tr, c), lambda i, me: (i, 0))], out_specs=pl.BlockSpec((tr, c), out_map)),
        compiler_params=_params("parallel"),
    )(me2, shard)


def _gather_ici_task(shards, axes, names):
    n = len(shards)
    blks = [s.shape for s in shards]
    bases = [_place_shard(s, ax, f"place_{nm}") for s, ax, nm in zip(shards, axes, names)]

    def copies(in_refs, out_refs, sem):
        x, y, c = _my_place()
        me = 2 * x + y
        sends, recvs = [], []
        for a in range(n):
            mine = _piece(out_refs[a], blks[a], axes[a], me, c)
            for k, (px, py) in enumerate(_other_chips(x, y)):
                sends.append(_remote(mine, mine, sem(3 * a + k), (px, py, c)))
                recvs.append(_remote(mine, _piece(out_refs[a], blks[a], axes[a], 2 * px + py, c), sem(3 * a + k),
                                     (px, py, c)))
        return sends, recvs

    return _Comm(bases, [jax.ShapeDtypeStruct(b.shape, b.dtype) for b in bases], {a: a for a in range(n)}, 3 * n, copies)


def _gather_d2d_task(partials, blks, axes):
    n = len(partials)

    def copies(in_refs, out_refs, sem):
        x, y, c = _my_place()
        sends, recvs = [], []
        for a in range(n):
            for k, (px, py) in enumerate(_other_chips(x, y)):
                mine = _piece(out_refs[a], blks[a], axes[a], 2 * px + py, c)
                theirs = _piece(out_refs[a], blks[a], axes[a], 2 * px + py, 1 - c)
                sends.append(_remote(mine, mine, sem(3 * a + k), (x, y, 1 - c)))
                recvs.append(_remote(mine, theirs, sem(3 * a + k), (x, y, 1 - c)))
        return sends, recvs

    return _Comm(partials, [jax.ShapeDtypeStruct(p.shape, p.dtype) for p in partials], {a: a for a in range(n)},
                 3 * n, copies)


def _scatter_task(grads, blks, axes):
    n = len(grads)

    def copies(in_refs, out_refs, sem):
        x, y, c = _my_place()
        sends, recvs = [], []
        for rel in range(1, N_DEVICES):
            px, py, pc = x ^ ((rel >> 2) & 1), y ^ ((rel >> 1) & 1), c ^ (rel & 1)
            for a in range(n):
                src = _piece(in_refs[a], blks[a], axes[a], 2 * px + py, pc)
                k = (N_DEVICES - 1) * a + rel - 1
                sends.append(_remote(src, out_refs[a].at[rel - 1], sem(k), (px, py, pc)))
                recvs.append(_remote(src, out_refs[a].at[rel - 1], sem(k), (px, py, pc)))
        return sends, recvs

    outs = [jax.ShapeDtypeStruct((N_DEVICES - 1, b[0] // 2) + tuple(b[1:]), g.dtype) for b, g in zip(blks, grads)]
    return _Comm(grads, outs, {}, (N_DEVICES - 1) * n, copies)


def _pack(arrays):
    flat = jnp.concatenate([a.reshape(-1).astype(F32) for a in arrays])
    pad = (-flat.shape[0]) % (SUBLANES_F32 * LANES)
    return jnp.pad(flat, (0, pad)).reshape(-1, LANES)


def _unpack(packed, shapes):
    flat = packed.reshape(-1)
    out, off = [], 0
    for s in shapes:
        n = int(np.prod(s))
        out.append(flat[off:off + n].reshape(s))
        off += n
    return out


def _local_step(x, target, W, shards, geom, small):
    W = dict(W)
    B, S, D = x.shape
    T = B * S
    x2 = x.reshape(T, D)
    tgt = target.reshape(T, D)
    bmap = jnp.asarray(_bucket_map())
    blk = lambda names: [geom[k][0] for k in names]
    axs = lambda names: [geom[k][1] for k in names]
    ici = lambda names: _gather_ici_task([shards[k] for k in names], axs(names), names)
    d2d = lambda names, partials: _gather_d2d_task(list(partials), blk(names), axs(names))
    big, slots = {}, {}

    def scatter(names):
        return _scatter_task([big[k] for k in names], blk(names), axs(names))

    def ffn_bwd(l, dout, h, saved, first, second):
        xn, u, v, a = saved
        da = _mm_nt(dout, W[f"w_down{l}"], BF16, f"ffn{l}_down_dx")
        big[f"w_down{l}"] = _mm_tn(a, dout, f"ffn{l}_down_dw")
        names = first + [f"w_down{l}"]
        du, g_cw, g_cb, landed = _ffn_gate_bwd(u, v, small["ffn_conv"][l], da, S, f"ffn{l}_gate_bwd", comm=scatter(names))
        slots.update(zip(names, landed))
        big[f"w_up{l}"] = _mm_tn(xn, du, f"ffn{l}_up_dw")
        norm = (h, small["ffn_norm"][l], dout)
        if second:
            (dh, g_norm), (slots[f"w_up{l}"],) = _mm_nt(du, W[f"w_up{l}"], None, f"ffn{l}_up_dx",
                                                        comm=scatter([f"w_up{l}"]), norm=norm)
        else:
            dh, g_norm = _mm_nt(du, W[f"w_up{l}"], None, f"ffn{l}_up_dx", norm=norm)
        return dh, g_cw, g_cb, g_norm[0]

    xn0 = _rmsnorm_fwd(x2, small["a_norm"][0], "a_norm")
    p, part = _mm_nn(xn0, W["w_in"], None, BF16, "a_in", comm=ici(["w_up0"]))
    z, (W["w_up0"],) = _gate_a_fwd(p, small["a_conv"][0], S, "a_gate", comm=d2d(["w_up0"], part))
    (h1, xn1), part = _mm_nn(z, W["w_out"], x2, F32, "a_out", comm=ici(["w_down0"]), norm_gains=[small["ffn_norm"][0]])
    later = ["w_kv", "w_q", "w_o", "w_up1"]
    u0, landed = _mm_nn(xn1, W["w_up0"], None, BF16, "ffn0_up", comm=_Comm.join([d2d(["w_down0"], part), ici(later)]))
    W["w_down0"] = landed[0]
    (a0, v0), landed = _ffn_gate_fwd(u0, small["ffn_conv"][0], small["ffn_conv_b"][0], S, "ffn0_gate",
                               comm=_Comm.join([d2d(later, landed[1:]), ici(["w_down1"])]))
    W.update(zip(later, landed[:len(later)]))
    (h2, kvn, xn3), (W["w_down1"],) = _mm_nn(a0, W["w_down0"], h1, F32, "ffn0_down",
                                            comm=d2d(["w_down1"], landed[len(later):]),
                                            norm_gains=[small["kv_norm"], small["b_norm"][0]])
    kv = _mm_nn(kvn, W["w_kv"], None, F32, "kv_proj")
    q = _mm_nn(xn3, W["w_q"], None, F32, "q_proj")
    bias = _bias_tables(small["rel_bias"], bmap, "rel_bias_tables")
    q3, kv3 = q.reshape(B, S, D), kv.reshape(B, S, 2 * D)
    o3, lse3 = _attn_fwd(q3, kv3, bias, "attn_fwd")
    o = o3.reshape(T, D)
    h3, xn4 = _mm_nn(o, W["w_o"], h2, F32, "o_proj", norm_gains=[small["ffn_norm"][1]])
    u1 = _mm_nn(xn4, W["w_up1"], None, BF16, "ffn1_up")
    a1, v1 = _ffn_gate_fwd(u1, small["ffn_conv"][1], small["ffn_conv_b"][1], S, "ffn1_gate")
    h4 = _mm_nn(a1, W["w_down1"], h3, F32, "ffn1_down")
    sq_err, dh4, g_final = _loss_head(h4, small["final_norm"], tgt, "loss_head")
    loss = 0.5 * jnp.sum(sq_err) / D

    dh3, g_cw1, g_cb1, g_fn1 = ffn_bwd(1, dh4, h3, (xn4, u1, v1, a1), [], False)
    do = _mm_nt(dh3, W["w_o"], F32, "o_proj_dx")
    big["w_o"] = _mm_tn(o, dh3, "o_proj_dw")
    (dq3, dk3, dv3, dbias), landed = _attn_bwd(q3, kv3, o3, lse3, do.reshape(B, S, D), bias, "attn_bwd",
                                               comm=scatter(["w_up1", "w_o"]))
    slots.update(zip(["w_up1", "w_o"], landed))
    g_rel = _bias_grad(dbias, bmap, "rel_bias_grad")[:, :REL_BUCKETS].T
    dq, dk, dv = dq3.reshape(T, D), dk3.reshape(T, D), dv3.reshape(T, D)
    dh2, g_bn = _mm_nt(dq, W["w_q"], None, "q_proj_dx", norm=(h2, small["b_norm"][0], dh3))
    big["w_q"] = _mm_tn(xn3, dq, "q_proj_dw")
    dh2, g_kvn = _mm_nt([dk, dv], W["w_kv"], None, "kv_proj_dx", norm=(h2, small["kv_norm"], dh2))
    big["w_kv"] = jnp.concatenate([_mm_tn(kvn, dk, "k_proj_dw"), _mm_tn(kvn, dv, "v_proj_dw")], axis=1)
    dh1, g_cw0, g_cb0, g_fn0 = ffn_bwd(0, dh2, h1, (xn1, u0, v0, a0), ["w_q", "w_kv"], True)
    dz = _mm_nt(dh1, W["w_out"], BF16, "a_out_dx")
    big["w_out"] = _mm_tn(z, dh1, "a_out_dw")
    dp, g_aconv, (slots["w_out"],) = _gate_a_bwd(p, small["a_conv"][0], dz, S, "a_gate_bwd", comm=scatter(["w_out"]))
    big["w_in"] = _mm_tn(xn0, dp, "a_in_dw")
    (dx, g_an), (slots["w_in"],) = _mm_nt(dp, W["w_in"], None, "a_in_dx", comm=scatter(["w_in"]),
                                          norm=(x2, small["a_norm"][0], dh1))
    g_bn, g_kvn, g_an = g_bn[0], g_kvn[0], g_an[0]

    small_g = {"a_norm": g_an[None], "a_conv": g_aconv[None], "kv_norm": g_kvn, "b_norm": g_bn[None],
               "rel_bias": g_rel, "ffn_norm": jnp.stack([g_fn0, g_fn1]), "ffn_conv": jnp.stack([g_cw0, g_cw1]),
               "ffn_conv_b": jnp.stack([g_cb0, g_cb1]), "final_norm": g_final}
    return loss, dx.reshape(B, S, D), big, slots, small_g


BIG = ("w_in", "w_out", "w_kv", "w_q", "w_o", "w_up0", "w_up1", "w_down0", "w_down1")
SMALL = ("a_norm", "a_conv", "kv_norm", "b_norm", "rel_bias", "ffn_norm", "ffn_conv", "ffn_conv_b", "final_norm")
SMALL_SHARDED = ("a_norm", "a_conv", "ffn_conv")
WEIGHT_ORDER = ("a_norm", "a_w_in", "a_conv", "a_w_out", "kv_norm", "w_kv", "b_norm", "b_w_q", "b_w_o", "rel_bias",
                "ffn_norm", "ffn_w_up", "ffn_conv", "ffn_conv_b", "ffn_w_down", "final_norm")
GRAD_OF = {"w_in": ("a_w_in", 0), "w_out": ("a_w_out", 0), "w_kv": ("w_kv", 0), "w_q": ("b_w_q", 0), "w_o": ("b_w_o", 0),
           "w_up0": ("ffn_w_up", 0), "w_up1": ("ffn_w_up", 1), "w_down0": ("ffn_w_down", 0), "w_down1": ("ffn_w_down", 1)}


def _as2d(a):
    return a.reshape(-1, a.shape[-1])


def kernel(x, a_norm, a_w_in, a_conv, a_w_out, kv_norm, w_kv, b_norm, b_w_q, b_w_o, rel_bias, ffn_norm, ffn_w_up, ffn_conv, ffn_conv_b, ffn_w_down, final_norm, loss_target, m_a_norm, m_a_w_in, m_a_conv, m_a_w_out, m_kv_norm, m_w_kv, m_b_norm, m_b_w_q, m_b_w_o, m_rel_bias, m_ffn_norm, m_ffn_w_up, m_ffn_conv, m_ffn_conv_b, m_ffn_w_down, m_final_norm, v_a_norm, v_a_w_in, v_a_conv, v_a_w_out, v_kv_norm, v_w_kv, v_b_norm, v_b_w_q, v_b_w_o, v_rel_bias, v_ffn_norm, v_ffn_w_up, v_ffn_conv, v_ffn_conv_b, v_ffn_w_down, v_final_norm):
    given = dict(a_norm=a_norm, a_w_in=a_w_in, a_conv=a_conv, a_w_out=a_w_out, kv_norm=kv_norm, w_kv=w_kv, b_norm=b_norm,
                 b_w_q=b_w_q, b_w_o=b_w_o, rel_bias=rel_bias, ffn_norm=ffn_norm, ffn_w_up=ffn_w_up, ffn_conv=ffn_conv,
                 ffn_conv_b=ffn_conv_b, ffn_w_down=ffn_w_down, final_norm=final_norm)
    mom_m = dict(a_norm=m_a_norm, a_w_in=m_a_w_in, a_conv=m_a_conv, a_w_out=m_a_w_out, kv_norm=m_kv_norm, w_kv=m_w_kv,
                 b_norm=m_b_norm, b_w_q=m_b_w_q, b_w_o=m_b_w_o, rel_bias=m_rel_bias, ffn_norm=m_ffn_norm,
                 ffn_w_up=m_ffn_w_up, ffn_conv=m_ffn_conv, ffn_conv_b=m_ffn_conv_b, ffn_w_down=m_ffn_w_down,
                 final_norm=m_final_norm)
    mom_v = dict(a_norm=v_a_norm, a_w_in=v_a_w_in, a_conv=v_a_conv, a_w_out=v_a_w_out, kv_norm=v_kv_norm, w_kv=v_w_kv,
                 b_norm=v_b_norm, b_w_q=v_b_w_q, b_w_o=v_b_w_o, rel_bias=v_rel_bias, ffn_norm=v_ffn_norm,
                 ffn_w_up=v_ffn_w_up, ffn_conv=v_ffn_conv, ffn_conv_b=v_ffn_conv_b, ffn_w_down=v_ffn_w_down,
                 final_norm=v_final_norm)

    shard = {"w_in": (a_w_in[0], 1), "w_out": (a_w_out[0], 0), "w_kv": (w_kv, 1), "w_q": (b_w_q[0], 0),
             "w_o": (b_w_o[0], 0), "w_up0": (ffn_w_up[0], 1), "w_up1": (ffn_w_up[1], 1),
             "w_down0": (ffn_w_down[0], 0), "w_down1": (ffn_w_down[1], 0)}
    blks = [shard[k][0].shape for k in BIG]
    axes = [shard[k][1] for k in BIG]

    small_sharded = [given[k] for k in SMALL_SHARDED]
    packed = _pack(small_sharded)
    first = ("w_in", "w_out")
    fulls, packed_all = _gather_weights([shard[k][0] for k in first], [shard[k][1] for k in first], first, packed)
    W = dict(zip(first, fulls))
    later = {k: shard[k][0] for k in BIG if k not in first}
    geom = {k: (shard[k][0].shape, shard[k][1]) for k in BIG}
    small = {k: given[k] for k in SMALL}
    per_shard = [_unpack(packed_all[j], [a.shape for a in small_sharded]) for j in range(N_SHARDS)]
    for i, k in enumerate(SMALL_SHARDED):
        small[k] = jnp.concatenate([per_shard[j][i] for j in range(N_SHARDS)], axis=-1)

    loss, grad_x, big_g, slots, small_g = _local_step(x, loss_target, W, later, geom, small)
    loss = lax.psum(loss, MESH_AXES)

    small_shapes = [small_g[k].shape for k in SMALL]
    _, small_slots = _scatter_grads([], [], [], _pack([small_g[k] for k in SMALL]))
    layers_of = {}
    for k in BIG:
        layers_of.setdefault(GRAD_OF[k][0], []).append(k)
    blocks = {}
    for name, members in layers_of.items():
        dest = None
        for k in members:
            dest = _sum_piece(dest, slots[k], big_g[k], geom[k][0], geom[k][1], GRAD_OF[k][1], len(members), f"sum_{k}")
        blocks[name] = dest
    reduced = dict(zip(blocks, _swap_halves(list(blocks.values()))))
    small_sum = _sum_slots(small_slots, None, "sum_small")
    small_red = dict(zip(SMALL, _unpack(small_sum, small_shapes)))
    j = 2 * lax.axis_index("x") + lax.axis_index("y")
    for k in SMALL_SHARDED:
        w = given[k].shape[-1]
        small_red[k] = lax.dynamic_slice_in_dim(small_red[k], j * w, w, axis=small_red[k].ndim - 1)

    grads, deltas, new_m, new_v = {}, {}, {}, {}
    for name in WEIGHT_ORDER:
        if name in reduced:
            g = reduced[name].reshape(given[name].shape)
            d, nm, nv = _adamw(_as2d(given[name]), _as2d(g), _as2d(mom_m[name]), _as2d(mom_v[name]), f"adamw_{name}")
            grads[name] = g
            deltas[name], new_m[name], new_v[name] = (t.reshape(given[name].shape) for t in (d, nm, nv))
    small_names = [n for n in WEIGHT_ORDER if n not in reduced]
    for n in small_names:
        grads[n] = small_red[n].reshape(given[n].shape)
    sw, sg, sm, sv = (_pack([d[n] for n in small_names]) for d in (given, grads, mom_m, mom_v))
    d, nm, nv = _adamw(sw, sg, sm, sv, "adamw_small")
    shapes = [given[n].shape for n in small_names]
    for n, a, b_, c_ in zip(small_names, _unpack(d, shapes), _unpack(nm, shapes), _unpack(nv, shapes)):
        deltas[n], new_m[n], new_v[n] = a, b_, c_

    return (loss, grad_x, *[grads[n] for n in WEIGHT_ORDER], *[deltas[n] for n in WEIGHT_ORDER],
            *[new_m[n] for n in WEIGHT_ORDER], *[new_v[n] for n in WEIGHT_ORDER])
```

```python
import functools
import math

import numpy as np

import jax
import jax.numpy as jnp
from jax import lax
from jax.experimental import pallas as pl
from jax.experimental.pallas import tpu as pltpu

F32 = jnp.float32
BF16 = jnp.bfloat16

RMS_EPS = 1e-6
HEAD_DIM = 64
ATT_BLOCK = 128
DILATED_BRANCHES = ((128, 1), (512, 4), (2048, 16))
REL_BUCKETS = 32
REL_MAX_DISTANCE = 2048
MASKED_LOGIT = -1e30
ATTN_FWD_UNROLL = 4
ATTN_BWD_UNROLL = 4

ADAM_LR = 0.001
ADAM_B1 = 0.9
ADAM_B2 = 0.999
ADAM_EPS = 1e-08
ADAM_WD = 0.01
ADAM_STEP = 10

LANES = 128
SUBLANES_F32 = 8
SUBLANES_BF16 = 16
VMEM_LIMIT_BYTES = 56 * 1024 * 1024

MESH_AXES = ("x", "y", "c")
N_SHARDS = 4
N_DEVICES = 8
ANY = pl.BlockSpec(memory_space=pl.ANY)


def _tile(n, pref, mult):
    best = None
    for t in range(mult, min(n, pref) + 1, mult):
        if n % t == 0:
            best = t
    if best is None:
        raise ValueError(f"no tile for {n} (multiple of {mult}, at most {pref})")
    return best


def _params(*sem):
    return pltpu.CompilerParams(dimension_semantics=sem, vmem_limit_bytes=VMEM_LIMIT_BYTES)


class _Comm:
    def __init__(self, ins, outs, aliases, n_sems, copies):
        self.ins, self.outs, self.aliases, self.n_sems, self.copies = list(ins), list(outs), dict(aliases), n_sems, copies

    @staticmethod
    def join(parts):
        parts = [p for p in parts if p is not None]
        ins, outs, aliases, offs, n_sems = [], [], {}, [], 0
        for p in parts:
            offs.append((len(ins), len(outs), n_sems))
            aliases.update({len(ins) + i: len(outs) + o for i, o in p.aliases.items()})
            ins += p.ins
            outs += p.outs
            n_sems += p.n_sems

        def copies(in_refs, out_refs, sem):
            sends, recvs = [], []
            for p, (i0, o0, s0) in zip(parts, offs):
                s, r = p.copies(in_refs[i0:i0 + len(p.ins)], out_refs[o0:o0 + len(p.outs)],
                                lambda k, s0=s0: sem(s0 + k))
                sends += s
                recvs += r
            return sends, recvs

        return _Comm(ins, outs, aliases, n_sems, copies)


def _pallas(body, *, comm=None, name, out_shape, grid=(), in_specs=(), out_specs=(), scratch_shapes=(),
            compiler_params=None):
    if comm is None:
        return pl.pallas_call(body, name=name, out_shape=out_shape, grid=grid, in_specs=in_specs, out_specs=out_specs,
                              scratch_shapes=scratch_shapes, compiler_params=compiler_params)
    single = not isinstance(out_shape, (tuple, list))
    outs = (out_shape,) if single else tuple(out_shape)
    o_specs = (out_specs,) if single else tuple(out_specs)
    n_in, n_cin, n_out, n_cout, n_scr = len(in_specs), len(comm.ins), len(outs), len(comm.outs), len(scratch_shapes)

    def carried(*refs):
        base_in, c_in = refs[:n_in], refs[n_in:n_in + n_cin]
        o0 = n_in + n_cin
        base_out, c_out = refs[o0:o0 + n_out], refs[o0 + n_out:o0 + n_out + n_cout]
        s0 = o0 + n_out + n_cout
        base_scr, (send_sems, recv_sems) = refs[s0:s0 + n_scr], refs[s0 + n_scr:]
        sem = lambda k: (send_sems.at[k], recv_sems.at[k])
        first = functools.reduce(jnp.logical_and, [pl.program_id(a) == 0 for a in range(len(grid))])
        last = functools.reduce(jnp.logical_and, [pl.program_id(a) == n - 1 for a, n in enumerate(grid)])

        @pl.when(first)
        def _():
            for cp in comm.copies(c_in, c_out, sem)[0]:
                cp.start()

        body(*base_in, *base_out, *base_scr)

        @pl.when(last)
        def _():
            sends, recvs = comm.copies(c_in, c_out, sem)
            for cp in recvs:
                cp.wait_recv()
            for cp in sends:
                cp.wait_send()

    call = pl.pallas_call(
        carried, name=name, out_shape=outs + tuple(comm.outs), grid=grid,
        in_specs=list(in_specs) + [ANY] * n_cin, out_specs=o_specs + (ANY,) * n_cout,
        scratch_shapes=list(scratch_shapes) + [pltpu.SemaphoreType.DMA((comm.n_sems,))] * 2,
        input_output_aliases={n_in + i: n_out + o for i, o in comm.aliases.items()},
        compiler_params=_params(*(["arbitrary"] * len(grid))))

    def run(*args):
        res = call(*args, *comm.ins)
        base = res[0] if single else tuple(res[:n_out])
        return base, list(res[n_out:])

    return run


def _rmsnorm_fwd(x, g, name, comm=None):
    T, D = x.shape
    tm = _tile(T, 512, SUBLANES_BF16)

    def body(x_ref, g_ref, o_ref):
        xf = x_ref[...]
        r = lax.rsqrt(jnp.mean(xf * xf, axis=-1, keepdims=True) + RMS_EPS)
        o_ref[...] = ((xf * r) * g_ref[...]).astype(o_ref.dtype)

    return _pallas(
        body, comm=comm, name=name, out_shape=jax.ShapeDtypeStruct((T, D), BF16), grid=(T // tm,),
        in_specs=[pl.BlockSpec((tm, D), lambda i: (i, 0)), pl.BlockSpec((1, D), lambda i: (0, 0))],
        out_specs=pl.BlockSpec((tm, D), lambda i: (i, 0)),
        compiler_params=_params("parallel"),
    )(x, g.reshape(1, D))


def _loss_head(h, g, target, name):
    T, D = h.shape
    tm = _tile(T, 512, SUBLANES_F32)

    def body(h_ref, g_ref, t_ref, dh_ref, acc_ref):
        @pl.when(pl.program_id(0) == 0)
        def _():
            acc_ref[...] = jnp.zeros_like(acc_ref)

        xf = h_ref[...]
        r = lax.rsqrt(jnp.mean(xf * xf, axis=-1, keepdims=True) + RMS_EPS)
        xhat = xf * r
        err = xhat * g_ref[...] - t_ref[...]
        dy = err * (1.0 / D)
        acc_ref[0:1, :] += jnp.sum(dy * xhat, axis=0, keepdims=True)
        acc_ref[1:2, :] += jnp.sum(err * err, axis=0, keepdims=True)
        t = dy * g_ref[...]
        dh_ref[...] = r * (t - xhat * jnp.mean(t * xhat, axis=-1, keepdims=True))

    row = pl.BlockSpec((tm, D), lambda i: (i, 0))
    dh, acc = pl.pallas_call(
        body, name=name,
        out_shape=(jax.ShapeDtypeStruct((T, D), F32), jax.ShapeDtypeStruct((SUBLANES_F32, D), F32)),
        grid=(T // tm,),
        in_specs=[row, pl.BlockSpec((1, D), lambda i: (0, 0)), row],
        out_specs=(row, pl.BlockSpec((SUBLANES_F32, D), lambda i: (0, 0))),
        compiler_params=_params("arbitrary"),
    )(h, g.reshape(1, D), target)
    return acc[1], dh, acc[0]


def _mm_nn(a, b, res, out_dtype, name, comm=None, norm_gains=()):
    T, K = a.shape
    N = b.shape[1]
    tm = _tile(T, 512, SUBLANES_BF16)
    tn = _tile(N, 3072, LANES)
    n_g = len(norm_gains)
    assert not n_g or tn == N, "the fused rmsnorm needs whole rows in one tile"

    def body(a_ref, b_ref, *rest):
        ins, outs = rest[:len(rest) - 1 - n_g], rest[len(rest) - 1 - n_g:]
        acc = jnp.dot(a_ref[...].astype(BF16), b_ref[...], preferred_element_type=F32)
        if res is not None:
            acc = acc + ins[0][...]
        outs[0][...] = acc.astype(outs[0].dtype)
        if n_g:
            y = acc * lax.rsqrt(jnp.mean(acc * acc, axis=-1, keepdims=True) + RMS_EPS)
            for g_ref, xn_ref in zip(ins[len(ins) - n_g:], outs[1:]):
                xn_ref[...] = (y * g_ref[...]).astype(xn_ref.dtype)

    tile = pl.BlockSpec((tm, tn), lambda j, i: (i, j))
    in_specs = [pl.BlockSpec((tm, K), lambda j, i: (i, 0)), pl.BlockSpec((K, tn), lambda j, i: (0, j))]
    args = [a, b]
    if res is not None:
        in_specs.append(tile)
        args.append(res)
    for g in norm_gains:
        in_specs.append(pl.BlockSpec((1, N), lambda j, i: (0, 0)))
        args.append(g.reshape(1, N))
    out_shape = jax.ShapeDtypeStruct((T, N), out_dtype)
    if n_g:
        out_shape = (out_shape,) + (jax.ShapeDtypeStruct((T, N), BF16),) * n_g
    return _pallas(
        body, comm=comm, name=name, out_shape=out_shape, grid=(N // tn, T // tm),
        in_specs=in_specs, out_specs=(tile,) * (1 + n_g) if n_g else tile,
        compiler_params=_params("parallel", "parallel"),
    )(*args)


def _mm_nt(dy, b, out_dtype, name, comm=None, norm=None):
    dys = list(dy) if isinstance(dy, (list, tuple)) else [dy]
    T, n_each = dys[0].shape
    K = b.shape[0]
    tm = _tile(T, 1024 if norm is None else 512, SUBLANES_BF16)
    tk = _tile(K, 1536, LANES)
    tn = _tile(n_each, 2816, LANES)
    per = n_each // tn
    n_steps = per * len(dys)
    assert norm is None or tk == K, "the fused rmsnorm backward needs whole rows in one tile"

    def body(*refs):
        dy_refs, b_ref, acc_ref = refs[:len(dys)], refs[len(dys)], refs[-1]
        i, n = pl.program_id(0), pl.program_id(2)

        @pl.when(n == 0)
        def _():
            acc_ref[...] = jnp.zeros_like(acc_ref)

        for p, dy_ref in enumerate(dy_refs):
            @pl.when(jnp.logical_and(n >= p * per, n < (p + 1) * per))
            def _(dy_ref=dy_ref):
                acc_ref[...] += lax.dot_general(dy_ref[...].astype(BF16), b_ref[...], (((1,), (1,)), ((), ())),
                                                preferred_element_type=F32)

        if norm is None:
            @pl.when(n == n_steps - 1)
            def _():
                refs[-2][...] = acc_ref[...].astype(refs[-2].dtype)
        else:
            x_ref, g_ref, dres_ref, dx_ref, dg_ref = refs[len(dys) + 1:-1]

            @pl.when(jnp.logical_and(i == 0, n == 0))
            def _():
                dg_ref[...] = jnp.zeros_like(dg_ref)

            @pl.when(n == n_steps - 1)
            def _():
                xf = x_ref[...]
                r = lax.rsqrt(jnp.mean(xf * xf, axis=-1, keepdims=True) + RMS_EPS)
                xhat = xf * r
                d = acc_ref[...]
                dg_ref[0:1, :] += jnp.sum(d * xhat, axis=0, keepdims=True)
                t = d * g_ref[...]
                dx_ref[...] = dres_ref[...] + r * (t - xhat * jnp.mean(t * xhat, axis=-1, keepdims=True))

    in_specs = [pl.BlockSpec((tm, tn), lambda i, k, n, p=p: (i, jnp.clip(n - p * per, 0, per - 1))) for p in range(len(dys))]
    in_specs.append(pl.BlockSpec((tk, tn), lambda i, k, n: (k, n)))
    args = dys + [b]
    tile = pl.BlockSpec((tm, tk), lambda i, k, n: (i, k))
    if norm is None:
        out_shape, out_specs = jax.ShapeDtypeStruct((T, K), out_dtype), tile
    else:
        x, g, dres = norm
        in_specs += [tile, pl.BlockSpec((1, K), lambda i, k, n: (0, 0)), tile]
        args += [x, g.reshape(1, K), dres]
        out_shape = (jax.ShapeDtypeStruct((T, K), F32), jax.ShapeDtypeStruct((SUBLANES_F32, K), F32))
        out_specs = (tile, pl.BlockSpec((SUBLANES_F32, K), lambda i, k, n: (0, 0)))
    return _pallas(
        body, comm=comm, name=name, out_shape=out_shape, grid=(T // tm, K // tk, n_steps),
        in_specs=in_specs, out_specs=out_specs, scratch_shapes=[pltpu.VMEM((tm, tk), F32)],
        compiler_params=_params("parallel", "parallel", "arbitrary") if norm is None else _params(*["arbitrary"] * 3),
    )(*args)


def _mm_tn(a, dy, name):
    T, K = a.shape
    N = dy.shape[1]
    tt = _tile(T, 2048, SUBLANES_BF16)
    tk = _tile(K, 1536, LANES)
    tn = _tile(N, 1536, LANES)
    t_steps = T // tt

    def body(a_ref, dy_ref, o_ref, acc_ref):
        t = pl.program_id(2)

        @pl.when(t == 0)
        def _():
            acc_ref[...] = jnp.zeros_like(acc_ref)

        acc_ref[...] += lax.dot_general(a_ref[...].astype(BF16), dy_ref[...].astype(BF16),
                                        (((0,), (0,)), ((), ())), preferred_element_type=F32)

        @pl.when(t == t_steps - 1)
        def _():
            o_ref[...] = acc_ref[...].astype(o_ref.dtype)

    return pl.pallas_call(
        body, name=name, out_shape=jax.ShapeDtypeStruct((K, N), BF16), grid=(K // tk, N // tn, t_steps),
        in_specs=[pl.BlockSpec((tt, tk), lambda k, n, t: (t, k)), pl.BlockSpec((tt, tn), lambda k, n, t: (t, n))],
        out_specs=pl.BlockSpec((tk, tn), lambda k, n, t: (k, n)),
        scratch_shapes=[pltpu.VMEM((tk, tn), F32)],
        compiler_params=_params("parallel", "parallel", "arbitrary"),
    )(a, dy)


def _rows_before(halo, cur, k):
    h = halo.shape[0]
    return pltpu.roll(jnp.concatenate([halo, cur], axis=0), k, 0)[h:]


def _rows_after(cur, halo, k):
    n = cur.shape[0]
    total = n + halo.shape[0]
    return pltpu.roll(jnp.concatenate([cur, halo], axis=0), total - k, 0)[:n]


def _halo_specs(tm, width, n_rows):
    per = tm // SUBLANES_BF16
    last = n_rows // SUBLANES_BF16 - 1
    prev = pl.BlockSpec((SUBLANES_BF16, width), lambda i: (jnp.maximum(i * per - 1, 0), 0))
    nxt = pl.BlockSpec((SUBLANES_BF16, width), lambda i: (jnp.minimum((i + 1) * per, last), 0))
    return prev, nxt


def _gate_a_fwd(p, cw, seq, name, comm=None):
    T, D3 = p.shape
    D = D3 // 3
    tm = _tile(seq, 512, SUBLANES_BF16)
    cc = _tile(D, 256, LANES)
    prev, _ = _halo_specs(tm, D3, T)

    def body(p_ref, ph_ref, cw_ref, z_ref):
        at_start = (pl.program_id(0) * tm) % seq == 0
        for c0 in range(0, D, cc):
            b = p_ref[:, c0:c0 + cc].astype(F32)
            u = p_ref[:, D + c0:D + c0 + cc].astype(F32) * p_ref[:, 2 * D + c0:2 * D + c0 + cc].astype(F32)
            uh = ph_ref[:, D + c0:D + c0 + cc].astype(F32) * ph_ref[:, 2 * D + c0:2 * D + c0 + cc].astype(F32)
            uh = jnp.where(at_start, 0.0, uh)
            w = cw_ref[:, c0:c0 + cc]
            cv = _rows_before(uh, u, 2) * w[0:1] + _rows_before(uh, u, 1) * w[1:2] + u * w[2:3]
            z_ref[:, c0:c0 + cc] = (b * cv).astype(z_ref.dtype)

    return _pallas(
        body, comm=comm, name=name, out_shape=jax.ShapeDtypeStruct((T, D), BF16), grid=(T // tm,),
        in_specs=[pl.BlockSpec((tm, D3), lambda i: (i, 0)), prev, pl.BlockSpec((3, D), lambda i: (0, 0))],
        out_specs=pl.BlockSpec((tm, D), lambda i: (i, 0)),
        compiler_params=_params("parallel"),
    )(p, p, cw)


def _gate_a_bwd(p, cw, dz, seq, name, comm=None):
    T, D3 = p.shape
    D = D3 // 3
    tm = _tile(seq, 512, SUBLANES_BF16)
    cc = _tile(D, 256, LANES)
    p_prev, p_next = _halo_specs(tm, D3, T)
    _, dz_next = _halo_specs(tm, D, T)

    def body(p_ref, pp_ref, pn_ref, dz_ref, dzn_ref, cw_ref, dp_ref, dcw_ref):
        i = pl.program_id(0)

        @pl.when(i == 0)
        def _():
            dcw_ref[...] = jnp.zeros_like(dcw_ref)

        at_start = (i * tm) % seq == 0
        at_end = ((i + 1) * tm) % seq == 0
        for c0 in range(0, D, cc):
            cb, cc_, ch = slice(c0, c0 + cc), slice(D + c0, D + c0 + cc), slice(2 * D + c0, 2 * D + c0 + cc)
            b = p_ref[:, cb].astype(F32)
            c = p_ref[:, cc_].astype(F32)
            hh = p_ref[:, ch].astype(F32)
            u = c * hh
            uh = jnp.where(at_start, 0.0, pp_ref[:, cc_].astype(F32) * pp_ref[:, ch].astype(F32))
            w = cw_ref[:, cb]
            u1 = _rows_before(uh, u, 1)
            u2 = _rows_before(uh, u, 2)
            cv = u2 * w[0:1] + u1 * w[1:2] + u * w[2:3]
            dz_t = dz_ref[:, cb].astype(F32)
            dcv = dz_t * b
            dcvn = jnp.where(at_end, 0.0, dzn_ref[:, cb].astype(F32) * pn_ref[:, cb].astype(F32))
            du = dcv * w[2:3] + _rows_after(dcv, dcvn, 1) * w[1:2] + _rows_after(dcv, dcvn, 2) * w[0:1]
            dp_ref[:, cb] = (dz_t * cv).astype(dp_ref.dtype)
            dp_ref[:, cc_] = (du * hh).astype(dp_ref.dtype)
            dp_ref[:, ch] = (du * c).astype(dp_ref.dtype)
            dcw_ref[0:1, cb] += jnp.sum(dcv * u2, axis=0, keepdims=True)
            dcw_ref[1:2, cb] += jnp.sum(dcv * u1, axis=0, keepdims=True)
            dcw_ref[2:3, cb] += jnp.sum(dcv * u, axis=0, keepdims=True)

    res = _pallas(
        body, comm=comm, name=name,
        out_shape=(jax.ShapeDtypeStruct((T, D3), BF16), jax.ShapeDtypeStruct((SUBLANES_F32, D), F32)),
        grid=(T // tm,),
        in_specs=[pl.BlockSpec((tm, D3), lambda i: (i, 0)), p_prev, p_next,
                  pl.BlockSpec((tm, D), lambda i: (i, 0)), dz_next, pl.BlockSpec((3, D), lambda i: (0, 0))],
        out_specs=(pl.BlockSpec((tm, D3), lambda i: (i, 0)), pl.BlockSpec((SUBLANES_F32, D), lambda i: (0, 0))),
        compiler_params=_params("arbitrary"),
    )(p, p, p, dz, dz, cw)
    if comm is None:
        return res[0], res[1][0:3]
    return res[0][0], res[0][1][0:3], res[1]


def _ffn_gate_fwd(u, cw, cb, seq, name, comm=None):
    T, F2 = u.shape
    F = F2 // 2
    tm = _tile(seq, 256, SUBLANES_BF16)
    cc = _tile(F, 256, LANES)
    prev, _ = _halo_specs(tm, F2, T)

    def body(u_ref, uh_ref, cw_ref, cb_ref, a_ref, v_ref):
        at_start = (pl.program_id(0) * tm) % seq == 0

        def conv(c0):
            cols = slice(c0, c0 + cc)
            cur = u_ref[:, cols].astype(F32)
            halo = jnp.where(at_start, 0.0, uh_ref[:, cols].astype(F32))
            w = cw_ref[:, cols]
            return (_rows_before(halo, cur, 2) * w[0:1] + _rows_before(halo, cur, 1) * w[1:2] + cur * w[2:3]
                    + cb_ref[:, cols])

        for c0 in range(0, F, cc):
            g = conv(c0)
            up = conv(F + c0)
            a_ref[:, c0:c0 + cc] = ((g * jax.nn.sigmoid(g)) * up).astype(a_ref.dtype)
            v_ref[:, c0:c0 + cc] = g.astype(v_ref.dtype)
            v_ref[:, F + c0:F + c0 + cc] = up.astype(v_ref.dtype)

    return _pallas(
        body, comm=comm, name=name,
        out_shape=(jax.ShapeDtypeStruct((T, F), BF16), jax.ShapeDtypeStruct((T, F2), BF16)), grid=(T // tm,),
        in_specs=[pl.BlockSpec((tm, F2), lambda i: (i, 0)), prev,
                  pl.BlockSpec((3, F2), lambda i: (0, 0)), pl.BlockSpec((1, F2), lambda i: (0, 0))],
        out_specs=(pl.BlockSpec((tm, F), lambda i: (i, 0)), pl.BlockSpec((tm, F2), lambda i: (i, 0))),
        compiler_params=_params("parallel"),
    )(u, u, cw, cb.reshape(1, F2))


def _ffn_gate_bwd(u, v, cw, da, seq, name, comm=None):
    T, F2 = u.shape
    F = F2 // 2
    H = SUBLANES_BF16
    tm = _tile(seq, 256, H)
    cc = _tile(F, 256, LANES)
    _, v_next = _halo_specs(tm, F2, T)
    _, da_next = _halo_specs(tm, F, T)

    def body(u_ref, v_ref, vn_ref, da_ref, dan_ref, cw_ref, du_ref, acc_ref):
        i = pl.program_id(0)

        @pl.when(i == 0)
        def _():
            acc_ref[...] = jnp.zeros_like(acc_ref)

        at_end = ((i + 1) * tm) % seq == 0
        n = tm + H

        def rows_and_next(ref, nxt, cols):
            return jnp.concatenate([ref[:, cols].astype(F32), nxt[:, cols].astype(F32)], axis=0)

        def back(d, cols):
            w = cw_ref[:, cols]
            d0 = d[:tm]
            d1 = pltpu.roll(d, n - 1, 0)[:tm]
            d2 = pltpu.roll(d, n - 2, 0)[:tm]
            du_ref[:, cols] = (d0 * w[2:3] + d1 * w[1:2] + d2 * w[0:1]).astype(du_ref.dtype)
            ut = u_ref[:, cols].astype(F32)
            acc_ref[0:1, cols] += jnp.sum(d2 * ut, axis=0, keepdims=True)
            acc_ref[1:2, cols] += jnp.sum(d1 * ut, axis=0, keepdims=True)
            acc_ref[2:3, cols] += jnp.sum(d0 * ut, axis=0, keepdims=True)
            acc_ref[3:4, cols] += jnp.sum(d0, axis=0, keepdims=True)

        for c0 in range(0, F, cc):
            gc, uc = slice(c0, c0 + cc), slice(F + c0, F + c0 + cc)
            g = rows_and_next(v_ref, vn_ref, gc)
            up = rows_and_next(v_ref, vn_ref, uc)
            da_ext = jnp.concatenate([da_ref[:, gc].astype(F32),
                                      jnp.where(at_end, 0.0, dan_ref[:, gc].astype(F32))], axis=0)
            sg = jax.nn.sigmoid(g)
            back(da_ext * up * (sg * (1.0 + g * (1.0 - sg))), gc)
            back(da_ext * (g * sg), uc)

    res = _pallas(
        body, comm=comm, name=name,
        out_shape=(jax.ShapeDtypeStruct((T, F2), BF16), jax.ShapeDtypeStruct((SUBLANES_F32, F2), F32)),
        grid=(T // tm,),
        in_specs=[pl.BlockSpec((tm, F2), lambda i: (i, 0)), pl.BlockSpec((tm, F2), lambda i: (i, 0)), v_next,
                  pl.BlockSpec((tm, F), lambda i: (i, 0)), da_next, pl.BlockSpec((3, F2), lambda i: (0, 0))],
        out_specs=(pl.BlockSpec((tm, F2), lambda i: (i, 0)), pl.BlockSpec((SUBLANES_F32, F2), lambda i: (0, 0))),
        compiler_params=_params("arbitrary"),
    )(u, v, v, da, da, cw)
    (du, acc), landed = res if comm is not None else (res, None)
    return (du, acc[0:3], acc[3]) if comm is None else (du, acc[0:3], acc[3], landed)


def _bucket_map():
    P = ATT_BLOCK
    qi = np.arange(P, dtype=np.int64)[:, None]
    kc = np.arange(2 * P, dtype=np.int64)[None, :]
    delta = qi + P - kc
    maps = []
    max_exact = REL_BUCKETS // 2
    for window, dilation in DILATED_BRANCHES:
        band = (delta >= 0) & (delta <= window // dilation)
        n = np.maximum(delta * dilation, 0)
        nf = np.maximum(n, max_exact).astype(np.float32)
        large = max_exact + (np.log(nf / np.float32(max_exact)) / np.float32(math.log(REL_MAX_DISTANCE / max_exact))
                             * np.float32(REL_BUCKETS - max_exact)).astype(np.int32)
        large = np.minimum(large, REL_BUCKETS - 1)
        bucket = np.where(n < max_exact, n, large)
        maps.append(np.where(band, bucket, -1).astype(np.int32))
    return np.stack(maps)


def _bias_tables(rel_bias, bmap, name):
    n_pairs = rel_bias.shape[1] // 2
    nbr, P, P2 = bmap.shape

    def body(rb_ref, bm_ref, o_ref):
        pair = pl.program_id(0)
        in_seq = lax.broadcasted_iota(jnp.int32, (P, P2), 1) >= P
        for br in range(nbr):
            bm = bm_ref[br]
            for hh in range(2):
                acc = jnp.full((P, P2), MASKED_LOGIT, F32)
                for b in range(REL_BUCKETS):
                    acc = jnp.where(bm == b, rb_ref[b, 2 * pair + hh], acc)
                o_ref[br, 0, 0, hh * P:(hh + 1) * P, :] = acc
                o_ref[br, 0, 1, hh * P:(hh + 1) * P, :] = jnp.where(in_seq, acc, MASKED_LOGIT)

    return pl.pallas_call(
        body, name=name, out_shape=jax.ShapeDtypeStruct((nbr, n_pairs, 2, 2 * P, P2), F32), grid=(n_pairs,),
        in_specs=[pl.BlockSpec(memory_space=pltpu.SMEM), pl.BlockSpec((nbr, P, P2), lambda h: (0, 0, 0))],
        out_specs=pl.BlockSpec((nbr, 1, 2, 2 * P, P2), lambda h: (0, h, 0, 0, 0)),
        compiler_params=_params("parallel"),
    )(rel_bias, bmap)


def _bias_grad(dbias, bmap, name):
    nbr, n_pairs, _, P2 = dbias.shape
    P = P2 // 2

    def body(db_ref, bm_ref, o_ref):
        lane = lax.broadcasted_iota(jnp.int32, (1, LANES), 1)
        for hh in range(2):
            row = jnp.zeros((1, LANES), F32)
            for br in range(nbr):
                bm = bm_ref[br]
                d = db_ref[br, 0, hh * P:(hh + 1) * P, :]
                for b in range(REL_BUCKETS):
                    hit = jnp.sum(jnp.where(bm == b, d, 0.0), axis=1, keepdims=True)
                    row = row + jnp.where(lane == b, jnp.sum(hit, axis=0, keepdims=True), 0.0)
            o_ref[hh] = row

    return pl.pallas_call(
        body, name=name, out_shape=jax.ShapeDtypeStruct((2 * n_pairs, 1, LANES), F32), grid=(n_pairs,),
        in_specs=[pl.BlockSpec((nbr, 1, P2, P2), lambda h: (0, h, 0, 0)), pl.BlockSpec((nbr, P, P2), lambda h: (0, 0, 0))],
        out_specs=pl.BlockSpec((2, 1, LANES), lambda h: (h, 0, 0)),
        compiler_params=_params("parallel"),
    )(dbias, bmap)[:, 0, :]


def _rows(start, dilation):
    if dilation == 1:
        return pl.ds(pl.multiple_of(start, ATT_BLOCK), ATT_BLOCK)
    return pl.ds(start, ATT_BLOCK, stride=dilation)


def _for_each_block(seq, unroll, fn):
    P = ATT_BLOCK
    for br, (_, d) in enumerate(DILATED_BRANCHES):
        nb = seq // d // P
        u = unroll if (unroll % nb == 0 or nb % unroll == 0) else 1
        step = d * P

        def some(i, carry, br=br, d=d, nb=nb, u=u, step=step):
            blocks = []
            if u % nb == 0:
                for k in range(u):
                    if k % nb == 0:
                        start = i * (u // nb) + k // nb
                        blocks.append((start, start, 1))
                    else:
                        blocks.append((blocks[-1][0] + step, blocks[-1][0], 0))
            else:
                r, j0 = (i * u) // nb, (i * u) % nb
                blocks.append((r + j0 * step, r + jnp.maximum(j0 - 1, 0) * step, jnp.where(j0 == 0, 1, 0)))
                for _ in range(1, u):
                    blocks.append((blocks[-1][0] + step, blocks[-1][0], 0))
            fn(br, d, blocks)
            return carry

        lax.fori_loop(0, nb * d // u, some, 0)


class _RowCache:
    def __init__(self, dilation):
        self.dilation, self.seen = dilation, {}

    def rows(self, ref, start):
        key = (id(ref), id(start))
        if key not in self.seen:
            self.seen[key] = ref[_rows(start, self.dilation), :].astype(BF16)
        return self.seen[key]

    def window(self, ref, start, prev):
        return jnp.concatenate([self.rows(ref, prev), self.rows(ref, start)], axis=0)


def _stack_heads(x, head0):
    return jnp.concatenate([jnp.where(head0, x, 0.0), jnp.where(head0, 0.0, x)], axis=0).astype(BF16)


def _attn_fwd(q, kv, bias, name):
    B, S, D = q.shape
    P = ATT_BLOCK
    n_pairs = D // LANES
    nbr = len(DILATED_BRANCHES)
    scale = HEAD_DIM ** -0.5

    def body(q_ref, k_ref, v_ref, bias_ref, o_ref, lse_ref, *stats):
        m_s, l_s, acc_s = stats[0:nbr], stats[nbr:2 * nbr], stats[2 * nbr:3 * nbr]
        head0 = lax.broadcasted_iota(jnp.int32, (P, LANES), 1) < HEAD_DIM

        def block(br, d, blocks):
            cache = _RowCache(d)
            s = [lax.dot_general(_stack_heads(q_ref[_rows(start, d), :] * scale, head0),
                                 cache.window(k_ref, start, prev), (((1,), (1,)), ((), ())),
                                 preferred_element_type=F32) + bias_ref[br, 0, first]
                 for start, prev, first in blocks]
            m = [jnp.max(x, axis=-1, keepdims=True) for x in s]
            p = [jnp.exp(x - y) for x, y in zip(s, m)]
            l = [jnp.sum(x, axis=-1, keepdims=True) for x in p]
            pv = [jnp.dot(x.astype(BF16), cache.window(v_ref, start, prev), preferred_element_type=F32)
                  for x, (start, prev, _) in zip(p, blocks)]
            for k, (start, _, _) in enumerate(blocks):
                rows = _rows(start, d)
                m_s[br][rows, :] = jnp.where(head0, m[k][:P], m[k][P:])
                l_s[br][rows, :] = jnp.where(head0, l[k][:P], l[k][P:])
                acc_s[br][rows, :] = jnp.where(head0, pv[k][:P], pv[k][P:])

        _for_each_block(S, ATTN_FWD_UNROLL, block)

        chunk = _tile(S, 256, SUBLANES_F32)

        def merge(i, carry):
            rows = pl.ds(pl.multiple_of(i * chunk, chunk), chunk)
            ms = [m_s[br][rows, :] for br in range(nbr)]
            m = functools.reduce(jnp.maximum, ms)
            l = jnp.zeros((chunk, LANES), F32)
            acc = jnp.zeros((chunk, LANES), F32)
            for br in range(nbr):
                w = jnp.exp(ms[br] - m)
                l = l + w * l_s[br][rows, :]
                acc = acc + w * acc_s[br][rows, :]
            o_ref[rows, :] = acc / l
            lse_ref[rows, :] = m + jnp.log(l)
            return carry

        lax.fori_loop(0, S // chunk, merge, 0)

    slab = lambda col0: pl.BlockSpec((None, S, LANES), lambda b, h: (b, 0, col0 + h))
    return pl.pallas_call(
        body, name=name,
        out_shape=(jax.ShapeDtypeStruct((B, S, D), F32), jax.ShapeDtypeStruct((B, S, D), F32)),
        grid=(B, n_pairs),
        in_specs=[slab(0), slab(0), slab(n_pairs),
                  pl.BlockSpec((nbr, 1, 2, 2 * P, 2 * P), lambda b, h: (0, h, 0, 0, 0))],
        out_specs=(slab(0), slab(0)),
        scratch_shapes=[pltpu.VMEM((S, LANES), F32)] * (3 * nbr),
        compiler_params=_params("parallel", "parallel"),
    )(q, kv, kv, bias)


def _attn_bwd(q, kv, o, lse, do, bias, name, comm=None):
    B, S, D = q.shape
    P = ATT_BLOCK
    n_pairs = D // LANES
    nbr = len(DILATED_BRANCHES)
    scale = HEAD_DIM ** -0.5

    def body(q_ref, k_ref, v_ref, o_ref, lse_ref, do_ref, bias_ref, dq_ref, dk_ref, dv_ref, dbias_ref, delta_s):
        head0 = lax.broadcasted_iota(jnp.int32, (P, LANES), 1) < HEAD_DIM

        @pl.when(pl.program_id(1) == 0)
        def _():
            dbias_ref[...] = jnp.zeros_like(dbias_ref)

        chunk = _tile(S, 512, SUBLANES_F32)

        def prepare(i, carry):
            rows = pl.ds(pl.multiple_of(i * chunk, chunk), chunk)
            x = do_ref[rows, :] * o_ref[rows, :]
            h0 = lax.broadcasted_iota(jnp.int32, (chunk, LANES), 1) < HEAD_DIM
            d0 = jnp.sum(jnp.where(h0, x, 0.0), axis=-1, keepdims=True)
            d1 = jnp.sum(jnp.where(h0, 0.0, x), axis=-1, keepdims=True)
            delta_s[rows, :] = jnp.where(h0, d0, d1)
            zero = jnp.zeros((chunk, LANES), F32)
            dq_ref[rows, :] = zero
            dk_ref[rows, :] = zero
            dv_ref[rows, :] = zero
            return carry

        lax.fori_loop(0, S // chunk, prepare, 0)

        def per_head(x):
            return jnp.concatenate([x[:, 0:1], x[:, HEAD_DIM:HEAD_DIM + 1]], axis=0)

        nt = (((1,), (1,)), ((), ()))
        tn = (((0,), (0,)), ((), ()))

        def block(br, d, blocks):
            cache = _RowCache(d)
            q2 = [_stack_heads(q_ref[_rows(start, d), :] * scale, head0) for start, _, _ in blocks]
            do2 = [_stack_heads(do_ref[_rows(start, d), :], head0) for start, _, _ in blocks]
            kb = [cache.window(k_ref, start, prev) for start, prev, _ in blocks]
            vb = [cache.window(v_ref, start, prev) for start, prev, _ in blocks]
            s = [lax.dot_general(a, b, nt, preferred_element_type=F32) + bias_ref[br, 0, first]
                 for a, b, (_, _, first) in zip(q2, kb, blocks)]
            dp = [lax.dot_general(a, b, nt, preferred_element_type=F32) for a, b in zip(do2, vb)]
            p = [jnp.exp(x - per_head(lse_ref[_rows(start, d), :])) for x, (start, _, _) in zip(s, blocks)]
            ds = [x * (y - per_head(delta_s[_rows(start, d), :])) for x, y, (start, _, _) in zip(p, dp, blocks)]
            for x in ds:
                dbias_ref[br, 0] += x
            ds16 = [x.astype(BF16) for x in ds]
            dq2 = [jnp.dot(a, b, preferred_element_type=F32) for a, b in zip(ds16, kb)]
            dk = [lax.dot_general(a, b, tn, preferred_element_type=F32) for a, b in zip(ds16, q2)]
            dv = [lax.dot_general(a.astype(BF16), b, tn, preferred_element_type=F32) for a, b in zip(p, do2)]
            parts = {}
            for k, (start, prev, first) in enumerate(blocks):
                dq_ref[_rows(start, d), :] += jnp.where(head0, dq2[k][:P], dq2[k][P:]) * scale
                parts.setdefault(id(start), [start, []])[1].append((dk[k][P:], dv[k][P:]))
                if not (isinstance(first, int) and first == 1):
                    parts.setdefault(id(prev), [prev, []])[1].append((dk[k][:P], dv[k][:P]))
            for start, terms in parts.values():
                rows = _rows(start, d)
                dk_ref[rows, :] += functools.reduce(jnp.add, [t[0] for t in terms])
                dv_ref[rows, :] += functools.reduce(jnp.add, [t[1] for t in terms])

        _for_each_block(S, ATTN_BWD_UNROLL, block)

    slab = lambda col0: pl.BlockSpec((None, S, LANES), lambda h, b: (b, 0, col0 + h))
    tab = pl.BlockSpec((nbr, 1, 2, 2 * P, 2 * P), lambda h, b: (0, h, 0, 0, 0))
    dtab = pl.BlockSpec((nbr, 1, 2 * P, 2 * P), lambda h, b: (0, h, 0, 0))
    shp = jax.ShapeDtypeStruct((B, S, D), F32)
    return _pallas(
        body, comm=comm, name=name,
        out_shape=(shp, shp, shp, jax.ShapeDtypeStruct((nbr, n_pairs, 2 * P, 2 * P), F32)),
        grid=(n_pairs, B),
        in_specs=[slab(0), slab(0), slab(n_pairs), slab(0), slab(0), slab(0), tab],
        out_specs=(slab(0), slab(0), slab(0), dtab),
        scratch_shapes=[pltpu.VMEM((S, LANES), F32)],
        compiler_params=_params("parallel", "arbitrary"),
    )(q, kv, kv, o, lse, do, bias)


def _adamw(w, g, m, v, name):
    R, C = w.shape
    tr = _tile(R, 256, SUBLANES_F32) if R % SUBLANES_F32 == 0 else R
    tc = _tile(C, 2048, LANES) if C % LANES == 0 else C

    def body(w_ref, g_ref, m_ref, v_ref, d_ref, nm_ref, nv_ref):
        g_ = g_ref[...]
        m2 = ADAM_B1 * m_ref[...] + (1.0 - ADAM_B1) * g_
        v2 = ADAM_B2 * v_ref[...] + (1.0 - ADAM_B2) * (g_ * g_)
        m_hat = m2 / (1.0 - ADAM_B1 ** ADAM_STEP)
        v_hat = v2 / (1.0 - ADAM_B2 ** ADAM_STEP)
        d_ref[...] = -ADAM_LR * (m_hat / (jnp.sqrt(v_hat) + ADAM_EPS) + ADAM_WD * w_ref[...])
        nm_ref[...] = m2
        nv_ref[...] = v2

    blk = pl.BlockSpec((tr, tc), lambda i, j: (i, j))
    shp = jax.ShapeDtypeStruct((R, C), F32)
    return pl.pallas_call(
        body, name=name, out_shape=(shp, shp, shp), grid=(R // tr, C // tc),
        in_specs=[blk] * 4, out_specs=(blk,) * 3, compiler_params=_params("parallel", "parallel"),
    )(w, g, m, v)


def _sum_slots(slots, own, name):
    n, R, C = slots.shape
    tr = _tile(R, 256, SUBLANES_BF16) if R % SUBLANES_BF16 == 0 else R
    tc = _tile(C, 2048, LANES) if C % LANES == 0 else C

    def body(s_ref, *rest):
        o_ref = rest[-1]
        acc = s_ref[0].astype(F32)
        if own is not None:
            acc = rest[0][...].astype(F32) + acc
        for k in range(1, n):
            acc = acc + s_ref[k].astype(F32)
        o_ref[...] = acc

    in_specs = [pl.BlockSpec((n, tr, tc), lambda i, j: (0, i, j))]
    args = [slots]
    if own is not None:
        in_specs.append(pl.BlockSpec((tr, tc), lambda i, j: (i, j)))
        args.append(own)
    return pl.pallas_call(
        body, name=name, out_shape=jax.ShapeDtypeStruct((R, C), F32), grid=(R // tr, C // tc),
        in_specs=in_specs, out_specs=pl.BlockSpec((tr, tc), lambda i, j: (i, j)),
        compiler_params=_params("parallel", "parallel"),
    )(*args)


def _my_place():
    return lax.axis_index("x"), lax.axis_index("y"), lax.axis_index("c")


def _other_chips(x, y):
    return [(1 - x, y), (x, 1 - y), (1 - x, 1 - y)]


def _piece(ref, blk, axis, shard, half):
    h0 = blk[0] // 2
    idx = []
    for dim, n in enumerate(blk):
        if dim == 0:
            start = half * h0 + (shard * n if axis == 0 else 0)
            idx.append(pl.ds(start, h0))
        elif dim == axis:
            idx.append(pl.ds(shard * n, n))
        else:
            idx.append(slice(None))
    return ref.at[tuple(idx)]


def _half_of(ref, blk, half):
    return ref.at[pl.ds(half * (blk[0] // 2), blk[0] // 2)]


def _gather_weights(shards, axes, names, small):
    n = len(shards)
    blks = [s.shape for s in shards]
    task = _gather_ici_task(shards, axes, names)

    def stage1(*refs):
        small_in, outs, small_out = refs[n], refs[n + 1:2 * n + 1], refs[2 * n + 1]
        send_sems, recv_sems, local_sems = refs[2 * n + 2:]
        sem = lambda k: (send_sems.at[k], recv_sems.at[k])
        x, y, c = _my_place()
        me = 2 * x + y
        local = pltpu.make_async_copy(small_in, small_out.at[me], local_sems.at[0])
        local.start()
        sends, recvs = task.copies(None, outs, sem)
        for k, (px, py) in enumerate(_other_chips(x, y)):
            sends.append(_remote(small_in, small_out.at[me], sem(task.n_sems + k), (px, py, c)))
            recvs.append(_remote(small_in, small_out.at[2 * px + py], sem(task.n_sems + k), (px, py, c)))
        for cp in sends:
            cp.start()
        for cp in recvs:
            cp.wait_recv()
        for cp in sends:
            cp.wait_send()
        local.wait()

    res = pl.pallas_call(
        stage1, name="gather_weights_ici",
        out_shape=task.outs + [jax.ShapeDtypeStruct((N_SHARDS,) + small.shape, small.dtype)],
        in_specs=[ANY] * (n + 1), out_specs=[ANY] * (n + 1), input_output_aliases={a: a for a in range(n)},
        scratch_shapes=[pltpu.SemaphoreType.DMA((task.n_sems + 3,)), pltpu.SemaphoreType.DMA((task.n_sems + 3,)),
                        pltpu.SemaphoreType.DMA((1,))],
    )(*task.ins, small)
    full = _run_comm(_gather_d2d_task(list(res[:n]), blks, axes), "gather_weights_d2d")
    return full, res[n]


def _run_comm(comm, name):
    n_in, n_out = len(comm.ins), len(comm.outs)

    def body(*refs):
        send_sems, recv_sems = refs[n_in + n_out:]
        sends, recvs = comm.copies(refs[:n_in], refs[n_in:n_in + n_out], lambda k: (send_sems.at[k], recv_sems.at[k]))
        for cp in sends:
            cp.start()
        for cp in recvs:
            cp.wait_recv()
        for cp in sends:
            cp.wait_send()

    return list(pl.pallas_call(
        body, name=name, out_shape=comm.outs, in_specs=[ANY] * n_in, out_specs=[ANY] * n_out,
        input_output_aliases=comm.aliases, scratch_shapes=[pltpu.SemaphoreType.DMA((comm.n_sems,))] * 2,
    )(*comm.ins))


def _scatter_grads(grads, blks, axes, small):
    n = len(grads)
    halves = [(b[0] // 2,) + tuple(b[1:]) for b in blks]

    def body(*refs):
        ins, small_in = refs[:n], refs[n]
        outs, small_out = refs[n + 1:2 * n + 1], refs[2 * n + 1]
        send_sems, recv_sems, local_sems = refs[2 * n + 2:]
        x, y, c = _my_place()
        me = 4 * x + 2 * y + c
        local = [pltpu.make_async_copy(small_in, small_out.at[me], local_sems.at[0])]
        for cp in local:
            cp.start()
        sends, recvs = [], []
        for rel in range(1, N_DEVICES):
            px = x ^ ((rel >> 2) & 1)
            py = y ^ ((rel >> 1) & 1)
            pc = c ^ (rel & 1)
            peer = 4 * px + 2 * py + pc
            for a in range(n + 1):
                if a < n:
                    src = _piece(ins[a], blks[a], axes[a], 2 * px + py, pc)
                    dst = land = outs[a].at[rel - 1]
                else:
                    src, dst, land = small_in, small_out.at[me], small_out.at[peer]
                sends.append(pltpu.make_async_remote_copy(
                    src_ref=src, dst_ref=dst, send_sem=send_sems.at[a, rel - 1], recv_sem=recv_sems.at[a, rel - 1],
                    device_id=(px, py, pc), device_id_type=pl.DeviceIdType.MESH))
                recvs.append(pltpu.make_async_remote_copy(
                    src_ref=src, dst_ref=land, send_sem=send_sems.at[a, rel - 1], recv_sem=recv_sems.at[a, rel - 1],
                    device_id=(px, py, pc), device_id_type=pl.DeviceIdType.MESH))
        for cp in sends:
            cp.start()
        for cp in recvs:
            cp.wait_recv()
        for cp in sends:
            cp.wait_send()
        for cp in local:
            cp.wait()

    out_shape = [jax.ShapeDtypeStruct((N_DEVICES - 1,) + h, g.dtype) for h, g in zip(halves, grads)]
    out_shape.append(jax.ShapeDtypeStruct((N_DEVICES,) + small.shape, small.dtype))
    res = pl.pallas_call(
        body, name="scatter_grads", out_shape=out_shape,
        in_specs=[ANY] * (n + 1), out_specs=[ANY] * (n + 1),
        scratch_shapes=[pltpu.SemaphoreType.DMA((n + 1, N_DEVICES - 1)), pltpu.SemaphoreType.DMA((n + 1, N_DEVICES - 1)),
                        pltpu.SemaphoreType.DMA((1,))],
    )(*grads, small)
    return list(res[:n]), res[n]


def _remote(src, dst, sems, device):
    return pltpu.make_async_remote_copy(src_ref=src, dst_ref=dst, send_sem=sems[0], recv_sem=sems[1],
                                        device_id=device, device_id_type=pl.DeviceIdType.MESH)


def _sum_piece(dest, slots, grad, blk, axis, layer, n_layers, name):
    r, c = blk
    h0 = r // 2
    tr = _tile(h0, 256, SUBLANES_BF16)
    tc = _tile(c, 2048, LANES)
    place = jnp.stack([2 * lax.axis_index("x") + lax.axis_index("y"), lax.axis_index("c")]).astype(jnp.int32)

    def body(p_ref, s_ref, g_ref, *rest):
        acc = g_ref[...].astype(F32) + s_ref[0].astype(F32)
        for k in range(1, N_DEVICES - 1):
            acc = acc + s_ref[k].astype(F32)
        rest[-1][...] = acc

    def g_map(i, j, p):
        return (p[1] * (h0 // tr) + (p[0] * (r // tr) if axis == 0 else 0) + i, (p[0] * (c // tc) if axis == 1 else 0) + j)

    in_specs = [pl.BlockSpec((N_DEVICES - 1, tr, tc), lambda i, j, p: (0, i, j)), pl.BlockSpec((tr, tc), g_map)]
    args = [place, slots, grad]
    if dest is not None:
        in_specs.append(ANY)
        args.append(dest)
    return pl.pallas_call(
        body, name=name, out_shape=jax.ShapeDtypeStruct((n_layers, r, c), F32),
        grid_spec=pltpu.PrefetchScalarGridSpec(
            num_scalar_prefetch=1, grid=(h0 // tr, c // tc), in_specs=in_specs,
            out_specs=pl.BlockSpec((None, tr, tc), lambda i, j, p: (layer, p[1] * (h0 // tr) + i, j))),
        input_output_aliases={3: 0} if dest is not None else {},
        compiler_params=_params("parallel", "parallel"),
    )(*args)


def _swap_halves(blocks):
    n = len(blocks)
    layers = [(a, l) for a, b in enumerate(blocks) for l in range(b.shape[0])]

    def copies(in_refs, out_refs, sem):
        x, y, c = _my_place()
        sends, recvs = [], []
        for k, (a, l) in enumerate(layers):
            blk = blocks[a].shape[1:]
            mine = _half_of(out_refs[a].at[l], blk, c)
            sends.append(_remote(mine, mine, sem(k), (x, y, 1 - c)))
            recvs.append(_remote(mine, _half_of(out_refs[a].at[l], blk, 1 - c), sem(k), (x, y, 1 - c)))
        return sends, recvs

    task = _Comm(blocks, [jax.ShapeDtypeStruct(b.shape, b.dtype) for b in blocks], {a: a for a in range(n)},
                 len(layers), copies)
    return _run_comm(task, "swap_grad_halves")


def _place_shard(shard, axis, name):
    r, c = shard.shape
    tr = _tile(r, 512, SUBLANES_BF16)
    full = (r * N_SHARDS, c) if axis == 0 else (r, c * N_SHARDS)
    me2 = (2 * lax.axis_index("x") + lax.axis_index("y")).astype(jnp.int32).reshape(1)

    def body(me_ref, s_ref, o_ref):
        o_ref[...] = s_ref[...].astype(o_ref.dtype)

    if axis == 0:
        out_map = lambda i, me: (me[0] * (r // tr) + i, 0)
    else:
        out_map = lambda i, me: (i, me[0])
    return pl.pallas_call(
        body, name=name, out_shape=jax.ShapeDtypeStruct(full, BF16),
        grid_spec=pltpu.PrefetchScalarGridSpec(
            num_scalar_prefetch=1, grid=(r // tr,),
            in_specs=[pl.BlockSpec((tr, c), lambda i, me: (i, 0))], out_specs=pl.BlockSpec((tr, c), out_map)),
        compiler_params=_params("parallel"),
    )(me2, shard)


def _gather_ici_task(shards, axes, names):
    n = len(shards)
    blks = [s.shape for s in shards]
    bases = [_place_shard(s, ax, f"place_{nm}") for s, ax, nm in zip(shards, axes, names)]

    def copies(in_refs, out_refs, sem):
        x, y, c = _my_place()
        me = 2 * x + y
        sends, recvs = [], []
        for a in range(n):
            mine = _piece(out_refs[a], blks[a], axes[a], me, c)
            for k, (px, py) in enumerate(_other_chips(x, y)):
                sends.append(_remote(mine, mine, sem(3 * a + k), (px, py, c)))
                recvs.append(_remote(mine, _piece(out_refs[a], blks[a], axes[a], 2 * px + py, c), sem(3 * a + k),
                                     (px, py, c)))
        return sends, recvs

    return _Comm(bases, [jax.ShapeDtypeStruct(b.shape, b.dtype) for b in bases], {a: a for a in range(n)}, 3 * n, copies)


def _gather_d2d_task(partials, blks, axes):
    n = len(partials)

    def copies(in_refs, out_refs, sem):
        x, y, c = _my_place()
        sends, recvs = [], []
        for a in range(n):
            for k, (px, py) in enumerate(_other_chips(x, y)):
                mine = _piece(out_refs[a], blks[a], axes[a], 2 * px + py, c)
                theirs = _piece(out_refs[a], blks[a], axes[a], 2 * px + py, 1 - c)
                sends.append(_remote(mine, mine, sem(3 * a + k), (x, y, 1 - c)))
                recvs.append(_remote(mine, theirs, sem(3 * a + k), (x, y, 1 - c)))
        return sends, recvs

    return _Comm(partials, [jax.ShapeDtypeStruct(p.shape, p.dtype) for p in partials], {a: a for a in range(n)},
                 3 * n, copies)


def _scatter_task(grads, blks, axes):
    n = len(grads)

    def copies(in_refs, out_refs, sem):
        x, y, c = _my_place()
        sends, recvs = [], []
        for rel in range(1, N_DEVICES):
            px, py, pc = x ^ ((rel >> 2) & 1), y ^ ((rel >> 1) & 1), c ^ (rel & 1)
            for a in range(n):
                src = _piece(in_refs[a], blks[a], axes[a], 2 * px + py, pc)
                k = (N_DEVICES - 1) * a + rel - 1
                sends.append(_remote(src, out_refs[a].at[rel - 1], sem(k), (px, py, pc)))
                recvs.append(_remote(src, out_refs[a].at[rel - 1], sem(k), (px, py, pc)))
        return sends, recvs

    outs = [jax.ShapeDtypeStruct((N_DEVICES - 1, b[0] // 2) + tuple(b[1:]), g.dtype) for b, g in zip(blks, grads)]
    return _Comm(grads, outs, {}, (N_DEVICES - 1) * n, copies)


def _pack(arrays):
    flat = jnp.concatenate([a.reshape(-1).astype(F32) for a in arrays])
    pad = (-flat.shape[0]) % (SUBLANES_F32 * LANES)
    return jnp.pad(flat, (0, pad)).reshape(-1, LANES)


def _unpack(packed, shapes):
    flat = packed.reshape(-1)
    out, off = [], 0
    for s in shapes:
        n = int(np.prod(s))
        out.append(flat[off:off + n].reshape(s))
        off += n
    return out


def _local_step(x, target, W, shards, geom, small):
    W = dict(W)
    B, S, D = x.shape
    T = B * S
    x2 = x.reshape(T, D)
    tgt = target.reshape(T, D)
    bmap = jnp.asarray(_bucket_map())
    blk = lambda names: [geom[k][0] for k in names]
    axs = lambda names: [geom[k][1] for k in names]
    ici = lambda names: _gather_ici_task([shards[k] for k in names], axs(names), names)
    d2d = lambda names, partials: _gather_d2d_task(list(partials), blk(names), axs(names))
    big, slots = {}, {}

    def scatter(names):
        return _scatter_task([big[k] for k in names], blk(names), axs(names))

    def ffn_bwd(l, dout, h, saved, first, second):
        xn, u, v, a = saved
        da = _mm_nt(dout, W[f"w_down{l}"], BF16, f"ffn{l}_down_dx")
        big[f"w_down{l}"] = _mm_tn(a, dout, f"ffn{l}_down_dw")
        names = first + [f"w_down{l}"]
        du, g_cw, g_cb, landed = _ffn_gate_bwd(u, v, small["ffn_conv"][l], da, S, f"ffn{l}_gate_bwd", comm=scatter(names))
        slots.update(zip(names, landed))
        big[f"w_up{l}"] = _mm_tn(xn, du, f"ffn{l}_up_dw")
        norm = (h, small["ffn_norm"][l], dout)
        if second:
            (dh, g_norm), (slots[f"w_up{l}"],) = _mm_nt(du, W[f"w_up{l}"], None, f"ffn{l}_up_dx",
                                                        comm=scatter([f"w_up{l}"]), norm=norm)
        else:
            dh, g_norm = _mm_nt(du, W[f"w_up{l}"], None, f"ffn{l}_up_dx", norm=norm)
        return dh, g_cw, g_cb, g_norm[0]

    xn0 = _rmsnorm_fwd(x2, small["a_norm"][0], "a_norm")
    p, part = _mm_nn(xn0, W["w_in"], None, BF16, "a_in", comm=ici(["w_up0"]))
    z, (W["w_up0"],) = _gate_a_fwd(p, small["a_conv"][0], S, "a_gate", comm=d2d(["w_up0"], part))
    (h1, xn1), part = _mm_nn(z, W["w_out"], x2, F32, "a_out", comm=ici(["w_down0"]), norm_gains=[small["ffn_norm"][0]])
    later = ["w_kv", "w_q", "w_o", "w_up1"]
    u0, landed = _mm_nn(xn1, W["w_up0"], None, BF16, "ffn0_up", comm=_Comm.join([d2d(["w_down0"], part), ici(later)]))
    W["w_down0"] = landed[0]
    (a0, v0), landed = _ffn_gate_fwd(u0, small["ffn_conv"][0], small["ffn_conv_b"][0], S, "ffn0_gate",
                               comm=_Comm.join([d2d(later, landed[1:]), ici(["w_down1"])]))
    W.update(zip(later, landed[:len(later)]))
    (h2, kvn, xn3), (W["w_down1"],) = _mm_nn(a0, W["w_down0"], h1, F32, "ffn0_down",
                                            comm=d2d(["w_down1"], landed[len(later):]),
                                            norm_gains=[small["kv_norm"], small["b_norm"][0]])
    kv = _mm_nn(kvn, W["w_kv"], None, F32, "kv_proj")
    q = _mm_nn(xn3, W["w_q"], None, F32, "q_proj")
    bias = _bias_tables(small["rel_bias"], bmap, "rel_bias_tables")
    q3, kv3 = q.reshape(B, S, D), kv.reshape(B, S, 2 * D)
    o3, lse3 = _attn_fwd(q3, kv3, bias, "attn_fwd")
    o = o3.reshape(T, D)
    h3, xn4 = _mm_nn(o, W["w_o"], h2, F32, "o_proj", norm_gains=[small["ffn_norm"][1]])
    u1 = _mm_nn(xn4, W["w_up1"], None, BF16, "ffn1_up")
    a1, v1 = _ffn_gate_fwd(u1, small["ffn_conv"][1], small["ffn_conv_b"][1], S, "ffn1_gate")
    h4 = _mm_nn(a1, W["w_down1"], h3, F32, "ffn1_down")
    sq_err, dh4, g_final = _loss_head(h4, small["final_norm"], tgt, "loss_head")
    loss = 0.5 * jnp.sum(sq_err) / D

    dh3, g_cw1, g_cb1, g_fn1 = ffn_bwd(1, dh4, h3, (xn4, u1, v1, a1), [], False)
    do = _mm_nt(dh3, W["w_o"], F32, "o_proj_dx")
    big["w_o"] = _mm_tn(o, dh3, "o_proj_dw")
    (dq3, dk3, dv3, dbias), landed = _attn_bwd(q3, kv3, o3, lse3, do.reshape(B, S, D), bias, "attn_bwd",
                                               comm=scatter(["w_up1", "w_o"]))
    slots.update(zip(["w_up1", "w_o"], landed))
    g_rel = _bias_grad(dbias, bmap, "rel_bias_grad")[:, :REL_BUCKETS].T
    dq, dk, dv = dq3.reshape(T, D), dk3.reshape(T, D), dv3.reshape(T, D)
    dh2, g_bn = _mm_nt(dq, W["w_q"], None, "q_proj_dx", norm=(h2, small["b_norm"][0], dh3))
    big["w_q"] = _mm_tn(xn3, dq, "q_proj_dw")
    dh2, g_kvn = _mm_nt([dk, dv], W["w_kv"], None, "kv_proj_dx", norm=(h2, small["kv_norm"], dh2))
    big["w_kv"] = jnp.concatenate([_mm_tn(kvn, dk, "k_proj_dw"), _mm_tn(kvn, dv, "v_proj_dw")], axis=1)
    dh1, g_cw0, g_cb0, g_fn0 = ffn_bwd(0, dh2, h1, (xn1, u0, v0, a0), ["w_q", "w_kv"], True)
    dz = _mm_nt(dh1, W["w_out"], BF16, "a_out_dx")
    big["w_out"] = _mm_tn(z, dh1, "a_out_dw")
    dp, g_aconv, (slots["w_out"],) = _gate_a_bwd(p, small["a_conv"][0], dz, S, "a_gate_bwd", comm=scatter(["w_out"]))
    big["w_in"] = _mm_tn(xn0, dp, "a_in_dw")
    (dx, g_an), (slots["w_in"],) = _mm_nt(dp, W["w_in"], None, "a_in_dx", comm=scatter(["w_in"]),
                                          norm=(x2, small["a_norm"][0], dh1))
    g_bn, g_kvn, g_an = g_bn[0], g_kvn[0], g_an[0]

    small_g = {"a_norm": g_an[None], "a_conv": g_aconv[None], "kv_norm": g_kvn, "b_norm": g_bn[None],
               "rel_bias": g_rel, "ffn_norm": jnp.stack([g_fn0, g_fn1]), "ffn_conv": jnp.stack([g_cw0, g_cw1]),
               "ffn_conv_b": jnp.stack([g_cb0, g_cb1]), "final_norm": g_final}
    return loss, dx.reshape(B, S, D), big, slots, small_g


BIG = ("w_in", "w_out", "w_kv", "w_q", "w_o", "w_up0", "w_up1", "w_down0", "w_down1")
SMALL = ("a_norm", "a_conv", "kv_norm", "b_norm", "rel_bias", "ffn_norm", "ffn_conv", "ffn_conv_b", "final_norm")
SMALL_SHARDED = ("a_norm", "a_conv", "ffn_conv")
WEIGHT_ORDER = ("a_norm", "a_w_in", "a_conv", "a_w_out", "kv_norm", "w_kv", "b_norm", "b_w_q", "b_w_o", "rel_bias",
                "ffn_norm", "ffn_w_up", "ffn_conv", "ffn_conv_b", "ffn_w_down", "final_norm")
GRAD_OF = {"w_in": ("a_w_in", 0), "w_out": ("a_w_out", 0), "w_kv": ("w_kv", 0), "w_q": ("b_w_q", 0), "w_o": ("b_w_o", 0),
           "w_up0": ("ffn_w_up", 0), "w_up1": ("ffn_w_up", 1), "w_down0": ("ffn_w_down", 0), "w_down1": ("ffn_w_down", 1)}


def _as2d(a):
    return a.reshape(-1, a.shape[-1])


def kernel(x, a_norm, a_w_in, a_conv, a_w_out, kv_norm, w_kv, b_norm, b_w_q, b_w_o, rel_bias, ffn_norm, ffn_w_up, ffn_conv, ffn_conv_b, ffn_w_down, final_norm, loss_target, m_a_norm, m_a_w_in, m_a_conv, m_a_w_out, m_kv_norm, m_w_kv, m_b_norm, m_b_w_q, m_b_w_o, m_rel_bias, m_ffn_norm, m_ffn_w_up, m_ffn_conv, m_ffn_conv_b, m_ffn_w_down, m_final_norm, v_a_norm, v_a_w_in, v_a_conv, v_a_w_out, v_kv_norm, v_w_kv, v_b_norm, v_b_w_q, v_b_w_o, v_rel_bias, v_ffn_norm, v_ffn_w_up, v_ffn_conv, v_ffn_conv_b, v_ffn_w_down, v_final_norm):
    given = dict(a_norm=a_norm, a_w_in=a_w_in, a_conv=a_conv, a_w_out=a_w_out, kv_norm=kv_norm, w_kv=w_kv, b_norm=b_norm,
                 b_w_q=b_w_q, b_w_o=b_w_o, rel_bias=rel_bias, ffn_norm=ffn_norm, ffn_w_up=ffn_w_up, ffn_conv=ffn_conv,
                 ffn_conv_b=ffn_conv_b, ffn_w_down=ffn_w_down, final_norm=final_norm)
    mom_m = dict(a_norm=m_a_norm, a_w_in=m_a_w_in, a_conv=m_a_conv, a_w_out=m_a_w_out, kv_norm=m_kv_norm, w_kv=m_w_kv,
                 b_norm=m_b_norm, b_w_q=m_b_w_q, b_w_o=m_b_w_o, rel_bias=m_rel_bias, ffn_norm=m_ffn_norm,
                 ffn_w_up=m_ffn_w_up, ffn_conv=m_ffn_conv, ffn_conv_b=m_ffn_conv_b, ffn_w_down=m_ffn_w_down,
                 final_norm=m_final_norm)
    mom_v = dict(a_norm=v_a_norm, a_w_in=v_a_w_in, a_conv=v_a_conv, a_w_out=v_a_w_out, kv_norm=v_kv_norm, w_kv=v_w_kv,
                 b_norm=v_b_norm, b_w_q=v_b_w_q, b_w_o=v_b_w_o, rel_bias=v_rel_bias, ffn_norm=v_ffn_norm,
                 ffn_w_up=v_ffn_w_up, ffn_conv=v_ffn_conv, ffn_conv_b=v_ffn_conv_b, ffn_w_down=v_ffn_w_down,
                 final_norm=v_final_norm)

    shard = {"w_in": (a_w_in[0], 1), "w_out": (a_w_out[0], 0), "w_kv": (w_kv, 1), "w_q": (b_w_q[0], 0),
             "w_o": (b_w_o[0], 0), "w_up0": (ffn_w_up[0], 1), "w_up1": (ffn_w_up[1], 1),
             "w_down0": (ffn_w_down[0], 0), "w_down1": (ffn_w_down[1], 0)}
    blks = [shard[k][0].shape for k in BIG]
    axes = [shard[k][1] for k in BIG]

    small_sharded = [given[k] for k in SMALL_SHARDED]
    packed = _pack(small_sharded)
    first = ("w_in", "w_out")
    fulls, packed_all = _gather_weights([shard[k][0] for k in first], [shard[k][1] for k in first], first, packed)
    W = dict(zip(first, fulls))
    later = {k: shard[k][0] for k in BIG if k not in first}
    geom = {k: (shard[k][0].shape, shard[k][1]) for k in BIG}
    small = {k: given[k] for k in SMALL}
    per_shard = [_unpack(packed_all[j], [a.shape for a in small_sharded]) for j in range(N_SHARDS)]
    for i, k in enumerate(SMALL_SHARDED):
        small[k] = jnp.concatenate([per_shard[j][i] for j in range(N_SHARDS)], axis=-1)

    loss, grad_x, big_g, slots, small_g = _local_step(x, loss_target, W, later, geom, small)
    loss = lax.psum(loss, MESH_AXES)

    small_shapes = [small_g[k].shape for k in SMALL]
    _, small_slots = _scatter_grads([], [], [], _pack([small_g[k] for k in SMALL]))
    layers_of = {}
    for k in BIG:
        layers_of.setdefault(GRAD_OF[k][0], []).append(k)
    blocks = {}
    for name, members in layers_of.items():
        dest = None
        for k in members:
            dest = _sum_piece(dest, slots[k], big_g[k], geom[k][0], geom[k][1], GRAD_OF[k][1], len(members), f"sum_{k}")
        blocks[name] = dest
    reduced = dict(zip(blocks, _swap_halves(list(blocks.values()))))
    small_sum = _sum_slots(small_slots, None, "sum_small")
    small_red = dict(zip(SMALL, _unpack(small_sum, small_shapes)))
    j = 2 * lax.axis_index("x") + lax.axis_index("y")
    for k in SMALL_SHARDED:
        w = given[k].shape[-1]
        small_red[k] = lax.dynamic_slice_in_dim(small_red[k], j * w, w, axis=small_red[k].ndim - 1)

    grads, deltas, new_m, new_v = {}, {}, {}, {}
    for name in WEIGHT_ORDER:
        if name in reduced:
            g = reduced[name].reshape(given[name].shape)
            d, nm, nv = _adamw(_as2d(given[name]), _as2d(g), _as2d(mom_m[name]), _as2d(mom_v[name]), f"adamw_{name}")
            grads[name] = g
            deltas[name], new_m[name], new_v[name] = (t.reshape(given[name].shape) for t in (d, nm, nv))
    small_names = [n for n in WEIGHT_ORDER if n not in reduced]
    for n in small_names:
        grads[n] = small_red[n].reshape(given[n].shape)
    sw, sg, sm, sv = (_pack([d[n] for n in small_names]) for d in (given, grads, mom_m, mom_v))
    d, nm, nv = _adamw(sw, sg, sm, sv, "adamw_small")
    shapes = [given[n].shape for n in small_names]
    for n, a, b_, c_ in zip(small_names, _unpack(d, shapes), _unpack(nm, shapes), _unpack(nv, shapes)):
        deltas[n], new_m[n], new_v[n] = a, b_, c_

    return (loss, grad_x, *[grads[n] for n in WEIGHT_ORDER], *[deltas[n] for n in WEIGHT_ORDER],
            *[new_m[n] for n in WEIGHT_ORDER], *[new_v[n] for n in WEIGHT_ORDER])
```

```python
import functools
import math

import numpy as np

import jax
import jax.numpy as jnp
from jax import lax
from jax.experimental import pallas as pl
from jax.experimental.pallas import tpu as pltpu

F32 = jnp.float32
BF16 = jnp.bfloat16

RMS_EPS = 1e-6
HEAD_DIM = 64
ATT_BLOCK = 128
DILATED_BRANCHES = ((128, 1), (512, 4), (2048, 16))
REL_BUCKETS = 32
REL_MAX_DISTANCE = 2048
MASKED_LOGIT = -1e30
ATTN_FWD_UNROLL = (8, 8, 4)
ATTN_BWD_UNROLL = (8, 8, 8)

ADAM_LR = 0.001
ADAM_B1 = 0.9
ADAM_B2 = 0.999
ADAM_EPS = 1e-08
ADAM_WD = 0.01
ADAM_STEP = 10

LANES = 128
SUBLANES_F32 = 8
SUBLANES_BF16 = 16
VMEM_LIMIT_BYTES = 56 * 1024 * 1024

MESH_AXES = ("x", "y", "c")
N_SHARDS = 4
N_DEVICES = 8
ANY = pl.BlockSpec(memory_space=pl.ANY)


def _tile(n, pref, mult):
    best = None
    for t in range(mult, min(n, pref) + 1, mult):
        if n % t == 0:
            best = t
    if best is None:
        raise ValueError(f"no tile for {n} (multiple of {mult}, at most {pref})")
    return best


def _params(*sem):
    return pltpu.CompilerParams(dimension_semantics=sem, vmem_limit_bytes=VMEM_LIMIT_BYTES)


class _Comm:
    def __init__(self, ins, outs, aliases, n_sems, copies):
        self.ins, self.outs, self.aliases, self.n_sems, self.copies = list(ins), list(outs), dict(aliases), n_sems, copies

    @staticmethod
    def join(parts):
        parts = [p for p in parts if p is not None]
        ins, outs, aliases, offs, n_sems = [], [], {}, [], 0
        for p in parts:
            offs.append((len(ins), len(outs), n_sems))
            aliases.update({len(ins) + i: len(outs) + o for i, o in p.aliases.items()})
            ins += p.ins
            outs += p.outs
            n_sems += p.n_sems

        def copies(in_refs, out_refs, sem):
            sends, recvs = [], []
            for p, (i0, o0, s0) in zip(parts, offs):
                s, r = p.copies(in_refs[i0:i0 + len(p.ins)], out_refs[o0:o0 + len(p.outs)],
                                lambda k, s0=s0: sem(s0 + k))
                sends += s
                recvs += r
            return sends, recvs

        return _Comm(ins, outs, aliases, n_sems, copies)


def _pallas(body, *, comm=None, name, out_shape, grid=(), in_specs=(), out_specs=(), scratch_shapes=(),
            compiler_params=None):
    if comm is None:
        return pl.pallas_call(body, name=name, out_shape=out_shape, grid=grid, in_specs=in_specs, out_specs=out_specs,
                              scratch_shapes=scratch_shapes, compiler_params=compiler_params)
    single = not isinstance(out_shape, (tuple, list))
    outs = (out_shape,) if single else tuple(out_shape)
    o_specs = (out_specs,) if single else tuple(out_specs)
    n_in, n_cin, n_out, n_cout, n_scr = len(in_specs), len(comm.ins), len(outs), len(comm.outs), len(scratch_shapes)

    def carried(*refs):
        base_in, c_in = refs[:n_in], refs[n_in:n_in + n_cin]
        o0 = n_in + n_cin
        base_out, c_out = refs[o0:o0 + n_out], refs[o0 + n_out:o0 + n_out + n_cout]
        s0 = o0 + n_out + n_cout
        base_scr, (send_sems, recv_sems) = refs[s0:s0 + n_scr], refs[s0 + n_scr:]
        sem = lambda k: (send_sems.at[k], recv_sems.at[k])
        first = functools.reduce(jnp.logical_and, [pl.program_id(a) == 0 for a in range(len(grid))])
        last = functools.reduce(jnp.logical_and, [pl.program_id(a) == n - 1 for a, n in enumerate(grid)])

        @pl.when(first)
        def _():
            for cp in comm.copies(c_in, c_out, sem)[0]:
                cp.start()

        body(*base_in, *base_out, *base_scr)

        @pl.when(last)
        def _():
            sends, recvs = comm.copies(c_in, c_out, sem)
            for cp in recvs:
                cp.wait_recv()
            for cp in sends:
                cp.wait_send()

    call = pl.pallas_call(
        carried, name=name, out_shape=outs + tuple(comm.outs), grid=grid,
        in_specs=list(in_specs) + [ANY] * n_cin, out_specs=o_specs + (ANY,) * n_cout,
        scratch_shapes=list(scratch_shapes) + [pltpu.SemaphoreType.DMA((comm.n_sems,))] * 2,
        input_output_aliases={n_in + i: n_out + o for i, o in comm.aliases.items()},
        compiler_params=_params(*(["arbitrary"] * len(grid))))

    def run(*args):
        res = call(*args, *comm.ins)
        base = res[0] if single else tuple(res[:n_out])
        return base, list(res[n_out:])

    return run


def _rmsnorm_fwd(x, g, name):
    T, D = x.shape
    tm = _tile(T, 512, SUBLANES_BF16)

    def body(x_ref, g_ref, o_ref):
        xf = x_ref[...]
        r = lax.rsqrt(jnp.mean(xf * xf, axis=-1, keepdims=True) + RMS_EPS)
        o_ref[...] = ((xf * r) * g_ref[...]).astype(o_ref.dtype)

    return pl.pallas_call(
        body, name=name, out_shape=jax.ShapeDtypeStruct((T, D), BF16), grid=(T // tm,),
        in_specs=[pl.BlockSpec((tm, D), lambda i: (i, 0)), pl.BlockSpec((1, D), lambda i: (0, 0))],
        out_specs=pl.BlockSpec((tm, D), lambda i: (i, 0)),
        compiler_params=_params("parallel"),
    )(x, g.reshape(1, D))


def _loss_head(h, g, target, name):
    T, D = h.shape
    tm = _tile(T, 512, SUBLANES_F32)

    def body(h_ref, g_ref, t_ref, dh_ref, acc_ref):
        @pl.when(pl.program_id(0) == 0)
        def _():
            acc_ref[...] = jnp.zeros_like(acc_ref)

        xf = h_ref[...]
        r = lax.rsqrt(jnp.mean(xf * xf, axis=-1, keepdims=True) + RMS_EPS)
        xhat = xf * r
        err = xhat * g_ref[...] - t_ref[...]
        dy = err * (1.0 / D)
        acc_ref[0:1, :] += jnp.sum(dy * xhat, axis=0, keepdims=True)
        acc_ref[1:2, :] += jnp.sum(err * err, axis=0, keepdims=True)
        t = dy * g_ref[...]
        dh_ref[...] = r * (t - xhat * jnp.mean(t * xhat, axis=-1, keepdims=True))

    row = pl.BlockSpec((tm, D), lambda i: (i, 0))
    dh, acc = pl.pallas_call(
        body, name=name,
        out_shape=(jax.ShapeDtypeStruct((T, D), F32), jax.ShapeDtypeStruct((SUBLANES_F32, D), F32)),
        grid=(T // tm,),
        in_specs=[row, pl.BlockSpec((1, D), lambda i: (0, 0)), row],
        out_specs=(row, pl.BlockSpec((SUBLANES_F32, D), lambda i: (0, 0))),
        compiler_params=_params("arbitrary"),
    )(h, g.reshape(1, D), target)
    return acc[1], dh, acc[0]


def _mm_nn(a, b, res, out_dtype, name, comm=None, norm_gains=()):
    T, K = a.shape
    N = b.shape[1]
    tm = _tile(T, 512, SUBLANES_BF16)
    tn = _tile(N, 3072, LANES)
    n_g = len(norm_gains)
    assert not n_g or tn == N, "the fused rmsnorm needs whole rows in one tile"

    def body(a_ref, b_ref, *rest):
        ins, outs = rest[:len(rest) - 1 - n_g], rest[len(rest) - 1 - n_g:]
        acc = jnp.dot(a_ref[...].astype(BF16), b_ref[...], preferred_element_type=F32)
        if res is not None:
            acc = acc + ins[0][...]
        outs[0][...] = acc.astype(outs[0].dtype)
        if n_g:
            y = acc * lax.rsqrt(jnp.mean(acc * acc, axis=-1, keepdims=True) + RMS_EPS)
            for g_ref, xn_ref in zip(ins[len(ins) - n_g:], outs[1:]):
                xn_ref[...] = (y * g_ref[...]).astype(xn_ref.dtype)

    tile = pl.BlockSpec((tm, tn), lambda j, i: (i, j))
    in_specs = [pl.BlockSpec((tm, K), lambda j, i: (i, 0)), pl.BlockSpec((K, tn), lambda j, i: (0, j))]
    args = [a, b]
    if res is not None:
        in_specs.append(tile)
        args.append(res)
    for g in norm_gains:
        in_specs.append(pl.BlockSpec((1, N), lambda j, i: (0, 0)))
        args.append(g.reshape(1, N))
    out_shape = jax.ShapeDtypeStruct((T, N), out_dtype)
    if n_g:
        out_shape = (out_shape,) + (jax.ShapeDtypeStruct((T, N), BF16),) * n_g
    return _pallas(
        body, comm=comm, name=name, out_shape=out_shape, grid=(N // tn, T // tm),
        in_specs=in_specs, out_specs=(tile,) * (1 + n_g) if n_g else tile,
        compiler_params=_params("parallel", "parallel"),
    )(*args)


def _mm_nt(dy, b, out_dtype, name, comm=None, norm=None):
    dys = list(dy) if isinstance(dy, (list, tuple)) else [dy]
    T, n_each = dys[0].shape
    K = b.shape[0]
    tm = _tile(T, 1024 if norm is None else 512, SUBLANES_BF16)
    tk = _tile(K, 1536, LANES)
    tn = _tile(n_each, 2816, LANES)
    per = n_each // tn
    n_steps = per * len(dys)
    assert norm is None or tk == K, "the fused rmsnorm backward needs whole rows in one tile"

    def body(*refs):
        dy_refs, b_ref, acc_ref = refs[:len(dys)], refs[len(dys)], refs[-1]
        i, n = pl.program_id(0), pl.program_id(2)

        @pl.when(n == 0)
        def _():
            acc_ref[...] = jnp.zeros_like(acc_ref)

        for p, dy_ref in enumerate(dy_refs):
            @pl.when(jnp.logical_and(n >= p * per, n < (p + 1) * per))
            def _(dy_ref=dy_ref):
                acc_ref[...] += lax.dot_general(dy_ref[...].astype(BF16), b_ref[...], (((1,), (1,)), ((), ())),
                                                preferred_element_type=F32)

        if norm is None:
            @pl.when(n == n_steps - 1)
            def _():
                refs[-2][...] = acc_ref[...].astype(refs[-2].dtype)
        else:
            x_ref, g_ref, dres_ref, dx_ref, dg_ref = refs[len(dys) + 1:-1]

            @pl.when(jnp.logical_and(i == 0, n == 0))
            def _():
                dg_ref[...] = jnp.zeros_like(dg_ref)

            @pl.when(n == n_steps - 1)
            def _():
                xf = x_ref[...]
                r = lax.rsqrt(jnp.mean(xf * xf, axis=-1, keepdims=True) + RMS_EPS)
                xhat = xf * r
                d = acc_ref[...]
                dg_ref[0:1, :] += jnp.sum(d * xhat, axis=0, keepdims=True)
                t = d * g_ref[...]
                dx_ref[...] = dres_ref[...] + r * (t - xhat * jnp.mean(t * xhat, axis=-1, keepdims=True))

    in_specs = [pl.BlockSpec((tm, tn), lambda i, k, n, p=p: (i, jnp.clip(n - p * per, 0, per - 1))) for p in range(len(dys))]
    in_specs.append(pl.BlockSpec((tk, tn), lambda i, k, n: (k, n)))
    args = dys + [b]
    tile = pl.BlockSpec((tm, tk), lambda i, k, n: (i, k))
    if norm is None:
        out_shape, out_specs = jax.ShapeDtypeStruct((T, K), out_dtype), tile
    else:
        x, g, dres = norm
        in_specs += [tile, pl.BlockSpec((1, K), lambda i, k, n: (0, 0)), tile]
        args += [x, g.reshape(1, K), dres]
        out_shape = (jax.ShapeDtypeStruct((T, K), F32), jax.ShapeDtypeStruct((SUBLANES_F32, K), F32))
        out_specs = (tile, pl.BlockSpec((SUBLANES_F32, K), lambda i, k, n: (0, 0)))
    return _pallas(
        body, comm=comm, name=name, out_shape=out_shape, grid=(T // tm, K // tk, n_steps),
        in_specs=in_specs, out_specs=out_specs, scratch_shapes=[pltpu.VMEM((tm, tk), F32)],
        compiler_params=_params("parallel", "parallel", "arbitrary") if norm is None else _params(*["arbitrary"] * 3),
    )(*args)


def _mm_tn(a, dy, name):
    T, K = a.shape
    N = dy.shape[1]
    tt = _tile(T, 2048, SUBLANES_BF16)
    tk = _tile(K, 1536, LANES)
    tn = _tile(N, 1536, LANES)
    t_steps = T // tt

    def body(a_ref, dy_ref, o_ref, acc_ref):
        t = pl.program_id(2)

        @pl.when(t == 0)
        def _():
            acc_ref[...] = jnp.zeros_like(acc_ref)

        acc_ref[...] += lax.dot_general(a_ref[...].astype(BF16), dy_ref[...].astype(BF16),
                                        (((0,), (0,)), ((), ())), preferred_element_type=F32)

        @pl.when(t == t_steps - 1)
        def _():
            o_ref[...] = acc_ref[...].astype(o_ref.dtype)

    return pl.pallas_call(
        body, name=name, out_shape=jax.ShapeDtypeStruct((K, N), BF16), grid=(K // tk, N // tn, t_steps),
        in_specs=[pl.BlockSpec((tt, tk), lambda k, n, t: (t, k)), pl.BlockSpec((tt, tn), lambda k, n, t: (t, n))],
        out_specs=pl.BlockSpec((tk, tn), lambda k, n, t: (k, n)),
        scratch_shapes=[pltpu.VMEM((tk, tn), F32)],
        compiler_params=_params("parallel", "parallel", "arbitrary"),
    )(a, dy)


def _rows_before(halo, cur, k):
    h = halo.shape[0]
    return pltpu.roll(jnp.concatenate([halo, cur], axis=0), k, 0)[h:]


def _rows_after(cur, halo, k):
    n = cur.shape[0]
    total = n + halo.shape[0]
    return pltpu.roll(jnp.concatenate([cur, halo], axis=0), total - k, 0)[:n]


def _halo_specs(tm, width, n_rows):
    per = tm // SUBLANES_BF16
    last = n_rows // SUBLANES_BF16 - 1
    prev = pl.BlockSpec((SUBLANES_BF16, width), lambda i: (jnp.maximum(i * per - 1, 0), 0))
    nxt = pl.BlockSpec((SUBLANES_BF16, width), lambda i: (jnp.minimum((i + 1) * per, last), 0))
    return prev, nxt


def _gate_a_fwd(p, cw, seq, name, comm=None):
    T, D3 = p.shape
    D = D3 // 3
    tm = _tile(seq, 512, SUBLANES_BF16)
    cc = _tile(D, 256, LANES)
    prev, _ = _halo_specs(tm, D3, T)

    def body(p_ref, ph_ref, cw_ref, z_ref):
        at_start = (pl.program_id(0) * tm) % seq == 0
        for c0 in range(0, D, cc):
            b = p_ref[:, c0:c0 + cc].astype(F32)
            u = p_ref[:, D + c0:D + c0 + cc].astype(F32) * p_ref[:, 2 * D + c0:2 * D + c0 + cc].astype(F32)
            uh = ph_ref[:, D + c0:D + c0 + cc].astype(F32) * ph_ref[:, 2 * D + c0:2 * D + c0 + cc].astype(F32)
            uh = jnp.where(at_start, 0.0, uh)
            w = cw_ref[:, c0:c0 + cc]
            cv = _rows_before(uh, u, 2) * w[0:1] + _rows_before(uh, u, 1) * w[1:2] + u * w[2:3]
            z_ref[:, c0:c0 + cc] = (b * cv).astype(z_ref.dtype)

    return _pallas(
        body, comm=comm, name=name, out_shape=jax.ShapeDtypeStruct((T, D), BF16), grid=(T // tm,),
        in_specs=[pl.BlockSpec((tm, D3), lambda i: (i, 0)), prev, pl.BlockSpec((3, D), lambda i: (0, 0))],
        out_specs=pl.BlockSpec((tm, D), lambda i: (i, 0)),
        compiler_params=_params("parallel"),
    )(p, p, cw)


def _gate_a_bwd(p, cw, dz, seq, name, comm=None):
    T, D3 = p.shape
    D = D3 // 3
    tm = _tile(seq, 512, SUBLANES_BF16)
    cc = _tile(D, 256, LANES)
    p_prev, p_next = _halo_specs(tm, D3, T)
    _, dz_next = _halo_specs(tm, D, T)

    def body(p_ref, pp_ref, pn_ref, dz_ref, dzn_ref, cw_ref, dp_ref, dcw_ref):
        i = pl.program_id(0)

        @pl.when(i == 0)
        def _():
            dcw_ref[...] = jnp.zeros_like(dcw_ref)

        at_start = (i * tm) % seq == 0
        at_end = ((i + 1) * tm) % seq == 0
        for c0 in range(0, D, cc):
            cb, cc_, ch = slice(c0, c0 + cc), slice(D + c0, D + c0 + cc), slice(2 * D + c0, 2 * D + c0 + cc)
            b = p_ref[:, cb].astype(F32)
            c = p_ref[:, cc_].astype(F32)
            hh = p_ref[:, ch].astype(F32)
            u = c * hh
            uh = jnp.where(at_start, 0.0, pp_ref[:, cc_].astype(F32) * pp_ref[:, ch].astype(F32))
            w = cw_ref[:, cb]
            u1 = _rows_before(uh, u, 1)
            u2 = _rows_before(uh, u, 2)
            cv = u2 * w[0:1] + u1 * w[1:2] + u * w[2:3]
            dz_t = dz_ref[:, cb].astype(F32)
            dcv = dz_t * b
            dcvn = jnp.where(at_end, 0.0, dzn_ref[:, cb].astype(F32) * pn_ref[:, cb].astype(F32))
            du = dcv * w[2:3] + _rows_after(dcv, dcvn, 1) * w[1:2] + _rows_after(dcv, dcvn, 2) * w[0:1]
            dp_ref[:, cb] = (dz_t * cv).astype(dp_ref.dtype)
            dp_ref[:, cc_] = (du * hh).astype(dp_ref.dtype)
            dp_ref[:, ch] = (du * c).astype(dp_ref.dtype)
            dcw_ref[0:1, cb] += jnp.sum(dcv * u2, axis=0, keepdims=True)
            dcw_ref[1:2, cb] += jnp.sum(dcv * u1, axis=0, keepdims=True)
            dcw_ref[2:3, cb] += jnp.sum(dcv * u, axis=0, keepdims=True)

    res = _pallas(
        body, comm=comm, name=name,
        out_shape=(jax.ShapeDtypeStruct((T, D3), BF16), jax.ShapeDtypeStruct((SUBLANES_F32, D), F32)),
        grid=(T // tm,),
        in_specs=[pl.BlockSpec((tm, D3), lambda i: (i, 0)), p_prev, p_next,
                  pl.BlockSpec((tm, D), lambda i: (i, 0)), dz_next, pl.BlockSpec((3, D), lambda i: (0, 0))],
        out_specs=(pl.BlockSpec((tm, D3), lambda i: (i, 0)), pl.BlockSpec((SUBLANES_F32, D), lambda i: (0, 0))),
        compiler_params=_params("arbitrary"),
    )(p, p, p, dz, dz, cw)
    if comm is None:
        return res[0], res[1][0:3]
    return res[0][0], res[0][1][0:3], res[1]


def _ffn_gate_fwd(u, cw, cb, seq, name, comm=None):
    T, F2 = u.shape
    F = F2 // 2
    tm = _tile(seq, 256, SUBLANES_BF16)
    cc = _tile(F, 256, LANES)
    prev, _ = _halo_specs(tm, F2, T)

    def body(u_ref, uh_ref, cw_ref, cb_ref, a_ref, v_ref):
        at_start = (pl.program_id(0) * tm) % seq == 0

        def conv(c0):
            cols = slice(c0, c0 + cc)
            cur = u_ref[:, cols].astype(F32)
            halo = jnp.where(at_start, 0.0, uh_ref[:, cols].astype(F32))
            w = cw_ref[:, cols]
            return (_rows_before(halo, cur, 2) * w[0:1] + _rows_before(halo, cur, 1) * w[1:2] + cur * w[2:3]
                    + cb_ref[:, cols])

        for c0 in range(0, F, cc):
            g = conv(c0)
            up = conv(F + c0)
            a_ref[:, c0:c0 + cc] = ((g * jax.nn.sigmoid(g)) * up).astype(a_ref.dtype)
            v_ref[:, c0:c0 + cc] = g.astype(v_ref.dtype)
            v_ref[:, F + c0:F + c0 + cc] = up.astype(v_ref.dtype)

    return _pallas(
        body, comm=comm, name=name,
        out_shape=(jax.ShapeDtypeStruct((T, F), BF16), jax.ShapeDtypeStruct((T, F2), BF16)), grid=(T // tm,),
        in_specs=[pl.BlockSpec((tm, F2), lambda i: (i, 0)), prev,
                  pl.BlockSpec((3, F2), lambda i: (0, 0)), pl.BlockSpec((1, F2), lambda i: (0, 0))],
        out_specs=(pl.BlockSpec((tm, F), lambda i: (i, 0)), pl.BlockSpec((tm, F2), lambda i: (i, 0))),
        compiler_params=_params("parallel"),
    )(u, u, cw, cb.reshape(1, F2))


def _ffn_gate_bwd(u, v, cw, da, seq, name, comm=None):
    T, F2 = u.shape
    F = F2 // 2
    H = SUBLANES_BF16
    tm = _tile(seq, 256, H)
    cc = _tile(F, 256, LANES)
    _, v_next = _halo_specs(tm, F2, T)
    _, da_next = _halo_specs(tm, F, T)

    def body(u_ref, v_ref, vn_ref, da_ref, dan_ref, cw_ref, du_ref, acc_ref):
        i = pl.program_id(0)

        @pl.when(i == 0)
        def _():
            acc_ref[...] = jnp.zeros_like(acc_ref)

        at_end = ((i + 1) * tm) % seq == 0
        n = tm + H

        def rows_and_next(ref, nxt, cols):
            return jnp.concatenate([ref[:, cols].astype(F32), nxt[:, cols].astype(F32)], axis=0)

        def back(d, cols):
            w = cw_ref[:, cols]
            d0 = d[:tm]
            d1 = pltpu.roll(d, n - 1, 0)[:tm]
            d2 = pltpu.roll(d, n - 2, 0)[:tm]
            du_ref[:, cols] = (d0 * w[2:3] + d1 * w[1:2] + d2 * w[0:1]).astype(du_ref.dtype)
            ut = u_ref[:, cols].astype(F32)
            acc_ref[0:1, cols] += jnp.sum(d2 * ut, axis=0, keepdims=True)
            acc_ref[1:2, cols] += jnp.sum(d1 * ut, axis=0, keepdims=True)
            acc_ref[2:3, cols] += jnp.sum(d0 * ut, axis=0, keepdims=True)
            acc_ref[3:4, cols] += jnp.sum(d0, axis=0, keepdims=True)

        for c0 in range(0, F, cc):
            gc, uc = slice(c0, c0 + cc), slice(F + c0, F + c0 + cc)
            g = rows_and_next(v_ref, vn_ref, gc)
            up = rows_and_next(v_ref, vn_ref, uc)
            da_ext = jnp.concatenate([da_ref[:, gc].astype(F32),
                                      jnp.where(at_end, 0.0, dan_ref[:, gc].astype(F32))], axis=0)
            sg = jax.nn.sigmoid(g)
            back(da_ext * up * (sg * (1.0 + g * (1.0 - sg))), gc)
            back(da_ext * (g * sg), uc)

    res = _pallas(
        body, comm=comm, name=name,
        out_shape=(jax.ShapeDtypeStruct((T, F2), BF16), jax.ShapeDtypeStruct((SUBLANES_F32, F2), F32)),
        grid=(T // tm,),
        in_specs=[pl.BlockSpec((tm, F2), lambda i: (i, 0)), pl.BlockSpec((tm, F2), lambda i: (i, 0)), v_next,
                  pl.BlockSpec((tm, F), lambda i: (i, 0)), da_next, pl.BlockSpec((3, F2), lambda i: (0, 0))],
        out_specs=(pl.BlockSpec((tm, F2), lambda i: (i, 0)), pl.BlockSpec((SUBLANES_F32, F2), lambda i: (0, 0))),
        compiler_params=_params("arbitrary"),
    )(u, v, v, da, da, cw)
    (du, acc), landed = res if comm is not None else (res, None)
    return (du, acc[0:3], acc[3]) if comm is None else (du, acc[0:3], acc[3], landed)


def _bucket_map():
    P = ATT_BLOCK
    qi = np.arange(P, dtype=np.int64)[:, None]
    kc = np.arange(2 * P, dtype=np.int64)[None, :]
    delta = qi + P - kc
    maps = []
    max_exact = REL_BUCKETS // 2
    for window, dilation in DILATED_BRANCHES:
        band = (delta >= 0) & (delta <= window // dilation)
        n = np.maximum(delta * dilation, 0)
        nf = np.maximum(n, max_exact).astype(np.float32)
        large = max_exact + (np.log(nf / np.float32(max_exact)) / np.float32(math.log(REL_MAX_DISTANCE / max_exact))
                             * np.float32(REL_BUCKETS - max_exact)).astype(np.int32)
        large = np.minimum(large, REL_BUCKETS - 1)
        bucket = np.where(n < max_exact, n, large)
        maps.append(np.where(band, bucket, -1).astype(np.int32))
    return np.stack(maps)


def _bias_tables(rel_bias, bmap, name):
    n_pairs = rel_bias.shape[1] // 2
    nbr, P, P2 = bmap.shape

    def body(rb_ref, bm_ref, o_ref):
        pair = pl.program_id(0)
        in_seq = lax.broadcasted_iota(jnp.int32, (P, P2), 1) >= P
        for br in range(nbr):
            bm = bm_ref[br]
            for hh in range(2):
                acc = jnp.full((P, P2), MASKED_LOGIT, F32)
                for b in range(REL_BUCKETS):
                    acc = jnp.where(bm == b, rb_ref[b, 2 * pair + hh], acc)
                o_ref[br, 0, 0, hh * P:(hh + 1) * P, :] = acc
                o_ref[br, 0, 1, hh * P:(hh + 1) * P, :] = jnp.where(in_seq, acc, MASKED_LOGIT)

    return pl.pallas_call(
        body, name=name, out_shape=jax.ShapeDtypeStruct((nbr, n_pairs, 2, 2 * P, P2), F32), grid=(n_pairs,),
        in_specs=[pl.BlockSpec(memory_space=pltpu.SMEM), pl.BlockSpec((nbr, P, P2), lambda h: (0, 0, 0))],
        out_specs=pl.BlockSpec((nbr, 1, 2, 2 * P, P2), lambda h: (0, h, 0, 0, 0)),
        compiler_params=_params("parallel"),
    )(rel_bias, bmap)


def _bias_grad(dbias, bmap, name):
    nbr, n_pairs, _, P2 = dbias.shape
    P = P2 // 2

    def body(db_ref, bm_ref, o_ref):
        lane = lax.broadcasted_iota(jnp.int32, (1, LANES), 1)
        for hh in range(2):
            row = jnp.zeros((1, LANES), F32)
            for br in range(nbr):
                bm = bm_ref[br]
                d = db_ref[br, 0, hh * P:(hh + 1) * P, :]
                for b in range(REL_BUCKETS):
                    hit = jnp.sum(jnp.where(bm == b, d, 0.0), axis=1, keepdims=True)
                    row = row + jnp.where(lane == b, jnp.sum(hit, axis=0, keepdims=True), 0.0)
            o_ref[hh] = row

    return pl.pallas_call(
        body, name=name, out_shape=jax.ShapeDtypeStruct((2 * n_pairs, 1, LANES), F32), grid=(n_pairs,),
        in_specs=[pl.BlockSpec((nbr, 1, P2, P2), lambda h: (0, h, 0, 0)), pl.BlockSpec((nbr, P, P2), lambda h: (0, 0, 0))],
        out_specs=pl.BlockSpec((2, 1, LANES), lambda h: (h, 0, 0)),
        compiler_params=_params("parallel"),
    )(dbias, bmap)[:, 0, :]


def _rows(start, dilation):
    if dilation == 1:
        return pl.ds(pl.multiple_of(start, ATT_BLOCK), ATT_BLOCK)
    return pl.ds(start, ATT_BLOCK, stride=dilation)


def _for_each_block(seq, unroll, fn):
    P = ATT_BLOCK
    for br, (_, d) in enumerate(DILATED_BRANCHES):
        nb = seq // d // P
        u = unroll[br] if (unroll[br] % nb == 0 or nb % unroll[br] == 0) else 1
        step = d * P

        def some(i, carry, br=br, d=d, nb=nb, u=u, step=step):
            blocks = []
            if u % nb == 0:
                for k in range(u):
                    if k % nb == 0:
                        start = i * (u // nb) + k // nb
                        blocks.append((start, start, 1))
                    else:
                        blocks.append((blocks[-1][0] + step, blocks[-1][0], 0))
            else:
                r, j0 = (i * u) // nb, (i * u) % nb
                blocks.append((r + j0 * step, r + jnp.maximum(j0 - 1, 0) * step, jnp.where(j0 == 0, 1, 0)))
                for _ in range(1, u):
                    blocks.append((blocks[-1][0] + step, blocks[-1][0], 0))
            fn(br, d, blocks)
            return carry

        lax.fori_loop(0, nb * d // u, some, 0)


class _RowCache:
    def __init__(self, dilation):
        self.dilation, self.seen = dilation, {}

    def rows(self, ref, start):
        key = (id(ref), id(start))
        if key not in self.seen:
            self.seen[key] = ref[_rows(start, self.dilation), :].astype(BF16)
        return self.seen[key]

    def window(self, ref, start, prev):
        return jnp.concatenate([self.rows(ref, prev), self.rows(ref, start)], axis=0)


def _stack_heads(x, head0):
    return jnp.concatenate([jnp.where(head0, x, 0.0), jnp.where(head0, 0.0, x)], axis=0).astype(BF16)


def _attn_fwd(q, kv, bias, name):
    B, S, D = q.shape
    P = ATT_BLOCK
    n_pairs = D // LANES
    nbr = len(DILATED_BRANCHES)
    scale = HEAD_DIM ** -0.5

    def body(q_ref, k_ref, v_ref, bias_ref, o_ref, lse_ref, *stats):
        m_s, l_s, acc_s = stats[0:nbr], stats[nbr:2 * nbr], stats[2 * nbr:3 * nbr]
        head0 = lax.broadcasted_iota(jnp.int32, (P, LANES), 1) < HEAD_DIM

        def block(br, d, blocks):
            cache = _RowCache(d)
            s = [lax.dot_general(_stack_heads(q_ref[_rows(start, d), :] * scale, head0),
                                 cache.window(k_ref, start, prev), (((1,), (1,)), ((), ())),
                                 preferred_element_type=F32) + bias_ref[br, 0, first]
                 for start, prev, first in blocks]
            m = [jnp.max(x, axis=-1, keepdims=True) for x in s]
            p = [jnp.exp(x - y) for x, y in zip(s, m)]
            l = [jnp.sum(x, axis=-1, keepdims=True) for x in p]
            pv = [jnp.dot(x.astype(BF16), cache.window(v_ref, start, prev), preferred_element_type=F32)
                  for x, (start, prev, _) in zip(p, blocks)]
            for k, (start, _, _) in enumerate(blocks):
                rows = _rows(start, d)
                m_s[br][rows, :] = jnp.where(head0, m[k][:P], m[k][P:])
                l_s[br][rows, :] = jnp.where(head0, l[k][:P], l[k][P:])
                acc_s[br][rows, :] = jnp.where(head0, pv[k][:P], pv[k][P:])

        _for_each_block(S, ATTN_FWD_UNROLL, block)

        chunk = _tile(S, 256, SUBLANES_F32)

        def merge(i, carry):
            rows = pl.ds(pl.multiple_of(i * chunk, chunk), chunk)
            ms = [m_s[br][rows, :] for br in range(nbr)]
            m = functools.reduce(jnp.maximum, ms)
            l = jnp.zeros((chunk, LANES), F32)
            acc = jnp.zeros((chunk, LANES), F32)
            for br in range(nbr):
                w = jnp.exp(ms[br] - m)
                l = l + w * l_s[br][rows, :]
                acc = acc + w * acc_s[br][rows, :]
            o_ref[rows, :] = acc / l
            lse_ref[rows, :] = m + jnp.log(l)
            return carry

        lax.fori_loop(0, S // chunk, merge, 0)

    slab = lambda col0: pl.BlockSpec((None, S, LANES), lambda b, h: (b, 0, col0 + h))
    return pl.pallas_call(
        body, name=name,
        out_shape=(jax.ShapeDtypeStruct((B, S, D), F32), jax.ShapeDtypeStruct((B, S, D), F32)),
        grid=(B, n_pairs),
        in_specs=[slab(0), slab(0), slab(n_pairs),
                  pl.BlockSpec((nbr, 1, 2, 2 * P, 2 * P), lambda b, h: (0, h, 0, 0, 0))],
        out_specs=(slab(0), slab(0)),
        scratch_shapes=[pltpu.VMEM((S, LANES), F32)] * (3 * nbr),
        compiler_params=_params("parallel", "parallel"),
    )(q, kv, kv, bias)


def _attn_bwd(q, kv, o, lse, do, bias, name, comm=None):
    B, S, D = q.shape
    P = ATT_BLOCK
    n_pairs = D // LANES
    nbr = len(DILATED_BRANCHES)
    scale = HEAD_DIM ** -0.5

    def body(q_ref, k_ref, v_ref, o_ref, lse_ref, do_ref, bias_ref, dq_ref, dk_ref, dv_ref, dbias_ref, delta_s):
        head0 = lax.broadcasted_iota(jnp.int32, (P, LANES), 1) < HEAD_DIM

        @pl.when(pl.program_id(1) == 0)
        def _():
            dbias_ref[...] = jnp.zeros_like(dbias_ref)

        chunk = _tile(S, 512, SUBLANES_F32)

        def prepare(i, carry):
            rows = pl.ds(pl.multiple_of(i * chunk, chunk), chunk)
            x = do_ref[rows, :] * o_ref[rows, :]
            h0 = lax.broadcasted_iota(jnp.int32, (chunk, LANES), 1) < HEAD_DIM
            d0 = jnp.sum(jnp.where(h0, x, 0.0), axis=-1, keepdims=True)
            d1 = jnp.sum(jnp.where(h0, 0.0, x), axis=-1, keepdims=True)
            delta_s[rows, :] = jnp.where(h0, d0, d1)
            zero = jnp.zeros((chunk, LANES), F32)
            dq_ref[rows, :] = zero
            dk_ref[rows, :] = zero
            dv_ref[rows, :] = zero
            return carry

        lax.fori_loop(0, S // chunk, prepare, 0)

        def per_head(x):
            return jnp.concatenate([x[:, 0:1], x[:, HEAD_DIM:HEAD_DIM + 1]], axis=0)

        nt = (((1,), (1,)), ((), ()))
        tn = (((0,), (0,)), ((), ()))

        def block(br, d, blocks):
            cache = _RowCache(d)
            q2 = [_stack_heads(q_ref[_rows(start, d), :] * scale, head0) for start, _, _ in blocks]
            do2 = [_stack_heads(do_ref[_rows(start, d), :], head0) for start, _, _ in blocks]
            kb = [cache.window(k_ref, start, prev) for start, prev, _ in blocks]
            vb = [cache.window(v_ref, start, prev) for start, prev, _ in blocks]
            s = [lax.dot_general(a, b, nt, preferred_element_type=F32) + bias_ref[br, 0, first]
                 for a, b, (_, _, first) in zip(q2, kb, blocks)]
            dp = [lax.dot_general(a, b, nt, preferred_element_type=F32) for a, b in zip(do2, vb)]
            p = [jnp.exp(x - per_head(lse_ref[_rows(start, d), :])) for x, (start, _, _) in zip(s, blocks)]
            ds = [x * (y - per_head(delta_s[_rows(start, d), :])) for x, y, (start, _, _) in zip(p, dp, blocks)]
            for x in ds:
                dbias_ref[br, 0] += x
            ds16 = [x.astype(BF16) for x in ds]
            dq2 = [jnp.dot(a, b, preferred_element_type=F32) for a, b in zip(ds16, kb)]
            dk = [lax.dot_general(a, b, tn, preferred_element_type=F32) for a, b in zip(ds16, q2)]
            dv = [lax.dot_general(a.astype(BF16), b, tn, preferred_element_type=F32) for a, b in zip(p, do2)]
            parts = {}
            for k, (start, prev, first) in enumerate(blocks):
                dq_ref[_rows(start, d), :] += jnp.where(head0, dq2[k][:P], dq2[k][P:]) * scale
                parts.setdefault(id(start), [start, []])[1].append((dk[k][P:], dv[k][P:]))
                if not (isinstance(first, int) and first == 1):
                    parts.setdefault(id(prev), [prev, []])[1].append((dk[k][:P], dv[k][:P]))
            for start, terms in parts.values():
                rows = _rows(start, d)
                dk_ref[rows, :] += functools.reduce(jnp.add, [t[0] for t in terms])
                dv_ref[rows, :] += functools.reduce(jnp.add, [t[1] for t in terms])

        _for_each_block(S, ATTN_BWD_UNROLL, block)

    slab = lambda col0: pl.BlockSpec((None, S, LANES), lambda h, b: (b, 0, col0 + h))
    tab = pl.BlockSpec((nbr, 1, 2, 2 * P, 2 * P), lambda h, b: (0, h, 0, 0, 0))
    dtab = pl.BlockSpec((nbr, 1, 2 * P, 2 * P), lambda h, b: (0, h, 0, 0))
    shp = jax.ShapeDtypeStruct((B, S, D), F32)
    return _pallas(
        body, comm=comm, name=name,
        out_shape=(shp, shp, shp, jax.ShapeDtypeStruct((nbr, n_pairs, 2 * P, 2 * P), F32)),
        grid=(n_pairs, B),
        in_specs=[slab(0), slab(0), slab(n_pairs), slab(0), slab(0), slab(0), tab],
        out_specs=(slab(0), slab(0), slab(0), dtab),
        scratch_shapes=[pltpu.VMEM((S, LANES), F32)],
        compiler_params=_params("parallel", "arbitrary"),
    )(q, kv, kv, o, lse, do, bias)


def _adamw(w, g, m, v, name):
    R, C = w.shape
    tr = _tile(R, 256, SUBLANES_F32) if R % SUBLANES_F32 == 0 else R
    tc = _tile(C, 2048, LANES) if C % LANES == 0 else C

    def body(w_ref, g_ref, m_ref, v_ref, d_ref, nm_ref, nv_ref):
        g_ = g_ref[...]
        m2 = ADAM_B1 * m_ref[...] + (1.0 - ADAM_B1) * g_
        v2 = ADAM_B2 * v_ref[...] + (1.0 - ADAM_B2) * (g_ * g_)
        m_hat = m2 / (1.0 - ADAM_B1 ** ADAM_STEP)
        v_hat = v2 / (1.0 - ADAM_B2 ** ADAM_STEP)
        d_ref[...] = -ADAM_LR * (m_hat / (jnp.sqrt(v_hat) + ADAM_EPS) + ADAM_WD * w_ref[...])
        nm_ref[...] = m2
        nv_ref[...] = v2

    blk = pl.BlockSpec((tr, tc), lambda i, j: (i, j))
    shp = jax.ShapeDtypeStruct((R, C), F32)
    return pl.pallas_call(
        body, name=name, out_shape=(shp, shp, shp), grid=(R // tr, C // tc),
        in_specs=[blk] * 4, out_specs=(blk,) * 3, compiler_params=_params("parallel", "parallel"),
    )(w, g, m, v)


def _sum_slots(slots, name):
    n, R, C = slots.shape
    tr = _tile(R, 256, SUBLANES_BF16) if R % SUBLANES_BF16 == 0 else R
    tc = _tile(C, 2048, LANES) if C % LANES == 0 else C

    def body(s_ref, o_ref):
        acc = s_ref[0].astype(F32)
        for k in range(1, n):
            acc = acc + s_ref[k].astype(F32)
        o_ref[...] = acc

    return pl.pallas_call(
        body, name=name, out_shape=jax.ShapeDtypeStruct((R, C), F32), grid=(R // tr, C // tc),
        in_specs=[pl.BlockSpec((n, tr, tc), lambda i, j: (0, i, j))],
        out_specs=pl.BlockSpec((tr, tc), lambda i, j: (i, j)),
        compiler_params=_params("parallel", "parallel"),
    )(slots)


def _my_place():
    return lax.axis_index("x"), lax.axis_index("y"), lax.axis_index("c")


def _other_chips(x, y):
    return [(1 - x, y), (x, 1 - y), (1 - x, 1 - y)]


def _piece(ref, blk, axis, shard, half):
    h0 = blk[0] // 2
    idx = []
    for dim, n in enumerate(blk):
        if dim == 0:
            start = half * h0 + (shard * n if axis == 0 else 0)
            idx.append(pl.ds(start, h0))
        elif dim == axis:
            idx.append(pl.ds(shard * n, n))
        else:
            idx.append(slice(None))
    return ref.at[tuple(idx)]


def _half_of(ref, blk, half):
    return ref.at[pl.ds(half * (blk[0] // 2), blk[0] // 2)]


def _gather_weights(shards, axes, names, small):
    n = len(shards)
    blks = [s.shape for s in shards]
    task = _gather_ici_task(shards, axes, names)

    def stage1(*refs):
        small_in, outs, small_out = refs[n], refs[n + 1:2 * n + 1], refs[2 * n + 1]
        send_sems, recv_sems, local_sems = refs[2 * n + 2:]
        sem = lambda k: (send_sems.at[k], recv_sems.at[k])
        x, y, c = _my_place()
        me = 2 * x + y
        local = pltpu.make_async_copy(small_in, small_out.at[me], local_sems.at[0])
        local.start()
        sends, recvs = task.copies(None, outs, sem)
        for k, (px, py) in enumerate(_other_chips(x, y)):
            sends.append(_remote(small_in, small_out.at[me], sem(task.n_sems + k), (px, py, c)))
            recvs.append(_remote(small_in, small_out.at[2 * px + py], sem(task.n_sems + k), (px, py, c)))
        for cp in sends:
            cp.start()
        for cp in recvs:
            cp.wait_recv()
        for cp in sends:
            cp.wait_send()
        local.wait()

    res = pl.pallas_call(
        stage1, name="gather_weights_ici",
        out_shape=task.outs + [jax.ShapeDtypeStruct((N_SHARDS,) + small.shape, small.dtype)],
        in_specs=[ANY] * (n + 1), out_specs=[ANY] * (n + 1), input_output_aliases={a: a for a in range(n)},
        scratch_shapes=[pltpu.SemaphoreType.DMA((task.n_sems + 3,)), pltpu.SemaphoreType.DMA((task.n_sems + 3,)),
                        pltpu.SemaphoreType.DMA((1,))],
    )(*task.ins, small)
    full = _run_comm(_gather_d2d_task(list(res[:n]), blks, axes), "gather_weights_d2d")
    return full, res[n]


def _run_comm(comm, name):
    n_in, n_out = len(comm.ins), len(comm.outs)

    def body(*refs):
        send_sems, recv_sems = refs[n_in + n_out:]
        sends, recvs = comm.copies(refs[:n_in], refs[n_in:n_in + n_out], lambda k: (send_sems.at[k], recv_sems.at[k]))
        for cp in sends:
            cp.start()
        for cp in recvs:
            cp.wait_recv()
        for cp in sends:
            cp.wait_send()

    return list(pl.pallas_call(
        body, name=name, out_shape=comm.outs, in_specs=[ANY] * n_in, out_specs=[ANY] * n_out,
        input_output_aliases=comm.aliases, scratch_shapes=[pltpu.SemaphoreType.DMA((comm.n_sems,))] * 2,
    )(*comm.ins))


def _scatter_grads(grads, blks, axes, small):
    n = len(grads)
    halves = [(b[0] // 2,) + tuple(b[1:]) for b in blks]

    def body(*refs):
        ins, small_in = refs[:n], refs[n]
        outs, small_out = refs[n + 1:2 * n + 1], refs[2 * n + 1]
        send_sems, recv_sems, local_sems = refs[2 * n + 2:]
        x, y, c = _my_place()
        me = 4 * x + 2 * y + c
        local = [pltpu.make_async_copy(small_in, small_out.at[me], local_sems.at[0])]
        for cp in local:
            cp.start()
        sends, recvs = [], []
        for rel in range(1, N_DEVICES):
            px = x ^ ((rel >> 2) & 1)
            py = y ^ ((rel >> 1) & 1)
            pc = c ^ (rel & 1)
            peer = 4 * px + 2 * py + pc
            for a in range(n + 1):
                if a < n:
                    src = _piece(ins[a], blks[a], axes[a], 2 * px + py, pc)
                    dst = land = outs[a].at[rel - 1]
                else:
                    src, dst, land = small_in, small_out.at[me], small_out.at[peer]
                sends.append(pltpu.make_async_remote_copy(
                    src_ref=src, dst_ref=dst, send_sem=send_sems.at[a, rel - 1], recv_sem=recv_sems.at[a, rel - 1],
                    device_id=(px, py, pc), device_id_type=pl.DeviceIdType.MESH))
                recvs.append(pltpu.make_async_remote_copy(
                    src_ref=src, dst_ref=land, send_sem=send_sems.at[a, rel - 1], recv_sem=recv_sems.at[a, rel - 1],
                    device_id=(px, py, pc), device_id_type=pl.DeviceIdType.MESH))
        for cp in sends:
            cp.start()
        for cp in recvs:
            cp.wait_recv()
        for cp in sends:
            cp.wait_send()
        for cp in local:
            cp.wait()

    out_shape = [jax.ShapeDtypeStruct((N_DEVICES - 1,) + h, g.dtype) for h, g in zip(halves, grads)]
    out_shape.append(jax.ShapeDtypeStruct((N_DEVICES,) + small.shape, small.dtype))
    res = pl.pallas_call(
        body, name="scatter_grads", out_shape=out_shape,
        in_specs=[ANY] * (n + 1), out_specs=[ANY] * (n + 1),
        scratch_shapes=[pltpu.SemaphoreType.DMA((n + 1, N_DEVICES - 1)), pltpu.SemaphoreType.DMA((n + 1, N_DEVICES - 1)),
                        pltpu.SemaphoreType.DMA((1,))],
    )(*grads, small)
    return list(res[:n]), res[n]


def _remote(src, dst, sems, device):
    return pltpu.make_async_remote_copy(src_ref=src, dst_ref=dst, send_sem=sems[0], recv_sem=sems[1],
                                        device_id=device, device_id_type=pl.DeviceIdType.MESH)


def _sum_piece(dest, slots, grad, blk, axis, layer, n_layers, name):
    r, c = blk
    h0 = r // 2
    tr = _tile(h0, 256, SUBLANES_BF16)
    tc = _tile(c, 2048, LANES)
    place = jnp.stack([2 * lax.axis_index("x") + lax.axis_index("y"), lax.axis_index("c")]).astype(jnp.int32)

    def body(p_ref, s_ref, g_ref, *rest):
        acc = g_ref[...].astype(F32) + s_ref[0].astype(F32)
        for k in range(1, N_DEVICES - 1):
            acc = acc + s_ref[k].astype(F32)
        rest[-1][...] = acc

    def g_map(i, j, p):
        return (p[1] * (h0 // tr) + (p[0] * (r // tr) if axis == 0 else 0) + i, (p[0] * (c // tc) if axis == 1 else 0) + j)

    in_specs = [pl.BlockSpec((N_DEVICES - 1, tr, tc), lambda i, j, p: (0, i, j)), pl.BlockSpec((tr, tc), g_map)]
    args = [place, slots, grad]
    if dest is not None:
        in_specs.append(ANY)
        args.append(dest)
    return pl.pallas_call(
        body, name=name, out_shape=jax.ShapeDtypeStruct((n_layers, r, c), F32),
        grid_spec=pltpu.PrefetchScalarGridSpec(
            num_scalar_prefetch=1, grid=(h0 // tr, c // tc), in_specs=in_specs,
            out_specs=pl.BlockSpec((None, tr, tc), lambda i, j, p: (layer, p[1] * (h0 // tr) + i, j))),
        input_output_aliases={3: 0} if dest is not None else {},
        compiler_params=_params("parallel", "parallel"),
    )(*args)


def _swap_halves(blocks):
    n = len(blocks)
    layers = [(a, l) for a, b in enumerate(blocks) for l in range(b.shape[0])]

    def copies(in_refs, out_refs, sem):
        x, y, c = _my_place()
        sends, recvs = [], []
        for k, (a, l) in enumerate(layers):
            blk = blocks[a].shape[1:]
            mine = _half_of(out_refs[a].at[l], blk, c)
            sends.append(_remote(mine, mine, sem(k), (x, y, 1 - c)))
            recvs.append(_remote(mine, _half_of(out_refs[a].at[l], blk, 1 - c), sem(k), (x, y, 1 - c)))
        return sends, recvs

    task = _Comm(blocks, [jax.ShapeDtypeStruct(b.shape, b.dtype) for b in blocks], {a: a for a in range(n)},
                 len(layers), copies)
    return _run_comm(task, "swap_grad_halves")


def _place_shard(shard, axis, name):
    r, c = shard.shape
    tr = _tile(r, 512, SUBLANES_BF16)
    full = (r * N_SHARDS, c) if axis == 0 else (r, c * N_SHARDS)
    me2 = (2 * lax.axis_index("x") + lax.axis_index("y")).astype(jnp.int32).reshape(1)

    def body(me_ref, s_ref, o_ref):
        o_ref[...] = s_ref[...].astype(o_ref.dtype)

    if axis == 0:
        out_map = lambda i, me: (me[0] * (r // tr) + i, 0)
    else:
        out_map = lambda i, me: (i, me[0])
    return pl.pallas_call(
        body, name=name, out_shape=jax.ShapeDtypeStruct(full, BF16),
        grid_spec=pltpu.PrefetchScalarGridSpec(
            num_scalar_prefetch=1, grid=(r // tr,),
            in_specs=[pl.BlockSpec((tr, c), lambda i, me: (i, 0))], out_specs=pl.BlockSpec((tr, c), out_map)),
        compiler_params=_params("parallel"),
    )(me2, shard)


def _gather_ici_task(shards, axes, names):
    n = len(shards)
    blks = [s.shape for s in shards]
    bases = [_place_shard(s, ax, f"place_{nm}") for s, ax, nm in zip(shards, axes, names)]

    def copies(in_refs, out_refs, sem):
        x, y, c = _my_place()
        me = 2 * x + y
        sends, recvs = [], []
        for a in range(n):
            mine = _piece(out_refs[a], blks[a], axes[a], me, c)
            for k, (px, py) in enumerate(_other_chips(x, y)):
                sends.append(_remote(mine, mine, sem(3 * a + k), (px, py, c)))
                recvs.append(_remote(mine, _piece(out_refs[a], blks[a], axes[a], 2 * px + py, c), sem(3 * a + k),
                                     (px, py, c)))
        return sends, recvs

    return _Comm(bases, [jax.ShapeDtypeStruct(b.shape, b.dtype) for b in bases], {a: a for a in range(n)}, 3 * n, copies)


def _gather_d2d_task(partials, blks, axes):
    n = len(partials)

    def copies(in_refs, out_refs, sem):
        x, y, c = _my_place()
        sends, recvs = [], []
        for a in range(n):
            for k, (px, py) in enumerate(_other_chips(x, y)):
                mine = _piece(out_refs[a], blks[a], axes[a], 2 * px + py, c)
                theirs = _piece(out_refs[a], blks[a], axes[a], 2 * px + py, 1 - c)
                sends.append(_remote(mine, mine, sem(3 * a + k), (x, y, 1 - c)))
                recvs.append(_remote(mine, theirs, sem(3 * a + k), (x, y, 1 - c)))
        return sends, recvs

    return _Comm(partials, [jax.ShapeDtypeStruct(p.shape, p.dtype) for p in partials], {a: a for a in range(n)},
                 3 * n, copies)


def _scatter_task(grads, blks, axes):
    n = len(grads)

    def copies(in_refs, out_refs, sem):
        x, y, c = _my_place()
        sends, recvs = [], []
        for rel in range(1, N_DEVICES):
            px, py, pc = x ^ ((rel >> 2) & 1), y ^ ((rel >> 1) & 1), c ^ (rel & 1)
            for a in range(n):
                src = _piece(in_refs[a], blks[a], axes[a], 2 * px + py, pc)
                k = (N_DEVICES - 1) * a + rel - 1
                sends.append(_remote(src, out_refs[a].at[rel - 1], sem(k), (px, py, pc)))
                recvs.append(_remote(src, out_refs[a].at[rel - 1], sem(k), (px, py, pc)))
        return sends, recvs

    outs = [jax.ShapeDtypeStruct((N_DEVICES - 1, b[0] // 2) + tuple(b[1:]), g.dtype) for b, g in zip(blks, grads)]
    return _Comm(grads, outs, {}, (N_DEVICES - 1) * n, copies)


def _pack(arrays):
    flat = jnp.concatenate([a.reshape(-1).astype(F32) for a in arrays])
    pad = (-flat.shape[0]) % (SUBLANES_F32 * LANES)
    return jnp.pad(flat, (0, pad)).reshape(-1, LANES)


def _unpack(packed, shapes):
    flat = packed.reshape(-1)
    out, off = [], 0
    for s in shapes:
        n = int(np.prod(s))
        out.append(flat[off:off + n].reshape(s))
        off += n
    return out


def _local_step(x, target, W, shards, geom, small):
    W = dict(W)
    B, S, D = x.shape
    T = B * S
    x2 = x.reshape(T, D)
    tgt = target.reshape(T, D)
    bmap = jnp.asarray(_bucket_map())
    blk = lambda names: [geom[k][0] for k in names]
    axs = lambda names: [geom[k][1] for k in names]
    ici = lambda names: _gather_ici_task([shards[k] for k in names], axs(names), names)
    d2d = lambda names, partials: _gather_d2d_task(list(partials), blk(names), axs(names))
    big, slots = {}, {}

    def scatter(names):
        return _scatter_task([big[k] for k in names], blk(names), axs(names))

    def ffn_bwd(l, dout, h, saved, first, second):
        xn, u, v, a = saved
        da = _mm_nt(dout, W[f"w_down{l}"], BF16, f"ffn{l}_down_dx")
        big[f"w_down{l}"] = _mm_tn(a, dout, f"ffn{l}_down_dw")
        names = first + [f"w_down{l}"]
        du, g_cw, g_cb, landed = _ffn_gate_bwd(u, v, small["ffn_conv"][l], da, S, f"ffn{l}_gate_bwd", comm=scatter(names))
        slots.update(zip(names, landed))
        big[f"w_up{l}"] = _mm_tn(xn, du, f"ffn{l}_up_dw")
        norm = (h, small["ffn_norm"][l], dout)
        if second:
            (dh, g_norm), (slots[f"w_up{l}"],) = _mm_nt(du, W[f"w_up{l}"], None, f"ffn{l}_up_dx",
                                                        comm=scatter([f"w_up{l}"]), norm=norm)
        else:
            dh, g_norm = _mm_nt(du, W[f"w_up{l}"], None, f"ffn{l}_up_dx", norm=norm)
        return dh, g_cw, g_cb, g_norm[0]

    xn0 = _rmsnorm_fwd(x2, small["a_norm"][0], "a_norm")
    p, part = _mm_nn(xn0, W["w_in"], None, BF16, "a_in", comm=ici(["w_up0"]))
    z, (W["w_up0"],) = _gate_a_fwd(p, small["a_conv"][0], S, "a_gate", comm=d2d(["w_up0"], part))
    (h1, xn1), part = _mm_nn(z, W["w_out"], x2, F32, "a_out", comm=ici(["w_down0"]), norm_gains=[small["ffn_norm"][0]])
    later = ["w_kv", "w_q", "w_o", "w_up1"]
    u0, landed = _mm_nn(xn1, W["w_up0"], None, BF16, "ffn0_up", comm=_Comm.join([d2d(["w_down0"], part), ici(later)]))
    W["w_down0"] = landed[0]
    (a0, v0), landed = _ffn_gate_fwd(u0, small["ffn_conv"][0], small["ffn_conv_b"][0], S, "ffn0_gate",
                               comm=_Comm.join([d2d(later, landed[1:]), ici(["w_down1"])]))
    W.update(zip(later, landed[:len(later)]))
    (h2, kvn, xn3), (W["w_down1"],) = _mm_nn(a0, W["w_down0"], h1, F32, "ffn0_down",
                                            comm=d2d(["w_down1"], landed[len(later):]),
                                            norm_gains=[small["kv_norm"], small["b_norm"][0]])
    kv = _mm_nn(kvn, W["w_kv"], None, F32, "kv_proj")
    q = _mm_nn(xn3, W["w_q"], None, F32, "q_proj")
    bias = _bias_tables(small["rel_bias"], bmap, "rel_bias_tables")
    q3, kv3 = q.reshape(B, S, D), kv.reshape(B, S, 2 * D)
    o3, lse3 = _attn_fwd(q3, kv3, bias, "attn_fwd")
    o = o3.reshape(T, D)
    h3, xn4 = _mm_nn(o, W["w_o"], h2, F32, "o_proj", norm_gains=[small["ffn_norm"][1]])
    u1 = _mm_nn(xn4, W["w_up1"], None, BF16, "ffn1_up")
    a1, v1 = _ffn_gate_fwd(u1, small["ffn_conv"][1], small["ffn_conv_b"][1], S, "ffn1_gate")
    h4 = _mm_nn(a1, W["w_down1"], h3, F32, "ffn1_down")
    sq_err, dh4, g_final = _loss_head(h4, small["final_norm"], tgt, "loss_head")
    loss = 0.5 * jnp.sum(sq_err) / D

    dh3, g_cw1, g_cb1, g_fn1 = ffn_bwd(1, dh4, h3, (xn4, u1, v1, a1), [], False)
    do = _mm_nt(dh3, W["w_o"], F32, "o_proj_dx")
    big["w_o"] = _mm_tn(o, dh3, "o_proj_dw")
    (dq3, dk3, dv3, dbias), landed = _attn_bwd(q3, kv3, o3, lse3, do.reshape(B, S, D), bias, "attn_bwd",
                                               comm=scatter(["w_up1", "w_o"]))
    slots.update(zip(["w_up1", "w_o"], landed))
    g_rel = _bias_grad(dbias, bmap, "rel_bias_grad")[:, :REL_BUCKETS].T
    dq, dk, dv = dq3.reshape(T, D), dk3.reshape(T, D), dv3.reshape(T, D)
    dh2, g_bn = _mm_nt(dq, W["w_q"], None, "q_proj_dx", norm=(h2, small["b_norm"][0], dh3))
    big["w_q"] = _mm_tn(xn3, dq, "q_proj_dw")
    dh2, g_kvn = _mm_nt([dk, dv], W["w_kv"], None, "kv_proj_dx", norm=(h2, small["kv_norm"], dh2))
    big["w_kv"] = jnp.concatenate([_mm_tn(kvn, dk, "k_proj_dw"), _mm_tn(kvn, dv, "v_proj_dw")], axis=1)
    dh1, g_cw0, g_cb0, g_fn0 = ffn_bwd(0, dh2, h1, (xn1, u0, v0, a0), ["w_q", "w_kv"], True)
    dz = _mm_nt(dh1, W["w_out"], BF16, "a_out_dx")
    big["w_out"] = _mm_tn(z, dh1, "a_out_dw")
    dp, g_aconv, (slots["w_out"],) = _gate_a_bwd(p, small["a_conv"][0], dz, S, "a_gate_bwd", comm=scatter(["w_out"]))
    big["w_in"] = _mm_tn(xn0, dp, "a_in_dw")
    (dx, g_an), (slots["w_in"],) = _mm_nt(dp, W["w_in"], None, "a_in_dx", comm=scatter(["w_in"]),
                                          norm=(x2, small["a_norm"][0], dh1))
    g_bn, g_kvn, g_an = g_bn[0], g_kvn[0], g_an[0]

    small_g = {"a_norm": g_an[None], "a_conv": g_aconv[None], "kv_norm": g_kvn, "b_norm": g_bn[None],
               "rel_bias": g_rel, "ffn_norm": jnp.stack([g_fn0, g_fn1]), "ffn_conv": jnp.stack([g_cw0, g_cw1]),
               "ffn_conv_b": jnp.stack([g_cb0, g_cb1]), "final_norm": g_final}
    return loss, dx.reshape(B, S, D), big, slots, small_g


BIG = ("w_in", "w_out", "w_kv", "w_q", "w_o", "w_up0", "w_up1", "w_down0", "w_down1")
SMALL = ("a_norm", "a_conv", "kv_norm", "b_norm", "rel_bias", "ffn_norm", "ffn_conv", "ffn_conv_b", "final_norm")
SMALL_SHARDED = ("a_norm", "a_conv", "ffn_conv")
WEIGHT_ORDER = ("a_norm", "a_w_in", "a_conv", "a_w_out", "kv_norm", "w_kv", "b_norm", "b_w_q", "b_w_o", "rel_bias",
                "ffn_norm", "ffn_w_up", "ffn_conv", "ffn_conv_b", "ffn_w_down", "final_norm")
GRAD_OF = {"w_in": ("a_w_in", 0), "w_out": ("a_w_out", 0), "w_kv": ("w_kv", 0), "w_q": ("b_w_q", 0), "w_o": ("b_w_o", 0),
           "w_up0": ("ffn_w_up", 0), "w_up1": ("ffn_w_up", 1), "w_down0": ("ffn_w_down", 0), "w_down1": ("ffn_w_down", 1)}


def _as2d(a):
    return a.reshape(-1, a.shape[-1])


def kernel(x, a_norm, a_w_in, a_conv, a_w_out, kv_norm, w_kv, b_norm, b_w_q, b_w_o, rel_bias, ffn_norm, ffn_w_up, ffn_conv, ffn_conv_b, ffn_w_down, final_norm, loss_target, m_a_norm, m_a_w_in, m_a_conv, m_a_w_out, m_kv_norm, m_w_kv, m_b_norm, m_b_w_q, m_b_w_o, m_rel_bias, m_ffn_norm, m_ffn_w_up, m_ffn_conv, m_ffn_conv_b, m_ffn_w_down, m_final_norm, v_a_norm, v_a_w_in, v_a_conv, v_a_w_out, v_kv_norm, v_w_kv, v_b_norm, v_b_w_q, v_b_w_o, v_rel_bias, v_ffn_norm, v_ffn_w_up, v_ffn_conv, v_ffn_conv_b, v_ffn_w_down, v_final_norm):
    given = dict(a_norm=a_norm, a_w_in=a_w_in, a_conv=a_conv, a_w_out=a_w_out, kv_norm=kv_norm, w_kv=w_kv, b_norm=b_norm,
                 b_w_q=b_w_q, b_w_o=b_w_o, rel_bias=rel_bias, ffn_norm=ffn_norm, ffn_w_up=ffn_w_up, ffn_conv=ffn_conv,
                 ffn_conv_b=ffn_conv_b, ffn_w_down=ffn_w_down, final_norm=final_norm)
    mom_m = dict(a_norm=m_a_norm, a_w_in=m_a_w_in, a_conv=m_a_conv, a_w_out=m_a_w_out, kv_norm=m_kv_norm, w_kv=m_w_kv,
                 b_norm=m_b_norm, b_w_q=m_b_w_q, b_w_o=m_b_w_o, rel_bias=m_rel_bias, ffn_norm=m_ffn_norm,
                 ffn_w_up=m_ffn_w_up, ffn_conv=m_ffn_conv, ffn_conv_b=m_ffn_conv_b, ffn_w_down=m_ffn_w_down,
                 final_norm=m_final_norm)
    mom_v = dict(a_norm=v_a_norm, a_w_in=v_a_w_in, a_conv=v_a_conv, a_w_out=v_a_w_out, kv_norm=v_kv_norm, w_kv=v_w_kv,
                 b_norm=v_b_norm, b_w_q=v_b_w_q, b_w_o=v_b_w_o, rel_bias=v_rel_bias, ffn_norm=v_ffn_norm,
                 ffn_w_up=v_ffn_w_up, ffn_conv=v_ffn_conv, ffn_conv_b=v_ffn_conv_b, ffn_w_down=v_ffn_w_down,
                 final_norm=v_final_norm)

    shard = {"w_in": (a_w_in[0], 1), "w_out": (a_w_out[0], 0), "w_kv": (w_kv, 1), "w_q": (b_w_q[0], 0),
             "w_o": (b_w_o[0], 0), "w_up0": (ffn_w_up[0], 1), "w_up1": (ffn_w_up[1], 1),
             "w_down0": (ffn_w_down[0], 0), "w_down1": (ffn_w_down[1], 0)}

    small_sharded = [given[k] for k in SMALL_SHARDED]
    packed = _pack(small_sharded)
    first = ("w_in", "w_out")
    fulls, packed_all = _gather_weights([shard[k][0] for k in first], [shard[k][1] for k in first], first, packed)
    W = dict(zip(first, fulls))
    later = {k: shard[k][0] for k in BIG if k not in first}
    geom = {k: (shard[k][0].shape, shard[k][1]) for k in BIG}
    small = {k: given[k] for k in SMALL}
    per_shard = [_unpack(packed_all[j], [a.shape for a in small_sharded]) for j in range(N_SHARDS)]
    for i, k in enumerate(SMALL_SHARDED):
        small[k] = jnp.concatenate([per_shard[j][i] for j in range(N_SHARDS)], axis=-1)

    loss, grad_x, big_g, slots, small_g = _local_step(x, loss_target, W, later, geom, small)
    loss = lax.psum(loss, MESH_AXES)

    small_shapes = [small_g[k].shape for k in SMALL]
    _, small_slots = _scatter_grads([], [], [], _pack([small_g[k] for k in SMALL]))
    layers_of = {}
    for k in BIG:
        layers_of.setdefault(GRAD_OF[k][0], []).append(k)
    blocks = {}
    for name, members in layers_of.items():
        dest = None
        for k in members:
            dest = _sum_piece(dest, slots[k], big_g[k], geom[k][0], geom[k][1], GRAD_OF[k][1], len(members), f"sum_{k}")
        blocks[name] = dest
    reduced = dict(zip(blocks, _swap_halves(list(blocks.values()))))
    small_sum = _sum_slots(small_slots, "sum_small")
    small_red = dict(zip(SMALL, _unpack(small_sum, small_shapes)))
    j = 2 * lax.axis_index("x") + lax.axis_index("y")
    for k in SMALL_SHARDED:
        w = given[k].shape[-1]
        small_red[k] = lax.dynamic_slice_in_dim(small_red[k], j * w, w, axis=small_red[k].ndim - 1)

    grads, deltas, new_m, new_v = {}, {}, {}, {}
    for name in WEIGHT_ORDER:
        if name in reduced:
            g = reduced[name].reshape(given[name].shape)
            d, nm, nv = _adamw(_as2d(given[name]), _as2d(g), _as2d(mom_m[name]), _as2d(mom_v[name]), f"adamw_{name}")
            grads[name] = g
            deltas[name], new_m[name], new_v[name] = (t.reshape(given[name].shape) for t in (d, nm, nv))
    small_names = [n for n in WEIGHT_ORDER if n not in reduced]
    for n in small_names:
        grads[n] = small_red[n].reshape(given[n].shape)
    sw, sg, sm, sv = (_pack([d[n] for n in small_names]) for d in (given, grads, mom_m, mom_v))
    d, nm, nv = _adamw(sw, sg, sm, sv, "adamw_small")
    shapes = [given[n].shape for n in small_names]
    for n, a, b_, c_ in zip(small_names, _unpack(d, shapes), _unpack(nm, shapes), _unpack(nv, shapes)):
        deltas[n], new_m[n], new_v[n] = a, b_, c_

    return (loss, grad_x, *[grads[n] for n in WEIGHT_ORDER], *[deltas[n] for n in WEIGHT_ORDER],
            *[new_m[n] for n in WEIGHT_ORDER], *[new_v[n] for n in WEIGHT_ORDER])
```

```python
import functools
import math

import numpy as np

import jax
import jax.numpy as jnp
from jax import lax
from jax.experimental import pallas as pl
from jax.experimental.pallas import tpu as pltpu

F32 = jnp.float32
BF16 = jnp.bfloat16

RMS_EPS = 1e-6
HEAD_DIM = 64
ATT_BLOCK = 128
DILATED_BRANCHES = ((128, 1), (512, 4), (2048, 16))
REL_BUCKETS = 32
REL_MAX_DISTANCE = 2048
MASKED_LOGIT = -1e30
ATTN_FWD_UNROLL = (8, 8, 4)
ATTN_BWD_UNROLL = (8, 8, 8)

ADAM_LR = 0.001
ADAM_B1 = 0.9
ADAM_B2 = 0.999
ADAM_EPS = 1e-08
ADAM_WD = 0.01
ADAM_STEP = 10

LANES = 128
SUBLANES_F32 = 8
SUBLANES_BF16 = 16
VMEM_LIMIT_BYTES = 56 * 1024 * 1024

MESH_AXES = ("x", "y", "c")
N_SHARDS = 4
N_DEVICES = 8
ANY = pl.BlockSpec(memory_space=pl.ANY)


def _tile(n, pref, mult):
    best = None
    for t in range(mult, min(n, pref) + 1, mult):
        if n % t == 0:
            best = t
    if best is None:
        raise ValueError(f"no tile for {n} (multiple of {mult}, at most {pref})")
    return best


def _params(*sem):
    return pltpu.CompilerParams(dimension_semantics=sem, vmem_limit_bytes=VMEM_LIMIT_BYTES)


class _Comm:
    def __init__(self, ins, outs, aliases, n_sems, copies):
        self.ins, self.outs, self.aliases, self.n_sems, self.copies = list(ins), list(outs), dict(aliases), n_sems, copies

    @staticmethod
    def join(parts):
        parts = [p for p in parts if p is not None]
        ins, outs, aliases, offs, n_sems = [], [], {}, [], 0
        for p in parts:
            offs.append((len(ins), len(outs), n_sems))
            aliases.update({len(ins) + i: len(outs) + o for i, o in p.aliases.items()})
            ins += p.ins
            outs += p.outs
            n_sems += p.n_sems

        def copies(in_refs, out_refs, sem):
            sends, recvs = [], []
            for p, (i0, o0, s0) in zip(parts, offs):
                s, r = p.copies(in_refs[i0:i0 + len(p.ins)], out_refs[o0:o0 + len(p.outs)],
                                lambda k, s0=s0: sem(s0 + k))
                sends += s
                recvs += r
            return sends, recvs

        return _Comm(ins, outs, aliases, n_sems, copies)


def _pallas(body, *, comm=None, name, out_shape, grid=(), in_specs=(), out_specs=(), scratch_shapes=(),
            compiler_params=None):
    if comm is None:
        return pl.pallas_call(body, name=name, out_shape=out_shape, grid=grid, in_specs=in_specs, out_specs=out_specs,
                              scratch_shapes=scratch_shapes, compiler_params=compiler_params)
    single = not isinstance(out_shape, (tuple, list))
    outs = (out_shape,) if single else tuple(out_shape)
    o_specs = (out_specs,) if single else tuple(out_specs)
    n_in, n_cin, n_out, n_cout, n_scr = len(in_specs), len(comm.ins), len(outs), len(comm.outs), len(scratch_shapes)

    def carried(*refs):
        base_in, c_in = refs[:n_in], refs[n_in:n_in + n_cin]
        o0 = n_in + n_cin
        base_out, c_out = refs[o0:o0 + n_out], refs[o0 + n_out:o0 + n_out + n_cout]
        s0 = o0 + n_out + n_cout
        base_scr, (send_sems, recv_sems) = refs[s0:s0 + n_scr], refs[s0 + n_scr:]
        sem = lambda k: (send_sems.at[k], recv_sems.at[k])
        first = functools.reduce(jnp.logical_and, [pl.program_id(a) == 0 for a in range(len(grid))])
        last = functools.reduce(jnp.logical_and, [pl.program_id(a) == n - 1 for a, n in enumerate(grid)])

        @pl.when(first)
        def _():
            for cp in comm.copies(c_in, c_out, sem)[0]:
                cp.start()

        body(*base_in, *base_out, *base_scr)

        @pl.when(last)
        def _():
            sends, recvs = comm.copies(c_in, c_out, sem)
            for cp in recvs:
                cp.wait_recv()
            for cp in sends:
                cp.wait_send()

    call = pl.pallas_call(
        carried, name=name, out_shape=outs + tuple(comm.outs), grid=grid,
        in_specs=list(in_specs) + [ANY] * n_cin, out_specs=o_specs + (ANY,) * n_cout,
        scratch_shapes=list(scratch_shapes) + [pltpu.SemaphoreType.DMA((comm.n_sems,))] * 2,
        input_output_aliases={n_in + i: n_out + o for i, o in comm.aliases.items()},
        compiler_params=_params(*(["arbitrary"] * len(grid))))

    def run(*args):
        res = call(*args, *comm.ins)
        base = res[0] if single else tuple(res[:n_out])
        return base, list(res[n_out:])

    return run


def _rmsnorm_fwd(x, g, name):
    T, D = x.shape
    tm = _tile(T, 512, SUBLANES_BF16)

    def body(x_ref, g_ref, o_ref):
        xf = x_ref[...]
        r = lax.rsqrt(jnp.mean(xf * xf, axis=-1, keepdims=True) + RMS_EPS)
        o_ref[...] = ((xf * r) * g_ref[...]).astype(o_ref.dtype)

    return pl.pallas_call(
        body, name=name, out_shape=jax.ShapeDtypeStruct((T, D), BF16), grid=(T // tm,),
        in_specs=[pl.BlockSpec((tm, D), lambda i: (i, 0)), pl.BlockSpec((1, D), lambda i: (0, 0))],
        out_specs=pl.BlockSpec((tm, D), lambda i: (i, 0)),
        compiler_params=_params("parallel"),
    )(x, g.reshape(1, D))


def _loss_head(h, g, target, name):
    T, D = h.shape
    tm = _tile(T, 512, SUBLANES_F32)

    def body(h_ref, g_ref, t_ref, dh_ref, acc_ref):
        @pl.when(pl.program_id(0) == 0)
        def _():
            acc_ref[...] = jnp.zeros_like(acc_ref)

        xf = h_ref[...]
        r = lax.rsqrt(jnp.mean(xf * xf, axis=-1, keepdims=True) + RMS_EPS)
        xhat = xf * r
        err = xhat * g_ref[...] - t_ref[...]
        dy = err * (1.0 / D)
        acc_ref[0:1, :] += jnp.sum(dy * xhat, axis=0, keepdims=True)
        acc_ref[1:2, :] += jnp.sum(err * err, axis=0, keepdims=True)
        t = dy * g_ref[...]
        dh_ref[...] = r * (t - xhat * jnp.mean(t * xhat, axis=-1, keepdims=True))

    row = pl.BlockSpec((tm, D), lambda i: (i, 0))
    dh, acc = pl.pallas_call(
        body, name=name,
        out_shape=(jax.ShapeDtypeStruct((T, D), F32), jax.ShapeDtypeStruct((SUBLANES_F32, D), F32)),
        grid=(T // tm,),
        in_specs=[row, pl.BlockSpec((1, D), lambda i: (0, 0)), row],
        out_specs=(row, pl.BlockSpec((SUBLANES_F32, D), lambda i: (0, 0))),
        compiler_params=_params("arbitrary"),
    )(h, g.reshape(1, D), target)
    return acc[1], dh, acc[0]


def _mm_nn(a, b, res, out_dtype, name, comm=None, norm_gains=()):
    T, K = a.shape
    N = b.shape[1]
    n_g = len(norm_gains)
    tm = _tile(T, 512 if n_g else 1024, SUBLANES_BF16)
    tn = _tile(N, 3072, LANES)
    assert not n_g or tn == N, "the fused rmsnorm needs whole rows in one tile"

    def body(a_ref, b_ref, *rest):
        ins, outs = rest[:len(rest) - 1 - n_g], rest[len(rest) - 1 - n_g:]
        acc = jnp.dot(a_ref[...].astype(BF16), b_ref[...], preferred_element_type=F32)
        if res is not None:
            acc = acc + ins[0][...]
        outs[0][...] = acc.astype(outs[0].dtype)
        if n_g:
            y = acc * lax.rsqrt(jnp.mean(acc * acc, axis=-1, keepdims=True) + RMS_EPS)
            for g_ref, xn_ref in zip(ins[len(ins) - n_g:], outs[1:]):
                xn_ref[...] = (y * g_ref[...]).astype(xn_ref.dtype)

    tile = pl.BlockSpec((tm, tn), lambda j, i: (i, j))
    in_specs = [pl.BlockSpec((tm, K), lambda j, i: (i, 0)), pl.BlockSpec((K, tn), lambda j, i: (0, j))]
    args = [a, b]
    if res is not None:
        in_specs.append(tile)
        args.append(res)
    for g in norm_gains:
        in_specs.append(pl.BlockSpec((1, N), lambda j, i: (0, 0)))
        args.append(g.reshape(1, N))
    out_shape = jax.ShapeDtypeStruct((T, N), out_dtype)
    if n_g:
        out_shape = (out_shape,) + (jax.ShapeDtypeStruct((T, N), BF16),) * n_g
    return _pallas(
        body, comm=comm, name=name, out_shape=out_shape, grid=(N // tn, T // tm),
        in_specs=in_specs, out_specs=(tile,) * (1 + n_g) if n_g else tile,
        compiler_params=_params("parallel", "parallel"),
    )(*args)


def _mm_nt(dy, b, out_dtype, name, comm=None, norm=None):
    dys = list(dy) if isinstance(dy, (list, tuple)) else [dy]
    T, n_each = dys[0].shape
    K = b.shape[0]
    tm = _tile(T, 2048 if norm is None else 512, SUBLANES_BF16)
    tk = _tile(K, 1536, LANES)
    tn = _tile(n_each, 2816, LANES)
    per = n_each // tn
    n_steps = per * len(dys)
    assert norm is None or tk == K, "the fused rmsnorm backward needs whole rows in one tile"

    def body(*refs):
        dy_refs, b_ref, acc_ref = refs[:len(dys)], refs[len(dys)], refs[-1]
        i, n = pl.program_id(0), pl.program_id(2)

        @pl.when(n == 0)
        def _():
            acc_ref[...] = jnp.zeros_like(acc_ref)

        for p, dy_ref in enumerate(dy_refs):
            @pl.when(jnp.logical_and(n >= p * per, n < (p + 1) * per))
            def _(dy_ref=dy_ref):
                acc_ref[...] += lax.dot_general(dy_ref[...].astype(BF16), b_ref[...], (((1,), (1,)), ((), ())),
                                                preferred_element_type=F32)

        if norm is None:
            @pl.when(n == n_steps - 1)
            def _():
                refs[-2][...] = acc_ref[...].astype(refs[-2].dtype)
        else:
            x_ref, g_ref, dres_ref, dx_ref, dg_ref = refs[len(dys) + 1:-1]

            @pl.when(jnp.logical_and(i == 0, n == 0))
            def _():
                dg_ref[...] = jnp.zeros_like(dg_ref)

            @pl.when(n == n_steps - 1)
            def _():
                xf = x_ref[...]
                r = lax.rsqrt(jnp.mean(xf * xf, axis=-1, keepdims=True) + RMS_EPS)
                xhat = xf * r
                d = acc_ref[...]
                dg_ref[0:1, :] += jnp.sum(d * xhat, axis=0, keepdims=True)
                t = d * g_ref[...]
                dx_ref[...] = dres_ref[...] + r * (t - xhat * jnp.mean(t * xhat, axis=-1, keepdims=True))

    in_specs = [pl.BlockSpec((tm, tn), lambda i, k, n, p=p: (i, jnp.clip(n - p * per, 0, per - 1))) for p in range(len(dys))]
    in_specs.append(pl.BlockSpec((tk, tn), lambda i, k, n: (k, n)))
    args = dys + [b]
    tile = pl.BlockSpec((tm, tk), lambda i, k, n: (i, k))
    if norm is None:
        out_shape, out_specs = jax.ShapeDtypeStruct((T, K), out_dtype), tile
    else:
        x, g, dres = norm
        in_specs += [tile, pl.BlockSpec((1, K), lambda i, k, n: (0, 0)), tile]
        args += [x, g.reshape(1, K), dres]
        out_shape = (jax.ShapeDtypeStruct((T, K), F32), jax.ShapeDtypeStruct((SUBLANES_F32, K), F32))
        out_specs = (tile, pl.BlockSpec((SUBLANES_F32, K), lambda i, k, n: (0, 0)))
    return _pallas(
        body, comm=comm, name=name, out_shape=out_shape, grid=(T // tm, K // tk, n_steps),
        in_specs=in_specs, out_specs=out_specs, scratch_shapes=[pltpu.VMEM((tm, tk), F32)],
        compiler_params=_params("parallel", "parallel", "arbitrary") if norm is None else _params(*["arbitrary"] * 3),
    )(*args)


def _mm_tn(a, dy, name):
    T, K = a.shape
    N = dy.shape[1]
    tt = _tile(T, 2048, SUBLANES_BF16)
    tk = _tile(K, 1536, LANES)
    tn = _tile(N, 1536, LANES)
    t_steps = T // tt

    def body(a_ref, dy_ref, o_ref, acc_ref):
        t = pl.program_id(2)

        @pl.when(t == 0)
        def _():
            acc_ref[...] = jnp.zeros_like(acc_ref)

        acc_ref[...] += lax.dot_general(a_ref[...].astype(BF16), dy_ref[...].astype(BF16),
                                        (((0,), (0,)), ((), ())), preferred_element_type=F32)

        @pl.when(t == t_steps - 1)
        def _():
            o_ref[...] = acc_ref[...].astype(o_ref.dtype)

    return pl.pallas_call(
        body, name=name, out_shape=jax.ShapeDtypeStruct((K, N), BF16), grid=(K // tk, N // tn, t_steps),
        in_specs=[pl.BlockSpec((tt, tk), lambda k, n, t: (t, k)), pl.BlockSpec((tt, tn), lambda k, n, t: (t, n))],
        out_specs=pl.BlockSpec((tk, tn), lambda k, n, t: (k, n)),
        scratch_shapes=[pltpu.VMEM((tk, tn), F32)],
        compiler_params=_params("parallel", "parallel", "arbitrary"),
    )(a, dy)


def _rows_before(halo, cur, k):
    h = halo.shape[0]
    return pltpu.roll(jnp.concatenate([halo, cur], axis=0), k, 0)[h:]


def _rows_after(cur, halo, k):
    n = cur.shape[0]
    total = n + halo.shape[0]
    return pltpu.roll(jnp.concatenate([cur, halo], axis=0), total - k, 0)[:n]


def _halo_specs(tm, width, n_rows):
    per = tm // SUBLANES_BF16
    last = n_rows // SUBLANES_BF16 - 1
    prev = pl.BlockSpec((SUBLANES_BF16, width), lambda i: (jnp.maximum(i * per - 1, 0), 0))
    nxt = pl.BlockSpec((SUBLANES_BF16, width), lambda i: (jnp.minimum((i + 1) * per, last), 0))
    return prev, nxt


def _gate_a_fwd(p, cw, seq, name, comm=None):
    T, D3 = p.shape
    D = D3 // 3
    tm = _tile(seq, 512, SUBLANES_BF16)
    cc = _tile(D, 256, LANES)
    prev, _ = _halo_specs(tm, D3, T)

    def body(p_ref, ph_ref, cw_ref, z_ref):
        at_start = (pl.program_id(0) * tm) % seq == 0
        for c0 in range(0, D, cc):
            b = p_ref[:, c0:c0 + cc].astype(F32)
            u = p_ref[:, D + c0:D + c0 + cc].astype(F32) * p_ref[:, 2 * D + c0:2 * D + c0 + cc].astype(F32)
            uh = ph_ref[:, D + c0:D + c0 + cc].astype(F32) * ph_ref[:, 2 * D + c0:2 * D + c0 + cc].astype(F32)
            uh = jnp.where(at_start, 0.0, uh)
            w = cw_ref[:, c0:c0 + cc]
            cv = _rows_before(uh, u, 2) * w[0:1] + _rows_before(uh, u, 1) * w[1:2] + u * w[2:3]
            z_ref[:, c0:c0 + cc] = (b * cv).astype(z_ref.dtype)

    return _pallas(
        body, comm=comm, name=name, out_shape=jax.ShapeDtypeStruct((T, D), BF16), grid=(T // tm,),
        in_specs=[pl.BlockSpec((tm, D3), lambda i: (i, 0)), prev, pl.BlockSpec((3, D), lambda i: (0, 0))],
        out_specs=pl.BlockSpec((tm, D), lambda i: (i, 0)),
        compiler_params=_params("parallel"),
    )(p, p, cw)


def _gate_a_bwd(p, cw, dz, seq, name, comm=None):
    T, D3 = p.shape
    D = D3 // 3
    tm = _tile(seq, 512, SUBLANES_BF16)
    cc = _tile(D, 256, LANES)
    p_prev, p_next = _halo_specs(tm, D3, T)
    _, dz_next = _halo_specs(tm, D, T)

    def body(p_ref, pp_ref, pn_ref, dz_ref, dzn_ref, cw_ref, dp_ref, dcw_ref):
        i = pl.program_id(0)

        @pl.when(i == 0)
        def _():
            dcw_ref[...] = jnp.zeros_like(dcw_ref)

        at_start = (i * tm) % seq == 0
        at_end = ((i + 1) * tm) % seq == 0
        for c0 in range(0, D, cc):
            cb, cc_, ch = slice(c0, c0 + cc), slice(D + c0, D + c0 + cc), slice(2 * D + c0, 2 * D + c0 + cc)
            b = p_ref[:, cb].astype(F32)
            c = p_ref[:, cc_].astype(F32)
            hh = p_ref[:, ch].astype(F32)
            u = c * hh
            uh = jnp.where(at_start, 0.0, pp_ref[:, cc_].astype(F32) * pp_ref[:, ch].astype(F32))
            w = cw_ref[:, cb]
            u1 = _rows_before(uh, u, 1)
            u2 = _rows_before(uh, u, 2)
            cv = u2 * w[0:1] + u1 * w[1:2] + u * w[2:3]
            dz_t = dz_ref[:, cb].astype(F32)
            dcv = dz_t * b
            dcvn = jnp.where(at_end, 0.0, dzn_ref[:, cb].astype(F32) * pn_ref[:, cb].astype(F32))
            du = dcv * w[2:3] + _rows_after(dcv, dcvn, 1) * w[1:2] + _rows_after(dcv, dcvn, 2) * w[0:1]
            dp_ref[:, cb] = (dz_t * cv).astype(dp_ref.dtype)
            dp_ref[:, cc_] = (du * hh).astype(dp_ref.dtype)
            dp_ref[:, ch] = (du * c).astype(dp_ref.dtype)
            dcw_ref[0:1, cb] += jnp.sum(dcv * u2, axis=0, keepdims=True)
            dcw_ref[1:2, cb] += jnp.sum(dcv * u1, axis=0, keepdims=True)
            dcw_ref[2:3, cb] += jnp.sum(dcv * u, axis=0, keepdims=True)

    res = _pallas(
        body, comm=comm, name=name,
        out_shape=(jax.ShapeDtypeStruct((T, D3), BF16), jax.ShapeDtypeStruct((SUBLANES_F32, D), F32)),
        grid=(T // tm,),
        in_specs=[pl.BlockSpec((tm, D3), lambda i: (i, 0)), p_prev, p_next,
                  pl.BlockSpec((tm, D), lambda i: (i, 0)), dz_next, pl.BlockSpec((3, D), lambda i: (0, 0))],
        out_specs=(pl.BlockSpec((tm, D3), lambda i: (i, 0)), pl.BlockSpec((SUBLANES_F32, D), lambda i: (0, 0))),
        compiler_params=_params("arbitrary"),
    )(p, p, p, dz, dz, cw)
    if comm is None:
        return res[0], res[1][0:3]
    return res[0][0], res[0][1][0:3], res[1]


def _ffn_gate_fwd(u, cw, cb, seq, name, comm=None):
    T, F2 = u.shape
    F = F2 // 2
    tm = _tile(seq, 256, SUBLANES_BF16)
    cc = _tile(F, 256, LANES)
    prev, _ = _halo_specs(tm, F2, T)

    def body(u_ref, uh_ref, cw_ref, cb_ref, a_ref, v_ref):
        at_start = (pl.program_id(0) * tm) % seq == 0

        def conv(c0):
            cols = slice(c0, c0 + cc)
            cur = u_ref[:, cols].astype(F32)
            halo = jnp.where(at_start, 0.0, uh_ref[:, cols].astype(F32))
            w = cw_ref[:, cols]
            return (_rows_before(halo, cur, 2) * w[0:1] + _rows_before(halo, cur, 1) * w[1:2] + cur * w[2:3]
                    + cb_ref[:, cols])

        for c0 in range(0, F, cc):
            g = conv(c0)
            up = conv(F + c0)
            a_ref[:, c0:c0 + cc] = ((g * jax.nn.sigmoid(g)) * up).astype(a_ref.dtype)
            v_ref[:, c0:c0 + cc] = g.astype(v_ref.dtype)
            v_ref[:, F + c0:F + c0 + cc] = up.astype(v_ref.dtype)

    return _pallas(
        body, comm=comm, name=name,
        out_shape=(jax.ShapeDtypeStruct((T, F), BF16), jax.ShapeDtypeStruct((T, F2), BF16)), grid=(T // tm,),
        in_specs=[pl.BlockSpec((tm, F2), lambda i: (i, 0)), prev,
                  pl.BlockSpec((3, F2), lambda i: (0, 0)), pl.BlockSpec((1, F2), lambda i: (0, 0))],
        out_specs=(pl.BlockSpec((tm, F), lambda i: (i, 0)), pl.BlockSpec((tm, F2), lambda i: (i, 0))),
        compiler_params=_params("parallel"),
    )(u, u, cw, cb.reshape(1, F2))


def _ffn_gate_bwd(u, v, cw, da, seq, name, comm=None):
    T, F2 = u.shape
    F = F2 // 2
    H = SUBLANES_BF16
    tm = _tile(seq, 256, H)
    cc = _tile(F, 256, LANES)
    _, v_next = _halo_specs(tm, F2, T)
    _, da_next = _halo_specs(tm, F, T)

    def body(u_ref, v_ref, vn_ref, da_ref, dan_ref, cw_ref, du_ref, acc_ref):
        i = pl.program_id(0)

        @pl.when(i == 0)
        def _():
            acc_ref[...] = jnp.zeros_like(acc_ref)

        at_end = ((i + 1) * tm) % seq == 0
        n = tm + H

        def rows_and_next(ref, nxt, cols):
            return jnp.concatenate([ref[:, cols].astype(F32), nxt[:, cols].astype(F32)], axis=0)

        def back(d, cols):
            w = cw_ref[:, cols]
            d0 = d[:tm]
            d1 = pltpu.roll(d, n - 1, 0)[:tm]
            d2 = pltpu.roll(d, n - 2, 0)[:tm]
            du_ref[:, cols] = (d0 * w[2:3] + d1 * w[1:2] + d2 * w[0:1]).astype(du_ref.dtype)
            ut = u_ref[:, cols].astype(F32)
            acc_ref[0:1, cols] += jnp.sum(d2 * ut, axis=0, keepdims=True)
            acc_ref[1:2, cols] += jnp.sum(d1 * ut, axis=0, keepdims=True)
            acc_ref[2:3, cols] += jnp.sum(d0 * ut, axis=0, keepdims=True)
            acc_ref[3:4, cols] += jnp.sum(d0, axis=0, keepdims=True)

        for c0 in range(0, F, cc):
            gc, uc = slice(c0, c0 + cc), slice(F + c0, F + c0 + cc)
            g = rows_and_next(v_ref, vn_ref, gc)
            up = rows_and_next(v_ref, vn_ref, uc)
            da_ext = jnp.concatenate([da_ref[:, gc].astype(F32),
                                      jnp.where(at_end, 0.0, dan_ref[:, gc].astype(F32))], axis=0)
            sg = jax.nn.sigmoid(g)
            back(da_ext * up * (sg * (1.0 + g * (1.0 - sg))), gc)
            back(da_ext * (g * sg), uc)

    res = _pallas(
        body, comm=comm, name=name,
        out_shape=(jax.ShapeDtypeStruct((T, F2), BF16), jax.ShapeDtypeStruct((SUBLANES_F32, F2), F32)),
        grid=(T // tm,),
        in_specs=[pl.BlockSpec((tm, F2), lambda i: (i, 0)), pl.BlockSpec((tm, F2), lambda i: (i, 0)), v_next,
                  pl.BlockSpec((tm, F), lambda i: (i, 0)), da_next, pl.BlockSpec((3, F2), lambda i: (0, 0))],
        out_specs=(pl.BlockSpec((tm, F2), lambda i: (i, 0)), pl.BlockSpec((SUBLANES_F32, F2), lambda i: (0, 0))),
        compiler_params=_params("arbitrary"),
    )(u, v, v, da, da, cw)
    (du, acc), landed = res if comm is not None else (res, None)
    return (du, acc[0:3], acc[3]) if comm is None else (du, acc[0:3], acc[3], landed)


def _bucket_map():
    P = ATT_BLOCK
    qi = np.arange(P, dtype=np.int64)[:, None]
    kc = np.arange(2 * P, dtype=np.int64)[None, :]
    delta = qi + P - kc
    maps = []
    max_exact = REL_BUCKETS // 2
    for window, dilation in DILATED_BRANCHES:
        band = (delta >= 0) & (delta <= window // dilation)
        n = np.maximum(delta * dilation, 0)
        nf = np.maximum(n, max_exact).astype(np.float32)
        large = max_exact + (np.log(nf / np.float32(max_exact)) / np.float32(math.log(REL_MAX_DISTANCE / max_exact))
                             * np.float32(REL_BUCKETS - max_exact)).astype(np.int32)
        large = np.minimum(large, REL_BUCKETS - 1)
        bucket = np.where(n < max_exact, n, large)
        maps.append(np.where(band, bucket, -1).astype(np.int32))
    return np.stack(maps)


def _bias_tables(rel_bias, bmap, name):
    n_pairs = rel_bias.shape[1] // 2
    nbr, P, P2 = bmap.shape

    def body(rb_ref, bm_ref, o_ref):
        pair = pl.program_id(0)
        in_seq = lax.broadcasted_iota(jnp.int32, (P, P2), 1) >= P
        for br in range(nbr):
            bm = bm_ref[br]
            for hh in range(2):
                acc = jnp.full((P, P2), MASKED_LOGIT, F32)
                for b in range(REL_BUCKETS):
                    acc = jnp.where(bm == b, rb_ref[b, 2 * pair + hh], acc)
                o_ref[br, 0, 0, hh * P:(hh + 1) * P, :] = acc
                o_ref[br, 0, 1, hh * P:(hh + 1) * P, :] = jnp.where(in_seq, acc, MASKED_LOGIT)

    return pl.pallas_call(
        body, name=name, out_shape=jax.ShapeDtypeStruct((nbr, n_pairs, 2, 2 * P, P2), F32), grid=(n_pairs,),
        in_specs=[pl.BlockSpec(memory_space=pltpu.SMEM), pl.BlockSpec((nbr, P, P2), lambda h: (0, 0, 0))],
        out_specs=pl.BlockSpec((nbr, 1, 2, 2 * P, P2), lambda h: (0, h, 0, 0, 0)),
        compiler_params=_params("parallel"),
    )(rel_bias, bmap)


def _bias_grad(dbias, bmap, name):
    nbr, n_pairs, _, P2 = dbias.shape
    P = P2 // 2

    def body(db_ref, bm_ref, o_ref):
        lane = lax.broadcasted_iota(jnp.int32, (1, LANES), 1)
        for hh in range(2):
            row = jnp.zeros((1, LANES), F32)
            for br in range(nbr):
                bm = bm_ref[br]
                d = db_ref[br, 0, hh * P:(hh + 1) * P, :]
                for b in range(REL_BUCKETS):
                    hit = jnp.sum(jnp.where(bm == b, d, 0.0), axis=1, keepdims=True)
                    row = row + jnp.where(lane == b, jnp.sum(hit, axis=0, keepdims=True), 0.0)
            o_ref[hh] = row

    return pl.pallas_call(
        body, name=name, out_shape=jax.ShapeDtypeStruct((2 * n_pairs, 1, LANES), F32), grid=(n_pairs,),
        in_specs=[pl.BlockSpec((nbr, 1, P2, P2), lambda h: (0, h, 0, 0)), pl.BlockSpec((nbr, P, P2), lambda h: (0, 0, 0))],
        out_specs=pl.BlockSpec((2, 1, LANES), lambda h: (h, 0, 0)),
        compiler_params=_params("parallel"),
    )(dbias, bmap)[:, 0, :]


def _rows(start, dilation):
    if dilation == 1:
        return pl.ds(pl.multiple_of(start, ATT_BLOCK), ATT_BLOCK)
    return pl.ds(start, ATT_BLOCK, stride=dilation)


def _for_each_block(seq, unroll, fn):
    P = ATT_BLOCK
    for br, (_, d) in enumerate(DILATED_BRANCHES):
        nb = seq // d // P
        u = unroll[br] if (unroll[br] % nb == 0 or nb % unroll[br] == 0) else 1
        step = d * P

        def some(i, carry, br=br, d=d, nb=nb, u=u, step=step):
            blocks = []
            if u % nb == 0:
                for k in range(u):
                    if k % nb == 0:
                        start = i * (u // nb) + k // nb
                        blocks.append((start, start, 1))
                    else:
                        blocks.append((blocks[-1][0] + step, blocks[-1][0], 0))
            else:
                r, j0 = (i * u) // nb, (i * u) % nb
                blocks.append((r + j0 * step, r + jnp.maximum(j0 - 1, 0) * step, jnp.where(j0 == 0, 1, 0)))
                for _ in range(1, u):
                    blocks.append((blocks[-1][0] + step, blocks[-1][0], 0))
            fn(br, d, blocks)
            return carry

        lax.fori_loop(0, nb * d // u, some, 0)


class _RowCache:
    def __init__(self, dilation):
        self.dilation, self.seen = dilation, {}

    def rows(self, ref, start):
        key = (id(ref), id(start))
        if key not in self.seen:
            self.seen[key] = ref[_rows(start, self.dilation), :].astype(BF16)
        return self.seen[key]

    def window(self, ref, start, prev):
        return jnp.concatenate([self.rows(ref, prev), self.rows(ref, start)], axis=0)


def _stack_heads(x, head0):
    return jnp.concatenate([jnp.where(head0, x, 0.0), jnp.where(head0, 0.0, x)], axis=0).astype(BF16)


def _attn_fwd(q, kv, bias, name):
    B, S, D = q.shape
    P = ATT_BLOCK
    n_pairs = D // LANES
    nbr = len(DILATED_BRANCHES)
    scale = HEAD_DIM ** -0.5

    def body(q_ref, k_ref, v_ref, bias_ref, o_ref, lse_ref, *stats):
        m_s, l_s, acc_s = stats[0:nbr], stats[nbr:2 * nbr], stats[2 * nbr:3 * nbr]
        head0 = lax.broadcasted_iota(jnp.int32, (P, LANES), 1) < HEAD_DIM

        def block(br, d, blocks):
            cache = _RowCache(d)
            s = [lax.dot_general(_stack_heads(q_ref[_rows(start, d), :] * scale, head0),
                                 cache.window(k_ref, start, prev), (((1,), (1,)), ((), ())),
                                 preferred_element_type=F32) + bias_ref[br, 0, first]
                 for start, prev, first in blocks]
            m = [jnp.max(x, axis=-1, keepdims=True) for x in s]
            p = [jnp.exp(x - y) for x, y in zip(s, m)]
            l = [jnp.sum(x, axis=-1, keepdims=True) for x in p]
            pv = [jnp.dot(x.astype(BF16), cache.window(v_ref, start, prev), preferred_element_type=F32)
                  for x, (start, prev, _) in zip(p, blocks)]
            for k, (start, _, _) in enumerate(blocks):
                rows = _rows(start, d)
                m_s[br][rows, :] = jnp.where(head0, m[k][:P], m[k][P:])
                l_s[br][rows, :] = jnp.where(head0, l[k][:P], l[k][P:])
                acc_s[br][rows, :] = jnp.where(head0, pv[k][:P], pv[k][P:])

        _for_each_block(S, ATTN_FWD_UNROLL, block)

        chunk = _tile(S, 256, SUBLANES_F32)

        def merge(i, carry):
            rows = pl.ds(pl.multiple_of(i * chunk, chunk), chunk)
            ms = [m_s[br][rows, :] for br in range(nbr)]
            m = functools.reduce(jnp.maximum, ms)
            l = jnp.zeros((chunk, LANES), F32)
            acc = jnp.zeros((chunk, LANES), F32)
            for br in range(nbr):
                w = jnp.exp(ms[br] - m)
                l = l + w * l_s[br][rows, :]
                acc = acc + w * acc_s[br][rows, :]
            o_ref[rows, :] = acc / l
            lse_ref[rows, :] = m + jnp.log(l)
            return carry

        lax.fori_loop(0, S // chunk, merge, 0)

    slab = lambda col0: pl.BlockSpec((None, S, LANES), lambda b, h: (b, 0, col0 + h))
    return pl.pallas_call(
        body, name=name,
        out_shape=(jax.ShapeDtypeStruct((B, S, D), F32), jax.ShapeDtypeStruct((B, S, D), F32)),
        grid=(B, n_pairs),
        in_specs=[slab(0), slab(0), slab(n_pairs),
                  pl.BlockSpec((nbr, 1, 2, 2 * P, 2 * P), lambda b, h: (0, h, 0, 0, 0))],
        out_specs=(slab(0), slab(0)),
        scratch_shapes=[pltpu.VMEM((S, LANES), F32)] * (3 * nbr),
        compiler_params=_params("parallel", "parallel"),
    )(q, kv, kv, bias)


def _attn_bwd(q, kv, o, lse, do, bias, name, comm=None):
    B, S, D = q.shape
    P = ATT_BLOCK
    n_pairs = D // LANES
    nbr = len(DILATED_BRANCHES)
    scale = HEAD_DIM ** -0.5

    def body(q_ref, k_ref, v_ref, o_ref, lse_ref, do_ref, bias_ref, dq_ref, dk_ref, dv_ref, dbias_ref, delta_s):
        head0 = lax.broadcasted_iota(jnp.int32, (P, LANES), 1) < HEAD_DIM

        @pl.when(pl.program_id(1) == 0)
        def _():
            dbias_ref[...] = jnp.zeros_like(dbias_ref)

        chunk = _tile(S, 512, SUBLANES_F32)

        def prepare(i, carry):
            rows = pl.ds(pl.multiple_of(i * chunk, chunk), chunk)
            x = do_ref[rows, :] * o_ref[rows, :]
            h0 = lax.broadcasted_iota(jnp.int32, (chunk, LANES), 1) < HEAD_DIM
            d0 = jnp.sum(jnp.where(h0, x, 0.0), axis=-1, keepdims=True)
            d1 = jnp.sum(jnp.where(h0, 0.0, x), axis=-1, keepdims=True)
            delta_s[rows, :] = jnp.where(h0, d0, d1)
            zero = jnp.zeros((chunk, LANES), F32)
            dq_ref[rows, :] = zero
            dk_ref[rows, :] = zero
            dv_ref[rows, :] = zero
            return carry

        lax.fori_loop(0, S // chunk, prepare, 0)

        def per_head(x):
            return jnp.concatenate([x[:, 0:1], x[:, HEAD_DIM:HEAD_DIM + 1]], axis=0)

        nt = (((1,), (1,)), ((), ()))
        tn = (((0,), (0,)), ((), ()))

        def block(br, d, blocks):
            cache = _RowCache(d)
            q2 = [_stack_heads(q_ref[_rows(start, d), :] * scale, head0) for start, _, _ in blocks]
            do2 = [_stack_heads(do_ref[_rows(start, d), :], head0) for start, _, _ in blocks]
            kb = [cache.window(k_ref, start, prev) for start, prev, _ in blocks]
            vb = [cache.window(v_ref, start, prev) for start, prev, _ in blocks]
            s = [lax.dot_general(a, b, nt, preferred_element_type=F32) + bias_ref[br, 0, first]
                 for a, b, (_, _, first) in zip(q2, kb, blocks)]
            dp = [lax.dot_general(a, b, nt, preferred_element_type=F32) for a, b in zip(do2, vb)]
            p = [jnp.exp(x - per_head(lse_ref[_rows(start, d), :])) for x, (start, _, _) in zip(s, blocks)]
            ds = [x * (y - per_head(delta_s[_rows(start, d), :])) for x, y, (start, _, _) in zip(p, dp, blocks)]
            for x in ds:
                dbias_ref[br, 0] += x
            ds16 = [x.astype(BF16) for x in ds]
            dq2 = [jnp.dot(a, b, preferred_element_type=F32) for a, b in zip(ds16, kb)]
            dk = [lax.dot_general(a, b, tn, preferred_element_type=F32) for a, b in zip(ds16, q2)]
            dv = [lax.dot_general(a.astype(BF16), b, tn, preferred_element_type=F32) for a, b in zip(p, do2)]
            parts = {}
            for k, (start, prev, first) in enumerate(blocks):
                dq_ref[_rows(start, d), :] += jnp.where(head0, dq2[k][:P], dq2[k][P:]) * scale
                parts.setdefault(id(start), [start, []])[1].append((dk[k][P:], dv[k][P:]))
                if not (isinstance(first, int) and first == 1):
                    parts.setdefault(id(prev), [prev, []])[1].append((dk[k][:P], dv[k][:P]))
            for start, terms in parts.values():
                rows = _rows(start, d)
                dk_ref[rows, :] += functools.reduce(jnp.add, [t[0] for t in terms])
                dv_ref[rows, :] += functools.reduce(jnp.add, [t[1] for t in terms])

        _for_each_block(S, ATTN_BWD_UNROLL, block)

    slab = lambda col0: pl.BlockSpec((None, S, LANES), lambda h, b: (b, 0, col0 + h))
    tab = pl.BlockSpec((nbr, 1, 2, 2 * P, 2 * P), lambda h, b: (0, h, 0, 0, 0))
    dtab = pl.BlockSpec((nbr, 1, 2 * P, 2 * P), lambda h, b: (0, h, 0, 0))
    shp = jax.ShapeDtypeStruct((B, S, D), F32)
    return _pallas(
        body, comm=comm, name=name,
        out_shape=(shp, shp, shp, jax.ShapeDtypeStruct((nbr, n_pairs, 2 * P, 2 * P), F32)),
        grid=(n_pairs, B),
        in_specs=[slab(0), slab(0), slab(n_pairs), slab(0), slab(0), slab(0), tab],
        out_specs=(slab(0), slab(0), slab(0), dtab),
        scratch_shapes=[pltpu.VMEM((S, LANES), F32)],
        compiler_params=_params("parallel", "arbitrary"),
    )(q, kv, kv, o, lse, do, bias)


def _adamw(w, g, m, v, name):
    R, C = w.shape
    tr = _tile(R, 256, SUBLANES_F32) if R % SUBLANES_F32 == 0 else R
    tc = _tile(C, 2048, LANES) if C % LANES == 0 else C

    def body(w_ref, g_ref, m_ref, v_ref, d_ref, nm_ref, nv_ref):
        g_ = g_ref[...]
        m2 = ADAM_B1 * m_ref[...] + (1.0 - ADAM_B1) * g_
        v2 = ADAM_B2 * v_ref[...] + (1.0 - ADAM_B2) * (g_ * g_)
        m_hat = m2 / (1.0 - ADAM_B1 ** ADAM_STEP)
        v_hat = v2 / (1.0 - ADAM_B2 ** ADAM_STEP)
        d_ref[...] = -ADAM_LR * (m_hat / (jnp.sqrt(v_hat) + ADAM_EPS) + ADAM_WD * w_ref[...])
        nm_ref[...] = m2
        nv_ref[...] = v2

    blk = pl.BlockSpec((tr, tc), lambda i, j: (i, j))
    shp = jax.ShapeDtypeStruct((R, C), F32)
    return pl.pallas_call(
        body, name=name, out_shape=(shp, shp, shp), grid=(R // tr, C // tc),
        in_specs=[blk] * 4, out_specs=(blk,) * 3, compiler_params=_params("parallel", "parallel"),
    )(w, g, m, v)


def _sum_slots(slots, name):
    n, R, C = slots.shape
    tr = _tile(R, 256, SUBLANES_BF16) if R % SUBLANES_BF16 == 0 else R
    tc = _tile(C, 2048, LANES) if C % LANES == 0 else C

    def body(s_ref, o_ref):
        acc = s_ref[0].astype(F32)
        for k in range(1, n):
            acc = acc + s_ref[k].astype(F32)
        o_ref[...] = acc

    return pl.pallas_call(
        body, name=name, out_shape=jax.ShapeDtypeStruct((R, C), F32), grid=(R // tr, C // tc),
        in_specs=[pl.BlockSpec((n, tr, tc), lambda i, j: (0, i, j))],
        out_specs=pl.BlockSpec((tr, tc), lambda i, j: (i, j)),
        compiler_params=_params("parallel", "parallel"),
    )(slots)


def _my_place():
    return lax.axis_index("x"), lax.axis_index("y"), lax.axis_index("c")


def _other_chips(x, y):
    return [(1 - x, y), (x, 1 - y), (1 - x, 1 - y)]


def _piece(ref, blk, axis, shard, half):
    h0 = blk[0] // 2
    idx = []
    for dim, n in enumerate(blk):
        if dim == 0:
            start = half * h0 + (shard * n if axis == 0 else 0)
            idx.append(pl.ds(start, h0))
        elif dim == axis:
            idx.append(pl.ds(shard * n, n))
        else:
            idx.append(slice(None))
    return ref.at[tuple(idx)]


def _half_of(ref, blk, half):
    return ref.at[pl.ds(half * (blk[0] // 2), blk[0] // 2)]


def _gather_weights(shards, axes, names, small):
    n = len(shards)
    blks = [s.shape for s in shards]
    task = _gather_ici_task(shards, axes, names)

    def stage1(*refs):
        small_in, outs, small_out = refs[n], refs[n + 1:2 * n + 1], refs[2 * n + 1]
        send_sems, recv_sems, local_sems = refs[2 * n + 2:]
        sem = lambda k: (send_sems.at[k], recv_sems.at[k])
        x, y, c = _my_place()
        me = 2 * x + y
        local = pltpu.make_async_copy(small_in, small_out.at[me], local_sems.at[0])
        local.start()
        sends, recvs = task.copies(None, outs, sem)
        for k, (px, py) in enumerate(_other_chips(x, y)):
            sends.append(_remote(small_in, small_out.at[me], sem(task.n_sems + k), (px, py, c)))
            recvs.append(_remote(small_in, small_out.at[2 * px + py], sem(task.n_sems + k), (px, py, c)))
        for cp in sends:
            cp.start()
        for cp in recvs:
            cp.wait_recv()
        for cp in sends:
            cp.wait_send()
        local.wait()

    res = pl.pallas_call(
        stage1, name="gather_weights_ici",
        out_shape=task.outs + [jax.ShapeDtypeStruct((N_SHARDS,) + small.shape, small.dtype)],
        in_specs=[ANY] * (n + 1), out_specs=[ANY] * (n + 1), input_output_aliases={a: a for a in range(n)},
        scratch_shapes=[pltpu.SemaphoreType.DMA((task.n_sems + 3,)), pltpu.SemaphoreType.DMA((task.n_sems + 3,)),
                        pltpu.SemaphoreType.DMA((1,))],
    )(*task.ins, small)
    full = _run_comm(_gather_d2d_task(list(res[:n]), blks, axes), "gather_weights_d2d")
    return full, res[n]


def _run_comm(comm, name):
    n_in, n_out = len(comm.ins), len(comm.outs)

    def body(*refs):
        send_sems, recv_sems = refs[n_in + n_out:]
        sends, recvs = comm.copies(refs[:n_in], refs[n_in:n_in + n_out], lambda k: (send_sems.at[k], recv_sems.at[k]))
        for cp in sends:
            cp.start()
        for cp in recvs:
            cp.wait_recv()
        for cp in sends:
            cp.wait_send()

    return list(pl.pallas_call(
        body, name=name, out_shape=comm.outs, in_specs=[ANY] * n_in, out_specs=[ANY] * n_out,
        input_output_aliases=comm.aliases, scratch_shapes=[pltpu.SemaphoreType.DMA((comm.n_sems,))] * 2,
    )(*comm.ins))


def _exchange_small(small, extra, name):
    n_in, n_out = len(extra.ins), len(extra.outs)

    def body(*refs):
        ex_in, small_in = refs[:n_in], refs[n_in]
        ex_out, small_out = refs[n_in + 1:n_in + 1 + n_out], refs[n_in + 1 + n_out]
        send_sems, recv_sems, local_sems = refs[n_in + n_out + 2:]
        sem = lambda k: (send_sems.at[k], recv_sems.at[k])
        x, y, c = _my_place()
        me = 4 * x + 2 * y + c
        local = pltpu.make_async_copy(small_in, small_out.at[me], local_sems.at[0])
        local.start()
        sends, recvs = extra.copies(ex_in, ex_out, sem)
        for rel in range(1, N_DEVICES):
            px, py, pc = x ^ ((rel >> 2) & 1), y ^ ((rel >> 1) & 1), c ^ (rel & 1)
            k = extra.n_sems + rel - 1
            sends.append(_remote(small_in, small_out.at[me], sem(k), (px, py, pc)))
            recvs.append(_remote(small_in, small_out.at[4 * px + 2 * py + pc], sem(k), (px, py, pc)))
        for cp in sends:
            cp.start()
        for cp in recvs:
            cp.wait_recv()
        for cp in sends:
            cp.wait_send()
        local.wait()

    n_sems = extra.n_sems + N_DEVICES - 1
    res = pl.pallas_call(
        body, name=name, out_shape=extra.outs + [jax.ShapeDtypeStruct((N_DEVICES,) + small.shape, small.dtype)],
        in_specs=[ANY] * (n_in + 1), out_specs=[ANY] * (n_out + 1), input_output_aliases=extra.aliases,
        scratch_shapes=[pltpu.SemaphoreType.DMA((n_sems,)), pltpu.SemaphoreType.DMA((n_sems,)),
                        pltpu.SemaphoreType.DMA((1,))],
    )(*extra.ins, small)
    return list(res[:n_out]), res[n_out]


def _remote(src, dst, sems, device):
    return pltpu.make_async_remote_copy(src_ref=src, dst_ref=dst, send_sem=sems[0], recv_sem=sems[1],
                                        device_id=device, device_id_type=pl.DeviceIdType.MESH)


def _sum_piece(dest, slots, grad, blk, axis, layer, n_layers, name):
    r, c = blk
    h0 = r // 2
    tr = _tile(h0, 256, SUBLANES_BF16)
    tc = _tile(c, 2048, LANES)
    place = jnp.stack([2 * lax.axis_index("x") + lax.axis_index("y"), lax.axis_index("c")]).astype(jnp.int32)

    def body(p_ref, s_ref, g_ref, *rest):
        acc = g_ref[...].astype(F32) + s_ref[0].astype(F32)
        for k in range(1, N_DEVICES - 1):
            acc = acc + s_ref[k].astype(F32)
        rest[-1][...] = acc

    def g_map(i, j, p):
        return (p[1] * (h0 // tr) + (p[0] * (r // tr) if axis == 0 else 0) + i, (p[0] * (c // tc) if axis == 1 else 0) + j)

    in_specs = [pl.BlockSpec((N_DEVICES - 1, tr, tc), lambda i, j, p: (0, i, j)), pl.BlockSpec((tr, tc), g_map)]
    args = [place, slots, grad]
    if dest is not None:
        in_specs.append(ANY)
        args.append(dest)
    return pl.pallas_call(
        body, name=name, out_shape=jax.ShapeDtypeStruct((n_layers, r, c), F32),
        grid_spec=pltpu.PrefetchScalarGridSpec(
            num_scalar_prefetch=1, grid=(h0 // tr, c // tc), in_specs=in_specs,
            out_specs=pl.BlockSpec((None, tr, tc), lambda i, j, p: (layer, p[1] * (h0 // tr) + i, j))),
        input_output_aliases={3: 0} if dest is not None else {},
        compiler_params=_params("parallel", "parallel"),
    )(*args)


def _swap_halves_task(blocks):
    n = len(blocks)
    layers = [(a, l) for a, b in enumerate(blocks) for l in range(b.shape[0])]

    def copies(in_refs, out_refs, sem):
        x, y, c = _my_place()
        sends, recvs = [], []
        for k, (a, l) in enumerate(layers):
            blk = blocks[a].shape[1:]
            mine = _half_of(out_refs[a].at[l], blk, c)
            sends.append(_remote(mine, mine, sem(k), (x, y, 1 - c)))
            recvs.append(_remote(mine, _half_of(out_refs[a].at[l], blk, 1 - c), sem(k), (x, y, 1 - c)))
        return sends, recvs

    return _Comm(blocks, [jax.ShapeDtypeStruct(b.shape, b.dtype) for b in blocks], {a: a for a in range(n)},
                 len(layers), copies)


def _place_shard(shard, axis, name):
    r, c = shard.shape
    tr = _tile(r, 512, SUBLANES_BF16)
    full = (r * N_SHARDS, c) if axis == 0 else (r, c * N_SHARDS)
    me2 = (2 * lax.axis_index("x") + lax.axis_index("y")).astype(jnp.int32).reshape(1)

    def body(me_ref, s_ref, o_ref):
        o_ref[...] = s_ref[...].astype(o_ref.dtype)

    if axis == 0:
        out_map = lambda i, me: (me[0] * (r // tr) + i, 0)
    else:
        out_map = lambda i, me: (i, me[0])
    return pl.pallas_call(
        body, name=name, out_shape=jax.ShapeDtypeStruct(full, BF16),
        grid_spec=pltpu.PrefetchScalarGridSpec(
            num_scalar_prefetch=1, grid=(r // tr,),
            in_specs=[pl.BlockSpec((tr, c), lambda i, me: (i, 0))], out_specs=pl.BlockSpec((tr, c), out_map)),
        compiler_params=_params("parallel"),
    )(me2, shard)


def _gather_ici_task(shards, axes, names):
    n = len(shards)
    blks = [s.shape for s in shards]
    bases = [_place_shard(s, ax, f"place_{nm}") for s, ax, nm in zip(shards, axes, names)]

    def copies(in_refs, out_refs, sem):
        x, y, c = _my_place()
        me = 2 * x + y
        sends, recvs = [], []
        for a in range(n):
            mine = _piece(out_refs[a], blks[a], axes[a], me, c)
            for k, (px, py) in enumerate(_other_chips(x, y)):
                sends.append(_remote(mine, mine, sem(3 * a + k), (px, py, c)))
                recvs.append(_remote(mine, _piece(out_refs[a], blks[a], axes[a], 2 * px + py, c), sem(3 * a + k),
                                     (px, py, c)))
        return sends, recvs

    return _Comm(bases, [jax.ShapeDtypeStruct(b.shape, b.dtype) for b in bases], {a: a for a in range(n)}, 3 * n, copies)


def _gather_d2d_task(partials, blks, axes):
    n = len(partials)

    def copies(in_refs, out_refs, sem):
        x, y, c = _my_place()
        sends, recvs = [], []
        for a in range(n):
            for k, (px, py) in enumerate(_other_chips(x, y)):
                mine = _piece(out_refs[a], blks[a], axes[a], 2 * px + py, c)
                theirs = _piece(out_refs[a], blks[a], axes[a], 2 * px + py, 1 - c)
                sends.append(_remote(mine, mine, sem(3 * a + k), (x, y, 1 - c)))
                recvs.append(_remote(mine, theirs, sem(3 * a + k), (x, y, 1 - c)))
        return sends, recvs

    return _Comm(partials, [jax.ShapeDtypeStruct(p.shape, p.dtype) for p in partials], {a: a for a in range(n)},
                 3 * n, copies)


def _scatter_task(grads, blks, axes):
    n = len(grads)

    def copies(in_refs, out_refs, sem):
        x, y, c = _my_place()
        sends, recvs = [], []
        for rel in range(1, N_DEVICES):
            px, py, pc = x ^ ((rel >> 2) & 1), y ^ ((rel >> 1) & 1), c ^ (rel & 1)
            for a in range(n):
                src = _piece(in_refs[a], blks[a], axes[a], 2 * px + py, pc)
                k = (N_DEVICES - 1) * a + rel - 1
                sends.append(_remote(src, out_refs[a].at[rel - 1], sem(k), (px, py, pc)))
                recvs.append(_remote(src, out_refs[a].at[rel - 1], sem(k), (px, py, pc)))
        return sends, recvs

    outs = [jax.ShapeDtypeStruct((N_DEVICES - 1, b[0] // 2) + tuple(b[1:]), g.dtype) for b, g in zip(blks, grads)]
    return _Comm(grads, outs, {}, (N_DEVICES - 1) * n, copies)


def _pack(arrays):
    flat = jnp.concatenate([a.reshape(-1).astype(F32) for a in arrays])
    pad = (-flat.shape[0]) % (SUBLANES_F32 * LANES)
    return jnp.pad(flat, (0, pad)).reshape(-1, LANES)


def _unpack(packed, shapes):
    flat = packed.reshape(-1)
    out, off = [], 0
    for s in shapes:
        n = int(np.prod(s))
        out.append(flat[off:off + n].reshape(s))
        off += n
    return out


def _local_step(x, target, W, shards, geom, small):
    W = dict(W)
    B, S, D = x.shape
    T = B * S
    x2 = x.reshape(T, D)
    tgt = target.reshape(T, D)
    bmap = jnp.asarray(_bucket_map())
    blk = lambda names: [geom[k][0] for k in names]
    axs = lambda names: [geom[k][1] for k in names]
    ici = lambda names: _gather_ici_task([shards[k] for k in names], axs(names), names)
    d2d = lambda names, partials: _gather_d2d_task(list(partials), blk(names), axs(names))
    big, slots = {}, {}

    def scatter(names):
        return _scatter_task([big[k] for k in names], blk(names), axs(names))

    def ffn_bwd(l, dout, h, saved, first, second):
        xn, u, v, a = saved
        da = _mm_nt(dout, W[f"w_down{l}"], BF16, f"ffn{l}_down_dx")
        big[f"w_down{l}"] = _mm_tn(a, dout, f"ffn{l}_down_dw")
        names = first + [f"w_down{l}"]
        du, g_cw, g_cb, landed = _ffn_gate_bwd(u, v, small["ffn_conv"][l], da, S, f"ffn{l}_gate_bwd", comm=scatter(names))
        slots.update(zip(names, landed))
        big[f"w_up{l}"] = _mm_tn(xn, du, f"ffn{l}_up_dw")
        norm = (h, small["ffn_norm"][l], dout)
        if second:
            (dh, g_norm), (slots[f"w_up{l}"],) = _mm_nt(du, W[f"w_up{l}"], None, f"ffn{l}_up_dx",
                                                        comm=scatter([f"w_up{l}"]), norm=norm)
        else:
            dh, g_norm = _mm_nt(du, W[f"w_up{l}"], None, f"ffn{l}_up_dx", norm=norm)
        return dh, g_cw, g_cb, g_norm[0]

    xn0 = _rmsnorm_fwd(x2, small["a_norm"][0], "a_norm")
    p, part = _mm_nn(xn0, W["w_in"], None, BF16, "a_in", comm=ici(["w_up0"]))
    z, (W["w_up0"],) = _gate_a_fwd(p, small["a_conv"][0], S, "a_gate", comm=d2d(["w_up0"], part))
    (h1, xn1), part = _mm_nn(z, W["w_out"], x2, F32, "a_out", comm=ici(["w_down0"]), norm_gains=[small["ffn_norm"][0]])
    later = ["w_kv", "w_q", "w_o", "w_up1"]
    u0, landed = _mm_nn(xn1, W["w_up0"], None, BF16, "ffn0_up", comm=_Comm.join([d2d(["w_down0"], part), ici(later)]))
    W["w_down0"] = landed[0]
    (a0, v0), landed = _ffn_gate_fwd(u0, small["ffn_conv"][0], small["ffn_conv_b"][0], S, "ffn0_gate",
                               comm=_Comm.join([d2d(later, landed[1:]), ici(["w_down1"])]))
    W.update(zip(later, landed[:len(later)]))
    (h2, kvn, xn3), (W["w_down1"],) = _mm_nn(a0, W["w_down0"], h1, F32, "ffn0_down",
                                            comm=d2d(["w_down1"], landed[len(later):]),
                                            norm_gains=[small["kv_norm"], small["b_norm"][0]])
    kv = _mm_nn(kvn, W["w_kv"], None, F32, "kv_proj")
    q = _mm_nn(xn3, W["w_q"], None, F32, "q_proj")
    bias = _bias_tables(small["rel_bias"], bmap, "rel_bias_tables")
    q3, kv3 = q.reshape(B, S, D), kv.reshape(B, S, 2 * D)
    o3, lse3 = _attn_fwd(q3, kv3, bias, "attn_fwd")
    o = o3.reshape(T, D)
    h3, xn4 = _mm_nn(o, W["w_o"], h2, F32, "o_proj", norm_gains=[small["ffn_norm"][1]])
    u1 = _mm_nn(xn4, W["w_up1"], None, BF16, "ffn1_up")
    a1, v1 = _ffn_gate_fwd(u1, small["ffn_conv"][1], small["ffn_conv_b"][1], S, "ffn1_gate")
    h4 = _mm_nn(a1, W["w_down1"], h3, F32, "ffn1_down")
    sq_err, dh4, g_final = _loss_head(h4, small["final_norm"], tgt, "loss_head")
    loss = 0.5 * jnp.sum(sq_err) / D

    dh3, g_cw1, g_cb1, g_fn1 = ffn_bwd(1, dh4, h3, (xn4, u1, v1, a1), [], False)
    do = _mm_nt(dh3, W["w_o"], F32, "o_proj_dx")
    big["w_o"] = _mm_tn(o, dh3, "o_proj_dw")
    (dq3, dk3, dv3, dbias), landed = _attn_bwd(q3, kv3, o3, lse3, do.reshape(B, S, D), bias, "attn_bwd",
                                               comm=scatter(["w_up1", "w_o"]))
    slots.update(zip(["w_up1", "w_o"], landed))
    g_rel = _bias_grad(dbias, bmap, "rel_bias_grad")[:, :REL_BUCKETS].T
    dq, dk, dv = dq3.reshape(T, D), dk3.reshape(T, D), dv3.reshape(T, D)
    dh2, g_bn = _mm_nt(dq, W["w_q"], None, "q_proj_dx", norm=(h2, small["b_norm"][0], dh3))
    big["w_q"] = _mm_tn(xn3, dq, "q_proj_dw")
    dh2, g_kvn = _mm_nt([dk, dv], W["w_kv"], None, "kv_proj_dx", norm=(h2, small["kv_norm"], dh2))
    big["w_kv"] = jnp.concatenate([_mm_tn(kvn, dk, "k_proj_dw"), _mm_tn(kvn, dv, "v_proj_dw")], axis=1)
    dh1, g_cw0, g_cb0, g_fn0 = ffn_bwd(0, dh2, h1, (xn1, u0, v0, a0), ["w_q", "w_kv"], True)
    dz = _mm_nt(dh1, W["w_out"], BF16, "a_out_dx")
    big["w_out"] = _mm_tn(z, dh1, "a_out_dw")
    dp, g_aconv, (slots["w_out"],) = _gate_a_bwd(p, small["a_conv"][0], dz, S, "a_gate_bwd", comm=scatter(["w_out"]))
    big["w_in"] = _mm_tn(xn0, dp, "a_in_dw")
    (dx, g_an), (slots["w_in"],) = _mm_nt(dp, W["w_in"], None, "a_in_dx", comm=scatter(["w_in"]),
                                          norm=(x2, small["a_norm"][0], dh1))
    g_bn, g_kvn, g_an = g_bn[0], g_kvn[0], g_an[0]

    small_g = {"a_norm": g_an[None], "a_conv": g_aconv[None], "kv_norm": g_kvn, "b_norm": g_bn[None],
               "rel_bias": g_rel, "ffn_norm": jnp.stack([g_fn0, g_fn1]), "ffn_conv": jnp.stack([g_cw0, g_cw1]),
               "ffn_conv_b": jnp.stack([g_cb0, g_cb1]), "final_norm": g_final}
    return loss, dx.reshape(B, S, D), big, slots, small_g


BIG = ("w_in", "w_out", "w_kv", "w_q", "w_o", "w_up0", "w_up1", "w_down0", "w_down1")
SMALL = ("a_norm", "a_conv", "kv_norm", "b_norm", "rel_bias", "ffn_norm", "ffn_conv", "ffn_conv_b", "final_norm")
SMALL_SHARDED = ("a_norm", "a_conv", "ffn_conv")
WEIGHT_ORDER = ("a_norm", "a_w_in", "a_conv", "a_w_out", "kv_norm", "w_kv", "b_norm", "b_w_q", "b_w_o", "rel_bias",
                "ffn_norm", "ffn_w_up", "ffn_conv", "ffn_conv_b", "ffn_w_down", "final_norm")
GRAD_OF = {"w_in": ("a_w_in", 0), "w_out": ("a_w_out", 0), "w_kv": ("w_kv", 0), "w_q": ("b_w_q", 0), "w_o": ("b_w_o", 0),
           "w_up0": ("ffn_w_up", 0), "w_up1": ("ffn_w_up", 1), "w_down0": ("ffn_w_down", 0), "w_down1": ("ffn_w_down", 1)}


def _as2d(a):
    return a.reshape(-1, a.shape[-1])


def kernel(x, a_norm, a_w_in, a_conv, a_w_out, kv_norm, w_kv, b_norm, b_w_q, b_w_o, rel_bias, ffn_norm, ffn_w_up, ffn_conv, ffn_conv_b, ffn_w_down, final_norm, loss_target, m_a_norm, m_a_w_in, m_a_conv, m_a_w_out, m_kv_norm, m_w_kv, m_b_norm, m_b_w_q, m_b_w_o, m_rel_bias, m_ffn_norm, m_ffn_w_up, m_ffn_conv, m_ffn_conv_b, m_ffn_w_down, m_final_norm, v_a_norm, v_a_w_in, v_a_conv, v_a_w_out, v_kv_norm, v_w_kv, v_b_norm, v_b_w_q, v_b_w_o, v_rel_bias, v_ffn_norm, v_ffn_w_up, v_ffn_conv, v_ffn_conv_b, v_ffn_w_down, v_final_norm):
    given = dict(a_norm=a_norm, a_w_in=a_w_in, a_conv=a_conv, a_w_out=a_w_out, kv_norm=kv_norm, w_kv=w_kv, b_norm=b_norm,
                 b_w_q=b_w_q, b_w_o=b_w_o, rel_bias=rel_bias, ffn_norm=ffn_norm, ffn_w_up=ffn_w_up, ffn_conv=ffn_conv,
                 ffn_conv_b=ffn_conv_b, ffn_w_down=ffn_w_down, final_norm=final_norm)
    mom_m = dict(a_norm=m_a_norm, a_w_in=m_a_w_in, a_conv=m_a_conv, a_w_out=m_a_w_out, kv_norm=m_kv_norm, w_kv=m_w_kv,
                 b_norm=m_b_norm, b_w_q=m_b_w_q, b_w_o=m_b_w_o, rel_bias=m_rel_bias, ffn_norm=m_ffn_norm,
                 ffn_w_up=m_ffn_w_up, ffn_conv=m_ffn_conv, ffn_conv_b=m_ffn_conv_b, ffn_w_down=m_ffn_w_down,
                 final_norm=m_final_norm)
    mom_v = dict(a_norm=v_a_norm, a_w_in=v_a_w_in, a_conv=v_a_conv, a_w_out=v_a_w_out, kv_norm=v_kv_norm, w_kv=v_w_kv,
                 b_norm=v_b_norm, b_w_q=v_b_w_q, b_w_o=v_b_w_o, rel_bias=v_rel_bias, ffn_norm=v_ffn_norm,
                 ffn_w_up=v_ffn_w_up, ffn_conv=v_ffn_conv, ffn_conv_b=v_ffn_conv_b, ffn_w_down=v_ffn_w_down,
                 final_norm=v_final_norm)

    shard = {"w_in": (a_w_in[0], 1), "w_out": (a_w_out[0], 0), "w_kv": (w_kv, 1), "w_q": (b_w_q[0], 0),
             "w_o": (b_w_o[0], 0), "w_up0": (ffn_w_up[0], 1), "w_up1": (ffn_w_up[1], 1),
             "w_down0": (ffn_w_down[0], 0), "w_down1": (ffn_w_down[1], 0)}

    small_sharded = [given[k] for k in SMALL_SHARDED]
    packed = _pack(small_sharded)
    first = ("w_in", "w_out")
    fulls, packed_all = _gather_weights([shard[k][0] for k in first], [shard[k][1] for k in first], first, packed)
    W = dict(zip(first, fulls))
    later = {k: shard[k][0] for k in BIG if k not in first}
    geom = {k: (shard[k][0].shape, shard[k][1]) for k in BIG}
    small = {k: given[k] for k in SMALL}
    per_shard = [_unpack(packed_all[j], [a.shape for a in small_sharded]) for j in range(N_SHARDS)]
    for i, k in enumerate(SMALL_SHARDED):
        small[k] = jnp.concatenate([per_shard[j][i] for j in range(N_SHARDS)], axis=-1)

    loss, grad_x, big_g, slots, small_g = _local_step(x, loss_target, W, later, geom, small)
    loss = lax.psum(loss, MESH_AXES)

    small_shapes = [small_g[k].shape for k in SMALL]
    layers_of = {}
    for k in BIG:
        layers_of.setdefault(GRAD_OF[k][0], []).append(k)
    blocks = {}
    for name, members in layers_of.items():
        dest = None
        for k in members:
            dest = _sum_piece(dest, slots[k], big_g[k], geom[k][0], geom[k][1], GRAD_OF[k][1], len(members), f"sum_{k}")
        blocks[name] = dest
    swapped, small_slots = _exchange_small(_pack([small_g[k] for k in SMALL]), _swap_halves_task(list(blocks.values())),
                                           "swap_halves_exchange_small")
    reduced = dict(zip(blocks, swapped))
    small_sum = _sum_slots(small_slots, "sum_small")
    small_red = dict(zip(SMALL, _unpack(small_sum, small_shapes)))
    j = 2 * lax.axis_index("x") + lax.axis_index("y")
    for k in SMALL_SHARDED:
        w = given[k].shape[-1]
        small_red[k] = lax.dynamic_slice_in_dim(small_red[k], j * w, w, axis=small_red[k].ndim - 1)

    grads, deltas, new_m, new_v = {}, {}, {}, {}
    for name in WEIGHT_ORDER:
        if name in reduced:
            g = reduced[name].reshape(given[name].shape)
            d, nm, nv = _adamw(_as2d(given[name]), _as2d(g), _as2d(mom_m[name]), _as2d(mom_v[name]), f"adamw_{name}")
            grads[name] = g
            deltas[name], new_m[name], new_v[name] = (t.reshape(given[name].shape) for t in (d, nm, nv))
    small_names = [n for n in WEIGHT_ORDER if n not in reduced]
    for n in small_names:
        grads[n] = small_red[n].reshape(given[n].shape)
    sw, sg, sm, sv = (_pack([d[n] for n in small_names]) for d in (given, grads, mom_m, mom_v))
    d, nm, nv = _adamw(sw, sg, sm, sv, "adamw_small")
    shapes = [given[n].shape for n in small_names]
    for n, a, b_, c_ in zip(small_names, _unpack(d, shapes), _unpack(nm, shapes), _unpack(nv, shapes)):
        deltas[n], new_m[n], new_v[n] = a, b_, c_

    return (loss, grad_x, *[grads[n] for n in WEIGHT_ORDER], *[deltas[n] for n in WEIGHT_ORDER],
            *[new_m[n] for n in WEIGHT_ORDER], *[new_v[n] for n in WEIGHT_ORDER])
```

```python
import functools
import math

import numpy as np

import jax
import jax.numpy as jnp
from jax import lax
from jax.experimental import pallas as pl
from jax.experimental.pallas import tpu as pltpu

F32 = jnp.float32
BF16 = jnp.bfloat16

RMS_EPS = 1e-6
HEAD_DIM = 64
ATT_BLOCK = 128
DILATED_BRANCHES = ((128, 1), (512, 4), (2048, 16))
REL_BUCKETS = 32
REL_MAX_DISTANCE = 2048
MASKED_LOGIT = -1e30
ATTN_FWD_UNROLL = (8, 8, 4)
ATTN_BWD_UNROLL = (8, 8, 8)

ADAM_LR = 0.001
ADAM_B1 = 0.9
ADAM_B2 = 0.999
ADAM_EPS = 1e-08
ADAM_WD = 0.01
ADAM_STEP = 10

LANES = 128
SUBLANES_F32 = 8
SUBLANES_BF16 = 16
VMEM_LIMIT_BYTES = 56 * 1024 * 1024

MESH_AXES = ("x", "y", "c")
N_SHARDS = 4
N_DEVICES = 8
ANY = pl.BlockSpec(memory_space=pl.ANY)


def _tile(n, pref, mult):
    best = None
    for t in range(mult, min(n, pref) + 1, mult):
        if n % t == 0:
            best = t
    if best is None:
        raise ValueError(f"no tile for {n} (multiple of {mult}, at most {pref})")
    return best


def _params(*sem):
    return pltpu.CompilerParams(dimension_semantics=sem, vmem_limit_bytes=VMEM_LIMIT_BYTES)


class _Comm:
    def __init__(self, ins, outs, aliases, n_sems, copies):
        self.ins, self.outs, self.aliases, self.n_sems, self.copies = list(ins), list(outs), dict(aliases), n_sems, copies

    @staticmethod
    def join(parts):
        parts = [p for p in parts if p is not None]
        ins, outs, aliases, offs, n_sems = [], [], {}, [], 0
        for p in parts:
            offs.append((len(ins), len(outs), n_sems))
            aliases.update({len(ins) + i: len(outs) + o for i, o in p.aliases.items()})
            ins += p.ins
            outs += p.outs
            n_sems += p.n_sems

        def copies(in_refs, out_refs, sem):
            sends, recvs = [], []
            for p, (i0, o0, s0) in zip(parts, offs):
                s, r = p.copies(in_refs[i0:i0 + len(p.ins)], out_refs[o0:o0 + len(p.outs)],
                                lambda k, s0=s0: sem(s0 + k))
                sends += s
                recvs += r
            return sends, recvs

        return _Comm(ins, outs, aliases, n_sems, copies)


def _pallas(body, *, comm=None, name, out_shape, grid=(), in_specs=(), out_specs=(), scratch_shapes=(),
            compiler_params=None):
    if comm is None:
        return pl.pallas_call(body, name=name, out_shape=out_shape, grid=grid, in_specs=in_specs, out_specs=out_specs,
                              scratch_shapes=scratch_shapes, compiler_params=compiler_params)
    single = not isinstance(out_shape, (tuple, list))
    outs = (out_shape,) if single else tuple(out_shape)
    o_specs = (out_specs,) if single else tuple(out_specs)
    n_in, n_cin, n_out, n_cout, n_scr = len(in_specs), len(comm.ins), len(outs), len(comm.outs), len(scratch_shapes)

    def carried(*refs):
        base_in, c_in = refs[:n_in], refs[n_in:n_in + n_cin]
        o0 = n_in + n_cin
        base_out, c_out = refs[o0:o0 + n_out], refs[o0 + n_out:o0 + n_out + n_cout]
        s0 = o0 + n_out + n_cout
        base_scr, (send_sems, recv_sems) = refs[s0:s0 + n_scr], refs[s0 + n_scr:]
        sem = lambda k: (send_sems.at[k], recv_sems.at[k])
        first = functools.reduce(jnp.logical_and, [pl.program_id(a) == 0 for a in range(len(grid))])
        last = functools.reduce(jnp.logical_and, [pl.program_id(a) == n - 1 for a, n in enumerate(grid)])

        @pl.when(first)
        def _():
            for cp in comm.copies(c_in, c_out, sem)[0]:
                cp.start()

        body(*base_in, *base_out, *base_scr)

        @pl.when(last)
        def _():
            sends, recvs = comm.copies(c_in, c_out, sem)
            for cp in recvs:
                cp.wait_recv()
            for cp in sends:
                cp.wait_send()

    call = pl.pallas_call(
        carried, name=name, out_shape=outs + tuple(comm.outs), grid=grid,
        in_specs=list(in_specs) + [ANY] * n_cin, out_specs=o_specs + (ANY,) * n_cout,
        scratch_shapes=list(scratch_shapes) + [pltpu.SemaphoreType.DMA((comm.n_sems,))] * 2,
        input_output_aliases={n_in + i: n_out + o for i, o in comm.aliases.items()},
        compiler_params=_params(*(["arbitrary"] * len(grid))))

    def run(*args):
        res = call(*args, *comm.ins)
        base = res[0] if single else tuple(res[:n_out])
        return base, list(res[n_out:])

    return run


def _rmsnorm_fwd(x, g, name):
    T, D = x.shape
    tm = _tile(T, 512, SUBLANES_BF16)

    def body(x_ref, g_ref, o_ref):
        xf = x_ref[...]
        r = lax.rsqrt(jnp.mean(xf * xf, axis=-1, keepdims=True) + RMS_EPS)
        o_ref[...] = ((xf * r) * g_ref[...]).astype(o_ref.dtype)

    return pl.pallas_call(
        body, name=name, out_shape=jax.ShapeDtypeStruct((T, D), BF16), grid=(T // tm,),
        in_specs=[pl.BlockSpec((tm, D), lambda i: (i, 0)), pl.BlockSpec((1, D), lambda i: (0, 0))],
        out_specs=pl.BlockSpec((tm, D), lambda i: (i, 0)),
        compiler_params=_params("parallel"),
    )(x, g.reshape(1, D))


def _loss_head(h, g, target, name):
    T, D = h.shape
    tm = _tile(T, 512, SUBLANES_F32)

    def body(h_ref, g_ref, t_ref, dh_ref, acc_ref):
        @pl.when(pl.program_id(0) == 0)
        def _():
            acc_ref[...] = jnp.zeros_like(acc_ref)

        xf = h_ref[...]
        r = lax.rsqrt(jnp.mean(xf * xf, axis=-1, keepdims=True) + RMS_EPS)
        xhat = xf * r
        err = xhat * g_ref[...] - t_ref[...]
        dy = err * (1.0 / D)
        acc_ref[0:1, :] += jnp.sum(dy * xhat, axis=0, keepdims=True)
        acc_ref[1:2, :] += jnp.sum(err * err, axis=0, keepdims=True)
        t = dy * g_ref[...]
        dh_ref[...] = r * (t - xhat * jnp.mean(t * xhat, axis=-1, keepdims=True))

    row = pl.BlockSpec((tm, D), lambda i: (i, 0))
    dh, acc = pl.pallas_call(
        body, name=name,
        out_shape=(jax.ShapeDtypeStruct((T, D), F32), jax.ShapeDtypeStruct((SUBLANES_F32, D), F32)),
        grid=(T // tm,),
        in_specs=[row, pl.BlockSpec((1, D), lambda i: (0, 0)), row],
        out_specs=(row, pl.BlockSpec((SUBLANES_F32, D), lambda i: (0, 0))),
        compiler_params=_params("arbitrary"),
    )(h, g.reshape(1, D), target)
    return acc[1], dh, acc[0]


def _mm_nn(a, b, res, out_dtype, name, comm=None, norm_gains=()):
    T, K = a.shape
    N = b.shape[1]
    n_g = len(norm_gains)
    tm = _tile(T, 512 if n_g else 1024, SUBLANES_BF16)
    tn = _tile(N, 3072, LANES)
    assert not n_g or tn == N, "the fused rmsnorm needs whole rows in one tile"

    def body(a_ref, b_ref, *rest):
        ins, outs = rest[:len(rest) - 1 - n_g], rest[len(rest) - 1 - n_g:]
        acc = jnp.dot(a_ref[...].astype(BF16), b_ref[...], preferred_element_type=F32)
        if res is not None:
            acc = acc + ins[0][...]
        outs[0][...] = acc.astype(outs[0].dtype)
        if n_g:
            y = acc * lax.rsqrt(jnp.mean(acc * acc, axis=-1, keepdims=True) + RMS_EPS)
            for g_ref, xn_ref in zip(ins[len(ins) - n_g:], outs[1:]):
                xn_ref[...] = (y * g_ref[...]).astype(xn_ref.dtype)

    tile = pl.BlockSpec((tm, tn), lambda j, i: (i, j))
    in_specs = [pl.BlockSpec((tm, K), lambda j, i: (i, 0)), pl.BlockSpec((K, tn), lambda j, i: (0, j))]
    args = [a, b]
    if res is not None:
        in_specs.append(tile)
        args.append(res)
    for g in norm_gains:
        in_specs.append(pl.BlockSpec((1, N), lambda j, i: (0, 0)))
        args.append(g.reshape(1, N))
    out_shape = jax.ShapeDtypeStruct((T, N), out_dtype)
    if n_g:
        out_shape = (out_shape,) + (jax.ShapeDtypeStruct((T, N), BF16),) * n_g
    return _pallas(
        body, comm=comm, name=name, out_shape=out_shape, grid=(N // tn, T // tm),
        in_specs=in_specs, out_specs=(tile,) * (1 + n_g) if n_g else tile,
        compiler_params=_params("parallel", "parallel"),
    )(*args)


def _mm_nt(dy, b, out_dtype, name, comm=None, norm=None):
    dys = list(dy) if isinstance(dy, (list, tuple)) else [dy]
    T, n_each = dys[0].shape
    K = b.shape[0]
    tm = _tile(T, 2048 if norm is None else 1024, SUBLANES_BF16)
    tk = _tile(K, 1536, LANES)
    tn = _tile(n_each, 2816 if norm is None else 1536, LANES)
    per = n_each // tn
    n_steps = per * len(dys)
    assert norm is None or tk == K, "the fused rmsnorm backward needs whole rows in one tile"

    def body(*refs):
        dy_refs, b_ref, acc_ref = refs[:len(dys)], refs[len(dys)], refs[-1]
        i, n = pl.program_id(0), pl.program_id(2)

        @pl.when(n == 0)
        def _():
            acc_ref[...] = jnp.zeros_like(acc_ref)

        for p, dy_ref in enumerate(dy_refs):
            @pl.when(jnp.logical_and(n >= p * per, n < (p + 1) * per))
            def _(dy_ref=dy_ref):
                acc_ref[...] += lax.dot_general(dy_ref[...].astype(BF16), b_ref[...], (((1,), (1,)), ((), ())),
                                                preferred_element_type=F32)

        if norm is None:
            @pl.when(n == n_steps - 1)
            def _():
                refs[-2][...] = acc_ref[...].astype(refs[-2].dtype)
        else:
            x_ref, g_ref, dres_ref, dx_ref, dg_ref = refs[len(dys) + 1:-1]

            @pl.when(jnp.logical_and(i == 0, n == 0))
            def _():
                dg_ref[...] = jnp.zeros_like(dg_ref)

            @pl.when(n == n_steps - 1)
            def _():
                xf = x_ref[...]
                r = lax.rsqrt(jnp.mean(xf * xf, axis=-1, keepdims=True) + RMS_EPS)
                xhat = xf * r
                d = acc_ref[...]
                dg_ref[0:1, :] += jnp.sum(d * xhat, axis=0, keepdims=True)
                t = d * g_ref[...]
                dx_ref[...] = dres_ref[...] + r * (t - xhat * jnp.mean(t * xhat, axis=-1, keepdims=True))

    in_specs = [pl.BlockSpec((tm, tn), lambda i, k, n, p=p: (i, jnp.clip(n - p * per, 0, per - 1))) for p in range(len(dys))]
    in_specs.append(pl.BlockSpec((tk, tn), lambda i, k, n: (k, n)))
    args = dys + [b]
    tile = pl.BlockSpec((tm, tk), lambda i, k, n: (i, k))
    if norm is None:
        out_shape, out_specs = jax.ShapeDtypeStruct((T, K), out_dtype), tile
    else:
        x, g, dres = norm
        in_specs += [tile, pl.BlockSpec((1, K), lambda i, k, n: (0, 0)), tile]
        args += [x, g.reshape(1, K), dres]
        out_shape = (jax.ShapeDtypeStruct((T, K), F32), jax.ShapeDtypeStruct((SUBLANES_F32, K), F32))
        out_specs = (tile, pl.BlockSpec((SUBLANES_F32, K), lambda i, k, n: (0, 0)))
    return _pallas(
        body, comm=comm, name=name, out_shape=out_shape, grid=(T // tm, K // tk, n_steps),
        in_specs=in_specs, out_specs=out_specs, scratch_shapes=[pltpu.VMEM((tm, tk), F32)],
        compiler_params=_params("parallel", "parallel", "arbitrary") if norm is None else _params(*["arbitrary"] * 3),
    )(*args)


def _mm_tn(a, dy, name):
    T, K = a.shape
    N = dy.shape[1]
    tt = _tile(T, 2048, SUBLANES_BF16)
    tk = _tile(K, 1536, LANES)
    tn = _tile(N, 1536, LANES)
    t_steps = T // tt

    def body(a_ref, dy_ref, o_ref, acc_ref):
        t = pl.program_id(2)

        @pl.when(t == 0)
        def _():
            acc_ref[...] = jnp.zeros_like(acc_ref)

        acc_ref[...] += lax.dot_general(a_ref[...].astype(BF16), dy_ref[...].astype(BF16),
                                        (((0,), (0,)), ((), ())), preferred_element_type=F32)

        @pl.when(t == t_steps - 1)
        def _():
            o_ref[...] = acc_ref[...].astype(o_ref.dtype)

    return pl.pallas_call(
        body, name=name, out_shape=jax.ShapeDtypeStruct((K, N), BF16), grid=(K // tk, N // tn, t_steps),
        in_specs=[pl.BlockSpec((tt, tk), lambda k, n, t: (t, k)), pl.BlockSpec((tt, tn), lambda k, n, t: (t, n))],
        out_specs=pl.BlockSpec((tk, tn), lambda k, n, t: (k, n)),
        scratch_shapes=[pltpu.VMEM((tk, tn), F32)],
        compiler_params=_params("parallel", "parallel", "arbitrary"),
    )(a, dy)


def _rows_before(halo, cur, k):
    h = halo.shape[0]
    return pltpu.roll(jnp.concatenate([halo, cur], axis=0), k, 0)[h:]


def _rows_after(cur, halo, k):
    n = cur.shape[0]
    total = n + halo.shape[0]
    return pltpu.roll(jnp.concatenate([cur, halo], axis=0), total - k, 0)[:n]


def _halo_specs(tm, width, n_rows):
    per = tm // SUBLANES_BF16
    last = n_rows // SUBLANES_BF16 - 1
    prev = pl.BlockSpec((SUBLANES_BF16, width), lambda i: (jnp.maximum(i * per - 1, 0), 0))
    nxt = pl.BlockSpec((SUBLANES_BF16, width), lambda i: (jnp.minimum((i + 1) * per, last), 0))
    return prev, nxt


def _gate_a_fwd(p, cw, seq, name, comm=None):
    T, D3 = p.shape
    D = D3 // 3
    tm = _tile(seq, 512, SUBLANES_BF16)
    cc = _tile(D, 256, LANES)
    prev, _ = _halo_specs(tm, D3, T)

    def body(p_ref, ph_ref, cw_ref, z_ref):
        at_start = (pl.program_id(0) * tm) % seq == 0
        for c0 in range(0, D, cc):
            b = p_ref[:, c0:c0 + cc].astype(F32)
            u = p_ref[:, D + c0:D + c0 + cc].astype(F32) * p_ref[:, 2 * D + c0:2 * D + c0 + cc].astype(F32)
            uh = ph_ref[:, D + c0:D + c0 + cc].astype(F32) * ph_ref[:, 2 * D + c0:2 * D + c0 + cc].astype(F32)
            uh = jnp.where(at_start, 0.0, uh)
            w = cw_ref[:, c0:c0 + cc]
            cv = _rows_before(uh, u, 2) * w[0:1] + _rows_before(uh, u, 1) * w[1:2] + u * w[2:3]
            z_ref[:, c0:c0 + cc] = (b * cv).astype(z_ref.dtype)

    return _pallas(
        body, comm=comm, name=name, out_shape=jax.ShapeDtypeStruct((T, D), BF16), grid=(T // tm,),
        in_specs=[pl.BlockSpec((tm, D3), lambda i: (i, 0)), prev, pl.BlockSpec((3, D), lambda i: (0, 0))],
        out_specs=pl.BlockSpec((tm, D), lambda i: (i, 0)),
        compiler_params=_params("parallel"),
    )(p, p, cw)


def _gate_a_bwd(p, cw, dz, seq, name, comm=None):
    T, D3 = p.shape
    D = D3 // 3
    tm = _tile(seq, 512, SUBLANES_BF16)
    cc = _tile(D, 256, LANES)
    p_prev, p_next = _halo_specs(tm, D3, T)
    _, dz_next = _halo_specs(tm, D, T)

    def body(p_ref, pp_ref, pn_ref, dz_ref, dzn_ref, cw_ref, dp_ref, dcw_ref):
        i = pl.program_id(0)

        @pl.when(i == 0)
        def _():
            dcw_ref[...] = jnp.zeros_like(dcw_ref)

        at_start = (i * tm) % seq == 0
        at_end = ((i + 1) * tm) % seq == 0
        for c0 in range(0, D, cc):
            cb, cc_, ch = slice(c0, c0 + cc), slice(D + c0, D + c0 + cc), slice(2 * D + c0, 2 * D + c0 + cc)
            b = p_ref[:, cb].astype(F32)
            c = p_ref[:, cc_].astype(F32)
            hh = p_ref[:, ch].astype(F32)
            u = c * hh
            uh = jnp.where(at_start, 0.0, pp_ref[:, cc_].astype(F32) * pp_ref[:, ch].astype(F32))
            w = cw_ref[:, cb]
            u1 = _rows_before(uh, u, 1)
            u2 = _rows_before(uh, u, 2)
            cv = u2 * w[0:1] + u1 * w[1:2] + u * w[2:3]
            dz_t = dz_ref[:, cb].astype(F32)
            dcv = dz_t * b
            dcvn = jnp.where(at_end, 0.0, dzn_ref[:, cb].astype(F32) * pn_ref[:, cb].astype(F32))
            du = dcv * w[2:3] + _rows_after(dcv, dcvn, 1) * w[1:2] + _rows_after(dcv, dcvn, 2) * w[0:1]
            dp_ref[:, cb] = (dz_t * cv).astype(dp_ref.dtype)
            dp_ref[:, cc_] = (du * hh).astype(dp_ref.dtype)
            dp_ref[:, ch] = (du * c).astype(dp_ref.dtype)
            dcw_ref[0:1, cb] += jnp.sum(dcv * u2, axis=0, keepdims=True)
            dcw_ref[1:2, cb] += jnp.sum(dcv * u1, axis=0, keepdims=True)
            dcw_ref[2:3, cb] += jnp.sum(dcv * u, axis=0, keepdims=True)

    res = _pallas(
        body, comm=comm, name=name,
        out_shape=(jax.ShapeDtypeStruct((T, D3), BF16), jax.ShapeDtypeStruct((SUBLANES_F32, D), F32)),
        grid=(T // tm,),
        in_specs=[pl.BlockSpec((tm, D3), lambda i: (i, 0)), p_prev, p_next,
                  pl.BlockSpec((tm, D), lambda i: (i, 0)), dz_next, pl.BlockSpec((3, D), lambda i: (0, 0))],
        out_specs=(pl.BlockSpec((tm, D3), lambda i: (i, 0)), pl.BlockSpec((SUBLANES_F32, D), lambda i: (0, 0))),
        compiler_params=_params("arbitrary"),
    )(p, p, p, dz, dz, cw)
    if comm is None:
        return res[0], res[1][0:3]
    return res[0][0], res[0][1][0:3], res[1]


def _ffn_gate_fwd(u, cw, cb, seq, name, comm=None):
    T, F2 = u.shape
    F = F2 // 2
    tm = _tile(seq, 256, SUBLANES_BF16)
    cc = _tile(F, 256, LANES)
    prev, _ = _halo_specs(tm, F2, T)

    def body(u_ref, uh_ref, cw_ref, cb_ref, a_ref, v_ref):
        at_start = (pl.program_id(0) * tm) % seq == 0

        def conv(c0):
            cols = slice(c0, c0 + cc)
            cur = u_ref[:, cols].astype(F32)
            halo = jnp.where(at_start, 0.0, uh_ref[:, cols].astype(F32))
            w = cw_ref[:, cols]
            return (_rows_before(halo, cur, 2) * w[0:1] + _rows_before(halo, cur, 1) * w[1:2] + cur * w[2:3]
                    + cb_ref[:, cols])

        for c0 in range(0, F, cc):
            g = conv(c0)
            up = conv(F + c0)
            a_ref[:, c0:c0 + cc] = ((g * jax.nn.sigmoid(g)) * up).astype(a_ref.dtype)
            v_ref[:, c0:c0 + cc] = g.astype(v_ref.dtype)
            v_ref[:, F + c0:F + c0 + cc] = up.astype(v_ref.dtype)

    return _pallas(
        body, comm=comm, name=name,
        out_shape=(jax.ShapeDtypeStruct((T, F), BF16), jax.ShapeDtypeStruct((T, F2), BF16)), grid=(T // tm,),
        in_specs=[pl.BlockSpec((tm, F2), lambda i: (i, 0)), prev,
                  pl.BlockSpec((3, F2), lambda i: (0, 0)), pl.BlockSpec((1, F2), lambda i: (0, 0))],
        out_specs=(pl.BlockSpec((tm, F), lambda i: (i, 0)), pl.BlockSpec((tm, F2), lambda i: (i, 0))),
        compiler_params=_params("parallel"),
    )(u, u, cw, cb.reshape(1, F2))


def _ffn_gate_bwd(u, v, cw, da, seq, name, comm=None):
    T, F2 = u.shape
    F = F2 // 2
    H = SUBLANES_BF16
    tm = _tile(seq, 256, H)
    cc = _tile(F, 256, LANES)
    _, v_next = _halo_specs(tm, F2, T)
    _, da_next = _halo_specs(tm, F, T)

    def body(u_ref, v_ref, vn_ref, da_ref, dan_ref, cw_ref, du_ref, acc_ref):
        i = pl.program_id(0)

        @pl.when(i == 0)
        def _():
            acc_ref[...] = jnp.zeros_like(acc_ref)

        at_end = ((i + 1) * tm) % seq == 0
        n = tm + H

        def rows_and_next(ref, nxt, cols):
            return jnp.concatenate([ref[:, cols].astype(F32), nxt[:, cols].astype(F32)], axis=0)

        def back(d, cols):
            w = cw_ref[:, cols]
            d0 = d[:tm]
            d1 = pltpu.roll(d, n - 1, 0)[:tm]
            d2 = pltpu.roll(d, n - 2, 0)[:tm]
            du_ref[:, cols] = (d0 * w[2:3] + d1 * w[1:2] + d2 * w[0:1]).astype(du_ref.dtype)
            ut = u_ref[:, cols].astype(F32)
            acc_ref[0:1, cols] += jnp.sum(d2 * ut, axis=0, keepdims=True)
            acc_ref[1:2, cols] += jnp.sum(d1 * ut, axis=0, keepdims=True)
            acc_ref[2:3, cols] += jnp.sum(d0 * ut, axis=0, keepdims=True)
            acc_ref[3:4, cols] += jnp.sum(d0, axis=0, keepdims=True)

        for c0 in range(0, F, cc):
            gc, uc = slice(c0, c0 + cc), slice(F + c0, F + c0 + cc)
            g = rows_and_next(v_ref, vn_ref, gc)
            up = rows_and_next(v_ref, vn_ref, uc)
            da_ext = jnp.concatenate([da_ref[:, gc].astype(F32),
                                      jnp.where(at_end, 0.0, dan_ref[:, gc].astype(F32))], axis=0)
            sg = jax.nn.sigmoid(g)
            back(da_ext * up * (sg * (1.0 + g * (1.0 - sg))), gc)
            back(da_ext * (g * sg), uc)

    res = _pallas(
        body, comm=comm, name=name,
        out_shape=(jax.ShapeDtypeStruct((T, F2), BF16), jax.ShapeDtypeStruct((SUBLANES_F32, F2), F32)),
        grid=(T // tm,),
        in_specs=[pl.BlockSpec((tm, F2), lambda i: (i, 0)), pl.BlockSpec((tm, F2), lambda i: (i, 0)), v_next,
                  pl.BlockSpec((tm, F), lambda i: (i, 0)), da_next, pl.BlockSpec((3, F2), lambda i: (0, 0))],
        out_specs=(pl.BlockSpec((tm, F2), lambda i: (i, 0)), pl.BlockSpec((SUBLANES_F32, F2), lambda i: (0, 0))),
        compiler_params=_params("arbitrary"),
    )(u, v, v, da, da, cw)
    (du, acc), landed = res if comm is not None else (res, None)
    return (du, acc[0:3], acc[3]) if comm is None else (du, acc[0:3], acc[3], landed)


def _bucket_map():
    P = ATT_BLOCK
    qi = np.arange(P, dtype=np.int64)[:, None]
    kc = np.arange(2 * P, dtype=np.int64)[None, :]
    delta = qi + P - kc
    maps = []
    max_exact = REL_BUCKETS // 2
    for window, dilation in DILATED_BRANCHES:
        band = (delta >= 0) & (delta <= window // dilation)
        n = np.maximum(delta * dilation, 0)
        nf = np.maximum(n, max_exact).astype(np.float32)
        large = max_exact + (np.log(nf / np.float32(max_exact)) / np.float32(math.log(REL_MAX_DISTANCE / max_exact))
                             * np.float32(REL_BUCKETS - max_exact)).astype(np.int32)
        large = np.minimum(large, REL_BUCKETS - 1)
        bucket = np.where(n < max_exact, n, large)
        maps.append(np.where(band, bucket, -1).astype(np.int32))
    return np.stack(maps)


def _bias_tables(rel_bias, bmap, name):
    n_pairs = rel_bias.shape[1] // 2
    nbr, P, P2 = bmap.shape

    def body(rb_ref, bm_ref, o_ref):
        pair = pl.program_id(0)
        in_seq = lax.broadcasted_iota(jnp.int32, (P, P2), 1) >= P
        for br in range(nbr):
            bm = bm_ref[br]
            for hh in range(2):
                acc = jnp.full((P, P2), MASKED_LOGIT, F32)
                for b in range(REL_BUCKETS):
                    acc = jnp.where(bm == b, rb_ref[b, 2 * pair + hh], acc)
                o_ref[br, 0, 0, hh * P:(hh + 1) * P, :] = acc
                o_ref[br, 0, 1, hh * P:(hh + 1) * P, :] = jnp.where(in_seq, acc, MASKED_LOGIT)

    return pl.pallas_call(
        body, name=name, out_shape=jax.ShapeDtypeStruct((nbr, n_pairs, 2, 2 * P, P2), F32), grid=(n_pairs,),
        in_specs=[pl.BlockSpec(memory_space=pltpu.SMEM), pl.BlockSpec((nbr, P, P2), lambda h: (0, 0, 0))],
        out_specs=pl.BlockSpec((nbr, 1, 2, 2 * P, P2), lambda h: (0, h, 0, 0, 0)),
        compiler_params=_params("parallel"),
    )(rel_bias, bmap)


def _bias_grad(dbias, bmap, name):
    nbr, n_pairs, _, P2 = dbias.shape
    P = P2 // 2

    def body(db_ref, bm_ref, o_ref):
        lane = lax.broadcasted_iota(jnp.int32, (1, LANES), 1)
        for hh in range(2):
            row = jnp.zeros((1, LANES), F32)
            for br in range(nbr):
                bm = bm_ref[br]
                d = db_ref[br, 0, hh * P:(hh + 1) * P, :]
                for b in range(REL_BUCKETS):
                    hit = jnp.sum(jnp.where(bm == b, d, 0.0), axis=1, keepdims=True)
                    row = row + jnp.where(lane == b, jnp.sum(hit, axis=0, keepdims=True), 0.0)
            o_ref[hh] = row

    return pl.pallas_call(
        body, name=name, out_shape=jax.ShapeDtypeStruct((2 * n_pairs, 1, LANES), F32), grid=(n_pairs,),
        in_specs=[pl.BlockSpec((nbr, 1, P2, P2), lambda h: (0, h, 0, 0)), pl.BlockSpec((nbr, P, P2), lambda h: (0, 0, 0))],
        out_specs=pl.BlockSpec((2, 1, LANES), lambda h: (h, 0, 0)),
        compiler_params=_params("parallel"),
    )(dbias, bmap)[:, 0, :]


def _rows(start, dilation):
    if dilation == 1:
        return pl.ds(pl.multiple_of(start, ATT_BLOCK), ATT_BLOCK)
    return pl.ds(start, ATT_BLOCK, stride=dilation)


def _for_each_block(seq, unroll, fn):
    P = ATT_BLOCK
    for br, (_, d) in enumerate(DILATED_BRANCHES):
        nb = seq // d // P
        u = unroll[br] if (unroll[br] % nb == 0 or nb % unroll[br] == 0) else 1
        step = d * P

        def some(i, carry, br=br, d=d, nb=nb, u=u, step=step):
            blocks = []
            if u % nb == 0:
                for k in range(u):
                    if k % nb == 0:
                        start = i * (u // nb) + k // nb
                        blocks.append((start, start, 1))
                    else:
                        blocks.append((blocks[-1][0] + step, blocks[-1][0], 0))
            else:
                r, j0 = (i * u) // nb, (i * u) % nb
                blocks.append((r + j0 * step, r + jnp.maximum(j0 - 1, 0) * step, jnp.where(j0 == 0, 1, 0)))
                for _ in range(1, u):
                    blocks.append((blocks[-1][0] + step, blocks[-1][0], 0))
            fn(br, d, blocks)
            return carry

        lax.fori_loop(0, nb * d // u, some, 0)


class _RowCache:
    def __init__(self, dilation):
        self.dilation, self.seen = dilation, {}

    def rows(self, ref, start):
        key = (id(ref), id(start))
        if key not in self.seen:
            self.seen[key] = ref[_rows(start, self.dilation), :].astype(BF16)
        return self.seen[key]

    def window(self, ref, start, prev):
        return jnp.concatenate([self.rows(ref, prev), self.rows(ref, start)], axis=0)


def _stack_heads(x, head0):
    return jnp.concatenate([jnp.where(head0, x, 0.0), jnp.where(head0, 0.0, x)], axis=0).astype(BF16)


def _attn_fwd(q, kv, bias, name):
    B, S, D = q.shape
    P = ATT_BLOCK
    n_pairs = D // LANES
    nbr = len(DILATED_BRANCHES)
    scale = HEAD_DIM ** -0.5

    def body(q_ref, k_ref, v_ref, bias_ref, o_ref, lse_ref, *stats):
        m_s, l_s, acc_s = stats[0:nbr], stats[nbr:2 * nbr], stats[2 * nbr:3 * nbr]
        head0 = lax.broadcasted_iota(jnp.int32, (P, LANES), 1) < HEAD_DIM

        def block(br, d, blocks):
            cache = _RowCache(d)
            s = [lax.dot_general(_stack_heads(q_ref[_rows(start, d), :] * scale, head0),
                                 cache.window(k_ref, start, prev), (((1,), (1,)), ((), ())),
                                 preferred_element_type=F32) + bias_ref[br, 0, first]
                 for start, prev, first in blocks]
            m = [jnp.max(x, axis=-1, keepdims=True) for x in s]
            p = [jnp.exp(x - y) for x, y in zip(s, m)]
            l = [jnp.sum(x, axis=-1, keepdims=True) for x in p]
            pv = [jnp.dot(x.astype(BF16), cache.window(v_ref, start, prev), preferred_element_type=F32)
                  for x, (start, prev, _) in zip(p, blocks)]
            for k, (start, _, _) in enumerate(blocks):
                rows = _rows(start, d)
                m_s[br][rows, :] = jnp.where(head0, m[k][:P], m[k][P:])
                l_s[br][rows, :] = jnp.where(head0, l[k][:P], l[k][P:])
                acc_s[br][rows, :] = jnp.where(head0, pv[k][:P], pv[k][P:])

        _for_each_block(S, ATTN_FWD_UNROLL, block)

        chunk = _tile(S, 256, SUBLANES_F32)

        def merge(i, carry):
            rows = pl.ds(pl.multiple_of(i * chunk, chunk), chunk)
            ms = [m_s[br][rows, :] for br in range(nbr)]
            m = functools.reduce(jnp.maximum, ms)
            l = jnp.zeros((chunk, LANES), F32)
            acc = jnp.zeros((chunk, LANES), F32)
            for br in range(nbr):
                w = jnp.exp(ms[br] - m)
                l = l + w * l_s[br][rows, :]
                acc = acc + w * acc_s[br][rows, :]
            o_ref[rows, :] = acc / l
            lse_ref[rows, :] = m + jnp.log(l)
            return carry

        lax.fori_loop(0, S // chunk, merge, 0)

    slab = lambda col0: pl.BlockSpec((None, S, LANES), lambda b, h: (b, 0, col0 + h))
    return pl.pallas_call(
        body, name=name,
        out_shape=(jax.ShapeDtypeStruct((B, S, D), F32), jax.ShapeDtypeStruct((B, S, D), F32)),
        grid=(B, n_pairs),
        in_specs=[slab(0), slab(0), slab(n_pairs),
                  pl.BlockSpec((nbr, 1, 2, 2 * P, 2 * P), lambda b, h: (0, h, 0, 0, 0))],
        out_specs=(slab(0), slab(0)),
        scratch_shapes=[pltpu.VMEM((S, LANES), F32)] * (3 * nbr),
        compiler_params=_params("parallel", "parallel"),
    )(q, kv, kv, bias)


def _attn_bwd(q, kv, o, lse, do, bias, name, comm=None):
    B, S, D = q.shape
    P = ATT_BLOCK
    n_pairs = D // LANES
    nbr = len(DILATED_BRANCHES)
    scale = HEAD_DIM ** -0.5

    def body(q_ref, k_ref, v_ref, o_ref, lse_ref, do_ref, bias_ref, dq_ref, dk_ref, dv_ref, dbias_ref, delta_s):
        head0 = lax.broadcasted_iota(jnp.int32, (P, LANES), 1) < HEAD_DIM

        @pl.when(pl.program_id(1) == 0)
        def _():
            dbias_ref[...] = jnp.zeros_like(dbias_ref)

        chunk = _tile(S, 512, SUBLANES_F32)

        def prepare(i, carry):
            rows = pl.ds(pl.multiple_of(i * chunk, chunk), chunk)
            x = do_ref[rows, :] * o_ref[rows, :]
            h0 = lax.broadcasted_iota(jnp.int32, (chunk, LANES), 1) < HEAD_DIM
            d0 = jnp.sum(jnp.where(h0, x, 0.0), axis=-1, keepdims=True)
            d1 = jnp.sum(jnp.where(h0, 0.0, x), axis=-1, keepdims=True)
            delta_s[rows, :] = jnp.where(h0, d0, d1)
            zero = jnp.zeros((chunk, LANES), F32)
            dq_ref[rows, :] = zero
            dk_ref[rows, :] = zero
            dv_ref[rows, :] = zero
            return carry

        lax.fori_loop(0, S // chunk, prepare, 0)

        def per_head(x):
            return jnp.concatenate([x[:, 0:1], x[:, HEAD_DIM:HEAD_DIM + 1]], axis=0)

        nt = (((1,), (1,)), ((), ()))
        tn = (((0,), (0,)), ((), ()))

        def block(br, d, blocks):
            cache = _RowCache(d)
            q2 = [_stack_heads(q_ref[_rows(start, d), :] * scale, head0) for start, _, _ in blocks]
            do2 = [_stack_heads(do_ref[_rows(start, d), :], head0) for start, _, _ in blocks]
            kb = [cache.window(k_ref, start, prev) for start, prev, _ in blocks]
            vb = [cache.window(v_ref, start, prev) for start, prev, _ in blocks]
            s = [lax.dot_general(a, b, nt, preferred_element_type=F32) + bias_ref[br, 0, first]
                 for a, b, (_, _, first) in zip(q2, kb, blocks)]
            dp = [lax.dot_general(a, b, nt, preferred_element_type=F32) for a, b in zip(do2, vb)]
            p = [jnp.exp(x - per_head(lse_ref[_rows(start, d), :])) for x, (start, _, _) in zip(s, blocks)]
            ds = [x * (y - per_head(delta_s[_rows(start, d), :])) for x, y, (start, _, _) in zip(p, dp, blocks)]
            for x in ds:
                dbias_ref[br, 0] += x
            ds16 = [x.astype(BF16) for x in ds]
            dq2 = [jnp.dot(a, b, preferred_element_type=F32) for a, b in zip(ds16, kb)]
            dk = [lax.dot_general(a, b, tn, preferred_element_type=F32) for a, b in zip(ds16, q2)]
            dv = [lax.dot_general(a.astype(BF16), b, tn, preferred_element_type=F32) for a, b in zip(p, do2)]
            parts = {}
            for k, (start, prev, first) in enumerate(blocks):
                dq_ref[_rows(start, d), :] += jnp.where(head0, dq2[k][:P], dq2[k][P:]) * scale
                parts.setdefault(id(start), [start, []])[1].append((dk[k][P:], dv[k][P:]))
                if not (isinstance(first, int) and first == 1):
                    parts.setdefault(id(prev), [prev, []])[1].append((dk[k][:P], dv[k][:P]))
            for start, terms in parts.values():
                rows = _rows(start, d)
                dk_ref[rows, :] += functools.reduce(jnp.add, [t[0] for t in terms])
                dv_ref[rows, :] += functools.reduce(jnp.add, [t[1] for t in terms])

        _for_each_block(S, ATTN_BWD_UNROLL, block)

    slab = lambda col0: pl.BlockSpec((None, S, LANES), lambda h, b: (b, 0, col0 + h))
    tab = pl.BlockSpec((nbr, 1, 2, 2 * P, 2 * P), lambda h, b: (0, h, 0, 0, 0))
    dtab = pl.BlockSpec((nbr, 1, 2 * P, 2 * P), lambda h, b: (0, h, 0, 0))
    shp = jax.ShapeDtypeStruct((B, S, D), F32)
    return _pallas(
        body, comm=comm, name=name,
        out_shape=(shp, shp, shp, jax.ShapeDtypeStruct((nbr, n_pairs, 2 * P, 2 * P), F32)),
        grid=(n_pairs, B),
        in_specs=[slab(0), slab(0), slab(n_pairs), slab(0), slab(0), slab(0), tab],
        out_specs=(slab(0), slab(0), slab(0), dtab),
        scratch_shapes=[pltpu.VMEM((S, LANES), F32)],
        compiler_params=_params("parallel", "arbitrary"),
    )(q, kv, kv, o, lse, do, bias)


def _adamw(w, g, m, v, name):
    R, C = w.shape
    tr = _tile(R, 256, SUBLANES_F32) if R % SUBLANES_F32 == 0 else R
    tc = _tile(C, 2048, LANES) if C % LANES == 0 else C

    def body(w_ref, g_ref, m_ref, v_ref, d_ref, nm_ref, nv_ref):
        g_ = g_ref[...]
        m2 = ADAM_B1 * m_ref[...] + (1.0 - ADAM_B1) * g_
        v2 = ADAM_B2 * v_ref[...] + (1.0 - ADAM_B2) * (g_ * g_)
        m_hat = m2 / (1.0 - ADAM_B1 ** ADAM_STEP)
        v_hat = v2 / (1.0 - ADAM_B2 ** ADAM_STEP)
        d_ref[...] = -ADAM_LR * (m_hat / (jnp.sqrt(v_hat) + ADAM_EPS) + ADAM_WD * w_ref[...])
        nm_ref[...] = m2
        nv_ref[...] = v2

    blk = pl.BlockSpec((tr, tc), lambda i, j: (i, j))
    shp = jax.ShapeDtypeStruct((R, C), F32)
    return pl.pallas_call(
        body, name=name, out_shape=(shp, shp, shp), grid=(R // tr, C // tc),
        in_specs=[blk] * 4, out_specs=(blk,) * 3, compiler_params=_params("parallel", "parallel"),
    )(w, g, m, v)


def _sum_slots(slots, name):
    n, R, C = slots.shape
    tr = _tile(R, 256, SUBLANES_BF16) if R % SUBLANES_BF16 == 0 else R
    tc = _tile(C, 2048, LANES) if C % LANES == 0 else C

    def body(s_ref, o_ref):
        acc = s_ref[0].astype(F32)
        for k in range(1, n):
            acc = acc + s_ref[k].astype(F32)
        o_ref[...] = acc

    return pl.pallas_call(
        body, name=name, out_shape=jax.ShapeDtypeStruct((R, C), F32), grid=(R // tr, C // tc),
        in_specs=[pl.BlockSpec((n, tr, tc), lambda i, j: (0, i, j))],
        out_specs=pl.BlockSpec((tr, tc), lambda i, j: (i, j)),
        compiler_params=_params("parallel", "parallel"),
    )(slots)


def _my_place():
    return lax.axis_index("x"), lax.axis_index("y"), lax.axis_index("c")


def _other_chips(x, y):
    return [(1 - x, y), (x, 1 - y), (1 - x, 1 - y)]


def _piece(ref, blk, axis, shard, half):
    h0 = blk[0] // 2
    idx = []
    for dim, n in enumerate(blk):
        if dim == 0:
            start = half * h0 + (shard * n if axis == 0 else 0)
            idx.append(pl.ds(start, h0))
        elif dim == axis:
            idx.append(pl.ds(shard * n, n))
        else:
            idx.append(slice(None))
    return ref.at[tuple(idx)]


def _half_of(ref, blk, half):
    return ref.at[pl.ds(half * (blk[0] // 2), blk[0] // 2)]


def _gather_weights(shards, axes, names, small):
    n = len(shards)
    blks = [s.shape for s in shards]
    task = _gather_ici_task(shards, axes, names)

    def stage1(*refs):
        small_in, outs, small_out = refs[n], refs[n + 1:2 * n + 1], refs[2 * n + 1]
        send_sems, recv_sems, local_sems = refs[2 * n + 2:]
        sem = lambda k: (send_sems.at[k], recv_sems.at[k])
        x, y, c = _my_place()
        me = 2 * x + y
        local = pltpu.make_async_copy(small_in, small_out.at[me], local_sems.at[0])
        local.start()
        sends, recvs = task.copies(None, outs, sem)
        for k, (px, py) in enumerate(_other_chips(x, y)):
            sends.append(_remote(small_in, small_out.at[me], sem(task.n_sems + k), (px, py, c)))
            recvs.append(_remote(small_in, small_out.at[2 * px + py], sem(task.n_sems + k), (px, py, c)))
        for cp in sends:
            cp.start()
        for cp in recvs:
            cp.wait_recv()
        for cp in sends:
            cp.wait_send()
        local.wait()

    res = pl.pallas_call(
        stage1, name="gather_weights_ici",
        out_shape=task.outs + [jax.ShapeDtypeStruct((N_SHARDS,) + small.shape, small.dtype)],
        in_specs=[ANY] * (n + 1), out_specs=[ANY] * (n + 1), input_output_aliases={a: a for a in range(n)},
        scratch_shapes=[pltpu.SemaphoreType.DMA((task.n_sems + 3,)), pltpu.SemaphoreType.DMA((task.n_sems + 3,)),
                        pltpu.SemaphoreType.DMA((1,))],
    )(*task.ins, small)
    full = _run_comm(_gather_d2d_task(list(res[:n]), blks, axes), "gather_weights_d2d")
    return full, res[n]


def _run_comm(comm, name):
    n_in, n_out = len(comm.ins), len(comm.outs)

    def body(*refs):
        send_sems, recv_sems = refs[n_in + n_out:]
        sends, recvs = comm.copies(refs[:n_in], refs[n_in:n_in + n_out], lambda k: (send_sems.at[k], recv_sems.at[k]))
        for cp in sends:
            cp.start()
        for cp in recvs:
            cp.wait_recv()
        for cp in sends:
            cp.wait_send()

    return list(pl.pallas_call(
        body, name=name, out_shape=comm.outs, in_specs=[ANY] * n_in, out_specs=[ANY] * n_out,
        input_output_aliases=comm.aliases, scratch_shapes=[pltpu.SemaphoreType.DMA((comm.n_sems,))] * 2,
    )(*comm.ins))


def _exchange_small(small, extra, name):
    n_in, n_out = len(extra.ins), len(extra.outs)

    def body(*refs):
        ex_in, small_in = refs[:n_in], refs[n_in]
        ex_out, small_out = refs[n_in + 1:n_in + 1 + n_out], refs[n_in + 1 + n_out]
        send_sems, recv_sems, local_sems = refs[n_in + n_out + 2:]
        sem = lambda k: (send_sems.at[k], recv_sems.at[k])
        x, y, c = _my_place()
        me = 4 * x + 2 * y + c
        local = pltpu.make_async_copy(small_in, small_out.at[me], local_sems.at[0])
        local.start()
        sends, recvs = extra.copies(ex_in, ex_out, sem)
        for rel in range(1, N_DEVICES):
            px, py, pc = x ^ ((rel >> 2) & 1), y ^ ((rel >> 1) & 1), c ^ (rel & 1)
            k = extra.n_sems + rel - 1
            sends.append(_remote(small_in, small_out.at[me], sem(k), (px, py, pc)))
            recvs.append(_remote(small_in, small_out.at[4 * px + 2 * py + pc], sem(k), (px, py, pc)))
        for cp in sends:
            cp.start()
        for cp in recvs:
            cp.wait_recv()
        for cp in sends:
            cp.wait_send()
        local.wait()

    n_sems = extra.n_sems + N_DEVICES - 1
    res = pl.pallas_call(
        body, name=name, out_shape=extra.outs + [jax.ShapeDtypeStruct((N_DEVICES,) + small.shape, small.dtype)],
        in_specs=[ANY] * (n_in + 1), out_specs=[ANY] * (n_out + 1), input_output_aliases=extra.aliases,
        scratch_shapes=[pltpu.SemaphoreType.DMA((n_sems,)), pltpu.SemaphoreType.DMA((n_sems,)),
                        pltpu.SemaphoreType.DMA((1,))],
    )(*extra.ins, small)
    return list(res[:n_out]), res[n_out]


def _remote(src, dst, sems, device):
    return pltpu.make_async_remote_copy(src_ref=src, dst_ref=dst, send_sem=sems[0], recv_sem=sems[1],
                                        device_id=device, device_id_type=pl.DeviceIdType.MESH)


def _sum_piece(dest, slots, grad, blk, axis, layer, n_layers, name):
    r, c = blk
    h0 = r // 2
    tr = _tile(h0, 256, SUBLANES_BF16)
    tc = _tile(c, 2048, LANES)
    place = jnp.stack([2 * lax.axis_index("x") + lax.axis_index("y"), lax.axis_index("c")]).astype(jnp.int32)

    def body(p_ref, s_ref, g_ref, *rest):
        acc = g_ref[...].astype(F32) + s_ref[0].astype(F32)
        for k in range(1, N_DEVICES - 1):
            acc = acc + s_ref[k].astype(F32)
        rest[-1][...] = acc

    def g_map(i, j, p):
        return (p[1] * (h0 // tr) + (p[0] * (r // tr) if axis == 0 else 0) + i, (p[0] * (c // tc) if axis == 1 else 0) + j)

    in_specs = [pl.BlockSpec((N_DEVICES - 1, tr, tc), lambda i, j, p: (0, i, j)), pl.BlockSpec((tr, tc), g_map)]
    args = [place, slots, grad]
    if dest is not None:
        in_specs.append(ANY)
        args.append(dest)
    return pl.pallas_call(
        body, name=name, out_shape=jax.ShapeDtypeStruct((n_layers, r, c), F32),
        grid_spec=pltpu.PrefetchScalarGridSpec(
            num_scalar_prefetch=1, grid=(h0 // tr, c // tc), in_specs=in_specs,
            out_specs=pl.BlockSpec((None, tr, tc), lambda i, j, p: (layer, p[1] * (h0 // tr) + i, j))),
        input_output_aliases={3: 0} if dest is not None else {},
        compiler_params=_params("parallel", "parallel"),
    )(*args)


def _swap_halves_task(blocks):
    n = len(blocks)
    layers = [(a, l) for a, b in enumerate(blocks) for l in range(b.shape[0])]

    def copies(in_refs, out_refs, sem):
        x, y, c = _my_place()
        sends, recvs = [], []
        for k, (a, l) in enumerate(layers):
            blk = blocks[a].shape[1:]
            mine = _half_of(out_refs[a].at[l], blk, c)
            sends.append(_remote(mine, mine, sem(k), (x, y, 1 - c)))
            recvs.append(_remote(mine, _half_of(out_refs[a].at[l], blk, 1 - c), sem(k), (x, y, 1 - c)))
        return sends, recvs

    return _Comm(blocks, [jax.ShapeDtypeStruct(b.shape, b.dtype) for b in blocks], {a: a for a in range(n)},
                 len(layers), copies)


def _place_shard(shard, axis, name):
    r, c = shard.shape
    tr = _tile(r, 512, SUBLANES_BF16)
    full = (r * N_SHARDS, c) if axis == 0 else (r, c * N_SHARDS)
    me2 = (2 * lax.axis_index("x") + lax.axis_index("y")).astype(jnp.int32).reshape(1)

    def body(me_ref, s_ref, o_ref):
        o_ref[...] = s_ref[...].astype(o_ref.dtype)

    if axis == 0:
        out_map = lambda i, me: (me[0] * (r // tr) + i, 0)
    else:
        out_map = lambda i, me: (i, me[0])
    return pl.pallas_call(
        body, name=name, out_shape=jax.ShapeDtypeStruct(full, BF16),
        grid_spec=pltpu.PrefetchScalarGridSpec(
            num_scalar_prefetch=1, grid=(r // tr,),
            in_specs=[pl.BlockSpec((tr, c), lambda i, me: (i, 0))], out_specs=pl.BlockSpec((tr, c), out_map)),
        compiler_params=_params("parallel"),
    )(me2, shard)


def _gather_ici_task(shards, axes, names):
    n = len(shards)
    blks = [s.shape for s in shards]
    bases = [_place_shard(s, ax, f"place_{nm}") for s, ax, nm in zip(shards, axes, names)]

    def copies(in_refs, out_refs, sem):
        x, y, c = _my_place()
        me = 2 * x + y
        sends, recvs = [], []
        for a in range(n):
            mine = _piece(out_refs[a], blks[a], axes[a], me, c)
            for k, (px, py) in enumerate(_other_chips(x, y)):
                sends.append(_remote(mine, mine, sem(3 * a + k), (px, py, c)))
                recvs.append(_remote(mine, _piece(out_refs[a], blks[a], axes[a], 2 * px + py, c), sem(3 * a + k),
                                     (px, py, c)))
        return sends, recvs

    return _Comm(bases, [jax.ShapeDtypeStruct(b.shape, b.dtype) for b in bases], {a: a for a in range(n)}, 3 * n, copies)


def _gather_d2d_task(partials, blks, axes):
    n = len(partials)

    def copies(in_refs, out_refs, sem):
        x, y, c = _my_place()
        sends, recvs = [], []
        for a in range(n):
            for k, (px, py) in enumerate(_other_chips(x, y)):
                mine = _piece(out_refs[a], blks[a], axes[a], 2 * px + py, c)
                theirs = _piece(out_refs[a], blks[a], axes[a], 2 * px + py, 1 - c)
                sends.append(_remote(mine, mine, sem(3 * a + k), (x, y, 1 - c)))
                recvs.append(_remote(mine, theirs, sem(3 * a + k), (x, y, 1 - c)))
        return sends, recvs

    return _Comm(partials, [jax.ShapeDtypeStruct(p.shape, p.dtype) for p in partials], {a: a for a in range(n)},
                 3 * n, copies)


def _scatter_task(grads, blks, axes):
    n = len(grads)

    def copies(in_refs, out_refs, sem):
        x, y, c = _my_place()
        sends, recvs = [], []
        for rel in range(1, N_DEVICES):
            px, py, pc = x ^ ((rel >> 2) & 1), y ^ ((rel >> 1) & 1), c ^ (rel & 1)
            for a in range(n):
                src = _piece(in_refs[a], blks[a], axes[a], 2 * px + py, pc)
                k = (N_DEVICES - 1) * a + rel - 1
                sends.append(_remote(src, out_refs[a].at[rel - 1], sem(k), (px, py, pc)))
                recvs.append(_remote(src, out_refs[a].at[rel - 1], sem(k), (px, py, pc)))
        return sends, recvs

    outs = [jax.ShapeDtypeStruct((N_DEVICES - 1, b[0] // 2) + tuple(b[1:]), g.dtype) for b, g in zip(blks, grads)]
    return _Comm(grads, outs, {}, (N_DEVICES - 1) * n, copies)


def _pack(arrays):
    flat = jnp.concatenate([a.reshape(-1).astype(F32) for a in arrays])
    pad = (-flat.shape[0]) % (SUBLANES_F32 * LANES)
    return jnp.pad(flat, (0, pad)).reshape(-1, LANES)


def _unpack(packed, shapes):
    flat = packed.reshape(-1)
    out, off = [], 0
    for s in shapes:
        n = int(np.prod(s))
        out.append(flat[off:off + n].reshape(s))
        off += n
    return out


def _local_step(x, target, W, shards, geom, small):
    W = dict(W)
    B, S, D = x.shape
    T = B * S
    x2 = x.reshape(T, D)
    tgt = target.reshape(T, D)
    bmap = jnp.asarray(_bucket_map())
    blk = lambda names: [geom[k][0] for k in names]
    axs = lambda names: [geom[k][1] for k in names]
    ici = lambda names: _gather_ici_task([shards[k] for k in names], axs(names), names)
    d2d = lambda names, partials: _gather_d2d_task(list(partials), blk(names), axs(names))
    big, slots = {}, {}

    def scatter(names):
        return _scatter_task([big[k] for k in names], blk(names), axs(names))

    def ffn_bwd(l, dout, h, saved, first, second):
        xn, u, v, a = saved
        da = _mm_nt(dout, W[f"w_down{l}"], BF16, f"ffn{l}_down_dx")
        big[f"w_down{l}"] = _mm_tn(a, dout, f"ffn{l}_down_dw")
        names = first + [f"w_down{l}"]
        du, g_cw, g_cb, landed = _ffn_gate_bwd(u, v, small["ffn_conv"][l], da, S, f"ffn{l}_gate_bwd", comm=scatter(names))
        slots.update(zip(names, landed))
        big[f"w_up{l}"] = _mm_tn(xn, du, f"ffn{l}_up_dw")
        norm = (h, small["ffn_norm"][l], dout)
        if second:
            (dh, g_norm), (slots[f"w_up{l}"],) = _mm_nt(du, W[f"w_up{l}"], None, f"ffn{l}_up_dx",
                                                        comm=scatter([f"w_up{l}"]), norm=norm)
        else:
            dh, g_norm = _mm_nt(du, W[f"w_up{l}"], None, f"ffn{l}_up_dx", norm=norm)
        return dh, g_cw, g_cb, g_norm[0]

    xn0 = _rmsnorm_fwd(x2, small["a_norm"][0], "a_norm")
    p, part = _mm_nn(xn0, W["w_in"], None, BF16, "a_in", comm=ici(["w_up0"]))
    z, (W["w_up0"],) = _gate_a_fwd(p, small["a_conv"][0], S, "a_gate", comm=d2d(["w_up0"], part))
    (h1, xn1), part = _mm_nn(z, W["w_out"], x2, F32, "a_out", comm=ici(["w_down0"]), norm_gains=[small["ffn_norm"][0]])
    later = ["w_kv", "w_q", "w_o", "w_up1"]
    u0, landed = _mm_nn(xn1, W["w_up0"], None, BF16, "ffn0_up", comm=_Comm.join([d2d(["w_down0"], part), ici(later)]))
    W["w_down0"] = landed[0]
    (a0, v0), landed = _ffn_gate_fwd(u0, small["ffn_conv"][0], small["ffn_conv_b"][0], S, "ffn0_gate",
                               comm=_Comm.join([d2d(later, landed[1:]), ici(["w_down1"])]))
    W.update(zip(later, landed[:len(later)]))
    (h2, kvn, xn3), (W["w_down1"],) = _mm_nn(a0, W["w_down0"], h1, F32, "ffn0_down",
                                            comm=d2d(["w_down1"], landed[len(later):]),
                                            norm_gains=[small["kv_norm"], small["b_norm"][0]])
    kv = _mm_nn(kvn, W["w_kv"], None, F32, "kv_proj")
    q = _mm_nn(xn3, W["w_q"], None, F32, "q_proj")
    bias = _bias_tables(small["rel_bias"], bmap, "rel_bias_tables")
    q3, kv3 = q.reshape(B, S, D), kv.reshape(B, S, 2 * D)
    o3, lse3 = _attn_fwd(q3, kv3, bias, "attn_fwd")
    o = o3.reshape(T, D)
    h3, xn4 = _mm_nn(o, W["w_o"], h2, F32, "o_proj", norm_gains=[small["ffn_norm"][1]])
    u1 = _mm_nn(xn4, W["w_up1"], None, BF16, "ffn1_up")
    a1, v1 = _ffn_gate_fwd(u1, small["ffn_conv"][1], small["ffn_conv_b"][1], S, "ffn1_gate")
    h4 = _mm_nn(a1, W["w_down1"], h3, F32, "ffn1_down")
    sq_err, dh4, g_final = _loss_head(h4, small["final_norm"], tgt, "loss_head")
    loss = 0.5 * jnp.sum(sq_err) / D

    dh3, g_cw1, g_cb1, g_fn1 = ffn_bwd(1, dh4, h3, (xn4, u1, v1, a1), [], False)
    do = _mm_nt(dh3, W["w_o"], F32, "o_proj_dx")
    big["w_o"] = _mm_tn(o, dh3, "o_proj_dw")
    (dq3, dk3, dv3, dbias), landed = _attn_bwd(q3, kv3, o3, lse3, do.reshape(B, S, D), bias, "attn_bwd",
                                               comm=scatter(["w_up1", "w_o"]))
    slots.update(zip(["w_up1", "w_o"], landed))
    g_rel = _bias_grad(dbias, bmap, "rel_bias_grad")[:, :REL_BUCKETS].T
    dq, dk, dv = dq3.reshape(T, D), dk3.reshape(T, D), dv3.reshape(T, D)
    dh2, g_bn = _mm_nt(dq, W["w_q"], None, "q_proj_dx", norm=(h2, small["b_norm"][0], dh3))
    big["w_q"] = _mm_tn(xn3, dq, "q_proj_dw")
    dh2, g_kvn = _mm_nt([dk, dv], W["w_kv"], None, "kv_proj_dx", norm=(h2, small["kv_norm"], dh2))
    big["w_kv"] = jnp.concatenate([_mm_tn(kvn, dk, "k_proj_dw"), _mm_tn(kvn, dv, "v_proj_dw")], axis=1)
    dh1, g_cw0, g_cb0, g_fn0 = ffn_bwd(0, dh2, h1, (xn1, u0, v0, a0), ["w_q", "w_kv"], True)
    dz = _mm_nt(dh1, W["w_out"], BF16, "a_out_dx")
    big["w_out"] = _mm_tn(z, dh1, "a_out_dw")
    dp, g_aconv, (slots["w_out"],) = _gate_a_bwd(p, small["a_conv"][0], dz, S, "a_gate_bwd", comm=scatter(["w_out"]))
    big["w_in"] = _mm_tn(xn0, dp, "a_in_dw")
    (dx, g_an), (slots["w_in"],) = _mm_nt(dp, W["w_in"], None, "a_in_dx", comm=scatter(["w_in"]),
                                          norm=(x2, small["a_norm"][0], dh1))
    g_bn, g_kvn, g_an = g_bn[0], g_kvn[0], g_an[0]

    small_g = {"a_norm": g_an[None], "a_conv": g_aconv[None], "kv_norm": g_kvn, "b_norm": g_bn[None],
               "rel_bias": g_rel, "ffn_norm": jnp.stack([g_fn0, g_fn1]), "ffn_conv": jnp.stack([g_cw0, g_cw1]),
               "ffn_conv_b": jnp.stack([g_cb0, g_cb1]), "final_norm": g_final}
    return loss, dx.reshape(B, S, D), big, slots, small_g


BIG = ("w_in", "w_out", "w_kv", "w_q", "w_o", "w_up0", "w_up1", "w_down0", "w_down1")
SMALL = ("a_norm", "a_conv", "kv_norm", "b_norm", "rel_bias", "ffn_norm", "ffn_conv", "ffn_conv_b", "final_norm")
SMALL_SHARDED = ("a_norm", "a_conv", "ffn_conv")
WEIGHT_ORDER = ("a_norm", "a_w_in", "a_conv", "a_w_out", "kv_norm", "w_kv", "b_norm", "b_w_q", "b_w_o", "rel_bias",
                "ffn_norm", "ffn_w_up", "ffn_conv", "ffn_conv_b", "ffn_w_down", "final_norm")
GRAD_OF = {"w_in": ("a_w_in", 0), "w_out": ("a_w_out", 0), "w_kv": ("w_kv", 0), "w_q": ("b_w_q", 0), "w_o": ("b_w_o", 0),
           "w_up0": ("ffn_w_up", 0), "w_up1": ("ffn_w_up", 1), "w_down0": ("ffn_w_down", 0), "w_down1": ("ffn_w_down", 1)}


def _as2d(a):
    return a.reshape(-1, a.shape[-1])


def kernel(x, a_norm, a_w_in, a_conv, a_w_out, kv_norm, w_kv, b_norm, b_w_q, b_w_o, rel_bias, ffn_norm, ffn_w_up, ffn_conv, ffn_conv_b, ffn_w_down, final_norm, loss_target, m_a_norm, m_a_w_in, m_a_conv, m_a_w_out, m_kv_norm, m_w_kv, m_b_norm, m_b_w_q, m_b_w_o, m_rel_bias, m_ffn_norm, m_ffn_w_up, m_ffn_conv, m_ffn_conv_b, m_ffn_w_down, m_final_norm, v_a_norm, v_a_w_in, v_a_conv, v_a_w_out, v_kv_norm, v_w_kv, v_b_norm, v_b_w_q, v_b_w_o, v_rel_bias, v_ffn_norm, v_ffn_w_up, v_ffn_conv, v_ffn_conv_b, v_ffn_w_down, v_final_norm):
    given = dict(a_norm=a_norm, a_w_in=a_w_in, a_conv=a_conv, a_w_out=a_w_out, kv_norm=kv_norm, w_kv=w_kv, b_norm=b_norm,
                 b_w_q=b_w_q, b_w_o=b_w_o, rel_bias=rel_bias, ffn_norm=ffn_norm, ffn_w_up=ffn_w_up, ffn_conv=ffn_conv,
                 ffn_conv_b=ffn_conv_b, ffn_w_down=ffn_w_down, final_norm=final_norm)
    mom_m = dict(a_norm=m_a_norm, a_w_in=m_a_w_in, a_conv=m_a_conv, a_w_out=m_a_w_out, kv_norm=m_kv_norm, w_kv=m_w_kv,
                 b_norm=m_b_norm, b_w_q=m_b_w_q, b_w_o=m_b_w_o, rel_bias=m_rel_bias, ffn_norm=m_ffn_norm,
                 ffn_w_up=m_ffn_w_up, ffn_conv=m_ffn_conv, ffn_conv_b=m_ffn_conv_b, ffn_w_down=m_ffn_w_down,
                 final_norm=m_final_norm)
    mom_v = dict(a_norm=v_a_norm, a_w_in=v_a_w_in, a_conv=v_a_conv, a_w_out=v_a_w_out, kv_norm=v_kv_norm, w_kv=v_w_kv,
                 b_norm=v_b_norm, b_w_q=v_b_w_q, b_w_o=v_b_w_o, rel_bias=v_rel_bias, ffn_norm=v_ffn_norm,
                 ffn_w_up=v_ffn_w_up, ffn_conv=v_ffn_conv, ffn_conv_b=v_ffn_conv_b, ffn_w_down=v_ffn_w_down,
                 final_norm=v_final_norm)

    shard = {"w_in": (a_w_in[0], 1), "w_out": (a_w_out[0], 0), "w_kv": (w_kv, 1), "w_q": (b_w_q[0], 0),
             "w_o": (b_w_o[0], 0), "w_up0": (ffn_w_up[0], 1), "w_up1": (ffn_w_up[1], 1),
             "w_down0": (ffn_w_down[0], 0), "w_down1": (ffn_w_down[1], 0)}

    small_sharded = [given[k] for k in SMALL_SHARDED]
    packed = _pack(small_sharded)
    first = ("w_in", "w_out")
    fulls, packed_all = _gather_weights([shard[k][0] for k in first], [shard[k][1] for k in first], first, packed)
    W = dict(zip(first, fulls))
    later = {k: shard[k][0] for k in BIG if k not in first}
    geom = {k: (shard[k][0].shape, shard[k][1]) for k in BIG}
    small = {k: given[k] for k in SMALL}
    per_shard = [_unpack(packed_all[j], [a.shape for a in small_sharded]) for j in range(N_SHARDS)]
    for i, k in enumerate(SMALL_SHARDED):
        small[k] = jnp.concatenate([per_shard[j][i] for j in range(N_SHARDS)], axis=-1)

    loss, grad_x, big_g, slots, small_g = _local_step(x, loss_target, W, later, geom, small)
    loss = lax.psum(loss, MESH_AXES)

    small_shapes = [small_g[k].shape for k in SMALL]
    layers_of = {}
    for k in BIG:
        layers_of.setdefault(GRAD_OF[k][0], []).append(k)
    blocks = {}
    for name, members in layers_of.items():
        dest = None
        for k in members:
            dest = _sum_piece(dest, slots[k], big_g[k], geom[k][0], geom[k][1], GRAD_OF[k][1], len(members), f"sum_{k}")
        blocks[name] = dest
    swapped, small_slots = _exchange_small(_pack([small_g[k] for k in SMALL]), _swap_halves_task(list(blocks.values())),
                                           "swap_halves_exchange_small")
    reduced = dict(zip(blocks, swapped))
    small_sum = _sum_slots(small_slots, "sum_small")
    small_red = dict(zip(SMALL, _unpack(small_sum, small_shapes)))
    j = 2 * lax.axis_index("x") + lax.axis_index("y")
    for k in SMALL_SHARDED:
        w = given[k].shape[-1]
        small_red[k] = lax.dynamic_slice_in_dim(small_red[k], j * w, w, axis=small_red[k].ndim - 1)

    grads, deltas, new_m, new_v = {}, {}, {}, {}
    for name in WEIGHT_ORDER:
        if name in reduced:
            g = reduced[name].reshape(given[name].shape)
            d, nm, nv = _adamw(_as2d(given[name]), _as2d(g), _as2d(mom_m[name]), _as2d(mom_v[name]), f"adamw_{name}")
            grads[name] = g
            deltas[name], new_m[name], new_v[name] = (t.reshape(given[name].shape) for t in (d, nm, nv))
    small_names = [n for n in WEIGHT_ORDER if n not in reduced]
    for n in small_names:
        grads[n] = small_red[n].reshape(given[n].shape)
    sw, sg, sm, sv = (_pack([d[n] for n in small_names]) for d in (given, grads, mom_m, mom_v))
    d, nm, nv = _adamw(sw, sg, sm, sv, "adamw_small")
    shapes = [given[n].shape for n in small_names]
    for n, a, b_, c_ in zip(small_names, _unpack(d, shapes), _unpack(nm, shapes), _unpack(nv, shapes)):
        deltas[n], new_m[n], new_v[n] = a, b_, c_

    return (loss, grad_x, *[grads[n] for n in WEIGHT_ORDER], *[deltas[n] for n in WEIGHT_ORDER],
            *[new_m[n] for n in WEIGHT_ORDER], *[new_v[n] for n in WEIGHT_ORDER])
```

```python
import functools
import math

import numpy as np

import jax
import jax.numpy as jnp
from jax import lax
from jax.experimental import pallas as pl
from jax.experimental.pallas import tpu as pltpu

F32 = jnp.float32
BF16 = jnp.bfloat16

RMS_EPS = 1e-6
HEAD_DIM = 64
ATT_BLOCK = 128
DILATED_BRANCHES = ((128, 1), (512, 4), (2048, 16))
REL_BUCKETS = 32
REL_MAX_DISTANCE = 2048
MASKED_LOGIT = -1e30
ATTN_FWD_UNROLL = (8, 8, 4)
ATTN_BWD_UNROLL = (8, 8, 8)

ADAM_LR = 0.001
ADAM_B1 = 0.9
ADAM_B2 = 0.999
ADAM_EPS = 1e-08
ADAM_WD = 0.01
ADAM_STEP = 10

LANES = 128
SUBLANES_F32 = 8
SUBLANES_BF16 = 16
VMEM_LIMIT_BYTES = 56 * 1024 * 1024

MESH_AXES = ("x", "y", "c")
N_SHARDS = 4
N_DEVICES = 8
ANY = pl.BlockSpec(memory_space=pl.ANY)


def _tile(n, pref, mult):
    best = None
    for t in range(mult, min(n, pref) + 1, mult):
        if n % t == 0:
            best = t
    if best is None:
        raise ValueError(f"no tile for {n} (multiple of {mult}, at most {pref})")
    return best


def _params(*sem):
    return pltpu.CompilerParams(dimension_semantics=sem, vmem_limit_bytes=VMEM_LIMIT_BYTES)


class _Comm:
    def __init__(self, ins, outs, aliases, n_sems, copies):
        self.ins, self.outs, self.aliases, self.n_sems, self.copies = list(ins), list(outs), dict(aliases), n_sems, copies

    @staticmethod
    def join(parts):
        parts = [p for p in parts if p is not None]
        ins, outs, aliases, offs, n_sems = [], [], {}, [], 0
        for p in parts:
            offs.append((len(ins), len(outs), n_sems))
            aliases.update({len(ins) + i: len(outs) + o for i, o in p.aliases.items()})
            ins += p.ins
            outs += p.outs
            n_sems += p.n_sems

        def copies(in_refs, out_refs, sem):
            sends, recvs = [], []
            for p, (i0, o0, s0) in zip(parts, offs):
                s, r = p.copies(in_refs[i0:i0 + len(p.ins)], out_refs[o0:o0 + len(p.outs)],
                                lambda k, s0=s0: sem(s0 + k))
                sends += s
                recvs += r
            return sends, recvs

        return _Comm(ins, outs, aliases, n_sems, copies)


def _pallas(body, *, comm=None, name, out_shape, grid=(), in_specs=(), out_specs=(), scratch_shapes=(),
            compiler_params=None):
    if comm is None:
        return pl.pallas_call(body, name=name, out_shape=out_shape, grid=grid, in_specs=in_specs, out_specs=out_specs,
                              scratch_shapes=scratch_shapes, compiler_params=compiler_params)
    single = not isinstance(out_shape, (tuple, list))
    outs = (out_shape,) if single else tuple(out_shape)
    o_specs = (out_specs,) if single else tuple(out_specs)
    n_in, n_cin, n_out, n_cout, n_scr = len(in_specs), len(comm.ins), len(outs), len(comm.outs), len(scratch_shapes)

    def carried(*refs):
        base_in, c_in = refs[:n_in], refs[n_in:n_in + n_cin]
        o0 = n_in + n_cin
        base_out, c_out = refs[o0:o0 + n_out], refs[o0 + n_out:o0 + n_out + n_cout]
        s0 = o0 + n_out + n_cout
        base_scr, (send_sems, recv_sems) = refs[s0:s0 + n_scr], refs[s0 + n_scr:]
        sem = lambda k: (send_sems.at[k], recv_sems.at[k])
        first = functools.reduce(jnp.logical_and, [pl.program_id(a) == 0 for a in range(len(grid))])
        last = functools.reduce(jnp.logical_and, [pl.program_id(a) == n - 1 for a, n in enumerate(grid)])

        @pl.when(first)
        def _():
            for cp in comm.copies(c_in, c_out, sem)[0]:
                cp.start()

        body(*base_in, *base_out, *base_scr)

        @pl.when(last)
        def _():
            sends, recvs = comm.copies(c_in, c_out, sem)
            for cp in recvs:
                cp.wait_recv()
            for cp in sends:
                cp.wait_send()

    call = pl.pallas_call(
        carried, name=name, out_shape=outs + tuple(comm.outs), grid=grid,
        in_specs=list(in_specs) + [ANY] * n_cin, out_specs=o_specs + (ANY,) * n_cout,
        scratch_shapes=list(scratch_shapes) + [pltpu.SemaphoreType.DMA((comm.n_sems,))] * 2,
        input_output_aliases={n_in + i: n_out + o for i, o in comm.aliases.items()},
        compiler_params=_params(*(["arbitrary"] * len(grid))))

    def run(*args):
        res = call(*args, *comm.ins)
        base = res[0] if single else tuple(res[:n_out])
        return base, list(res[n_out:])

    return run


def _rmsnorm_fwd(x, g, name):
    T, D = x.shape
    tm = _tile(T, 512, SUBLANES_BF16)

    def body(x_ref, g_ref, o_ref):
        xf = x_ref[...]
        r = lax.rsqrt(jnp.mean(xf * xf, axis=-1, keepdims=True) + RMS_EPS)
        o_ref[...] = ((xf * r) * g_ref[...]).astype(o_ref.dtype)

    return pl.pallas_call(
        body, name=name, out_shape=jax.ShapeDtypeStruct((T, D), BF16), grid=(T // tm,),
        in_specs=[pl.BlockSpec((tm, D), lambda i: (i, 0)), pl.BlockSpec((1, D), lambda i: (0, 0))],
        out_specs=pl.BlockSpec((tm, D), lambda i: (i, 0)),
        compiler_params=_params("parallel"),
    )(x, g.reshape(1, D))


def _loss_head(a, b, res, g, target, name):
    T, K = a.shape
    D = b.shape[1]
    tm = _tile(T, 512, SUBLANES_BF16)

    def body(a_ref, b_ref, res_ref, g_ref, t_ref, dh_ref, acc_ref):
        @pl.when(pl.program_id(0) == 0)
        def _():
            acc_ref[...] = jnp.zeros_like(acc_ref)

        xf = jnp.dot(a_ref[...].astype(BF16), b_ref[...], preferred_element_type=F32) + res_ref[...]
        r = lax.rsqrt(jnp.mean(xf * xf, axis=-1, keepdims=True) + RMS_EPS)
        xhat = xf * r
        err = xhat * g_ref[...] - t_ref[...]
        dy = err * (1.0 / D)
        acc_ref[0:1, :] += jnp.sum(dy * xhat, axis=0, keepdims=True)
        acc_ref[1:2, :] += jnp.sum(err * err, axis=0, keepdims=True)
        t = dy * g_ref[...]
        dh_ref[...] = r * (t - xhat * jnp.mean(t * xhat, axis=-1, keepdims=True))

    row = pl.BlockSpec((tm, D), lambda i: (i, 0))
    dh, acc = pl.pallas_call(
        body, name=name,
        out_shape=(jax.ShapeDtypeStruct((T, D), F32), jax.ShapeDtypeStruct((SUBLANES_F32, D), F32)),
        grid=(T // tm,),
        in_specs=[pl.BlockSpec((tm, K), lambda i: (i, 0)), pl.BlockSpec((K, D), lambda i: (0, 0)), row,
                  pl.BlockSpec((1, D), lambda i: (0, 0)), row],
        out_specs=(row, pl.BlockSpec((SUBLANES_F32, D), lambda i: (0, 0))),
        compiler_params=_params("arbitrary"),
    )(a, b, res, g.reshape(1, D), target)
    return acc[1], dh, acc[0]


def _mm_nn(a, b, res, out_dtype, name, comm=None, norm_gains=()):
    T, K = a.shape
    N = b.shape[1]
    n_g = len(norm_gains)
    tm = _tile(T, 512 if n_g else 1024, SUBLANES_BF16)
    tn = _tile(N, 3072, LANES)
    assert not n_g or tn == N, "the fused rmsnorm needs whole rows in one tile"

    def body(a_ref, b_ref, *rest):
        ins, outs = rest[:len(rest) - 1 - n_g], rest[len(rest) - 1 - n_g:]
        acc = jnp.dot(a_ref[...].astype(BF16), b_ref[...], preferred_element_type=F32)
        if res is not None:
            acc = acc + ins[0][...]
        outs[0][...] = acc.astype(outs[0].dtype)
        if n_g:
            y = acc * lax.rsqrt(jnp.mean(acc * acc, axis=-1, keepdims=True) + RMS_EPS)
            for g_ref, xn_ref in zip(ins[len(ins) - n_g:], outs[1:]):
                xn_ref[...] = (y * g_ref[...]).astype(xn_ref.dtype)

    tile = pl.BlockSpec((tm, tn), lambda j, i: (i, j))
    in_specs = [pl.BlockSpec((tm, K), lambda j, i: (i, 0)), pl.BlockSpec((K, tn), lambda j, i: (0, j))]
    args = [a, b]
    if res is not None:
        in_specs.append(tile)
        args.append(res)
    for g in norm_gains:
        in_specs.append(pl.BlockSpec((1, N), lambda j, i: (0, 0)))
        args.append(g.reshape(1, N))
    out_shape = jax.ShapeDtypeStruct((T, N), out_dtype)
    if n_g:
        out_shape = (out_shape,) + (jax.ShapeDtypeStruct((T, N), BF16),) * n_g
    return _pallas(
        body, comm=comm, name=name, out_shape=out_shape, grid=(N // tn, T // tm),
        in_specs=in_specs, out_specs=(tile,) * (1 + n_g) if n_g else tile,
        compiler_params=_params("parallel", "parallel"),
    )(*args)


def _mm_nt(dy, b, out_dtype, name, comm=None, norm=None, head_dot=None):
    dys = list(dy) if isinstance(dy, (list, tuple)) else [dy]
    T, n_each = dys[0].shape
    K = b.shape[0]
    tm = _tile(T, 2048 if norm is None and head_dot is None else 1024, SUBLANES_BF16)
    tk = _tile(K, 1536, LANES)
    tn = _tile(n_each, 2816 if norm is None else 1536, LANES)
    per = n_each // tn
    n_steps = per * len(dys)
    assert norm is None or tk == K, "the fused rmsnorm backward needs whole rows in one tile"

    def body(*refs):
        dy_refs, b_ref, acc_ref = refs[:len(dys)], refs[len(dys)], refs[-1]
        i, n = pl.program_id(0), pl.program_id(2)

        @pl.when(n == 0)
        def _():
            acc_ref[...] = jnp.zeros_like(acc_ref)

        for p, dy_ref in enumerate(dy_refs):
            @pl.when(jnp.logical_and(n >= p * per, n < (p + 1) * per))
            def _(dy_ref=dy_ref):
                acc_ref[...] += lax.dot_general(dy_ref[...].astype(BF16), b_ref[...], (((1,), (1,)), ((), ())),
                                                preferred_element_type=F32)

        if norm is None and head_dot is None:
            @pl.when(n == n_steps - 1)
            def _():
                refs[-2][...] = acc_ref[...].astype(refs[-2].dtype)
        elif norm is None:
            o_ref, out_ref, dot_ref = refs[len(dys) + 1:-1]

            @pl.when(n == n_steps - 1)
            def _():
                d = acc_ref[...]
                out_ref[...] = d.astype(out_ref.dtype)
                head0 = lax.broadcasted_iota(jnp.int32, (tm, LANES), 1) < HEAD_DIM
                for c0 in range(0, tk, LANES):
                    x = d[:, c0:c0 + LANES] * o_ref[:, c0:c0 + LANES]
                    d0 = jnp.sum(jnp.where(head0, x, 0.0), axis=-1, keepdims=True)
                    d1 = jnp.sum(jnp.where(head0, 0.0, x), axis=-1, keepdims=True)
                    dot_ref[:, c0:c0 + LANES] = jnp.where(head0, d0, d1)
        else:
            x_ref, g_ref, dres_ref, dx_ref, dg_ref = refs[len(dys) + 1:-1]

            @pl.when(jnp.logical_and(i == 0, n == 0))
            def _():
                dg_ref[...] = jnp.zeros_like(dg_ref)

            @pl.when(n == n_steps - 1)
            def _():
                xf = x_ref[...]
                r = lax.rsqrt(jnp.mean(xf * xf, axis=-1, keepdims=True) + RMS_EPS)
                xhat = xf * r
                d = acc_ref[...]
                dg_ref[0:1, :] += jnp.sum(d * xhat, axis=0, keepdims=True)
                t = d * g_ref[...]
                dx_ref[...] = dres_ref[...] + r * (t - xhat * jnp.mean(t * xhat, axis=-1, keepdims=True))

    in_specs = [pl.BlockSpec((tm, tn), lambda i, k, n, p=p: (i, jnp.clip(n - p * per, 0, per - 1))) for p in range(len(dys))]
    in_specs.append(pl.BlockSpec((tk, tn), lambda i, k, n: (k, n)))
    args = dys + [b]
    tile = pl.BlockSpec((tm, tk), lambda i, k, n: (i, k))
    if norm is None and head_dot is None:
        out_shape, out_specs = jax.ShapeDtypeStruct((T, K), out_dtype), tile
    elif norm is None:
        in_specs.append(tile)
        args.append(head_dot)
        out_shape = (jax.ShapeDtypeStruct((T, K), out_dtype), jax.ShapeDtypeStruct((T, K), F32))
        out_specs = (tile, tile)
    else:
        x, g, dres = norm
        in_specs += [tile, pl.BlockSpec((1, K), lambda i, k, n: (0, 0)), tile]
        args += [x, g.reshape(1, K), dres]
        out_shape = (jax.ShapeDtypeStruct((T, K), F32), jax.ShapeDtypeStruct((SUBLANES_F32, K), F32))
        out_specs = (tile, pl.BlockSpec((SUBLANES_F32, K), lambda i, k, n: (0, 0)))
    return _pallas(
        body, comm=comm, name=name, out_shape=out_shape, grid=(T // tm, K // tk, n_steps),
        in_specs=in_specs, out_specs=out_specs, scratch_shapes=[pltpu.VMEM((tm, tk), F32)],
        compiler_params=_params("parallel", "parallel", "arbitrary") if norm is None else _params(*["arbitrary"] * 3),
    )(*args)


def _mm_tn(a, dy, name):
    T, K = a.shape
    N = dy.shape[1]
    tt = _tile(T, 2048, SUBLANES_BF16)
    tk = _tile(K, 1536, LANES)
    tn = _tile(N, 1536, LANES)
    t_steps = T // tt

    def body(a_ref, dy_ref, o_ref, acc_ref):
        t = pl.program_id(2)

        @pl.when(t == 0)
        def _():
            acc_ref[...] = jnp.zeros_like(acc_ref)

        acc_ref[...] += lax.dot_general(a_ref[...].astype(BF16), dy_ref[...].astype(BF16),
                                        (((0,), (0,)), ((), ())), preferred_element_type=F32)

        @pl.when(t == t_steps - 1)
        def _():
            o_ref[...] = acc_ref[...].astype(o_ref.dtype)

    return pl.pallas_call(
        body, name=name, out_shape=jax.ShapeDtypeStruct((K, N), BF16), grid=(K // tk, N // tn, t_steps),
        in_specs=[pl.BlockSpec((tt, tk), lambda k, n, t: (t, k)), pl.BlockSpec((tt, tn), lambda k, n, t: (t, n))],
        out_specs=pl.BlockSpec((tk, tn), lambda k, n, t: (k, n)),
        scratch_shapes=[pltpu.VMEM((tk, tn), F32)],
        compiler_params=_params("parallel", "parallel", "arbitrary"),
    )(a, dy)


def _rows_before(halo, cur, k):
    h = halo.shape[0]
    return pltpu.roll(jnp.concatenate([halo, cur], axis=0), k, 0)[h:]


def _rows_after(cur, halo, k):
    n = cur.shape[0]
    total = n + halo.shape[0]
    return pltpu.roll(jnp.concatenate([cur, halo], axis=0), total - k, 0)[:n]


def _halo_specs(tm, width, n_rows):
    per = tm // SUBLANES_BF16
    last = n_rows // SUBLANES_BF16 - 1
    prev = pl.BlockSpec((SUBLANES_BF16, width), lambda i: (jnp.maximum(i * per - 1, 0), 0))
    nxt = pl.BlockSpec((SUBLANES_BF16, width), lambda i: (jnp.minimum((i + 1) * per, last), 0))
    return prev, nxt


def _gate_a_fwd(p, cw, seq, name, comm=None):
    T, D3 = p.shape
    D = D3 // 3
    tm = _tile(seq, 512, SUBLANES_BF16)
    cc = _tile(D, 256, LANES)
    prev, _ = _halo_specs(tm, D3, T)

    def body(p_ref, ph_ref, cw_ref, z_ref):
        at_start = (pl.program_id(0) * tm) % seq == 0
        for c0 in range(0, D, cc):
            b = p_ref[:, c0:c0 + cc].astype(F32)
            u = p_ref[:, D + c0:D + c0 + cc].astype(F32) * p_ref[:, 2 * D + c0:2 * D + c0 + cc].astype(F32)
            uh = ph_ref[:, D + c0:D + c0 + cc].astype(F32) * ph_ref[:, 2 * D + c0:2 * D + c0 + cc].astype(F32)
            uh = jnp.where(at_start, 0.0, uh)
            w = cw_ref[:, c0:c0 + cc]
            cv = _rows_before(uh, u, 2) * w[0:1] + _rows_before(uh, u, 1) * w[1:2] + u * w[2:3]
            z_ref[:, c0:c0 + cc] = (b * cv).astype(z_ref.dtype)

    return _pallas(
        body, comm=comm, name=name, out_shape=jax.ShapeDtypeStruct((T, D), BF16), grid=(T // tm,),
        in_specs=[pl.BlockSpec((tm, D3), lambda i: (i, 0)), prev, pl.BlockSpec((3, D), lambda i: (0, 0))],
        out_specs=pl.BlockSpec((tm, D), lambda i: (i, 0)),
        compiler_params=_params("parallel"),
    )(p, p, cw)


def _gate_a_bwd(p, cw, dz, seq, name, comm=None):
    T, D3 = p.shape
    D = D3 // 3
    tm = _tile(seq, 512, SUBLANES_BF16)
    cc = _tile(D, 256, LANES)
    p_prev, p_next = _halo_specs(tm, D3, T)
    _, dz_next = _halo_specs(tm, D, T)

    def body(p_ref, pp_ref, pn_ref, dz_ref, dzn_ref, cw_ref, dp_ref, dcw_ref):
        i = pl.program_id(0)

        @pl.when(i == 0)
        def _():
            dcw_ref[...] = jnp.zeros_like(dcw_ref)

        at_start = (i * tm) % seq == 0
        at_end = ((i + 1) * tm) % seq == 0
        for c0 in range(0, D, cc):
            cb, cc_, ch = slice(c0, c0 + cc), slice(D + c0, D + c0 + cc), slice(2 * D + c0, 2 * D + c0 + cc)
            b = p_ref[:, cb].astype(F32)
            c = p_ref[:, cc_].astype(F32)
            hh = p_ref[:, ch].astype(F32)
            u = c * hh
            uh = jnp.where(at_start, 0.0, pp_ref[:, cc_].astype(F32) * pp_ref[:, ch].astype(F32))
            w = cw_ref[:, cb]
            u1 = _rows_before(uh, u, 1)
            u2 = _rows_before(uh, u, 2)
            cv = u2 * w[0:1] + u1 * w[1:2] + u * w[2:3]
            dz_t = dz_ref[:, cb].astype(F32)
            dcv = dz_t * b
            dcvn = jnp.where(at_end, 0.0, dzn_ref[:, cb].astype(F32) * pn_ref[:, cb].astype(F32))
            du = dcv * w[2:3] + _rows_after(dcv, dcvn, 1) * w[1:2] + _rows_after(dcv, dcvn, 2) * w[0:1]
            dp_ref[:, cb] = (dz_t * cv).astype(dp_ref.dtype)
            dp_ref[:, cc_] = (du * hh).astype(dp_ref.dtype)
            dp_ref[:, ch] = (du * c).astype(dp_ref.dtype)
            dcw_ref[0:1, cb] += jnp.sum(dcv * u2, axis=0, keepdims=True)
            dcw_ref[1:2, cb] += jnp.sum(dcv * u1, axis=0, keepdims=True)
            dcw_ref[2:3, cb] += jnp.sum(dcv * u, axis=0, keepdims=True)

    res = _pallas(
        body, comm=comm, name=name,
        out_shape=(jax.ShapeDtypeStruct((T, D3), BF16), jax.ShapeDtypeStruct((SUBLANES_F32, D), F32)),
        grid=(T // tm,),
        in_specs=[pl.BlockSpec((tm, D3), lambda i: (i, 0)), p_prev, p_next,
                  pl.BlockSpec((tm, D), lambda i: (i, 0)), dz_next, pl.BlockSpec((3, D), lambda i: (0, 0))],
        out_specs=(pl.BlockSpec((tm, D3), lambda i: (i, 0)), pl.BlockSpec((SUBLANES_F32, D), lambda i: (0, 0))),
        compiler_params=_params("arbitrary"),
    )(p, p, p, dz, dz, cw)
    if comm is None:
        return res[0], res[1][0:3]
    return res[0][0], res[0][1][0:3], res[1]


def _ffn_gate_fwd(u, cw, cb, seq, name, comm=None):
    T, F2 = u.shape
    F = F2 // 2
    tm = _tile(seq, 256, SUBLANES_BF16)
    cc = _tile(F, 256, LANES)
    prev, _ = _halo_specs(tm, F2, T)

    def body(u_ref, uh_ref, cw_ref, cb_ref, a_ref, v_ref):
        at_start = (pl.program_id(0) * tm) % seq == 0

        def conv(c0):
            cols = slice(c0, c0 + cc)
            cur = u_ref[:, cols].astype(F32)
            halo = jnp.where(at_start, 0.0, uh_ref[:, cols].astype(F32))
            w = cw_ref[:, cols]
            return (_rows_before(halo, cur, 2) * w[0:1] + _rows_before(halo, cur, 1) * w[1:2] + cur * w[2:3]
                    + cb_ref[:, cols])

        for c0 in range(0, F, cc):
            g = conv(c0)
            up = conv(F + c0)
            a_ref[:, c0:c0 + cc] = ((g * jax.nn.sigmoid(g)) * up).astype(a_ref.dtype)
            v_ref[:, c0:c0 + cc] = g.astype(v_ref.dtype)
            v_ref[:, F + c0:F + c0 + cc] = up.astype(v_ref.dtype)

    return _pallas(
        body, comm=comm, name=name,
        out_shape=(jax.ShapeDtypeStruct((T, F), BF16), jax.ShapeDtypeStruct((T, F2), BF16)), grid=(T // tm,),
        in_specs=[pl.BlockSpec((tm, F2), lambda i: (i, 0)), prev,
                  pl.BlockSpec((3, F2), lambda i: (0, 0)), pl.BlockSpec((1, F2), lambda i: (0, 0))],
        out_specs=(pl.BlockSpec((tm, F), lambda i: (i, 0)), pl.BlockSpec((tm, F2), lambda i: (i, 0))),
        compiler_params=_params("parallel"),
    )(u, u, cw, cb.reshape(1, F2))


def _ffn_gate_bwd(u, v, cw, da, seq, name, comm=None):
    T, F2 = u.shape
    F = F2 // 2
    H = SUBLANES_BF16
    tm = _tile(seq, 256, H)
    cc = _tile(F, 256, LANES)
    _, v_next = _halo_specs(tm, F2, T)
    _, da_next = _halo_specs(tm, F, T)

    def body(u_ref, v_ref, vn_ref, da_ref, dan_ref, cw_ref, du_ref, acc_ref):
        i = pl.program_id(0)

        @pl.when(i == 0)
        def _():
            acc_ref[...] = jnp.zeros_like(acc_ref)

        at_end = ((i + 1) * tm) % seq == 0
        n = tm + H

        def rows_and_next(ref, nxt, cols):
            return jnp.concatenate([ref[:, cols].astype(F32), nxt[:, cols].astype(F32)], axis=0)

        def back(d, cols):
            w = cw_ref[:, cols]
            d0 = d[:tm]
            d1 = pltpu.roll(d, n - 1, 0)[:tm]
            d2 = pltpu.roll(d, n - 2, 0)[:tm]
            du_ref[:, cols] = (d0 * w[2:3] + d1 * w[1:2] + d2 * w[0:1]).astype(du_ref.dtype)
            ut = u_ref[:, cols].astype(F32)
            acc_ref[0:1, cols] += jnp.sum(d2 * ut, axis=0, keepdims=True)
            acc_ref[1:2, cols] += jnp.sum(d1 * ut, axis=0, keepdims=True)
            acc_ref[2:3, cols] += jnp.sum(d0 * ut, axis=0, keepdims=True)
            acc_ref[3:4, cols] += jnp.sum(d0, axis=0, keepdims=True)

        for c0 in range(0, F, cc):
            gc, uc = slice(c0, c0 + cc), slice(F + c0, F + c0 + cc)
            g = rows_and_next(v_ref, vn_ref, gc)
            up = rows_and_next(v_ref, vn_ref, uc)
            da_ext = jnp.concatenate([da_ref[:, gc].astype(F32),
                                      jnp.where(at_end, 0.0, dan_ref[:, gc].astype(F32))], axis=0)
            sg = jax.nn.sigmoid(g)
            back(da_ext * up * (sg * (1.0 + g * (1.0 - sg))), gc)
            back(da_ext * (g * sg), uc)

    res = _pallas(
        body, comm=comm, name=name,
        out_shape=(jax.ShapeDtypeStruct((T, F2), BF16), jax.ShapeDtypeStruct((SUBLANES_F32, F2), F32)),
        grid=(T // tm,),
        in_specs=[pl.BlockSpec((tm, F2), lambda i: (i, 0)), pl.BlockSpec((tm, F2), lambda i: (i, 0)), v_next,
                  pl.BlockSpec((tm, F), lambda i: (i, 0)), da_next, pl.BlockSpec((3, F2), lambda i: (0, 0))],
        out_specs=(pl.BlockSpec((tm, F2), lambda i: (i, 0)), pl.BlockSpec((SUBLANES_F32, F2), lambda i: (0, 0))),
        compiler_params=_params("arbitrary"),
    )(u, v, v, da, da, cw)
    (du, acc), landed = res if comm is not None else (res, None)
    return (du, acc[0:3], acc[3]) if comm is None else (du, acc[0:3], acc[3], landed)


def _bucket_map():
    P = ATT_BLOCK
    qi = np.arange(P, dtype=np.int64)[:, None]
    kc = np.arange(2 * P, dtype=np.int64)[None, :]
    delta = qi + P - kc
    maps = []
    max_exact = REL_BUCKETS // 2
    for window, dilation in DILATED_BRANCHES:
        band = (delta >= 0) & (delta <= window // dilation)
        n = np.maximum(delta * dilation, 0)
        nf = np.maximum(n, max_exact).astype(np.float32)
        large = max_exact + (np.log(nf / np.float32(max_exact)) / np.float32(math.log(REL_MAX_DISTANCE / max_exact))
                             * np.float32(REL_BUCKETS - max_exact)).astype(np.int32)
        large = np.minimum(large, REL_BUCKETS - 1)
        bucket = np.where(n < max_exact, n, large)
        maps.append(np.where(band, bucket, -1).astype(np.int32))
    return np.stack(maps)


def _bias_tables(rel_bias, bmap, name):
    n_pairs = rel_bias.shape[1] // 2
    nbr, P, P2 = bmap.shape

    def body(rb_ref, bm_ref, o_ref):
        pair = pl.program_id(0)
        in_seq = lax.broadcasted_iota(jnp.int32, (P, P2), 1) >= P
        for br in range(nbr):
            bm = bm_ref[br]
            for hh in range(2):
                acc = jnp.full((P, P2), MASKED_LOGIT, F32)
                for b in range(REL_BUCKETS):
                    acc = jnp.where(bm == b, rb_ref[b, 2 * pair + hh], acc)
                o_ref[br, 0, 0, hh * P:(hh + 1) * P, :] = acc
                o_ref[br, 0, 1, hh * P:(hh + 1) * P, :] = jnp.where(in_seq, acc, MASKED_LOGIT)

    return pl.pallas_call(
        body, name=name, out_shape=jax.ShapeDtypeStruct((nbr, n_pairs, 2, 2 * P, P2), F32), grid=(n_pairs,),
        in_specs=[pl.BlockSpec(memory_space=pltpu.SMEM), pl.BlockSpec((nbr, P, P2), lambda h: (0, 0, 0))],
        out_specs=pl.BlockSpec((nbr, 1, 2, 2 * P, P2), lambda h: (0, h, 0, 0, 0)),
        compiler_params=_params("parallel"),
    )(rel_bias, bmap)


def _bias_grad(dbias, bmap, name):
    nbr, n_pairs, _, P2 = dbias.shape
    P = P2 // 2

    def body(db_ref, bm_ref, o_ref):
        lane = lax.broadcasted_iota(jnp.int32, (1, LANES), 1)
        for hh in range(2):
            row = jnp.zeros((1, LANES), F32)
            for br in range(nbr):
                bm = bm_ref[br]
                d = db_ref[br, 0, hh * P:(hh + 1) * P, :]
                for b in range(REL_BUCKETS):
                    hit = jnp.sum(jnp.where(bm == b, d, 0.0), axis=1, keepdims=True)
                    row = row + jnp.where(lane == b, jnp.sum(hit, axis=0, keepdims=True), 0.0)
            o_ref[hh] = row

    return pl.pallas_call(
        body, name=name, out_shape=jax.ShapeDtypeStruct((2 * n_pairs, 1, LANES), F32), grid=(n_pairs,),
        in_specs=[pl.BlockSpec((nbr, 1, P2, P2), lambda h: (0, h, 0, 0)), pl.BlockSpec((nbr, P, P2), lambda h: (0, 0, 0))],
        out_specs=pl.BlockSpec((2, 1, LANES), lambda h: (h, 0, 0)),
        compiler_params=_params("parallel"),
    )(dbias, bmap)[:, 0, :]


def _rows(start, dilation):
    if dilation == 1:
        return pl.ds(pl.multiple_of(start, ATT_BLOCK), ATT_BLOCK)
    return pl.ds(start, ATT_BLOCK, stride=dilation)


def _for_each_block(seq, unroll, fn):
    P = ATT_BLOCK
    for br, (_, d) in enumerate(DILATED_BRANCHES):
        nb = seq // d // P
        u = unroll[br] if (unroll[br] % nb == 0 or nb % unroll[br] == 0) else 1
        step = d * P

        def some(i, carry, br=br, d=d, nb=nb, u=u, step=step):
            blocks = []
            if u % nb == 0:
                for k in range(u):
                    if k % nb == 0:
                        start = i * (u // nb) + k // nb
                        blocks.append((start, start, 1))
                    else:
                        blocks.append((blocks[-1][0] + step, blocks[-1][0], 0))
            else:
                r, j0 = (i * u) // nb, (i * u) % nb
                blocks.append((r + j0 * step, r + jnp.maximum(j0 - 1, 0) * step, jnp.where(j0 == 0, 1, 0)))
                for _ in range(1, u):
                    blocks.append((blocks[-1][0] + step, blocks[-1][0], 0))
            fn(br, d, blocks)
            return carry

        lax.fori_loop(0, nb * d // u, some, 0)


class _RowCache:
    def __init__(self, dilation):
        self.dilation, self.seen = dilation, {}

    def rows(self, ref, start):
        key = (id(ref), id(start))
        if key not in self.seen:
            self.seen[key] = ref[_rows(start, self.dilation), :].astype(BF16)
        return self.seen[key]

    def window(self, ref, start, prev):
        return jnp.concatenate([self.rows(ref, prev), self.rows(ref, start)], axis=0)


def _stack_heads(x, head0):
    return jnp.concatenate([jnp.where(head0, x, 0.0), jnp.where(head0, 0.0, x)], axis=0).astype(BF16)


def _attn_fwd(q, kv, bias, name):
    B, S, D = q.shape
    P = ATT_BLOCK
    n_pairs = D // LANES
    nbr = len(DILATED_BRANCHES)
    scale = HEAD_DIM ** -0.5

    def body(q_ref, k_ref, v_ref, bias_ref, o_ref, lse_ref, *stats):
        m_s, l_s, acc_s = stats[0:nbr], stats[nbr:2 * nbr], stats[2 * nbr:3 * nbr]
        head0 = lax.broadcasted_iota(jnp.int32, (P, LANES), 1) < HEAD_DIM

        def block(br, d, blocks):
            cache = _RowCache(d)
            s = [lax.dot_general(_stack_heads(q_ref[_rows(start, d), :] * scale, head0),
                                 cache.window(k_ref, start, prev), (((1,), (1,)), ((), ())),
                                 preferred_element_type=F32) + bias_ref[br, 0, first]
                 for start, prev, first in blocks]
            m = [jnp.max(x, axis=-1, keepdims=True) for x in s]
            p = [jnp.exp(x - y) for x, y in zip(s, m)]
            l = [jnp.sum(x, axis=-1, keepdims=True) for x in p]
            pv = [jnp.dot(x.astype(BF16), cache.window(v_ref, start, prev), preferred_element_type=F32)
                  for x, (start, prev, _) in zip(p, blocks)]
            for k, (start, _, _) in enumerate(blocks):
                rows = _rows(start, d)
                m_s[br][rows, :] = jnp.where(head0, m[k][:P], m[k][P:])
                l_s[br][rows, :] = jnp.where(head0, l[k][:P], l[k][P:])
                acc_s[br][rows, :] = jnp.where(head0, pv[k][:P], pv[k][P:])

        _for_each_block(S, ATTN_FWD_UNROLL, block)

        chunk = _tile(S, 256, SUBLANES_F32)

        def merge(i, carry):
            rows = pl.ds(pl.multiple_of(i * chunk, chunk), chunk)
            ms = [m_s[br][rows, :] for br in range(nbr)]
            m = functools.reduce(jnp.maximum, ms)
            l = jnp.zeros((chunk, LANES), F32)
            acc = jnp.zeros((chunk, LANES), F32)
            for br in range(nbr):
                w = jnp.exp(ms[br] - m)
                l = l + w * l_s[br][rows, :]
                acc = acc + w * acc_s[br][rows, :]
            o_ref[rows, :] = acc / l
            lse_ref[rows, :] = m + jnp.log(l)
            return carry

        lax.fori_loop(0, S // chunk, merge, 0)

    slab = lambda col0: pl.BlockSpec((None, S, LANES), lambda b, h: (b, 0, col0 + h))
    return pl.pallas_call(
        body, name=name,
        out_shape=(jax.ShapeDtypeStruct((B, S, D), F32), jax.ShapeDtypeStruct((B, S, D), F32)),
        grid=(B, n_pairs),
        in_specs=[slab(0), slab(0), slab(n_pairs),
                  pl.BlockSpec((nbr, 1, 2, 2 * P, 2 * P), lambda b, h: (0, h, 0, 0, 0))],
        out_specs=(slab(0), slab(0)),
        scratch_shapes=[pltpu.VMEM((S, LANES), F32)] * (3 * nbr),
        compiler_params=_params("parallel", "parallel"),
    )(q, kv, kv, bias)


def _attn_bwd(q, kv, delta, lse, do, bias, name, comm=None):
    B, S, D = q.shape
    P = ATT_BLOCK
    n_pairs = D // LANES
    nbr = len(DILATED_BRANCHES)
    scale = HEAD_DIM ** -0.5

    def body(q_ref, k_ref, v_ref, delta_s, lse_ref, do_ref, bias_ref, dq_ref, dk_ref, dv_ref, dbias_ref):
        head0 = lax.broadcasted_iota(jnp.int32, (P, LANES), 1) < HEAD_DIM

        @pl.when(pl.program_id(1) == 0)
        def _():
            dbias_ref[...] = jnp.zeros_like(dbias_ref)

        chunk = _tile(S, 512, SUBLANES_F32)

        def prepare(i, carry):
            rows = pl.ds(pl.multiple_of(i * chunk, chunk), chunk)
            zero = jnp.zeros((chunk, LANES), F32)
            dq_ref[rows, :] = zero
            dk_ref[rows, :] = zero
            dv_ref[rows, :] = zero
            return carry

        lax.fori_loop(0, S // chunk, prepare, 0)

        def per_head(x):
            return jnp.concatenate([x[:, 0:1], x[:, HEAD_DIM:HEAD_DIM + 1]], axis=0)

        nt = (((1,), (1,)), ((), ()))
        tn = (((0,), (0,)), ((), ()))

        def block(br, d, blocks):
            cache = _RowCache(d)
            q2 = [_stack_heads(q_ref[_rows(start, d), :] * scale, head0) for start, _, _ in blocks]
            do2 = [_stack_heads(do_ref[_rows(start, d), :], head0) for start, _, _ in blocks]
            kb = [cache.window(k_ref, start, prev) for start, prev, _ in blocks]
            vb = [cache.window(v_ref, start, prev) for start, prev, _ in blocks]
            s = [lax.dot_general(a, b, nt, preferred_element_type=F32) + bias_ref[br, 0, first]
                 for a, b, (_, _, first) in zip(q2, kb, blocks)]
            dp = [lax.dot_general(a, b, nt, preferred_element_type=F32) for a, b in zip(do2, vb)]
            p = [jnp.exp(x - per_head(lse_ref[_rows(start, d), :])) for x, (start, _, _) in zip(s, blocks)]
            ds = [x * (y - per_head(delta_s[_rows(start, d), :])) for x, y, (start, _, _) in zip(p, dp, blocks)]
            for x in ds:
                dbias_ref[br, 0] += x
            ds16 = [x.astype(BF16) for x in ds]
            dq2 = [jnp.dot(a, b, preferred_element_type=F32) for a, b in zip(ds16, kb)]
            dk = [lax.dot_general(a, b, tn, preferred_element_type=F32) for a, b in zip(ds16, q2)]
            dv = [lax.dot_general(a.astype(BF16), b, tn, preferred_element_type=F32) for a, b in zip(p, do2)]
            parts = {}
            for k, (start, prev, first) in enumerate(blocks):
                dq_ref[_rows(start, d), :] += jnp.where(head0, dq2[k][:P], dq2[k][P:]) * scale
                parts.setdefault(id(start), [start, []])[1].append((dk[k][P:], dv[k][P:]))
                if not (isinstance(first, int) and first == 1):
                    parts.setdefault(id(prev), [prev, []])[1].append((dk[k][:P], dv[k][:P]))
            for start, terms in parts.values():
                rows = _rows(start, d)
                dk_ref[rows, :] += functools.reduce(jnp.add, [t[0] for t in terms])
                dv_ref[rows, :] += functools.reduce(jnp.add, [t[1] for t in terms])

        _for_each_block(S, ATTN_BWD_UNROLL, block)

    slab = lambda col0: pl.BlockSpec((None, S, LANES), lambda h, b: (b, 0, col0 + h))
    tab = pl.BlockSpec((nbr, 1, 2, 2 * P, 2 * P), lambda h, b: (0, h, 0, 0, 0))
    dtab = pl.BlockSpec((nbr, 1, 2 * P, 2 * P), lambda h, b: (0, h, 0, 0))
    shp = jax.ShapeDtypeStruct((B, S, D), F32)
    return _pallas(
        body, comm=comm, name=name,
        out_shape=(shp, shp, shp, jax.ShapeDtypeStruct((nbr, n_pairs, 2 * P, 2 * P), F32)),
        grid=(n_pairs, B),
        in_specs=[slab(0), slab(0), slab(n_pairs), slab(0), slab(0), slab(0), tab],
        out_specs=(slab(0), slab(0), slab(0), dtab),
        compiler_params=_params("parallel", "arbitrary"),
    )(q, kv, kv, delta, lse, do, bias)


def _adamw(w, g, m, v, name):
    R, C = w.shape
    tr = _tile(R, 256, SUBLANES_F32) if R % SUBLANES_F32 == 0 else R
    tc = _tile(C, 2048, LANES) if C % LANES == 0 else C

    def body(w_ref, g_ref, m_ref, v_ref, d_ref, nm_ref, nv_ref):
        g_ = g_ref[...]
        m2 = ADAM_B1 * m_ref[...] + (1.0 - ADAM_B1) * g_
        v2 = ADAM_B2 * v_ref[...] + (1.0 - ADAM_B2) * (g_ * g_)
        m_hat = m2 / (1.0 - ADAM_B1 ** ADAM_STEP)
        v_hat = v2 / (1.0 - ADAM_B2 ** ADAM_STEP)
        d_ref[...] = -ADAM_LR * (m_hat / (jnp.sqrt(v_hat) + ADAM_EPS) + ADAM_WD * w_ref[...])
        nm_ref[...] = m2
        nv_ref[...] = v2

    blk = pl.BlockSpec((tr, tc), lambda i, j: (i, j))
    shp = jax.ShapeDtypeStruct((R, C), F32)
    return pl.pallas_call(
        body, name=name, out_shape=(shp, shp, shp), grid=(R // tr, C // tc),
        in_specs=[blk] * 4, out_specs=(blk,) * 3, compiler_params=_params("parallel", "parallel"),
    )(w, g, m, v)


def _sum_slots(slots, name):
    n, R, C = slots.shape
    tr = _tile(R, 256, SUBLANES_BF16) if R % SUBLANES_BF16 == 0 else R
    tc = _tile(C, 2048, LANES) if C % LANES == 0 else C

    def body(s_ref, o_ref):
        acc = s_ref[0].astype(F32)
        for k in range(1, n):
            acc = acc + s_ref[k].astype(F32)
        o_ref[...] = acc

    return pl.pallas_call(
        body, name=name, out_shape=jax.ShapeDtypeStruct((R, C), F32), grid=(R // tr, C // tc),
        in_specs=[pl.BlockSpec((n, tr, tc), lambda i, j: (0, i, j))],
        out_specs=pl.BlockSpec((tr, tc), lambda i, j: (i, j)),
        compiler_params=_params("parallel", "parallel"),
    )(slots)


def _my_place():
    return lax.axis_index("x"), lax.axis_index("y"), lax.axis_index("c")


def _other_chips(x, y):
    return [(1 - x, y), (x, 1 - y), (1 - x, 1 - y)]


def _piece(ref, blk, axis, shard, half):
    h0 = blk[0] // 2
    idx = []
    for dim, n in enumerate(blk):
        if dim == 0:
            start = half * h0 + (shard * n if axis == 0 else 0)
            idx.append(pl.ds(start, h0))
        elif dim == axis:
            idx.append(pl.ds(shard * n, n))
        else:
            idx.append(slice(None))
    return ref.at[tuple(idx)]


def _half_of(ref, blk, half):
    return ref.at[pl.ds(half * (blk[0] // 2), blk[0] // 2)]


def _gather_weights(shards, axes, names, small):
    n = len(shards)
    blks = [s.shape for s in shards]
    task = _gather_ici_task(shards, axes, names)

    def stage1(*refs):
        small_in, outs, small_out = refs[n], refs[n + 1:2 * n + 1], refs[2 * n + 1]
        send_sems, recv_sems, local_sems = refs[2 * n + 2:]
        sem = lambda k: (send_sems.at[k], recv_sems.at[k])
        x, y, c = _my_place()
        me = 2 * x + y
        local = pltpu.make_async_copy(small_in, small_out.at[me], local_sems.at[0])
        local.start()
        sends, recvs = task.copies(None, outs, sem)
        for k, (px, py) in enumerate(_other_chips(x, y)):
            sends.append(_remote(small_in, small_out.at[me], sem(task.n_sems + k), (px, py, c)))
            recvs.append(_remote(small_in, small_out.at[2 * px + py], sem(task.n_sems + k), (px, py, c)))
        for cp in sends:
            cp.start()
        for cp in recvs:
            cp.wait_recv()
        for cp in sends:
            cp.wait_send()
        local.wait()

    res = pl.pallas_call(
        stage1, name="gather_weights_ici",
        out_shape=task.outs + [jax.ShapeDtypeStruct((N_SHARDS,) + small.shape, small.dtype)],
        in_specs=[ANY] * (n + 1), out_specs=[ANY] * (n + 1), input_output_aliases={a: a for a in range(n)},
        scratch_shapes=[pltpu.SemaphoreType.DMA((task.n_sems + 3,)), pltpu.SemaphoreType.DMA((task.n_sems + 3,)),
                        pltpu.SemaphoreType.DMA((1,))],
    )(*task.ins, small)
    full = _run_comm(_gather_d2d_task(list(res[:n]), blks, axes), "gather_weights_d2d")
    return full, res[n]


def _run_comm(comm, name):
    n_in, n_out = len(comm.ins), len(comm.outs)

    def body(*refs):
        send_sems, recv_sems = refs[n_in + n_out:]
        sends, recvs = comm.copies(refs[:n_in], refs[n_in:n_in + n_out], lambda k: (send_sems.at[k], recv_sems.at[k]))
        for cp in sends:
            cp.start()
        for cp in recvs:
            cp.wait_recv()
        for cp in sends:
            cp.wait_send()

    return list(pl.pallas_call(
        body, name=name, out_shape=comm.outs, in_specs=[ANY] * n_in, out_specs=[ANY] * n_out,
        input_output_aliases=comm.aliases, scratch_shapes=[pltpu.SemaphoreType.DMA((comm.n_sems,))] * 2,
    )(*comm.ins))


def _exchange_small(small, extra, name):
    n_in, n_out = len(extra.ins), len(extra.outs)

    def body(*refs):
        ex_in, small_in = refs[:n_in], refs[n_in]
        ex_out, small_out = refs[n_in + 1:n_in + 1 + n_out], refs[n_in + 1 + n_out]
        send_sems, recv_sems, local_sems = refs[n_in + n_out + 2:]
        sem = lambda k: (send_sems.at[k], recv_sems.at[k])
        x, y, c = _my_place()
        me = 4 * x + 2 * y + c
        local = pltpu.make_async_copy(small_in, small_out.at[me], local_sems.at[0])
        local.start()
        sends, recvs = extra.copies(ex_in, ex_out, sem)
        for rel in range(1, N_DEVICES):
            px, py, pc = x ^ ((rel >> 2) & 1), y ^ ((rel >> 1) & 1), c ^ (rel & 1)
            k = extra.n_sems + rel - 1
            sends.append(_remote(small_in, small_out.at[me], sem(k), (px, py, pc)))
            recvs.append(_remote(small_in, small_out.at[4 * px + 2 * py + pc], sem(k), (px, py, pc)))
        for cp in sends:
            cp.start()
        for cp in recvs:
            cp.wait_recv()
        for cp in sends:
            cp.wait_send()
        local.wait()

    n_sems = extra.n_sems + N_DEVICES - 1
    res = pl.pallas_call(
        body, name=name, out_shape=extra.outs + [jax.ShapeDtypeStruct((N_DEVICES,) + small.shape, small.dtype)],
        in_specs=[ANY] * (n_in + 1), out_specs=[ANY] * (n_out + 1), input_output_aliases=extra.aliases,
        scratch_shapes=[pltpu.SemaphoreType.DMA((n_sems,)), pltpu.SemaphoreType.DMA((n_sems,)),
                        pltpu.SemaphoreType.DMA((1,))],
    )(*extra.ins, small)
    return list(res[:n_out]), res[n_out]


def _remote(src, dst, sems, device):
    return pltpu.make_async_remote_copy(src_ref=src, dst_ref=dst, send_sem=sems[0], recv_sem=sems[1],
                                        device_id=device, device_id_type=pl.DeviceIdType.MESH)


def _sum_piece(dest, slots, grad, blk, axis, layer, n_layers, name):
    r, c = blk
    h0 = r // 2
    tr = _tile(h0, 256, SUBLANES_BF16)
    tc = _tile(c, 2048, LANES)
    place = jnp.stack([2 * lax.axis_index("x") + lax.axis_index("y"), lax.axis_index("c")]).astype(jnp.int32)

    def body(p_ref, s_ref, g_ref, *rest):
        acc = g_ref[...].astype(F32) + s_ref[0].astype(F32)
        for k in range(1, N_DEVICES - 1):
            acc = acc + s_ref[k].astype(F32)
        rest[-1][...] = acc

    def g_map(i, j, p):
        return (p[1] * (h0 // tr) + (p[0] * (r // tr) if axis == 0 else 0) + i, (p[0] * (c // tc) if axis == 1 else 0) + j)

    in_specs = [pl.BlockSpec((N_DEVICES - 1, tr, tc), lambda i, j, p: (0, i, j)), pl.BlockSpec((tr, tc), g_map)]
    args = [place, slots, grad]
    if dest is not None:
        in_specs.append(ANY)
        args.append(dest)
    return pl.pallas_call(
        body, name=name, out_shape=jax.ShapeDtypeStruct((n_layers, r, c), F32),
        grid_spec=pltpu.PrefetchScalarGridSpec(
            num_scalar_prefetch=1, grid=(h0 // tr, c // tc), in_specs=in_specs,
            out_specs=pl.BlockSpec((None, tr, tc), lambda i, j, p: (layer, p[1] * (h0 // tr) + i, j))),
        input_output_aliases={3: 0} if dest is not None else {},
        compiler_params=_params("parallel", "parallel"),
    )(*args)


def _swap_halves_task(blocks):
    n = len(blocks)
    layers = [(a, l) for a, b in enumerate(blocks) for l in range(b.shape[0])]

    def copies(in_refs, out_refs, sem):
        x, y, c = _my_place()
        sends, recvs = [], []
        for k, (a, l) in enumerate(layers):
            blk = blocks[a].shape[1:]
            mine = _half_of(out_refs[a].at[l], blk, c)
            sends.append(_remote(mine, mine, sem(k), (x, y, 1 - c)))
            recvs.append(_remote(mine, _half_of(out_refs[a].at[l], blk, 1 - c), sem(k), (x, y, 1 - c)))
        return sends, recvs

    return _Comm(blocks, [jax.ShapeDtypeStruct(b.shape, b.dtype) for b in blocks], {a: a for a in range(n)},
                 len(layers), copies)


def _place_shard(shard, axis, name):
    r, c = shard.shape
    tr = _tile(r, 512, SUBLANES_BF16)
    full = (r * N_SHARDS, c) if axis == 0 else (r, c * N_SHARDS)
    me2 = (2 * lax.axis_index("x") + lax.axis_index("y")).astype(jnp.int32).reshape(1)

    def body(me_ref, s_ref, o_ref):
        o_ref[...] = s_ref[...].astype(o_ref.dtype)

    if axis == 0:
        out_map = lambda i, me: (me[0] * (r // tr) + i, 0)
    else:
        out_map = lambda i, me: (i, me[0])
    return pl.pallas_call(
        body, name=name, out_shape=jax.ShapeDtypeStruct(full, BF16),
        grid_spec=pltpu.PrefetchScalarGridSpec(
            num_scalar_prefetch=1, grid=(r // tr,),
            in_specs=[pl.BlockSpec((tr, c), lambda i, me: (i, 0))], out_specs=pl.BlockSpec((tr, c), out_map)),
        compiler_params=_params("parallel"),
    )(me2, shard)


def _gather_ici_task(shards, axes, names):
    n = len(shards)
    blks = [s.shape for s in shards]
    bases = [_place_shard(s, ax, f"place_{nm}") for s, ax, nm in zip(shards, axes, names)]

    def copies(in_refs, out_refs, sem):
        x, y, c = _my_place()
        me = 2 * x + y
        sends, recvs = [], []
        for a in range(n):
            mine = _piece(out_refs[a], blks[a], axes[a], me, c)
            for k, (px, py) in enumerate(_other_chips(x, y)):
                sends.append(_remote(mine, mine, sem(3 * a + k), (px, py, c)))
                recvs.append(_remote(mine, _piece(out_refs[a], blks[a], axes[a], 2 * px + py, c), sem(3 * a + k),
                                     (px, py, c)))
        return sends, recvs

    return _Comm(bases, [jax.ShapeDtypeStruct(b.shape, b.dtype) for b in bases], {a: a for a in range(n)}, 3 * n, copies)


def _gather_d2d_task(partials, blks, axes):
    n = len(partials)

    def copies(in_refs, out_refs, sem):
        x, y, c = _my_place()
        sends, recvs = [], []
        for a in range(n):
            for k, (px, py) in enumerate(_other_chips(x, y)):
                mine = _piece(out_refs[a], blks[a], axes[a], 2 * px + py, c)
                theirs = _piece(out_refs[a], blks[a], axes[a], 2 * px + py, 1 - c)
                sends.append(_remote(mine, mine, sem(3 * a + k), (x, y, 1 - c)))
                recvs.append(_remote(mine, theirs, sem(3 * a + k), (x, y, 1 - c)))
        return sends, recvs

    return _Comm(partials, [jax.ShapeDtypeStruct(p.shape, p.dtype) for p in partials], {a: a for a in range(n)},
                 3 * n, copies)


def _scatter_task(grads, blks, axes):
    n = len(grads)

    def copies(in_refs, out_refs, sem):
        x, y, c = _my_place()
        sends, recvs = [], []
        for rel in range(1, N_DEVICES):
            px, py, pc = x ^ ((rel >> 2) & 1), y ^ ((rel >> 1) & 1), c ^ (rel & 1)
            for a in range(n):
                src = _piece(in_refs[a], blks[a], axes[a], 2 * px + py, pc)
                k = (N_DEVICES - 1) * a + rel - 1
                sends.append(_remote(src, out_refs[a].at[rel - 1], sem(k), (px, py, pc)))
                recvs.append(_remote(src, out_refs[a].at[rel - 1], sem(k), (px, py, pc)))
        return sends, recvs

    outs = [jax.ShapeDtypeStruct((N_DEVICES - 1, b[0] // 2) + tuple(b[1:]), g.dtype) for b, g in zip(blks, grads)]
    return _Comm(grads, outs, {}, (N_DEVICES - 1) * n, copies)


def _pack(arrays):
    flat = jnp.concatenate([a.reshape(-1).astype(F32) for a in arrays])
    pad = (-flat.shape[0]) % (SUBLANES_F32 * LANES)
    return jnp.pad(flat, (0, pad)).reshape(-1, LANES)


def _unpack(packed, shapes):
    flat = packed.reshape(-1)
    out, off = [], 0
    for s in shapes:
        n = int(np.prod(s))
        out.append(flat[off:off + n].reshape(s))
        off += n
    return out


def _local_step(x, target, W, shards, geom, small):
    W = dict(W)
    B, S, D = x.shape
    T = B * S
    x2 = x.reshape(T, D)
    tgt = target.reshape(T, D)
    bmap = jnp.asarray(_bucket_map())
    blk = lambda names: [geom[k][0] for k in names]
    axs = lambda names: [geom[k][1] for k in names]
    ici = lambda names: _gather_ici_task([shards[k] for k in names], axs(names), names)
    d2d = lambda names, partials: _gather_d2d_task(list(partials), blk(names), axs(names))
    big, slots = {}, {}

    def scatter(names):
        return _scatter_task([big[k] for k in names], blk(names), axs(names))

    def ffn_bwd(l, dout, h, saved, first, second):
        xn, u, v, a = saved
        da = _mm_nt(dout, W[f"w_down{l}"], BF16, f"ffn{l}_down_dx")
        big[f"w_down{l}"] = _mm_tn(a, dout, f"ffn{l}_down_dw")
        names = first + [f"w_down{l}"]
        du, g_cw, g_cb, landed = _ffn_gate_bwd(u, v, small["ffn_conv"][l], da, S, f"ffn{l}_gate_bwd", comm=scatter(names))
        slots.update(zip(names, landed))
        big[f"w_up{l}"] = _mm_tn(xn, du, f"ffn{l}_up_dw")
        norm = (h, small["ffn_norm"][l], dout)
        if second:
            (dh, g_norm), (slots[f"w_up{l}"],) = _mm_nt(du, W[f"w_up{l}"], None, f"ffn{l}_up_dx",
                                                        comm=scatter([f"w_up{l}"]), norm=norm)
        else:
            dh, g_norm = _mm_nt(du, W[f"w_up{l}"], None, f"ffn{l}_up_dx", norm=norm)
        return dh, g_cw, g_cb, g_norm[0]

    xn0 = _rmsnorm_fwd(x2, small["a_norm"][0], "a_norm")
    p, part = _mm_nn(xn0, W["w_in"], None, BF16, "a_in", comm=ici(["w_up0"]))
    z, (W["w_up0"],) = _gate_a_fwd(p, small["a_conv"][0], S, "a_gate", comm=d2d(["w_up0"], part))
    (h1, xn1), part = _mm_nn(z, W["w_out"], x2, F32, "a_out", comm=ici(["w_down0"]), norm_gains=[small["ffn_norm"][0]])
    later = ["w_kv", "w_q", "w_o", "w_up1"]
    u0, landed = _mm_nn(xn1, W["w_up0"], None, BF16, "ffn0_up", comm=_Comm.join([d2d(["w_down0"], part), ici(later)]))
    W["w_down0"] = landed[0]
    (a0, v0), landed = _ffn_gate_fwd(u0, small["ffn_conv"][0], small["ffn_conv_b"][0], S, "ffn0_gate",
                               comm=_Comm.join([d2d(later, landed[1:]), ici(["w_down1"])]))
    W.update(zip(later, landed[:len(later)]))
    (h2, kvn, xn3), (W["w_down1"],) = _mm_nn(a0, W["w_down0"], h1, F32, "ffn0_down",
                                            comm=d2d(["w_down1"], landed[len(later):]),
                                            norm_gains=[small["kv_norm"], small["b_norm"][0]])
    kv = _mm_nn(kvn, W["w_kv"], None, F32, "kv_proj")
    q = _mm_nn(xn3, W["w_q"], None, F32, "q_proj")
    bias = _bias_tables(small["rel_bias"], bmap, "rel_bias_tables")
    q3, kv3 = q.reshape(B, S, D), kv.reshape(B, S, 2 * D)
    o3, lse3 = _attn_fwd(q3, kv3, bias, "attn_fwd")
    o = o3.reshape(T, D)
    h3, xn4 = _mm_nn(o, W["w_o"], h2, F32, "o_proj", norm_gains=[small["ffn_norm"][1]])
    u1 = _mm_nn(xn4, W["w_up1"], None, BF16, "ffn1_up")
    a1, v1 = _ffn_gate_fwd(u1, small["ffn_conv"][1], small["ffn_conv_b"][1], S, "ffn1_gate")
    sq_err, dh4, g_final = _loss_head(a1, W["w_down1"], h3, small["final_norm"], tgt, "ffn1_down_loss")
    loss = 0.5 * jnp.sum(sq_err) / D

    dh3, g_cw1, g_cb1, g_fn1 = ffn_bwd(1, dh4, h3, (xn4, u1, v1, a1), [], False)
    do, delta = _mm_nt(dh3, W["w_o"], F32, "o_proj_dx", head_dot=o)
    big["w_o"] = _mm_tn(o, dh3, "o_proj_dw")
    (dq3, dk3, dv3, dbias), landed = _attn_bwd(q3, kv3, delta.reshape(B, S, D), lse3, do.reshape(B, S, D), bias, "attn_bwd",
                                               comm=scatter(["w_up1", "w_o"]))
    slots.update(zip(["w_up1", "w_o"], landed))
    g_rel = _bias_grad(dbias, bmap, "rel_bias_grad")[:, :REL_BUCKETS].T
    dq, dk, dv = dq3.reshape(T, D), dk3.reshape(T, D), dv3.reshape(T, D)
    dh2, g_bn = _mm_nt(dq, W["w_q"], None, "q_proj_dx", norm=(h2, small["b_norm"][0], dh3))
    big["w_q"] = _mm_tn(xn3, dq, "q_proj_dw")
    dh2, g_kvn = _mm_nt([dk, dv], W["w_kv"], None, "kv_proj_dx", norm=(h2, small["kv_norm"], dh2))
    big["w_kv"] = jnp.concatenate([_mm_tn(kvn, dk, "k_proj_dw"), _mm_tn(kvn, dv, "v_proj_dw")], axis=1)
    dh1, g_cw0, g_cb0, g_fn0 = ffn_bwd(0, dh2, h1, (xn1, u0, v0, a0), ["w_q", "w_kv"], True)
    dz = _mm_nt(dh1, W["w_out"], BF16, "a_out_dx")
    big["w_out"] = _mm_tn(z, dh1, "a_out_dw")
    dp, g_aconv, (slots["w_out"],) = _gate_a_bwd(p, small["a_conv"][0], dz, S, "a_gate_bwd", comm=scatter(["w_out"]))
    big["w_in"] = _mm_tn(xn0, dp, "a_in_dw")
    (dx, g_an), (slots["w_in"],) = _mm_nt(dp, W["w_in"], None, "a_in_dx", comm=scatter(["w_in"]),
                                          norm=(x2, small["a_norm"][0], dh1))
    g_bn, g_kvn, g_an = g_bn[0], g_kvn[0], g_an[0]

    small_g = {"a_norm": g_an[None], "a_conv": g_aconv[None], "kv_norm": g_kvn, "b_norm": g_bn[None],
               "rel_bias": g_rel, "ffn_norm": jnp.stack([g_fn0, g_fn1]), "ffn_conv": jnp.stack([g_cw0, g_cw1]),
               "ffn_conv_b": jnp.stack([g_cb0, g_cb1]), "final_norm": g_final}
    return loss, dx.reshape(B, S, D), big, slots, small_g


BIG = ("w_in", "w_out", "w_kv", "w_q", "w_o", "w_up0", "w_up1", "w_down0", "w_down1")
SMALL = ("a_norm", "a_conv", "kv_norm", "b_norm", "rel_bias", "ffn_norm", "ffn_conv", "ffn_conv_b", "final_norm")
SMALL_SHARDED = ("a_norm", "a_conv", "ffn_conv")
WEIGHT_ORDER = ("a_norm", "a_w_in", "a_conv", "a_w_out", "kv_norm", "w_kv", "b_norm", "b_w_q", "b_w_o", "rel_bias",
                "ffn_norm", "ffn_w_up", "ffn_conv", "ffn_conv_b", "ffn_w_down", "final_norm")
GRAD_OF = {"w_in": ("a_w_in", 0), "w_out": ("a_w_out", 0), "w_kv": ("w_kv", 0), "w_q": ("b_w_q", 0), "w_o": ("b_w_o", 0),
           "w_up0": ("ffn_w_up", 0), "w_up1": ("ffn_w_up", 1), "w_down0": ("ffn_w_down", 0), "w_down1": ("ffn_w_down", 1)}


def _as2d(a):
    return a.reshape(-1, a.shape[-1])


def kernel(x, a_norm, a_w_in, a_conv, a_w_out, kv_norm, w_kv, b_norm, b_w_q, b_w_o, rel_bias, ffn_norm, ffn_w_up, ffn_conv, ffn_conv_b, ffn_w_down, final_norm, loss_target, m_a_norm, m_a_w_in, m_a_conv, m_a_w_out, m_kv_norm, m_w_kv, m_b_norm, m_b_w_q, m_b_w_o, m_rel_bias, m_ffn_norm, m_ffn_w_up, m_ffn_conv, m_ffn_conv_b, m_ffn_w_down, m_final_norm, v_a_norm, v_a_w_in, v_a_conv, v_a_w_out, v_kv_norm, v_w_kv, v_b_norm, v_b_w_q, v_b_w_o, v_rel_bias, v_ffn_norm, v_ffn_w_up, v_ffn_conv, v_ffn_conv_b, v_ffn_w_down, v_final_norm):
    given = dict(a_norm=a_norm, a_w_in=a_w_in, a_conv=a_conv, a_w_out=a_w_out, kv_norm=kv_norm, w_kv=w_kv, b_norm=b_norm,
                 b_w_q=b_w_q, b_w_o=b_w_o, rel_bias=rel_bias, ffn_norm=ffn_norm, ffn_w_up=ffn_w_up, ffn_conv=ffn_conv,
                 ffn_conv_b=ffn_conv_b, ffn_w_down=ffn_w_down, final_norm=final_norm)
    mom_m = dict(a_norm=m_a_norm, a_w_in=m_a_w_in, a_conv=m_a_conv, a_w_out=m_a_w_out, kv_norm=m_kv_norm, w_kv=m_w_kv,
                 b_norm=m_b_norm, b_w_q=m_b_w_q, b_w_o=m_b_w_o, rel_bias=m_rel_bias, ffn_norm=m_ffn_norm,
                 ffn_w_up=m_ffn_w_up, ffn_conv=m_ffn_conv, ffn_conv_b=m_ffn_conv_b, ffn_w_down=m_ffn_w_down,
                 final_norm=m_final_norm)
    mom_v = dict(a_norm=v_a_norm, a_w_in=v_a_w_in, a_conv=v_a_conv, a_w_out=v_a_w_out, kv_norm=v_kv_norm, w_kv=v_w_kv,
                 b_norm=v_b_norm, b_w_q=v_b_w_q, b_w_o=v_b_w_o, rel_bias=v_rel_bias, ffn_norm=v_ffn_norm,
                 ffn_w_up=v_ffn_w_up, ffn_conv=v_ffn_conv, ffn_conv_b=v_ffn_conv_b, ffn_w_down=v_ffn_w_down,
                 final_norm=v_final_norm)

    shard = {"w_in": (a_w_in[0], 1), "w_out": (a_w_out[0], 0), "w_kv": (w_kv, 1), "w_q": (b_w_q[0], 0),
             "w_o": (b_w_o[0], 0), "w_up0": (ffn_w_up[0], 1), "w_up1": (ffn_w_up[1], 1),
             "w_down0": (ffn_w_down[0], 0), "w_down1": (ffn_w_down[1], 0)}

    small_sharded = [given[k] for k in SMALL_SHARDED]
    packed = _pack(small_sharded)
    first = ("w_in", "w_out")
    fulls, packed_all = _gather_weights([shard[k][0] for k in first], [shard[k][1] for k in first], first, packed)
    W = dict(zip(first, fulls))
    later = {k: shard[k][0] for k in BIG if k not in first}
    geom = {k: (shard[k][0].shape, shard[k][1]) for k in BIG}
    small = {k: given[k] for k in SMALL}
    per_shard = [_unpack(packed_all[j], [a.shape for a in small_sharded]) for j in range(N_SHARDS)]
    for i, k in enumerate(SMALL_SHARDED):
        small[k] = jnp.concatenate([per_shard[j][i] for j in range(N_SHARDS)], axis=-1)

    loss, grad_x, big_g, slots, small_g = _local_step(x, loss_target, W, later, geom, small)
    loss = lax.psum(loss, MESH_AXES)

    small_shapes = [small_g[k].shape for k in SMALL]
    layers_of = {}
    for k in BIG:
        layers_of.setdefault(GRAD_OF[k][0], []).append(k)
    blocks = {}
    for name, members in layers_of.items():
        dest = None
        for k in members:
            dest = _sum_piece(dest, slots[k], big_g[k], geom[k][0], geom[k][1], GRAD_OF[k][1], len(members), f"sum_{k}")
        blocks[name] = dest
    swapped, small_slots = _exchange_small(_pack([small_g[k] for k in SMALL]), _swap_halves_task(list(blocks.values())),
                                           "swap_halves_exchange_small")
    reduced = dict(zip(blocks, swapped))
    small_sum = _sum_slots(small_slots, "sum_small")
    small_red = dict(zip(SMALL, _unpack(small_sum, small_shapes)))
    j = 2 * lax.axis_index("x") + lax.axis_index("y")
    for k in SMALL_SHARDED:
        w = given[k].shape[-1]
        small_red[k] = lax.dynamic_slice_in_dim(small_red[k], j * w, w, axis=small_red[k].ndim - 1)

    grads, deltas, new_m, new_v = {}, {}, {}, {}
    for name in WEIGHT_ORDER:
        if name in reduced:
            g = reduced[name].reshape(given[name].shape)
            d, nm, nv = _adamw(_as2d(given[name]), _as2d(g), _as2d(mom_m[name]), _as2d(mom_v[name]), f"adamw_{name}")
            grads[name] = g
            deltas[name], new_m[name], new_v[name] = (t.reshape(given[name].shape) for t in (d, nm, nv))
    small_names = [n for n in WEIGHT_ORDER if n not in reduced]
    for n in small_names:
        grads[n] = small_red[n].reshape(given[n].shape)
    sw, sg, sm, sv = (_pack([d[n] for n in small_names]) for d in (given, grads, mom_m, mom_v))
    d, nm, nv = _adamw(sw, sg, sm, sv, "adamw_small")
    shapes = [given[n].shape for n in small_names]
    for n, a, b_, c_ in zip(small_names, _unpack(d, shapes), _unpack(nm, shapes), _unpack(nv, shapes)):
        deltas[n], new_m[n], new_v[n] = a, b_, c_

    return (loss, grad_x, *[grads[n] for n in WEIGHT_ORDER], *[deltas[n] for n in WEIGHT_ORDER],
            *[new_m[n] for n in WEIGHT_ORDER], *[new_v[n] for n in WEIGHT_ORDER])
```

```python
import functools
import math

import numpy as np

import jax
import jax.numpy as jnp
from jax import lax
from jax.experimental import pallas as pl
from jax.experimental.pallas import tpu as pltpu

F32 = jnp.float32
BF16 = jnp.bfloat16

RMS_EPS = 1e-6
HEAD_DIM = 64
ATT_BLOCK = 128
DILATED_BRANCHES = ((128, 1), (512, 4), (2048, 16))
REL_BUCKETS = 32
REL_MAX_DISTANCE = 2048
MASKED_LOGIT = -1e30
ATTN_FWD_UNROLL = (8, 8, 4)
ATTN_BWD_UNROLL = (8, 8, 8)

ADAM_LR = 0.001
ADAM_B1 = 0.9
ADAM_B2 = 0.999
ADAM_EPS = 1e-08
ADAM_WD = 0.01
ADAM_STEP = 10

LANES = 128
SUBLANES_F32 = 8
SUBLANES_BF16 = 16
VMEM_LIMIT_BYTES = 56 * 1024 * 1024

MESH_AXES = ("x", "y", "c")
N_SHARDS = 4
N_DEVICES = 8
ANY = pl.BlockSpec(memory_space=pl.ANY)


def _tile(n, pref, mult):
    best = None
    for t in range(mult, min(n, pref) + 1, mult):
        if n % t == 0:
            best = t
    if best is None:
        raise ValueError(f"no tile for {n} (multiple of {mult}, at most {pref})")
    return best


def _params(*sem):
    return pltpu.CompilerParams(dimension_semantics=sem, vmem_limit_bytes=VMEM_LIMIT_BYTES)


class _Comm:
    def __init__(self, ins, outs, aliases, n_sems, copies):
        self.ins, self.outs, self.aliases, self.n_sems, self.copies = list(ins), list(outs), dict(aliases), n_sems, copies

    @staticmethod
    def join(parts):
        parts = [p for p in parts if p is not None]
        ins, outs, aliases, offs, n_sems = [], [], {}, [], 0
        for p in parts:
            offs.append((len(ins), len(outs), n_sems))
            aliases.update({len(ins) + i: len(outs) + o for i, o in p.aliases.items()})
            ins += p.ins
            outs += p.outs
            n_sems += p.n_sems

        def copies(in_refs, out_refs, sem):
            sends, recvs = [], []
            for p, (i0, o0, s0) in zip(parts, offs):
                s, r = p.copies(in_refs[i0:i0 + len(p.ins)], out_refs[o0:o0 + len(p.outs)],
                                lambda k, s0=s0: sem(s0 + k))
                sends += s
                recvs += r
            return sends, recvs

        return _Comm(ins, outs, aliases, n_sems, copies)


def _pallas(body, *, comm=None, name, out_shape, grid=(), in_specs=(), out_specs=(), scratch_shapes=(),
            compiler_params=None):
    if comm is None:
        return pl.pallas_call(body, name=name, out_shape=out_shape, grid=grid, in_specs=in_specs, out_specs=out_specs,
                              scratch_shapes=scratch_shapes, compiler_params=compiler_params)
    single = not isinstance(out_shape, (tuple, list))
    outs = (out_shape,) if single else tuple(out_shape)
    o_specs = (out_specs,) if single else tuple(out_specs)
    n_in, n_cin, n_out, n_cout, n_scr = len(in_specs), len(comm.ins), len(outs), len(comm.outs), len(scratch_shapes)

    def carried(*refs):
        base_in, c_in = refs[:n_in], refs[n_in:n_in + n_cin]
        o0 = n_in + n_cin
        base_out, c_out = refs[o0:o0 + n_out], refs[o0 + n_out:o0 + n_out + n_cout]
        s0 = o0 + n_out + n_cout
        base_scr, (send_sems, recv_sems) = refs[s0:s0 + n_scr], refs[s0 + n_scr:]
        sem = lambda k: (send_sems.at[k], recv_sems.at[k])
        first = functools.reduce(jnp.logical_and, [pl.program_id(a) == 0 for a in range(len(grid))])
        last = functools.reduce(jnp.logical_and, [pl.program_id(a) == n - 1 for a, n in enumerate(grid)])

        @pl.when(first)
        def _():
            for cp in comm.copies(c_in, c_out, sem)[0]:
                cp.start()

        body(*base_in, *base_out, *base_scr)

        @pl.when(last)
        def _():
            sends, recvs = comm.copies(c_in, c_out, sem)
            for cp in recvs:
                cp.wait_recv()
            for cp in sends:
                cp.wait_send()

    call = pl.pallas_call(
        carried, name=name, out_shape=outs + tuple(comm.outs), grid=grid,
        in_specs=list(in_specs) + [ANY] * n_cin, out_specs=o_specs + (ANY,) * n_cout,
        scratch_shapes=list(scratch_shapes) + [pltpu.SemaphoreType.DMA((comm.n_sems,))] * 2,
        input_output_aliases={n_in + i: n_out + o for i, o in comm.aliases.items()},
        compiler_params=_params(*(["arbitrary"] * len(grid))))

    def run(*args):
        res = call(*args, *comm.ins)
        base = res[0] if single else tuple(res[:n_out])
        return base, list(res[n_out:])

    return run


def _rmsnorm_fwd(x, g, name):
    T, D = x.shape
    tm = _tile(T, 512, SUBLANES_BF16)

    def body(x_ref, g_ref, o_ref):
        xf = x_ref[...]
        r = lax.rsqrt(jnp.mean(xf * xf, axis=-1, keepdims=True) + RMS_EPS)
        o_ref[...] = ((xf * r) * g_ref[...]).astype(o_ref.dtype)

    return pl.pallas_call(
        body, name=name, out_shape=jax.ShapeDtypeStruct((T, D), BF16), grid=(T // tm,),
        in_specs=[pl.BlockSpec((tm, D), lambda i: (i, 0)), pl.BlockSpec((1, D), lambda i: (0, 0))],
        out_specs=pl.BlockSpec((tm, D), lambda i: (i, 0)),
        compiler_params=_params("parallel"),
    )(x, g.reshape(1, D))


def _loss_head(a, b, res, g, target, name):
    T, K = a.shape
    D = b.shape[1]
    tm = _tile(T, 512, SUBLANES_BF16)

    def body(a_ref, b_ref, res_ref, g_ref, t_ref, dh_ref, acc_ref):
        @pl.when(pl.program_id(0) == 0)
        def _():
            acc_ref[...] = jnp.zeros_like(acc_ref)

        xf = jnp.dot(a_ref[...].astype(BF16), b_ref[...], preferred_element_type=F32) + res_ref[...]
        r = lax.rsqrt(jnp.mean(xf * xf, axis=-1, keepdims=True) + RMS_EPS)
        xhat = xf * r
        err = xhat * g_ref[...] - t_ref[...]
        dy = err * (1.0 / D)
        acc_ref[0:1, :] += jnp.sum(dy * xhat, axis=0, keepdims=True)
        acc_ref[1:2, :] += jnp.sum(err * err, axis=0, keepdims=True)
        t = dy * g_ref[...]
        dh_ref[...] = r * (t - xhat * jnp.mean(t * xhat, axis=-1, keepdims=True))

    row = pl.BlockSpec((tm, D), lambda i: (i, 0))
    dh, acc = pl.pallas_call(
        body, name=name,
        out_shape=(jax.ShapeDtypeStruct((T, D), F32), jax.ShapeDtypeStruct((SUBLANES_F32, D), F32)),
        grid=(T // tm,),
        in_specs=[pl.BlockSpec((tm, K), lambda i: (i, 0)), pl.BlockSpec((K, D), lambda i: (0, 0)), row,
                  pl.BlockSpec((1, D), lambda i: (0, 0)), row],
        out_specs=(row, pl.BlockSpec((SUBLANES_F32, D), lambda i: (0, 0))),
        compiler_params=_params("arbitrary"),
    )(a, b, res, g.reshape(1, D), target)
    return acc[1], dh, acc[0]


def _mm_nn(a, b, res, out_dtype, name, comm=None, norm_gains=()):
    T, K = a.shape
    N = b.shape[1]
    n_g = len(norm_gains)
    tm = _tile(T, 512 if n_g else 1024, SUBLANES_BF16)
    tn = _tile(N, 3072, LANES)
    assert not n_g or tn == N, "the fused rmsnorm needs whole rows in one tile"

    def body(a_ref, b_ref, *rest):
        ins, outs = rest[:len(rest) - 1 - n_g], rest[len(rest) - 1 - n_g:]
        acc = jnp.dot(a_ref[...].astype(BF16), b_ref[...], preferred_element_type=F32)
        if res is not None:
            acc = acc + ins[0][...]
        outs[0][...] = acc.astype(outs[0].dtype)
        if n_g:
            y = acc * lax.rsqrt(jnp.mean(acc * acc, axis=-1, keepdims=True) + RMS_EPS)
            for g_ref, xn_ref in zip(ins[len(ins) - n_g:], outs[1:]):
                xn_ref[...] = (y * g_ref[...]).astype(xn_ref.dtype)

    tile = pl.BlockSpec((tm, tn), lambda j, i: (i, j))
    in_specs = [pl.BlockSpec((tm, K), lambda j, i: (i, 0)), pl.BlockSpec((K, tn), lambda j, i: (0, j))]
    args = [a, b]
    if res is not None:
        in_specs.append(tile)
        args.append(res)
    for g in norm_gains:
        in_specs.append(pl.BlockSpec((1, N), lambda j, i: (0, 0)))
        args.append(g.reshape(1, N))
    out_shape = jax.ShapeDtypeStruct((T, N), out_dtype)
    if n_g:
        out_shape = (out_shape,) + (jax.ShapeDtypeStruct((T, N), BF16),) * n_g
    return _pallas(
        body, comm=comm, name=name, out_shape=out_shape, grid=(N // tn, T // tm),
        in_specs=in_specs, out_specs=(tile,) * (1 + n_g) if n_g else tile,
        compiler_params=_params("parallel", "parallel"),
    )(*args)


def _mm_nt(dy, b, out_dtype, name, comm=None, norm=None, head_dot=None):
    dys = list(dy) if isinstance(dy, (list, tuple)) else [dy]
    T, n_each = dys[0].shape
    K = b.shape[0]
    tm = _tile(T, 2048 if norm is None and head_dot is None else 1024, SUBLANES_BF16)
    tk = _tile(K, 1536, LANES)
    tn = _tile(n_each, 2816 if norm is None else 1536, LANES)
    per = n_each // tn
    n_steps = per * len(dys)
    assert norm is None or tk == K, "the fused rmsnorm backward needs whole rows in one tile"

    def body(*refs):
        dy_refs, b_ref, acc_ref = refs[:len(dys)], refs[len(dys)], refs[-1]
        i, n = pl.program_id(0), pl.program_id(2)

        @pl.when(n == 0)
        def _():
            acc_ref[...] = jnp.zeros_like(acc_ref)

        for p, dy_ref in enumerate(dy_refs):
            @pl.when(jnp.logical_and(n >= p * per, n < (p + 1) * per))
            def _(dy_ref=dy_ref):
                acc_ref[...] += lax.dot_general(dy_ref[...].astype(BF16), b_ref[...], (((1,), (1,)), ((), ())),
                                                preferred_element_type=F32)

        if norm is None and head_dot is None:
            @pl.when(n == n_steps - 1)
            def _():
                refs[-2][...] = acc_ref[...].astype(refs[-2].dtype)
        elif norm is None:
            o_ref, out_ref, dot_ref = refs[len(dys) + 1:-1]

            @pl.when(n == n_steps - 1)
            def _():
                d = acc_ref[...]
                out_ref[...] = d.astype(out_ref.dtype)
                head0 = lax.broadcasted_iota(jnp.int32, (tm, LANES), 1) < HEAD_DIM
                for c0 in range(0, tk, LANES):
                    x = d[:, c0:c0 + LANES] * o_ref[:, c0:c0 + LANES]
                    d0 = jnp.sum(jnp.where(head0, x, 0.0), axis=-1, keepdims=True)
                    d1 = jnp.sum(jnp.where(head0, 0.0, x), axis=-1, keepdims=True)
                    dot_ref[:, c0:c0 + LANES] = jnp.where(head0, d0, d1)
        else:
            x_ref, g_ref, dres_ref, dx_ref, dg_ref = refs[len(dys) + 1:-1]

            @pl.when(jnp.logical_and(i == 0, n == 0))
            def _():
                dg_ref[...] = jnp.zeros_like(dg_ref)

            @pl.when(n == n_steps - 1)
            def _():
                xf = x_ref[...]
                r = lax.rsqrt(jnp.mean(xf * xf, axis=-1, keepdims=True) + RMS_EPS)
                xhat = xf * r
                d = acc_ref[...]
                dg_ref[0:1, :] += jnp.sum(d * xhat, axis=0, keepdims=True)
                t = d * g_ref[...]
                dx_ref[...] = dres_ref[...] + r * (t - xhat * jnp.mean(t * xhat, axis=-1, keepdims=True))

    in_specs = [pl.BlockSpec((tm, tn), lambda i, k, n, p=p: (i, jnp.clip(n - p * per, 0, per - 1))) for p in range(len(dys))]
    in_specs.append(pl.BlockSpec((tk, tn), lambda i, k, n: (k, n)))
    args = dys + [b]
    tile = pl.BlockSpec((tm, tk), lambda i, k, n: (i, k))
    if norm is None and head_dot is None:
        out_shape, out_specs = jax.ShapeDtypeStruct((T, K), out_dtype), tile
    elif norm is None:
        in_specs.append(tile)
        args.append(head_dot)
        out_shape = (jax.ShapeDtypeStruct((T, K), out_dtype), jax.ShapeDtypeStruct((T, K), F32))
        out_specs = (tile, tile)
    else:
        x, g, dres = norm
        in_specs += [tile, pl.BlockSpec((1, K), lambda i, k, n: (0, 0)), tile]
        args += [x, g.reshape(1, K), dres]
        out_shape = (jax.ShapeDtypeStruct((T, K), F32), jax.ShapeDtypeStruct((SUBLANES_F32, K), F32))
        out_specs = (tile, pl.BlockSpec((SUBLANES_F32, K), lambda i, k, n: (0, 0)))
    return _pallas(
        body, comm=comm, name=name, out_shape=out_shape, grid=(T // tm, K // tk, n_steps),
        in_specs=in_specs, out_specs=out_specs, scratch_shapes=[pltpu.VMEM((tm, tk), F32)],
        compiler_params=_params("parallel", "parallel", "arbitrary") if norm is None else _params(*["arbitrary"] * 3),
    )(*args)


def _mm_tn(a, dy, name):
    dys = list(dy) if isinstance(dy, (list, tuple)) else [dy]
    T, K = a.shape
    n_each = dys[0].shape[1]
    tt = _tile(T, 2048, SUBLANES_BF16)
    tk = _tile(K, 1536, LANES)
    tn = _tile(n_each, 1536, LANES)
    per = n_each // tn
    t_steps = T // tt

    def body(a_ref, *rest):
        dy_refs, (o_ref, acc_ref) = rest[:len(dys)], rest[len(dys):]
        n, t = pl.program_id(1), pl.program_id(2)

        @pl.when(t == 0)
        def _():
            acc_ref[...] = jnp.zeros_like(acc_ref)

        for p, dy_ref in enumerate(dy_refs):
            @pl.when(jnp.logical_and(n >= p * per, n < (p + 1) * per))
            def _(dy_ref=dy_ref):
                acc_ref[...] += lax.dot_general(a_ref[...].astype(BF16), dy_ref[...].astype(BF16),
                                                (((0,), (0,)), ((), ())), preferred_element_type=F32)

        @pl.when(t == t_steps - 1)
        def _():
            o_ref[...] = acc_ref[...].astype(o_ref.dtype)

    dy_specs = [pl.BlockSpec((tt, tn), lambda k, n, t, p=p: (t, jnp.clip(n - p * per, 0, per - 1))) for p in range(len(dys))]
    return pl.pallas_call(
        body, name=name, out_shape=jax.ShapeDtypeStruct((K, n_each * len(dys)), BF16),
        grid=(K // tk, per * len(dys), t_steps),
        in_specs=[pl.BlockSpec((tt, tk), lambda k, n, t: (t, k))] + dy_specs,
        out_specs=pl.BlockSpec((tk, tn), lambda k, n, t: (k, n)),
        scratch_shapes=[pltpu.VMEM((tk, tn), F32)],
        compiler_params=_params("parallel", "parallel", "arbitrary"),
    )(a, *dys)


def _rows_before(halo, cur, k):
    h = halo.shape[0]
    return pltpu.roll(jnp.concatenate([halo, cur], axis=0), k, 0)[h:]


def _rows_after(cur, halo, k):
    n = cur.shape[0]
    total = n + halo.shape[0]
    return pltpu.roll(jnp.concatenate([cur, halo], axis=0), total - k, 0)[:n]


def _halo_specs(tm, width, n_rows):
    per = tm // SUBLANES_BF16
    last = n_rows // SUBLANES_BF16 - 1
    prev = pl.BlockSpec((SUBLANES_BF16, width), lambda i: (jnp.maximum(i * per - 1, 0), 0))
    nxt = pl.BlockSpec((SUBLANES_BF16, width), lambda i: (jnp.minimum((i + 1) * per, last), 0))
    return prev, nxt


def _gate_a_fwd(p, cw, seq, name, comm=None):
    T, D3 = p.shape
    D = D3 // 3
    tm = _tile(seq, 512, SUBLANES_BF16)
    cc = _tile(D, 256, LANES)
    prev, _ = _halo_specs(tm, D3, T)

    def body(p_ref, ph_ref, cw_ref, z_ref):
        at_start = (pl.program_id(0) * tm) % seq == 0
        for c0 in range(0, D, cc):
            b = p_ref[:, c0:c0 + cc].astype(F32)
            u = p_ref[:, D + c0:D + c0 + cc].astype(F32) * p_ref[:, 2 * D + c0:2 * D + c0 + cc].astype(F32)
            uh = ph_ref[:, D + c0:D + c0 + cc].astype(F32) * ph_ref[:, 2 * D + c0:2 * D + c0 + cc].astype(F32)
            uh = jnp.where(at_start, 0.0, uh)
            w = cw_ref[:, c0:c0 + cc]
            cv = _rows_before(uh, u, 2) * w[0:1] + _rows_before(uh, u, 1) * w[1:2] + u * w[2:3]
            z_ref[:, c0:c0 + cc] = (b * cv).astype(z_ref.dtype)

    return _pallas(
        body, comm=comm, name=name, out_shape=jax.ShapeDtypeStruct((T, D), BF16), grid=(T // tm,),
        in_specs=[pl.BlockSpec((tm, D3), lambda i: (i, 0)), prev, pl.BlockSpec((3, D), lambda i: (0, 0))],
        out_specs=pl.BlockSpec((tm, D), lambda i: (i, 0)),
        compiler_params=_params("parallel"),
    )(p, p, cw)


def _gate_a_bwd(p, cw, dz, seq, name, comm=None):
    T, D3 = p.shape
    D = D3 // 3
    tm = _tile(seq, 512, SUBLANES_BF16)
    cc = _tile(D, 256, LANES)
    p_prev, p_next = _halo_specs(tm, D3, T)
    _, dz_next = _halo_specs(tm, D, T)

    def body(p_ref, pp_ref, pn_ref, dz_ref, dzn_ref, cw_ref, dp_ref, dcw_ref):
        i = pl.program_id(0)

        @pl.when(i == 0)
        def _():
            dcw_ref[...] = jnp.zeros_like(dcw_ref)

        at_start = (i * tm) % seq == 0
        at_end = ((i + 1) * tm) % seq == 0
        for c0 in range(0, D, cc):
            cb, cc_, ch = slice(c0, c0 + cc), slice(D + c0, D + c0 + cc), slice(2 * D + c0, 2 * D + c0 + cc)
            b = p_ref[:, cb].astype(F32)
            c = p_ref[:, cc_].astype(F32)
            hh = p_ref[:, ch].astype(F32)
            u = c * hh
            uh = jnp.where(at_start, 0.0, pp_ref[:, cc_].astype(F32) * pp_ref[:, ch].astype(F32))
            w = cw_ref[:, cb]
            u1 = _rows_before(uh, u, 1)
            u2 = _rows_before(uh, u, 2)
            cv = u2 * w[0:1] + u1 * w[1:2] + u * w[2:3]
            dz_t = dz_ref[:, cb].astype(F32)
            dcv = dz_t * b
            dcvn = jnp.where(at_end, 0.0, dzn_ref[:, cb].astype(F32) * pn_ref[:, cb].astype(F32))
            du = dcv * w[2:3] + _rows_after(dcv, dcvn, 1) * w[1:2] + _rows_after(dcv, dcvn, 2) * w[0:1]
            dp_ref[:, cb] = (dz_t * cv).astype(dp_ref.dtype)
            dp_ref[:, cc_] = (du * hh).astype(dp_ref.dtype)
            dp_ref[:, ch] = (du * c).astype(dp_ref.dtype)
            dcw_ref[0:1, cb] += jnp.sum(dcv * u2, axis=0, keepdims=True)
            dcw_ref[1:2, cb] += jnp.sum(dcv * u1, axis=0, keepdims=True)
            dcw_ref[2:3, cb] += jnp.sum(dcv * u, axis=0, keepdims=True)

    res = _pallas(
        body, comm=comm, name=name,
        out_shape=(jax.ShapeDtypeStruct((T, D3), BF16), jax.ShapeDtypeStruct((SUBLANES_F32, D), F32)),
        grid=(T // tm,),
        in_specs=[pl.BlockSpec((tm, D3), lambda i: (i, 0)), p_prev, p_next,
                  pl.BlockSpec((tm, D), lambda i: (i, 0)), dz_next, pl.BlockSpec((3, D), lambda i: (0, 0))],
        out_specs=(pl.BlockSpec((tm, D3), lambda i: (i, 0)), pl.BlockSpec((SUBLANES_F32, D), lambda i: (0, 0))),
        compiler_params=_params("arbitrary"),
    )(p, p, p, dz, dz, cw)
    if comm is None:
        return res[0], res[1][0:3]
    return res[0][0], res[0][1][0:3], res[1]


def _ffn_gate_fwd(u, cw, cb, seq, name, comm=None):
    T, F2 = u.shape
    F = F2 // 2
    tm = _tile(seq, 256, SUBLANES_BF16)
    cc = _tile(F, 256, LANES)
    prev, _ = _halo_specs(tm, F2, T)

    def body(u_ref, uh_ref, cw_ref, cb_ref, a_ref, v_ref):
        at_start = (pl.program_id(0) * tm) % seq == 0

        def conv(c0):
            cols = slice(c0, c0 + cc)
            cur = u_ref[:, cols].astype(F32)
            halo = jnp.where(at_start, 0.0, uh_ref[:, cols].astype(F32))
            w = cw_ref[:, cols]
            return (_rows_before(halo, cur, 2) * w[0:1] + _rows_before(halo, cur, 1) * w[1:2] + cur * w[2:3]
                    + cb_ref[:, cols])

        for c0 in range(0, F, cc):
            g = conv(c0)
            up = conv(F + c0)
            a_ref[:, c0:c0 + cc] = ((g * jax.nn.sigmoid(g)) * up).astype(a_ref.dtype)
            v_ref[:, c0:c0 + cc] = g.astype(v_ref.dtype)
            v_ref[:, F + c0:F + c0 + cc] = up.astype(v_ref.dtype)

    return _pallas(
        body, comm=comm, name=name,
        out_shape=(jax.ShapeDtypeStruct((T, F), BF16), jax.ShapeDtypeStruct((T, F2), BF16)), grid=(T // tm,),
        in_specs=[pl.BlockSpec((tm, F2), lambda i: (i, 0)), prev,
                  pl.BlockSpec((3, F2), lambda i: (0, 0)), pl.BlockSpec((1, F2), lambda i: (0, 0))],
        out_specs=(pl.BlockSpec((tm, F), lambda i: (i, 0)), pl.BlockSpec((tm, F2), lambda i: (i, 0))),
        compiler_params=_params("parallel"),
    )(u, u, cw, cb.reshape(1, F2))


def _ffn_gate_bwd(u, v, cw, da, seq, name, comm=None):
    T, F2 = u.shape
    F = F2 // 2
    H = SUBLANES_BF16
    tm = _tile(seq, 256, H)
    cc = _tile(F, 256, LANES)
    _, v_next = _halo_specs(tm, F2, T)
    _, da_next = _halo_specs(tm, F, T)

    def body(u_ref, v_ref, vn_ref, da_ref, dan_ref, cw_ref, du_ref, acc_ref):
        i = pl.program_id(0)

        @pl.when(i == 0)
        def _():
            acc_ref[...] = jnp.zeros_like(acc_ref)

        at_end = ((i + 1) * tm) % seq == 0
        n = tm + H

        def rows_and_next(ref, nxt, cols):
            return jnp.concatenate([ref[:, cols].astype(F32), nxt[:, cols].astype(F32)], axis=0)

        def back(d, cols):
            w = cw_ref[:, cols]
            d0 = d[:tm]
            d1 = pltpu.roll(d, n - 1, 0)[:tm]
            d2 = pltpu.roll(d, n - 2, 0)[:tm]
            du_ref[:, cols] = (d0 * w[2:3] + d1 * w[1:2] + d2 * w[0:1]).astype(du_ref.dtype)
            ut = u_ref[:, cols].astype(F32)
            acc_ref[0:1, cols] += jnp.sum(d2 * ut, axis=0, keepdims=True)
            acc_ref[1:2, cols] += jnp.sum(d1 * ut, axis=0, keepdims=True)
            acc_ref[2:3, cols] += jnp.sum(d0 * ut, axis=0, keepdims=True)
            acc_ref[3:4, cols] += jnp.sum(d0, axis=0, keepdims=True)

        for c0 in range(0, F, cc):
            gc, uc = slice(c0, c0 + cc), slice(F + c0, F + c0 + cc)
            g = rows_and_next(v_ref, vn_ref, gc)
            up = rows_and_next(v_ref, vn_ref, uc)
            da_ext = jnp.concatenate([da_ref[:, gc].astype(F32),
                                      jnp.where(at_end, 0.0, dan_ref[:, gc].astype(F32))], axis=0)
            sg = jax.nn.sigmoid(g)
            back(da_ext * up * (sg * (1.0 + g * (1.0 - sg))), gc)
            back(da_ext * (g * sg), uc)

    res = _pallas(
        body, comm=comm, name=name,
        out_shape=(jax.ShapeDtypeStruct((T, F2), BF16), jax.ShapeDtypeStruct((SUBLANES_F32, F2), F32)),
        grid=(T // tm,),
        in_specs=[pl.BlockSpec((tm, F2), lambda i: (i, 0)), pl.BlockSpec((tm, F2), lambda i: (i, 0)), v_next,
                  pl.BlockSpec((tm, F), lambda i: (i, 0)), da_next, pl.BlockSpec((3, F2), lambda i: (0, 0))],
        out_specs=(pl.BlockSpec((tm, F2), lambda i: (i, 0)), pl.BlockSpec((SUBLANES_F32, F2), lambda i: (0, 0))),
        compiler_params=_params("arbitrary"),
    )(u, v, v, da, da, cw)
    (du, acc), landed = res if comm is not None else (res, None)
    return (du, acc[0:3], acc[3]) if comm is None else (du, acc[0:3], acc[3], landed)


def _bucket_map():
    P = ATT_BLOCK
    qi = np.arange(P, dtype=np.int64)[:, None]
    kc = np.arange(2 * P, dtype=np.int64)[None, :]
    delta = qi + P - kc
    maps = []
    max_exact = REL_BUCKETS // 2
    for window, dilation in DILATED_BRANCHES:
        band = (delta >= 0) & (delta <= window // dilation)
        n = np.maximum(delta * dilation, 0)
        nf = np.maximum(n, max_exact).astype(np.float32)
        large = max_exact + (np.log(nf / np.float32(max_exact)) / np.float32(math.log(REL_MAX_DISTANCE / max_exact))
                             * np.float32(REL_BUCKETS - max_exact)).astype(np.int32)
        large = np.minimum(large, REL_BUCKETS - 1)
        bucket = np.where(n < max_exact, n, large)
        maps.append(np.where(band, bucket, -1).astype(np.int32))
    return np.stack(maps)


def _bias_tables(rel_bias, bmap, name):
    n_pairs = rel_bias.shape[1] // 2
    nbr, P, P2 = bmap.shape

    def body(rb_ref, bm_ref, o_ref):
        pair = pl.program_id(0)
        in_seq = lax.broadcasted_iota(jnp.int32, (P, P2), 1) >= P
        for br in range(nbr):
            bm = bm_ref[br]
            for hh in range(2):
                acc = jnp.full((P, P2), MASKED_LOGIT, F32)
                for b in range(REL_BUCKETS):
                    acc = jnp.where(bm == b, rb_ref[b, 2 * pair + hh], acc)
                o_ref[br, 0, 0, hh * P:(hh + 1) * P, :] = acc
                o_ref[br, 0, 1, hh * P:(hh + 1) * P, :] = jnp.where(in_seq, acc, MASKED_LOGIT)

    return pl.pallas_call(
        body, name=name, out_shape=jax.ShapeDtypeStruct((nbr, n_pairs, 2, 2 * P, P2), F32), grid=(n_pairs,),
        in_specs=[pl.BlockSpec(memory_space=pltpu.SMEM), pl.BlockSpec((nbr, P, P2), lambda h: (0, 0, 0))],
        out_specs=pl.BlockSpec((nbr, 1, 2, 2 * P, P2), lambda h: (0, h, 0, 0, 0)),
        compiler_params=_params("parallel"),
    )(rel_bias, bmap)


def _bias_grad(dbias, bmap, name):
    nbr, n_pairs, _, P2 = dbias.shape
    P = P2 // 2

    def body(db_ref, bm_ref, o_ref):
        lane = lax.broadcasted_iota(jnp.int32, (1, LANES), 1)
        for hh in range(2):
            row = jnp.zeros((1, LANES), F32)
            for br in range(nbr):
                bm = bm_ref[br]
                d = db_ref[br, 0, hh * P:(hh + 1) * P, :]
                for b in range(REL_BUCKETS):
                    hit = jnp.sum(jnp.where(bm == b, d, 0.0), axis=1, keepdims=True)
                    row = row + jnp.where(lane == b, jnp.sum(hit, axis=0, keepdims=True), 0.0)
            o_ref[hh] = row

    return pl.pallas_call(
        body, name=name, out_shape=jax.ShapeDtypeStruct((2 * n_pairs, 1, LANES), F32), grid=(n_pairs,),
        in_specs=[pl.BlockSpec((nbr, 1, P2, P2), lambda h: (0, h, 0, 0)), pl.BlockSpec((nbr, P, P2), lambda h: (0, 0, 0))],
        out_specs=pl.BlockSpec((2, 1, LANES), lambda h: (h, 0, 0)),
        compiler_params=_params("parallel"),
    )(dbias, bmap)[:, 0, :]


def _rows(start, dilation):
    if dilation == 1:
        return pl.ds(pl.multiple_of(start, ATT_BLOCK), ATT_BLOCK)
    return pl.ds(start, ATT_BLOCK, stride=dilation)


def _for_each_block(seq, unroll, fn):
    P = ATT_BLOCK
    for br, (_, d) in enumerate(DILATED_BRANCHES):
        nb = seq // d // P
        u = unroll[br] if (unroll[br] % nb == 0 or nb % unroll[br] == 0) else 1
        step = d * P

        def some(i, carry, br=br, d=d, nb=nb, u=u, step=step):
            blocks = []
            if u % nb == 0:
                for k in range(u):
                    if k % nb == 0:
                        start = i * (u // nb) + k // nb
                        blocks.append((start, start, 1))
                    else:
                        blocks.append((blocks[-1][0] + step, blocks[-1][0], 0))
            else:
                r, j0 = (i * u) // nb, (i * u) % nb
                blocks.append((r + j0 * step, r + jnp.maximum(j0 - 1, 0) * step, jnp.where(j0 == 0, 1, 0)))
                for _ in range(1, u):
                    blocks.append((blocks[-1][0] + step, blocks[-1][0], 0))
            fn(br, d, blocks)
            return carry

        lax.fori_loop(0, nb * d // u, some, 0)


class _RowCache:
    def __init__(self, dilation):
        self.dilation, self.seen = dilation, {}

    def rows(self, ref, start):
        key = (id(ref), id(start))
        if key not in self.seen:
            self.seen[key] = ref[_rows(start, self.dilation), :].astype(BF16)
        return self.seen[key]

    def window(self, ref, start, prev):
        return jnp.concatenate([self.rows(ref, prev), self.rows(ref, start)], axis=0)


def _stack_heads(x, head0):
    return jnp.concatenate([jnp.where(head0, x, 0.0), jnp.where(head0, 0.0, x)], axis=0).astype(BF16)


def _attn_fwd(q, kv, bias, name):
    B, S, D = q.shape
    P = ATT_BLOCK
    n_pairs = D // LANES
    nbr = len(DILATED_BRANCHES)
    scale = HEAD_DIM ** -0.5

    def body(q_ref, k_ref, v_ref, bias_ref, o_ref, lse_ref, *stats):
        m_s, l_s, acc_s = stats[0:nbr], stats[nbr:2 * nbr], stats[2 * nbr:3 * nbr]
        head0 = lax.broadcasted_iota(jnp.int32, (P, LANES), 1) < HEAD_DIM

        def block(br, d, blocks):
            cache = _RowCache(d)
            s = [lax.dot_general(_stack_heads(q_ref[_rows(start, d), :] * scale, head0),
                                 cache.window(k_ref, start, prev), (((1,), (1,)), ((), ())),
                                 preferred_element_type=F32) + bias_ref[br, 0, first]
                 for start, prev, first in blocks]
            m = [jnp.max(x, axis=-1, keepdims=True) for x in s]
            p = [jnp.exp(x - y) for x, y in zip(s, m)]
            l = [jnp.sum(x, axis=-1, keepdims=True) for x in p]
            pv = [jnp.dot(x.astype(BF16), cache.window(v_ref, start, prev), preferred_element_type=F32)
                  for x, (start, prev, _) in zip(p, blocks)]
            for k, (start, _, _) in enumerate(blocks):
                rows = _rows(start, d)
                m_s[br][rows, :] = jnp.where(head0, m[k][:P], m[k][P:])
                l_s[br][rows, :] = jnp.where(head0, l[k][:P], l[k][P:])
                acc_s[br][rows, :] = jnp.where(head0, pv[k][:P], pv[k][P:])

        _for_each_block(S, ATTN_FWD_UNROLL, block)

        chunk = _tile(S, 256, SUBLANES_F32)

        def merge(i, carry):
            rows = pl.ds(pl.multiple_of(i * chunk, chunk), chunk)
            ms = [m_s[br][rows, :] for br in range(nbr)]
            m = functools.reduce(jnp.maximum, ms)
            l = jnp.zeros((chunk, LANES), F32)
            acc = jnp.zeros((chunk, LANES), F32)
            for br in range(nbr):
                w = jnp.exp(ms[br] - m)
                l = l + w * l_s[br][rows, :]
                acc = acc + w * acc_s[br][rows, :]
            o_ref[rows, :] = acc / l
            lse_ref[rows, :] = m + jnp.log(l)
            return carry

        lax.fori_loop(0, S // chunk, merge, 0)

    slab = lambda col0: pl.BlockSpec((None, S, LANES), lambda b, h: (b, 0, col0 + h))
    return pl.pallas_call(
        body, name=name,
        out_shape=(jax.ShapeDtypeStruct((B, S, D), F32), jax.ShapeDtypeStruct((B, S, D), F32)),
        grid=(B, n_pairs),
        in_specs=[slab(0), slab(0), slab(n_pairs),
                  pl.BlockSpec((nbr, 1, 2, 2 * P, 2 * P), lambda b, h: (0, h, 0, 0, 0))],
        out_specs=(slab(0), slab(0)),
        scratch_shapes=[pltpu.VMEM((S, LANES), F32)] * (3 * nbr),
        compiler_params=_params("parallel", "parallel"),
    )(q, kv, kv, bias)


def _attn_bwd(q, kv, delta, lse, do, bias, name, comm=None):
    B, S, D = q.shape
    P = ATT_BLOCK
    n_pairs = D // LANES
    nbr = len(DILATED_BRANCHES)
    scale = HEAD_DIM ** -0.5

    def body(q_ref, k_ref, v_ref, delta_s, lse_ref, do_ref, bias_ref, dq_ref, dk_ref, dv_ref, dbias_ref):
        head0 = lax.broadcasted_iota(jnp.int32, (P, LANES), 1) < HEAD_DIM

        @pl.when(pl.program_id(1) == 0)
        def _():
            dbias_ref[...] = jnp.zeros_like(dbias_ref)

        chunk = _tile(S, 512, SUBLANES_F32)

        def prepare(i, carry):
            rows = pl.ds(pl.multiple_of(i * chunk, chunk), chunk)
            zero = jnp.zeros((chunk, LANES), F32)
            dq_ref[rows, :] = zero
            dk_ref[rows, :] = zero
            dv_ref[rows, :] = zero
            return carry

        lax.fori_loop(0, S // chunk, prepare, 0)

        def per_head(x):
            return jnp.concatenate([x[:, 0:1], x[:, HEAD_DIM:HEAD_DIM + 1]], axis=0)

        nt = (((1,), (1,)), ((), ()))
        tn = (((0,), (0,)), ((), ()))

        def block(br, d, blocks):
            cache = _RowCache(d)
            q2 = [_stack_heads(q_ref[_rows(start, d), :] * scale, head0) for start, _, _ in blocks]
            do2 = [_stack_heads(do_ref[_rows(start, d), :], head0) for start, _, _ in blocks]
            kb = [cache.window(k_ref, start, prev) for start, prev, _ in blocks]
            vb = [cache.window(v_ref, start, prev) for start, prev, _ in blocks]
            s = [lax.dot_general(a, b, nt, preferred_element_type=F32) + bias_ref[br, 0, first]
                 for a, b, (_, _, first) in zip(q2, kb, blocks)]
            dp = [lax.dot_general(a, b, nt, preferred_element_type=F32) for a, b in zip(do2, vb)]
            p = [jnp.exp(x - per_head(lse_ref[_rows(start, d), :])) for x, (start, _, _) in zip(s, blocks)]
            ds = [x * (y - per_head(delta_s[_rows(start, d), :])) for x, y, (start, _, _) in zip(p, dp, blocks)]
            for x in ds:
                dbias_ref[br, 0] += x
            ds16 = [x.astype(BF16) for x in ds]
            dq2 = [jnp.dot(a, b, preferred_element_type=F32) for a, b in zip(ds16, kb)]
            dk = [lax.dot_general(a, b, tn, preferred_element_type=F32) for a, b in zip(ds16, q2)]
            dv = [lax.dot_general(a.astype(BF16), b, tn, preferred_element_type=F32) for a, b in zip(p, do2)]
            parts = {}
            for k, (start, prev, first) in enumerate(blocks):
                dq_ref[_rows(start, d), :] += jnp.where(head0, dq2[k][:P], dq2[k][P:]) * scale
                parts.setdefault(id(start), [start, []])[1].append((dk[k][P:], dv[k][P:]))
                if not (isinstance(first, int) and first == 1):
                    parts.setdefault(id(prev), [prev, []])[1].append((dk[k][:P], dv[k][:P]))
            for start, terms in parts.values():
                rows = _rows(start, d)
                dk_ref[rows, :] += functools.reduce(jnp.add, [t[0] for t in terms])
                dv_ref[rows, :] += functools.reduce(jnp.add, [t[1] for t in terms])

        _for_each_block(S, ATTN_BWD_UNROLL, block)

    slab = lambda col0: pl.BlockSpec((None, S, LANES), lambda h, b: (b, 0, col0 + h))
    tab = pl.BlockSpec((nbr, 1, 2, 2 * P, 2 * P), lambda h, b: (0, h, 0, 0, 0))
    dtab = pl.BlockSpec((nbr, 1, 2 * P, 2 * P), lambda h, b: (0, h, 0, 0))
    shp = jax.ShapeDtypeStruct((B, S, D), F32)
    return _pallas(
        body, comm=comm, name=name,
        out_shape=(shp, shp, shp, jax.ShapeDtypeStruct((nbr, n_pairs, 2 * P, 2 * P), F32)),
        grid=(n_pairs, B),
        in_specs=[slab(0), slab(0), slab(n_pairs), slab(0), slab(0), slab(0), tab],
        out_specs=(slab(0), slab(0), slab(0), dtab),
        compiler_params=_params("parallel", "arbitrary"),
    )(q, kv, kv, delta, lse, do, bias)


def _adamw(w, g, m, v, name):
    R, C = w.shape
    tr = _tile(R, 256, SUBLANES_F32) if R % SUBLANES_F32 == 0 else R
    tc = _tile(C, 2048, LANES) if C % LANES == 0 else C

    def body(w_ref, g_ref, m_ref, v_ref, d_ref, nm_ref, nv_ref):
        g_ = g_ref[...]
        m2 = ADAM_B1 * m_ref[...] + (1.0 - ADAM_B1) * g_
        v2 = ADAM_B2 * v_ref[...] + (1.0 - ADAM_B2) * (g_ * g_)
        m_hat = m2 / (1.0 - ADAM_B1 ** ADAM_STEP)
        v_hat = v2 / (1.0 - ADAM_B2 ** ADAM_STEP)
        d_ref[...] = -ADAM_LR * (m_hat / (jnp.sqrt(v_hat) + ADAM_EPS) + ADAM_WD * w_ref[...])
        nm_ref[...] = m2
        nv_ref[...] = v2

    blk = pl.BlockSpec((tr, tc), lambda i, j: (i, j))
    shp = jax.ShapeDtypeStruct((R, C), F32)
    return pl.pallas_call(
        body, name=name, out_shape=(shp, shp, shp), grid=(R // tr, C // tc),
        in_specs=[blk] * 4, out_specs=(blk,) * 3, compiler_params=_params("parallel", "parallel"),
    )(w, g, m, v)


def _sum_slots(slots, name):
    n, R, C = slots.shape
    tr = _tile(R, 256, SUBLANES_BF16) if R % SUBLANES_BF16 == 0 else R
    tc = _tile(C, 2048, LANES) if C % LANES == 0 else C

    def body(s_ref, o_ref):
        acc = s_ref[0].astype(F32)
        for k in range(1, n):
            acc = acc + s_ref[k].astype(F32)
        o_ref[...] = acc

    return pl.pallas_call(
        body, name=name, out_shape=jax.ShapeDtypeStruct((R, C), F32), grid=(R // tr, C // tc),
        in_specs=[pl.BlockSpec((n, tr, tc), lambda i, j: (0, i, j))],
        out_specs=pl.BlockSpec((tr, tc), lambda i, j: (i, j)),
        compiler_params=_params("parallel", "parallel"),
    )(slots)


def _my_place():
    return lax.axis_index("x"), lax.axis_index("y"), lax.axis_index("c")


def _other_chips(x, y):
    return [(1 - x, y), (x, 1 - y), (1 - x, 1 - y)]


def _piece(ref, blk, axis, shard, half):
    h0 = blk[0] // 2
    idx = []
    for dim, n in enumerate(blk):
        if dim == 0:
            start = half * h0 + (shard * n if axis == 0 else 0)
            idx.append(pl.ds(start, h0))
        elif dim == axis:
            idx.append(pl.ds(shard * n, n))
        else:
            idx.append(slice(None))
    return ref.at[tuple(idx)]


def _half_of(ref, blk, half):
    return ref.at[pl.ds(half * (blk[0] // 2), blk[0] // 2)]


def _gather_weights(shards, axes, names, small):
    n = len(shards)
    blks = [_shard_shape(s) for s in shards]
    task = _gather_ici_task(shards, axes, names)

    def stage1(*refs):
        small_in, outs, small_out = refs[n], refs[n + 1:2 * n + 1], refs[2 * n + 1]
        send_sems, recv_sems, local_sems = refs[2 * n + 2:]
        sem = lambda k: (send_sems.at[k], recv_sems.at[k])
        x, y, c = _my_place()
        me = 2 * x + y
        local = pltpu.make_async_copy(small_in, small_out.at[me], local_sems.at[0])
        local.start()
        sends, recvs = task.copies(None, outs, sem)
        for k, (px, py) in enumerate(_other_chips(x, y)):
            sends.append(_remote(small_in, small_out.at[me], sem(task.n_sems + k), (px, py, c)))
            recvs.append(_remote(small_in, small_out.at[2 * px + py], sem(task.n_sems + k), (px, py, c)))
        for cp in sends:
            cp.start()
        for cp in recvs:
            cp.wait_recv()
        for cp in sends:
            cp.wait_send()
        local.wait()

    res = pl.pallas_call(
        stage1, name="gather_weights_ici",
        out_shape=task.outs + [jax.ShapeDtypeStruct((N_SHARDS,) + small.shape, small.dtype)],
        in_specs=[ANY] * (n + 1), out_specs=[ANY] * (n + 1), input_output_aliases={a: a for a in range(n)},
        scratch_shapes=[pltpu.SemaphoreType.DMA((task.n_sems + 3,)), pltpu.SemaphoreType.DMA((task.n_sems + 3,)),
                        pltpu.SemaphoreType.DMA((1,))],
    )(*task.ins, small)
    full = _run_comm(_gather_d2d_task(list(res[:n]), blks, axes), "gather_weights_d2d")
    return full, res[n]


def _run_comm(comm, name):
    n_in, n_out = len(comm.ins), len(comm.outs)

    def body(*refs):
        send_sems, recv_sems = refs[n_in + n_out:]
        sends, recvs = comm.copies(refs[:n_in], refs[n_in:n_in + n_out], lambda k: (send_sems.at[k], recv_sems.at[k]))
        for cp in sends:
            cp.start()
        for cp in recvs:
            cp.wait_recv()
        for cp in sends:
            cp.wait_send()

    return list(pl.pallas_call(
        body, name=name, out_shape=comm.outs, in_specs=[ANY] * n_in, out_specs=[ANY] * n_out,
        input_output_aliases=comm.aliases, scratch_shapes=[pltpu.SemaphoreType.DMA((comm.n_sems,))] * 2,
    )(*comm.ins))


def _exchange_small(small, extra, name):
    n_in, n_out = len(extra.ins), len(extra.outs)

    def body(*refs):
        ex_in, small_in = refs[:n_in], refs[n_in]
        ex_out, small_out = refs[n_in + 1:n_in + 1 + n_out], refs[n_in + 1 + n_out]
        send_sems, recv_sems, local_sems = refs[n_in + n_out + 2:]
        sem = lambda k: (send_sems.at[k], recv_sems.at[k])
        x, y, c = _my_place()
        me = 4 * x + 2 * y + c
        local = pltpu.make_async_copy(small_in, small_out.at[me], local_sems.at[0])
        local.start()
        sends, recvs = extra.copies(ex_in, ex_out, sem)
        for rel in range(1, N_DEVICES):
            px, py, pc = x ^ ((rel >> 2) & 1), y ^ ((rel >> 1) & 1), c ^ (rel & 1)
            k = extra.n_sems + rel - 1
            sends.append(_remote(small_in, small_out.at[me], sem(k), (px, py, pc)))
            recvs.append(_remote(small_in, small_out.at[4 * px + 2 * py + pc], sem(k), (px, py, pc)))
        for cp in sends:
            cp.start()
        for cp in recvs:
            cp.wait_recv()
        for cp in sends:
            cp.wait_send()
        local.wait()

    n_sems = extra.n_sems + N_DEVICES - 1
    res = pl.pallas_call(
        body, name=name, out_shape=extra.outs + [jax.ShapeDtypeStruct((N_DEVICES,) + small.shape, small.dtype)],
        in_specs=[ANY] * (n_in + 1), out_specs=[ANY] * (n_out + 1), input_output_aliases=extra.aliases,
        scratch_shapes=[pltpu.SemaphoreType.DMA((n_sems,)), pltpu.SemaphoreType.DMA((n_sems,)),
                        pltpu.SemaphoreType.DMA((1,))],
    )(*extra.ins, small)
    return list(res[:n_out]), res[n_out]


def _remote(src, dst, sems, device):
    return pltpu.make_async_remote_copy(src_ref=src, dst_ref=dst, send_sem=sems[0], recv_sem=sems[1],
                                        device_id=device, device_id_type=pl.DeviceIdType.MESH)


def _sum_piece(dest, slots, grad, blk, axis, layer, n_layers, name):
    r, c = blk
    h0 = r // 2
    tr = _tile(h0, 256, SUBLANES_BF16)
    tc = _tile(c, 2048, LANES)
    place = jnp.stack([2 * lax.axis_index("x") + lax.axis_index("y"), lax.axis_index("c")]).astype(jnp.int32)

    def body(p_ref, s_ref, g_ref, *rest):
        acc = g_ref[...].astype(F32) + s_ref[0].astype(F32)
        for k in range(1, N_DEVICES - 1):
            acc = acc + s_ref[k].astype(F32)
        rest[-1][...] = acc

    def g_map(i, j, p):
        return (p[1] * (h0 // tr) + (p[0] * (r // tr) if axis == 0 else 0) + i, (p[0] * (c // tc) if axis == 1 else 0) + j)

    in_specs = [pl.BlockSpec((N_DEVICES - 1, tr, tc), lambda i, j, p: (0, i, j)), pl.BlockSpec((tr, tc), g_map)]
    args = [place, slots, grad]
    if dest is not None:
        in_specs.append(ANY)
        args.append(dest)
    return pl.pallas_call(
        body, name=name, out_shape=jax.ShapeDtypeStruct((n_layers, r, c), F32),
        grid_spec=pltpu.PrefetchScalarGridSpec(
            num_scalar_prefetch=1, grid=(h0 // tr, c // tc), in_specs=in_specs,
            out_specs=pl.BlockSpec((None, tr, tc), lambda i, j, p: (layer, p[1] * (h0 // tr) + i, j))),
        input_output_aliases={3: 0} if dest is not None else {},
        compiler_params=_params("parallel", "parallel"),
    )(*args)


def _swap_halves_task(blocks):
    n = len(blocks)
    layers = [(a, l) for a, b in enumerate(blocks) for l in range(b.shape[0])]

    def copies(in_refs, out_refs, sem):
        x, y, c = _my_place()
        sends, recvs = [], []
        for k, (a, l) in enumerate(layers):
            blk = blocks[a].shape[1:]
            mine = _half_of(out_refs[a].at[l], blk, c)
            sends.append(_remote(mine, mine, sem(k), (x, y, 1 - c)))
            recvs.append(_remote(mine, _half_of(out_refs[a].at[l], blk, 1 - c), sem(k), (x, y, 1 - c)))
        return sends, recvs

    return _Comm(blocks, [jax.ShapeDtypeStruct(b.shape, b.dtype) for b in blocks], {a: a for a in range(n)},
                 len(layers), copies)


def _shard_shape(shard):
    return shard[0].shape[1:] if isinstance(shard, tuple) else shard.shape


def _place_shard(shard, axis, name):
    r, c = _shard_shape(shard)
    tr = _tile(r, 512, SUBLANES_BF16)
    full = (r * N_SHARDS, c) if axis == 0 else (r, c * N_SHARDS)
    me2 = (2 * lax.axis_index("x") + lax.axis_index("y")).astype(jnp.int32).reshape(1)

    def body(me_ref, s_ref, o_ref):
        o_ref[...] = s_ref[...].astype(o_ref.dtype)

    if axis == 0:
        out_map = lambda i, me: (me[0] * (r // tr) + i, 0)
    else:
        out_map = lambda i, me: (i, me[0])
    if isinstance(shard, tuple):
        src, layer = shard
        in_spec = pl.BlockSpec((None, tr, c), lambda i, me: (layer, i, 0))
    else:
        src, in_spec = shard, pl.BlockSpec((tr, c), lambda i, me: (i, 0))
    return pl.pallas_call(
        body, name=name, out_shape=jax.ShapeDtypeStruct(full, BF16),
        grid_spec=pltpu.PrefetchScalarGridSpec(
            num_scalar_prefetch=1, grid=(r // tr,), in_specs=[in_spec], out_specs=pl.BlockSpec((tr, c), out_map)),
        compiler_params=_params("parallel"),
    )(me2, src)


def _gather_ici_task(shards, axes, names):
    n = len(shards)
    blks = [_shard_shape(s) for s in shards]
    bases = [_place_shard(s, ax, f"place_{nm}") for s, ax, nm in zip(shards, axes, names)]

    def copies(in_refs, out_refs, sem):
        x, y, c = _my_place()
        me = 2 * x + y
        sends, recvs = [], []
        for a in range(n):
            mine = _piece(out_refs[a], blks[a], axes[a], me, c)
            for k, (px, py) in enumerate(_other_chips(x, y)):
                sends.append(_remote(mine, mine, sem(3 * a + k), (px, py, c)))
                recvs.append(_remote(mine, _piece(out_refs[a], blks[a], axes[a], 2 * px + py, c), sem(3 * a + k),
                                     (px, py, c)))
        return sends, recvs

    return _Comm(bases, [jax.ShapeDtypeStruct(b.shape, b.dtype) for b in bases], {a: a for a in range(n)}, 3 * n, copies)


def _gather_d2d_task(partials, blks, axes):
    n = len(partials)

    def copies(in_refs, out_refs, sem):
        x, y, c = _my_place()
        sends, recvs = [], []
        for a in range(n):
            for k, (px, py) in enumerate(_other_chips(x, y)):
                mine = _piece(out_refs[a], blks[a], axes[a], 2 * px + py, c)
                theirs = _piece(out_refs[a], blks[a], axes[a], 2 * px + py, 1 - c)
                sends.append(_remote(mine, mine, sem(3 * a + k), (x, y, 1 - c)))
                recvs.append(_remote(mine, theirs, sem(3 * a + k), (x, y, 1 - c)))
        return sends, recvs

    return _Comm(partials, [jax.ShapeDtypeStruct(p.shape, p.dtype) for p in partials], {a: a for a in range(n)},
                 3 * n, copies)


def _scatter_task(grads, blks, axes):
    n = len(grads)

    def copies(in_refs, out_refs, sem):
        x, y, c = _my_place()
        sends, recvs = [], []
        for rel in range(1, N_DEVICES):
            px, py, pc = x ^ ((rel >> 2) & 1), y ^ ((rel >> 1) & 1), c ^ (rel & 1)
            for a in range(n):
                src = _piece(in_refs[a], blks[a], axes[a], 2 * px + py, pc)
                k = (N_DEVICES - 1) * a + rel - 1
                sends.append(_remote(src, out_refs[a].at[rel - 1], sem(k), (px, py, pc)))
                recvs.append(_remote(src, out_refs[a].at[rel - 1], sem(k), (px, py, pc)))
        return sends, recvs

    outs = [jax.ShapeDtypeStruct((N_DEVICES - 1, b[0] // 2) + tuple(b[1:]), g.dtype) for b, g in zip(blks, grads)]
    return _Comm(grads, outs, {}, (N_DEVICES - 1) * n, copies)


def _pack(arrays):
    flat = jnp.concatenate([a.reshape(-1).astype(F32) for a in arrays])
    pad = (-flat.shape[0]) % (SUBLANES_F32 * LANES)
    return jnp.pad(flat, (0, pad)).reshape(-1, LANES)


def _unpack(packed, shapes):
    flat = packed.reshape(-1)
    out, off = [], 0
    for s in shapes:
        n = int(np.prod(s))
        out.append(flat[off:off + n].reshape(s))
        off += n
    return out


def _local_step(x, target, W, shards, geom, small):
    W = dict(W)
    B, S, D = x.shape
    T = B * S
    x2 = x.reshape(T, D)
    tgt = target.reshape(T, D)
    bmap = jnp.asarray(_bucket_map())
    blk = lambda names: [geom[k][0] for k in names]
    axs = lambda names: [geom[k][1] for k in names]
    ici = lambda names: _gather_ici_task([shards[k] for k in names], axs(names), names)
    d2d = lambda names, partials: _gather_d2d_task(list(partials), blk(names), axs(names))
    big, slots = {}, {}

    def scatter(names):
        return _scatter_task([big[k] for k in names], blk(names), axs(names))

    def ffn_bwd(l, dout, h, saved, first, second):
        xn, u, v, a = saved
        da = _mm_nt(dout, W[f"w_down{l}"], BF16, f"ffn{l}_down_dx")
        big[f"w_down{l}"] = _mm_tn(a, dout, f"ffn{l}_down_dw")
        names = first + [f"w_down{l}"]
        du, g_cw, g_cb, landed = _ffn_gate_bwd(u, v, small["ffn_conv"][l], da, S, f"ffn{l}_gate_bwd", comm=scatter(names))
        slots.update(zip(names, landed))
        big[f"w_up{l}"] = _mm_tn(xn, du, f"ffn{l}_up_dw")
        norm = (h, small["ffn_norm"][l], dout)
        if second:
            (dh, g_norm), (slots[f"w_up{l}"],) = _mm_nt(du, W[f"w_up{l}"], None, f"ffn{l}_up_dx",
                                                        comm=scatter([f"w_up{l}"]), norm=norm)
        else:
            dh, g_norm = _mm_nt(du, W[f"w_up{l}"], None, f"ffn{l}_up_dx", norm=norm)
        return dh, g_cw, g_cb, g_norm[0]

    xn0 = _rmsnorm_fwd(x2, small["a_norm"][0], "a_norm")
    p, part = _mm_nn(xn0, W["w_in"], None, BF16, "a_in", comm=ici(["w_up0"]))
    z, (W["w_up0"],) = _gate_a_fwd(p, small["a_conv"][0], S, "a_gate", comm=d2d(["w_up0"], part))
    (h1, xn1), part = _mm_nn(z, W["w_out"], x2, F32, "a_out", comm=ici(["w_down0"]), norm_gains=[small["ffn_norm"][0]])
    later = ["w_kv", "w_q", "w_o", "w_up1"]
    u0, landed = _mm_nn(xn1, W["w_up0"], None, BF16, "ffn0_up", comm=_Comm.join([d2d(["w_down0"], part), ici(later)]))
    W["w_down0"] = landed[0]
    (a0, v0), landed = _ffn_gate_fwd(u0, small["ffn_conv"][0], small["ffn_conv_b"][0], S, "ffn0_gate",
                               comm=_Comm.join([d2d(later, landed[1:]), ici(["w_down1"])]))
    W.update(zip(later, landed[:len(later)]))
    (h2, kvn, xn3), (W["w_down1"],) = _mm_nn(a0, W["w_down0"], h1, F32, "ffn0_down",
                                            comm=d2d(["w_down1"], landed[len(later):]),
                                            norm_gains=[small["kv_norm"], small["b_norm"][0]])
    kv = _mm_nn(kvn, W["w_kv"], None, F32, "kv_proj")
    q = _mm_nn(xn3, W["w_q"], None, F32, "q_proj")
    bias = _bias_tables(small["rel_bias"], bmap, "rel_bias_tables")
    q3, kv3 = q.reshape(B, S, D), kv.reshape(B, S, 2 * D)
    o3, lse3 = _attn_fwd(q3, kv3, bias, "attn_fwd")
    o = o3.reshape(T, D)
    h3, xn4 = _mm_nn(o, W["w_o"], h2, F32, "o_proj", norm_gains=[small["ffn_norm"][1]])
    u1 = _mm_nn(xn4, W["w_up1"], None, BF16, "ffn1_up")
    a1, v1 = _ffn_gate_fwd(u1, small["ffn_conv"][1], small["ffn_conv_b"][1], S, "ffn1_gate")
    sq_err, dh4, g_final = _loss_head(a1, W["w_down1"], h3, small["final_norm"], tgt, "ffn1_down_loss")
    loss = 0.5 * jnp.sum(sq_err) / D

    dh3, g_cw1, g_cb1, g_fn1 = ffn_bwd(1, dh4, h3, (xn4, u1, v1, a1), [], False)
    do, delta = _mm_nt(dh3, W["w_o"], F32, "o_proj_dx", head_dot=o)
    big["w_o"] = _mm_tn(o, dh3, "o_proj_dw")
    (dq3, dk3, dv3, dbias), landed = _attn_bwd(q3, kv3, delta.reshape(B, S, D), lse3, do.reshape(B, S, D), bias, "attn_bwd",
                                               comm=scatter(["w_up1", "w_o"]))
    slots.update(zip(["w_up1", "w_o"], landed))
    g_rel = _bias_grad(dbias, bmap, "rel_bias_grad")[:, :REL_BUCKETS].T
    dq, dk, dv = dq3.reshape(T, D), dk3.reshape(T, D), dv3.reshape(T, D)
    dh2, g_bn = _mm_nt(dq, W["w_q"], None, "q_proj_dx", norm=(h2, small["b_norm"][0], dh3))
    big["w_q"] = _mm_tn(xn3, dq, "q_proj_dw")
    dh2, g_kvn = _mm_nt([dk, dv], W["w_kv"], None, "kv_proj_dx", norm=(h2, small["kv_norm"], dh2))
    big["w_kv"] = _mm_tn(kvn, [dk, dv], "kv_proj_dw")
    dh1, g_cw0, g_cb0, g_fn0 = ffn_bwd(0, dh2, h1, (xn1, u0, v0, a0), ["w_q", "w_kv"], True)
    dz = _mm_nt(dh1, W["w_out"], BF16, "a_out_dx")
    big["w_out"] = _mm_tn(z, dh1, "a_out_dw")
    dp, g_aconv, (slots["w_out"],) = _gate_a_bwd(p, small["a_conv"][0], dz, S, "a_gate_bwd", comm=scatter(["w_out"]))
    big["w_in"] = _mm_tn(xn0, dp, "a_in_dw")
    (dx, g_an), (slots["w_in"],) = _mm_nt(dp, W["w_in"], None, "a_in_dx", comm=scatter(["w_in"]),
                                          norm=(x2, small["a_norm"][0], dh1))
    g_bn, g_kvn, g_an = g_bn[0], g_kvn[0], g_an[0]

    small_g = {"a_norm": g_an[None], "a_conv": g_aconv[None], "kv_norm": g_kvn, "b_norm": g_bn[None],
               "rel_bias": g_rel, "ffn_norm": jnp.stack([g_fn0, g_fn1]), "ffn_conv": jnp.stack([g_cw0, g_cw1]),
               "ffn_conv_b": jnp.stack([g_cb0, g_cb1]), "final_norm": g_final}
    return loss, dx.reshape(B, S, D), big, slots, small_g


BIG = ("w_in", "w_out", "w_kv", "w_q", "w_o", "w_up0", "w_up1", "w_down0", "w_down1")
SMALL = ("a_norm", "a_conv", "kv_norm", "b_norm", "rel_bias", "ffn_norm", "ffn_conv", "ffn_conv_b", "final_norm")
SMALL_SHARDED = ("a_norm", "a_conv", "ffn_conv")
WEIGHT_ORDER = ("a_norm", "a_w_in", "a_conv", "a_w_out", "kv_norm", "w_kv", "b_norm", "b_w_q", "b_w_o", "rel_bias",
                "ffn_norm", "ffn_w_up", "ffn_conv", "ffn_conv_b", "ffn_w_down", "final_norm")
GRAD_OF = {"w_in": ("a_w_in", 0), "w_out": ("a_w_out", 0), "w_kv": ("w_kv", 0), "w_q": ("b_w_q", 0), "w_o": ("b_w_o", 0),
           "w_up0": ("ffn_w_up", 0), "w_up1": ("ffn_w_up", 1), "w_down0": ("ffn_w_down", 0), "w_down1": ("ffn_w_down", 1)}


def _as2d(a):
    return a.reshape(-1, a.shape[-1])


def kernel(x, a_norm, a_w_in, a_conv, a_w_out, kv_norm, w_kv, b_norm, b_w_q, b_w_o, rel_bias, ffn_norm, ffn_w_up, ffn_conv, ffn_conv_b, ffn_w_down, final_norm, loss_target, m_a_norm, m_a_w_in, m_a_conv, m_a_w_out, m_kv_norm, m_w_kv, m_b_norm, m_b_w_q, m_b_w_o, m_rel_bias, m_ffn_norm, m_ffn_w_up, m_ffn_conv, m_ffn_conv_b, m_ffn_w_down, m_final_norm, v_a_norm, v_a_w_in, v_a_conv, v_a_w_out, v_kv_norm, v_w_kv, v_b_norm, v_b_w_q, v_b_w_o, v_rel_bias, v_ffn_norm, v_ffn_w_up, v_ffn_conv, v_ffn_conv_b, v_ffn_w_down, v_final_norm):
    given = dict(a_norm=a_norm, a_w_in=a_w_in, a_conv=a_conv, a_w_out=a_w_out, kv_norm=kv_norm, w_kv=w_kv, b_norm=b_norm,
                 b_w_q=b_w_q, b_w_o=b_w_o, rel_bias=rel_bias, ffn_norm=ffn_norm, ffn_w_up=ffn_w_up, ffn_conv=ffn_conv,
                 ffn_conv_b=ffn_conv_b, ffn_w_down=ffn_w_down, final_norm=final_norm)
    mom_m = dict(a_norm=m_a_norm, a_w_in=m_a_w_in, a_conv=m_a_conv, a_w_out=m_a_w_out, kv_norm=m_kv_norm, w_kv=m_w_kv,
                 b_norm=m_b_norm, b_w_q=m_b_w_q, b_w_o=m_b_w_o, rel_bias=m_rel_bias, ffn_norm=m_ffn_norm,
                 ffn_w_up=m_ffn_w_up, ffn_conv=m_ffn_conv, ffn_conv_b=m_ffn_conv_b, ffn_w_down=m_ffn_w_down,
                 final_norm=m_final_norm)
    mom_v = dict(a_norm=v_a_norm, a_w_in=v_a_w_in, a_conv=v_a_conv, a_w_out=v_a_w_out, kv_norm=v_kv_norm, w_kv=v_w_kv,
                 b_norm=v_b_norm, b_w_q=v_b_w_q, b_w_o=v_b_w_o, rel_bias=v_rel_bias, ffn_norm=v_ffn_norm,
                 ffn_w_up=v_ffn_w_up, ffn_conv=v_ffn_conv, ffn_conv_b=v_ffn_conv_b, ffn_w_down=v_ffn_w_down,
                 final_norm=v_final_norm)

    shard = {"w_in": ((a_w_in, 0), 1), "w_out": ((a_w_out, 0), 0), "w_kv": (w_kv, 1), "w_q": ((b_w_q, 0), 0),
             "w_o": ((b_w_o, 0), 0), "w_up0": ((ffn_w_up, 0), 1), "w_up1": ((ffn_w_up, 1), 1),
             "w_down0": ((ffn_w_down, 0), 0), "w_down1": ((ffn_w_down, 1), 0)}

    small_sharded = [given[k] for k in SMALL_SHARDED]
    packed = _pack(small_sharded)
    first = ("w_in", "w_out")
    fulls, packed_all = _gather_weights([shard[k][0] for k in first], [shard[k][1] for k in first], first, packed)
    W = dict(zip(first, fulls))
    later = {k: shard[k][0] for k in BIG if k not in first}
    geom = {k: (_shard_shape(shard[k][0]), shard[k][1]) for k in BIG}
    small = {k: given[k] for k in SMALL}
    per_shard = [_unpack(packed_all[j], [a.shape for a in small_sharded]) for j in range(N_SHARDS)]
    for i, k in enumerate(SMALL_SHARDED):
        small[k] = jnp.concatenate([per_shard[j][i] for j in range(N_SHARDS)], axis=-1)

    loss, grad_x, big_g, slots, small_g = _local_step(x, loss_target, W, later, geom, small)
    loss = lax.psum(loss, MESH_AXES)

    small_shapes = [small_g[k].shape for k in SMALL]
    layers_of = {}
    for k in BIG:
        layers_of.setdefault(GRAD_OF[k][0], []).append(k)
    blocks = {}
    for name, members in layers_of.items():
        dest = None
        for k in members:
            dest = _sum_piece(dest, slots[k], big_g[k], geom[k][0], geom[k][1], GRAD_OF[k][1], len(members), f"sum_{k}")
        blocks[name] = dest
    swapped, small_slots = _exchange_small(_pack([small_g[k] for k in SMALL]), _swap_halves_task(list(blocks.values())),
                                           "swap_halves_exchange_small")
    reduced = dict(zip(blocks, swapped))
    small_sum = _sum_slots(small_slots, "sum_small")
    small_red = dict(zip(SMALL, _unpack(small_sum, small_shapes)))
    j = 2 * lax.axis_index("x") + lax.axis_index("y")
    for k in SMALL_SHARDED:
        w = given[k].shape[-1]
        small_red[k] = lax.dynamic_slice_in_dim(small_red[k], j * w, w, axis=small_red[k].ndim - 1)

    grads, deltas, new_m, new_v = {}, {}, {}, {}
    for name in WEIGHT_ORDER:
        if name in reduced:
            g = reduced[name].reshape(given[name].shape)
            d, nm, nv = _adamw(_as2d(given[name]), _as2d(g), _as2d(mom_m[name]), _as2d(mom_v[name]), f"adamw_{name}")
            grads[name] = g
            deltas[name], new_m[name], new_v[name] = (t.reshape(given[name].shape) for t in (d, nm, nv))
    small_names = [n for n in WEIGHT_ORDER if n not in reduced]
    for n in small_names:
        grads[n] = small_red[n].reshape(given[n].shape)
    sw, sg, sm, sv = (_pack([d[n] for n in small_names]) for d in (given, grads, mom_m, mom_v))
    d, nm, nv = _adamw(sw, sg, sm, sv, "adamw_small")
    shapes = [given[n].shape for n in small_names]
    for n, a, b_, c_ in zip(small_names, _unpack(d, shapes), _unpack(nm, shapes), _unpack(nv, shapes)):
        deltas[n], new_m[n], new_v[n] = a, b_, c_

    return (loss, grad_x, *[grads[n] for n in WEIGHT_ORDER], *[deltas[n] for n in WEIGHT_ORDER],
            *[new_m[n] for n in WEIGHT_ORDER], *[new_v[n] for n in WEIGHT_ORDER])
```

```python
import functools
import math

import numpy as np

import jax
import jax.numpy as jnp
from jax import lax
from jax.experimental import pallas as pl
from jax.experimental.pallas import tpu as pltpu

F32 = jnp.float32
BF16 = jnp.bfloat16

RMS_EPS = 1e-6
HEAD_DIM = 64
ATT_BLOCK = 128
DILATED_BRANCHES = ((128, 1), (512, 4), (2048, 16))
REL_BUCKETS = 32
REL_MAX_DISTANCE = 2048
MASKED_LOGIT = -1e30
ATTN_FWD_UNROLL = (8, 8, 4)
ATTN_BWD_UNROLL = (8, 8, 8)

ADAM_LR = 0.001
ADAM_B1 = 0.9
ADAM_B2 = 0.999
ADAM_EPS = 1e-08
ADAM_WD = 0.01
ADAM_STEP = 10

LANES = 128
SUBLANES_F32 = 8
SUBLANES_BF16 = 16
VMEM_LIMIT_BYTES = 56 * 1024 * 1024

N_SHARDS = 4
N_DEVICES = 8
ANY = pl.BlockSpec(memory_space=pl.ANY)


def _tile(n, pref, mult):
    best = None
    for t in range(mult, min(n, pref) + 1, mult):
        if n % t == 0:
            best = t
    if best is None:
        raise ValueError(f"no tile for {n} (multiple of {mult}, at most {pref})")
    return best


def _params(*sem):
    return pltpu.CompilerParams(dimension_semantics=sem, vmem_limit_bytes=VMEM_LIMIT_BYTES)


class _Comm:
    def __init__(self, ins, outs, aliases, n_sems, copies):
        self.ins, self.outs, self.aliases, self.n_sems, self.copies = list(ins), list(outs), dict(aliases), n_sems, copies

    @staticmethod
    def join(parts):
        parts = [p for p in parts if p is not None]
        ins, outs, aliases, offs, n_sems = [], [], {}, [], 0
        for p in parts:
            offs.append((len(ins), len(outs), n_sems))
            aliases.update({len(ins) + i: len(outs) + o for i, o in p.aliases.items()})
            ins += p.ins
            outs += p.outs
            n_sems += p.n_sems

        def copies(in_refs, out_refs, sem):
            sends, recvs = [], []
            for p, (i0, o0, s0) in zip(parts, offs):
                s, r = p.copies(in_refs[i0:i0 + len(p.ins)], out_refs[o0:o0 + len(p.outs)],
                                lambda k, s0=s0: sem(s0 + k))
                sends += s
                recvs += r
            return sends, recvs

        return _Comm(ins, outs, aliases, n_sems, copies)


def _pallas(body, *, comm=None, name, out_shape, grid=(), in_specs=(), out_specs=(), scratch_shapes=(),
            compiler_params=None):
    if comm is None:
        return pl.pallas_call(body, name=name, out_shape=out_shape, grid=grid, in_specs=in_specs, out_specs=out_specs,
                              scratch_shapes=scratch_shapes, compiler_params=compiler_params)
    single = not isinstance(out_shape, (tuple, list))
    outs = (out_shape,) if single else tuple(out_shape)
    o_specs = (out_specs,) if single else tuple(out_specs)
    n_in, n_cin, n_out, n_cout, n_scr = len(in_specs), len(comm.ins), len(outs), len(comm.outs), len(scratch_shapes)

    def carried(*refs):
        base_in, c_in = refs[:n_in], refs[n_in:n_in + n_cin]
        o0 = n_in + n_cin
        base_out, c_out = refs[o0:o0 + n_out], refs[o0 + n_out:o0 + n_out + n_cout]
        s0 = o0 + n_out + n_cout
        base_scr, (send_sems, recv_sems) = refs[s0:s0 + n_scr], refs[s0 + n_scr:]
        sem = lambda k: (send_sems.at[k], recv_sems.at[k])
        first = functools.reduce(jnp.logical_and, [pl.program_id(a) == 0 for a in range(len(grid))])
        last = functools.reduce(jnp.logical_and, [pl.program_id(a) == n - 1 for a, n in enumerate(grid)])

        @pl.when(first)
        def _():
            for cp in comm.copies(c_in, c_out, sem)[0]:
                cp.start()

        body(*base_in, *base_out, *base_scr)

        @pl.when(last)
        def _():
            sends, recvs = comm.copies(c_in, c_out, sem)
            for cp in recvs:
                cp.wait_recv()
            for cp in sends:
                cp.wait_send()

    call = pl.pallas_call(
        carried, name=name, out_shape=outs + tuple(comm.outs), grid=grid,
        in_specs=list(in_specs) + [ANY] * n_cin, out_specs=o_specs + (ANY,) * n_cout,
        scratch_shapes=list(scratch_shapes) + [pltpu.SemaphoreType.DMA((comm.n_sems,))] * 2,
        input_output_aliases={n_in + i: n_out + o for i, o in comm.aliases.items()},
        compiler_params=_params(*(["arbitrary"] * len(grid))))

    def run(*args):
        res = call(*args, *comm.ins)
        base = res[0] if single else tuple(res[:n_out])
        return base, list(res[n_out:])

    return run


def _rmsnorm_fwd(x, g, name):
    T, D = x.shape
    tm = _tile(T, 512, SUBLANES_BF16)

    def body(x_ref, g_ref, o_ref):
        xf = x_ref[...]
        r = lax.rsqrt(jnp.mean(xf * xf, axis=-1, keepdims=True) + RMS_EPS)
        o_ref[...] = ((xf * r) * g_ref[...]).astype(o_ref.dtype)

    return pl.pallas_call(
        body, name=name, out_shape=jax.ShapeDtypeStruct((T, D), BF16), grid=(T // tm,),
        in_specs=[pl.BlockSpec((tm, D), lambda i: (i, 0)), pl.BlockSpec((1, D), lambda i: (0, 0))],
        out_specs=pl.BlockSpec((tm, D), lambda i: (i, 0)),
        compiler_params=_params("parallel"),
    )(x, g.reshape(1, D))


def _loss_head(a, b, res, g, target, name):
    T, K = a.shape
    D = b.shape[1]
    tm = _tile(T, 512, SUBLANES_BF16)

    def body(a_ref, b_ref, res_ref, g_ref, t_ref, dh_ref, acc_ref):
        @pl.when(pl.program_id(0) == 0)
        def _():
            acc_ref[...] = jnp.zeros_like(acc_ref)

        xf = jnp.dot(a_ref[...].astype(BF16), b_ref[...], preferred_element_type=F32) + res_ref[...]
        r = lax.rsqrt(jnp.mean(xf * xf, axis=-1, keepdims=True) + RMS_EPS)
        xhat = xf * r
        err = xhat * g_ref[...] - t_ref[...]
        dy = err * (1.0 / D)
        acc_ref[0:1, :] += jnp.sum(dy * xhat, axis=0, keepdims=True)
        acc_ref[1:2, :] += jnp.sum(err * err, axis=0, keepdims=True)
        t = dy * g_ref[...]
        dh_ref[...] = r * (t - xhat * jnp.mean(t * xhat, axis=-1, keepdims=True))

    row = pl.BlockSpec((tm, D), lambda i: (i, 0))
    dh, acc = pl.pallas_call(
        body, name=name,
        out_shape=(jax.ShapeDtypeStruct((T, D), F32), jax.ShapeDtypeStruct((SUBLANES_F32, D), F32)),
        grid=(T // tm,),
        in_specs=[pl.BlockSpec((tm, K), lambda i: (i, 0)), pl.BlockSpec((K, D), lambda i: (0, 0)), row,
                  pl.BlockSpec((1, D), lambda i: (0, 0)), row],
        out_specs=(row, pl.BlockSpec((SUBLANES_F32, D), lambda i: (0, 0))),
        compiler_params=_params("arbitrary"),
    )(a, b, res, g.reshape(1, D), target)
    return acc[1], dh, acc[0]


def _mm_nn(a, b, res, out_dtype, name, comm=None, norm_gains=()):
    T, K = a.shape
    N = b.shape[1]
    n_g = len(norm_gains)
    tm = _tile(T, 512 if n_g else 1024, SUBLANES_BF16)
    tn = _tile(N, 3072, LANES)
    assert not n_g or tn == N, "the fused rmsnorm needs whole rows in one tile"

    def body(a_ref, b_ref, *rest):
        ins, outs = rest[:len(rest) - 1 - n_g], rest[len(rest) - 1 - n_g:]
        acc = jnp.dot(a_ref[...].astype(BF16), b_ref[...], preferred_element_type=F32)
        if res is not None:
            acc = acc + ins[0][...]
        outs[0][...] = acc.astype(outs[0].dtype)
        if n_g:
            y = acc * lax.rsqrt(jnp.mean(acc * acc, axis=-1, keepdims=True) + RMS_EPS)
            for g_ref, xn_ref in zip(ins[len(ins) - n_g:], outs[1:]):
                xn_ref[...] = (y * g_ref[...]).astype(xn_ref.dtype)

    tile = pl.BlockSpec((tm, tn), lambda j, i: (i, j))
    in_specs = [pl.BlockSpec((tm, K), lambda j, i: (i, 0)), pl.BlockSpec((K, tn), lambda j, i: (0, j))]
    args = [a, b]
    if res is not None:
        in_specs.append(tile)
        args.append(res)
    for g in norm_gains:
        in_specs.append(pl.BlockSpec((1, N), lambda j, i: (0, 0)))
        args.append(g.reshape(1, N))
    out_shape = jax.ShapeDtypeStruct((T, N), out_dtype)
    if n_g:
        out_shape = (out_shape,) + (jax.ShapeDtypeStruct((T, N), BF16),) * n_g
    return _pallas(
        body, comm=comm, name=name, out_shape=out_shape, grid=(N // tn, T // tm),
        in_specs=in_specs, out_specs=(tile,) * (1 + n_g) if n_g else tile,
        compiler_params=_params("parallel", "parallel"),
    )(*args)


def _mm_nt(dy, b, out_dtype, name, comm=None, norm=None, head_dot=None):
    dys = list(dy) if isinstance(dy, (list, tuple)) else [dy]
    T, n_each = dys[0].shape
    K = b.shape[0]
    tm = _tile(T, 2048 if norm is None and head_dot is None else 1024, SUBLANES_BF16)
    tk = _tile(K, 1536, LANES)
    tn = _tile(n_each, 2816 if norm is None else 1536, LANES)
    per = n_each // tn
    n_steps = per * len(dys)
    assert norm is None or tk == K, "the fused rmsnorm backward needs whole rows in one tile"

    def body(*refs):
        dy_refs, b_ref, acc_ref = refs[:len(dys)], refs[len(dys)], refs[-1]
        i, n = pl.program_id(0), pl.program_id(2)

        @pl.when(n == 0)
        def _():
            acc_ref[...] = jnp.zeros_like(acc_ref)

        for p, dy_ref in enumerate(dy_refs):
            @pl.when(jnp.logical_and(n >= p * per, n < (p + 1) * per))
            def _(dy_ref=dy_ref):
                acc_ref[...] += lax.dot_general(dy_ref[...].astype(BF16), b_ref[...], (((1,), (1,)), ((), ())),
                                                preferred_element_type=F32)

        if norm is None and head_dot is None:
            @pl.when(n == n_steps - 1)
            def _():
                refs[-2][...] = acc_ref[...].astype(refs[-2].dtype)
        elif norm is None:
            o_ref, out_ref, dot_ref = refs[len(dys) + 1:-1]

            @pl.when(n == n_steps - 1)
            def _():
                d = acc_ref[...]
                out_ref[...] = d.astype(out_ref.dtype)
                head0 = lax.broadcasted_iota(jnp.int32, (tm, LANES), 1) < HEAD_DIM
                for c0 in range(0, tk, LANES):
                    x = d[:, c0:c0 + LANES] * o_ref[:, c0:c0 + LANES]
                    d0 = jnp.sum(jnp.where(head0, x, 0.0), axis=-1, keepdims=True)
                    d1 = jnp.sum(jnp.where(head0, 0.0, x), axis=-1, keepdims=True)
                    dot_ref[:, c0:c0 + LANES] = jnp.where(head0, d0, d1)
        else:
            x_ref, g_ref, dres_ref, dx_ref, dg_ref = refs[len(dys) + 1:-1]

            @pl.when(jnp.logical_and(i == 0, n == 0))
            def _():
                dg_ref[...] = jnp.zeros_like(dg_ref)

            @pl.when(n == n_steps - 1)
            def _():
                xf = x_ref[...]
                r = lax.rsqrt(jnp.mean(xf * xf, axis=-1, keepdims=True) + RMS_EPS)
                xhat = xf * r
                d = acc_ref[...]
                dg_ref[0:1, :] += jnp.sum(d * xhat, axis=0, keepdims=True)
                t = d * g_ref[...]
                dx_ref[...] = dres_ref[...] + r * (t - xhat * jnp.mean(t * xhat, axis=-1, keepdims=True))

    in_specs = [pl.BlockSpec((tm, tn), lambda i, k, n, p=p: (i, jnp.clip(n - p * per, 0, per - 1))) for p in range(len(dys))]
    in_specs.append(pl.BlockSpec((tk, tn), lambda i, k, n: (k, n)))
    args = dys + [b]
    tile = pl.BlockSpec((tm, tk), lambda i, k, n: (i, k))
    if norm is None and head_dot is None:
        out_shape, out_specs = jax.ShapeDtypeStruct((T, K), out_dtype), tile
    elif norm is None:
        in_specs.append(tile)
        args.append(head_dot)
        out_shape = (jax.ShapeDtypeStruct((T, K), out_dtype), jax.ShapeDtypeStruct((T, K), F32))
        out_specs = (tile, tile)
    else:
        x, g, dres = norm
        in_specs += [tile, pl.BlockSpec((1, K), lambda i, k, n: (0, 0)), tile]
        args += [x, g.reshape(1, K), dres]
        out_shape = (jax.ShapeDtypeStruct((T, K), F32), jax.ShapeDtypeStruct((SUBLANES_F32, K), F32))
        out_specs = (tile, pl.BlockSpec((SUBLANES_F32, K), lambda i, k, n: (0, 0)))
    return _pallas(
        body, comm=comm, name=name, out_shape=out_shape, grid=(T // tm, K // tk, n_steps),
        in_specs=in_specs, out_specs=out_specs, scratch_shapes=[pltpu.VMEM((tm, tk), F32)],
        compiler_params=_params("parallel", "parallel", "arbitrary") if norm is None else _params(*["arbitrary"] * 3),
    )(*args)


def _mm_tn(a, dy, name):
    T, K = a.shape
    N = dy.shape[1]
    tt = _tile(T, 2048, SUBLANES_BF16)
    tk = _tile(K, 1536, LANES)
    tn = _tile(N, 1536, LANES)
    t_steps = T // tt

    def body(a_ref, dy_ref, o_ref, acc_ref):
        t = pl.program_id(2)

        @pl.when(t == 0)
        def _():
            acc_ref[...] = jnp.zeros_like(acc_ref)

        acc_ref[...] += lax.dot_general(a_ref[...].astype(BF16), dy_ref[...].astype(BF16),
                                        (((0,), (0,)), ((), ())), preferred_element_type=F32)

        @pl.when(t == t_steps - 1)
        def _():
            o_ref[...] = acc_ref[...].astype(o_ref.dtype)

    return pl.pallas_call(
        body, name=name, out_shape=jax.ShapeDtypeStruct((K, N), BF16), grid=(K // tk, N // tn, t_steps),
        in_specs=[pl.BlockSpec((tt, tk), lambda k, n, t: (t, k)), pl.BlockSpec((tt, tn), lambda k, n, t: (t, n))],
        out_specs=pl.BlockSpec((tk, tn), lambda k, n, t: (k, n)),
        scratch_shapes=[pltpu.VMEM((tk, tn), F32)],
        compiler_params=_params("parallel", "parallel", "arbitrary"),
    )(a, dy)


def _rows_before(halo, cur, k):
    h = halo.shape[0]
    return pltpu.roll(jnp.concatenate([halo, cur], axis=0), k, 0)[h:]


def _rows_after(cur, halo, k):
    n = cur.shape[0]
    total = n + halo.shape[0]
    return pltpu.roll(jnp.concatenate([cur, halo], axis=0), total - k, 0)[:n]


def _halo_specs(tm, width, n_rows):
    per = tm // SUBLANES_BF16
    last = n_rows // SUBLANES_BF16 - 1
    prev = pl.BlockSpec((SUBLANES_BF16, width), lambda i: (jnp.maximum(i * per - 1, 0), 0))
    nxt = pl.BlockSpec((SUBLANES_BF16, width), lambda i: (jnp.minimum((i + 1) * per, last), 0))
    return prev, nxt


def _gate_a_fwd(p, cw, seq, name, comm=None):
    T, D3 = p.shape
    D = D3 // 3
    tm = _tile(seq, 512, SUBLANES_BF16)
    cc = _tile(D, 256, LANES)
    prev, _ = _halo_specs(tm, D3, T)

    def body(p_ref, ph_ref, cw_ref, z_ref):
        at_start = (pl.program_id(0) * tm) % seq == 0
        for c0 in range(0, D, cc):
            b = p_ref[:, c0:c0 + cc].astype(F32)
            u = p_ref[:, D + c0:D + c0 + cc].astype(F32) * p_ref[:, 2 * D + c0:2 * D + c0 + cc].astype(F32)
            uh = ph_ref[:, D + c0:D + c0 + cc].astype(F32) * ph_ref[:, 2 * D + c0:2 * D + c0 + cc].astype(F32)
            uh = jnp.where(at_start, 0.0, uh)
            w = cw_ref[:, c0:c0 + cc]
            cv = _rows_before(uh, u, 2) * w[0:1] + _rows_before(uh, u, 1) * w[1:2] + u * w[2:3]
            z_ref[:, c0:c0 + cc] = (b * cv).astype(z_ref.dtype)

    return _pallas(
        body, comm=comm, name=name, out_shape=jax.ShapeDtypeStruct((T, D), BF16), grid=(T // tm,),
        in_specs=[pl.BlockSpec((tm, D3), lambda i: (i, 0)), prev, pl.BlockSpec((3, D), lambda i: (0, 0))],
        out_specs=pl.BlockSpec((tm, D), lambda i: (i, 0)),
        compiler_params=_params("parallel"),
    )(p, p, cw)


def _gate_a_bwd(p, cw, dz, seq, name, comm=None):
    T, D3 = p.shape
    D = D3 // 3
    tm = _tile(seq, 512, SUBLANES_BF16)
    cc = _tile(D, 256, LANES)
    p_prev, p_next = _halo_specs(tm, D3, T)
    _, dz_next = _halo_specs(tm, D, T)

    def body(p_ref, pp_ref, pn_ref, dz_ref, dzn_ref, cw_ref, dp_ref, dcw_ref):
        i = pl.program_id(0)

        @pl.when(i == 0)
        def _():
            dcw_ref[...] = jnp.zeros_like(dcw_ref)

        at_start = (i * tm) % seq == 0
        at_end = ((i + 1) * tm) % seq == 0
        for c0 in range(0, D, cc):
            cb, cc_, ch = slice(c0, c0 + cc), slice(D + c0, D + c0 + cc), slice(2 * D + c0, 2 * D + c0 + cc)
            b = p_ref[:, cb].astype(F32)
            c = p_ref[:, cc_].astype(F32)
            hh = p_ref[:, ch].astype(F32)
            u = c * hh
            uh = jnp.where(at_start, 0.0, pp_ref[:, cc_].astype(F32) * pp_ref[:, ch].astype(F32))
            w = cw_ref[:, cb]
            u1 = _rows_before(uh, u, 1)
            u2 = _rows_before(uh, u, 2)
            cv = u2 * w[0:1] + u1 * w[1:2] + u * w[2:3]
            dz_t = dz_ref[:, cb].astype(F32)
            dcv = dz_t * b
            dcvn = jnp.where(at_end, 0.0, dzn_ref[:, cb].astype(F32) * pn_ref[:, cb].astype(F32))
            du = dcv * w[2:3] + _rows_after(dcv, dcvn, 1) * w[1:2] + _rows_after(dcv, dcvn, 2) * w[0:1]
            dp_ref[:, cb] = (dz_t * cv).astype(dp_ref.dtype)
            dp_ref[:, cc_] = (du * hh).astype(dp_ref.dtype)
            dp_ref[:, ch] = (du * c).astype(dp_ref.dtype)
            dcw_ref[0:1, cb] += jnp.sum(dcv * u2, axis=0, keepdims=True)
            dcw_ref[1:2, cb] += jnp.sum(dcv * u1, axis=0, keepdims=True)
            dcw_ref[2:3, cb] += jnp.sum(dcv * u, axis=0, keepdims=True)

    res = _pallas(
        body, comm=comm, name=name,
        out_shape=(jax.ShapeDtypeStruct((T, D3), BF16), jax.ShapeDtypeStruct((SUBLANES_F32, D), F32)),
        grid=(T // tm,),
        in_specs=[pl.BlockSpec((tm, D3), lambda i: (i, 0)), p_prev, p_next,
                  pl.BlockSpec((tm, D), lambda i: (i, 0)), dz_next, pl.BlockSpec((3, D), lambda i: (0, 0))],
        out_specs=(pl.BlockSpec((tm, D3), lambda i: (i, 0)), pl.BlockSpec((SUBLANES_F32, D), lambda i: (0, 0))),
        compiler_params=_params("arbitrary"),
    )(p, p, p, dz, dz, cw)
    if comm is None:
        return res[0], res[1][0:3]
    return res[0][0], res[0][1][0:3], res[1]


def _ffn_gate_fwd(u, cw, cb, seq, name, comm=None):
    T, F2 = u.shape
    F = F2 // 2
    tm = _tile(seq, 256, SUBLANES_BF16)
    cc = _tile(F, 256, LANES)
    prev, _ = _halo_specs(tm, F2, T)

    def body(u_ref, uh_ref, cw_ref, cb_ref, a_ref, v_ref):
        at_start = (pl.program_id(0) * tm) % seq == 0

        def conv(c0):
            cols = slice(c0, c0 + cc)
            cur = u_ref[:, cols].astype(F32)
            halo = jnp.where(at_start, 0.0, uh_ref[:, cols].astype(F32))
            w = cw_ref[:, cols]
            return (_rows_before(halo, cur, 2) * w[0:1] + _rows_before(halo, cur, 1) * w[1:2] + cur * w[2:3]
                    + cb_ref[:, cols])

        for c0 in range(0, F, cc):
            g = conv(c0)
            up = conv(F + c0)
            a_ref[:, c0:c0 + cc] = ((g * jax.nn.sigmoid(g)) * up).astype(a_ref.dtype)
            v_ref[:, c0:c0 + cc] = g.astype(v_ref.dtype)
            v_ref[:, F + c0:F + c0 + cc] = up.astype(v_ref.dtype)

    return _pallas(
        body, comm=comm, name=name,
        out_shape=(jax.ShapeDtypeStruct((T, F), BF16), jax.ShapeDtypeStruct((T, F2), BF16)), grid=(T // tm,),
        in_specs=[pl.BlockSpec((tm, F2), lambda i: (i, 0)), prev,
                  pl.BlockSpec((3, F2), lambda i: (0, 0)), pl.BlockSpec((1, F2), lambda i: (0, 0))],
        out_specs=(pl.BlockSpec((tm, F), lambda i: (i, 0)), pl.BlockSpec((tm, F2), lambda i: (i, 0))),
        compiler_params=_params("parallel"),
    )(u, u, cw, cb.reshape(1, F2))


def _ffn_gate_bwd(u, v, cw, da, seq, name, comm=None):
    T, F2 = u.shape
    F = F2 // 2
    H = SUBLANES_BF16
    tm = _tile(seq, 256, H)
    cc = _tile(F, 256, LANES)
    _, v_next = _halo_specs(tm, F2, T)
    _, da_next = _halo_specs(tm, F, T)

    def body(u_ref, v_ref, vn_ref, da_ref, dan_ref, cw_ref, du_ref, acc_ref):
        i = pl.program_id(0)

        @pl.when(i == 0)
        def _():
            acc_ref[...] = jnp.zeros_like(acc_ref)

        at_end = ((i + 1) * tm) % seq == 0
        n = tm + H

        def rows_and_next(ref, nxt, cols):
            return jnp.concatenate([ref[:, cols].astype(F32), nxt[:, cols].astype(F32)], axis=0)

        def back(d, cols):
            w = cw_ref[:, cols]
            d0 = d[:tm]
            d1 = pltpu.roll(d, n - 1, 0)[:tm]
            d2 = pltpu.roll(d, n - 2, 0)[:tm]
            du_ref[:, cols] = (d0 * w[2:3] + d1 * w[1:2] + d2 * w[0:1]).astype(du_ref.dtype)
            ut = u_ref[:, cols].astype(F32)
            acc_ref[0:1, cols] += jnp.sum(d2 * ut, axis=0, keepdims=True)
            acc_ref[1:2, cols] += jnp.sum(d1 * ut, axis=0, keepdims=True)
            acc_ref[2:3, cols] += jnp.sum(d0 * ut, axis=0, keepdims=True)
            acc_ref[3:4, cols] += jnp.sum(d0, axis=0, keepdims=True)

        for c0 in range(0, F, cc):
            gc, uc = slice(c0, c0 + cc), slice(F + c0, F + c0 + cc)
            g = rows_and_next(v_ref, vn_ref, gc)
            up = rows_and_next(v_ref, vn_ref, uc)
            da_ext = jnp.concatenate([da_ref[:, gc].astype(F32),
                                      jnp.where(at_end, 0.0, dan_ref[:, gc].astype(F32))], axis=0)
            sg = jax.nn.sigmoid(g)
            back(da_ext * up * (sg * (1.0 + g * (1.0 - sg))), gc)
            back(da_ext * (g * sg), uc)

    res = _pallas(
        body, comm=comm, name=name,
        out_shape=(jax.ShapeDtypeStruct((T, F2), BF16), jax.ShapeDtypeStruct((SUBLANES_F32, F2), F32)),
        grid=(T // tm,),
        in_specs=[pl.BlockSpec((tm, F2), lambda i: (i, 0)), pl.BlockSpec((tm, F2), lambda i: (i, 0)), v_next,
                  pl.BlockSpec((tm, F), lambda i: (i, 0)), da_next, pl.BlockSpec((3, F2), lambda i: (0, 0))],
        out_specs=(pl.BlockSpec((tm, F2), lambda i: (i, 0)), pl.BlockSpec((SUBLANES_F32, F2), lambda i: (0, 0))),
        compiler_params=_params("arbitrary"),
    )(u, v, v, da, da, cw)
    (du, acc), landed = res if comm is not None else (res, None)
    return (du, acc[0:3], acc[3]) if comm is None else (du, acc[0:3], acc[3], landed)


def _bucket_map():
    P = ATT_BLOCK
    qi = np.arange(P, dtype=np.int64)[:, None]
    kc = np.arange(2 * P, dtype=np.int64)[None, :]
    delta = qi + P - kc
    maps = []
    max_exact = REL_BUCKETS // 2
    for window, dilation in DILATED_BRANCHES:
        band = (delta >= 0) & (delta <= window // dilation)
        n = np.maximum(delta * dilation, 0)
        nf = np.maximum(n, max_exact).astype(np.float32)
        large = max_exact + (np.log(nf / np.float32(max_exact)) / np.float32(math.log(REL_MAX_DISTANCE / max_exact))
                             * np.float32(REL_BUCKETS - max_exact)).astype(np.int32)
        large = np.minimum(large, REL_BUCKETS - 1)
        bucket = np.where(n < max_exact, n, large)
        maps.append(np.where(band, bucket, -1).astype(np.int32))
    return np.stack(maps)


def _bias_tables(rel_bias, bmap, name):
    n_pairs = rel_bias.shape[1] // 2
    nbr, P, P2 = bmap.shape

    def body(rb_ref, bm_ref, o_ref):
        pair = pl.program_id(0)
        in_seq = lax.broadcasted_iota(jnp.int32, (P, P2), 1) >= P
        for br in range(nbr):
            bm = bm_ref[br]
            for hh in range(2):
                acc = jnp.full((P, P2), MASKED_LOGIT, F32)
                for b in range(REL_BUCKETS):
                    acc = jnp.where(bm == b, rb_ref[b, 2 * pair + hh], acc)
                o_ref[br, 0, 0, hh * P:(hh + 1) * P, :] = acc
                o_ref[br, 0, 1, hh * P:(hh + 1) * P, :] = jnp.where(in_seq, acc, MASKED_LOGIT)

    return pl.pallas_call(
        body, name=name, out_shape=jax.ShapeDtypeStruct((nbr, n_pairs, 2, 2 * P, P2), F32), grid=(n_pairs,),
        in_specs=[pl.BlockSpec(memory_space=pltpu.SMEM), pl.BlockSpec((nbr, P, P2), lambda h: (0, 0, 0))],
        out_specs=pl.BlockSpec((nbr, 1, 2, 2 * P, P2), lambda h: (0, h, 0, 0, 0)),
        compiler_params=_params("parallel"),
    )(rel_bias, bmap)


def _bias_grad(dbias, bmap, name):
    nbr, n_pairs, _, P2 = dbias.shape
    P = P2 // 2

    def body(db_ref, bm_ref, o_ref):
        lane = lax.broadcasted_iota(jnp.int32, (1, LANES), 1)
        for hh in range(2):
            row = jnp.zeros((1, LANES), F32)
            for br in range(nbr):
                bm = bm_ref[br]
                d = db_ref[br, 0, hh * P:(hh + 1) * P, :]
                for b in range(REL_BUCKETS):
                    hit = jnp.sum(jnp.where(bm == b, d, 0.0), axis=1, keepdims=True)
                    row = row + jnp.where(lane == b, jnp.sum(hit, axis=0, keepdims=True), 0.0)
            o_ref[hh] = row

    return pl.pallas_call(
        body, name=name, out_shape=jax.ShapeDtypeStruct((2 * n_pairs, 1, LANES), F32), grid=(n_pairs,),
        in_specs=[pl.BlockSpec((nbr, 1, P2, P2), lambda h: (0, h, 0, 0)), pl.BlockSpec((nbr, P, P2), lambda h: (0, 0, 0))],
        out_specs=pl.BlockSpec((2, 1, LANES), lambda h: (h, 0, 0)),
        compiler_params=_params("parallel"),
    )(dbias, bmap)[:, 0, :]


def _rows(start, dilation):
    if dilation == 1:
        return pl.ds(pl.multiple_of(start, ATT_BLOCK), ATT_BLOCK)
    return pl.ds(start, ATT_BLOCK, stride=dilation)


def _for_each_block(seq, unroll, fn):
    P = ATT_BLOCK
    for br, (_, d) in enumerate(DILATED_BRANCHES):
        nb = seq // d // P
        u = unroll[br] if (unroll[br] % nb == 0 or nb % unroll[br] == 0) else 1
        step = d * P

        def some(i, carry, br=br, d=d, nb=nb, u=u, step=step):
            blocks = []
            if u % nb == 0:
                for k in range(u):
                    if k % nb == 0:
                        start = i * (u // nb) + k // nb
                        blocks.append((start, start, 1))
                    else:
                        blocks.append((blocks[-1][0] + step, blocks[-1][0], 0))
            else:
                r, j0 = (i * u) // nb, (i * u) % nb
                blocks.append((r + j0 * step, r + jnp.maximum(j0 - 1, 0) * step, jnp.where(j0 == 0, 1, 0)))
                for _ in range(1, u):
                    blocks.append((blocks[-1][0] + step, blocks[-1][0], 0))
            fn(br, d, blocks)
            return carry

        lax.fori_loop(0, nb * d // u, some, 0)


class _RowCache:
    def __init__(self, dilation):
        self.dilation, self.seen = dilation, {}

    def rows(self, ref, start):
        key = (id(ref), id(start))
        if key not in self.seen:
            self.seen[key] = ref[_rows(start, self.dilation), :].astype(BF16)
        return self.seen[key]

    def window(self, ref, start, prev):
        return jnp.concatenate([self.rows(ref, prev), self.rows(ref, start)], axis=0)


def _stack_heads(x, head0):
    return jnp.concatenate([jnp.where(head0, x, 0.0), jnp.where(head0, 0.0, x)], axis=0).astype(BF16)


def _attn_fwd(q, kv, bias, name):
    B, S, D = q.shape
    P = ATT_BLOCK
    n_pairs = D // LANES
    nbr = len(DILATED_BRANCHES)
    scale = HEAD_DIM ** -0.5

    def body(q_ref, k_ref, v_ref, bias_ref, o_ref, lse_ref, *stats):
        m_s, l_s, acc_s = stats[0:nbr], stats[nbr:2 * nbr], stats[2 * nbr:3 * nbr]
        head0 = lax.broadcasted_iota(jnp.int32, (P, LANES), 1) < HEAD_DIM

        def block(br, d, blocks):
            cache = _RowCache(d)
            s = [lax.dot_general(_stack_heads(q_ref[_rows(start, d), :] * scale, head0),
                                 cache.window(k_ref, start, prev), (((1,), (1,)), ((), ())),
                                 preferred_element_type=F32) + bias_ref[br, 0, first]
                 for start, prev, first in blocks]
            m = [jnp.max(x, axis=-1, keepdims=True) for x in s]
            p = [jnp.exp(x - y) for x, y in zip(s, m)]
            l = [jnp.sum(x, axis=-1, keepdims=True) for x in p]
            pv = [jnp.dot(x.astype(BF16), cache.window(v_ref, start, prev), preferred_element_type=F32)
                  for x, (start, prev, _) in zip(p, blocks)]
            for k, (start, _, _) in enumerate(blocks):
                rows = _rows(start, d)
                m_s[br][rows, :] = jnp.where(head0, m[k][:P], m[k][P:])
                l_s[br][rows, :] = jnp.where(head0, l[k][:P], l[k][P:])
                acc_s[br][rows, :] = jnp.where(head0, pv[k][:P], pv[k][P:])

        _for_each_block(S, ATTN_FWD_UNROLL, block)

        chunk = _tile(S, 256, SUBLANES_F32)

        def merge(i, carry):
            rows = pl.ds(pl.multiple_of(i * chunk, chunk), chunk)
            ms = [m_s[br][rows, :] for br in range(nbr)]
            m = functools.reduce(jnp.maximum, ms)
            l = jnp.zeros((chunk, LANES), F32)
            acc = jnp.zeros((chunk, LANES), F32)
            for br in range(nbr):
                w = jnp.exp(ms[br] - m)
                l = l + w * l_s[br][rows, :]
                acc = acc + w * acc_s[br][rows, :]
            o_ref[rows, :] = acc / l
            lse_ref[rows, :] = m + jnp.log(l)
            return carry

        lax.fori_loop(0, S // chunk, merge, 0)

    slab = lambda col0: pl.BlockSpec((None, S, LANES), lambda b, h: (b, 0, col0 + h))
    return pl.pallas_call(
        body, name=name,
        out_shape=(jax.ShapeDtypeStruct((B, S, D), F32), jax.ShapeDtypeStruct((B, S, D), F32)),
        grid=(B, n_pairs),
        in_specs=[slab(0), slab(0), slab(n_pairs),
                  pl.BlockSpec((nbr, 1, 2, 2 * P, 2 * P), lambda b, h: (0, h, 0, 0, 0))],
        out_specs=(slab(0), slab(0)),
        scratch_shapes=[pltpu.VMEM((S, LANES), F32)] * (3 * nbr),
        compiler_params=_params("parallel", "parallel"),
    )(q, kv, kv, bias)


def _attn_bwd(q, kv, delta, lse, do, bias, name, comm=None):
    B, S, D = q.shape
    P = ATT_BLOCK
    n_pairs = D // LANES
    nbr = len(DILATED_BRANCHES)
    scale = HEAD_DIM ** -0.5

    def body(q_ref, k_ref, v_ref, delta_s, lse_ref, do_ref, bias_ref, dq_ref, dk_ref, dv_ref, dbias_ref):
        head0 = lax.broadcasted_iota(jnp.int32, (P, LANES), 1) < HEAD_DIM

        @pl.when(pl.program_id(1) == 0)
        def _():
            dbias_ref[...] = jnp.zeros_like(dbias_ref)

        chunk = _tile(S, 512, SUBLANES_F32)

        def prepare(i, carry):
            rows = pl.ds(pl.multiple_of(i * chunk, chunk), chunk)
            zero = jnp.zeros((chunk, LANES), F32)
            dq_ref[rows, :] = zero
            dk_ref[rows, :] = zero
            dv_ref[rows, :] = zero
            return carry

        lax.fori_loop(0, S // chunk, prepare, 0)

        def per_head(x):
            return jnp.concatenate([x[:, 0:1], x[:, HEAD_DIM:HEAD_DIM + 1]], axis=0)

        nt = (((1,), (1,)), ((), ()))
        tn = (((0,), (0,)), ((), ()))

        def block(br, d, blocks):
            cache = _RowCache(d)
            q2 = [_stack_heads(q_ref[_rows(start, d), :] * scale, head0) for start, _, _ in blocks]
            do2 = [_stack_heads(do_ref[_rows(start, d), :], head0) for start, _, _ in blocks]
            kb = [cache.window(k_ref, start, prev) for start, prev, _ in blocks]
            vb = [cache.window(v_ref, start, prev) for start, prev, _ in blocks]
            s = [lax.dot_general(a, b, nt, preferred_element_type=F32) + bias_ref[br, 0, first]
                 for a, b, (_, _, first) in zip(q2, kb, blocks)]
            dp = [lax.dot_general(a, b, nt, preferred_element_type=F32) for a, b in zip(do2, vb)]
            p = [jnp.exp(x - per_head(lse_ref[_rows(start, d), :])) for x, (start, _, _) in zip(s, blocks)]
            ds = [x * (y - per_head(delta_s[_rows(start, d), :])) for x, y, (start, _, _) in zip(p, dp, blocks)]
            for x in ds:
                dbias_ref[br, 0] += x
            ds16 = [x.astype(BF16) for x in ds]
            dq2 = [jnp.dot(a, b, preferred_element_type=F32) for a, b in zip(ds16, kb)]
            dk = [lax.dot_general(a, b, tn, preferred_element_type=F32) for a, b in zip(ds16, q2)]
            dv = [lax.dot_general(a.astype(BF16), b, tn, preferred_element_type=F32) for a, b in zip(p, do2)]
            parts = {}
            for k, (start, prev, first) in enumerate(blocks):
                dq_ref[_rows(start, d), :] += jnp.where(head0, dq2[k][:P], dq2[k][P:]) * scale
                parts.setdefault(id(start), [start, []])[1].append((dk[k][P:], dv[k][P:]))
                if not (isinstance(first, int) and first == 1):
                    parts.setdefault(id(prev), [prev, []])[1].append((dk[k][:P], dv[k][:P]))
            for start, terms in parts.values():
                rows = _rows(start, d)
                dk_ref[rows, :] += functools.reduce(jnp.add, [t[0] for t in terms])
                dv_ref[rows, :] += functools.reduce(jnp.add, [t[1] for t in terms])

        _for_each_block(S, ATTN_BWD_UNROLL, block)

    slab = lambda col0: pl.BlockSpec((None, S, LANES), lambda h, b: (b, 0, col0 + h))
    tab = pl.BlockSpec((nbr, 1, 2, 2 * P, 2 * P), lambda h, b: (0, h, 0, 0, 0))
    dtab = pl.BlockSpec((nbr, 1, 2 * P, 2 * P), lambda h, b: (0, h, 0, 0))
    shp = jax.ShapeDtypeStruct((B, S, D), F32)
    return _pallas(
        body, comm=comm, name=name,
        out_shape=(shp, shp, shp, jax.ShapeDtypeStruct((nbr, n_pairs, 2 * P, 2 * P), F32)),
        grid=(n_pairs, B),
        in_specs=[slab(0), slab(0), slab(n_pairs), slab(0), slab(0), slab(0), tab],
        out_specs=(slab(0), slab(0), slab(0), dtab),
        compiler_params=_params("parallel", "arbitrary"),
    )(q, kv, kv, delta, lse, do, bias)


def _adamw(w, g, m, v, name):
    R, C = w.shape
    tr = _tile(R, 256, SUBLANES_F32) if R % SUBLANES_F32 == 0 else R
    tc = _tile(C, 2048, LANES) if C % LANES == 0 else C

    def body(w_ref, g_ref, m_ref, v_ref, d_ref, nm_ref, nv_ref):
        g_ = g_ref[...]
        m2 = ADAM_B1 * m_ref[...] + (1.0 - ADAM_B1) * g_
        v2 = ADAM_B2 * v_ref[...] + (1.0 - ADAM_B2) * (g_ * g_)
        m_hat = m2 / (1.0 - ADAM_B1 ** ADAM_STEP)
        v_hat = v2 / (1.0 - ADAM_B2 ** ADAM_STEP)
        d_ref[...] = -ADAM_LR * (m_hat / (jnp.sqrt(v_hat) + ADAM_EPS) + ADAM_WD * w_ref[...])
        nm_ref[...] = m2
        nv_ref[...] = v2

    blk = pl.BlockSpec((tr, tc), lambda i, j: (i, j))
    shp = jax.ShapeDtypeStruct((R, C), F32)
    return pl.pallas_call(
        body, name=name, out_shape=(shp, shp, shp), grid=(R // tr, C // tc),
        in_specs=[blk] * 4, out_specs=(blk,) * 3, compiler_params=_params("parallel", "parallel"),
    )(w, g, m, v)


def _sum_slots(slots, name):
    n, R, C = slots.shape
    tr = _tile(R, 256, SUBLANES_BF16) if R % SUBLANES_BF16 == 0 else R
    tc = _tile(C, 2048, LANES) if C % LANES == 0 else C

    def body(s_ref, o_ref):
        acc = s_ref[0].astype(F32)
        for k in range(1, n):
            acc = acc + s_ref[k].astype(F32)
        o_ref[...] = acc

    return pl.pallas_call(
        body, name=name, out_shape=jax.ShapeDtypeStruct((R, C), F32), grid=(R // tr, C // tc),
        in_specs=[pl.BlockSpec((n, tr, tc), lambda i, j: (0, i, j))],
        out_specs=pl.BlockSpec((tr, tc), lambda i, j: (i, j)),
        compiler_params=_params("parallel", "parallel"),
    )(slots)


def _my_place():
    return lax.axis_index("x"), lax.axis_index("y"), lax.axis_index("c")


def _other_chips(x, y):
    return [(1 - x, y), (x, 1 - y), (1 - x, 1 - y)]


def _piece(ref, blk, axis, shard, half):
    h0 = blk[0] // 2
    idx = []
    for dim, n in enumerate(blk):
        if dim == 0:
            start = half * h0 + (shard * n if axis == 0 else 0)
            idx.append(pl.ds(start, h0))
        elif dim == axis:
            idx.append(pl.ds(shard * n, n))
        else:
            idx.append(slice(None))
    return ref.at[tuple(idx)]


def _half_of(ref, blk, half):
    return ref.at[pl.ds(half * (blk[0] // 2), blk[0] // 2)]


def _gather_weights(shards, axes, names, small):
    n = len(shards)
    blks = [_shard_shape(s) for s in shards]
    task = _gather_ici_task(shards, axes, names)

    def stage1(*refs):
        small_in, outs, small_out = refs[n], refs[n + 1:2 * n + 1], refs[2 * n + 1]
        send_sems, recv_sems, local_sems = refs[2 * n + 2:]
        sem = lambda k: (send_sems.at[k], recv_sems.at[k])
        x, y, c = _my_place()
        me = 2 * x + y
        local = pltpu.make_async_copy(small_in, small_out.at[me], local_sems.at[0])
        local.start()
        sends, recvs = task.copies(None, outs, sem)
        for k, (px, py) in enumerate(_other_chips(x, y)):
            sends.append(_remote(small_in, small_out.at[me], sem(task.n_sems + k), (px, py, c)))
            recvs.append(_remote(small_in, small_out.at[2 * px + py], sem(task.n_sems + k), (px, py, c)))
        for cp in sends:
            cp.start()
        for cp in recvs:
            cp.wait_recv()
        for cp in sends:
            cp.wait_send()
        local.wait()

    res = pl.pallas_call(
        stage1, name="gather_weights_ici",
        out_shape=task.outs + [jax.ShapeDtypeStruct((N_SHARDS,) + small.shape, small.dtype)],
        in_specs=[ANY] * (n + 1), out_specs=[ANY] * (n + 1), input_output_aliases={a: a for a in range(n)},
        scratch_shapes=[pltpu.SemaphoreType.DMA((task.n_sems + 3,)), pltpu.SemaphoreType.DMA((task.n_sems + 3,)),
                        pltpu.SemaphoreType.DMA((1,))],
    )(*task.ins, small)
    full = _run_comm(_gather_d2d_task(list(res[:n]), blks, axes), "gather_weights_d2d")
    return full, res[n]


def _run_comm(comm, name):
    n_in, n_out = len(comm.ins), len(comm.outs)

    def body(*refs):
        send_sems, recv_sems = refs[n_in + n_out:]
        sends, recvs = comm.copies(refs[:n_in], refs[n_in:n_in + n_out], lambda k: (send_sems.at[k], recv_sems.at[k]))
        for cp in sends:
            cp.start()
        for cp in recvs:
            cp.wait_recv()
        for cp in sends:
            cp.wait_send()

    return list(pl.pallas_call(
        body, name=name, out_shape=comm.outs, in_specs=[ANY] * n_in, out_specs=[ANY] * n_out,
        input_output_aliases=comm.aliases, scratch_shapes=[pltpu.SemaphoreType.DMA((comm.n_sems,))] * 2,
    )(*comm.ins))


def _exchange_small(small, extra, name):
    n_in, n_out = len(extra.ins), len(extra.outs)

    def body(*refs):
        ex_in, small_in = refs[:n_in], refs[n_in]
        ex_out, small_out = refs[n_in + 1:n_in + 1 + n_out], refs[n_in + 1 + n_out]
        send_sems, recv_sems, local_sems = refs[n_in + n_out + 2:]
        sem = lambda k: (send_sems.at[k], recv_sems.at[k])
        x, y, c = _my_place()
        me = 4 * x + 2 * y + c
        local = pltpu.make_async_copy(small_in, small_out.at[me], local_sems.at[0])
        local.start()
        sends, recvs = extra.copies(ex_in, ex_out, sem)
        for rel in range(1, N_DEVICES):
            px, py, pc = x ^ ((rel >> 2) & 1), y ^ ((rel >> 1) & 1), c ^ (rel & 1)
            k = extra.n_sems + rel - 1
            sends.append(_remote(small_in, small_out.at[me], sem(k), (px, py, pc)))
            recvs.append(_remote(small_in, small_out.at[4 * px + 2 * py + pc], sem(k), (px, py, pc)))
        for cp in sends:
            cp.start()
        for cp in recvs:
            cp.wait_recv()
        for cp in sends:
            cp.wait_send()
        local.wait()

    n_sems = extra.n_sems + N_DEVICES - 1
    res = pl.pallas_call(
        body, name=name, out_shape=extra.outs + [jax.ShapeDtypeStruct((N_DEVICES,) + small.shape, small.dtype)],
        in_specs=[ANY] * (n_in + 1), out_specs=[ANY] * (n_out + 1), input_output_aliases=extra.aliases,
        scratch_shapes=[pltpu.SemaphoreType.DMA((n_sems,)), pltpu.SemaphoreType.DMA((n_sems,)),
                        pltpu.SemaphoreType.DMA((1,))],
    )(*extra.ins, small)
    return list(res[:n_out]), res[n_out]


def _remote(src, dst, sems, device):
    return pltpu.make_async_remote_copy(src_ref=src, dst_ref=dst, send_sem=sems[0], recv_sem=sems[1],
                                        device_id=device, device_id_type=pl.DeviceIdType.MESH)


def _sum_piece(dest, slots, grad, blk, axis, layer, n_layers, name):
    r, c = blk
    h0 = r // 2
    tr = _tile(h0, 256, SUBLANES_BF16)
    tc = _tile(c, 2048, LANES)
    place = jnp.stack([2 * lax.axis_index("x") + lax.axis_index("y"), lax.axis_index("c")]).astype(jnp.int32)

    def body(p_ref, s_ref, g_ref, *rest):
        acc = g_ref[...].astype(F32) + s_ref[0].astype(F32)
        for k in range(1, N_DEVICES - 1):
            acc = acc + s_ref[k].astype(F32)
        rest[-1][...] = acc

    def g_map(i, j, p):
        return (p[1] * (h0 // tr) + (p[0] * (r // tr) if axis == 0 else 0) + i, (p[0] * (c // tc) if axis == 1 else 0) + j)

    in_specs = [pl.BlockSpec((N_DEVICES - 1, tr, tc), lambda i, j, p: (0, i, j)), pl.BlockSpec((tr, tc), g_map)]
    args = [place, slots, grad]
    if dest is not None:
        in_specs.append(ANY)
        args.append(dest)
    return pl.pallas_call(
        body, name=name, out_shape=jax.ShapeDtypeStruct((n_layers, r, c), F32),
        grid_spec=pltpu.PrefetchScalarGridSpec(
            num_scalar_prefetch=1, grid=(h0 // tr, c // tc), in_specs=in_specs,
            out_specs=pl.BlockSpec((None, tr, tc), lambda i, j, p: (layer, p[1] * (h0 // tr) + i, j))),
        input_output_aliases={3: 0} if dest is not None else {},
        compiler_params=_params("parallel", "parallel"),
    )(*args)


def _swap_halves_task(blocks):
    n = len(blocks)
    layers = [(a, l) for a, b in enumerate(blocks) for l in range(b.shape[0])]

    def copies(in_refs, out_refs, sem):
        x, y, c = _my_place()
        sends, recvs = [], []
        for k, (a, l) in enumerate(layers):
            blk = blocks[a].shape[1:]
            mine = _half_of(out_refs[a].at[l], blk, c)
            sends.append(_remote(mine, mine, sem(k), (x, y, 1 - c)))
            recvs.append(_remote(mine, _half_of(out_refs[a].at[l], blk, 1 - c), sem(k), (x, y, 1 - c)))
        return sends, recvs

    return _Comm(blocks, [jax.ShapeDtypeStruct(b.shape, b.dtype) for b in blocks], {a: a for a in range(n)},
                 len(layers), copies)


def _shard_shape(shard):
    return shard[0].shape[1:] if isinstance(shard, tuple) else shard.shape


def _place_shard(shard, axis, name):
    r, c = _shard_shape(shard)
    tr = _tile(r, 512, SUBLANES_BF16)
    full = (r * N_SHARDS, c) if axis == 0 else (r, c * N_SHARDS)
    me2 = (2 * lax.axis_index("x") + lax.axis_index("y")).astype(jnp.int32).reshape(1)

    def body(me_ref, s_ref, o_ref):
        o_ref[...] = s_ref[...].astype(o_ref.dtype)

    if axis == 0:
        out_map = lambda i, me: (me[0] * (r // tr) + i, 0)
    else:
        out_map = lambda i, me: (i, me[0])
    if isinstance(shard, tuple):
        src, layer = shard
        in_spec = pl.BlockSpec((None, tr, c), lambda i, me: (layer, i, 0))
    else:
        src, in_spec = shard, pl.BlockSpec((tr, c), lambda i, me: (i, 0))
    return pl.pallas_call(
        body, name=name, out_shape=jax.ShapeDtypeStruct(full, BF16),
        grid_spec=pltpu.PrefetchScalarGridSpec(
            num_scalar_prefetch=1, grid=(r // tr,), in_specs=[in_spec], out_specs=pl.BlockSpec((tr, c), out_map)),
        compiler_params=_params("parallel"),
    )(me2, src)


def _gather_ici_task(shards, axes, names):
    n = len(shards)
    blks = [_shard_shape(s) for s in shards]
    bases = [_place_shard(s, ax, f"place_{nm}") for s, ax, nm in zip(shards, axes, names)]

    def copies(in_refs, out_refs, sem):
        x, y, c = _my_place()
        me = 2 * x + y
        sends, recvs = [], []
        for a in range(n):
            mine = _piece(out_refs[a], blks[a], axes[a], me, c)
            for k, (px, py) in enumerate(_other_chips(x, y)):
                sends.append(_remote(mine, mine, sem(3 * a + k), (px, py, c)))
                recvs.append(_remote(mine, _piece(out_refs[a], blks[a], axes[a], 2 * px + py, c), sem(3 * a + k),
                                     (px, py, c)))
        return sends, recvs

    return _Comm(bases, [jax.ShapeDtypeStruct(b.shape, b.dtype) for b in bases], {a: a for a in range(n)}, 3 * n, copies)


def _gather_d2d_task(partials, blks, axes):
    n = len(partials)

    def copies(in_refs, out_refs, sem):
        x, y, c = _my_place()
        sends, recvs = [], []
        for a in range(n):
            for k, (px, py) in enumerate(_other_chips(x, y)):
                mine = _piece(out_refs[a], blks[a], axes[a], 2 * px + py, c)
                theirs = _piece(out_refs[a], blks[a], axes[a], 2 * px + py, 1 - c)
                sends.append(_remote(mine, mine, sem(3 * a + k), (x, y, 1 - c)))
                recvs.append(_remote(mine, theirs, sem(3 * a + k), (x, y, 1 - c)))
        return sends, recvs

    return _Comm(partials, [jax.ShapeDtypeStruct(p.shape, p.dtype) for p in partials], {a: a for a in range(n)},
                 3 * n, copies)


def _scatter_task(grads, blks, axes):
    n = len(grads)

    def copies(in_refs, out_refs, sem):
        x, y, c = _my_place()
        sends, recvs = [], []
        for rel in range(1, N_DEVICES):
            px, py, pc = x ^ ((rel >> 2) & 1), y ^ ((rel >> 1) & 1), c ^ (rel & 1)
            for a in range(n):
                src = _piece(in_refs[a], blks[a], axes[a], 2 * px + py, pc)
                k = (N_DEVICES - 1) * a + rel - 1
                sends.append(_remote(src, out_refs[a].at[rel - 1], sem(k), (px, py, pc)))
                recvs.append(_remote(src, out_refs[a].at[rel - 1], sem(k), (px, py, pc)))
        return sends, recvs

    outs = [jax.ShapeDtypeStruct((N_DEVICES - 1, b[0] // 2) + tuple(b[1:]), g.dtype) for b, g in zip(blks, grads)]
    return _Comm(grads, outs, {}, (N_DEVICES - 1) * n, copies)


def _pack(arrays):
    flat = jnp.concatenate([a.reshape(-1).astype(F32) for a in arrays])
    pad = (-flat.shape[0]) % (SUBLANES_F32 * LANES)
    return jnp.pad(flat, (0, pad)).reshape(-1, LANES)


def _unpack(packed, shapes):
    flat = packed.reshape(-1)
    out, off = [], 0
    for s in shapes:
        n = int(np.prod(s))
        out.append(flat[off:off + n].reshape(s))
        off += n
    return out


def _local_step(x, target, W, shards, geom, small):
    W = dict(W)
    B, S, D = x.shape
    T = B * S
    x2 = x.reshape(T, D)
    tgt = target.reshape(T, D)
    bmap = jnp.asarray(_bucket_map())
    blk = lambda names: [geom[k][0] for k in names]
    axs = lambda names: [geom[k][1] for k in names]
    ici = lambda names: _gather_ici_task([shards[k] for k in names], axs(names), names)
    d2d = lambda names, partials: _gather_d2d_task(list(partials), blk(names), axs(names))
    big, slots = {}, {}

    def scatter(names):
        return _scatter_task([big[k] for k in names], blk(names), axs(names))

    def ffn_bwd(l, dout, h, saved, first, second):
        xn, u, v, a = saved
        da = _mm_nt(dout, W[f"w_down{l}"], BF16, f"ffn{l}_down_dx")
        big[f"w_down{l}"] = _mm_tn(a, dout, f"ffn{l}_down_dw")
        names = first + [f"w_down{l}"]
        du, g_cw, g_cb, landed = _ffn_gate_bwd(u, v, small["ffn_conv"][l], da, S, f"ffn{l}_gate_bwd", comm=scatter(names))
        slots.update(zip(names, landed))
        big[f"w_up{l}"] = _mm_tn(xn, du, f"ffn{l}_up_dw")
        norm = (h, small["ffn_norm"][l], dout)
        if second:
            (dh, g_norm), (slots[f"w_up{l}"],) = _mm_nt(du, W[f"w_up{l}"], None, f"ffn{l}_up_dx",
                                                        comm=scatter([f"w_up{l}"]), norm=norm)
        else:
            dh, g_norm = _mm_nt(du, W[f"w_up{l}"], None, f"ffn{l}_up_dx", norm=norm)
        return dh, g_cw, g_cb, g_norm[0]

    xn0 = _rmsnorm_fwd(x2, small["a_norm"][0], "a_norm")
    p, part = _mm_nn(xn0, W["w_in"], None, BF16, "a_in", comm=ici(["w_up0"]))
    z, (W["w_up0"],) = _gate_a_fwd(p, small["a_conv"][0], S, "a_gate", comm=d2d(["w_up0"], part))
    (h1, xn1), part = _mm_nn(z, W["w_out"], x2, F32, "a_out", comm=ici(["w_down0"]), norm_gains=[small["ffn_norm"][0]])
    later = ["w_kv", "w_q", "w_o", "w_up1"]
    u0, landed = _mm_nn(xn1, W["w_up0"], None, BF16, "ffn0_up", comm=_Comm.join([d2d(["w_down0"], part), ici(later)]))
    W["w_down0"] = landed[0]
    (a0, v0), landed = _ffn_gate_fwd(u0, small["ffn_conv"][0], small["ffn_conv_b"][0], S, "ffn0_gate",
                               comm=_Comm.join([d2d(later, landed[1:]), ici(["w_down1"])]))
    W.update(zip(later, landed[:len(later)]))
    (h2, kvn, xn3), (W["w_down1"],) = _mm_nn(a0, W["w_down0"], h1, F32, "ffn0_down",
                                            comm=d2d(["w_down1"], landed[len(later):]),
                                            norm_gains=[small["kv_norm"], small["b_norm"][0]])
    kv = _mm_nn(kvn, W["w_kv"], None, F32, "kv_proj")
    q = _mm_nn(xn3, W["w_q"], None, F32, "q_proj")
    bias = _bias_tables(small["rel_bias"], bmap, "rel_bias_tables")
    q3, kv3 = q.reshape(B, S, D), kv.reshape(B, S, 2 * D)
    o3, lse3 = _attn_fwd(q3, kv3, bias, "attn_fwd")
    o = o3.reshape(T, D)
    h3, xn4 = _mm_nn(o, W["w_o"], h2, F32, "o_proj", norm_gains=[small["ffn_norm"][1]])
    u1 = _mm_nn(xn4, W["w_up1"], None, BF16, "ffn1_up")
    a1, v1 = _ffn_gate_fwd(u1, small["ffn_conv"][1], small["ffn_conv_b"][1], S, "ffn1_gate")
    sq_err, dh4, g_final = _loss_head(a1, W["w_down1"], h3, small["final_norm"], tgt, "ffn1_down_loss")
    loss = 0.5 * jnp.sum(sq_err) / D

    dh3, g_cw1, g_cb1, g_fn1 = ffn_bwd(1, dh4, h3, (xn4, u1, v1, a1), [], False)
    do, delta = _mm_nt(dh3, W["w_o"], F32, "o_proj_dx", head_dot=o)
    big["w_o"] = _mm_tn(o, dh3, "o_proj_dw")
    (dq3, dk3, dv3, dbias), landed = _attn_bwd(q3, kv3, delta.reshape(B, S, D), lse3, do.reshape(B, S, D), bias, "attn_bwd",
                                               comm=scatter(["w_up1", "w_o"]))
    slots.update(zip(["w_up1", "w_o"], landed))
    g_rel = _bias_grad(dbias, bmap, "rel_bias_grad")[:, :REL_BUCKETS].T
    dq, dk, dv = dq3.reshape(T, D), dk3.reshape(T, D), dv3.reshape(T, D)
    dh2, g_bn = _mm_nt(dq, W["w_q"], None, "q_proj_dx", norm=(h2, small["b_norm"][0], dh3))
    big["w_q"] = _mm_tn(xn3, dq, "q_proj_dw")
    dh2, g_kvn = _mm_nt([dk, dv], W["w_kv"], None, "kv_proj_dx", norm=(h2, small["kv_norm"], dh2))
    big["w_kv"] = jnp.concatenate([_mm_tn(kvn, dk, "k_proj_dw"), _mm_tn(kvn, dv, "v_proj_dw")], axis=1)
    dh1, g_cw0, g_cb0, g_fn0 = ffn_bwd(0, dh2, h1, (xn1, u0, v0, a0), ["w_q", "w_kv"], True)
    dz = _mm_nt(dh1, W["w_out"], BF16, "a_out_dx")
    big["w_out"] = _mm_tn(z, dh1, "a_out_dw")
    dp, g_aconv, (slots["w_out"],) = _gate_a_bwd(p, small["a_conv"][0], dz, S, "a_gate_bwd", comm=scatter(["w_out"]))
    big["w_in"] = _mm_tn(xn0, dp, "a_in_dw")
    (dx, g_an), (slots["w_in"],) = _mm_nt(dp, W["w_in"], None, "a_in_dx", comm=scatter(["w_in"]),
                                          norm=(x2, small["a_norm"][0], dh1))
    g_bn, g_kvn, g_an = g_bn[0], g_kvn[0], g_an[0]

    small_g = {"a_norm": g_an[None], "a_conv": g_aconv[None], "kv_norm": g_kvn, "b_norm": g_bn[None],
               "rel_bias": g_rel, "ffn_norm": jnp.stack([g_fn0, g_fn1]), "ffn_conv": jnp.stack([g_cw0, g_cw1]),
               "ffn_conv_b": jnp.stack([g_cb0, g_cb1]), "final_norm": g_final}
    return loss, dx.reshape(B, S, D), big, slots, small_g


BIG = ("w_in", "w_out", "w_kv", "w_q", "w_o", "w_up0", "w_up1", "w_down0", "w_down1")
SMALL = ("a_norm", "a_conv", "kv_norm", "b_norm", "rel_bias", "ffn_norm", "ffn_conv", "ffn_conv_b", "final_norm")
SMALL_SHARDED = ("a_norm", "a_conv", "ffn_conv")
WEIGHT_ORDER = ("a_norm", "a_w_in", "a_conv", "a_w_out", "kv_norm", "w_kv", "b_norm", "b_w_q", "b_w_o", "rel_bias",
                "ffn_norm", "ffn_w_up", "ffn_conv", "ffn_conv_b", "ffn_w_down", "final_norm")
GRAD_OF = {"w_in": ("a_w_in", 0), "w_out": ("a_w_out", 0), "w_kv": ("w_kv", 0), "w_q": ("b_w_q", 0), "w_o": ("b_w_o", 0),
           "w_up0": ("ffn_w_up", 0), "w_up1": ("ffn_w_up", 1), "w_down0": ("ffn_w_down", 0), "w_down1": ("ffn_w_down", 1)}


def _as2d(a):
    return a.reshape(-1, a.shape[-1])


def kernel(x, a_norm, a_w_in, a_conv, a_w_out, kv_norm, w_kv, b_norm, b_w_q, b_w_o, rel_bias, ffn_norm, ffn_w_up, ffn_conv, ffn_conv_b, ffn_w_down, final_norm, loss_target, m_a_norm, m_a_w_in, m_a_conv, m_a_w_out, m_kv_norm, m_w_kv, m_b_norm, m_b_w_q, m_b_w_o, m_rel_bias, m_ffn_norm, m_ffn_w_up, m_ffn_conv, m_ffn_conv_b, m_ffn_w_down, m_final_norm, v_a_norm, v_a_w_in, v_a_conv, v_a_w_out, v_kv_norm, v_w_kv, v_b_norm, v_b_w_q, v_b_w_o, v_rel_bias, v_ffn_norm, v_ffn_w_up, v_ffn_conv, v_ffn_conv_b, v_ffn_w_down, v_final_norm):
    given = dict(a_norm=a_norm, a_w_in=a_w_in, a_conv=a_conv, a_w_out=a_w_out, kv_norm=kv_norm, w_kv=w_kv, b_norm=b_norm,
                 b_w_q=b_w_q, b_w_o=b_w_o, rel_bias=rel_bias, ffn_norm=ffn_norm, ffn_w_up=ffn_w_up, ffn_conv=ffn_conv,
                 ffn_conv_b=ffn_conv_b, ffn_w_down=ffn_w_down, final_norm=final_norm)
    mom_m = dict(a_norm=m_a_norm, a_w_in=m_a_w_in, a_conv=m_a_conv, a_w_out=m_a_w_out, kv_norm=m_kv_norm, w_kv=m_w_kv,
                 b_norm=m_b_norm, b_w_q=m_b_w_q, b_w_o=m_b_w_o, rel_bias=m_rel_bias, ffn_norm=m_ffn_norm,
                 ffn_w_up=m_ffn_w_up, ffn_conv=m_ffn_conv, ffn_conv_b=m_ffn_conv_b, ffn_w_down=m_ffn_w_down,
                 final_norm=m_final_norm)
    mom_v = dict(a_norm=v_a_norm, a_w_in=v_a_w_in, a_conv=v_a_conv, a_w_out=v_a_w_out, kv_norm=v_kv_norm, w_kv=v_w_kv,
                 b_norm=v_b_norm, b_w_q=v_b_w_q, b_w_o=v_b_w_o, rel_bias=v_rel_bias, ffn_norm=v_ffn_norm,
                 ffn_w_up=v_ffn_w_up, ffn_conv=v_ffn_conv, ffn_conv_b=v_ffn_conv_b, ffn_w_down=v_ffn_w_down,
                 final_norm=v_final_norm)

    shard = {"w_in": ((a_w_in, 0), 1), "w_out": ((a_w_out, 0), 0), "w_kv": (w_kv, 1), "w_q": ((b_w_q, 0), 0),
             "w_o": ((b_w_o, 0), 0), "w_up0": ((ffn_w_up, 0), 1), "w_up1": ((ffn_w_up, 1), 1),
             "w_down0": ((ffn_w_down, 0), 0), "w_down1": ((ffn_w_down, 1), 0)}

    small_sharded = [given[k] for k in SMALL_SHARDED]
    packed = _pack(small_sharded)
    first = ("w_in", "w_out")
    fulls, packed_all = _gather_weights([shard[k][0] for k in first], [shard[k][1] for k in first], first, packed)
    W = dict(zip(first, fulls))
    later = {k: shard[k][0] for k in BIG if k not in first}
    geom = {k: (_shard_shape(shard[k][0]), shard[k][1]) for k in BIG}
    small = {k: given[k] for k in SMALL}
    per_shard = [_unpack(packed_all[j], [a.shape for a in small_sharded]) for j in range(N_SHARDS)]
    for i, k in enumerate(SMALL_SHARDED):
        small[k] = jnp.concatenate([per_shard[j][i] for j in range(N_SHARDS)], axis=-1)

    loss, grad_x, big_g, slots, small_g = _local_step(x, loss_target, W, later, geom, small)

    small_shapes = [small_g[k].shape for k in SMALL] + [(1,)]
    layers_of = {}
    for k in BIG:
        layers_of.setdefault(GRAD_OF[k][0], []).append(k)
    blocks = {}
    for name, members in layers_of.items():
        dest = None
        for k in members:
            dest = _sum_piece(dest, slots[k], big_g[k], geom[k][0], geom[k][1], GRAD_OF[k][1], len(members), f"sum_{k}")
        blocks[name] = dest
    swapped, small_slots = _exchange_small(_pack([small_g[k] for k in SMALL] + [loss.reshape(1)]),
                                           _swap_halves_task(list(blocks.values())), "swap_halves_exchange_small")
    reduced = dict(zip(blocks, swapped))
    *small_sums, loss = _unpack(_sum_slots(small_slots, "sum_small"), small_shapes)
    loss = loss[0]
    small_red = dict(zip(SMALL, small_sums))
    j = 2 * lax.axis_index("x") + lax.axis_index("y")
    for k in SMALL_SHARDED:
        w = given[k].shape[-1]
        small_red[k] = lax.dynamic_slice_in_dim(small_red[k], j * w, w, axis=small_red[k].ndim - 1)

    grads, deltas, new_m, new_v = {}, {}, {}, {}
    for name in WEIGHT_ORDER:
        if name in reduced:
            g = reduced[name].reshape(given[name].shape)
            d, nm, nv = _adamw(_as2d(given[name]), _as2d(g), _as2d(mom_m[name]), _as2d(mom_v[name]), f"adamw_{name}")
            grads[name] = g
            deltas[name], new_m[name], new_v[name] = (t.reshape(given[name].shape) for t in (d, nm, nv))
    small_names = [n for n in WEIGHT_ORDER if n not in reduced]
    for n in small_names:
        grads[n] = small_red[n].reshape(given[n].shape)
    sw, sg, sm, sv = (_pack([d[n] for n in small_names]) for d in (given, grads, mom_m, mom_v))
    d, nm, nv = _adamw(sw, sg, sm, sv, "adamw_small")
    shapes = [given[n].shape for n in small_names]
    for n, a, b_, c_ in zip(small_names, _unpack(d, shapes), _unpack(nm, shapes), _unpack(nv, shapes)):
        deltas[n], new_m[n], new_v[n] = a, b_, c_

    return (loss, grad_x, *[grads[n] for n in WEIGHT_ORDER], *[deltas[n] for n in WEIGHT_ORDER],
            *[new_m[n] for n in WEIGHT_ORDER], *[new_v[n] for n in WEIGHT_ORDER])
```

```python
import functools
import math

import numpy as np

import jax
import jax.numpy as jnp
from jax import lax
from jax.experimental import pallas as pl
from jax.experimental.pallas import tpu as pltpu

F32 = jnp.float32
BF16 = jnp.bfloat16

RMS_EPS = 1e-6
HEAD_DIM = 64
ATT_BLOCK = 128
DILATED_BRANCHES = ((128, 1), (512, 4), (2048, 16))
REL_BUCKETS = 32
REL_MAX_DISTANCE = 2048
MASKED_LOGIT = -1e30
ATTN_FWD_UNROLL = (8, 8, 4)
ATTN_BWD_UNROLL = (8, 8, 8)

ADAM_LR = 0.001
ADAM_B1 = 0.9
ADAM_B2 = 0.999
ADAM_EPS = 1e-08
ADAM_WD = 0.01
ADAM_STEP = 10

LANES = 128
SUBLANES_F32 = 8
SUBLANES_BF16 = 16
VMEM_LIMIT_BYTES = 56 * 1024 * 1024

N_SHARDS = 4
N_DEVICES = 8
ANY = pl.BlockSpec(memory_space=pl.ANY)


def _tile(n, pref, mult):
    best = None
    for t in range(mult, min(n, pref) + 1, mult):
        if n % t == 0:
            best = t
    if best is None:
        raise ValueError(f"no tile for {n} (multiple of {mult}, at most {pref})")
    return best


def _params(*sem):
    return pltpu.CompilerParams(dimension_semantics=sem, vmem_limit_bytes=VMEM_LIMIT_BYTES)


class _Comm:
    def __init__(self, ins, outs, aliases, n_sems, copies):
        self.ins, self.outs, self.aliases, self.n_sems, self.copies = list(ins), list(outs), dict(aliases), n_sems, copies

    @staticmethod
    def join(parts):
        parts = [p for p in parts if p is not None]
        ins, outs, aliases, offs, n_sems = [], [], {}, [], 0
        for p in parts:
            offs.append((len(ins), len(outs), n_sems))
            aliases.update({len(ins) + i: len(outs) + o for i, o in p.aliases.items()})
            ins += p.ins
            outs += p.outs
            n_sems += p.n_sems

        def copies(in_refs, out_refs, sem):
            sends, recvs = [], []
            for p, (i0, o0, s0) in zip(parts, offs):
                s, r = p.copies(in_refs[i0:i0 + len(p.ins)], out_refs[o0:o0 + len(p.outs)],
                                lambda k, s0=s0: sem(s0 + k))
                sends += s
                recvs += r
            return sends, recvs

        return _Comm(ins, outs, aliases, n_sems, copies)


def _pallas(body, *, comm=None, name, out_shape, grid=(), in_specs=(), out_specs=(), scratch_shapes=(),
            compiler_params=None):
    if comm is None:
        return pl.pallas_call(body, name=name, out_shape=out_shape, grid=grid, in_specs=in_specs, out_specs=out_specs,
                              scratch_shapes=scratch_shapes, compiler_params=compiler_params)
    single = not isinstance(out_shape, (tuple, list))
    outs = (out_shape,) if single else tuple(out_shape)
    o_specs = (out_specs,) if single else tuple(out_specs)
    n_in, n_cin, n_out, n_cout, n_scr = len(in_specs), len(comm.ins), len(outs), len(comm.outs), len(scratch_shapes)

    def carried(*refs):
        base_in, c_in = refs[:n_in], refs[n_in:n_in + n_cin]
        o0 = n_in + n_cin
        base_out, c_out = refs[o0:o0 + n_out], refs[o0 + n_out:o0 + n_out + n_cout]
        s0 = o0 + n_out + n_cout
        base_scr, (send_sems, recv_sems) = refs[s0:s0 + n_scr], refs[s0 + n_scr:]
        sem = lambda k: (send_sems.at[k], recv_sems.at[k])
        first = functools.reduce(jnp.logical_and, [pl.program_id(a) == 0 for a in range(len(grid))])
        last = functools.reduce(jnp.logical_and, [pl.program_id(a) == n - 1 for a, n in enumerate(grid)])

        @pl.when(first)
        def _():
            for cp in comm.copies(c_in, c_out, sem)[0]:
                cp.start()

        body(*base_in, *base_out, *base_scr)

        @pl.when(last)
        def _():
            sends, recvs = comm.copies(c_in, c_out, sem)
            for cp in recvs:
                cp.wait_recv()
            for cp in sends:
                cp.wait_send()

    call = pl.pallas_call(
        carried, name=name, out_shape=outs + tuple(comm.outs), grid=grid,
        in_specs=list(in_specs) + [ANY] * n_cin, out_specs=o_specs + (ANY,) * n_cout,
        scratch_shapes=list(scratch_shapes) + [pltpu.SemaphoreType.DMA((comm.n_sems,))] * 2,
        input_output_aliases={n_in + i: n_out + o for i, o in comm.aliases.items()},
        compiler_params=_params(*(["arbitrary"] * len(grid))))

    def run(*args):
        res = call(*args, *comm.ins)
        base = res[0] if single else tuple(res[:n_out])
        return base, list(res[n_out:])

    return run


def _rmsnorm_fwd(x, g, name):
    T, D = x.shape
    tm = _tile(T, 512, SUBLANES_BF16)

    def body(x_ref, g_ref, o_ref):
        xf = x_ref[...]
        r = lax.rsqrt(jnp.mean(xf * xf, axis=-1, keepdims=True) + RMS_EPS)
        o_ref[...] = ((xf * r) * g_ref[...]).astype(o_ref.dtype)

    return pl.pallas_call(
        body, name=name, out_shape=jax.ShapeDtypeStruct((T, D), BF16), grid=(T // tm,),
        in_specs=[pl.BlockSpec((tm, D), lambda i: (i, 0)), pl.BlockSpec((1, D), lambda i: (0, 0))],
        out_specs=pl.BlockSpec((tm, D), lambda i: (i, 0)),
        compiler_params=_params("parallel"),
    )(x, g.reshape(1, D))


def _loss_head(a, b, res, g, target, name):
    T, K = a.shape
    D = b.shape[1]
    tm = _tile(T, 512, SUBLANES_BF16)

    def body(a_ref, b_ref, res_ref, g_ref, t_ref, dh_ref, acc_ref):
        @pl.when(pl.program_id(0) == 0)
        def _():
            acc_ref[...] = jnp.zeros_like(acc_ref)

        xf = jnp.dot(a_ref[...].astype(BF16), b_ref[...], preferred_element_type=F32) + res_ref[...]
        r = lax.rsqrt(jnp.mean(xf * xf, axis=-1, keepdims=True) + RMS_EPS)
        xhat = xf * r
        err = xhat * g_ref[...] - t_ref[...]
        dy = err * (1.0 / D)
        acc_ref[0:1, :] += jnp.sum(dy * xhat, axis=0, keepdims=True)
        acc_ref[1:2, :] += jnp.sum(err * err, axis=0, keepdims=True)
        t = dy * g_ref[...]
        dh_ref[...] = r * (t - xhat * jnp.mean(t * xhat, axis=-1, keepdims=True))

    row = pl.BlockSpec((tm, D), lambda i: (i, 0))
    dh, acc = pl.pallas_call(
        body, name=name,
        out_shape=(jax.ShapeDtypeStruct((T, D), F32), jax.ShapeDtypeStruct((SUBLANES_F32, D), F32)),
        grid=(T // tm,),
        in_specs=[pl.BlockSpec((tm, K), lambda i: (i, 0)), pl.BlockSpec((K, D), lambda i: (0, 0)), row,
                  pl.BlockSpec((1, D), lambda i: (0, 0)), row],
        out_specs=(row, pl.BlockSpec((SUBLANES_F32, D), lambda i: (0, 0))),
        compiler_params=_params("arbitrary"),
    )(a, b, res, g.reshape(1, D), target)
    return acc[1], dh, acc[0]


def _mm_nn(a, b, res, out_dtype, name, comm=None, norm_gains=()):
    T, K = a.shape
    N = b.shape[1]
    n_g = len(norm_gains)
    tm = _tile(T, 512 if n_g else 1024, SUBLANES_BF16)
    tn = _tile(N, 3072, LANES)
    assert not n_g or tn == N, "the fused rmsnorm needs whole rows in one tile"

    def body(a_ref, b_ref, *rest):
        ins, outs = rest[:len(rest) - 1 - n_g], rest[len(rest) - 1 - n_g:]
        acc = jnp.dot(a_ref[...].astype(BF16), b_ref[...], preferred_element_type=F32)
        if res is not None:
            acc = acc + ins[0][...]
        outs[0][...] = acc.astype(outs[0].dtype)
        if n_g:
            y = acc * lax.rsqrt(jnp.mean(acc * acc, axis=-1, keepdims=True) + RMS_EPS)
            for g_ref, xn_ref in zip(ins[len(ins) - n_g:], outs[1:]):
                xn_ref[...] = (y * g_ref[...]).astype(xn_ref.dtype)

    tile = pl.BlockSpec((tm, tn), lambda j, i: (i, j))
    in_specs = [pl.BlockSpec((tm, K), lambda j, i: (i, 0)), pl.BlockSpec((K, tn), lambda j, i: (0, j))]
    args = [a, b]
    if res is not None:
        in_specs.append(tile)
        args.append(res)
    for g in norm_gains:
        in_specs.append(pl.BlockSpec((1, N), lambda j, i: (0, 0)))
        args.append(g.reshape(1, N))
    out_shape = jax.ShapeDtypeStruct((T, N), out_dtype)
    if n_g:
        out_shape = (out_shape,) + (jax.ShapeDtypeStruct((T, N), BF16),) * n_g
    return _pallas(
        body, comm=comm, name=name, out_shape=out_shape, grid=(N // tn, T // tm),
        in_specs=in_specs, out_specs=(tile,) * (1 + n_g) if n_g else tile,
        compiler_params=_params("parallel", "parallel"),
    )(*args)


def _mm_nt(dy, b, out_dtype, name, comm=None, norm=None, head_dot=None):
    dys = list(dy) if isinstance(dy, (list, tuple)) else [dy]
    T, n_each = dys[0].shape
    K = b.shape[0]
    tm = _tile(T, 2048 if norm is None and head_dot is None else 1024, SUBLANES_BF16)
    tk = _tile(K, 1536, LANES)
    tn = _tile(n_each, 2816 if norm is None else 1536, LANES)
    per = n_each // tn
    n_steps = per * len(dys)
    assert norm is None or tk == K, "the fused rmsnorm backward needs whole rows in one tile"

    def body(*refs):
        dy_refs, b_ref, acc_ref = refs[:len(dys)], refs[len(dys)], refs[-1]
        i, n = pl.program_id(0), pl.program_id(2)

        @pl.when(n == 0)
        def _():
            acc_ref[...] = jnp.zeros_like(acc_ref)

        for p, dy_ref in enumerate(dy_refs):
            @pl.when(jnp.logical_and(n >= p * per, n < (p + 1) * per))
            def _(dy_ref=dy_ref):
                acc_ref[...] += lax.dot_general(dy_ref[...].astype(BF16), b_ref[...], (((1,), (1,)), ((), ())),
                                                preferred_element_type=F32)

        if norm is None and head_dot is None:
            @pl.when(n == n_steps - 1)
            def _():
                refs[-2][...] = acc_ref[...].astype(refs[-2].dtype)
        elif norm is None:
            o_ref, out_ref, dot_ref = refs[len(dys) + 1:-1]

            @pl.when(n == n_steps - 1)
            def _():
                d = acc_ref[...]
                out_ref[...] = d.astype(out_ref.dtype)
                head0 = lax.broadcasted_iota(jnp.int32, (tm, LANES), 1) < HEAD_DIM
                for c0 in range(0, tk, LANES):
                    x = d[:, c0:c0 + LANES] * o_ref[:, c0:c0 + LANES]
                    d0 = jnp.sum(jnp.where(head0, x, 0.0), axis=-1, keepdims=True)
                    d1 = jnp.sum(jnp.where(head0, 0.0, x), axis=-1, keepdims=True)
                    dot_ref[:, c0:c0 + LANES] = jnp.where(head0, d0, d1)
        else:
            x_ref, g_ref, dres_ref, dx_ref, dg_ref = refs[len(dys) + 1:-1]

            @pl.when(jnp.logical_and(i == 0, n == 0))
            def _():
                dg_ref[...] = jnp.zeros_like(dg_ref)

            @pl.when(n == n_steps - 1)
            def _():
                xf = x_ref[...]
                r = lax.rsqrt(jnp.mean(xf * xf, axis=-1, keepdims=True) + RMS_EPS)
                xhat = xf * r
                d = acc_ref[...]
                dg_ref[0:1, :] += jnp.sum(d * xhat, axis=0, keepdims=True)
                t = d * g_ref[...]
                dx_ref[...] = dres_ref[...] + r * (t - xhat * jnp.mean(t * xhat, axis=-1, keepdims=True))

    in_specs = [pl.BlockSpec((tm, tn), lambda i, k, n, p=p: (i, jnp.clip(n - p * per, 0, per - 1))) for p in range(len(dys))]
    in_specs.append(pl.BlockSpec((tk, tn), lambda i, k, n: (k, n)))
    args = dys + [b]
    tile = pl.BlockSpec((tm, tk), lambda i, k, n: (i, k))
    if norm is None and head_dot is None:
        out_shape, out_specs = jax.ShapeDtypeStruct((T, K), out_dtype), tile
    elif norm is None:
        in_specs.append(tile)
        args.append(head_dot)
        out_shape = (jax.ShapeDtypeStruct((T, K), out_dtype), jax.ShapeDtypeStruct((T, K), F32))
        out_specs = (tile, tile)
    else:
        x, g, dres = norm
        in_specs += [tile, pl.BlockSpec((1, K), lambda i, k, n: (0, 0)), tile]
        args += [x, g.reshape(1, K), dres]
        out_shape = (jax.ShapeDtypeStruct((T, K), F32), jax.ShapeDtypeStruct((SUBLANES_F32, K), F32))
        out_specs = (tile, pl.BlockSpec((SUBLANES_F32, K), lambda i, k, n: (0, 0)))
    return _pallas(
        body, comm=comm, name=name, out_shape=out_shape, grid=(T // tm, K // tk, n_steps),
        in_specs=in_specs, out_specs=out_specs, scratch_shapes=[pltpu.VMEM((tm, tk), F32)],
        compiler_params=_params("parallel", "parallel", "arbitrary") if norm is None else _params(*["arbitrary"] * 3),
    )(*args)


def _mm_tn(a, dy, name):
    T, K = a.shape
    N = dy.shape[1]
    tt = _tile(T, 2048, SUBLANES_BF16)
    tk = _tile(K, 1536, LANES)
    tn = _tile(N, 1536, LANES)
    t_steps = T // tt

    def body(a_ref, dy_ref, o_ref, acc_ref):
        t = pl.program_id(2)

        @pl.when(t == 0)
        def _():
            acc_ref[...] = jnp.zeros_like(acc_ref)

        acc_ref[...] += lax.dot_general(a_ref[...].astype(BF16), dy_ref[...].astype(BF16),
                                        (((0,), (0,)), ((), ())), preferred_element_type=F32)

        @pl.when(t == t_steps - 1)
        def _():
            o_ref[...] = acc_ref[...].astype(o_ref.dtype)

    return pl.pallas_call(
        body, name=name, out_shape=jax.ShapeDtypeStruct((K, N), BF16), grid=(K // tk, N // tn, t_steps),
        in_specs=[pl.BlockSpec((tt, tk), lambda k, n, t: (t, k)), pl.BlockSpec((tt, tn), lambda k, n, t: (t, n))],
        out_specs=pl.BlockSpec((tk, tn), lambda k, n, t: (k, n)),
        scratch_shapes=[pltpu.VMEM((tk, tn), F32)],
        compiler_params=_params("parallel", "parallel", "arbitrary"),
    )(a, dy)


def _rows_before(halo, cur, k):
    h = halo.shape[0]
    return pltpu.roll(jnp.concatenate([halo, cur], axis=0), k, 0)[h:]


def _rows_after(cur, halo, k):
    n = cur.shape[0]
    total = n + halo.shape[0]
    return pltpu.roll(jnp.concatenate([cur, halo], axis=0), total - k, 0)[:n]


def _halo_specs(tm, width, n_rows):
    per = tm // SUBLANES_BF16
    last = n_rows // SUBLANES_BF16 - 1
    prev = pl.BlockSpec((SUBLANES_BF16, width), lambda i: (jnp.maximum(i * per - 1, 0), 0))
    nxt = pl.BlockSpec((SUBLANES_BF16, width), lambda i: (jnp.minimum((i + 1) * per, last), 0))
    return prev, nxt


def _gate_a_fwd(p, cw, seq, name, comm=None):
    T, D3 = p.shape
    D = D3 // 3
    tm = _tile(seq, 512, SUBLANES_BF16)
    cc = _tile(D, 256, LANES)
    prev, _ = _halo_specs(tm, D3, T)

    def body(p_ref, ph_ref, cw_ref, z_ref):
        at_start = (pl.program_id(0) * tm) % seq == 0
        for c0 in range(0, D, cc):
            b = p_ref[:, c0:c0 + cc].astype(F32)
            u = p_ref[:, D + c0:D + c0 + cc].astype(F32) * p_ref[:, 2 * D + c0:2 * D + c0 + cc].astype(F32)
            uh = ph_ref[:, D + c0:D + c0 + cc].astype(F32) * ph_ref[:, 2 * D + c0:2 * D + c0 + cc].astype(F32)
            uh = jnp.where(at_start, 0.0, uh)
            w = cw_ref[:, c0:c0 + cc]
            cv = _rows_before(uh, u, 2) * w[0:1] + _rows_before(uh, u, 1) * w[1:2] + u * w[2:3]
            z_ref[:, c0:c0 + cc] = (b * cv).astype(z_ref.dtype)

    return _pallas(
        body, comm=comm, name=name, out_shape=jax.ShapeDtypeStruct((T, D), BF16), grid=(T // tm,),
        in_specs=[pl.BlockSpec((tm, D3), lambda i: (i, 0)), prev, pl.BlockSpec((3, D), lambda i: (0, 0))],
        out_specs=pl.BlockSpec((tm, D), lambda i: (i, 0)),
        compiler_params=_params("parallel"),
    )(p, p, cw)


def _gate_a_bwd(p, cw, dz, seq, name, comm=None):
    T, D3 = p.shape
    D = D3 // 3
    tm = _tile(seq, 512, SUBLANES_BF16)
    cc = _tile(D, 256, LANES)
    p_prev, p_next = _halo_specs(tm, D3, T)
    _, dz_next = _halo_specs(tm, D, T)

    def body(p_ref, pp_ref, pn_ref, dz_ref, dzn_ref, cw_ref, dp_ref, dcw_ref):
        i = pl.program_id(0)

        @pl.when(i == 0)
        def _():
            dcw_ref[...] = jnp.zeros_like(dcw_ref)

        at_start = (i * tm) % seq == 0
        at_end = ((i + 1) * tm) % seq == 0
        for c0 in range(0, D, cc):
            cb, cc_, ch = slice(c0, c0 + cc), slice(D + c0, D + c0 + cc), slice(2 * D + c0, 2 * D + c0 + cc)
            b = p_ref[:, cb].astype(F32)
            c = p_ref[:, cc_].astype(F32)
            hh = p_ref[:, ch].astype(F32)
            u = c * hh
            uh = jnp.where(at_start, 0.0, pp_ref[:, cc_].astype(F32) * pp_ref[:, ch].astype(F32))
            w = cw_ref[:, cb]
            u1 = _rows_before(uh, u, 1)
            u2 = _rows_before(uh, u, 2)
            cv = u2 * w[0:1] + u1 * w[1:2] + u * w[2:3]
            dz_t = dz_ref[:, cb].astype(F32)
            dcv = dz_t * b
            dcvn = jnp.where(at_end, 0.0, dzn_ref[:, cb].astype(F32) * pn_ref[:, cb].astype(F32))
            du = dcv * w[2:3] + _rows_after(dcv, dcvn, 1) * w[1:2] + _rows_after(dcv, dcvn, 2) * w[0:1]
            dp_ref[:, cb] = (dz_t * cv).astype(dp_ref.dtype)
            dp_ref[:, cc_] = (du * hh).astype(dp_ref.dtype)
            dp_ref[:, ch] = (du * c).astype(dp_ref.dtype)
            dcw_ref[0:1, cb] += jnp.sum(dcv * u2, axis=0, keepdims=True)
            dcw_ref[1:2, cb] += jnp.sum(dcv * u1, axis=0, keepdims=True)
            dcw_ref[2:3, cb] += jnp.sum(dcv * u, axis=0, keepdims=True)

    res = _pallas(
        body, comm=comm, name=name,
        out_shape=(jax.ShapeDtypeStruct((T, D3), BF16), jax.ShapeDtypeStruct((SUBLANES_F32, D), F32)),
        grid=(T // tm,),
        in_specs=[pl.BlockSpec((tm, D3), lambda i: (i, 0)), p_prev, p_next,
                  pl.BlockSpec((tm, D), lambda i: (i, 0)), dz_next, pl.BlockSpec((3, D), lambda i: (0, 0))],
        out_specs=(pl.BlockSpec((tm, D3), lambda i: (i, 0)), pl.BlockSpec((SUBLANES_F32, D), lambda i: (0, 0))),
        compiler_params=_params("arbitrary"),
    )(p, p, p, dz, dz, cw)
    if comm is None:
        return res[0], res[1][0:3]
    return res[0][0], res[0][1][0:3], res[1]


def _ffn_gate_fwd(u, cw, cb, seq, name, comm=None):
    T, F2 = u.shape
    F = F2 // 2
    tm = _tile(seq, 256, SUBLANES_BF16)
    cc = _tile(F, 256, LANES)
    prev, _ = _halo_specs(tm, F2, T)

    def body(u_ref, uh_ref, cw_ref, cb_ref, a_ref, v_ref):
        at_start = (pl.program_id(0) * tm) % seq == 0

        def conv(c0):
            cols = slice(c0, c0 + cc)
            cur = u_ref[:, cols].astype(F32)
            halo = jnp.where(at_start, 0.0, uh_ref[:, cols].astype(F32))
            w = cw_ref[:, cols]
            return (_rows_before(halo, cur, 2) * w[0:1] + _rows_before(halo, cur, 1) * w[1:2] + cur * w[2:3]
                    + cb_ref[:, cols])

        for c0 in range(0, F, cc):
            g = conv(c0)
            up = conv(F + c0)
            a_ref[:, c0:c0 + cc] = ((g * jax.nn.sigmoid(g)) * up).astype(a_ref.dtype)
            v_ref[:, c0:c0 + cc] = g.astype(v_ref.dtype)
            v_ref[:, F + c0:F + c0 + cc] = up.astype(v_ref.dtype)

    return _pallas(
        body, comm=comm, name=name,
        out_shape=(jax.ShapeDtypeStruct((T, F), BF16), jax.ShapeDtypeStruct((T, F2), BF16)), grid=(T // tm,),
        in_specs=[pl.BlockSpec((tm, F2), lambda i: (i, 0)), prev,
                  pl.BlockSpec((3, F2), lambda i: (0, 0)), pl.BlockSpec((1, F2), lambda i: (0, 0))],
        out_specs=(pl.BlockSpec((tm, F), lambda i: (i, 0)), pl.BlockSpec((tm, F2), lambda i: (i, 0))),
        compiler_params=_params("parallel"),
    )(u, u, cw, cb.reshape(1, F2))


def _ffn_gate_bwd(u, v, cw, da, seq, name, comm=None):
    T, F2 = u.shape
    F = F2 // 2
    H = SUBLANES_BF16
    tm = _tile(seq, 256, H)
    cc = _tile(F, 256, LANES)
    _, v_next = _halo_specs(tm, F2, T)
    _, da_next = _halo_specs(tm, F, T)

    def body(u_ref, v_ref, vn_ref, da_ref, dan_ref, cw_ref, du_ref, acc_ref):
        i = pl.program_id(0)

        @pl.when(i == 0)
        def _():
            acc_ref[...] = jnp.zeros_like(acc_ref)

        at_end = ((i + 1) * tm) % seq == 0
        n = tm + H

        def rows_and_next(ref, nxt, cols):
            return jnp.concatenate([ref[:, cols].astype(F32), nxt[:, cols].astype(F32)], axis=0)

        def back(d, cols):
            w = cw_ref[:, cols]
            d0 = d[:tm]
            d1 = pltpu.roll(d, n - 1, 0)[:tm]
            d2 = pltpu.roll(d, n - 2, 0)[:tm]
            du_ref[:, cols] = (d0 * w[2:3] + d1 * w[1:2] + d2 * w[0:1]).astype(du_ref.dtype)
            ut = u_ref[:, cols].astype(F32)
            acc_ref[0:1, cols] += jnp.sum(d2 * ut, axis=0, keepdims=True)
            acc_ref[1:2, cols] += jnp.sum(d1 * ut, axis=0, keepdims=True)
            acc_ref[2:3, cols] += jnp.sum(d0 * ut, axis=0, keepdims=True)
            acc_ref[3:4, cols] += jnp.sum(d0, axis=0, keepdims=True)

        for c0 in range(0, F, cc):
            gc, uc = slice(c0, c0 + cc), slice(F + c0, F + c0 + cc)
            g = rows_and_next(v_ref, vn_ref, gc)
            up = rows_and_next(v_ref, vn_ref, uc)
            da_ext = jnp.concatenate([da_ref[:, gc].astype(F32),
                                      jnp.where(at_end, 0.0, dan_ref[:, gc].astype(F32))], axis=0)
            sg = jax.nn.sigmoid(g)
            back(da_ext * up * (sg * (1.0 + g * (1.0 - sg))), gc)
            back(da_ext * (g * sg), uc)

    res = _pallas(
        body, comm=comm, name=name,
        out_shape=(jax.ShapeDtypeStruct((T, F2), BF16), jax.ShapeDtypeStruct((SUBLANES_F32, F2), F32)),
        grid=(T // tm,),
        in_specs=[pl.BlockSpec((tm, F2), lambda i: (i, 0)), pl.BlockSpec((tm, F2), lambda i: (i, 0)), v_next,
                  pl.BlockSpec((tm, F), lambda i: (i, 0)), da_next, pl.BlockSpec((3, F2), lambda i: (0, 0))],
        out_specs=(pl.BlockSpec((tm, F2), lambda i: (i, 0)), pl.BlockSpec((SUBLANES_F32, F2), lambda i: (0, 0))),
        compiler_params=_params("arbitrary"),
    )(u, v, v, da, da, cw)
    (du, acc), landed = res if comm is not None else (res, None)
    return (du, acc[0:3], acc[3]) if comm is None else (du, acc[0:3], acc[3], landed)


def _bucket_map():
    P = ATT_BLOCK
    qi = np.arange(P, dtype=np.int64)[:, None]
    kc = np.arange(2 * P, dtype=np.int64)[None, :]
    delta = qi + P - kc
    maps = []
    max_exact = REL_BUCKETS // 2
    for window, dilation in DILATED_BRANCHES:
        band = (delta >= 0) & (delta <= window // dilation)
        n = np.maximum(delta * dilation, 0)
        nf = np.maximum(n, max_exact).astype(np.float32)
        large = max_exact + (np.log(nf / np.float32(max_exact)) / np.float32(math.log(REL_MAX_DISTANCE / max_exact))
                             * np.float32(REL_BUCKETS - max_exact)).astype(np.int32)
        large = np.minimum(large, REL_BUCKETS - 1)
        bucket = np.where(n < max_exact, n, large)
        maps.append(np.where(band, bucket, -1).astype(np.int32))
    return np.stack(maps)


def _bias_tables(rel_bias, bmap, name):
    n_pairs = rel_bias.shape[1] // 2
    nbr, P, P2 = bmap.shape

    def body(rb_ref, bm_ref, o_ref):
        pair = pl.program_id(0)
        in_seq = lax.broadcasted_iota(jnp.int32, (P, P2), 1) >= P
        for br in range(nbr):
            bm = bm_ref[br]
            for hh in range(2):
                acc = jnp.full((P, P2), MASKED_LOGIT, F32)
                for b in range(REL_BUCKETS):
                    acc = jnp.where(bm == b, rb_ref[b, 2 * pair + hh], acc)
                o_ref[br, 0, 0, hh * P:(hh + 1) * P, :] = acc
                o_ref[br, 0, 1, hh * P:(hh + 1) * P, :] = jnp.where(in_seq, acc, MASKED_LOGIT)

    return pl.pallas_call(
        body, name=name, out_shape=jax.ShapeDtypeStruct((nbr, n_pairs, 2, 2 * P, P2), F32), grid=(n_pairs,),
        in_specs=[pl.BlockSpec(memory_space=pltpu.SMEM), pl.BlockSpec((nbr, P, P2), lambda h: (0, 0, 0))],
        out_specs=pl.BlockSpec((nbr, 1, 2, 2 * P, P2), lambda h: (0, h, 0, 0, 0)),
        compiler_params=_params("parallel"),
    )(rel_bias, bmap)


def _bias_grad(dbias, bmap, name):
    nbr, n_pairs, _, P2 = dbias.shape
    P = P2 // 2

    def body(db_ref, bm_ref, o_ref):
        lane = lax.broadcasted_iota(jnp.int32, (1, LANES), 1)
        for hh in range(2):
            row = jnp.zeros((1, LANES), F32)
            for br in range(nbr):
                bm = bm_ref[br]
                d = db_ref[br, 0, hh * P:(hh + 1) * P, :]
                for b in range(REL_BUCKETS):
                    hit = jnp.sum(jnp.where(bm == b, d, 0.0), axis=1, keepdims=True)
                    row = row + jnp.where(lane == b, jnp.sum(hit, axis=0, keepdims=True), 0.0)
            o_ref[hh] = row

    return pl.pallas_call(
        body, name=name, out_shape=jax.ShapeDtypeStruct((2 * n_pairs, 1, LANES), F32), grid=(n_pairs,),
        in_specs=[pl.BlockSpec((nbr, 1, P2, P2), lambda h: (0, h, 0, 0)), pl.BlockSpec((nbr, P, P2), lambda h: (0, 0, 0))],
        out_specs=pl.BlockSpec((2, 1, LANES), lambda h: (h, 0, 0)),
        compiler_params=_params("parallel"),
    )(dbias, bmap)[:, 0, :]


def _rows(start, dilation):
    if dilation == 1:
        return pl.ds(pl.multiple_of(start, ATT_BLOCK), ATT_BLOCK)
    return pl.ds(start, ATT_BLOCK, stride=dilation)


def _for_each_block(seq, unroll, fn):
    P = ATT_BLOCK
    for br, (_, d) in enumerate(DILATED_BRANCHES):
        nb = seq // d // P
        u = unroll[br] if (unroll[br] % nb == 0 or nb % unroll[br] == 0) else 1
        step = d * P

        def some(i, carry, br=br, d=d, nb=nb, u=u, step=step):
            blocks = []
            if u % nb == 0:
                for k in range(u):
                    if k % nb == 0:
                        start = i * (u // nb) + k // nb
                        blocks.append((start, start, 1))
                    else:
                        blocks.append((blocks[-1][0] + step, blocks[-1][0], 0))
            else:
                r, j0 = (i * u) // nb, (i * u) % nb
                blocks.append((r + j0 * step, r + jnp.maximum(j0 - 1, 0) * step, jnp.where(j0 == 0, 1, 0)))
                for _ in range(1, u):
                    blocks.append((blocks[-1][0] + step, blocks[-1][0], 0))
            fn(br, d, blocks)
            return carry

        lax.fori_loop(0, nb * d // u, some, 0)


class _RowCache:
    def __init__(self, dilation):
        self.dilation, self.seen = dilation, {}

    def rows(self, ref, start):
        key = (id(ref), id(start))
        if key not in self.seen:
            self.seen[key] = ref[_rows(start, self.dilation), :].astype(BF16)
        return self.seen[key]

    def window(self, ref, start, prev):
        return jnp.concatenate([self.rows(ref, prev), self.rows(ref, start)], axis=0)


def _stack_heads(x, head0):
    return jnp.concatenate([jnp.where(head0, x, 0.0), jnp.where(head0, 0.0, x)], axis=0).astype(BF16)


def _attn_fwd(q, kv, bias, name):
    B, S, D = q.shape
    P = ATT_BLOCK
    n_pairs = D // LANES
    nbr = len(DILATED_BRANCHES)
    scale = HEAD_DIM ** -0.5

    def body(q_ref, k_ref, v_ref, bias_ref, o_ref, lse_ref, *stats):
        m_s, l_s, acc_s = stats[0:nbr], stats[nbr:2 * nbr], stats[2 * nbr:3 * nbr]
        head0 = lax.broadcasted_iota(jnp.int32, (P, LANES), 1) < HEAD_DIM

        def block(br, d, blocks):
            cache = _RowCache(d)
            s = [lax.dot_general(_stack_heads(q_ref[_rows(start, d), :] * scale, head0),
                                 cache.window(k_ref, start, prev), (((1,), (1,)), ((), ())),
                                 preferred_element_type=F32) + bias_ref[br, 0, first]
                 for start, prev, first in blocks]
            m = [jnp.max(x, axis=-1, keepdims=True) for x in s]
            p = [jnp.exp(x - y) for x, y in zip(s, m)]
            l = [jnp.sum(x, axis=-1, keepdims=True) for x in p]
            pv = [jnp.dot(x.astype(BF16), cache.window(v_ref, start, prev), preferred_element_type=F32)
                  for x, (start, prev, _) in zip(p, blocks)]
            for k, (start, _, _) in enumerate(blocks):
                rows = _rows(start, d)
                m_s[br][rows, :] = jnp.where(head0, m[k][:P], m[k][P:])
                l_s[br][rows, :] = jnp.where(head0, l[k][:P], l[k][P:])
                acc_s[br][rows, :] = jnp.where(head0, pv[k][:P], pv[k][P:])

        _for_each_block(S, ATTN_FWD_UNROLL, block)

        chunk = _tile(S, 256, SUBLANES_F32)

        def merge(i, carry):
            rows = pl.ds(pl.multiple_of(i * chunk, chunk), chunk)
            ms = [m_s[br][rows, :] for br in range(nbr)]
            m = functools.reduce(jnp.maximum, ms)
            l = jnp.zeros((chunk, LANES), F32)
            acc = jnp.zeros((chunk, LANES), F32)
            for br in range(nbr):
                w = jnp.exp(ms[br] - m)
                l = l + w * l_s[br][rows, :]
                acc = acc + w * acc_s[br][rows, :]
            o_ref[rows, :] = acc / l
            lse_ref[rows, :] = m + jnp.log(l)
            return carry

        lax.fori_loop(0, S // chunk, merge, 0)

    slab = lambda col0: pl.BlockSpec((None, S, LANES), lambda b, h: (b, 0, col0 + h))
    return pl.pallas_call(
        body, name=name,
        out_shape=(jax.ShapeDtypeStruct((B, S, D), F32), jax.ShapeDtypeStruct((B, S, D), F32)),
        grid=(B, n_pairs),
        in_specs=[slab(0), slab(0), slab(n_pairs),
                  pl.BlockSpec((nbr, 1, 2, 2 * P, 2 * P), lambda b, h: (0, h, 0, 0, 0))],
        out_specs=(slab(0), slab(0)),
        scratch_shapes=[pltpu.VMEM((S, LANES), F32)] * (3 * nbr),
        compiler_params=_params("parallel", "parallel"),
    )(q, kv, kv, bias)


def _attn_bwd(q, kv, delta, lse, do, bias, name, comm=None):
    B, S, D = q.shape
    P = ATT_BLOCK
    n_pairs = D // LANES
    nbr = len(DILATED_BRANCHES)
    scale = HEAD_DIM ** -0.5

    def body(q_ref, k_ref, v_ref, delta_s, lse_ref, do_ref, bias_ref, dq_ref, dk_ref, dv_ref, dbias_ref):
        head0 = lax.broadcasted_iota(jnp.int32, (P, LANES), 1) < HEAD_DIM

        @pl.when(pl.program_id(1) == 0)
        def _():
            dbias_ref[...] = jnp.zeros_like(dbias_ref)

        chunk = _tile(S, 512, SUBLANES_F32)

        def prepare(i, carry):
            rows = pl.ds(pl.multiple_of(i * chunk, chunk), chunk)
            zero = jnp.zeros((chunk, LANES), F32)
            dq_ref[rows, :] = zero
            dk_ref[rows, :] = zero
            dv_ref[rows, :] = zero
            return carry

        lax.fori_loop(0, S // chunk, prepare, 0)

        def per_head(x):
            return jnp.concatenate([x[:, 0:1], x[:, HEAD_DIM:HEAD_DIM + 1]], axis=0)

        nt = (((1,), (1,)), ((), ()))
        tn = (((0,), (0,)), ((), ()))

        def block(br, d, blocks):
            cache = _RowCache(d)
            q2 = [_stack_heads(q_ref[_rows(start, d), :] * scale, head0) for start, _, _ in blocks]
            do2 = [_stack_heads(do_ref[_rows(start, d), :], head0) for start, _, _ in blocks]
            kb = [cache.window(k_ref, start, prev) for start, prev, _ in blocks]
            vb = [cache.window(v_ref, start, prev) for start, prev, _ in blocks]
            s = [lax.dot_general(a, b, nt, preferred_element_type=F32) + bias_ref[br, 0, first]
                 for a, b, (_, _, first) in zip(q2, kb, blocks)]
            dp = [lax.dot_general(a, b, nt, preferred_element_type=F32) for a, b in zip(do2, vb)]
            p = [jnp.exp(x - per_head(lse_ref[_rows(start, d), :])) for x, (start, _, _) in zip(s, blocks)]
            ds = [x * (y - per_head(delta_s[_rows(start, d), :])) for x, y, (start, _, _) in zip(p, dp, blocks)]
            for x in ds:
                dbias_ref[br, 0] += x
            ds16 = [x.astype(BF16) for x in ds]
            dq2 = [jnp.dot(a, b, preferred_element_type=F32) for a, b in zip(ds16, kb)]
            dk = [lax.dot_general(a, b, tn, preferred_element_type=F32) for a, b in zip(ds16, q2)]
            dv = [lax.dot_general(a.astype(BF16), b, tn, preferred_element_type=F32) for a, b in zip(p, do2)]
            parts = {}
            for k, (start, prev, first) in enumerate(blocks):
                dq_ref[_rows(start, d), :] += jnp.where(head0, dq2[k][:P], dq2[k][P:]) * scale
                parts.setdefault(id(start), [start, []])[1].append((dk[k][P:], dv[k][P:]))
                if not (isinstance(first, int) and first == 1):
                    parts.setdefault(id(prev), [prev, []])[1].append((dk[k][:P], dv[k][:P]))
            for start, terms in parts.values():
                rows = _rows(start, d)
                dk_ref[rows, :] += functools.reduce(jnp.add, [t[0] for t in terms])
                dv_ref[rows, :] += functools.reduce(jnp.add, [t[1] for t in terms])

        _for_each_block(S, ATTN_BWD_UNROLL, block)

    slab = lambda col0: pl.BlockSpec((None, S, LANES), lambda h, b: (b, 0, col0 + h))
    tab = pl.BlockSpec((nbr, 1, 2, 2 * P, 2 * P), lambda h, b: (0, h, 0, 0, 0))
    dtab = pl.BlockSpec((nbr, 1, 2 * P, 2 * P), lambda h, b: (0, h, 0, 0))
    shp = jax.ShapeDtypeStruct((B, S, D), F32)
    return _pallas(
        body, comm=comm, name=name,
        out_shape=(shp, shp, shp, jax.ShapeDtypeStruct((nbr, n_pairs, 2 * P, 2 * P), F32)),
        grid=(n_pairs, B),
        in_specs=[slab(0), slab(0), slab(n_pairs), slab(0), slab(0), slab(0), tab],
        out_specs=(slab(0), slab(0), slab(0), dtab),
        compiler_params=_params("parallel", "arbitrary"),
    )(q, kv, kv, delta, lse, do, bias)


def _adamw(w, g, m, v, name):
    R, C = w.shape
    tr = _tile(R, 256, SUBLANES_F32) if R % SUBLANES_F32 == 0 else R
    tc = _tile(C, 2048, LANES) if C % LANES == 0 else C

    def body(w_ref, g_ref, m_ref, v_ref, d_ref, nm_ref, nv_ref):
        g_ = g_ref[...]
        m2 = ADAM_B1 * m_ref[...] + (1.0 - ADAM_B1) * g_
        v2 = ADAM_B2 * v_ref[...] + (1.0 - ADAM_B2) * (g_ * g_)
        m_hat = m2 / (1.0 - ADAM_B1 ** ADAM_STEP)
        v_hat = v2 / (1.0 - ADAM_B2 ** ADAM_STEP)
        d_ref[...] = -ADAM_LR * (m_hat / (jnp.sqrt(v_hat) + ADAM_EPS) + ADAM_WD * w_ref[...])
        nm_ref[...] = m2
        nv_ref[...] = v2

    blk = pl.BlockSpec((tr, tc), lambda i, j: (i, j))
    shp = jax.ShapeDtypeStruct((R, C), F32)
    return pl.pallas_call(
        body, name=name, out_shape=(shp, shp, shp), grid=(R // tr, C // tc),
        in_specs=[blk] * 4, out_specs=(blk,) * 3, compiler_params=_params("parallel", "parallel"),
    )(w, g, m, v)


def _sum_slots(slots, name):
    n, R, C = slots.shape
    tr = _tile(R, 256, SUBLANES_BF16) if R % SUBLANES_BF16 == 0 else R
    tc = _tile(C, 2048, LANES) if C % LANES == 0 else C

    def body(s_ref, o_ref):
        acc = s_ref[0].astype(F32)
        for k in range(1, n):
            acc = acc + s_ref[k].astype(F32)
        o_ref[...] = acc

    return pl.pallas_call(
        body, name=name, out_shape=jax.ShapeDtypeStruct((R, C), F32), grid=(R // tr, C // tc),
        in_specs=[pl.BlockSpec((n, tr, tc), lambda i, j: (0, i, j))],
        out_specs=pl.BlockSpec((tr, tc), lambda i, j: (i, j)),
        compiler_params=_params("parallel", "parallel"),
    )(slots)


def _my_place():
    return lax.axis_index("x"), lax.axis_index("y"), lax.axis_index("c")


def _other_chips(x, y):
    return [(1 - x, y), (x, 1 - y), (1 - x, 1 - y)]


def _piece(ref, blk, axis, shard, half):
    h0 = blk[0] // 2
    idx = []
    for dim, n in enumerate(blk):
        if dim == 0:
            start = half * h0 + (shard * n if axis == 0 else 0)
            idx.append(pl.ds(start, h0))
        elif dim == axis:
            idx.append(pl.ds(shard * n, n))
        else:
            idx.append(slice(None))
    return ref.at[tuple(idx)]


def _half_of(ref, blk, half):
    return ref.at[pl.ds(half * (blk[0] // 2), blk[0] // 2)]


def _gather_weights(shards, axes, names, small):
    n = len(shards)
    blks = [_shard_shape(s) for s in shards]
    task = _gather_ici_task(shards, axes, names)

    def stage1(*refs):
        small_in, outs, small_out = refs[n], refs[n + 1:2 * n + 1], refs[2 * n + 1]
        send_sems, recv_sems, local_sems = refs[2 * n + 2:]
        sem = lambda k: (send_sems.at[k], recv_sems.at[k])
        x, y, c = _my_place()
        me = 2 * x + y
        local = pltpu.make_async_copy(small_in, small_out.at[me], local_sems.at[0])
        local.start()
        sends, recvs = task.copies(None, outs, sem)
        for k, (px, py) in enumerate(_other_chips(x, y)):
            sends.append(_remote(small_in, small_out.at[me], sem(task.n_sems + k), (px, py, c)))
            recvs.append(_remote(small_in, small_out.at[2 * px + py], sem(task.n_sems + k), (px, py, c)))
        for cp in sends:
            cp.start()
        for cp in recvs:
            cp.wait_recv()
        for cp in sends:
            cp.wait_send()
        local.wait()

    res = pl.pallas_call(
        stage1, name="gather_weights_ici",
        out_shape=task.outs + [jax.ShapeDtypeStruct((N_SHARDS,) + small.shape, small.dtype)],
        in_specs=[ANY] * (n + 1), out_specs=[ANY] * (n + 1), input_output_aliases={a: a for a in range(n)},
        scratch_shapes=[pltpu.SemaphoreType.DMA((task.n_sems + 3,)), pltpu.SemaphoreType.DMA((task.n_sems + 3,)),
                        pltpu.SemaphoreType.DMA((1,))],
    )(*task.ins, small)
    full = _run_comm(_gather_d2d_task(list(res[:n]), blks, axes), "gather_weights_d2d")
    return full, res[n]


def _run_comm(comm, name):
    n_in, n_out = len(comm.ins), len(comm.outs)

    def body(*refs):
        send_sems, recv_sems = refs[n_in + n_out:]
        sends, recvs = comm.copies(refs[:n_in], refs[n_in:n_in + n_out], lambda k: (send_sems.at[k], recv_sems.at[k]))
        for cp in sends:
            cp.start()
        for cp in recvs:
            cp.wait_recv()
        for cp in sends:
            cp.wait_send()

    return list(pl.pallas_call(
        body, name=name, out_shape=comm.outs, in_specs=[ANY] * n_in, out_specs=[ANY] * n_out,
        input_output_aliases=comm.aliases, scratch_shapes=[pltpu.SemaphoreType.DMA((comm.n_sems,))] * 2,
    )(*comm.ins))


def _exchange_small(small, extra, name):
    n_in, n_out = len(extra.ins), len(extra.outs)

    def body(*refs):
        ex_in, small_in = refs[:n_in], refs[n_in]
        ex_out, small_out = refs[n_in + 1:n_in + 1 + n_out], refs[n_in + 1 + n_out]
        send_sems, recv_sems, local_sems = refs[n_in + n_out + 2:]
        sem = lambda k: (send_sems.at[k], recv_sems.at[k])
        x, y, c = _my_place()
        me = 4 * x + 2 * y + c
        local = pltpu.make_async_copy(small_in, small_out.at[me], local_sems.at[0])
        local.start()
        sends, recvs = extra.copies(ex_in, ex_out, sem)
        for rel in range(1, N_DEVICES):
            px, py, pc = x ^ ((rel >> 2) & 1), y ^ ((rel >> 1) & 1), c ^ (rel & 1)
            k = extra.n_sems + rel - 1
            sends.append(_remote(small_in, small_out.at[me], sem(k), (px, py, pc)))
            recvs.append(_remote(small_in, small_out.at[4 * px + 2 * py + pc], sem(k), (px, py, pc)))
        for cp in sends:
            cp.start()
        for cp in recvs:
            cp.wait_recv()
        for cp in sends:
            cp.wait_send()
        local.wait()

    n_sems = extra.n_sems + N_DEVICES - 1
    res = pl.pallas_call(
        body, name=name, out_shape=extra.outs + [jax.ShapeDtypeStruct((N_DEVICES,) + small.shape, small.dtype)],
        in_specs=[ANY] * (n_in + 1), out_specs=[ANY] * (n_out + 1), input_output_aliases=extra.aliases,
        scratch_shapes=[pltpu.SemaphoreType.DMA((n_sems,)), pltpu.SemaphoreType.DMA((n_sems,)),
                        pltpu.SemaphoreType.DMA((1,))],
    )(*extra.ins, small)
    return list(res[:n_out]), res[n_out]


def _remote(src, dst, sems, device):
    return pltpu.make_async_remote_copy(src_ref=src, dst_ref=dst, send_sem=sems[0], recv_sem=sems[1],
                                        device_id=device, device_id_type=pl.DeviceIdType.MESH)


def _sum_piece(dest, slots, grad, blk, axis, layer, n_layers, name):
    r, c = blk
    h0 = r // 2
    tr = _tile(h0, 256, SUBLANES_BF16)
    tc = _tile(c, 2048, LANES)
    place = jnp.stack([2 * lax.axis_index("x") + lax.axis_index("y"), lax.axis_index("c")]).astype(jnp.int32)

    def body(p_ref, s_ref, g_ref, *rest):
        acc = g_ref[...].astype(F32) + s_ref[0].astype(F32)
        for k in range(1, N_DEVICES - 1):
            acc = acc + s_ref[k].astype(F32)
        rest[-1][...] = acc

    def g_map(i, j, p):
        return (p[1] * (h0 // tr) + (p[0] * (r // tr) if axis == 0 else 0) + i, (p[0] * (c // tc) if axis == 1 else 0) + j)

    in_specs = [pl.BlockSpec((N_DEVICES - 1, tr, tc), lambda i, j, p: (0, i, j)), pl.BlockSpec((tr, tc), g_map)]
    args = [place, slots, grad]
    if dest is not None:
        in_specs.append(ANY)
        args.append(dest)
    return pl.pallas_call(
        body, name=name, out_shape=jax.ShapeDtypeStruct((n_layers, r, c), F32),
        grid_spec=pltpu.PrefetchScalarGridSpec(
            num_scalar_prefetch=1, grid=(h0 // tr, c // tc), in_specs=in_specs,
            out_specs=pl.BlockSpec((None, tr, tc), lambda i, j, p: (layer, p[1] * (h0 // tr) + i, j))),
        input_output_aliases={3: 0} if dest is not None else {},
        compiler_params=_params("parallel", "parallel"),
    )(*args)


def _swap_halves_task(blocks):
    n = len(blocks)
    layers = [(a, l) for a, b in enumerate(blocks) for l in range(b.shape[0])]

    def copies(in_refs, out_refs, sem):
        x, y, c = _my_place()
        sends, recvs = [], []
        for k, (a, l) in enumerate(layers):
            blk = blocks[a].shape[1:]
            mine = _half_of(out_refs[a].at[l], blk, c)
            sends.append(_remote(mine, mine, sem(k), (x, y, 1 - c)))
            recvs.append(_remote(mine, _half_of(out_refs[a].at[l], blk, 1 - c), sem(k), (x, y, 1 - c)))
        return sends, recvs

    return _Comm(blocks, [jax.ShapeDtypeStruct(b.shape, b.dtype) for b in blocks], {a: a for a in range(n)},
                 len(layers), copies)


def _shard_shape(shard):
    return shard[0].shape[1:] if isinstance(shard, tuple) else shard.shape


def _place_shard(shard, axis, name):
    r, c = _shard_shape(shard)
    tr = _tile(r, 512, SUBLANES_BF16)
    full = (r * N_SHARDS, c) if axis == 0 else (r, c * N_SHARDS)
    me2 = (2 * lax.axis_index("x") + lax.axis_index("y")).astype(jnp.int32).reshape(1)

    def body(me_ref, s_ref, o_ref):
        o_ref[...] = s_ref[...].astype(o_ref.dtype)

    if axis == 0:
        out_map = lambda i, me: (me[0] * (r // tr) + i, 0)
    else:
        out_map = lambda i, me: (i, me[0])
    if isinstance(shard, tuple):
        src, layer = shard
        in_spec = pl.BlockSpec((None, tr, c), lambda i, me: (layer, i, 0))
    else:
        src, in_spec = shard, pl.BlockSpec((tr, c), lambda i, me: (i, 0))
    return pl.pallas_call(
        body, name=name, out_shape=jax.ShapeDtypeStruct(full, BF16),
        grid_spec=pltpu.PrefetchScalarGridSpec(
            num_scalar_prefetch=1, grid=(r // tr,), in_specs=[in_spec], out_specs=pl.BlockSpec((tr, c), out_map)),
        compiler_params=_params("parallel"),
    )(me2, src)


def _gather_ici_task(shards, axes, names):
    n = len(shards)
    blks = [_shard_shape(s) for s in shards]
    bases = [_place_shard(s, ax, f"place_{nm}") for s, ax, nm in zip(shards, axes, names)]

    def copies(in_refs, out_refs, sem):
        x, y, c = _my_place()
        me = 2 * x + y
        sends, recvs = [], []
        for a in range(n):
            mine = _piece(out_refs[a], blks[a], axes[a], me, c)
            for k, (px, py) in enumerate(_other_chips(x, y)):
                sends.append(_remote(mine, mine, sem(3 * a + k), (px, py, c)))
                recvs.append(_remote(mine, _piece(out_refs[a], blks[a], axes[a], 2 * px + py, c), sem(3 * a + k),
                                     (px, py, c)))
        return sends, recvs

    return _Comm(bases, [jax.ShapeDtypeStruct(b.shape, b.dtype) for b in bases], {a: a for a in range(n)}, 3 * n, copies)


def _gather_d2d_task(partials, blks, axes):
    n = len(partials)

    def copies(in_refs, out_refs, sem):
        x, y, c = _my_place()
        sends, recvs = [], []
        for a in range(n):
            for k, (px, py) in enumerate(_other_chips(x, y)):
                mine = _piece(out_refs[a], blks[a], axes[a], 2 * px + py, c)
                theirs = _piece(out_refs[a], blks[a], axes[a], 2 * px + py, 1 - c)
                sends.append(_remote(mine, mine, sem(3 * a + k), (x, y, 1 - c)))
                recvs.append(_remote(mine, theirs, sem(3 * a + k), (x, y, 1 - c)))
        return sends, recvs

    return _Comm(partials, [jax.ShapeDtypeStruct(p.shape, p.dtype) for p in partials], {a: a for a in range(n)},
                 3 * n, copies)


def _scatter_task(grads, blks, axes):
    n = len(grads)

    def copies(in_refs, out_refs, sem):
        x, y, c = _my_place()
        sends, recvs = [], []
        for rel in range(1, N_DEVICES):
            px, py, pc = x ^ ((rel >> 2) & 1), y ^ ((rel >> 1) & 1), c ^ (rel & 1)
            for a in range(n):
                src = _piece(in_refs[a], blks[a], axes[a], 2 * px + py, pc)
                k = (N_DEVICES - 1) * a + rel - 1
                sends.append(_remote(src, out_refs[a].at[rel - 1], sem(k), (px, py, pc)))
                recvs.append(_remote(src, out_refs[a].at[rel - 1], sem(k), (px, py, pc)))
        return sends, recvs

    outs = [jax.ShapeDtypeStruct((N_DEVICES - 1, b[0] // 2) + tuple(b[1:]), g.dtype) for b, g in zip(blks, grads)]
    return _Comm(grads, outs, {}, (N_DEVICES - 1) * n, copies)


def _pack(arrays):
    flat = jnp.concatenate([a.reshape(-1).astype(F32) for a in arrays])
    pad = (-flat.shape[0]) % (SUBLANES_F32 * LANES)
    return jnp.pad(flat, (0, pad)).reshape(-1, LANES)


def _unpack(packed, shapes):
    flat = packed.reshape(-1)
    out, off = [], 0
    for s in shapes:
        n = int(np.prod(s))
        out.append(flat[off:off + n].reshape(s))
        off += n
    return out


def _local_step(x, target, W, shards, geom, small):
    W = dict(W)
    B, S, D = x.shape
    T = B * S
    x2 = x.reshape(T, D)
    tgt = target.reshape(T, D)
    bmap = jnp.asarray(_bucket_map())
    blk = lambda names: [geom[k][0] for k in names]
    axs = lambda names: [geom[k][1] for k in names]
    ici = lambda names: _gather_ici_task([shards[k] for k in names], axs(names), names)
    d2d = lambda names, partials: _gather_d2d_task(list(partials), blk(names), axs(names))
    big, slots = {}, {}

    def scatter(names):
        return _scatter_task([big[k] for k in names], blk(names), axs(names))

    def ffn_bwd(l, dout, h, saved, first, second):
        xn, u, v, a = saved
        da = _mm_nt(dout, W[f"w_down{l}"], BF16, f"ffn{l}_down_dx")
        big[f"w_down{l}"] = _mm_tn(a, dout, f"ffn{l}_down_dw")
        names = first + [f"w_down{l}"]
        du, g_cw, g_cb, landed = _ffn_gate_bwd(u, v, small["ffn_conv"][l], da, S, f"ffn{l}_gate_bwd", comm=scatter(names))
        slots.update(zip(names, landed))
        big[f"w_up{l}"] = _mm_tn(xn, du, f"ffn{l}_up_dw")
        norm = (h, small["ffn_norm"][l], dout)
        if second:
            (dh, g_norm), (slots[f"w_up{l}"],) = _mm_nt(du, W[f"w_up{l}"], None, f"ffn{l}_up_dx",
                                                        comm=scatter([f"w_up{l}"]), norm=norm)
        else:
            dh, g_norm = _mm_nt(du, W[f"w_up{l}"], None, f"ffn{l}_up_dx", norm=norm)
        return dh, g_cw, g_cb, g_norm[0]

    xn0 = _rmsnorm_fwd(x2, small["a_norm"][0], "a_norm")
    p, part = _mm_nn(xn0, W["w_in"], None, BF16, "a_in", comm=ici(["w_up0"]))
    z, (W["w_up0"],) = _gate_a_fwd(p, small["a_conv"][0], S, "a_gate", comm=d2d(["w_up0"], part))
    (h1, xn1), part = _mm_nn(z, W["w_out"], x2, F32, "a_out", comm=ici(["w_down0"]), norm_gains=[small["ffn_norm"][0]])
    attn, ffn1 = ["w_kv", "w_q", "w_o"], ["w_up1", "w_down1"]
    u0, landed = _mm_nn(xn1, W["w_up0"], None, BF16, "ffn0_up", comm=_Comm.join([d2d(["w_down0"], part), ici(attn)]))
    W["w_down0"] = landed[0]
    (a0, v0), landed = _ffn_gate_fwd(u0, small["ffn_conv"][0], small["ffn_conv_b"][0], S, "ffn0_gate",
                                     comm=_Comm.join([d2d(attn, landed[1:]), ici(ffn1)]))
    W.update(zip(attn, landed[:len(attn)]))
    (h2, kvn, xn3), landed = _mm_nn(a0, W["w_down0"], h1, F32, "ffn0_down", comm=d2d(ffn1, landed[len(attn):]),
                                    norm_gains=[small["kv_norm"], small["b_norm"][0]])
    W.update(zip(ffn1, landed))
    kv = _mm_nn(kvn, W["w_kv"], None, F32, "kv_proj")
    q = _mm_nn(xn3, W["w_q"], None, F32, "q_proj")
    bias = _bias_tables(small["rel_bias"], bmap, "rel_bias_tables")
    q3, kv3 = q.reshape(B, S, D), kv.reshape(B, S, 2 * D)
    o3, lse3 = _attn_fwd(q3, kv3, bias, "attn_fwd")
    o = o3.reshape(T, D)
    h3, xn4 = _mm_nn(o, W["w_o"], h2, F32, "o_proj", norm_gains=[small["ffn_norm"][1]])
    u1 = _mm_nn(xn4, W["w_up1"], None, BF16, "ffn1_up")
    a1, v1 = _ffn_gate_fwd(u1, small["ffn_conv"][1], small["ffn_conv_b"][1], S, "ffn1_gate")
    sq_err, dh4, g_final = _loss_head(a1, W["w_down1"], h3, small["final_norm"], tgt, "ffn1_down_loss")
    loss = 0.5 * jnp.sum(sq_err) / D

    dh3, g_cw1, g_cb1, g_fn1 = ffn_bwd(1, dh4, h3, (xn4, u1, v1, a1), [], False)
    do, delta = _mm_nt(dh3, W["w_o"], F32, "o_proj_dx", head_dot=o)
    big["w_o"] = _mm_tn(o, dh3, "o_proj_dw")
    (dq3, dk3, dv3, dbias), landed = _attn_bwd(q3, kv3, delta.reshape(B, S, D), lse3, do.reshape(B, S, D), bias, "attn_bwd",
                                               comm=scatter(["w_up1", "w_o"]))
    slots.update(zip(["w_up1", "w_o"], landed))
    g_rel = _bias_grad(dbias, bmap, "rel_bias_grad")[:, :REL_BUCKETS].T
    dq, dk, dv = dq3.reshape(T, D), dk3.reshape(T, D), dv3.reshape(T, D)
    dh2, g_bn = _mm_nt(dq, W["w_q"], None, "q_proj_dx", norm=(h2, small["b_norm"][0], dh3))
    big["w_q"] = _mm_tn(xn3, dq, "q_proj_dw")
    dh2, g_kvn = _mm_nt([dk, dv], W["w_kv"], None, "kv_proj_dx", norm=(h2, small["kv_norm"], dh2))
    big["w_kv"] = jnp.concatenate([_mm_tn(kvn, dk, "k_proj_dw"), _mm_tn(kvn, dv, "v_proj_dw")], axis=1)
    dh1, g_cw0, g_cb0, g_fn0 = ffn_bwd(0, dh2, h1, (xn1, u0, v0, a0), ["w_q", "w_kv"], True)
    dz = _mm_nt(dh1, W["w_out"], BF16, "a_out_dx")
    big["w_out"] = _mm_tn(z, dh1, "a_out_dw")
    dp, g_aconv, (slots["w_out"],) = _gate_a_bwd(p, small["a_conv"][0], dz, S, "a_gate_bwd", comm=scatter(["w_out"]))
    big["w_in"] = _mm_tn(xn0, dp, "a_in_dw")
    (dx, g_an), (slots["w_in"],) = _mm_nt(dp, W["w_in"], None, "a_in_dx", comm=scatter(["w_in"]),
                                          norm=(x2, small["a_norm"][0], dh1))
    g_bn, g_kvn, g_an = g_bn[0], g_kvn[0], g_an[0]

    small_g = {"a_norm": g_an[None], "a_conv": g_aconv[None], "kv_norm": g_kvn, "b_norm": g_bn[None],
               "rel_bias": g_rel, "ffn_norm": jnp.stack([g_fn0, g_fn1]), "ffn_conv": jnp.stack([g_cw0, g_cw1]),
               "ffn_conv_b": jnp.stack([g_cb0, g_cb1]), "final_norm": g_final}
    return loss, dx.reshape(B, S, D), big, slots, small_g


BIG = ("w_in", "w_out", "w_kv", "w_q", "w_o", "w_up0", "w_up1", "w_down0", "w_down1")
SMALL = ("a_norm", "a_conv", "kv_norm", "b_norm", "rel_bias", "ffn_norm", "ffn_conv", "ffn_conv_b", "final_norm")
SMALL_SHARDED = ("a_norm", "a_conv", "ffn_conv")
WEIGHT_ORDER = ("a_norm", "a_w_in", "a_conv", "a_w_out", "kv_norm", "w_kv", "b_norm", "b_w_q", "b_w_o", "rel_bias",
                "ffn_norm", "ffn_w_up", "ffn_conv", "ffn_conv_b", "ffn_w_down", "final_norm")
GRAD_OF = {"w_in": ("a_w_in", 0), "w_out": ("a_w_out", 0), "w_kv": ("w_kv", 0), "w_q": ("b_w_q", 0), "w_o": ("b_w_o", 0),
           "w_up0": ("ffn_w_up", 0), "w_up1": ("ffn_w_up", 1), "w_down0": ("ffn_w_down", 0), "w_down1": ("ffn_w_down", 1)}


def _as2d(a):
    return a.reshape(-1, a.shape[-1])


def kernel(x, a_norm, a_w_in, a_conv, a_w_out, kv_norm, w_kv, b_norm, b_w_q, b_w_o, rel_bias, ffn_norm, ffn_w_up, ffn_conv, ffn_conv_b, ffn_w_down, final_norm, loss_target, m_a_norm, m_a_w_in, m_a_conv, m_a_w_out, m_kv_norm, m_w_kv, m_b_norm, m_b_w_q, m_b_w_o, m_rel_bias, m_ffn_norm, m_ffn_w_up, m_ffn_conv, m_ffn_conv_b, m_ffn_w_down, m_final_norm, v_a_norm, v_a_w_in, v_a_conv, v_a_w_out, v_kv_norm, v_w_kv, v_b_norm, v_b_w_q, v_b_w_o, v_rel_bias, v_ffn_norm, v_ffn_w_up, v_ffn_conv, v_ffn_conv_b, v_ffn_w_down, v_final_norm):
    given = dict(a_norm=a_norm, a_w_in=a_w_in, a_conv=a_conv, a_w_out=a_w_out, kv_norm=kv_norm, w_kv=w_kv, b_norm=b_norm,
                 b_w_q=b_w_q, b_w_o=b_w_o, rel_bias=rel_bias, ffn_norm=ffn_norm, ffn_w_up=ffn_w_up, ffn_conv=ffn_conv,
                 ffn_conv_b=ffn_conv_b, ffn_w_down=ffn_w_down, final_norm=final_norm)
    mom_m = dict(a_norm=m_a_norm, a_w_in=m_a_w_in, a_conv=m_a_conv, a_w_out=m_a_w_out, kv_norm=m_kv_norm, w_kv=m_w_kv,
                 b_norm=m_b_norm, b_w_q=m_b_w_q, b_w_o=m_b_w_o, rel_bias=m_rel_bias, ffn_norm=m_ffn_norm,
                 ffn_w_up=m_ffn_w_up, ffn_conv=m_ffn_conv, ffn_conv_b=m_ffn_conv_b, ffn_w_down=m_ffn_w_down,
                 final_norm=m_final_norm)
    mom_v = dict(a_norm=v_a_norm, a_w_in=v_a_w_in, a_conv=v_a_conv, a_w_out=v_a_w_out, kv_norm=v_kv_norm, w_kv=v_w_kv,
                 b_norm=v_b_norm, b_w_q=v_b_w_q, b_w_o=v_b_w_o, rel_bias=v_rel_bias, ffn_norm=v_ffn_norm,
                 ffn_w_up=v_ffn_w_up, ffn_conv=v_ffn_conv, ffn_conv_b=v_ffn_conv_b, ffn_w_down=v_ffn_w_down,
                 final_norm=v_final_norm)

    shard = {"w_in": ((a_w_in, 0), 1), "w_out": ((a_w_out, 0), 0), "w_kv": (w_kv, 1), "w_q": ((b_w_q, 0), 0),
             "w_o": ((b_w_o, 0), 0), "w_up0": ((ffn_w_up, 0), 1), "w_up1": ((ffn_w_up, 1), 1),
             "w_down0": ((ffn_w_down, 0), 0), "w_down1": ((ffn_w_down, 1), 0)}

    small_sharded = [given[k] for k in SMALL_SHARDED]
    packed = _pack(small_sharded)
    first = ("w_in", "w_out")
    fulls, packed_all = _gather_weights([shard[k][0] for k in first], [shard[k][1] for k in first], first, packed)
    W = dict(zip(first, fulls))
    later = {k: shard[k][0] for k in BIG if k not in first}
    geom = {k: (_shard_shape(shard[k][0]), shard[k][1]) for k in BIG}
    small = {k: given[k] for k in SMALL}
    per_shard = [_unpack(packed_all[j], [a.shape for a in small_sharded]) for j in range(N_SHARDS)]
    for i, k in enumerate(SMALL_SHARDED):
        small[k] = jnp.concatenate([per_shard[j][i] for j in range(N_SHARDS)], axis=-1)

    loss, grad_x, big_g, slots, small_g = _local_step(x, loss_target, W, later, geom, small)

    small_shapes = [small_g[k].shape for k in SMALL] + [(1,)]
    layers_of = {}
    for k in BIG:
        layers_of.setdefault(GRAD_OF[k][0], []).append(k)
    blocks = {}
    for name, members in layers_of.items():
        dest = None
        for k in members:
            dest = _sum_piece(dest, slots[k], big_g[k], geom[k][0], geom[k][1], GRAD_OF[k][1], len(members), f"sum_{k}")
        blocks[name] = dest
    swapped, small_slots = _exchange_small(_pack([small_g[k] for k in SMALL] + [loss.reshape(1)]),
                                           _swap_halves_task(list(blocks.values())), "swap_halves_exchange_small")
    reduced = dict(zip(blocks, swapped))
    *small_sums, loss = _unpack(_sum_slots(small_slots, "sum_small"), small_shapes)
    loss = loss[0]
    small_red = dict(zip(SMALL, small_sums))
    j = 2 * lax.axis_index("x") + lax.axis_index("y")
    for k in SMALL_SHARDED:
        w = given[k].shape[-1]
        small_red[k] = lax.dynamic_slice_in_dim(small_red[k], j * w, w, axis=small_red[k].ndim - 1)

    grads, deltas, new_m, new_v = {}, {}, {}, {}
    for name in WEIGHT_ORDER:
        if name in reduced:
            g = reduced[name].reshape(given[name].shape)
            d, nm, nv = _adamw(_as2d(given[name]), _as2d(g), _as2d(mom_m[name]), _as2d(mom_v[name]), f"adamw_{name}")
            grads[name] = g
            deltas[name], new_m[name], new_v[name] = (t.reshape(given[name].shape) for t in (d, nm, nv))
    small_names = [n for n in WEIGHT_ORDER if n not in reduced]
    for n in small_names:
        grads[n] = small_red[n].reshape(given[n].shape)
    sw, sg, sm, sv = (_pack([d[n] for n in small_names]) for d in (given, grads, mom_m, mom_v))
    d, nm, nv = _adamw(sw, sg, sm, sv, "adamw_small")
    shapes = [given[n].shape for n in small_names]
    for n, a, b_, c_ in zip(small_names, _unpack(d, shapes), _unpack(nm, shapes), _unpack(nv, shapes)):
        deltas[n], new_m[n], new_v[n] = a, b_, c_

    return (loss, grad_x, *[grads[n] for n in WEIGHT_ORDER], *[deltas[n] for n in WEIGHT_ORDER],
            *[new_m[n] for n in WEIGHT_ORDER], *[new_v[n] for n in WEIGHT_ORDER])
```

```python
import functools
import math

import numpy as np

import jax
import jax.numpy as jnp
from jax import lax
from jax.experimental import pallas as pl
from jax.experimental.pallas import tpu as pltpu

F32 = jnp.float32
BF16 = jnp.bfloat16

RMS_EPS = 1e-6
HEAD_DIM = 64
ATT_BLOCK = 128
DILATED_BRANCHES = ((128, 1), (512, 4), (2048, 16))
REL_BUCKETS = 32
REL_MAX_DISTANCE = 2048
MASKED_LOGIT = -1e30
ATTN_FWD_UNROLL = (8, 8, 4)
ATTN_BWD_UNROLL = (8, 8, 8)

ADAM_LR = 0.001
ADAM_B1 = 0.9
ADAM_B2 = 0.999
ADAM_EPS = 1e-08
ADAM_WD = 0.01
ADAM_STEP = 10

LANES = 128
SUBLANES_F32 = 8
SUBLANES_BF16 = 16
VMEM_LIMIT_BYTES = 56 * 1024 * 1024

N_SHARDS = 4
N_DEVICES = 8
ANY = pl.BlockSpec(memory_space=pl.ANY)


def _tile(n, pref, mult):
    best = None
    for t in range(mult, min(n, pref) + 1, mult):
        if n % t == 0:
            best = t
    if best is None:
        raise ValueError(f"no tile for {n} (multiple of {mult}, at most {pref})")
    return best


def _params(*sem):
    return pltpu.CompilerParams(dimension_semantics=sem, vmem_limit_bytes=VMEM_LIMIT_BYTES)


class _Comm:
    def __init__(self, ins, outs, aliases, n_sems, copies):
        self.ins, self.outs, self.aliases, self.n_sems, self.copies = list(ins), list(outs), dict(aliases), n_sems, copies

    @staticmethod
    def join(parts):
        parts = [p for p in parts if p is not None]
        ins, outs, aliases, offs, n_sems = [], [], {}, [], 0
        for p in parts:
            offs.append((len(ins), len(outs), n_sems))
            aliases.update({len(ins) + i: len(outs) + o for i, o in p.aliases.items()})
            ins += p.ins
            outs += p.outs
            n_sems += p.n_sems

        def copies(in_refs, out_refs, sem):
            sends, recvs = [], []
            for p, (i0, o0, s0) in zip(parts, offs):
                s, r = p.copies(in_refs[i0:i0 + len(p.ins)], out_refs[o0:o0 + len(p.outs)],
                                lambda k, s0=s0: sem(s0 + k))
                sends += s
                recvs += r
            return sends, recvs

        return _Comm(ins, outs, aliases, n_sems, copies)


def _pallas(body, *, comm=None, name, out_shape, grid=(), in_specs=(), out_specs=(), scratch_shapes=(),
            compiler_params=None):
    if comm is None:
        return pl.pallas_call(body, name=name, out_shape=out_shape, grid=grid, in_specs=in_specs, out_specs=out_specs,
                              scratch_shapes=scratch_shapes, compiler_params=compiler_params)
    single = not isinstance(out_shape, (tuple, list))
    outs = (out_shape,) if single else tuple(out_shape)
    o_specs = (out_specs,) if single else tuple(out_specs)
    n_in, n_cin, n_out, n_cout, n_scr = len(in_specs), len(comm.ins), len(outs), len(comm.outs), len(scratch_shapes)

    def carried(*refs):
        base_in, c_in = refs[:n_in], refs[n_in:n_in + n_cin]
        o0 = n_in + n_cin
        base_out, c_out = refs[o0:o0 + n_out], refs[o0 + n_out:o0 + n_out + n_cout]
        s0 = o0 + n_out + n_cout
        base_scr, (send_sems, recv_sems) = refs[s0:s0 + n_scr], refs[s0 + n_scr:]
        sem = lambda k: (send_sems.at[k], recv_sems.at[k])
        first = functools.reduce(jnp.logical_and, [pl.program_id(a) == 0 for a in range(len(grid))])
        last = functools.reduce(jnp.logical_and, [pl.program_id(a) == n - 1 for a, n in enumerate(grid)])

        @pl.when(first)
        def _():
            for cp in comm.copies(c_in, c_out, sem)[0]:
                cp.start()

        body(*base_in, *base_out, *base_scr)

        @pl.when(last)
        def _():
            sends, recvs = comm.copies(c_in, c_out, sem)
            for cp in recvs:
                cp.wait_recv()
            for cp in sends:
                cp.wait_send()

    call = pl.pallas_call(
        carried, name=name, out_shape=outs + tuple(comm.outs), grid=grid,
        in_specs=list(in_specs) + [ANY] * n_cin, out_specs=o_specs + (ANY,) * n_cout,
        scratch_shapes=list(scratch_shapes) + [pltpu.SemaphoreType.DMA((comm.n_sems,))] * 2,
        input_output_aliases={n_in + i: n_out + o for i, o in comm.aliases.items()},
        compiler_params=_params(*(["arbitrary"] * len(grid))))

    def run(*args):
        res = call(*args, *comm.ins)
        base = res[0] if single else tuple(res[:n_out])
        return base, list(res[n_out:])

    return run


def _rmsnorm_fwd(x, g, name):
    T, D = x.shape
    tm = _tile(T, 512, SUBLANES_BF16)

    def body(x_ref, g_ref, o_ref):
        xf = x_ref[...]
        r = lax.rsqrt(jnp.mean(xf * xf, axis=-1, keepdims=True) + RMS_EPS)
        o_ref[...] = ((xf * r) * g_ref[...]).astype(o_ref.dtype)

    return pl.pallas_call(
        body, name=name, out_shape=jax.ShapeDtypeStruct((T, D), BF16), grid=(T // tm,),
        in_specs=[pl.BlockSpec((tm, D), lambda i: (i, 0)), pl.BlockSpec((1, D), lambda i: (0, 0))],
        out_specs=pl.BlockSpec((tm, D), lambda i: (i, 0)),
        compiler_params=_params("parallel"),
    )(x, g.reshape(1, D))


def _loss_head(a, b, res, g, target, name):
    T, K = a.shape
    D = b.shape[1]
    tm = _tile(T, 512, SUBLANES_BF16)

    def body(a_ref, b_ref, res_ref, g_ref, t_ref, dh_ref, acc_ref):
        @pl.when(pl.program_id(0) == 0)
        def _():
            acc_ref[...] = jnp.zeros_like(acc_ref)

        xf = jnp.dot(a_ref[...].astype(BF16), b_ref[...], preferred_element_type=F32) + res_ref[...]
        r = lax.rsqrt(jnp.mean(xf * xf, axis=-1, keepdims=True) + RMS_EPS)
        xhat = xf * r
        err = xhat * g_ref[...] - t_ref[...]
        dy = err * (1.0 / D)
        acc_ref[0:1, :] += jnp.sum(dy * xhat, axis=0, keepdims=True)
        acc_ref[1:2, :] += jnp.sum(err * err, axis=0, keepdims=True)
        t = dy * g_ref[...]
        dh_ref[...] = r * (t - xhat * jnp.mean(t * xhat, axis=-1, keepdims=True))

    row = pl.BlockSpec((tm, D), lambda i: (i, 0))
    dh, acc = pl.pallas_call(
        body, name=name,
        out_shape=(jax.ShapeDtypeStruct((T, D), F32), jax.ShapeDtypeStruct((SUBLANES_F32, D), F32)),
        grid=(T // tm,),
        in_specs=[pl.BlockSpec((tm, K), lambda i: (i, 0)), pl.BlockSpec((K, D), lambda i: (0, 0)), row,
                  pl.BlockSpec((1, D), lambda i: (0, 0)), row],
        out_specs=(row, pl.BlockSpec((SUBLANES_F32, D), lambda i: (0, 0))),
        compiler_params=_params("arbitrary"),
    )(a, b, res, g.reshape(1, D), target)
    return acc[1], dh, acc[0]


def _mm_nn(a, b, res, out_dtype, name, comm=None, norm_gains=()):
    T, K = a.shape
    N = b.shape[1]
    n_g = len(norm_gains)
    tm = _tile(T, 512 if n_g else 1024, SUBLANES_BF16)
    tn = _tile(N, 3072, LANES)
    assert not n_g or tn == N, "the fused rmsnorm needs whole rows in one tile"

    def body(a_ref, b_ref, *rest):
        ins, outs = rest[:len(rest) - 1 - n_g], rest[len(rest) - 1 - n_g:]
        acc = jnp.dot(a_ref[...].astype(BF16), b_ref[...], preferred_element_type=F32)
        if res is not None:
            acc = acc + ins[0][...]
        outs[0][...] = acc.astype(outs[0].dtype)
        if n_g:
            y = acc * lax.rsqrt(jnp.mean(acc * acc, axis=-1, keepdims=True) + RMS_EPS)
            for g_ref, xn_ref in zip(ins[len(ins) - n_g:], outs[1:]):
                xn_ref[...] = (y * g_ref[...]).astype(xn_ref.dtype)

    tile = pl.BlockSpec((tm, tn), lambda j, i: (i, j))
    in_specs = [pl.BlockSpec((tm, K), lambda j, i: (i, 0)), pl.BlockSpec((K, tn), lambda j, i: (0, j))]
    args = [a, b]
    if res is not None:
        in_specs.append(tile)
        args.append(res)
    for g in norm_gains:
        in_specs.append(pl.BlockSpec((1, N), lambda j, i: (0, 0)))
        args.append(g.reshape(1, N))
    out_shape = jax.ShapeDtypeStruct((T, N), out_dtype)
    if n_g:
        out_shape = (out_shape,) + (jax.ShapeDtypeStruct((T, N), BF16),) * n_g
    return _pallas(
        body, comm=comm, name=name, out_shape=out_shape, grid=(N // tn, T // tm),
        in_specs=in_specs, out_specs=(tile,) * (1 + n_g) if n_g else tile,
        compiler_params=_params("parallel", "parallel"),
    )(*args)


def _mm_nt(dy, b, out_dtype, name, comm=None, norm=None, head_dot=None):
    dys = list(dy) if isinstance(dy, (list, tuple)) else [dy]
    T, n_each = dys[0].shape
    K = b.shape[0]
    tm = _tile(T, 2048 if norm is None and head_dot is None else 1024, SUBLANES_BF16)
    tk = _tile(K, 1536, LANES)
    tn = _tile(n_each, 2816 if norm is None else 1536, LANES)
    per = n_each // tn
    n_steps = per * len(dys)
    assert norm is None or tk == K, "the fused rmsnorm backward needs whole rows in one tile"

    def body(*refs):
        dy_refs, b_ref, acc_ref = refs[:len(dys)], refs[len(dys)], refs[-1]
        i, n = pl.program_id(0), pl.program_id(2)

        @pl.when(n == 0)
        def _():
            acc_ref[...] = jnp.zeros_like(acc_ref)

        for p, dy_ref in enumerate(dy_refs):
            @pl.when(jnp.logical_and(n >= p * per, n < (p + 1) * per))
            def _(dy_ref=dy_ref):
                acc_ref[...] += lax.dot_general(dy_ref[...].astype(BF16), b_ref[...], (((1,), (1,)), ((), ())),
                                                preferred_element_type=F32)

        if norm is None and head_dot is None:
            @pl.when(n == n_steps - 1)
            def _():
                refs[-2][...] = acc_ref[...].astype(refs[-2].dtype)
        elif norm is None:
            o_ref, out_ref, dot_ref = refs[len(dys) + 1:-1]

            @pl.when(n == n_steps - 1)
            def _():
                d = acc_ref[...]
                out_ref[...] = d.astype(out_ref.dtype)
                head0 = lax.broadcasted_iota(jnp.int32, (tm, LANES), 1) < HEAD_DIM
                for c0 in range(0, tk, LANES):
                    x = d[:, c0:c0 + LANES] * o_ref[:, c0:c0 + LANES]
                    d0 = jnp.sum(jnp.where(head0, x, 0.0), axis=-1, keepdims=True)
                    d1 = jnp.sum(jnp.where(head0, 0.0, x), axis=-1, keepdims=True)
                    dot_ref[:, c0:c0 + LANES] = jnp.where(head0, d0, d1)
        else:
            x_ref, g_ref, dres_ref, dx_ref, dg_ref = refs[len(dys) + 1:-1]

            @pl.when(jnp.logical_and(i == 0, n == 0))
            def _():
                dg_ref[...] = jnp.zeros_like(dg_ref)

            @pl.when(n == n_steps - 1)
            def _():
                xf = x_ref[...]
                r = lax.rsqrt(jnp.mean(xf * xf, axis=-1, keepdims=True) + RMS_EPS)
                xhat = xf * r
                d = acc_ref[...]
                dg_ref[0:1, :] += jnp.sum(d * xhat, axis=0, keepdims=True)
                t = d * g_ref[...]
                dx_ref[...] = dres_ref[...] + r * (t - xhat * jnp.mean(t * xhat, axis=-1, keepdims=True))

    in_specs = [pl.BlockSpec((tm, tn), lambda i, k, n, p=p: (i, jnp.clip(n - p * per, 0, per - 1))) for p in range(len(dys))]
    in_specs.append(pl.BlockSpec((tk, tn), lambda i, k, n: (k, n)))
    args = dys + [b]
    tile = pl.BlockSpec((tm, tk), lambda i, k, n: (i, k))
    if norm is None and head_dot is None:
        out_shape, out_specs = jax.ShapeDtypeStruct((T, K), out_dtype), tile
    elif norm is None:
        in_specs.append(tile)
        args.append(head_dot)
        out_shape = (jax.ShapeDtypeStruct((T, K), out_dtype), jax.ShapeDtypeStruct((T, K), F32))
        out_specs = (tile, tile)
    else:
        x, g, dres = norm
        in_specs += [tile, pl.BlockSpec((1, K), lambda i, k, n: (0, 0)), tile]
        args += [x, g.reshape(1, K), dres]
        out_shape = (jax.ShapeDtypeStruct((T, K), F32), jax.ShapeDtypeStruct((SUBLANES_F32, K), F32))
        out_specs = (tile, pl.BlockSpec((SUBLANES_F32, K), lambda i, k, n: (0, 0)))
    return _pallas(
        body, comm=comm, name=name, out_shape=out_shape, grid=(T // tm, K // tk, n_steps),
        in_specs=in_specs, out_specs=out_specs, scratch_shapes=[pltpu.VMEM((tm, tk), F32)],
        compiler_params=_params("parallel", "parallel", "arbitrary") if norm is None else _params(*["arbitrary"] * 3),
    )(*args)


def _mm_tn(a, dy, name):
    T, K = a.shape
    N = dy.shape[1]
    tt = _tile(T, 2048, SUBLANES_BF16)
    tk = _tile(K, 1536, LANES)
    tn = _tile(N, 1536, LANES)
    t_steps = T // tt

    def body(a_ref, dy_ref, o_ref, acc_ref):
        t = pl.program_id(2)

        @pl.when(t == 0)
        def _():
            acc_ref[...] = jnp.zeros_like(acc_ref)

        acc_ref[...] += lax.dot_general(a_ref[...].astype(BF16), dy_ref[...].astype(BF16),
                                        (((0,), (0,)), ((), ())), preferred_element_type=F32)

        @pl.when(t == t_steps - 1)
        def _():
            o_ref[...] = acc_ref[...].astype(o_ref.dtype)

    return pl.pallas_call(
        body, name=name, out_shape=jax.ShapeDtypeStruct((K, N), BF16), grid=(K // tk, N // tn, t_steps),
        in_specs=[pl.BlockSpec((tt, tk), lambda k, n, t: (t, k)), pl.BlockSpec((tt, tn), lambda k, n, t: (t, n))],
        out_specs=pl.BlockSpec((tk, tn), lambda k, n, t: (k, n)),
        scratch_shapes=[pltpu.VMEM((tk, tn), F32)],
        compiler_params=_params("parallel", "parallel", "arbitrary"),
    )(a, dy)


def _rows_before(halo, cur, k):
    h = halo.shape[0]
    return pltpu.roll(jnp.concatenate([halo, cur], axis=0), k, 0)[h:]


def _rows_after(cur, halo, k):
    n = cur.shape[0]
    total = n + halo.shape[0]
    return pltpu.roll(jnp.concatenate([cur, halo], axis=0), total - k, 0)[:n]


def _halo_specs(tm, width, n_rows):
    per = tm // SUBLANES_BF16
    last = n_rows // SUBLANES_BF16 - 1
    prev = pl.BlockSpec((SUBLANES_BF16, width), lambda i: (jnp.maximum(i * per - 1, 0), 0))
    nxt = pl.BlockSpec((SUBLANES_BF16, width), lambda i: (jnp.minimum((i + 1) * per, last), 0))
    return prev, nxt


def _gate_a_fwd(p, cw, seq, name, comm=None):
    T, D3 = p.shape
    D = D3 // 3
    tm = _tile(seq, 512, SUBLANES_BF16)
    cc = _tile(D, 256, LANES)
    prev, _ = _halo_specs(tm, D3, T)

    def body(p_ref, ph_ref, cw_ref, z_ref):
        at_start = (pl.program_id(0) * tm) % seq == 0
        for c0 in range(0, D, cc):
            b = p_ref[:, c0:c0 + cc].astype(F32)
            u = p_ref[:, D + c0:D + c0 + cc].astype(F32) * p_ref[:, 2 * D + c0:2 * D + c0 + cc].astype(F32)
            uh = ph_ref[:, D + c0:D + c0 + cc].astype(F32) * ph_ref[:, 2 * D + c0:2 * D + c0 + cc].astype(F32)
            uh = jnp.where(at_start, 0.0, uh)
            w = cw_ref[:, c0:c0 + cc]
            cv = _rows_before(uh, u, 2) * w[0:1] + _rows_before(uh, u, 1) * w[1:2] + u * w[2:3]
            z_ref[:, c0:c0 + cc] = (b * cv).astype(z_ref.dtype)

    return _pallas(
        body, comm=comm, name=name, out_shape=jax.ShapeDtypeStruct((T, D), BF16), grid=(T // tm,),
        in_specs=[pl.BlockSpec((tm, D3), lambda i: (i, 0)), prev, pl.BlockSpec((3, D), lambda i: (0, 0))],
        out_specs=pl.BlockSpec((tm, D), lambda i: (i, 0)),
        compiler_params=_params("parallel"),
    )(p, p, cw)


def _gate_a_bwd(p, cw, dz, seq, name, comm=None):
    T, D3 = p.shape
    D = D3 // 3
    tm = _tile(seq, 512, SUBLANES_BF16)
    cc = _tile(D, 256, LANES)
    p_prev, p_next = _halo_specs(tm, D3, T)
    _, dz_next = _halo_specs(tm, D, T)

    def body(p_ref, pp_ref, pn_ref, dz_ref, dzn_ref, cw_ref, dp_ref, dcw_ref):
        i = pl.program_id(0)

        @pl.when(i == 0)
        def _():
            dcw_ref[...] = jnp.zeros_like(dcw_ref)

        at_start = (i * tm) % seq == 0
        at_end = ((i + 1) * tm) % seq == 0
        for c0 in range(0, D, cc):
            cb, cc_, ch = slice(c0, c0 + cc), slice(D + c0, D + c0 + cc), slice(2 * D + c0, 2 * D + c0 + cc)
            b = p_ref[:, cb].astype(F32)
            c = p_ref[:, cc_].astype(F32)
            hh = p_ref[:, ch].astype(F32)
            u = c * hh
            uh = jnp.where(at_start, 0.0, pp_ref[:, cc_].astype(F32) * pp_ref[:, ch].astype(F32))
            w = cw_ref[:, cb]
            u1 = _rows_before(uh, u, 1)
            u2 = _rows_before(uh, u, 2)
            cv = u2 * w[0:1] + u1 * w[1:2] + u * w[2:3]
            dz_t = dz_ref[:, cb].astype(F32)
            dcv = dz_t * b
            dcvn = jnp.where(at_end, 0.0, dzn_ref[:, cb].astype(F32) * pn_ref[:, cb].astype(F32))
            du = dcv * w[2:3] + _rows_after(dcv, dcvn, 1) * w[1:2] + _rows_after(dcv, dcvn, 2) * w[0:1]
            dp_ref[:, cb] = (dz_t * cv).astype(dp_ref.dtype)
            dp_ref[:, cc_] = (du * hh).astype(dp_ref.dtype)
            dp_ref[:, ch] = (du * c).astype(dp_ref.dtype)
            dcw_ref[0:1, cb] += jnp.sum(dcv * u2, axis=0, keepdims=True)
            dcw_ref[1:2, cb] += jnp.sum(dcv * u1, axis=0, keepdims=True)
            dcw_ref[2:3, cb] += jnp.sum(dcv * u, axis=0, keepdims=True)

    res = _pallas(
        body, comm=comm, name=name,
        out_shape=(jax.ShapeDtypeStruct((T, D3), BF16), jax.ShapeDtypeStruct((SUBLANES_F32, D), F32)),
        grid=(T // tm,),
        in_specs=[pl.BlockSpec((tm, D3), lambda i: (i, 0)), p_prev, p_next,
                  pl.BlockSpec((tm, D), lambda i: (i, 0)), dz_next, pl.BlockSpec((3, D), lambda i: (0, 0))],
        out_specs=(pl.BlockSpec((tm, D3), lambda i: (i, 0)), pl.BlockSpec((SUBLANES_F32, D), lambda i: (0, 0))),
        compiler_params=_params("arbitrary"),
    )(p, p, p, dz, dz, cw)
    if comm is None:
        return res[0], res[1][0:3]
    return res[0][0], res[0][1][0:3], res[1]


def _ffn_gate_fwd(u, cw, cb, seq, name, comm=None):
    T, F2 = u.shape
    F = F2 // 2
    tm = _tile(seq, 512, SUBLANES_BF16)
    cc = _tile(F, 256, LANES)
    prev, _ = _halo_specs(tm, F2, T)

    def body(u_ref, uh_ref, cw_ref, cb_ref, a_ref, v_ref):
        at_start = (pl.program_id(0) * tm) % seq == 0

        def conv(c0):
            cols = slice(c0, c0 + cc)
            cur = u_ref[:, cols].astype(F32)
            halo = jnp.where(at_start, 0.0, uh_ref[:, cols].astype(F32))
            w = cw_ref[:, cols]
            return (_rows_before(halo, cur, 2) * w[0:1] + _rows_before(halo, cur, 1) * w[1:2] + cur * w[2:3]
                    + cb_ref[:, cols])

        for c0 in range(0, F, cc):
            g = conv(c0)
            up = conv(F + c0)
            a_ref[:, c0:c0 + cc] = ((g * jax.nn.sigmoid(g)) * up).astype(a_ref.dtype)
            v_ref[:, c0:c0 + cc] = g.astype(v_ref.dtype)
            v_ref[:, F + c0:F + c0 + cc] = up.astype(v_ref.dtype)

    return _pallas(
        body, comm=comm, name=name,
        out_shape=(jax.ShapeDtypeStruct((T, F), BF16), jax.ShapeDtypeStruct((T, F2), BF16)), grid=(T // tm,),
        in_specs=[pl.BlockSpec((tm, F2), lambda i: (i, 0)), prev,
                  pl.BlockSpec((3, F2), lambda i: (0, 0)), pl.BlockSpec((1, F2), lambda i: (0, 0))],
        out_specs=(pl.BlockSpec((tm, F), lambda i: (i, 0)), pl.BlockSpec((tm, F2), lambda i: (i, 0))),
        compiler_params=_params("parallel"),
    )(u, u, cw, cb.reshape(1, F2))


def _ffn_gate_bwd(u, v, cw, da, seq, name, comm=None):
    T, F2 = u.shape
    F = F2 // 2
    H = SUBLANES_BF16
    tm = _tile(seq, 512, H)
    cc = _tile(F, 256, LANES)
    _, v_next = _halo_specs(tm, F2, T)
    _, da_next = _halo_specs(tm, F, T)

    def body(u_ref, v_ref, vn_ref, da_ref, dan_ref, cw_ref, du_ref, acc_ref):
        i = pl.program_id(0)

        @pl.when(i == 0)
        def _():
            acc_ref[...] = jnp.zeros_like(acc_ref)

        at_end = ((i + 1) * tm) % seq == 0
        n = tm + H

        def rows_and_next(ref, nxt, cols):
            return jnp.concatenate([ref[:, cols].astype(F32), nxt[:, cols].astype(F32)], axis=0)

        def back(d, cols):
            w = cw_ref[:, cols]
            d0 = d[:tm]
            d1 = pltpu.roll(d, n - 1, 0)[:tm]
            d2 = pltpu.roll(d, n - 2, 0)[:tm]
            du_ref[:, cols] = (d0 * w[2:3] + d1 * w[1:2] + d2 * w[0:1]).astype(du_ref.dtype)
            ut = u_ref[:, cols].astype(F32)
            acc_ref[0:1, cols] += jnp.sum(d2 * ut, axis=0, keepdims=True)
            acc_ref[1:2, cols] += jnp.sum(d1 * ut, axis=0, keepdims=True)
            acc_ref[2:3, cols] += jnp.sum(d0 * ut, axis=0, keepdims=True)
            acc_ref[3:4, cols] += jnp.sum(d0, axis=0, keepdims=True)

        for c0 in range(0, F, cc):
            gc, uc = slice(c0, c0 + cc), slice(F + c0, F + c0 + cc)
            g = rows_and_next(v_ref, vn_ref, gc)
            up = rows_and_next(v_ref, vn_ref, uc)
            da_ext = jnp.concatenate([da_ref[:, gc].astype(F32),
                                      jnp.where(at_end, 0.0, dan_ref[:, gc].astype(F32))], axis=0)
            sg = jax.nn.sigmoid(g)
            back(da_ext * up * (sg * (1.0 + g * (1.0 - sg))), gc)
            back(da_ext * (g * sg), uc)

    res = _pallas(
        body, comm=comm, name=name,
        out_shape=(jax.ShapeDtypeStruct((T, F2), BF16), jax.ShapeDtypeStruct((SUBLANES_F32, F2), F32)),
        grid=(T // tm,),
        in_specs=[pl.BlockSpec((tm, F2), lambda i: (i, 0)), pl.BlockSpec((tm, F2), lambda i: (i, 0)), v_next,
                  pl.BlockSpec((tm, F), lambda i: (i, 0)), da_next, pl.BlockSpec((3, F2), lambda i: (0, 0))],
        out_specs=(pl.BlockSpec((tm, F2), lambda i: (i, 0)), pl.BlockSpec((SUBLANES_F32, F2), lambda i: (0, 0))),
        compiler_params=_params("arbitrary"),
    )(u, v, v, da, da, cw)
    (du, acc), landed = res if comm is not None else (res, None)
    return (du, acc[0:3], acc[3]) if comm is None else (du, acc[0:3], acc[3], landed)


def _bucket_map():
    P = ATT_BLOCK
    qi = np.arange(P, dtype=np.int64)[:, None]
    kc = np.arange(2 * P, dtype=np.int64)[None, :]
    delta = qi + P - kc
    maps = []
    max_exact = REL_BUCKETS // 2
    for window, dilation in DILATED_BRANCHES:
        band = (delta >= 0) & (delta <= window // dilation)
        n = np.maximum(delta * dilation, 0)
        nf = np.maximum(n, max_exact).astype(np.float32)
        large = max_exact + (np.log(nf / np.float32(max_exact)) / np.float32(math.log(REL_MAX_DISTANCE / max_exact))
                             * np.float32(REL_BUCKETS - max_exact)).astype(np.int32)
        large = np.minimum(large, REL_BUCKETS - 1)
        bucket = np.where(n < max_exact, n, large)
        maps.append(np.where(band, bucket, -1).astype(np.int32))
    return np.stack(maps)


def _bias_tables(rel_bias, bmap, name):
    n_pairs = rel_bias.shape[1] // 2
    nbr, P, P2 = bmap.shape

    def body(rb_ref, bm_ref, o_ref):
        pair = pl.program_id(0)
        in_seq = lax.broadcasted_iota(jnp.int32, (P, P2), 1) >= P
        for br in range(nbr):
            bm = bm_ref[br]
            for hh in range(2):
                acc = jnp.full((P, P2), MASKED_LOGIT, F32)
                for b in range(REL_BUCKETS):
                    acc = jnp.where(bm == b, rb_ref[b, 2 * pair + hh], acc)
                o_ref[br, 0, 0, hh * P:(hh + 1) * P, :] = acc
                o_ref[br, 0, 1, hh * P:(hh + 1) * P, :] = jnp.where(in_seq, acc, MASKED_LOGIT)

    return pl.pallas_call(
        body, name=name, out_shape=jax.ShapeDtypeStruct((nbr, n_pairs, 2, 2 * P, P2), F32), grid=(n_pairs,),
        in_specs=[pl.BlockSpec(memory_space=pltpu.SMEM), pl.BlockSpec((nbr, P, P2), lambda h: (0, 0, 0))],
        out_specs=pl.BlockSpec((nbr, 1, 2, 2 * P, P2), lambda h: (0, h, 0, 0, 0)),
        compiler_params=_params("parallel"),
    )(rel_bias, bmap)


def _bias_grad(dbias, bmap, name):
    nbr, n_pairs, _, P2 = dbias.shape
    P = P2 // 2

    def body(db_ref, bm_ref, o_ref):
        lane = lax.broadcasted_iota(jnp.int32, (1, LANES), 1)
        for hh in range(2):
            row = jnp.zeros((1, LANES), F32)
            for br in range(nbr):
                bm = bm_ref[br]
                d = db_ref[br, 0, hh * P:(hh + 1) * P, :]
                for b in range(REL_BUCKETS):
                    hit = jnp.sum(jnp.where(bm == b, d, 0.0), axis=1, keepdims=True)
                    row = row + jnp.where(lane == b, jnp.sum(hit, axis=0, keepdims=True), 0.0)
            o_ref[hh] = row

    return pl.pallas_call(
        body, name=name, out_shape=jax.ShapeDtypeStruct((2 * n_pairs, 1, LANES), F32), grid=(n_pairs,),
        in_specs=[pl.BlockSpec((nbr, 1, P2, P2), lambda h: (0, h, 0, 0)), pl.BlockSpec((nbr, P, P2), lambda h: (0, 0, 0))],
        out_specs=pl.BlockSpec((2, 1, LANES), lambda h: (h, 0, 0)),
        compiler_params=_params("parallel"),
    )(dbias, bmap)[:, 0, :]


def _rows(start, dilation):
    if dilation == 1:
        return pl.ds(pl.multiple_of(start, ATT_BLOCK), ATT_BLOCK)
    return pl.ds(start, ATT_BLOCK, stride=dilation)


def _for_each_block(seq, unroll, fn):
    P = ATT_BLOCK
    for br, (_, d) in enumerate(DILATED_BRANCHES):
        nb = seq // d // P
        u = unroll[br] if (unroll[br] % nb == 0 or nb % unroll[br] == 0) else 1
        step = d * P

        def some(i, carry, br=br, d=d, nb=nb, u=u, step=step):
            blocks = []
            if u % nb == 0:
                for k in range(u):
                    if k % nb == 0:
                        start = i * (u // nb) + k // nb
                        blocks.append((start, start, 1))
                    else:
                        blocks.append((blocks[-1][0] + step, blocks[-1][0], 0))
            else:
                r, j0 = (i * u) // nb, (i * u) % nb
                blocks.append((r + j0 * step, r + jnp.maximum(j0 - 1, 0) * step, jnp.where(j0 == 0, 1, 0)))
                for _ in range(1, u):
                    blocks.append((blocks[-1][0] + step, blocks[-1][0], 0))
            fn(br, d, blocks)
            return carry

        lax.fori_loop(0, nb * d // u, some, 0)


class _RowCache:
    def __init__(self, dilation):
        self.dilation, self.seen = dilation, {}

    def rows(self, ref, start):
        key = (id(ref), id(start))
        if key not in self.seen:
            self.seen[key] = ref[_rows(start, self.dilation), :].astype(BF16)
        return self.seen[key]

    def window(self, ref, start, prev):
        return jnp.concatenate([self.rows(ref, prev), self.rows(ref, start)], axis=0)


def _stack_heads(x, head0):
    return jnp.concatenate([jnp.where(head0, x, 0.0), jnp.where(head0, 0.0, x)], axis=0).astype(BF16)


def _attn_fwd(q, kv, bias, name):
    B, S, D = q.shape
    P = ATT_BLOCK
    n_pairs = D // LANES
    nbr = len(DILATED_BRANCHES)
    scale = HEAD_DIM ** -0.5

    def body(q_ref, k_ref, v_ref, bias_ref, o_ref, lse_ref, *stats):
        m_s, l_s, acc_s = stats[0:nbr], stats[nbr:2 * nbr], stats[2 * nbr:3 * nbr]
        head0 = lax.broadcasted_iota(jnp.int32, (P, LANES), 1) < HEAD_DIM

        def block(br, d, blocks):
            cache = _RowCache(d)
            s = [lax.dot_general(_stack_heads(q_ref[_rows(start, d), :] * scale, head0),
                                 cache.window(k_ref, start, prev), (((1,), (1,)), ((), ())),
                                 preferred_element_type=F32) + bias_ref[br, 0, first]
                 for start, prev, first in blocks]
            m = [jnp.max(x, axis=-1, keepdims=True) for x in s]
            p = [jnp.exp(x - y) for x, y in zip(s, m)]
            l = [jnp.sum(x, axis=-1, keepdims=True) for x in p]
            pv = [jnp.dot(x.astype(BF16), cache.window(v_ref, start, prev), preferred_element_type=F32)
                  for x, (start, prev, _) in zip(p, blocks)]
            for k, (start, _, _) in enumerate(blocks):
                rows = _rows(start, d)
                m_s[br][rows, :] = jnp.where(head0, m[k][:P], m[k][P:])
                l_s[br][rows, :] = jnp.where(head0, l[k][:P], l[k][P:])
                acc_s[br][rows, :] = jnp.where(head0, pv[k][:P], pv[k][P:])

        _for_each_block(S, ATTN_FWD_UNROLL, block)

        chunk = _tile(S, 256, SUBLANES_F32)

        def merge(i, carry):
            rows = pl.ds(pl.multiple_of(i * chunk, chunk), chunk)
            ms = [m_s[br][rows, :] for br in range(nbr)]
            m = functools.reduce(jnp.maximum, ms)
            l = jnp.zeros((chunk, LANES), F32)
            acc = jnp.zeros((chunk, LANES), F32)
            for br in range(nbr):
                w = jnp.exp(ms[br] - m)
                l = l + w * l_s[br][rows, :]
                acc = acc + w * acc_s[br][rows, :]
            o_ref[rows, :] = acc / l
            lse_ref[rows, :] = m + jnp.log(l)
            return carry

        lax.fori_loop(0, S // chunk, merge, 0)

    slab = lambda col0: pl.BlockSpec((None, S, LANES), lambda b, h: (b, 0, col0 + h))
    return pl.pallas_call(
        body, name=name,
        out_shape=(jax.ShapeDtypeStruct((B, S, D), F32), jax.ShapeDtypeStruct((B, S, D), F32)),
        grid=(B, n_pairs),
        in_specs=[slab(0), slab(0), slab(n_pairs),
                  pl.BlockSpec((nbr, 1, 2, 2 * P, 2 * P), lambda b, h: (0, h, 0, 0, 0))],
        out_specs=(slab(0), slab(0)),
        scratch_shapes=[pltpu.VMEM((S, LANES), F32)] * (3 * nbr),
        compiler_params=_params("parallel", "parallel"),
    )(q, kv, kv, bias)


def _attn_bwd(q, kv, delta, lse, do, bias, name, comm=None):
    B, S, D = q.shape
    P = ATT_BLOCK
    n_pairs = D // LANES
    nbr = len(DILATED_BRANCHES)
    scale = HEAD_DIM ** -0.5

    def body(q_ref, k_ref, v_ref, delta_s, lse_ref, do_ref, bias_ref, dq_ref, dk_ref, dv_ref, dbias_ref):
        head0 = lax.broadcasted_iota(jnp.int32, (P, LANES), 1) < HEAD_DIM

        @pl.when(pl.program_id(1) == 0)
        def _():
            dbias_ref[...] = jnp.zeros_like(dbias_ref)

        chunk = _tile(S, 512, SUBLANES_F32)

        def prepare(i, carry):
            rows = pl.ds(pl.multiple_of(i * chunk, chunk), chunk)
            zero = jnp.zeros((chunk, LANES), F32)
            dq_ref[rows, :] = zero
            dk_ref[rows, :] = zero
            dv_ref[rows, :] = zero
            return carry

        lax.fori_loop(0, S // chunk, prepare, 0)

        def per_head(x):
            return jnp.concatenate([x[:, 0:1], x[:, HEAD_DIM:HEAD_DIM + 1]], axis=0)

        nt = (((1,), (1,)), ((), ()))
        tn = (((0,), (0,)), ((), ()))

        def block(br, d, blocks):
            cache = _RowCache(d)
            q2 = [_stack_heads(q_ref[_rows(start, d), :] * scale, head0) for start, _, _ in blocks]
            do2 = [_stack_heads(do_ref[_rows(start, d), :], head0) for start, _, _ in blocks]
            kb = [cache.window(k_ref, start, prev) for start, prev, _ in blocks]
            vb = [cache.window(v_ref, start, prev) for start, prev, _ in blocks]
            s = [lax.dot_general(a, b, nt, preferred_element_type=F32) + bias_ref[br, 0, first]
                 for a, b, (_, _, first) in zip(q2, kb, blocks)]
            dp = [lax.dot_general(a, b, nt, preferred_element_type=F32) for a, b in zip(do2, vb)]
            p = [jnp.exp(x - per_head(lse_ref[_rows(start, d), :])) for x, (start, _, _) in zip(s, blocks)]
            ds = [x * (y - per_head(delta_s[_rows(start, d), :])) for x, y, (start, _, _) in zip(p, dp, blocks)]
            for x in ds:
                dbias_ref[br, 0] += x
            ds16 = [x.astype(BF16) for x in ds]
            dq2 = [jnp.dot(a, b, preferred_element_type=F32) for a, b in zip(ds16, kb)]
            dk = [lax.dot_general(a, b, tn, preferred_element_type=F32) for a, b in zip(ds16, q2)]
            dv = [lax.dot_general(a.astype(BF16), b, tn, preferred_element_type=F32) for a, b in zip(p, do2)]
            parts = {}
            for k, (start, prev, first) in enumerate(blocks):
                dq_ref[_rows(start, d), :] += jnp.where(head0, dq2[k][:P], dq2[k][P:]) * scale
                parts.setdefault(id(start), [start, []])[1].append((dk[k][P:], dv[k][P:]))
                if not (isinstance(first, int) and first == 1):
                    parts.setdefault(id(prev), [prev, []])[1].append((dk[k][:P], dv[k][:P]))
            for start, terms in parts.values():
                rows = _rows(start, d)
                dk_ref[rows, :] += functools.reduce(jnp.add, [t[0] for t in terms])
                dv_ref[rows, :] += functools.reduce(jnp.add, [t[1] for t in terms])

        _for_each_block(S, ATTN_BWD_UNROLL, block)

    slab = lambda col0: pl.BlockSpec((None, S, LANES), lambda h, b: (b, 0, col0 + h))
    tab = pl.BlockSpec((nbr, 1, 2, 2 * P, 2 * P), lambda h, b: (0, h, 0, 0, 0))
    dtab = pl.BlockSpec((nbr, 1, 2 * P, 2 * P), lambda h, b: (0, h, 0, 0))
    shp = jax.ShapeDtypeStruct((B, S, D), F32)
    return _pallas(
        body, comm=comm, name=name,
        out_shape=(shp, shp, shp, jax.ShapeDtypeStruct((nbr, n_pairs, 2 * P, 2 * P), F32)),
        grid=(n_pairs, B),
        in_specs=[slab(0), slab(0), slab(n_pairs), slab(0), slab(0), slab(0), tab],
        out_specs=(slab(0), slab(0), slab(0), dtab),
        compiler_params=_params("parallel", "arbitrary"),
    )(q, kv, kv, delta, lse, do, bias)


def _adamw(w, g, m, v, name):
    R, C = w.shape
    tr = _tile(R, 256, SUBLANES_F32) if R % SUBLANES_F32 == 0 else R
    tc = _tile(C, 2048, LANES) if C % LANES == 0 else C

    def body(w_ref, g_ref, m_ref, v_ref, d_ref, nm_ref, nv_ref):
        g_ = g_ref[...]
        m2 = ADAM_B1 * m_ref[...] + (1.0 - ADAM_B1) * g_
        v2 = ADAM_B2 * v_ref[...] + (1.0 - ADAM_B2) * (g_ * g_)
        m_hat = m2 / (1.0 - ADAM_B1 ** ADAM_STEP)
        v_hat = v2 / (1.0 - ADAM_B2 ** ADAM_STEP)
        d_ref[...] = -ADAM_LR * (m_hat / (jnp.sqrt(v_hat) + ADAM_EPS) + ADAM_WD * w_ref[...])
        nm_ref[...] = m2
        nv_ref[...] = v2

    blk = pl.BlockSpec((tr, tc), lambda i, j: (i, j))
    shp = jax.ShapeDtypeStruct((R, C), F32)
    return pl.pallas_call(
        body, name=name, out_shape=(shp, shp, shp), grid=(R // tr, C // tc),
        in_specs=[blk] * 4, out_specs=(blk,) * 3, compiler_params=_params("parallel", "parallel"),
    )(w, g, m, v)


def _sum_slots(slots, name):
    n, R, C = slots.shape
    tr = _tile(R, 256, SUBLANES_BF16) if R % SUBLANES_BF16 == 0 else R
    tc = _tile(C, 2048, LANES) if C % LANES == 0 else C

    def body(s_ref, o_ref):
        acc = s_ref[0].astype(F32)
        for k in range(1, n):
            acc = acc + s_ref[k].astype(F32)
        o_ref[...] = acc

    return pl.pallas_call(
        body, name=name, out_shape=jax.ShapeDtypeStruct((R, C), F32), grid=(R // tr, C // tc),
        in_specs=[pl.BlockSpec((n, tr, tc), lambda i, j: (0, i, j))],
        out_specs=pl.BlockSpec((tr, tc), lambda i, j: (i, j)),
        compiler_params=_params("parallel", "parallel"),
    )(slots)


def _my_place():
    return lax.axis_index("x"), lax.axis_index("y"), lax.axis_index("c")


def _other_chips(x, y):
    return [(1 - x, y), (x, 1 - y), (1 - x, 1 - y)]


def _piece(ref, blk, axis, shard, half):
    h0 = blk[0] // 2
    idx = []
    for dim, n in enumerate(blk):
        if dim == 0:
            start = half * h0 + (shard * n if axis == 0 else 0)
            idx.append(pl.ds(start, h0))
        elif dim == axis:
            idx.append(pl.ds(shard * n, n))
        else:
            idx.append(slice(None))
    return ref.at[tuple(idx)]


def _half_of(ref, blk, half):
    return ref.at[pl.ds(half * (blk[0] // 2), blk[0] // 2)]


def _gather_weights(shards, axes, names, small):
    n = len(shards)
    blks = [_shard_shape(s) for s in shards]
    task = _gather_ici_task(shards, axes, names)

    def stage1(*refs):
        small_in, outs, small_out = refs[n], refs[n + 1:2 * n + 1], refs[2 * n + 1]
        send_sems, recv_sems, local_sems = refs[2 * n + 2:]
        sem = lambda k: (send_sems.at[k], recv_sems.at[k])
        x, y, c = _my_place()
        me = 2 * x + y
        local = pltpu.make_async_copy(small_in, small_out.at[me], local_sems.at[0])
        local.start()
        sends, recvs = task.copies(None, outs, sem)
        for k, (px, py) in enumerate(_other_chips(x, y)):
            sends.append(_remote(small_in, small_out.at[me], sem(task.n_sems + k), (px, py, c)))
            recvs.append(_remote(small_in, small_out.at[2 * px + py], sem(task.n_sems + k), (px, py, c)))
        for cp in sends:
            cp.start()
        for cp in recvs:
            cp.wait_recv()
        for cp in sends:
            cp.wait_send()
        local.wait()

    res = pl.pallas_call(
        stage1, name="gather_weights_ici",
        out_shape=task.outs + [jax.ShapeDtypeStruct((N_SHARDS,) + small.shape, small.dtype)],
        in_specs=[ANY] * (n + 1), out_specs=[ANY] * (n + 1), input_output_aliases={a: a for a in range(n)},
        scratch_shapes=[pltpu.SemaphoreType.DMA((task.n_sems + 3,)), pltpu.SemaphoreType.DMA((task.n_sems + 3,)),
                        pltpu.SemaphoreType.DMA((1,))],
    )(*task.ins, small)
    full = _run_comm(_gather_d2d_task(list(res[:n]), blks, axes), "gather_weights_d2d")
    return full, res[n]


def _run_comm(comm, name):
    n_in, n_out = len(comm.ins), len(comm.outs)

    def body(*refs):
        send_sems, recv_sems = refs[n_in + n_out:]
        sends, recvs = comm.copies(refs[:n_in], refs[n_in:n_in + n_out], lambda k: (send_sems.at[k], recv_sems.at[k]))
        for cp in sends:
            cp.start()
        for cp in recvs:
            cp.wait_recv()
        for cp in sends:
            cp.wait_send()

    return list(pl.pallas_call(
        body, name=name, out_shape=comm.outs, in_specs=[ANY] * n_in, out_specs=[ANY] * n_out,
        input_output_aliases=comm.aliases, scratch_shapes=[pltpu.SemaphoreType.DMA((comm.n_sems,))] * 2,
    )(*comm.ins))


def _exchange_small(small, extra, name):
    n_in, n_out = len(extra.ins), len(extra.outs)

    def body(*refs):
        ex_in, small_in = refs[:n_in], refs[n_in]
        ex_out, small_out = refs[n_in + 1:n_in + 1 + n_out], refs[n_in + 1 + n_out]
        send_sems, recv_sems, local_sems = refs[n_in + n_out + 2:]
        sem = lambda k: (send_sems.at[k], recv_sems.at[k])
        x, y, c = _my_place()
        me = 4 * x + 2 * y + c
        local = pltpu.make_async_copy(small_in, small_out.at[me], local_sems.at[0])
        local.start()
        sends, recvs = extra.copies(ex_in, ex_out, sem)
        for rel in range(1, N_DEVICES):
            px, py, pc = x ^ ((rel >> 2) & 1), y ^ ((rel >> 1) & 1), c ^ (rel & 1)
            k = extra.n_sems + rel - 1
            sends.append(_remote(small_in, small_out.at[me], sem(k), (px, py, pc)))
            recvs.append(_remote(small_in, small_out.at[4 * px + 2 * py + pc], sem(k), (px, py, pc)))
        for cp in sends:
            cp.start()
        for cp in recvs:
            cp.wait_recv()
        for cp in sends:
            cp.wait_send()
        local.wait()

    n_sems = extra.n_sems + N_DEVICES - 1
    res = pl.pallas_call(
        body, name=name, out_shape=extra.outs + [jax.ShapeDtypeStruct((N_DEVICES,) + small.shape, small.dtype)],
        in_specs=[ANY] * (n_in + 1), out_specs=[ANY] * (n_out + 1), input_output_aliases=extra.aliases,
        scratch_shapes=[pltpu.SemaphoreType.DMA((n_sems,)), pltpu.SemaphoreType.DMA((n_sems,)),
                        pltpu.SemaphoreType.DMA((1,))],
    )(*extra.ins, small)
    return list(res[:n_out]), res[n_out]


def _remote(src, dst, sems, device):
    return pltpu.make_async_remote_copy(src_ref=src, dst_ref=dst, send_sem=sems[0], recv_sem=sems[1],
                                        device_id=device, device_id_type=pl.DeviceIdType.MESH)


def _sum_piece(dest, slots, grad, blk, axis, layer, n_layers, name):
    r, c = blk
    h0 = r // 2
    tr = _tile(h0, 256, SUBLANES_BF16)
    tc = _tile(c, 2048, LANES)
    place = jnp.stack([2 * lax.axis_index("x") + lax.axis_index("y"), lax.axis_index("c")]).astype(jnp.int32)

    def body(p_ref, s_ref, g_ref, *rest):
        acc = g_ref[...].astype(F32) + s_ref[0].astype(F32)
        for k in range(1, N_DEVICES - 1):
            acc = acc + s_ref[k].astype(F32)
        rest[-1][...] = acc

    def g_map(i, j, p):
        return (p[1] * (h0 // tr) + (p[0] * (r // tr) if axis == 0 else 0) + i, (p[0] * (c // tc) if axis == 1 else 0) + j)

    in_specs = [pl.BlockSpec((N_DEVICES - 1, tr, tc), lambda i, j, p: (0, i, j)), pl.BlockSpec((tr, tc), g_map)]
    args = [place, slots, grad]
    if dest is not None:
        in_specs.append(ANY)
        args.append(dest)
    return pl.pallas_call(
        body, name=name, out_shape=jax.ShapeDtypeStruct((n_layers, r, c), F32),
        grid_spec=pltpu.PrefetchScalarGridSpec(
            num_scalar_prefetch=1, grid=(h0 // tr, c // tc), in_specs=in_specs,
            out_specs=pl.BlockSpec((None, tr, tc), lambda i, j, p: (layer, p[1] * (h0 // tr) + i, j))),
        input_output_aliases={3: 0} if dest is not None else {},
        compiler_params=_params("parallel", "parallel"),
    )(*args)


def _swap_halves_task(blocks):
    n = len(blocks)
    layers = [(a, l) for a, b in enumerate(blocks) for l in range(b.shape[0])]

    def copies(in_refs, out_refs, sem):
        x, y, c = _my_place()
        sends, recvs = [], []
        for k, (a, l) in enumerate(layers):
            blk = blocks[a].shape[1:]
            mine = _half_of(out_refs[a].at[l], blk, c)
            sends.append(_remote(mine, mine, sem(k), (x, y, 1 - c)))
            recvs.append(_remote(mine, _half_of(out_refs[a].at[l], blk, 1 - c), sem(k), (x, y, 1 - c)))
        return sends, recvs

    return _Comm(blocks, [jax.ShapeDtypeStruct(b.shape, b.dtype) for b in blocks], {a: a for a in range(n)},
                 len(layers), copies)


def _shard_shape(shard):
    return shard[0].shape[1:] if isinstance(shard, tuple) else shard.shape


def _place_shard(shard, axis, name):
    r, c = _shard_shape(shard)
    tr = _tile(r, 512, SUBLANES_BF16)
    full = (r * N_SHARDS, c) if axis == 0 else (r, c * N_SHARDS)
    me2 = (2 * lax.axis_index("x") + lax.axis_index("y")).astype(jnp.int32).reshape(1)

    def body(me_ref, s_ref, o_ref):
        o_ref[...] = s_ref[...].astype(o_ref.dtype)

    if axis == 0:
        out_map = lambda i, me: (me[0] * (r // tr) + i, 0)
    else:
        out_map = lambda i, me: (i, me[0])
    if isinstance(shard, tuple):
        src, layer = shard
        in_spec = pl.BlockSpec((None, tr, c), lambda i, me: (layer, i, 0))
    else:
        src, in_spec = shard, pl.BlockSpec((tr, c), lambda i, me: (i, 0))
    return pl.pallas_call(
        body, name=name, out_shape=jax.ShapeDtypeStruct(full, BF16),
        grid_spec=pltpu.PrefetchScalarGridSpec(
            num_scalar_prefetch=1, grid=(r // tr,), in_specs=[in_spec], out_specs=pl.BlockSpec((tr, c), out_map)),
        compiler_params=_params("parallel"),
    )(me2, src)


def _gather_ici_task(shards, axes, names):
    n = len(shards)
    blks = [_shard_shape(s) for s in shards]
    bases = [_place_shard(s, ax, f"place_{nm}") for s, ax, nm in zip(shards, axes, names)]

    def copies(in_refs, out_refs, sem):
        x, y, c = _my_place()
        me = 2 * x + y
        sends, recvs = [], []
        for a in range(n):
            mine = _piece(out_refs[a], blks[a], axes[a], me, c)
            for k, (px, py) in enumerate(_other_chips(x, y)):
                sends.append(_remote(mine, mine, sem(3 * a + k), (px, py, c)))
                recvs.append(_remote(mine, _piece(out_refs[a], blks[a], axes[a], 2 * px + py, c), sem(3 * a + k),
                                     (px, py, c)))
        return sends, recvs

    return _Comm(bases, [jax.ShapeDtypeStruct(b.shape, b.dtype) for b in bases], {a: a for a in range(n)}, 3 * n, copies)


def _gather_d2d_task(partials, blks, axes):
    n = len(partials)

    def copies(in_refs, out_refs, sem):
        x, y, c = _my_place()
        sends, recvs = [], []
        for a in range(n):
            for k, (px, py) in enumerate(_other_chips(x, y)):
                mine = _piece(out_refs[a], blks[a], axes[a], 2 * px + py, c)
                theirs = _piece(out_refs[a], blks[a], axes[a], 2 * px + py, 1 - c)
                sends.append(_remote(mine, mine, sem(3 * a + k), (x, y, 1 - c)))
                recvs.append(_remote(mine, theirs, sem(3 * a + k), (x, y, 1 - c)))
        return sends, recvs

    return _Comm(partials, [jax.ShapeDtypeStruct(p.shape, p.dtype) for p in partials], {a: a for a in range(n)},
                 3 * n, copies)


def _scatter_task(grads, blks, axes):
    n = len(grads)

    def copies(in_refs, out_refs, sem):
        x, y, c = _my_place()
        sends, recvs = [], []
        for rel in range(1, N_DEVICES):
            px, py, pc = x ^ ((rel >> 2) & 1), y ^ ((rel >> 1) & 1), c ^ (rel & 1)
            for a in range(n):
                src = _piece(in_refs[a], blks[a], axes[a], 2 * px + py, pc)
                k = (N_DEVICES - 1) * a + rel - 1
                sends.append(_remote(src, out_refs[a].at[rel - 1], sem(k), (px, py, pc)))
                recvs.append(_remote(src, out_refs[a].at[rel - 1], sem(k), (px, py, pc)))
        return sends, recvs

    outs = [jax.ShapeDtypeStruct((N_DEVICES - 1, b[0] // 2) + tuple(b[1:]), g.dtype) for b, g in zip(blks, grads)]
    return _Comm(grads, outs, {}, (N_DEVICES - 1) * n, copies)


def _pack(arrays):
    flat = jnp.concatenate([a.reshape(-1).astype(F32) for a in arrays])
    pad = (-flat.shape[0]) % (SUBLANES_F32 * LANES)
    return jnp.pad(flat, (0, pad)).reshape(-1, LANES)


def _unpack(packed, shapes):
    flat = packed.reshape(-1)
    out, off = [], 0
    for s in shapes:
        n = int(np.prod(s))
        out.append(flat[off:off + n].reshape(s))
        off += n
    return out


def _local_step(x, target, W, shards, geom, small):
    W = dict(W)
    B, S, D = x.shape
    T = B * S
    x2 = x.reshape(T, D)
    tgt = target.reshape(T, D)
    bmap = jnp.asarray(_bucket_map())
    blk = lambda names: [geom[k][0] for k in names]
    axs = lambda names: [geom[k][1] for k in names]
    ici = lambda names: _gather_ici_task([shards[k] for k in names], axs(names), names)
    d2d = lambda names, partials: _gather_d2d_task(list(partials), blk(names), axs(names))
    big, slots = {}, {}

    def scatter(names):
        return _scatter_task([big[k] for k in names], blk(names), axs(names))

    def ffn_bwd(l, dout, h, saved, first, second):
        xn, u, v, a = saved
        da = _mm_nt(dout, W[f"w_down{l}"], BF16, f"ffn{l}_down_dx")
        big[f"w_down{l}"] = _mm_tn(a, dout, f"ffn{l}_down_dw")
        names = first + [f"w_down{l}"]
        du, g_cw, g_cb, landed = _ffn_gate_bwd(u, v, small["ffn_conv"][l], da, S, f"ffn{l}_gate_bwd", comm=scatter(names))
        slots.update(zip(names, landed))
        big[f"w_up{l}"] = _mm_tn(xn, du, f"ffn{l}_up_dw")
        norm = (h, small["ffn_norm"][l], dout)
        if second:
            (dh, g_norm), (slots[f"w_up{l}"],) = _mm_nt(du, W[f"w_up{l}"], None, f"ffn{l}_up_dx",
                                                        comm=scatter([f"w_up{l}"]), norm=norm)
        else:
            dh, g_norm = _mm_nt(du, W[f"w_up{l}"], None, f"ffn{l}_up_dx", norm=norm)
        return dh, g_cw, g_cb, g_norm[0]

    xn0 = _rmsnorm_fwd(x2, small["a_norm"][0], "a_norm")
    p, part = _mm_nn(xn0, W["w_in"], None, BF16, "a_in", comm=ici(["w_up0"]))
    z, (W["w_up0"],) = _gate_a_fwd(p, small["a_conv"][0], S, "a_gate", comm=d2d(["w_up0"], part))
    (h1, xn1), part = _mm_nn(z, W["w_out"], x2, F32, "a_out", comm=ici(["w_down0"]), norm_gains=[small["ffn_norm"][0]])
    attn, ffn1 = ["w_kv", "w_q", "w_o"], ["w_up1", "w_down1"]
    u0, landed = _mm_nn(xn1, W["w_up0"], None, BF16, "ffn0_up", comm=_Comm.join([d2d(["w_down0"], part), ici(attn)]))
    W["w_down0"] = landed[0]
    (a0, v0), landed = _ffn_gate_fwd(u0, small["ffn_conv"][0], small["ffn_conv_b"][0], S, "ffn0_gate",
                                     comm=_Comm.join([d2d(attn, landed[1:]), ici(ffn1)]))
    W.update(zip(attn, landed[:len(attn)]))
    (h2, kvn, xn3), landed = _mm_nn(a0, W["w_down0"], h1, F32, "ffn0_down", comm=d2d(ffn1, landed[len(attn):]),
                                    norm_gains=[small["kv_norm"], small["b_norm"][0]])
    W.update(zip(ffn1, landed))
    kv = _mm_nn(kvn, W["w_kv"], None, F32, "kv_proj")
    q = _mm_nn(xn3, W["w_q"], None, F32, "q_proj")
    bias = _bias_tables(small["rel_bias"], bmap, "rel_bias_tables")
    q3, kv3 = q.reshape(B, S, D), kv.reshape(B, S, 2 * D)
    o3, lse3 = _attn_fwd(q3, kv3, bias, "attn_fwd")
    o = o3.reshape(T, D)
    h3, xn4 = _mm_nn(o, W["w_o"], h2, F32, "o_proj", norm_gains=[small["ffn_norm"][1]])
    u1 = _mm_nn(xn4, W["w_up1"], None, BF16, "ffn1_up")
    a1, v1 = _ffn_gate_fwd(u1, small["ffn_conv"][1], small["ffn_conv_b"][1], S, "ffn1_gate")
    sq_err, dh4, g_final = _loss_head(a1, W["w_down1"], h3, small["final_norm"], tgt, "ffn1_down_loss")
    loss = 0.5 * jnp.sum(sq_err) / D

    dh3, g_cw1, g_cb1, g_fn1 = ffn_bwd(1, dh4, h3, (xn4, u1, v1, a1), [], False)
    do, delta = _mm_nt(dh3, W["w_o"], F32, "o_proj_dx", head_dot=o)
    big["w_o"] = _mm_tn(o, dh3, "o_proj_dw")
    (dq3, dk3, dv3, dbias), landed = _attn_bwd(q3, kv3, delta.reshape(B, S, D), lse3, do.reshape(B, S, D), bias, "attn_bwd",
                                               comm=scatter(["w_up1", "w_o"]))
    slots.update(zip(["w_up1", "w_o"], landed))
    g_rel = _bias_grad(dbias, bmap, "rel_bias_grad")[:, :REL_BUCKETS].T
    dq, dk, dv = dq3.reshape(T, D), dk3.reshape(T, D), dv3.reshape(T, D)
    dh2, g_bn = _mm_nt(dq, W["w_q"], None, "q_proj_dx", norm=(h2, small["b_norm"][0], dh3))
    big["w_q"] = _mm_tn(xn3, dq, "q_proj_dw")
    dh2, g_kvn = _mm_nt([dk, dv], W["w_kv"], None, "kv_proj_dx", norm=(h2, small["kv_norm"], dh2))
    big["w_kv"] = jnp.concatenate([_mm_tn(kvn, dk, "k_proj_dw"), _mm_tn(kvn, dv, "v_proj_dw")], axis=1)
    dh1, g_cw0, g_cb0, g_fn0 = ffn_bwd(0, dh2, h1, (xn1, u0, v0, a0), ["w_q", "w_kv"], True)
    dz = _mm_nt(dh1, W["w_out"], BF16, "a_out_dx")
    big["w_out"] = _mm_tn(z, dh1, "a_out_dw")
    dp, g_aconv, (slots["w_out"],) = _gate_a_bwd(p, small["a_conv"][0], dz, S, "a_gate_bwd", comm=scatter(["w_out"]))
    big["w_in"] = _mm_tn(xn0, dp, "a_in_dw")
    (dx, g_an), (slots["w_in"],) = _mm_nt(dp, W["w_in"], None, "a_in_dx", comm=scatter(["w_in"]),
                                          norm=(x2, small["a_norm"][0], dh1))
    g_bn, g_kvn, g_an = g_bn[0], g_kvn[0], g_an[0]

    small_g = {"a_norm": g_an[None], "a_conv": g_aconv[None], "kv_norm": g_kvn, "b_norm": g_bn[None],
               "rel_bias": g_rel, "ffn_norm": jnp.stack([g_fn0, g_fn1]), "ffn_conv": jnp.stack([g_cw0, g_cw1]),
               "ffn_conv_b": jnp.stack([g_cb0, g_cb1]), "final_norm": g_final}
    return loss, dx.reshape(B, S, D), big, slots, small_g


BIG = ("w_in", "w_out", "w_kv", "w_q", "w_o", "w_up0", "w_up1", "w_down0", "w_down1")
SMALL = ("a_norm", "a_conv", "kv_norm", "b_norm", "rel_bias", "ffn_norm", "ffn_conv", "ffn_conv_b", "final_norm")
SMALL_SHARDED = ("a_norm", "a_conv", "ffn_conv")
WEIGHT_ORDER = ("a_norm", "a_w_in", "a_conv", "a_w_out", "kv_norm", "w_kv", "b_norm", "b_w_q", "b_w_o", "rel_bias",
                "ffn_norm", "ffn_w_up", "ffn_conv", "ffn_conv_b", "ffn_w_down", "final_norm")
GRAD_OF = {"w_in": ("a_w_in", 0), "w_out": ("a_w_out", 0), "w_kv": ("w_kv", 0), "w_q": ("b_w_q", 0), "w_o": ("b_w_o", 0),
           "w_up0": ("ffn_w_up", 0), "w_up1": ("ffn_w_up", 1), "w_down0": ("ffn_w_down", 0), "w_down1": ("ffn_w_down", 1)}


def _as2d(a):
    return a.reshape(-1, a.shape[-1])


def kernel(x, a_norm, a_w_in, a_conv, a_w_out, kv_norm, w_kv, b_norm, b_w_q, b_w_o, rel_bias, ffn_norm, ffn_w_up, ffn_conv, ffn_conv_b, ffn_w_down, final_norm, loss_target, m_a_norm, m_a_w_in, m_a_conv, m_a_w_out, m_kv_norm, m_w_kv, m_b_norm, m_b_w_q, m_b_w_o, m_rel_bias, m_ffn_norm, m_ffn_w_up, m_ffn_conv, m_ffn_conv_b, m_ffn_w_down, m_final_norm, v_a_norm, v_a_w_in, v_a_conv, v_a_w_out, v_kv_norm, v_w_kv, v_b_norm, v_b_w_q, v_b_w_o, v_rel_bias, v_ffn_norm, v_ffn_w_up, v_ffn_conv, v_ffn_conv_b, v_ffn_w_down, v_final_norm):
    given = dict(a_norm=a_norm, a_w_in=a_w_in, a_conv=a_conv, a_w_out=a_w_out, kv_norm=kv_norm, w_kv=w_kv, b_norm=b_norm,
                 b_w_q=b_w_q, b_w_o=b_w_o, rel_bias=rel_bias, ffn_norm=ffn_norm, ffn_w_up=ffn_w_up, ffn_conv=ffn_conv,
                 ffn_conv_b=ffn_conv_b, ffn_w_down=ffn_w_down, final_norm=final_norm)
    mom_m = dict(a_norm=m_a_norm, a_w_in=m_a_w_in, a_conv=m_a_conv, a_w_out=m_a_w_out, kv_norm=m_kv_norm, w_kv=m_w_kv,
                 b_norm=m_b_norm, b_w_q=m_b_w_q, b_w_o=m_b_w_o, rel_bias=m_rel_bias, ffn_norm=m_ffn_norm,
                 ffn_w_up=m_ffn_w_up, ffn_conv=m_ffn_conv, ffn_conv_b=m_ffn_conv_b, ffn_w_down=m_ffn_w_down,
                 final_norm=m_final_norm)
    mom_v = dict(a_norm=v_a_norm, a_w_in=v_a_w_in, a_conv=v_a_conv, a_w_out=v_a_w_out, kv_norm=v_kv_norm, w_kv=v_w_kv,
                 b_norm=v_b_norm, b_w_q=v_b_w_q, b_w_o=v_b_w_o, rel_bias=v_rel_bias, ffn_norm=v_ffn_norm,
                 ffn_w_up=v_ffn_w_up, ffn_conv=v_ffn_conv, ffn_conv_b=v_ffn_conv_b, ffn_w_down=v_ffn_w_down,
                 final_norm=v_final_norm)

    shard = {"w_in": ((a_w_in, 0), 1), "w_out": ((a_w_out, 0), 0), "w_kv": (w_kv, 1), "w_q": ((b_w_q, 0), 0),
             "w_o": ((b_w_o, 0), 0), "w_up0": ((ffn_w_up, 0), 1), "w_up1": ((ffn_w_up, 1), 1),
             "w_down0": ((ffn_w_down, 0), 0), "w_down1": ((ffn_w_down, 1), 0)}

    small_sharded = [given[k] for k in SMALL_SHARDED]
    packed = _pack(small_sharded)
    first = ("w_in", "w_out")
    fulls, packed_all = _gather_weights([shard[k][0] for k in first], [shard[k][1] for k in first], first, packed)
    W = dict(zip(first, fulls))
    later = {k: shard[k][0] for k in BIG if k not in first}
    geom = {k: (_shard_shape(shard[k][0]), shard[k][1]) for k in BIG}
    small = {k: given[k] for k in SMALL}
    per_shard = [_unpack(packed_all[j], [a.shape for a in small_sharded]) for j in range(N_SHARDS)]
    for i, k in enumerate(SMALL_SHARDED):
        small[k] = jnp.concatenate([per_shard[j][i] for j in range(N_SHARDS)], axis=-1)

    loss, grad_x, big_g, slots, small_g = _local_step(x, loss_target, W, later, geom, small)

    small_shapes = [small_g[k].shape for k in SMALL] + [(1,)]
    layers_of = {}
    for k in BIG:
        layers_of.setdefault(GRAD_OF[k][0], []).append(k)
    blocks = {}
    for name, members in layers_of.items():
        dest = None
        for k in members:
            dest = _sum_piece(dest, slots[k], big_g[k], geom[k][0], geom[k][1], GRAD_OF[k][1], len(members), f"sum_{k}")
        blocks[name] = dest
    swapped, small_slots = _exchange_small(_pack([small_g[k] for k in SMALL] + [loss.reshape(1)]),
                                           _swap_halves_task(list(blocks.values())), "swap_halves_exchange_small")
    reduced = dict(zip(blocks, swapped))
    *small_sums, loss = _unpack(_sum_slots(small_slots, "sum_small"), small_shapes)
    loss = loss[0]
    small_red = dict(zip(SMALL, small_sums))
    j = 2 * lax.axis_index("x") + lax.axis_index("y")
    for k in SMALL_SHARDED:
        w = given[k].shape[-1]
        small_red[k] = lax.dynamic_slice_in_dim(small_red[k], j * w, w, axis=small_red[k].ndim - 1)

    grads, deltas, new_m, new_v = {}, {}, {}, {}
    for name in WEIGHT_ORDER:
        if name in reduced:
            g = reduced[name].reshape(given[name].shape)
            d, nm, nv = _adamw(_as2d(given[name]), _as2d(g), _as2d(mom_m[name]), _as2d(mom_v[name]), f"adamw_{name}")
            grads[name] = g
            deltas[name], new_m[name], new_v[name] = (t.reshape(given[name].shape) for t in (d, nm, nv))
    small_names = [n for n in WEIGHT_ORDER if n not in reduced]
    for n in small_names:
        grads[n] = small_red[n].reshape(given[n].shape)
    sw, sg, sm, sv = (_pack([d[n] for n in small_names]) for d in (given, grads, mom_m, mom_v))
    d, nm, nv = _adamw(sw, sg, sm, sv, "adamw_small")
    shapes = [given[n].shape for n in small_names]
    for n, a, b_, c_ in zip(small_names, _unpack(d, shapes), _unpack(nm, shapes), _unpack(nv, shapes)):
        deltas[n], new_m[n], new_v[n] = a, b_, c_

    return (loss, grad_x, *[grads[n] for n in WEIGHT_ORDER], *[deltas[n] for n in WEIGHT_ORDER],
            *[new_m[n] for n in WEIGHT_ORDER], *[new_v[n] for n in WEIGHT_ORDER])
```

```python
import functools
import math

import numpy as np

import jax
import jax.numpy as jnp
from jax import lax
from jax.experimental import pallas as pl
from jax.experimental.pallas import tpu as pltpu

F32 = jnp.float32
BF16 = jnp.bfloat16

RMS_EPS = 1e-6
HEAD_DIM = 64
ATT_BLOCK = 128
DILATED_BRANCHES = ((128, 1), (512, 4), (2048, 16))
REL_BUCKETS = 32
REL_MAX_DISTANCE = 2048
MASKED_LOGIT = -1e30
ATTN_FWD_UNROLL = (8, 8, 4)
ATTN_BWD_UNROLL = (8, 8, 8)

ADAM_LR = 0.001
ADAM_B1 = 0.9
ADAM_B2 = 0.999
ADAM_EPS = 1e-08
ADAM_WD = 0.01
ADAM_STEP = 10

LANES = 128
SUBLANES_F32 = 8
SUBLANES_BF16 = 16
VMEM_LIMIT_BYTES = 56 * 1024 * 1024

N_SHARDS = 4
N_DEVICES = 8
ANY = pl.BlockSpec(memory_space=pl.ANY)


def _tile(n, pref, mult):
    best = None
    for t in range(mult, min(n, pref) + 1, mult):
        if n % t == 0:
            best = t
    if best is None:
        raise ValueError(f"no tile for {n} (multiple of {mult}, at most {pref})")
    return best


def _params(*sem):
    return pltpu.CompilerParams(dimension_semantics=sem, vmem_limit_bytes=VMEM_LIMIT_BYTES)


class _Comm:
    def __init__(self, ins, outs, aliases, n_sems, copies):
        self.ins, self.outs, self.aliases, self.n_sems, self.copies = list(ins), list(outs), dict(aliases), n_sems, copies

    @staticmethod
    def join(parts):
        parts = [p for p in parts if p is not None]
        ins, outs, aliases, offs, n_sems = [], [], {}, [], 0
        for p in parts:
            offs.append((len(ins), len(outs), n_sems))
            aliases.update({len(ins) + i: len(outs) + o for i, o in p.aliases.items()})
            ins += p.ins
            outs += p.outs
            n_sems += p.n_sems

        def copies(in_refs, out_refs, sem):
            sends, recvs = [], []
            for p, (i0, o0, s0) in zip(parts, offs):
                s, r = p.copies(in_refs[i0:i0 + len(p.ins)], out_refs[o0:o0 + len(p.outs)],
                                lambda k, s0=s0: sem(s0 + k))
                sends += s
                recvs += r
            return sends, recvs

        return _Comm(ins, outs, aliases, n_sems, copies)


def _pallas(body, *, comm=None, name, out_shape, grid=(), in_specs=(), out_specs=(), scratch_shapes=(),
            compiler_params=None):
    if comm is None:
        return pl.pallas_call(body, name=name, out_shape=out_shape, grid=grid, in_specs=in_specs, out_specs=out_specs,
                              scratch_shapes=scratch_shapes, compiler_params=compiler_params)
    single = not isinstance(out_shape, (tuple, list))
    outs = (out_shape,) if single else tuple(out_shape)
    o_specs = (out_specs,) if single else tuple(out_specs)
    n_in, n_cin, n_out, n_cout, n_scr = len(in_specs), len(comm.ins), len(outs), len(comm.outs), len(scratch_shapes)

    def carried(*refs):
        base_in, c_in = refs[:n_in], refs[n_in:n_in + n_cin]
        o0 = n_in + n_cin
        base_out, c_out = refs[o0:o0 + n_out], refs[o0 + n_out:o0 + n_out + n_cout]
        s0 = o0 + n_out + n_cout
        base_scr, (send_sems, recv_sems) = refs[s0:s0 + n_scr], refs[s0 + n_scr:]
        sem = lambda k: (send_sems.at[k], recv_sems.at[k])
        first = functools.reduce(jnp.logical_and, [pl.program_id(a) == 0 for a in range(len(grid))])
        last = functools.reduce(jnp.logical_and, [pl.program_id(a) == n - 1 for a, n in enumerate(grid)])

        @pl.when(first)
        def _():
            for cp in comm.copies(c_in, c_out, sem)[0]:
                cp.start()

        body(*base_in, *base_out, *base_scr)

        @pl.when(last)
        def _():
            sends, recvs = comm.copies(c_in, c_out, sem)
            for cp in recvs:
                cp.wait_recv()
            for cp in sends:
                cp.wait_send()

    call = pl.pallas_call(
        carried, name=name, out_shape=outs + tuple(comm.outs), grid=grid,
        in_specs=list(in_specs) + [ANY] * n_cin, out_specs=o_specs + (ANY,) * n_cout,
        scratch_shapes=list(scratch_shapes) + [pltpu.SemaphoreType.DMA((comm.n_sems,))] * 2,
        input_output_aliases={n_in + i: n_out + o for i, o in comm.aliases.items()},
        compiler_params=_params(*(["arbitrary"] * len(grid))))

    def run(*args):
        res = call(*args, *comm.ins)
        base = res[0] if single else tuple(res[:n_out])
        return base, list(res[n_out:])

    return run


def _rmsnorm_fwd(x, g, name):
    T, D = x.shape
    tm = _tile(T, 512, SUBLANES_BF16)

    def body(x_ref, g_ref, o_ref):
        xf = x_ref[...]
        r = lax.rsqrt(jnp.mean(xf * xf, axis=-1, keepdims=True) + RMS_EPS)
        o_ref[...] = ((xf * r) * g_ref[...]).astype(o_ref.dtype)

    return pl.pallas_call(
        body, name=name, out_shape=jax.ShapeDtypeStruct((T, D), BF16), grid=(T // tm,),
        in_specs=[pl.BlockSpec((tm, D), lambda i: (i, 0)), pl.BlockSpec((1, D), lambda i: (0, 0))],
        out_specs=pl.BlockSpec((tm, D), lambda i: (i, 0)),
        compiler_params=_params("parallel"),
    )(x, g.reshape(1, D))


def _loss_head(a, b, res, g, target, name):
    T, K = a.shape
    D = b.shape[1]
    tm = _tile(T, 512, SUBLANES_BF16)

    def body(a_ref, b_ref, res_ref, g_ref, t_ref, dh_ref, acc_ref):
        @pl.when(pl.program_id(0) == 0)
        def _():
            acc_ref[...] = jnp.zeros_like(acc_ref)

        xf = jnp.dot(a_ref[...].astype(BF16), b_ref[...], preferred_element_type=F32) + res_ref[...]
        r = lax.rsqrt(jnp.mean(xf * xf, axis=-1, keepdims=True) + RMS_EPS)
        xhat = xf * r
        err = xhat * g_ref[...] - t_ref[...]
        dy = err * (1.0 / D)
        acc_ref[0:1, :] += jnp.sum(dy * xhat, axis=0, keepdims=True)
        acc_ref[1:2, :] += jnp.sum(err * err, axis=0, keepdims=True)
        t = dy * g_ref[...]
        dh_ref[...] = r * (t - xhat * jnp.mean(t * xhat, axis=-1, keepdims=True))

    row = pl.BlockSpec((tm, D), lambda i: (i, 0))
    dh, acc = pl.pallas_call(
        body, name=name,
        out_shape=(jax.ShapeDtypeStruct((T, D), F32), jax.ShapeDtypeStruct((SUBLANES_F32, D), F32)),
        grid=(T // tm,),
        in_specs=[pl.BlockSpec((tm, K), lambda i: (i, 0)), pl.BlockSpec((K, D), lambda i: (0, 0)), row,
                  pl.BlockSpec((1, D), lambda i: (0, 0)), row],
        out_specs=(row, pl.BlockSpec((SUBLANES_F32, D), lambda i: (0, 0))),
        compiler_params=_params("arbitrary"),
    )(a, b, res, g.reshape(1, D), target)
    return acc[1], dh, acc[0]


def _mm_nn(a, b, res, out_dtype, name, comm=None, norm_gains=()):
    T, K = a.shape
    N = b.shape[1]
    n_g = len(norm_gains)
    tm = _tile(T, 512 if n_g else 1024, SUBLANES_BF16)
    tn = _tile(N, 3072, LANES)
    assert not n_g or tn == N, "the fused rmsnorm needs whole rows in one tile"
    n_i = T // tm
    total = (N // tn) * n_i
    slots = 3

    def body(a_hbm, b_ref, *rest):
        (a_buf, a_sem), rest = rest[-2:], rest[:-2]
        step = pl.program_id(0) * n_i + pl.program_id(1)

        def fetch(s):
            start = (s % n_i) * tm
            rows = pl.ds(start if isinstance(s, int) else pl.multiple_of(start, tm), tm)
            return pltpu.make_async_copy(a_hbm.at[rows, :], a_buf.at[s % slots], a_sem.at[s % slots])

        @pl.when(step == 0)
        def _():
            for s in range(min(slots - 1, total)):
                fetch(s).start()

        @pl.when(step + slots - 1 < total)
        def _():
            fetch(step + slots - 1).start()

        fetch(step).wait()
        ins, outs = rest[:len(rest) - 1 - n_g], rest[len(rest) - 1 - n_g:]
        acc = jnp.dot(a_buf[step % slots].astype(BF16), b_ref[...], preferred_element_type=F32)
        if res is not None:
            acc = acc + ins[0][...]
        outs[0][...] = acc.astype(outs[0].dtype)
        if n_g:
            y = acc * lax.rsqrt(jnp.mean(acc * acc, axis=-1, keepdims=True) + RMS_EPS)
            for g_ref, xn_ref in zip(ins[len(ins) - n_g:], outs[1:]):
                xn_ref[...] = (y * g_ref[...]).astype(xn_ref.dtype)

    tile = pl.BlockSpec((tm, tn), lambda j, i: (i, j))
    in_specs = [ANY, pl.BlockSpec((K, tn), lambda j, i: (0, j))]
    args = [a, b]
    if res is not None:
        in_specs.append(tile)
        args.append(res)
    for g in norm_gains:
        in_specs.append(pl.BlockSpec((1, N), lambda j, i: (0, 0)))
        args.append(g.reshape(1, N))
    out_shape = jax.ShapeDtypeStruct((T, N), out_dtype)
    if n_g:
        out_shape = (out_shape,) + (jax.ShapeDtypeStruct((T, N), BF16),) * n_g
    return _pallas(
        body, comm=comm, name=name, out_shape=out_shape, grid=(N // tn, n_i),
        in_specs=in_specs, out_specs=(tile,) * (1 + n_g) if n_g else tile,
        scratch_shapes=[pltpu.VMEM((slots, tm, K), a.dtype), pltpu.SemaphoreType.DMA((slots,))],
        compiler_params=_params("arbitrary", "arbitrary"),
    )(*args)


def _mm_nt(dy, b, out_dtype, name, comm=None, norm=None, head_dot=None):
    dys = list(dy) if isinstance(dy, (list, tuple)) else [dy]
    T, n_each = dys[0].shape
    K = b.shape[0]
    tm = _tile(T, 2048 if norm is None and head_dot is None else 1024, SUBLANES_BF16)
    tk = _tile(K, 1536, LANES)
    tn = _tile(n_each, 2816 if norm is None else 1536, LANES)
    per = n_each // tn
    n_steps = per * len(dys)
    assert norm is None or tk == K, "the fused rmsnorm backward needs whole rows in one tile"

    def body(*refs):
        dy_refs, b_ref, acc_ref = refs[:len(dys)], refs[len(dys)], refs[-1]
        i, n = pl.program_id(0), pl.program_id(2)

        @pl.when(n == 0)
        def _():
            acc_ref[...] = jnp.zeros_like(acc_ref)

        for p, dy_ref in enumerate(dy_refs):
            @pl.when(jnp.logical_and(n >= p * per, n < (p + 1) * per))
            def _(dy_ref=dy_ref):
                acc_ref[...] += lax.dot_general(dy_ref[...].astype(BF16), b_ref[...], (((1,), (1,)), ((), ())),
                                                preferred_element_type=F32)

        if norm is None and head_dot is None:
            @pl.when(n == n_steps - 1)
            def _():
                refs[-2][...] = acc_ref[...].astype(refs[-2].dtype)
        elif norm is None:
            o_ref, out_ref, dot_ref = refs[len(dys) + 1:-1]

            @pl.when(n == n_steps - 1)
            def _():
                d = acc_ref[...]
                out_ref[...] = d.astype(out_ref.dtype)
                head0 = lax.broadcasted_iota(jnp.int32, (tm, LANES), 1) < HEAD_DIM
                for c0 in range(0, tk, LANES):
                    x = d[:, c0:c0 + LANES] * o_ref[:, c0:c0 + LANES]
                    d0 = jnp.sum(jnp.where(head0, x, 0.0), axis=-1, keepdims=True)
                    d1 = jnp.sum(jnp.where(head0, 0.0, x), axis=-1, keepdims=True)
                    dot_ref[:, c0:c0 + LANES] = jnp.where(head0, d0, d1)
        else:
            x_ref, g_ref, dres_ref, dx_ref, dg_ref = refs[len(dys) + 1:-1]

            @pl.when(jnp.logical_and(i == 0, n == 0))
            def _():
                dg_ref[...] = jnp.zeros_like(dg_ref)

            @pl.when(n == n_steps - 1)
            def _():
                xf = x_ref[...]
                r = lax.rsqrt(jnp.mean(xf * xf, axis=-1, keepdims=True) + RMS_EPS)
                xhat = xf * r
                d = acc_ref[...]
                dg_ref[0:1, :] += jnp.sum(d * xhat, axis=0, keepdims=True)
                t = d * g_ref[...]
                dx_ref[...] = dres_ref[...] + r * (t - xhat * jnp.mean(t * xhat, axis=-1, keepdims=True))

    in_specs = [pl.BlockSpec((tm, tn), lambda i, k, n, p=p: (i, jnp.clip(n - p * per, 0, per - 1))) for p in range(len(dys))]
    in_specs.append(pl.BlockSpec((tk, tn), lambda i, k, n: (k, n)))
    args = dys + [b]
    tile = pl.BlockSpec((tm, tk), lambda i, k, n: (i, k))
    if norm is None and head_dot is None:
        out_shape, out_specs = jax.ShapeDtypeStruct((T, K), out_dtype), tile
    elif norm is None:
        in_specs.append(tile)
        args.append(head_dot)
        out_shape = (jax.ShapeDtypeStruct((T, K), out_dtype), jax.ShapeDtypeStruct((T, K), F32))
        out_specs = (tile, tile)
    else:
        x, g, dres = norm
        in_specs += [tile, pl.BlockSpec((1, K), lambda i, k, n: (0, 0)), tile]
        args += [x, g.reshape(1, K), dres]
        out_shape = (jax.ShapeDtypeStruct((T, K), F32), jax.ShapeDtypeStruct((SUBLANES_F32, K), F32))
        out_specs = (tile, pl.BlockSpec((SUBLANES_F32, K), lambda i, k, n: (0, 0)))
    return _pallas(
        body, comm=comm, name=name, out_shape=out_shape, grid=(T // tm, K // tk, n_steps),
        in_specs=in_specs, out_specs=out_specs, scratch_shapes=[pltpu.VMEM((tm, tk), F32)],
        compiler_params=_params("parallel", "parallel", "arbitrary") if norm is None else _params(*["arbitrary"] * 3),
    )(*args)


def _mm_tn(a, dy, name):
    T, K = a.shape
    N = dy.shape[1]
    tt = _tile(T, 2048, SUBLANES_BF16)
    tk = _tile(K, 1536, LANES)
    tn = _tile(N, 1536, LANES)
    t_steps = T // tt

    def body(a_ref, dy_ref, o_ref, acc_ref):
        t = pl.program_id(2)

        @pl.when(t == 0)
        def _():
            acc_ref[...] = jnp.zeros_like(acc_ref)

        acc_ref[...] += lax.dot_general(a_ref[...].astype(BF16), dy_ref[...].astype(BF16),
                                        (((0,), (0,)), ((), ())), preferred_element_type=F32)

        @pl.when(t == t_steps - 1)
        def _():
            o_ref[...] = acc_ref[...].astype(o_ref.dtype)

    return pl.pallas_call(
        body, name=name, out_shape=jax.ShapeDtypeStruct((K, N), BF16), grid=(K // tk, N // tn, t_steps),
        in_specs=[pl.BlockSpec((tt, tk), lambda k, n, t: (t, k)), pl.BlockSpec((tt, tn), lambda k, n, t: (t, n))],
        out_specs=pl.BlockSpec((tk, tn), lambda k, n, t: (k, n)),
        scratch_shapes=[pltpu.VMEM((tk, tn), F32)],
        compiler_params=_params("parallel", "parallel", "arbitrary"),
    )(a, dy)


def _rows_before(halo, cur, k):
    h = halo.shape[0]
    return pltpu.roll(jnp.concatenate([halo, cur], axis=0), k, 0)[h:]


def _rows_after(cur, halo, k):
    n = cur.shape[0]
    total = n + halo.shape[0]
    return pltpu.roll(jnp.concatenate([cur, halo], axis=0), total - k, 0)[:n]


def _halo_specs(tm, width, n_rows):
    per = tm // SUBLANES_BF16
    last = n_rows // SUBLANES_BF16 - 1
    prev = pl.BlockSpec((SUBLANES_BF16, width), lambda i: (jnp.maximum(i * per - 1, 0), 0))
    nxt = pl.BlockSpec((SUBLANES_BF16, width), lambda i: (jnp.minimum((i + 1) * per, last), 0))
    return prev, nxt


def _gate_a_fwd(p, cw, seq, name, comm=None):
    T, D3 = p.shape
    D = D3 // 3
    tm = _tile(seq, 512, SUBLANES_BF16)
    cc = _tile(D, 256, LANES)
    prev, _ = _halo_specs(tm, D3, T)

    def body(p_ref, ph_ref, cw_ref, z_ref):
        at_start = (pl.program_id(0) * tm) % seq == 0
        for c0 in range(0, D, cc):
            b = p_ref[:, c0:c0 + cc].astype(F32)
            u = p_ref[:, D + c0:D + c0 + cc].astype(F32) * p_ref[:, 2 * D + c0:2 * D + c0 + cc].astype(F32)
            uh = ph_ref[:, D + c0:D + c0 + cc].astype(F32) * ph_ref[:, 2 * D + c0:2 * D + c0 + cc].astype(F32)
            uh = jnp.where(at_start, 0.0, uh)
            w = cw_ref[:, c0:c0 + cc]
            cv = _rows_before(uh, u, 2) * w[0:1] + _rows_before(uh, u, 1) * w[1:2] + u * w[2:3]
            z_ref[:, c0:c0 + cc] = (b * cv).astype(z_ref.dtype)

    return _pallas(
        body, comm=comm, name=name, out_shape=jax.ShapeDtypeStruct((T, D), BF16), grid=(T // tm,),
        in_specs=[pl.BlockSpec((tm, D3), lambda i: (i, 0)), prev, pl.BlockSpec((3, D), lambda i: (0, 0))],
        out_specs=pl.BlockSpec((tm, D), lambda i: (i, 0)),
        compiler_params=_params("parallel"),
    )(p, p, cw)


def _gate_a_bwd(p, cw, dz, seq, name, comm=None):
    T, D3 = p.shape
    D = D3 // 3
    tm = _tile(seq, 512, SUBLANES_BF16)
    cc = _tile(D, 256, LANES)
    p_prev, p_next = _halo_specs(tm, D3, T)
    _, dz_next = _halo_specs(tm, D, T)

    def body(p_ref, pp_ref, pn_ref, dz_ref, dzn_ref, cw_ref, dp_ref, dcw_ref):
        i = pl.program_id(0)

        @pl.when(i == 0)
        def _():
            dcw_ref[...] = jnp.zeros_like(dcw_ref)

        at_start = (i * tm) % seq == 0
        at_end = ((i + 1) * tm) % seq == 0
        for c0 in range(0, D, cc):
            cb, cc_, ch = slice(c0, c0 + cc), slice(D + c0, D + c0 + cc), slice(2 * D + c0, 2 * D + c0 + cc)
            b = p_ref[:, cb].astype(F32)
            c = p_ref[:, cc_].astype(F32)
            hh = p_ref[:, ch].astype(F32)
            u = c * hh
            uh = jnp.where(at_start, 0.0, pp_ref[:, cc_].astype(F32) * pp_ref[:, ch].astype(F32))
            w = cw_ref[:, cb]
            u1 = _rows_before(uh, u, 1)
            u2 = _rows_before(uh, u, 2)
            cv = u2 * w[0:1] + u1 * w[1:2] + u * w[2:3]
            dz_t = dz_ref[:, cb].astype(F32)
            dcv = dz_t * b
            dcvn = jnp.where(at_end, 0.0, dzn_ref[:, cb].astype(F32) * pn_ref[:, cb].astype(F32))
            du = dcv * w[2:3] + _rows_after(dcv, dcvn, 1) * w[1:2] + _rows_after(dcv, dcvn, 2) * w[0:1]
            dp_ref[:, cb] = (dz_t * cv).astype(dp_ref.dtype)
            dp_ref[:, cc_] = (du * hh).astype(dp_ref.dtype)
            dp_ref[:, ch] = (du * c).astype(dp_ref.dtype)
            dcw_ref[0:1, cb] += jnp.sum(dcv * u2, axis=0, keepdims=True)
            dcw_ref[1:2, cb] += jnp.sum(dcv * u1, axis=0, keepdims=True)
            dcw_ref[2:3, cb] += jnp.sum(dcv * u, axis=0, keepdims=True)

    res = _pallas(
        body, comm=comm, name=name,
        out_shape=(jax.ShapeDtypeStruct((T, D3), BF16), jax.ShapeDtypeStruct((SUBLANES_F32, D), F32)),
        grid=(T // tm,),
        in_specs=[pl.BlockSpec((tm, D3), lambda i: (i, 0)), p_prev, p_next,
                  pl.BlockSpec((tm, D), lambda i: (i, 0)), dz_next, pl.BlockSpec((3, D), lambda i: (0, 0))],
        out_specs=(pl.BlockSpec((tm, D3), lambda i: (i, 0)), pl.BlockSpec((SUBLANES_F32, D), lambda i: (0, 0))),
        compiler_params=_params("arbitrary"),
    )(p, p, p, dz, dz, cw)
    if comm is None:
        return res[0], res[1][0:3]
    return res[0][0], res[0][1][0:3], res[1]


def _ffn_gate_fwd(u, cw, cb, seq, name, comm=None):
    T, F2 = u.shape
    F = F2 // 2
    tm = _tile(seq, 512, SUBLANES_BF16)
    cc = _tile(F, 256, LANES)
    prev, _ = _halo_specs(tm, F2, T)

    def body(u_ref, uh_ref, cw_ref, cb_ref, a_ref, v_ref):
        at_start = (pl.program_id(0) * tm) % seq == 0

        def conv(c0):
            cols = slice(c0, c0 + cc)
            cur = u_ref[:, cols].astype(F32)
            halo = jnp.where(at_start, 0.0, uh_ref[:, cols].astype(F32))
            w = cw_ref[:, cols]
            return (_rows_before(halo, cur, 2) * w[0:1] + _rows_before(halo, cur, 1) * w[1:2] + cur * w[2:3]
                    + cb_ref[:, cols])

        for c0 in range(0, F, cc):
            g = conv(c0)
            up = conv(F + c0)
            a_ref[:, c0:c0 + cc] = ((g * jax.nn.sigmoid(g)) * up).astype(a_ref.dtype)
            v_ref[:, c0:c0 + cc] = g.astype(v_ref.dtype)
            v_ref[:, F + c0:F + c0 + cc] = up.astype(v_ref.dtype)

    return _pallas(
        body, comm=comm, name=name,
        out_shape=(jax.ShapeDtypeStruct((T, F), BF16), jax.ShapeDtypeStruct((T, F2), BF16)), grid=(T // tm,),
        in_specs=[pl.BlockSpec((tm, F2), lambda i: (i, 0)), prev,
                  pl.BlockSpec((3, F2), lambda i: (0, 0)), pl.BlockSpec((1, F2), lambda i: (0, 0))],
        out_specs=(pl.BlockSpec((tm, F), lambda i: (i, 0)), pl.BlockSpec((tm, F2), lambda i: (i, 0))),
        compiler_params=_params("parallel"),
    )(u, u, cw, cb.reshape(1, F2))


def _ffn_gate_bwd(u, v, cw, da, seq, name, comm=None):
    T, F2 = u.shape
    F = F2 // 2
    H = SUBLANES_BF16
    tm = _tile(seq, 512, H)
    cc = _tile(F, 256, LANES)
    _, v_next = _halo_specs(tm, F2, T)
    _, da_next = _halo_specs(tm, F, T)

    def body(u_ref, v_ref, vn_ref, da_ref, dan_ref, cw_ref, du_ref, acc_ref):
        i = pl.program_id(0)

        @pl.when(i == 0)
        def _():
            acc_ref[...] = jnp.zeros_like(acc_ref)

        at_end = ((i + 1) * tm) % seq == 0
        n = tm + H

        def rows_and_next(ref, nxt, cols):
            return jnp.concatenate([ref[:, cols].astype(F32), nxt[:, cols].astype(F32)], axis=0)

        def back(d, cols):
            w = cw_ref[:, cols]
            d0 = d[:tm]
            d1 = pltpu.roll(d, n - 1, 0)[:tm]
            d2 = pltpu.roll(d, n - 2, 0)[:tm]
            du_ref[:, cols] = (d0 * w[2:3] + d1 * w[1:2] + d2 * w[0:1]).astype(du_ref.dtype)
            ut = u_ref[:, cols].astype(F32)
            acc_ref[0:1, cols] += jnp.sum(d2 * ut, axis=0, keepdims=True)
            acc_ref[1:2, cols] += jnp.sum(d1 * ut, axis=0, keepdims=True)
            acc_ref[2:3, cols] += jnp.sum(d0 * ut, axis=0, keepdims=True)
            acc_ref[3:4, cols] += jnp.sum(d0, axis=0, keepdims=True)

        for c0 in range(0, F, cc):
            gc, uc = slice(c0, c0 + cc), slice(F + c0, F + c0 + cc)
            g = rows_and_next(v_ref, vn_ref, gc)
            up = rows_and_next(v_ref, vn_ref, uc)
            da_ext = jnp.concatenate([da_ref[:, gc].astype(F32),
                                      jnp.where(at_end, 0.0, dan_ref[:, gc].astype(F32))], axis=0)
            sg = jax.nn.sigmoid(g)
            back(da_ext * up * (sg * (1.0 + g * (1.0 - sg))), gc)
            back(da_ext * (g * sg), uc)

    res = _pallas(
        body, comm=comm, name=name,
        out_shape=(jax.ShapeDtypeStruct((T, F2), BF16), jax.ShapeDtypeStruct((SUBLANES_F32, F2), F32)),
        grid=(T // tm,),
        in_specs=[pl.BlockSpec((tm, F2), lambda i: (i, 0)), pl.BlockSpec((tm, F2), lambda i: (i, 0)), v_next,
                  pl.BlockSpec((tm, F), lambda i: (i, 0)), da_next, pl.BlockSpec((3, F2), lambda i: (0, 0))],
        out_specs=(pl.BlockSpec((tm, F2), lambda i: (i, 0)), pl.BlockSpec((SUBLANES_F32, F2), lambda i: (0, 0))),
        compiler_params=_params("arbitrary"),
    )(u, v, v, da, da, cw)
    (du, acc), landed = res if comm is not None else (res, None)
    return (du, acc[0:3], acc[3]) if comm is None else (du, acc[0:3], acc[3], landed)


def _bucket_map():
    P = ATT_BLOCK
    qi = np.arange(P, dtype=np.int64)[:, None]
    kc = np.arange(2 * P, dtype=np.int64)[None, :]
    delta = qi + P - kc
    maps = []
    max_exact = REL_BUCKETS // 2
    for window, dilation in DILATED_BRANCHES:
        band = (delta >= 0) & (delta <= window // dilation)
        n = np.maximum(delta * dilation, 0)
        nf = np.maximum(n, max_exact).astype(np.float32)
        large = max_exact + (np.log(nf / np.float32(max_exact)) / np.float32(math.log(REL_MAX_DISTANCE / max_exact))
                             * np.float32(REL_BUCKETS - max_exact)).astype(np.int32)
        large = np.minimum(large, REL_BUCKETS - 1)
        bucket = np.where(n < max_exact, n, large)
        maps.append(np.where(band, bucket, -1).astype(np.int32))
    return np.stack(maps)


def _bias_tables(rel_bias, bmap, name):
    n_pairs = rel_bias.shape[1] // 2
    nbr, P, P2 = bmap.shape

    def body(rb_ref, bm_ref, o_ref):
        pair = pl.program_id(0)
        in_seq = lax.broadcasted_iota(jnp.int32, (P, P2), 1) >= P
        for br in range(nbr):
            bm = bm_ref[br]
            for hh in range(2):
                acc = jnp.full((P, P2), MASKED_LOGIT, F32)
                for b in range(REL_BUCKETS):
                    acc = jnp.where(bm == b, rb_ref[b, 2 * pair + hh], acc)
                o_ref[br, 0, 0, hh * P:(hh + 1) * P, :] = acc
                o_ref[br, 0, 1, hh * P:(hh + 1) * P, :] = jnp.where(in_seq, acc, MASKED_LOGIT)

    return pl.pallas_call(
        body, name=name, out_shape=jax.ShapeDtypeStruct((nbr, n_pairs, 2, 2 * P, P2), F32), grid=(n_pairs,),
        in_specs=[pl.BlockSpec(memory_space=pltpu.SMEM), pl.BlockSpec((nbr, P, P2), lambda h: (0, 0, 0))],
        out_specs=pl.BlockSpec((nbr, 1, 2, 2 * P, P2), lambda h: (0, h, 0, 0, 0)),
        compiler_params=_params("parallel"),
    )(rel_bias, bmap)


def _bias_grad(dbias, bmap, name):
    nbr, n_pairs, _, P2 = dbias.shape
    P = P2 // 2

    def body(db_ref, bm_ref, o_ref):
        lane = lax.broadcasted_iota(jnp.int32, (1, LANES), 1)
        for hh in range(2):
            row = jnp.zeros((1, LANES), F32)
            for br in range(nbr):
                bm = bm_ref[br]
                d = db_ref[br, 0, hh * P:(hh + 1) * P, :]
                for b in range(REL_BUCKETS):
                    hit = jnp.sum(jnp.where(bm == b, d, 0.0), axis=1, keepdims=True)
                    row = row + jnp.where(lane == b, jnp.sum(hit, axis=0, keepdims=True), 0.0)
            o_ref[hh] = row

    return pl.pallas_call(
        body, name=name, out_shape=jax.ShapeDtypeStruct((2 * n_pairs, 1, LANES), F32), grid=(n_pairs,),
        in_specs=[pl.BlockSpec((nbr, 1, P2, P2), lambda h: (0, h, 0, 0)), pl.BlockSpec((nbr, P, P2), lambda h: (0, 0, 0))],
        out_specs=pl.BlockSpec((2, 1, LANES), lambda h: (h, 0, 0)),
        compiler_params=_params("parallel"),
    )(dbias, bmap)[:, 0, :]


def _rows(start, dilation):
    if dilation == 1:
        return pl.ds(pl.multiple_of(start, ATT_BLOCK), ATT_BLOCK)
    return pl.ds(start, ATT_BLOCK, stride=dilation)


def _for_each_block(seq, unroll, fn):
    P = ATT_BLOCK
    for br, (_, d) in enumerate(DILATED_BRANCHES):
        nb = seq // d // P
        u = unroll[br] if (unroll[br] % nb == 0 or nb % unroll[br] == 0) else 1
        step = d * P

        def some(i, carry, br=br, d=d, nb=nb, u=u, step=step):
            blocks = []
            if u % nb == 0:
                for k in range(u):
                    if k % nb == 0:
                        start = i * (u // nb) + k // nb
                        blocks.append((start, start, 1))
                    else:
                        blocks.append((blocks[-1][0] + step, blocks[-1][0], 0))
            else:
                r, j0 = (i * u) // nb, (i * u) % nb
                blocks.append((r + j0 * step, r + jnp.maximum(j0 - 1, 0) * step, jnp.where(j0 == 0, 1, 0)))
                for _ in range(1, u):
                    blocks.append((blocks[-1][0] + step, blocks[-1][0], 0))
            fn(br, d, blocks)
            return carry

        lax.fori_loop(0, nb * d // u, some, 0)


class _RowCache:
    def __init__(self, dilation):
        self.dilation, self.seen = dilation, {}

    def rows(self, ref, start):
        key = (id(ref), id(start))
        if key not in self.seen:
            self.seen[key] = ref[_rows(start, self.dilation), :].astype(BF16)
        return self.seen[key]

    def window(self, ref, start, prev):
        return jnp.concatenate([self.rows(ref, prev), self.rows(ref, start)], axis=0)


def _stack_heads(x, head0):
    return jnp.concatenate([jnp.where(head0, x, 0.0), jnp.where(head0, 0.0, x)], axis=0).astype(BF16)


def _attn_fwd(q, kv, bias, name):
    B, S, D = q.shape
    P = ATT_BLOCK
    n_pairs = D // LANES
    nbr = len(DILATED_BRANCHES)
    scale = HEAD_DIM ** -0.5

    def body(q_ref, k_ref, v_ref, bias_ref, o_ref, lse_ref, *stats):
        m_s, l_s, acc_s = stats[0:nbr], stats[nbr:2 * nbr], stats[2 * nbr:3 * nbr]
        head0 = lax.broadcasted_iota(jnp.int32, (P, LANES), 1) < HEAD_DIM

        def block(br, d, blocks):
            cache = _RowCache(d)
            s = [lax.dot_general(_stack_heads(q_ref[_rows(start, d), :] * scale, head0),
                                 cache.window(k_ref, start, prev), (((1,), (1,)), ((), ())),
                                 preferred_element_type=F32) + bias_ref[br, 0, first]
                 for start, prev, first in blocks]
            m = [jnp.max(x, axis=-1, keepdims=True) for x in s]
            p = [jnp.exp(x - y) for x, y in zip(s, m)]
            l = [jnp.sum(x, axis=-1, keepdims=True) for x in p]
            pv = [jnp.dot(x.astype(BF16), cache.window(v_ref, start, prev), preferred_element_type=F32)
                  for x, (start, prev, _) in zip(p, blocks)]
            for k, (start, _, _) in enumerate(blocks):
                rows = _rows(start, d)
                m_s[br][rows, :] = jnp.where(head0, m[k][:P], m[k][P:])
                l_s[br][rows, :] = jnp.where(head0, l[k][:P], l[k][P:])
                acc_s[br][rows, :] = jnp.where(head0, pv[k][:P], pv[k][P:])

        _for_each_block(S, ATTN_FWD_UNROLL, block)

        chunk = _tile(S, 256, SUBLANES_F32)

        def merge(i, carry):
            rows = pl.ds(pl.multiple_of(i * chunk, chunk), chunk)
            ms = [m_s[br][rows, :] for br in range(nbr)]
            m = functools.reduce(jnp.maximum, ms)
            l = jnp.zeros((chunk, LANES), F32)
            acc = jnp.zeros((chunk, LANES), F32)
            for br in range(nbr):
                w = jnp.exp(ms[br] - m)
                l = l + w * l_s[br][rows, :]
                acc = acc + w * acc_s[br][rows, :]
            o_ref[rows, :] = acc / l
            lse_ref[rows, :] = m + jnp.log(l)
            return carry

        lax.fori_loop(0, S // chunk, merge, 0)

    slab = lambda col0: pl.BlockSpec((None, S, LANES), lambda b, h: (b, 0, col0 + h))
    return pl.pallas_call(
        body, name=name,
        out_shape=(jax.ShapeDtypeStruct((B, S, D), F32), jax.ShapeDtypeStruct((B, S, D), F32)),
        grid=(B, n_pairs),
        in_specs=[slab(0), slab(0), slab(n_pairs),
                  pl.BlockSpec((nbr, 1, 2, 2 * P, 2 * P), lambda b, h: (0, h, 0, 0, 0))],
        out_specs=(slab(0), slab(0)),
        scratch_shapes=[pltpu.VMEM((S, LANES), F32)] * (3 * nbr),
        compiler_params=_params("parallel", "parallel"),
    )(q, kv, kv, bias)


def _attn_bwd(q, kv, delta, lse, do, bias, name, comm=None):
    B, S, D = q.shape
    P = ATT_BLOCK
    n_pairs = D // LANES
    nbr = len(DILATED_BRANCHES)
    scale = HEAD_DIM ** -0.5

    def body(q_ref, k_ref, v_ref, delta_s, lse_ref, do_ref, bias_ref, dq_ref, dk_ref, dv_ref, dbias_ref):
        head0 = lax.broadcasted_iota(jnp.int32, (P, LANES), 1) < HEAD_DIM

        @pl.when(pl.program_id(1) == 0)
        def _():
            dbias_ref[...] = jnp.zeros_like(dbias_ref)

        chunk = _tile(S, 512, SUBLANES_F32)

        def prepare(i, carry):
            rows = pl.ds(pl.multiple_of(i * chunk, chunk), chunk)
            zero = jnp.zeros((chunk, LANES), F32)
            dq_ref[rows, :] = zero
            dk_ref[rows, :] = zero
            dv_ref[rows, :] = zero
            return carry

        lax.fori_loop(0, S // chunk, prepare, 0)

        def per_head(x):
            return jnp.concatenate([x[:, 0:1], x[:, HEAD_DIM:HEAD_DIM + 1]], axis=0)

        nt = (((1,), (1,)), ((), ()))
        tn = (((0,), (0,)), ((), ()))

        def block(br, d, blocks):
            cache = _RowCache(d)
            q2 = [_stack_heads(q_ref[_rows(start, d), :] * scale, head0) for start, _, _ in blocks]
            do2 = [_stack_heads(do_ref[_rows(start, d), :], head0) for start, _, _ in blocks]
            kb = [cache.window(k_ref, start, prev) for start, prev, _ in blocks]
            vb = [cache.window(v_ref, start, prev) for start, prev, _ in blocks]
            s = [lax.dot_general(a, b, nt, preferred_element_type=F32) + bias_ref[br, 0, first]
                 for a, b, (_, _, first) in zip(q2, kb, blocks)]
            dp = [lax.dot_general(a, b, nt, preferred_element_type=F32) for a, b in zip(do2, vb)]
            p = [jnp.exp(x - per_head(lse_ref[_rows(start, d), :])) for x, (start, _, _) in zip(s, blocks)]
            ds = [x * (y - per_head(delta_s[_rows(start, d), :])) for x, y, (start, _, _) in zip(p, dp, blocks)]
            for x in ds:
                dbias_ref[br, 0] += x
            ds16 = [x.astype(BF16) for x in ds]
            dq2 = [jnp.dot(a, b, preferred_element_type=F32) for a, b in zip(ds16, kb)]
            dk = [lax.dot_general(a, b, tn, preferred_element_type=F32) for a, b in zip(ds16, q2)]
            dv = [lax.dot_general(a.astype(BF16), b, tn, preferred_element_type=F32) for a, b in zip(p, do2)]
            parts = {}
            for k, (start, prev, first) in enumerate(blocks):
                dq_ref[_rows(start, d), :] += jnp.where(head0, dq2[k][:P], dq2[k][P:]) * scale
                parts.setdefault(id(start), [start, []])[1].append((dk[k][P:], dv[k][P:]))
                if not (isinstance(first, int) and first == 1):
                    parts.setdefault(id(prev), [prev, []])[1].append((dk[k][:P], dv[k][:P]))
            for start, terms in parts.values():
                rows = _rows(start, d)
                dk_ref[rows, :] += functools.reduce(jnp.add, [t[0] for t in terms])
                dv_ref[rows, :] += functools.reduce(jnp.add, [t[1] for t in terms])

        _for_each_block(S, ATTN_BWD_UNROLL, block)

    slab = lambda col0: pl.BlockSpec((None, S, LANES), lambda h, b: (b, 0, col0 + h))
    tab = pl.BlockSpec((nbr, 1, 2, 2 * P, 2 * P), lambda h, b: (0, h, 0, 0, 0))
    dtab = pl.BlockSpec((nbr, 1, 2 * P, 2 * P), lambda h, b: (0, h, 0, 0))
    shp = jax.ShapeDtypeStruct((B, S, D), F32)
    return _pallas(
        body, comm=comm, name=name,
        out_shape=(shp, shp, shp, jax.ShapeDtypeStruct((nbr, n_pairs, 2 * P, 2 * P), F32)),
        grid=(n_pairs, B),
        in_specs=[slab(0), slab(0), slab(n_pairs), slab(0), slab(0), slab(0), tab],
        out_specs=(slab(0), slab(0), slab(0), dtab),
        compiler_params=_params("parallel", "arbitrary"),
    )(q, kv, kv, delta, lse, do, bias)


def _adamw(w, g, m, v, name):
    R, C = w.shape
    tr = _tile(R, 256, SUBLANES_F32) if R % SUBLANES_F32 == 0 else R
    tc = _tile(C, 2048, LANES) if C % LANES == 0 else C

    def body(w_ref, g_ref, m_ref, v_ref, d_ref, nm_ref, nv_ref):
        g_ = g_ref[...]
        m2 = ADAM_B1 * m_ref[...] + (1.0 - ADAM_B1) * g_
        v2 = ADAM_B2 * v_ref[...] + (1.0 - ADAM_B2) * (g_ * g_)
        m_hat = m2 / (1.0 - ADAM_B1 ** ADAM_STEP)
        v_hat = v2 / (1.0 - ADAM_B2 ** ADAM_STEP)
        d_ref[...] = -ADAM_LR * (m_hat / (jnp.sqrt(v_hat) + ADAM_EPS) + ADAM_WD * w_ref[...])
        nm_ref[...] = m2
        nv_ref[...] = v2

    blk = pl.BlockSpec((tr, tc), lambda i, j: (i, j))
    shp = jax.ShapeDtypeStruct((R, C), F32)
    return pl.pallas_call(
        body, name=name, out_shape=(shp, shp, shp), grid=(R // tr, C // tc),
        in_specs=[blk] * 4, out_specs=(blk,) * 3, compiler_params=_params("parallel", "parallel"),
    )(w, g, m, v)


def _sum_slots(slots, name):
    n, R, C = slots.shape
    tr = _tile(R, 256, SUBLANES_BF16) if R % SUBLANES_BF16 == 0 else R
    tc = _tile(C, 2048, LANES) if C % LANES == 0 else C

    def body(s_ref, o_ref):
        acc = s_ref[0].astype(F32)
        for k in range(1, n):
            acc = acc + s_ref[k].astype(F32)
        o_ref[...] = acc

    return pl.pallas_call(
        body, name=name, out_shape=jax.ShapeDtypeStruct((R, C), F32), grid=(R // tr, C // tc),
        in_specs=[pl.BlockSpec((n, tr, tc), lambda i, j: (0, i, j))],
        out_specs=pl.BlockSpec((tr, tc), lambda i, j: (i, j)),
        compiler_params=_params("parallel", "parallel"),
    )(slots)


def _my_place():
    return lax.axis_index("x"), lax.axis_index("y"), lax.axis_index("c")


def _other_chips(x, y):
    return [(1 - x, y), (x, 1 - y), (1 - x, 1 - y)]


def _piece(ref, blk, axis, shard, half):
    h0 = blk[0] // 2
    idx = []
    for dim, n in enumerate(blk):
        if dim == 0:
            start = half * h0 + (shard * n if axis == 0 else 0)
            idx.append(pl.ds(start, h0))
        elif dim == axis:
            idx.append(pl.ds(shard * n, n))
        else:
            idx.append(slice(None))
    return ref.at[tuple(idx)]


def _half_of(ref, blk, half):
    return ref.at[pl.ds(half * (blk[0] // 2), blk[0] // 2)]


def _gather_weights(shards, axes, names, small):
    n = len(shards)
    blks = [_shard_shape(s) for s in shards]
    task = _gather_ici_task(shards, axes, names)

    def stage1(*refs):
        small_in, outs, small_out = refs[n], refs[n + 1:2 * n + 1], refs[2 * n + 1]
        send_sems, recv_sems, local_sems = refs[2 * n + 2:]
        sem = lambda k: (send_sems.at[k], recv_sems.at[k])
        x, y, c = _my_place()
        me = 2 * x + y
        local = pltpu.make_async_copy(small_in, small_out.at[me], local_sems.at[0])
        local.start()
        sends, recvs = task.copies(None, outs, sem)
        for k, (px, py) in enumerate(_other_chips(x, y)):
            sends.append(_remote(small_in, small_out.at[me], sem(task.n_sems + k), (px, py, c)))
            recvs.append(_remote(small_in, small_out.at[2 * px + py], sem(task.n_sems + k), (px, py, c)))
        for cp in sends:
            cp.start()
        for cp in recvs:
            cp.wait_recv()
        for cp in sends:
            cp.wait_send()
        local.wait()

    res = pl.pallas_call(
        stage1, name="gather_weights_ici",
        out_shape=task.outs + [jax.ShapeDtypeStruct((N_SHARDS,) + small.shape, small.dtype)],
        in_specs=[ANY] * (n + 1), out_specs=[ANY] * (n + 1), input_output_aliases={a: a for a in range(n)},
        scratch_shapes=[pltpu.SemaphoreType.DMA((task.n_sems + 3,)), pltpu.SemaphoreType.DMA((task.n_sems + 3,)),
                        pltpu.SemaphoreType.DMA((1,))],
    )(*task.ins, small)
    full = _run_comm(_gather_d2d_task(list(res[:n]), blks, axes), "gather_weights_d2d")
    return full, res[n]


def _run_comm(comm, name):
    n_in, n_out = len(comm.ins), len(comm.outs)

    def body(*refs):
        send_sems, recv_sems = refs[n_in + n_out:]
        sends, recvs = comm.copies(refs[:n_in], refs[n_in:n_in + n_out], lambda k: (send_sems.at[k], recv_sems.at[k]))
        for cp in sends:
            cp.start()
        for cp in recvs:
            cp.wait_recv()
        for cp in sends:
            cp.wait_send()

    return list(pl.pallas_call(
        body, name=name, out_shape=comm.outs, in_specs=[ANY] * n_in, out_specs=[ANY] * n_out,
        input_output_aliases=comm.aliases, scratch_shapes=[pltpu.SemaphoreType.DMA((comm.n_sems,))] * 2,
    )(*comm.ins))


def _exchange_small(small, extra, name):
    n_in, n_out = len(extra.ins), len(extra.outs)

    def body(*refs):
        ex_in, small_in = refs[:n_in], refs[n_in]
        ex_out, small_out = refs[n_in + 1:n_in + 1 + n_out], refs[n_in + 1 + n_out]
        send_sems, recv_sems, local_sems = refs[n_in + n_out + 2:]
        sem = lambda k: (send_sems.at[k], recv_sems.at[k])
        x, y, c = _my_place()
        me = 4 * x + 2 * y + c
        local = pltpu.make_async_copy(small_in, small_out.at[me], local_sems.at[0])
        local.start()
        sends, recvs = extra.copies(ex_in, ex_out, sem)
        for rel in range(1, N_DEVICES):
            px, py, pc = x ^ ((rel >> 2) & 1), y ^ ((rel >> 1) & 1), c ^ (rel & 1)
            k = extra.n_sems + rel - 1
            sends.append(_remote(small_in, small_out.at[me], sem(k), (px, py, pc)))
            recvs.append(_remote(small_in, small_out.at[4 * px + 2 * py + pc], sem(k), (px, py, pc)))
        for cp in sends:
            cp.start()
        for cp in recvs:
            cp.wait_recv()
        for cp in sends:
            cp.wait_send()
        local.wait()

    n_sems = extra.n_sems + N_DEVICES - 1
    res = pl.pallas_call(
        body, name=name, out_shape=extra.outs + [jax.ShapeDtypeStruct((N_DEVICES,) + small.shape, small.dtype)],
        in_specs=[ANY] * (n_in + 1), out_specs=[ANY] * (n_out + 1), input_output_aliases=extra.aliases,
        scratch_shapes=[pltpu.SemaphoreType.DMA((n_sems,)), pltpu.SemaphoreType.DMA((n_sems,)),
                        pltpu.SemaphoreType.DMA((1,))],
    )(*extra.ins, small)
    return list(res[:n_out]), res[n_out]


def _remote(src, dst, sems, device):
    return pltpu.make_async_remote_copy(src_ref=src, dst_ref=dst, send_sem=sems[0], recv_sem=sems[1],
                                        device_id=device, device_id_type=pl.DeviceIdType.MESH)


def _sum_piece(dest, slots, grad, blk, axis, layer, n_layers, name):
    r, c = blk
    h0 = r // 2
    tr = _tile(h0, 256, SUBLANES_BF16)
    tc = _tile(c, 2048, LANES)
    place = jnp.stack([2 * lax.axis_index("x") + lax.axis_index("y"), lax.axis_index("c")]).astype(jnp.int32)

    def body(p_ref, s_ref, g_ref, *rest):
        acc = g_ref[...].astype(F32) + s_ref[0].astype(F32)
        for k in range(1, N_DEVICES - 1):
            acc = acc + s_ref[k].astype(F32)
        rest[-1][...] = acc

    def g_map(i, j, p):
        return (p[1] * (h0 // tr) + (p[0] * (r // tr) if axis == 0 else 0) + i, (p[0] * (c // tc) if axis == 1 else 0) + j)

    in_specs = [pl.BlockSpec((N_DEVICES - 1, tr, tc), lambda i, j, p: (0, i, j)), pl.BlockSpec((tr, tc), g_map)]
    args = [place, slots, grad]
    if dest is not None:
        in_specs.append(ANY)
        args.append(dest)
    return pl.pallas_call(
        body, name=name, out_shape=jax.ShapeDtypeStruct((n_layers, r, c), F32),
        grid_spec=pltpu.PrefetchScalarGridSpec(
            num_scalar_prefetch=1, grid=(h0 // tr, c // tc), in_specs=in_specs,
            out_specs=pl.BlockSpec((None, tr, tc), lambda i, j, p: (layer, p[1] * (h0 // tr) + i, j))),
        input_output_aliases={3: 0} if dest is not None else {},
        compiler_params=_params("parallel", "parallel"),
    )(*args)


def _swap_halves_task(blocks):
    n = len(blocks)
    layers = [(a, l) for a, b in enumerate(blocks) for l in range(b.shape[0])]

    def copies(in_refs, out_refs, sem):
        x, y, c = _my_place()
        sends, recvs = [], []
        for k, (a, l) in enumerate(layers):
            blk = blocks[a].shape[1:]
            mine = _half_of(out_refs[a].at[l], blk, c)
            sends.append(_remote(mine, mine, sem(k), (x, y, 1 - c)))
            recvs.append(_remote(mine, _half_of(out_refs[a].at[l], blk, 1 - c), sem(k), (x, y, 1 - c)))
        return sends, recvs

    return _Comm(blocks, [jax.ShapeDtypeStruct(b.shape, b.dtype) for b in blocks], {a: a for a in range(n)},
                 len(layers), copies)


def _shard_shape(shard):
    return shard[0].shape[1:] if isinstance(shard, tuple) else shard.shape


def _place_shard(shard, axis, name):
    r, c = _shard_shape(shard)
    tr = _tile(r, 512, SUBLANES_BF16)
    full = (r * N_SHARDS, c) if axis == 0 else (r, c * N_SHARDS)
    me2 = (2 * lax.axis_index("x") + lax.axis_index("y")).astype(jnp.int32).reshape(1)

    def body(me_ref, s_ref, o_ref):
        o_ref[...] = s_ref[...].astype(o_ref.dtype)

    if axis == 0:
        out_map = lambda i, me: (me[0] * (r // tr) + i, 0)
    else:
        out_map = lambda i, me: (i, me[0])
    if isinstance(shard, tuple):
        src, layer = shard
        in_spec = pl.BlockSpec((None, tr, c), lambda i, me: (layer, i, 0))
    else:
        src, in_spec = shard, pl.BlockSpec((tr, c), lambda i, me: (i, 0))
    return pl.pallas_call(
        body, name=name, out_shape=jax.ShapeDtypeStruct(full, BF16),
        grid_spec=pltpu.PrefetchScalarGridSpec(
            num_scalar_prefetch=1, grid=(r // tr,), in_specs=[in_spec], out_specs=pl.BlockSpec((tr, c), out_map)),
        compiler_params=_params("parallel"),
    )(me2, src)


def _gather_ici_task(shards, axes, names):
    n = len(shards)
    blks = [_shard_shape(s) for s in shards]
    bases = [_place_shard(s, ax, f"place_{nm}") for s, ax, nm in zip(shards, axes, names)]

    def copies(in_refs, out_refs, sem):
        x, y, c = _my_place()
        me = 2 * x + y
        sends, recvs = [], []
        for a in range(n):
            mine = _piece(out_refs[a], blks[a], axes[a], me, c)
            for k, (px, py) in enumerate(_other_chips(x, y)):
                sends.append(_remote(mine, mine, sem(3 * a + k), (px, py, c)))
                recvs.append(_remote(mine, _piece(out_refs[a], blks[a], axes[a], 2 * px + py, c), sem(3 * a + k),
                                     (px, py, c)))
        return sends, recvs

    return _Comm(bases, [jax.ShapeDtypeStruct(b.shape, b.dtype) for b in bases], {a: a for a in range(n)}, 3 * n, copies)


def _gather_d2d_task(partials, blks, axes):
    n = len(partials)

    def copies(in_refs, out_refs, sem):
        x, y, c = _my_place()
        sends, recvs = [], []
        for a in range(n):
            for k, (px, py) in enumerate(_other_chips(x, y)):
                mine = _piece(out_refs[a], blks[a], axes[a], 2 * px + py, c)
                theirs = _piece(out_refs[a], blks[a], axes[a], 2 * px + py, 1 - c)
                sends.append(_remote(mine, mine, sem(3 * a + k), (x, y, 1 - c)))
                recvs.append(_remote(mine, theirs, sem(3 * a + k), (x, y, 1 - c)))
        return sends, recvs

    return _Comm(partials, [jax.ShapeDtypeStruct(p.shape, p.dtype) for p in partials], {a: a for a in range(n)},
                 3 * n, copies)


def _scatter_task(grads, blks, axes):
    n = len(grads)

    def copies(in_refs, out_refs, sem):
        x, y, c = _my_place()
        sends, recvs = [], []
        for rel in range(1, N_DEVICES):
            px, py, pc = x ^ ((rel >> 2) & 1), y ^ ((rel >> 1) & 1), c ^ (rel & 1)
            for a in range(n):
                src = _piece(in_refs[a], blks[a], axes[a], 2 * px + py, pc)
                k = (N_DEVICES - 1) * a + rel - 1
                sends.append(_remote(src, out_refs[a].at[rel - 1], sem(k), (px, py, pc)))
                recvs.append(_remote(src, out_refs[a].at[rel - 1], sem(k), (px, py, pc)))
        return sends, recvs

    outs = [jax.ShapeDtypeStruct((N_DEVICES - 1, b[0] // 2) + tuple(b[1:]), g.dtype) for b, g in zip(blks, grads)]
    return _Comm(grads, outs, {}, (N_DEVICES - 1) * n, copies)


def _pack(arrays):
    flat = jnp.concatenate([a.reshape(-1).astype(F32) for a in arrays])
    pad = (-flat.shape[0]) % (SUBLANES_F32 * LANES)
    return jnp.pad(flat, (0, pad)).reshape(-1, LANES)


def _unpack(packed, shapes):
    flat = packed.reshape(-1)
    out, off = [], 0
    for s in shapes:
        n = int(np.prod(s))
        out.append(flat[off:off + n].reshape(s))
        off += n
    return out


def _local_step(x, target, W, shards, geom, small):
    W = dict(W)
    B, S, D = x.shape
    T = B * S
    x2 = x.reshape(T, D)
    tgt = target.reshape(T, D)
    bmap = jnp.asarray(_bucket_map())
    blk = lambda names: [geom[k][0] for k in names]
    axs = lambda names: [geom[k][1] for k in names]
    ici = lambda names: _gather_ici_task([shards[k] for k in names], axs(names), names)
    d2d = lambda names, partials: _gather_d2d_task(list(partials), blk(names), axs(names))
    big, slots = {}, {}

    def scatter(names):
        return _scatter_task([big[k] for k in names], blk(names), axs(names))

    def ffn_bwd(l, dout, h, saved, first, second):
        xn, u, v, a = saved
        da = _mm_nt(dout, W[f"w_down{l}"], BF16, f"ffn{l}_down_dx")
        big[f"w_down{l}"] = _mm_tn(a, dout, f"ffn{l}_down_dw")
        names = first + [f"w_down{l}"]
        du, g_cw, g_cb, landed = _ffn_gate_bwd(u, v, small["ffn_conv"][l], da, S, f"ffn{l}_gate_bwd", comm=scatter(names))
        slots.update(zip(names, landed))
        big[f"w_up{l}"] = _mm_tn(xn, du, f"ffn{l}_up_dw")
        norm = (h, small["ffn_norm"][l], dout)
        if second:
            (dh, g_norm), (slots[f"w_up{l}"],) = _mm_nt(du, W[f"w_up{l}"], None, f"ffn{l}_up_dx",
                                                        comm=scatter([f"w_up{l}"]), norm=norm)
        else:
            dh, g_norm = _mm_nt(du, W[f"w_up{l}"], None, f"ffn{l}_up_dx", norm=norm)
        return dh, g_cw, g_cb, g_norm[0]

    xn0 = _rmsnorm_fwd(x2, small["a_norm"][0], "a_norm")
    p, part = _mm_nn(xn0, W["w_in"], None, BF16, "a_in", comm=ici(["w_up0"]))
    z, (W["w_up0"],) = _gate_a_fwd(p, small["a_conv"][0], S, "a_gate", comm=d2d(["w_up0"], part))
    (h1, xn1), part = _mm_nn(z, W["w_out"], x2, F32, "a_out", comm=ici(["w_down0"]), norm_gains=[small["ffn_norm"][0]])
    attn, ffn1 = ["w_kv", "w_q", "w_o"], ["w_up1", "w_down1"]
    u0, landed = _mm_nn(xn1, W["w_up0"], None, BF16, "ffn0_up", comm=_Comm.join([d2d(["w_down0"], part), ici(attn)]))
    W["w_down0"] = landed[0]
    (a0, v0), landed = _ffn_gate_fwd(u0, small["ffn_conv"][0], small["ffn_conv_b"][0], S, "ffn0_gate",
                                     comm=_Comm.join([d2d(attn, landed[1:]), ici(ffn1)]))
    W.update(zip(attn, landed[:len(attn)]))
    (h2, kvn, xn3), landed = _mm_nn(a0, W["w_down0"], h1, F32, "ffn0_down", comm=d2d(ffn1, landed[len(attn):]),
                                    norm_gains=[small["kv_norm"], small["b_norm"][0]])
    W.update(zip(ffn1, landed))
    kv = _mm_nn(kvn, W["w_kv"], None, F32, "kv_proj")
    q = _mm_nn(xn3, W["w_q"], None, F32, "q_proj")
    bias = _bias_tables(small["rel_bias"], bmap, "rel_bias_tables")
    q3, kv3 = q.reshape(B, S, D), kv.reshape(B, S, 2 * D)
    o3, lse3 = _attn_fwd(q3, kv3, bias, "attn_fwd")
    o = o3.reshape(T, D)
    h3, xn4 = _mm_nn(o, W["w_o"], h2, F32, "o_proj", norm_gains=[small["ffn_norm"][1]])
    u1 = _mm_nn(xn4, W["w_up1"], None, BF16, "ffn1_up")
    a1, v1 = _ffn_gate_fwd(u1, small["ffn_conv"][1], small["ffn_conv_b"][1], S, "ffn1_gate")
    sq_err, dh4, g_final = _loss_head(a1, W["w_down1"], h3, small["final_norm"], tgt, "ffn1_down_loss")
    loss = 0.5 * jnp.sum(sq_err) / D

    dh3, g_cw1, g_cb1, g_fn1 = ffn_bwd(1, dh4, h3, (xn4, u1, v1, a1), [], False)
    do, delta = _mm_nt(dh3, W["w_o"], F32, "o_proj_dx", head_dot=o)
    big["w_o"] = _mm_tn(o, dh3, "o_proj_dw")
    (dq3, dk3, dv3, dbias), landed = _attn_bwd(q3, kv3, delta.reshape(B, S, D), lse3, do.reshape(B, S, D), bias, "attn_bwd",
                                               comm=scatter(["w_up1", "w_o"]))
    slots.update(zip(["w_up1", "w_o"], landed))
    g_rel = _bias_grad(dbias, bmap, "rel_bias_grad")[:, :REL_BUCKETS].T
    dq, dk, dv = dq3.reshape(T, D), dk3.reshape(T, D), dv3.reshape(T, D)
    dh2, g_bn = _mm_nt(dq, W["w_q"], None, "q_proj_dx", norm=(h2, small["b_norm"][0], dh3))
    big["w_q"] = _mm_tn(xn3, dq, "q_proj_dw")
    dh2, g_kvn = _mm_nt([dk, dv], W["w_kv"], None, "kv_proj_dx", norm=(h2, small["kv_norm"], dh2))
    big["w_kv"] = jnp.concatenate([_mm_tn(kvn, dk, "k_proj_dw"), _mm_tn(kvn, dv, "v_proj_dw")], axis=1)
    dh1, g_cw0, g_cb0, g_fn0 = ffn_bwd(0, dh2, h1, (xn1, u0, v0, a0), ["w_q", "w_kv"], True)
    dz = _mm_nt(dh1, W["w_out"], BF16, "a_out_dx")
    big["w_out"] = _mm_tn(z, dh1, "a_out_dw")
    dp, g_aconv, (slots["w_out"],) = _gate_a_bwd(p, small["a_conv"][0], dz, S, "a_gate_bwd", comm=scatter(["w_out"]))
    big["w_in"] = _mm_tn(xn0, dp, "a_in_dw")
    (dx, g_an), (slots["w_in"],) = _mm_nt(dp, W["w_in"], None, "a_in_dx", comm=scatter(["w_in"]),
                                          norm=(x2, small["a_norm"][0], dh1))
    g_bn, g_kvn, g_an = g_bn[0], g_kvn[0], g_an[0]

    small_g = {"a_norm": g_an[None], "a_conv": g_aconv[None], "kv_norm": g_kvn, "b_norm": g_bn[None],
               "rel_bias": g_rel, "ffn_norm": jnp.stack([g_fn0, g_fn1]), "ffn_conv": jnp.stack([g_cw0, g_cw1]),
               "ffn_conv_b": jnp.stack([g_cb0, g_cb1]), "final_norm": g_final}
    return loss, dx.reshape(B, S, D), big, slots, small_g


BIG = ("w_in", "w_out", "w_kv", "w_q", "w_o", "w_up0", "w_up1", "w_down0", "w_down1")
SMALL = ("a_norm", "a_conv", "kv_norm", "b_norm", "rel_bias", "ffn_norm", "ffn_conv", "ffn_conv_b", "final_norm")
SMALL_SHARDED = ("a_norm", "a_conv", "ffn_conv")
WEIGHT_ORDER = ("a_norm", "a_w_in", "a_conv", "a_w_out", "kv_norm", "w_kv", "b_norm", "b_w_q", "b_w_o", "rel_bias",
                "ffn_norm", "ffn_w_up", "ffn_conv", "ffn_conv_b", "ffn_w_down", "final_norm")
GRAD_OF = {"w_in": ("a_w_in", 0), "w_out": ("a_w_out", 0), "w_kv": ("w_kv", 0), "w_q": ("b_w_q", 0), "w_o": ("b_w_o", 0),
           "w_up0": ("ffn_w_up", 0), "w_up1": ("ffn_w_up", 1), "w_down0": ("ffn_w_down", 0), "w_down1": ("ffn_w_down", 1)}


def _as2d(a):
    return a.reshape(-1, a.shape[-1])


def kernel(x, a_norm, a_w_in, a_conv, a_w_out, kv_norm, w_kv, b_norm, b_w_q, b_w_o, rel_bias, ffn_norm, ffn_w_up, ffn_conv, ffn_conv_b, ffn_w_down, final_norm, loss_target, m_a_norm, m_a_w_in, m_a_conv, m_a_w_out, m_kv_norm, m_w_kv, m_b_norm, m_b_w_q, m_b_w_o, m_rel_bias, m_ffn_norm, m_ffn_w_up, m_ffn_conv, m_ffn_conv_b, m_ffn_w_down, m_final_norm, v_a_norm, v_a_w_in, v_a_conv, v_a_w_out, v_kv_norm, v_w_kv, v_b_norm, v_b_w_q, v_b_w_o, v_rel_bias, v_ffn_norm, v_ffn_w_up, v_ffn_conv, v_ffn_conv_b, v_ffn_w_down, v_final_norm):
    given = dict(a_norm=a_norm, a_w_in=a_w_in, a_conv=a_conv, a_w_out=a_w_out, kv_norm=kv_norm, w_kv=w_kv, b_norm=b_norm,
                 b_w_q=b_w_q, b_w_o=b_w_o, rel_bias=rel_bias, ffn_norm=ffn_norm, ffn_w_up=ffn_w_up, ffn_conv=ffn_conv,
                 ffn_conv_b=ffn_conv_b, ffn_w_down=ffn_w_down, final_norm=final_norm)
    mom_m = dict(a_norm=m_a_norm, a_w_in=m_a_w_in, a_conv=m_a_conv, a_w_out=m_a_w_out, kv_norm=m_kv_norm, w_kv=m_w_kv,
                 b_norm=m_b_norm, b_w_q=m_b_w_q, b_w_o=m_b_w_o, rel_bias=m_rel_bias, ffn_norm=m_ffn_norm,
                 ffn_w_up=m_ffn_w_up, ffn_conv=m_ffn_conv, ffn_conv_b=m_ffn_conv_b, ffn_w_down=m_ffn_w_down,
                 final_norm=m_final_norm)
    mom_v = dict(a_norm=v_a_norm, a_w_in=v_a_w_in, a_conv=v_a_conv, a_w_out=v_a_w_out, kv_norm=v_kv_norm, w_kv=v_w_kv,
                 b_norm=v_b_norm, b_w_q=v_b_w_q, b_w_o=v_b_w_o, rel_bias=v_rel_bias, ffn_norm=v_ffn_norm,
                 ffn_w_up=v_ffn_w_up, ffn_conv=v_ffn_conv, ffn_conv_b=v_ffn_conv_b, ffn_w_down=v_ffn_w_down,
                 final_norm=v_final_norm)

    shard = {"w_in": ((a_w_in, 0), 1), "w_out": ((a_w_out, 0), 0), "w_kv": (w_kv, 1), "w_q": ((b_w_q, 0), 0),
             "w_o": ((b_w_o, 0), 0), "w_up0": ((ffn_w_up, 0), 1), "w_up1": ((ffn_w_up, 1), 1),
             "w_down0": ((ffn_w_down, 0), 0), "w_down1": ((ffn_w_down, 1), 0)}

    small_sharded = [given[k] for k in SMALL_SHARDED]
    packed = _pack(small_sharded)
    first = ("w_in", "w_out")
    fulls, packed_all = _gather_weights([shard[k][0] for k in first], [shard[k][1] for k in first], first, packed)
    W = dict(zip(first, fulls))
    later = {k: shard[k][0] for k in BIG if k not in first}
    geom = {k: (_shard_shape(shard[k][0]), shard[k][1]) for k in BIG}
    small = {k: given[k] for k in SMALL}
    per_shard = [_unpack(packed_all[j], [a.shape for a in small_sharded]) for j in range(N_SHARDS)]
    for i, k in enumerate(SMALL_SHARDED):
        small[k] = jnp.concatenate([per_shard[j][i] for j in range(N_SHARDS)], axis=-1)

    loss, grad_x, big_g, slots, small_g = _local_step(x, loss_target, W, later, geom, small)

    small_shapes = [small_g[k].shape for k in SMALL] + [(1,)]
    layers_of = {}
    for k in BIG:
        layers_of.setdefault(GRAD_OF[k][0], []).append(k)
    blocks = {}
    for name, members in layers_of.items():
        dest = None
        for k in members:
            dest = _sum_piece(dest, slots[k], big_g[k], geom[k][0], geom[k][1], GRAD_OF[k][1], len(members), f"sum_{k}")
        blocks[name] = dest
    swapped, small_slots = _exchange_small(_pack([small_g[k] for k in SMALL] + [loss.reshape(1)]),
                                           _swap_halves_task(list(blocks.values())), "swap_halves_exchange_small")
    reduced = dict(zip(blocks, swapped))
    *small_sums, loss = _unpack(_sum_slots(small_slots, "sum_small"), small_shapes)
    loss = loss[0]
    small_red = dict(zip(SMALL, small_sums))
    j = 2 * lax.axis_index("x") + lax.axis_index("y")
    for k in SMALL_SHARDED:
        w = given[k].shape[-1]
        small_red[k] = lax.dynamic_slice_in_dim(small_red[k], j * w, w, axis=small_red[k].ndim - 1)

    grads, deltas, new_m, new_v = {}, {}, {}, {}
    for name in WEIGHT_ORDER:
        if name in reduced:
            g = reduced[name].reshape(given[name].shape)
            d, nm, nv = _adamw(_as2d(given[name]), _as2d(g), _as2d(mom_m[name]), _as2d(mom_v[name]), f"adamw_{name}")
            grads[name] = g
            deltas[name], new_m[name], new_v[name] = (t.reshape(given[name].shape) for t in (d, nm, nv))
    small_names = [n for n in WEIGHT_ORDER if n not in reduced]
    for n in small_names:
        grads[n] = small_red[n].reshape(given[n].shape)
    sw, sg, sm, sv = (_pack([d[n] for n in small_names]) for d in (given, grads, mom_m, mom_v))
    d, nm, nv = _adamw(sw, sg, sm, sv, "adamw_small")
    shapes = [given[n].shape for n in small_names]
    for n, a, b_, c_ in zip(small_names, _unpack(d, shapes), _unpack(nm, shapes), _unpack(nv, shapes)):
        deltas[n], new_m[n], new_v[n] = a, b_, c_

    return (loss, grad_x, *[grads[n] for n in WEIGHT_ORDER], *[deltas[n] for n in WEIGHT_ORDER],
            *[new_m[n] for n in WEIGHT_ORDER], *[new_v[n] for n in WEIGHT_ORDER])
```

```python
import functools
import math

import numpy as np

import jax
import jax.numpy as jnp
from jax import lax
from jax.experimental import pallas as pl
from jax.experimental.pallas import tpu as pltpu

F32 = jnp.float32
BF16 = jnp.bfloat16

RMS_EPS = 1e-6
HEAD_DIM = 64
ATT_BLOCK = 128
DILATED_BRANCHES = ((128, 1), (512, 4), (2048, 16))
REL_BUCKETS = 32
REL_MAX_DISTANCE = 2048
MASKED_LOGIT = -1e30
ATTN_FWD_UNROLL = (8, 8, 4)
ATTN_BWD_UNROLL = (8, 8, 8)

ADAM_LR = 0.001
ADAM_B1 = 0.9
ADAM_B2 = 0.999
ADAM_EPS = 1e-08
ADAM_WD = 0.01
ADAM_STEP = 10

LANES = 128
SUBLANES_F32 = 8
SUBLANES_BF16 = 16
VMEM_LIMIT_BYTES = 56 * 1024 * 1024

N_SHARDS = 4
N_DEVICES = 8
ANY = pl.BlockSpec(memory_space=pl.ANY)


def _tile(n, pref, mult):
    best = None
    for t in range(mult, min(n, pref) + 1, mult):
        if n % t == 0:
            best = t
    if best is None:
        raise ValueError(f"no tile for {n} (multiple of {mult}, at most {pref})")
    return best


def _params(*sem):
    return pltpu.CompilerParams(dimension_semantics=sem, vmem_limit_bytes=VMEM_LIMIT_BYTES)


class _Comm:
    def __init__(self, ins, outs, aliases, n_sems, copies):
        self.ins, self.outs, self.aliases, self.n_sems, self.copies = list(ins), list(outs), dict(aliases), n_sems, copies

    @staticmethod
    def join(parts):
        parts = [p for p in parts if p is not None]
        ins, outs, aliases, offs, n_sems = [], [], {}, [], 0
        for p in parts:
            offs.append((len(ins), len(outs), n_sems))
            aliases.update({len(ins) + i: len(outs) + o for i, o in p.aliases.items()})
            ins += p.ins
            outs += p.outs
            n_sems += p.n_sems

        def copies(in_refs, out_refs, sem):
            sends, recvs = [], []
            for p, (i0, o0, s0) in zip(parts, offs):
                s, r = p.copies(in_refs[i0:i0 + len(p.ins)], out_refs[o0:o0 + len(p.outs)],
                                lambda k, s0=s0: sem(s0 + k))
                sends += s
                recvs += r
            return sends, recvs

        return _Comm(ins, outs, aliases, n_sems, copies)


def _pallas(body, *, comm=None, name, out_shape, grid=(), in_specs=(), out_specs=(), scratch_shapes=(),
            compiler_params=None):
    if comm is None:
        return pl.pallas_call(body, name=name, out_shape=out_shape, grid=grid, in_specs=in_specs, out_specs=out_specs,
                              scratch_shapes=scratch_shapes, compiler_params=compiler_params)
    single = not isinstance(out_shape, (tuple, list))
    outs = (out_shape,) if single else tuple(out_shape)
    o_specs = (out_specs,) if single else tuple(out_specs)
    n_in, n_cin, n_out, n_cout, n_scr = len(in_specs), len(comm.ins), len(outs), len(comm.outs), len(scratch_shapes)

    def carried(*refs):
        base_in, c_in = refs[:n_in], refs[n_in:n_in + n_cin]
        o0 = n_in + n_cin
        base_out, c_out = refs[o0:o0 + n_out], refs[o0 + n_out:o0 + n_out + n_cout]
        s0 = o0 + n_out + n_cout
        base_scr, (send_sems, recv_sems) = refs[s0:s0 + n_scr], refs[s0 + n_scr:]
        sem = lambda k: (send_sems.at[k], recv_sems.at[k])
        first = functools.reduce(jnp.logical_and, [pl.program_id(a) == 0 for a in range(len(grid))])
        last = functools.reduce(jnp.logical_and, [pl.program_id(a) == n - 1 for a, n in enumerate(grid)])

        @pl.when(first)
        def _():
            for cp in comm.copies(c_in, c_out, sem)[0]:
                cp.start()

        body(*base_in, *base_out, *base_scr)

        @pl.when(last)
        def _():
            sends, recvs = comm.copies(c_in, c_out, sem)
            for cp in recvs:
                cp.wait_recv()
            for cp in sends:
                cp.wait_send()

    call = pl.pallas_call(
        carried, name=name, out_shape=outs + tuple(comm.outs), grid=grid,
        in_specs=list(in_specs) + [ANY] * n_cin, out_specs=o_specs + (ANY,) * n_cout,
        scratch_shapes=list(scratch_shapes) + [pltpu.SemaphoreType.DMA((comm.n_sems,))] * 2,
        input_output_aliases={n_in + i: n_out + o for i, o in comm.aliases.items()},
        compiler_params=_params(*(["arbitrary"] * len(grid))))

    def run(*args):
        res = call(*args, *comm.ins)
        base = res[0] if single else tuple(res[:n_out])
        return base, list(res[n_out:])

    return run


def _rmsnorm_fwd(x, g, name):
    T, D = x.shape
    tm = _tile(T, 512, SUBLANES_BF16)

    def body(x_ref, g_ref, o_ref):
        xf = x_ref[...]
        r = lax.rsqrt(jnp.mean(xf * xf, axis=-1, keepdims=True) + RMS_EPS)
        o_ref[...] = ((xf * r) * g_ref[...]).astype(o_ref.dtype)

    return pl.pallas_call(
        body, name=name, out_shape=jax.ShapeDtypeStruct((T, D), BF16), grid=(T // tm,),
        in_specs=[pl.BlockSpec((tm, D), lambda i: (i, 0)), pl.BlockSpec((1, D), lambda i: (0, 0))],
        out_specs=pl.BlockSpec((tm, D), lambda i: (i, 0)),
        compiler_params=_params("parallel"),
    )(x, g.reshape(1, D))


def _loss_head(a, b, res, g, target, name):
    T, K = a.shape
    D = b.shape[1]
    tm = _tile(T, 512, SUBLANES_BF16)
    total = T // tm
    slots = 3

    def body(a_hbm, b_ref, res_ref, g_ref, t_ref, dh_ref, acc_ref, a_buf, a_sem):
        step = pl.program_id(0)

        def fetch(s):
            start = s * tm
            rows = pl.ds(start if isinstance(s, int) else pl.multiple_of(start, tm), tm)
            return pltpu.make_async_copy(a_hbm.at[rows, :], a_buf.at[s % slots], a_sem.at[s % slots])

        @pl.when(step == 0)
        def _():
            acc_ref[...] = jnp.zeros_like(acc_ref)
            for s in range(min(slots - 1, total)):
                fetch(s).start()

        @pl.when(step + slots - 1 < total)
        def _():
            fetch(step + slots - 1).start()

        fetch(step).wait()
        xf = jnp.dot(a_buf[step % slots].astype(BF16), b_ref[...], preferred_element_type=F32) + res_ref[...]
        r = lax.rsqrt(jnp.mean(xf * xf, axis=-1, keepdims=True) + RMS_EPS)
        xhat = xf * r
        err = xhat * g_ref[...] - t_ref[...]
        dy = err * (1.0 / D)
        acc_ref[0:1, :] += jnp.sum(dy * xhat, axis=0, keepdims=True)
        acc_ref[1:2, :] += jnp.sum(err * err, axis=0, keepdims=True)
        t = dy * g_ref[...]
        dh_ref[...] = r * (t - xhat * jnp.mean(t * xhat, axis=-1, keepdims=True))

    row = pl.BlockSpec((tm, D), lambda i: (i, 0))
    dh, acc = pl.pallas_call(
        body, name=name,
        out_shape=(jax.ShapeDtypeStruct((T, D), F32), jax.ShapeDtypeStruct((SUBLANES_F32, D), F32)),
        grid=(T // tm,),
        in_specs=[ANY, pl.BlockSpec((K, D), lambda i: (0, 0)), row, pl.BlockSpec((1, D), lambda i: (0, 0)), row],
        out_specs=(row, pl.BlockSpec((SUBLANES_F32, D), lambda i: (0, 0))),
        scratch_shapes=[pltpu.VMEM((slots, tm, K), a.dtype), pltpu.SemaphoreType.DMA((slots,))],
        compiler_params=_params("arbitrary"),
    )(a, b, res, g.reshape(1, D), target)
    return acc[1], dh, acc[0]


def _mm_nn(a, b, res, out_dtype, name, comm=None, norm_gains=()):
    T, K = a.shape
    N = b.shape[1]
    n_g = len(norm_gains)
    tm = _tile(T, 512 if n_g else 1024, SUBLANES_BF16)
    tn = _tile(N, 3072, LANES)
    assert not n_g or tn == N, "the fused rmsnorm needs whole rows in one tile"
    n_i = T // tm
    total = (N // tn) * n_i
    slots = 3

    def body(a_hbm, b_ref, *rest):
        (a_buf, a_sem), rest = rest[-2:], rest[:-2]
        step = pl.program_id(0) * n_i + pl.program_id(1)

        def fetch(s):
            start = (s % n_i) * tm
            rows = pl.ds(start if isinstance(s, int) else pl.multiple_of(start, tm), tm)
            return pltpu.make_async_copy(a_hbm.at[rows, :], a_buf.at[s % slots], a_sem.at[s % slots])

        @pl.when(step == 0)
        def _():
            for s in range(min(slots - 1, total)):
                fetch(s).start()

        @pl.when(step + slots - 1 < total)
        def _():
            fetch(step + slots - 1).start()

        fetch(step).wait()
        ins, outs = rest[:len(rest) - 1 - n_g], rest[len(rest) - 1 - n_g:]
        acc = jnp.dot(a_buf[step % slots].astype(BF16), b_ref[...], preferred_element_type=F32)
        if res is not None:
            acc = acc + ins[0][...]
        outs[0][...] = acc.astype(outs[0].dtype)
        if n_g:
            y = acc * lax.rsqrt(jnp.mean(acc * acc, axis=-1, keepdims=True) + RMS_EPS)
            for g_ref, xn_ref in zip(ins[len(ins) - n_g:], outs[1:]):
                xn_ref[...] = (y * g_ref[...]).astype(xn_ref.dtype)

    tile = pl.BlockSpec((tm, tn), lambda j, i: (i, j))
    in_specs = [ANY, pl.BlockSpec((K, tn), lambda j, i: (0, j))]
    args = [a, b]
    if res is not None:
        in_specs.append(tile)
        args.append(res)
    for g in norm_gains:
        in_specs.append(pl.BlockSpec((1, N), lambda j, i: (0, 0)))
        args.append(g.reshape(1, N))
    out_shape = jax.ShapeDtypeStruct((T, N), out_dtype)
    if n_g:
        out_shape = (out_shape,) + (jax.ShapeDtypeStruct((T, N), BF16),) * n_g
    return _pallas(
        body, comm=comm, name=name, out_shape=out_shape, grid=(N // tn, n_i),
        in_specs=in_specs, out_specs=(tile,) * (1 + n_g) if n_g else tile,
        scratch_shapes=[pltpu.VMEM((slots, tm, K), a.dtype), pltpu.SemaphoreType.DMA((slots,))],
        compiler_params=_params("arbitrary", "arbitrary"),
    )(*args)


def _mm_nt(dy, b, out_dtype, name, comm=None, norm=None, head_dot=None):
    dys = list(dy) if isinstance(dy, (list, tuple)) else [dy]
    T, n_each = dys[0].shape
    K = b.shape[0]
    tm = _tile(T, 2048 if norm is None and head_dot is None else 1024, SUBLANES_BF16)
    tk = _tile(K, 1536, LANES)
    tn = _tile(n_each, 2816 if norm is None else 1536, LANES)
    per = n_each // tn
    n_steps = per * len(dys)
    assert norm is None or tk == K, "the fused rmsnorm backward needs whole rows in one tile"

    def body(*refs):
        dy_refs, b_ref, acc_ref = refs[:len(dys)], refs[len(dys)], refs[-1]
        i, n = pl.program_id(0), pl.program_id(2)

        @pl.when(n == 0)
        def _():
            acc_ref[...] = jnp.zeros_like(acc_ref)

        for p, dy_ref in enumerate(dy_refs):
            @pl.when(jnp.logical_and(n >= p * per, n < (p + 1) * per))
            def _(dy_ref=dy_ref):
                acc_ref[...] += lax.dot_general(dy_ref[...].astype(BF16), b_ref[...], (((1,), (1,)), ((), ())),
                                                preferred_element_type=F32)

        if norm is None and head_dot is None:
            @pl.when(n == n_steps - 1)
            def _():
                refs[-2][...] = acc_ref[...].astype(refs[-2].dtype)
        elif norm is None:
            o_ref, out_ref, dot_ref = refs[len(dys) + 1:-1]

            @pl.when(n == n_steps - 1)
            def _():
                d = acc_ref[...]
                out_ref[...] = d.astype(out_ref.dtype)
                head0 = lax.broadcasted_iota(jnp.int32, (tm, LANES), 1) < HEAD_DIM
                for c0 in range(0, tk, LANES):
                    x = d[:, c0:c0 + LANES] * o_ref[:, c0:c0 + LANES]
                    d0 = jnp.sum(jnp.where(head0, x, 0.0), axis=-1, keepdims=True)
                    d1 = jnp.sum(jnp.where(head0, 0.0, x), axis=-1, keepdims=True)
                    dot_ref[:, c0:c0 + LANES] = jnp.where(head0, d0, d1)
        else:
            x_ref, g_ref, dres_ref, dx_ref, dg_ref = refs[len(dys) + 1:-1]

            @pl.when(jnp.logical_and(i == 0, n == 0))
            def _():
                dg_ref[...] = jnp.zeros_like(dg_ref)

            @pl.when(n == n_steps - 1)
            def _():
                xf = x_ref[...]
                r = lax.rsqrt(jnp.mean(xf * xf, axis=-1, keepdims=True) + RMS_EPS)
                xhat = xf * r
                d = acc_ref[...]
                dg_ref[0:1, :] += jnp.sum(d * xhat, axis=0, keepdims=True)
                t = d * g_ref[...]
                dx_ref[...] = dres_ref[...] + r * (t - xhat * jnp.mean(t * xhat, axis=-1, keepdims=True))

    in_specs = [pl.BlockSpec((tm, tn), lambda i, k, n, p=p: (i, jnp.clip(n - p * per, 0, per - 1))) for p in range(len(dys))]
    in_specs.append(pl.BlockSpec((tk, tn), lambda i, k, n: (k, n)))
    args = dys + [b]
    tile = pl.BlockSpec((tm, tk), lambda i, k, n: (i, k))
    if norm is None and head_dot is None:
        out_shape, out_specs = jax.ShapeDtypeStruct((T, K), out_dtype), tile
    elif norm is None:
        in_specs.append(tile)
        args.append(head_dot)
        out_shape = (jax.ShapeDtypeStruct((T, K), out_dtype), jax.ShapeDtypeStruct((T, K), F32))
        out_specs = (tile, tile)
    else:
        x, g, dres = norm
        in_specs += [tile, pl.BlockSpec((1, K), lambda i, k, n: (0, 0)), tile]
        args += [x, g.reshape(1, K), dres]
        out_shape = (jax.ShapeDtypeStruct((T, K), F32), jax.ShapeDtypeStruct((SUBLANES_F32, K), F32))
        out_specs = (tile, pl.BlockSpec((SUBLANES_F32, K), lambda i, k, n: (0, 0)))
    return _pallas(
        body, comm=comm, name=name, out_shape=out_shape, grid=(T // tm, K // tk, n_steps),
        in_specs=in_specs, out_specs=out_specs, scratch_shapes=[pltpu.VMEM((tm, tk), F32)],
        compiler_params=_params("parallel", "parallel", "arbitrary") if norm is None else _params(*["arbitrary"] * 3),
    )(*args)


def _mm_tn(a, dy, name):
    T, K = a.shape
    N = dy.shape[1]
    tt = _tile(T, 2048, SUBLANES_BF16)
    tk = _tile(K, 1536, LANES)
    tn = _tile(N, 1536, LANES)
    t_steps = T // tt

    def body(a_ref, dy_ref, o_ref, acc_ref):
        t = pl.program_id(2)

        @pl.when(t == 0)
        def _():
            acc_ref[...] = jnp.zeros_like(acc_ref)

        acc_ref[...] += lax.dot_general(a_ref[...].astype(BF16), dy_ref[...].astype(BF16),
                                        (((0,), (0,)), ((), ())), preferred_element_type=F32)

        @pl.when(t == t_steps - 1)
        def _():
            o_ref[...] = acc_ref[...].astype(o_ref.dtype)

    return pl.pallas_call(
        body, name=name, out_shape=jax.ShapeDtypeStruct((K, N), BF16), grid=(K // tk, N // tn, t_steps),
        in_specs=[pl.BlockSpec((tt, tk), lambda k, n, t: (t, k)), pl.BlockSpec((tt, tn), lambda k, n, t: (t, n))],
        out_specs=pl.BlockSpec((tk, tn), lambda k, n, t: (k, n)),
        scratch_shapes=[pltpu.VMEM((tk, tn), F32)],
        compiler_params=_params("parallel", "parallel", "arbitrary"),
    )(a, dy)


def _rows_before(halo, cur, k):
    h = halo.shape[0]
    return pltpu.roll(jnp.concatenate([halo, cur], axis=0), k, 0)[h:]


def _rows_after(cur, halo, k):
    n = cur.shape[0]
    total = n + halo.shape[0]
    return pltpu.roll(jnp.concatenate([cur, halo], axis=0), total - k, 0)[:n]


def _halo_specs(tm, width, n_rows):
    per = tm // SUBLANES_BF16
    last = n_rows // SUBLANES_BF16 - 1
    prev = pl.BlockSpec((SUBLANES_BF16, width), lambda i: (jnp.maximum(i * per - 1, 0), 0))
    nxt = pl.BlockSpec((SUBLANES_BF16, width), lambda i: (jnp.minimum((i + 1) * per, last), 0))
    return prev, nxt


def _gate_a_fwd(p, cw, seq, name, comm=None):
    T, D3 = p.shape
    D = D3 // 3
    tm = _tile(seq, 512, SUBLANES_BF16)
    cc = _tile(D, 256, LANES)
    prev, _ = _halo_specs(tm, D3, T)

    def body(p_ref, ph_ref, cw_ref, z_ref):
        at_start = (pl.program_id(0) * tm) % seq == 0
        for c0 in range(0, D, cc):
            b = p_ref[:, c0:c0 + cc].astype(F32)
            u = p_ref[:, D + c0:D + c0 + cc].astype(F32) * p_ref[:, 2 * D + c0:2 * D + c0 + cc].astype(F32)
            uh = ph_ref[:, D + c0:D + c0 + cc].astype(F32) * ph_ref[:, 2 * D + c0:2 * D + c0 + cc].astype(F32)
            uh = jnp.where(at_start, 0.0, uh)
            w = cw_ref[:, c0:c0 + cc]
            cv = _rows_before(uh, u, 2) * w[0:1] + _rows_before(uh, u, 1) * w[1:2] + u * w[2:3]
            z_ref[:, c0:c0 + cc] = (b * cv).astype(z_ref.dtype)

    return _pallas(
        body, comm=comm, name=name, out_shape=jax.ShapeDtypeStruct((T, D), BF16), grid=(T // tm,),
        in_specs=[pl.BlockSpec((tm, D3), lambda i: (i, 0)), prev, pl.BlockSpec((3, D), lambda i: (0, 0))],
        out_specs=pl.BlockSpec((tm, D), lambda i: (i, 0)),
        compiler_params=_params("parallel"),
    )(p, p, cw)


def _gate_a_bwd(p, cw, dz, seq, name, comm=None):
    T, D3 = p.shape
    D = D3 // 3
    tm = _tile(seq, 512, SUBLANES_BF16)
    cc = _tile(D, 256, LANES)
    p_prev, p_next = _halo_specs(tm, D3, T)
    _, dz_next = _halo_specs(tm, D, T)

    def body(p_ref, pp_ref, pn_ref, dz_ref, dzn_ref, cw_ref, dp_ref, dcw_ref):
        i = pl.program_id(0)

        @pl.when(i == 0)
        def _():
            dcw_ref[...] = jnp.zeros_like(dcw_ref)

        at_start = (i * tm) % seq == 0
        at_end = ((i + 1) * tm) % seq == 0
        for c0 in range(0, D, cc):
            cb, cc_, ch = slice(c0, c0 + cc), slice(D + c0, D + c0 + cc), slice(2 * D + c0, 2 * D + c0 + cc)
            b = p_ref[:, cb].astype(F32)
            c = p_ref[:, cc_].astype(F32)
            hh = p_ref[:, ch].astype(F32)
            u = c * hh
            uh = jnp.where(at_start, 0.0, pp_ref[:, cc_].astype(F32) * pp_ref[:, ch].astype(F32))
            w = cw_ref[:, cb]
            u1 = _rows_before(uh, u, 1)
            u2 = _rows_before(uh, u, 2)
            cv = u2 * w[0:1] + u1 * w[1:2] + u * w[2:3]
            dz_t = dz_ref[:, cb].astype(F32)
            dcv = dz_t * b
            dcvn = jnp.where(at_end, 0.0, dzn_ref[:, cb].astype(F32) * pn_ref[:, cb].astype(F32))
            du = dcv * w[2:3] + _rows_after(dcv, dcvn, 1) * w[1:2] + _rows_after(dcv, dcvn, 2) * w[0:1]
            dp_ref[:, cb] = (dz_t * cv).astype(dp_ref.dtype)
            dp_ref[:, cc_] = (du * hh).astype(dp_ref.dtype)
            dp_ref[:, ch] = (du * c).astype(dp_ref.dtype)
            dcw_ref[0:1, cb] += jnp.sum(dcv * u2, axis=0, keepdims=True)
            dcw_ref[1:2, cb] += jnp.sum(dcv * u1, axis=0, keepdims=True)
            dcw_ref[2:3, cb] += jnp.sum(dcv * u, axis=0, keepdims=True)

    res = _pallas(
        body, comm=comm, name=name,
        out_shape=(jax.ShapeDtypeStruct((T, D3), BF16), jax.ShapeDtypeStruct((SUBLANES_F32, D), F32)),
        grid=(T // tm,),
        in_specs=[pl.BlockSpec((tm, D3), lambda i: (i, 0)), p_prev, p_next,
                  pl.BlockSpec((tm, D), lambda i: (i, 0)), dz_next, pl.BlockSpec((3, D), lambda i: (0, 0))],
        out_specs=(pl.BlockSpec((tm, D3), lambda i: (i, 0)), pl.BlockSpec((SUBLANES_F32, D), lambda i: (0, 0))),
        compiler_params=_params("arbitrary"),
    )(p, p, p, dz, dz, cw)
    if comm is None:
        return res[0], res[1][0:3]
    return res[0][0], res[0][1][0:3], res[1]


def _ffn_gate_fwd(u, cw, cb, seq, name, comm=None):
    T, F2 = u.shape
    F = F2 // 2
    tm = _tile(seq, 512, SUBLANES_BF16)
    cc = _tile(F, 256, LANES)
    prev, _ = _halo_specs(tm, F2, T)

    def body(u_ref, uh_ref, cw_ref, cb_ref, a_ref, v_ref):
        at_start = (pl.program_id(0) * tm) % seq == 0

        def conv(c0):
            cols = slice(c0, c0 + cc)
            cur = u_ref[:, cols].astype(F32)
            halo = jnp.where(at_start, 0.0, uh_ref[:, cols].astype(F32))
            w = cw_ref[:, cols]
            return (_rows_before(halo, cur, 2) * w[0:1] + _rows_before(halo, cur, 1) * w[1:2] + cur * w[2:3]
                    + cb_ref[:, cols])

        for c0 in range(0, F, cc):
            g = conv(c0)
            up = conv(F + c0)
            a_ref[:, c0:c0 + cc] = ((g * jax.nn.sigmoid(g)) * up).astype(a_ref.dtype)
            v_ref[:, c0:c0 + cc] = g.astype(v_ref.dtype)
            v_ref[:, F + c0:F + c0 + cc] = up.astype(v_ref.dtype)

    return _pallas(
        body, comm=comm, name=name,
        out_shape=(jax.ShapeDtypeStruct((T, F), BF16), jax.ShapeDtypeStruct((T, F2), BF16)), grid=(T // tm,),
        in_specs=[pl.BlockSpec((tm, F2), lambda i: (i, 0)), prev,
                  pl.BlockSpec((3, F2), lambda i: (0, 0)), pl.BlockSpec((1, F2), lambda i: (0, 0))],
        out_specs=(pl.BlockSpec((tm, F), lambda i: (i, 0)), pl.BlockSpec((tm, F2), lambda i: (i, 0))),
        compiler_params=_params("parallel"),
    )(u, u, cw, cb.reshape(1, F2))


def _ffn_gate_bwd(u, v, cw, da, seq, name, comm=None):
    T, F2 = u.shape
    F = F2 // 2
    H = SUBLANES_BF16
    tm = _tile(seq, 512, H)
    cc = _tile(F, 256, LANES)
    _, v_next = _halo_specs(tm, F2, T)
    _, da_next = _halo_specs(tm, F, T)

    def body(u_ref, v_ref, vn_ref, da_ref, dan_ref, cw_ref, du_ref, acc_ref):
        i = pl.program_id(0)

        @pl.when(i == 0)
        def _():
            acc_ref[...] = jnp.zeros_like(acc_ref)

        at_end = ((i + 1) * tm) % seq == 0
        n = tm + H

        def rows_and_next(ref, nxt, cols):
            return jnp.concatenate([ref[:, cols].astype(F32), nxt[:, cols].astype(F32)], axis=0)

        def back(d, cols):
            w = cw_ref[:, cols]
            d0 = d[:tm]
            d1 = pltpu.roll(d, n - 1, 0)[:tm]
            d2 = pltpu.roll(d, n - 2, 0)[:tm]
            du_ref[:, cols] = (d0 * w[2:3] + d1 * w[1:2] + d2 * w[0:1]).astype(du_ref.dtype)
            ut = u_ref[:, cols].astype(F32)
            acc_ref[0:1, cols] += jnp.sum(d2 * ut, axis=0, keepdims=True)
            acc_ref[1:2, cols] += jnp.sum(d1 * ut, axis=0, keepdims=True)
            acc_ref[2:3, cols] += jnp.sum(d0 * ut, axis=0, keepdims=True)
            acc_ref[3:4, cols] += jnp.sum(d0, axis=0, keepdims=True)

        for c0 in range(0, F, cc):
            gc, uc = slice(c0, c0 + cc), slice(F + c0, F + c0 + cc)
            g = rows_and_next(v_ref, vn_ref, gc)
            up = rows_and_next(v_ref, vn_ref, uc)
            da_ext = jnp.concatenate([da_ref[:, gc].astype(F32),
                                      jnp.where(at_end, 0.0, dan_ref[:, gc].astype(F32))], axis=0)
            sg = jax.nn.sigmoid(g)
            back(da_ext * up * (sg * (1.0 + g * (1.0 - sg))), gc)
            back(da_ext * (g * sg), uc)

    res = _pallas(
        body, comm=comm, name=name,
        out_shape=(jax.ShapeDtypeStruct((T, F2), BF16), jax.ShapeDtypeStruct((SUBLANES_F32, F2), F32)),
        grid=(T // tm,),
        in_specs=[pl.BlockSpec((tm, F2), lambda i: (i, 0)), pl.BlockSpec((tm, F2), lambda i: (i, 0)), v_next,
                  pl.BlockSpec((tm, F), lambda i: (i, 0)), da_next, pl.BlockSpec((3, F2), lambda i: (0, 0))],
        out_specs=(pl.BlockSpec((tm, F2), lambda i: (i, 0)), pl.BlockSpec((SUBLANES_F32, F2), lambda i: (0, 0))),
        compiler_params=_params("arbitrary"),
    )(u, v, v, da, da, cw)
    (du, acc), landed = res if comm is not None else (res, None)
    return (du, acc[0:3], acc[3]) if comm is None else (du, acc[0:3], acc[3], landed)


def _bucket_map():
    P = ATT_BLOCK
    qi = np.arange(P, dtype=np.int64)[:, None]
    kc = np.arange(2 * P, dtype=np.int64)[None, :]
    delta = qi + P - kc
    maps = []
    max_exact = REL_BUCKETS // 2
    for window, dilation in DILATED_BRANCHES:
        band = (delta >= 0) & (delta <= window // dilation)
        n = np.maximum(delta * dilation, 0)
        nf = np.maximum(n, max_exact).astype(np.float32)
        large = max_exact + (np.log(nf / np.float32(max_exact)) / np.float32(math.log(REL_MAX_DISTANCE / max_exact))
                             * np.float32(REL_BUCKETS - max_exact)).astype(np.int32)
        large = np.minimum(large, REL_BUCKETS - 1)
        bucket = np.where(n < max_exact, n, large)
        maps.append(np.where(band, bucket, -1).astype(np.int32))
    return np.stack(maps)


def _bias_tables(rel_bias, bmap, name):
    n_pairs = rel_bias.shape[1] // 2
    nbr, P, P2 = bmap.shape

    def body(rb_ref, bm_ref, o_ref):
        pair = pl.program_id(0)
        in_seq = lax.broadcasted_iota(jnp.int32, (P, P2), 1) >= P
        for br in range(nbr):
            bm = bm_ref[br]
            for hh in range(2):
                acc = jnp.full((P, P2), MASKED_LOGIT, F32)
                for b in range(REL_BUCKETS):
                    acc = jnp.where(bm == b, rb_ref[b, 2 * pair + hh], acc)
                o_ref[br, 0, 0, hh * P:(hh + 1) * P, :] = acc
                o_ref[br, 0, 1, hh * P:(hh + 1) * P, :] = jnp.where(in_seq, acc, MASKED_LOGIT)

    return pl.pallas_call(
        body, name=name, out_shape=jax.ShapeDtypeStruct((nbr, n_pairs, 2, 2 * P, P2), F32), grid=(n_pairs,),
        in_specs=[pl.BlockSpec(memory_space=pltpu.SMEM), pl.BlockSpec((nbr, P, P2), lambda h: (0, 0, 0))],
        out_specs=pl.BlockSpec((nbr, 1, 2, 2 * P, P2), lambda h: (0, h, 0, 0, 0)),
        compiler_params=_params("parallel"),
    )(rel_bias, bmap)


def _bias_grad(dbias, bmap, name):
    nbr, n_pairs, _, P2 = dbias.shape
    P = P2 // 2

    def body(db_ref, bm_ref, o_ref):
        lane = lax.broadcasted_iota(jnp.int32, (1, LANES), 1)
        for hh in range(2):
            row = jnp.zeros((1, LANES), F32)
            for br in range(nbr):
                bm = bm_ref[br]
                d = db_ref[br, 0, hh * P:(hh + 1) * P, :]
                for b in range(REL_BUCKETS):
                    hit = jnp.sum(jnp.where(bm == b, d, 0.0), axis=1, keepdims=True)
                    row = row + jnp.where(lane == b, jnp.sum(hit, axis=0, keepdims=True), 0.0)
            o_ref[hh] = row

    return pl.pallas_call(
        body, name=name, out_shape=jax.ShapeDtypeStruct((2 * n_pairs, 1, LANES), F32), grid=(n_pairs,),
        in_specs=[pl.BlockSpec((nbr, 1, P2, P2), lambda h: (0, h, 0, 0)), pl.BlockSpec((nbr, P, P2), lambda h: (0, 0, 0))],
        out_specs=pl.BlockSpec((2, 1, LANES), lambda h: (h, 0, 0)),
        compiler_params=_params("parallel"),
    )(dbias, bmap)[:, 0, :]


def _rows(start, dilation):
    if dilation == 1:
        return pl.ds(pl.multiple_of(start, ATT_BLOCK), ATT_BLOCK)
    return pl.ds(start, ATT_BLOCK, stride=dilation)


def _for_each_block(seq, unroll, fn):
    P = ATT_BLOCK
    for br, (_, d) in enumerate(DILATED_BRANCHES):
        nb = seq // d // P
        u = unroll[br] if (unroll[br] % nb == 0 or nb % unroll[br] == 0) else 1
        step = d * P

        def some(i, carry, br=br, d=d, nb=nb, u=u, step=step):
            blocks = []
            if u % nb == 0:
                for k in range(u):
                    if k % nb == 0:
                        start = i * (u // nb) + k // nb
                        blocks.append((start, start, 1))
                    else:
                        blocks.append((blocks[-1][0] + step, blocks[-1][0], 0))
            else:
                r, j0 = (i * u) // nb, (i * u) % nb
                blocks.append((r + j0 * step, r + jnp.maximum(j0 - 1, 0) * step, jnp.where(j0 == 0, 1, 0)))
                for _ in range(1, u):
                    blocks.append((blocks[-1][0] + step, blocks[-1][0], 0))
            fn(br, d, blocks)
            return carry

        lax.fori_loop(0, nb * d // u, some, 0)


class _RowCache:
    def __init__(self, dilation):
        self.dilation, self.seen = dilation, {}

    def rows(self, ref, start):
        key = (id(ref), id(start))
        if key not in self.seen:
            self.seen[key] = ref[_rows(start, self.dilation), :].astype(BF16)
        return self.seen[key]

    def window(self, ref, start, prev):
        return jnp.concatenate([self.rows(ref, prev), self.rows(ref, start)], axis=0)


def _stack_heads(x, head0):
    return jnp.concatenate([jnp.where(head0, x, 0.0), jnp.where(head0, 0.0, x)], axis=0).astype(BF16)


def _attn_fwd(q, kv, bias, name):
    B, S, D = q.shape
    P = ATT_BLOCK
    n_pairs = D // LANES
    nbr = len(DILATED_BRANCHES)
    scale = HEAD_DIM ** -0.5

    def body(q_ref, k_ref, v_ref, bias_ref, o_ref, lse_ref, *stats):
        m_s, l_s, acc_s = stats[0:nbr], stats[nbr:2 * nbr], stats[2 * nbr:3 * nbr]
        head0 = lax.broadcasted_iota(jnp.int32, (P, LANES), 1) < HEAD_DIM

        def block(br, d, blocks):
            cache = _RowCache(d)
            s = [lax.dot_general(_stack_heads(q_ref[_rows(start, d), :] * scale, head0),
                                 cache.window(k_ref, start, prev), (((1,), (1,)), ((), ())),
                                 preferred_element_type=F32) + bias_ref[br, 0, first]
                 for start, prev, first in blocks]
            m = [jnp.max(x, axis=-1, keepdims=True) for x in s]
            p = [jnp.exp(x - y) for x, y in zip(s, m)]
            l = [jnp.sum(x, axis=-1, keepdims=True) for x in p]
            pv = [jnp.dot(x.astype(BF16), cache.window(v_ref, start, prev), preferred_element_type=F32)
                  for x, (start, prev, _) in zip(p, blocks)]
            for k, (start, _, _) in enumerate(blocks):
                rows = _rows(start, d)
                m_s[br][rows, :] = jnp.where(head0, m[k][:P], m[k][P:])
                l_s[br][rows, :] = jnp.where(head0, l[k][:P], l[k][P:])
                acc_s[br][rows, :] = jnp.where(head0, pv[k][:P], pv[k][P:])

        _for_each_block(S, ATTN_FWD_UNROLL, block)

        chunk = _tile(S, 256, SUBLANES_F32)

        def merge(i, carry):
            rows = pl.ds(pl.multiple_of(i * chunk, chunk), chunk)
            ms = [m_s[br][rows, :] for br in range(nbr)]
            m = functools.reduce(jnp.maximum, ms)
            l = jnp.zeros((chunk, LANES), F32)
            acc = jnp.zeros((chunk, LANES), F32)
            for br in range(nbr):
                w = jnp.exp(ms[br] - m)
                l = l + w * l_s[br][rows, :]
                acc = acc + w * acc_s[br][rows, :]
            o_ref[rows, :] = acc / l
            lse_ref[rows, :] = m + jnp.log(l)
            return carry

        lax.fori_loop(0, S // chunk, merge, 0)

    slab = lambda col0: pl.BlockSpec((None, S, LANES), lambda b, h: (b, 0, col0 + h))
    return pl.pallas_call(
        body, name=name,
        out_shape=(jax.ShapeDtypeStruct((B, S, D), F32), jax.ShapeDtypeStruct((B, S, D), F32)),
        grid=(B, n_pairs),
        in_specs=[slab(0), slab(0), slab(n_pairs),
                  pl.BlockSpec((nbr, 1, 2, 2 * P, 2 * P), lambda b, h: (0, h, 0, 0, 0))],
        out_specs=(slab(0), slab(0)),
        scratch_shapes=[pltpu.VMEM((S, LANES), F32)] * (3 * nbr),
        compiler_params=_params("parallel", "parallel"),
    )(q, kv, kv, bias)


def _attn_bwd(q, kv, delta, lse, do, bias, name, comm=None):
    B, S, D = q.shape
    P = ATT_BLOCK
    n_pairs = D // LANES
    nbr = len(DILATED_BRANCHES)
    scale = HEAD_DIM ** -0.5

    def body(q_ref, k_ref, v_ref, delta_s, lse_ref, do_ref, bias_ref, dq_ref, dk_ref, dv_ref, dbias_ref):
        head0 = lax.broadcasted_iota(jnp.int32, (P, LANES), 1) < HEAD_DIM

        @pl.when(pl.program_id(1) == 0)
        def _():
            dbias_ref[...] = jnp.zeros_like(dbias_ref)

        chunk = _tile(S, 512, SUBLANES_F32)

        def prepare(i, carry):
            rows = pl.ds(pl.multiple_of(i * chunk, chunk), chunk)
            zero = jnp.zeros((chunk, LANES), F32)
            dq_ref[rows, :] = zero
            dk_ref[rows, :] = zero
            dv_ref[rows, :] = zero
            return carry

        lax.fori_loop(0, S // chunk, prepare, 0)

        def per_head(x):
            return jnp.concatenate([x[:, 0:1], x[:, HEAD_DIM:HEAD_DIM + 1]], axis=0)

        nt = (((1,), (1,)), ((), ()))
        tn = (((0,), (0,)), ((), ()))

        def block(br, d, blocks):
            cache = _RowCache(d)
            q2 = [_stack_heads(q_ref[_rows(start, d), :] * scale, head0) for start, _, _ in blocks]
            do2 = [_stack_heads(do_ref[_rows(start, d), :], head0) for start, _, _ in blocks]
            kb = [cache.window(k_ref, start, prev) for start, prev, _ in blocks]
            vb = [cache.window(v_ref, start, prev) for start, prev, _ in blocks]
            s = [lax.dot_general(a, b, nt, preferred_element_type=F32) + bias_ref[br, 0, first]
                 for a, b, (_, _, first) in zip(q2, kb, blocks)]
            dp = [lax.dot_general(a, b, nt, preferred_element_type=F32) for a, b in zip(do2, vb)]
            p = [jnp.exp(x - per_head(lse_ref[_rows(start, d), :])) for x, (start, _, _) in zip(s, blocks)]
            ds = [x * (y - per_head(delta_s[_rows(start, d), :])) for x, y, (start, _, _) in zip(p, dp, blocks)]
            for x in ds:
                dbias_ref[br, 0] += x
            ds16 = [x.astype(BF16) for x in ds]
            dq2 = [jnp.dot(a, b, preferred_element_type=F32) for a, b in zip(ds16, kb)]
            dk = [lax.dot_general(a, b, tn, preferred_element_type=F32) for a, b in zip(ds16, q2)]
            dv = [lax.dot_general(a.astype(BF16), b, tn, preferred_element_type=F32) for a, b in zip(p, do2)]
            parts = {}
            for k, (start, prev, first) in enumerate(blocks):
                dq_ref[_rows(start, d), :] += jnp.where(head0, dq2[k][:P], dq2[k][P:]) * scale
                parts.setdefault(id(start), [start, []])[1].append((dk[k][P:], dv[k][P:]))
                if not (isinstance(first, int) and first == 1):
                    parts.setdefault(id(prev), [prev, []])[1].append((dk[k][:P], dv[k][:P]))
            for start, terms in parts.values():
                rows = _rows(start, d)
                dk_ref[rows, :] += functools.reduce(jnp.add, [t[0] for t in terms])
                dv_ref[rows, :] += functools.reduce(jnp.add, [t[1] for t in terms])

        _for_each_block(S, ATTN_BWD_UNROLL, block)

    slab = lambda col0: pl.BlockSpec((None, S, LANES), lambda h, b: (b, 0, col0 + h))
    tab = pl.BlockSpec((nbr, 1, 2, 2 * P, 2 * P), lambda h, b: (0, h, 0, 0, 0))
    dtab = pl.BlockSpec((nbr, 1, 2 * P, 2 * P), lambda h, b: (0, h, 0, 0))
    shp = jax.ShapeDtypeStruct((B, S, D), F32)
    return _pallas(
        body, comm=comm, name=name,
        out_shape=(shp, shp, shp, jax.ShapeDtypeStruct((nbr, n_pairs, 2 * P, 2 * P), F32)),
        grid=(n_pairs, B),
        in_specs=[slab(0), slab(0), slab(n_pairs), slab(0), slab(0), slab(0), tab],
        out_specs=(slab(0), slab(0), slab(0), dtab),
        compiler_params=_params("parallel", "arbitrary"),
    )(q, kv, kv, delta, lse, do, bias)


def _adamw(w, g, m, v, name):
    R, C = w.shape
    tr = _tile(R, 256, SUBLANES_F32) if R % SUBLANES_F32 == 0 else R
    tc = _tile(C, 2048, LANES) if C % LANES == 0 else C

    def body(w_ref, g_ref, m_ref, v_ref, d_ref, nm_ref, nv_ref):
        g_ = g_ref[...]
        m2 = ADAM_B1 * m_ref[...] + (1.0 - ADAM_B1) * g_
        v2 = ADAM_B2 * v_ref[...] + (1.0 - ADAM_B2) * (g_ * g_)
        m_hat = m2 / (1.0 - ADAM_B1 ** ADAM_STEP)
        v_hat = v2 / (1.0 - ADAM_B2 ** ADAM_STEP)
        d_ref[...] = -ADAM_LR * (m_hat / (jnp.sqrt(v_hat) + ADAM_EPS) + ADAM_WD * w_ref[...])
        nm_ref[...] = m2
        nv_ref[...] = v2

    blk = pl.BlockSpec((tr, tc), lambda i, j: (i, j))
    shp = jax.ShapeDtypeStruct((R, C), F32)
    return pl.pallas_call(
        body, name=name, out_shape=(shp, shp, shp), grid=(R // tr, C // tc),
        in_specs=[blk] * 4, out_specs=(blk,) * 3, compiler_params=_params("parallel", "parallel"),
    )(w, g, m, v)


def _sum_slots(slots, name):
    n, R, C = slots.shape
    tr = _tile(R, 256, SUBLANES_BF16) if R % SUBLANES_BF16 == 0 else R
    tc = _tile(C, 2048, LANES) if C % LANES == 0 else C

    def body(s_ref, o_ref):
        acc = s_ref[0].astype(F32)
        for k in range(1, n):
            acc = acc + s_ref[k].astype(F32)
        o_ref[...] = acc

    return pl.pallas_call(
        body, name=name, out_shape=jax.ShapeDtypeStruct((R, C), F32), grid=(R // tr, C // tc),
        in_specs=[pl.BlockSpec((n, tr, tc), lambda i, j: (0, i, j))],
        out_specs=pl.BlockSpec((tr, tc), lambda i, j: (i, j)),
        compiler_params=_params("parallel", "parallel"),
    )(slots)


def _my_place():
    return lax.axis_index("x"), lax.axis_index("y"), lax.axis_index("c")


def _other_chips(x, y):
    return [(1 - x, y), (x, 1 - y), (1 - x, 1 - y)]


def _piece(ref, blk, axis, shard, half):
    h0 = blk[0] // 2
    idx = []
    for dim, n in enumerate(blk):
        if dim == 0:
            start = half * h0 + (shard * n if axis == 0 else 0)
            idx.append(pl.ds(start, h0))
        elif dim == axis:
            idx.append(pl.ds(shard * n, n))
        else:
            idx.append(slice(None))
    return ref.at[tuple(idx)]


def _half_of(ref, blk, half):
    return ref.at[pl.ds(half * (blk[0] // 2), blk[0] // 2)]


def _gather_weights(shards, axes, names, small):
    n = len(shards)
    blks = [_shard_shape(s) for s in shards]
    task = _gather_ici_task(shards, axes, names)

    def stage1(*refs):
        small_in, outs, small_out = refs[n], refs[n + 1:2 * n + 1], refs[2 * n + 1]
        send_sems, recv_sems, local_sems = refs[2 * n + 2:]
        sem = lambda k: (send_sems.at[k], recv_sems.at[k])
        x, y, c = _my_place()
        me = 2 * x + y
        local = pltpu.make_async_copy(small_in, small_out.at[me], local_sems.at[0])
        local.start()
        sends, recvs = task.copies(None, outs, sem)
        for k, (px, py) in enumerate(_other_chips(x, y)):
            sends.append(_remote(small_in, small_out.at[me], sem(task.n_sems + k), (px, py, c)))
            recvs.append(_remote(small_in, small_out.at[2 * px + py], sem(task.n_sems + k), (px, py, c)))
        for cp in sends:
            cp.start()
        for cp in recvs:
            cp.wait_recv()
        for cp in sends:
            cp.wait_send()
        local.wait()

    res = pl.pallas_call(
        stage1, name="gather_weights_ici",
        out_shape=task.outs + [jax.ShapeDtypeStruct((N_SHARDS,) + small.shape, small.dtype)],
        in_specs=[ANY] * (n + 1), out_specs=[ANY] * (n + 1), input_output_aliases={a: a for a in range(n)},
        scratch_shapes=[pltpu.SemaphoreType.DMA((task.n_sems + 3,)), pltpu.SemaphoreType.DMA((task.n_sems + 3,)),
                        pltpu.SemaphoreType.DMA((1,))],
    )(*task.ins, small)
    full = _run_comm(_gather_d2d_task(list(res[:n]), blks, axes), "gather_weights_d2d")
    return full, res[n]


def _run_comm(comm, name):
    n_in, n_out = len(comm.ins), len(comm.outs)

    def body(*refs):
        send_sems, recv_sems = refs[n_in + n_out:]
        sends, recvs = comm.copies(refs[:n_in], refs[n_in:n_in + n_out], lambda k: (send_sems.at[k], recv_sems.at[k]))
        for cp in sends:
            cp.start()
        for cp in recvs:
            cp.wait_recv()
        for cp in sends:
            cp.wait_send()

    return list(pl.pallas_call(
        body, name=name, out_shape=comm.outs, in_specs=[ANY] * n_in, out_specs=[ANY] * n_out,
        input_output_aliases=comm.aliases, scratch_shapes=[pltpu.SemaphoreType.DMA((comm.n_sems,))] * 2,
    )(*comm.ins))


def _exchange_small(small, extra, name):
    n_in, n_out = len(extra.ins), len(extra.outs)

    def body(*refs):
        ex_in, small_in = refs[:n_in], refs[n_in]
        ex_out, small_out = refs[n_in + 1:n_in + 1 + n_out], refs[n_in + 1 + n_out]
        send_sems, recv_sems, local_sems = refs[n_in + n_out + 2:]
        sem = lambda k: (send_sems.at[k], recv_sems.at[k])
        x, y, c = _my_place()
        me = 4 * x + 2 * y + c
        local = pltpu.make_async_copy(small_in, small_out.at[me], local_sems.at[0])
        local.start()
        sends, recvs = extra.copies(ex_in, ex_out, sem)
        for rel in range(1, N_DEVICES):
            px, py, pc = x ^ ((rel >> 2) & 1), y ^ ((rel >> 1) & 1), c ^ (rel & 1)
            k = extra.n_sems + rel - 1
            sends.append(_remote(small_in, small_out.at[me], sem(k), (px, py, pc)))
            recvs.append(_remote(small_in, small_out.at[4 * px + 2 * py + pc], sem(k), (px, py, pc)))
        for cp in sends:
            cp.start()
        for cp in recvs:
            cp.wait_recv()
        for cp in sends:
            cp.wait_send()
        local.wait()

    n_sems = extra.n_sems + N_DEVICES - 1
    res = pl.pallas_call(
        body, name=name, out_shape=extra.outs + [jax.ShapeDtypeStruct((N_DEVICES,) + small.shape, small.dtype)],
        in_specs=[ANY] * (n_in + 1), out_specs=[ANY] * (n_out + 1), input_output_aliases=extra.aliases,
        scratch_shapes=[pltpu.SemaphoreType.DMA((n_sems,)), pltpu.SemaphoreType.DMA((n_sems,)),
                        pltpu.SemaphoreType.DMA((1,))],
    )(*extra.ins, small)
    return list(res[:n_out]), res[n_out]


def _remote(src, dst, sems, device):
    return pltpu.make_async_remote_copy(src_ref=src, dst_ref=dst, send_sem=sems[0], recv_sem=sems[1],
                                        device_id=device, device_id_type=pl.DeviceIdType.MESH)


def _sum_piece(dest, slots, grad, blk, axis, layer, n_layers, name):
    r, c = blk
    h0 = r // 2
    tr = _tile(h0, 256, SUBLANES_BF16)
    tc = _tile(c, 2048, LANES)
    place = jnp.stack([2 * lax.axis_index("x") + lax.axis_index("y"), lax.axis_index("c")]).astype(jnp.int32)

    def body(p_ref, s_ref, g_ref, *rest):
        acc = g_ref[...].astype(F32) + s_ref[0].astype(F32)
        for k in range(1, N_DEVICES - 1):
            acc = acc + s_ref[k].astype(F32)
        rest[-1][...] = acc

    def g_map(i, j, p):
        return (p[1] * (h0 // tr) + (p[0] * (r // tr) if axis == 0 else 0) + i, (p[0] * (c // tc) if axis == 1 else 0) + j)

    in_specs = [pl.BlockSpec((N_DEVICES - 1, tr, tc), lambda i, j, p: (0, i, j)), pl.BlockSpec((tr, tc), g_map)]
    args = [place, slots, grad]
    if dest is not None:
        in_specs.append(ANY)
        args.append(dest)
    return pl.pallas_call(
        body, name=name, out_shape=jax.ShapeDtypeStruct((n_layers, r, c), F32),
        grid_spec=pltpu.PrefetchScalarGridSpec(
            num_scalar_prefetch=1, grid=(h0 // tr, c // tc), in_specs=in_specs,
            out_specs=pl.BlockSpec((None, tr, tc), lambda i, j, p: (layer, p[1] * (h0 // tr) + i, j))),
        input_output_aliases={3: 0} if dest is not None else {},
        compiler_params=_params("parallel", "parallel"),
    )(*args)


def _swap_halves_task(blocks):
    n = len(blocks)
    layers = [(a, l) for a, b in enumerate(blocks) for l in range(b.shape[0])]

    def copies(in_refs, out_refs, sem):
        x, y, c = _my_place()
        sends, recvs = [], []
        for k, (a, l) in enumerate(layers):
            blk = blocks[a].shape[1:]
            mine = _half_of(out_refs[a].at[l], blk, c)
            sends.append(_remote(mine, mine, sem(k), (x, y, 1 - c)))
            recvs.append(_remote(mine, _half_of(out_refs[a].at[l], blk, 1 - c), sem(k), (x, y, 1 - c)))
        return sends, recvs

    return _Comm(blocks, [jax.ShapeDtypeStruct(b.shape, b.dtype) for b in blocks], {a: a for a in range(n)},
                 len(layers), copies)


def _shard_shape(shard):
    return shard[0].shape[1:] if isinstance(shard, tuple) else shard.shape


def _place_shard(shard, axis, name):
    r, c = _shard_shape(shard)
    tr = _tile(r, 512, SUBLANES_BF16)
    full = (r * N_SHARDS, c) if axis == 0 else (r, c * N_SHARDS)
    me2 = (2 * lax.axis_index("x") + lax.axis_index("y")).astype(jnp.int32).reshape(1)

    def body(me_ref, s_ref, o_ref):
        o_ref[...] = s_ref[...].astype(o_ref.dtype)

    if axis == 0:
        out_map = lambda i, me: (me[0] * (r // tr) + i, 0)
    else:
        out_map = lambda i, me: (i, me[0])
    if isinstance(shard, tuple):
        src, layer = shard
        in_spec = pl.BlockSpec((None, tr, c), lambda i, me: (layer, i, 0))
    else:
        src, in_spec = shard, pl.BlockSpec((tr, c), lambda i, me: (i, 0))
    return pl.pallas_call(
        body, name=name, out_shape=jax.ShapeDtypeStruct(full, BF16),
        grid_spec=pltpu.PrefetchScalarGridSpec(
            num_scalar_prefetch=1, grid=(r // tr,), in_specs=[in_spec], out_specs=pl.BlockSpec((tr, c), out_map)),
        compiler_params=_params("parallel"),
    )(me2, src)


def _gather_ici_task(shards, axes, names):
    n = len(shards)
    blks = [_shard_shape(s) for s in shards]
    bases = [_place_shard(s, ax, f"place_{nm}") for s, ax, nm in zip(shards, axes, names)]

    def copies(in_refs, out_refs, sem):
        x, y, c = _my_place()
        me = 2 * x + y
        sends, recvs = [], []
        for a in range(n):
            mine = _piece(out_refs[a], blks[a], axes[a], me, c)
            for k, (px, py) in enumerate(_other_chips(x, y)):
                sends.append(_remote(mine, mine, sem(3 * a + k), (px, py, c)))
                recvs.append(_remote(mine, _piece(out_refs[a], blks[a], axes[a], 2 * px + py, c), sem(3 * a + k),
                                     (px, py, c)))
        return sends, recvs

    return _Comm(bases, [jax.ShapeDtypeStruct(b.shape, b.dtype) for b in bases], {a: a for a in range(n)}, 3 * n, copies)


def _gather_d2d_task(partials, blks, axes):
    n = len(partials)

    def copies(in_refs, out_refs, sem):
        x, y, c = _my_place()
        sends, recvs = [], []
        for a in range(n):
            for k, (px, py) in enumerate(_other_chips(x, y)):
                mine = _piece(out_refs[a], blks[a], axes[a], 2 * px + py, c)
                theirs = _piece(out_refs[a], blks[a], axes[a], 2 * px + py, 1 - c)
                sends.append(_remote(mine, mine, sem(3 * a + k), (x, y, 1 - c)))
                recvs.append(_remote(mine, theirs, sem(3 * a + k), (x, y, 1 - c)))
        return sends, recvs

    return _Comm(partials, [jax.ShapeDtypeStruct(p.shape, p.dtype) for p in partials], {a: a for a in range(n)},
                 3 * n, copies)


def _scatter_task(grads, blks, axes):
    n = len(grads)

    def copies(in_refs, out_refs, sem):
        x, y, c = _my_place()
        sends, recvs = [], []
        for rel in range(1, N_DEVICES):
            px, py, pc = x ^ ((rel >> 2) & 1), y ^ ((rel >> 1) & 1), c ^ (rel & 1)
            for a in range(n):
                src = _piece(in_refs[a], blks[a], axes[a], 2 * px + py, pc)
                k = (N_DEVICES - 1) * a + rel - 1
                sends.append(_remote(src, out_refs[a].at[rel - 1], sem(k), (px, py, pc)))
                recvs.append(_remote(src, out_refs[a].at[rel - 1], sem(k), (px, py, pc)))
        return sends, recvs

    outs = [jax.ShapeDtypeStruct((N_DEVICES - 1, b[0] // 2) + tuple(b[1:]), g.dtype) for b, g in zip(blks, grads)]
    return _Comm(grads, outs, {}, (N_DEVICES - 1) * n, copies)


def _pack(arrays):
    flat = jnp.concatenate([a.reshape(-1).astype(F32) for a in arrays])
    pad = (-flat.shape[0]) % (SUBLANES_F32 * LANES)
    return jnp.pad(flat, (0, pad)).reshape(-1, LANES)


def _unpack(packed, shapes):
    flat = packed.reshape(-1)
    out, off = [], 0
    for s in shapes:
        n = int(np.prod(s))
        out.append(flat[off:off + n].reshape(s))
        off += n
    return out


def _local_step(x, target, W, shards, geom, small):
    W = dict(W)
    B, S, D = x.shape
    T = B * S
    x2 = x.reshape(T, D)
    tgt = target.reshape(T, D)
    bmap = jnp.asarray(_bucket_map())
    blk = lambda names: [geom[k][0] for k in names]
    axs = lambda names: [geom[k][1] for k in names]
    ici = lambda names: _gather_ici_task([shards[k] for k in names], axs(names), names)
    d2d = lambda names, partials: _gather_d2d_task(list(partials), blk(names), axs(names))
    big, slots = {}, {}

    def scatter(names):
        return _scatter_task([big[k] for k in names], blk(names), axs(names))

    def ffn_bwd(l, dout, h, saved, first, second):
        xn, u, v, a = saved
        da = _mm_nt(dout, W[f"w_down{l}"], BF16, f"ffn{l}_down_dx")
        big[f"w_down{l}"] = _mm_tn(a, dout, f"ffn{l}_down_dw")
        names = first + [f"w_down{l}"]
        du, g_cw, g_cb, landed = _ffn_gate_bwd(u, v, small["ffn_conv"][l], da, S, f"ffn{l}_gate_bwd", comm=scatter(names))
        slots.update(zip(names, landed))
        big[f"w_up{l}"] = _mm_tn(xn, du, f"ffn{l}_up_dw")
        norm = (h, small["ffn_norm"][l], dout)
        if second:
            (dh, g_norm), (slots[f"w_up{l}"],) = _mm_nt(du, W[f"w_up{l}"], None, f"ffn{l}_up_dx",
                                                        comm=scatter([f"w_up{l}"]), norm=norm)
        else:
            dh, g_norm = _mm_nt(du, W[f"w_up{l}"], None, f"ffn{l}_up_dx", norm=norm)
        return dh, g_cw, g_cb, g_norm[0]

    xn0 = _rmsnorm_fwd(x2, small["a_norm"][0], "a_norm")
    p, part = _mm_nn(xn0, W["w_in"], None, BF16, "a_in", comm=ici(["w_up0"]))
    z, (W["w_up0"],) = _gate_a_fwd(p, small["a_conv"][0], S, "a_gate", comm=d2d(["w_up0"], part))
    (h1, xn1), part = _mm_nn(z, W["w_out"], x2, F32, "a_out", comm=ici(["w_down0"]), norm_gains=[small["ffn_norm"][0]])
    attn, ffn1 = ["w_kv", "w_q", "w_o"], ["w_up1", "w_down1"]
    u0, landed = _mm_nn(xn1, W["w_up0"], None, BF16, "ffn0_up", comm=_Comm.join([d2d(["w_down0"], part), ici(attn)]))
    W["w_down0"] = landed[0]
    (a0, v0), landed = _ffn_gate_fwd(u0, small["ffn_conv"][0], small["ffn_conv_b"][0], S, "ffn0_gate",
                                     comm=_Comm.join([d2d(attn, landed[1:]), ici(ffn1)]))
    W.update(zip(attn, landed[:len(attn)]))
    (h2, kvn, xn3), landed = _mm_nn(a0, W["w_down0"], h1, F32, "ffn0_down", comm=d2d(ffn1, landed[len(attn):]),
                                    norm_gains=[small["kv_norm"], small["b_norm"][0]])
    W.update(zip(ffn1, landed))
    kv = _mm_nn(kvn, W["w_kv"], None, F32, "kv_proj")
    q = _mm_nn(xn3, W["w_q"], None, F32, "q_proj")
    bias = _bias_tables(small["rel_bias"], bmap, "rel_bias_tables")
    q3, kv3 = q.reshape(B, S, D), kv.reshape(B, S, 2 * D)
    o3, lse3 = _attn_fwd(q3, kv3, bias, "attn_fwd")
    o = o3.reshape(T, D)
    h3, xn4 = _mm_nn(o, W["w_o"], h2, F32, "o_proj", norm_gains=[small["ffn_norm"][1]])
    u1 = _mm_nn(xn4, W["w_up1"], None, BF16, "ffn1_up")
    a1, v1 = _ffn_gate_fwd(u1, small["ffn_conv"][1], small["ffn_conv_b"][1], S, "ffn1_gate")
    sq_err, dh4, g_final = _loss_head(a1, W["w_down1"], h3, small["final_norm"], tgt, "ffn1_down_loss")
    loss = 0.5 * jnp.sum(sq_err) / D

    dh3, g_cw1, g_cb1, g_fn1 = ffn_bwd(1, dh4, h3, (xn4, u1, v1, a1), [], False)
    do, delta = _mm_nt(dh3, W["w_o"], F32, "o_proj_dx", head_dot=o)
    big["w_o"] = _mm_tn(o, dh3, "o_proj_dw")
    (dq3, dk3, dv3, dbias), landed = _attn_bwd(q3, kv3, delta.reshape(B, S, D), lse3, do.reshape(B, S, D), bias, "attn_bwd",
                                               comm=scatter(["w_up1", "w_o"]))
    slots.update(zip(["w_up1", "w_o"], landed))
    g_rel = _bias_grad(dbias, bmap, "rel_bias_grad")[:, :REL_BUCKETS].T
    dq, dk, dv = dq3.reshape(T, D), dk3.reshape(T, D), dv3.reshape(T, D)
    dh2, g_bn = _mm_nt(dq, W["w_q"], None, "q_proj_dx", norm=(h2, small["b_norm"][0], dh3))
    big["w_q"] = _mm_tn(xn3, dq, "q_proj_dw")
    dh2, g_kvn = _mm_nt([dk, dv], W["w_kv"], None, "kv_proj_dx", norm=(h2, small["kv_norm"], dh2))
    big["w_kv"] = jnp.concatenate([_mm_tn(kvn, dk, "k_proj_dw"), _mm_tn(kvn, dv, "v_proj_dw")], axis=1)
    dh1, g_cw0, g_cb0, g_fn0 = ffn_bwd(0, dh2, h1, (xn1, u0, v0, a0), ["w_q", "w_kv"], True)
    dz = _mm_nt(dh1, W["w_out"], BF16, "a_out_dx")
    big["w_out"] = _mm_tn(z, dh1, "a_out_dw")
    dp, g_aconv, (slots["w_out"],) = _gate_a_bwd(p, small["a_conv"][0], dz, S, "a_gate_bwd", comm=scatter(["w_out"]))
    big["w_in"] = _mm_tn(xn0, dp, "a_in_dw")
    (dx, g_an), (slots["w_in"],) = _mm_nt(dp, W["w_in"], None, "a_in_dx", comm=scatter(["w_in"]),
                                          norm=(x2, small["a_norm"][0], dh1))
    g_bn, g_kvn, g_an = g_bn[0], g_kvn[0], g_an[0]

    small_g = {"a_norm": g_an[None], "a_conv": g_aconv[None], "kv_norm": g_kvn, "b_norm": g_bn[None],
               "rel_bias": g_rel, "ffn_norm": jnp.stack([g_fn0, g_fn1]), "ffn_conv": jnp.stack([g_cw0, g_cw1]),
               "ffn_conv_b": jnp.stack([g_cb0, g_cb1]), "final_norm": g_final}
    return loss, dx.reshape(B, S, D), big, slots, small_g


BIG = ("w_in", "w_out", "w_kv", "w_q", "w_o", "w_up0", "w_up1", "w_down0", "w_down1")
SMALL = ("a_norm", "a_conv", "kv_norm", "b_norm", "rel_bias", "ffn_norm", "ffn_conv", "ffn_conv_b", "final_norm")
SMALL_SHARDED = ("a_norm", "a_conv", "ffn_conv")
WEIGHT_ORDER = ("a_norm", "a_w_in", "a_conv", "a_w_out", "kv_norm", "w_kv", "b_norm", "b_w_q", "b_w_o", "rel_bias",
                "ffn_norm", "ffn_w_up", "ffn_conv", "ffn_conv_b", "ffn_w_down", "final_norm")
GRAD_OF = {"w_in": ("a_w_in", 0), "w_out": ("a_w_out", 0), "w_kv": ("w_kv", 0), "w_q": ("b_w_q", 0), "w_o": ("b_w_o", 0),
           "w_up0": ("ffn_w_up", 0), "w_up1": ("ffn_w_up", 1), "w_down0": ("ffn_w_down", 0), "w_down1": ("ffn_w_down", 1)}


def _as2d(a):
    return a.reshape(-1, a.shape[-1])


def kernel(x, a_norm, a_w_in, a_conv, a_w_out, kv_norm, w_kv, b_norm, b_w_q, b_w_o, rel_bias, ffn_norm, ffn_w_up, ffn_conv, ffn_conv_b, ffn_w_down, final_norm, loss_target, m_a_norm, m_a_w_in, m_a_conv, m_a_w_out, m_kv_norm, m_w_kv, m_b_norm, m_b_w_q, m_b_w_o, m_rel_bias, m_ffn_norm, m_ffn_w_up, m_ffn_conv, m_ffn_conv_b, m_ffn_w_down, m_final_norm, v_a_norm, v_a_w_in, v_a_conv, v_a_w_out, v_kv_norm, v_w_kv, v_b_norm, v_b_w_q, v_b_w_o, v_rel_bias, v_ffn_norm, v_ffn_w_up, v_ffn_conv, v_ffn_conv_b, v_ffn_w_down, v_final_norm):
    given = dict(a_norm=a_norm, a_w_in=a_w_in, a_conv=a_conv, a_w_out=a_w_out, kv_norm=kv_norm, w_kv=w_kv, b_norm=b_norm,
                 b_w_q=b_w_q, b_w_o=b_w_o, rel_bias=rel_bias, ffn_norm=ffn_norm, ffn_w_up=ffn_w_up, ffn_conv=ffn_conv,
                 ffn_conv_b=ffn_conv_b, ffn_w_down=ffn_w_down, final_norm=final_norm)
    mom_m = dict(a_norm=m_a_norm, a_w_in=m_a_w_in, a_conv=m_a_conv, a_w_out=m_a_w_out, kv_norm=m_kv_norm, w_kv=m_w_kv,
                 b_norm=m_b_norm, b_w_q=m_b_w_q, b_w_o=m_b_w_o, rel_bias=m_rel_bias, ffn_norm=m_ffn_norm,
                 ffn_w_up=m_ffn_w_up, ffn_conv=m_ffn_conv, ffn_conv_b=m_ffn_conv_b, ffn_w_down=m_ffn_w_down,
                 final_norm=m_final_norm)
    mom_v = dict(a_norm=v_a_norm, a_w_in=v_a_w_in, a_conv=v_a_conv, a_w_out=v_a_w_out, kv_norm=v_kv_norm, w_kv=v_w_kv,
                 b_norm=v_b_norm, b_w_q=v_b_w_q, b_w_o=v_b_w_o, rel_bias=v_rel_bias, ffn_norm=v_ffn_norm,
                 ffn_w_up=v_ffn_w_up, ffn_conv=v_ffn_conv, ffn_conv_b=v_ffn_conv_b, ffn_w_down=v_ffn_w_down,
                 final_norm=v_final_norm)

    shard = {"w_in": ((a_w_in, 0), 1), "w_out": ((a_w_out, 0), 0), "w_kv": (w_kv, 1), "w_q": ((b_w_q, 0), 0),
             "w_o": ((b_w_o, 0), 0), "w_up0": ((ffn_w_up, 0), 1), "w_up1": ((ffn_w_up, 1), 1),
             "w_down0": ((ffn_w_down, 0), 0), "w_down1": ((ffn_w_down, 1), 0)}

    small_sharded = [given[k] for k in SMALL_SHARDED]
    packed = _pack(small_sharded)
    first = ("w_in", "w_out")
    fulls, packed_all = _gather_weights([shard[k][0] for k in first], [shard[k][1] for k in first], first, packed)
    W = dict(zip(first, fulls))
    later = {k: shard[k][0] for k in BIG if k not in first}
    geom = {k: (_shard_shape(shard[k][0]), shard[k][1]) for k in BIG}
    small = {k: given[k] for k in SMALL}
    per_shard = [_unpack(packed_all[j], [a.shape for a in small_sharded]) for j in range(N_SHARDS)]
    for i, k in enumerate(SMALL_SHARDED):
        small[k] = jnp.concatenate([per_shard[j][i] for j in range(N_SHARDS)], axis=-1)

    loss, grad_x, big_g, slots, small_g = _local_step(x, loss_target, W, later, geom, small)

    small_shapes = [small_g[k].shape for k in SMALL] + [(1,)]
    layers_of = {}
    for k in BIG:
        layers_of.setdefault(GRAD_OF[k][0], []).append(k)
    blocks = {}
    for name, members in layers_of.items():
        dest = None
        for k in members:
            dest = _sum_piece(dest, slots[k], big_g[k], geom[k][0], geom[k][1], GRAD_OF[k][1], len(members), f"sum_{k}")
        blocks[name] = dest
    swapped, small_slots = _exchange_small(_pack([small_g[k] for k in SMALL] + [loss.reshape(1)]),
                                           _swap_halves_task(list(blocks.values())), "swap_halves_exchange_small")
    reduced = dict(zip(blocks, swapped))
    *small_sums, loss = _unpack(_sum_slots(small_slots, "sum_small"), small_shapes)
    loss = loss[0]
    small_red = dict(zip(SMALL, small_sums))
    j = 2 * lax.axis_index("x") + lax.axis_index("y")
    for k in SMALL_SHARDED:
        w = given[k].shape[-1]
        small_red[k] = lax.dynamic_slice_in_dim(small_red[k], j * w, w, axis=small_red[k].ndim - 1)

    grads, deltas, new_m, new_v = {}, {}, {}, {}
    for name in WEIGHT_ORDER:
        if name in reduced:
            g = reduced[name].reshape(given[name].shape)
            d, nm, nv = _adamw(_as2d(given[name]), _as2d(g), _as2d(mom_m[name]), _as2d(mom_v[name]), f"adamw_{name}")
            grads[name] = g
            deltas[name], new_m[name], new_v[name] = (t.reshape(given[name].shape) for t in (d, nm, nv))
    small_names = [n for n in WEIGHT_ORDER if n not in reduced]
    for n in small_names:
        grads[n] = small_red[n].reshape(given[n].shape)
    sw, sg, sm, sv = (_pack([d[n] for n in small_names]) for d in (given, grads, mom_m, mom_v))
    d, nm, nv = _adamw(sw, sg, sm, sv, "adamw_small")
    shapes = [given[n].shape for n in small_names]
    for n, a, b_, c_ in zip(small_names, _unpack(d, shapes), _unpack(nm, shapes), _unpack(nv, shapes)):
        deltas[n], new_m[n], new_v[n] = a, b_, c_

    return (loss, grad_x, *[grads[n] for n in WEIGHT_ORDER], *[deltas[n] for n in WEIGHT_ORDER],
            *[new_m[n] for n in WEIGHT_ORDER], *[new_v[n] for n in WEIGHT_ORDER])
```

```python
import functools
import math

import numpy as np

import jax
import jax.numpy as jnp
from jax import lax
from jax.experimental import pallas as pl
from jax.experimental.pallas import tpu as pltpu

F32 = jnp.float32
BF16 = jnp.bfloat16

RMS_EPS = 1e-6
HEAD_DIM = 64
ATT_BLOCK = 128
DILATED_BRANCHES = ((128, 1), (512, 4), (2048, 16))
REL_BUCKETS = 32
REL_MAX_DISTANCE = 2048
MASKED_LOGIT = -1e30
ATTN_FWD_UNROLL = (8, 8, 4)
ATTN_BWD_UNROLL = (8, 8, 8)

ADAM_LR = 0.001
ADAM_B1 = 0.9
ADAM_B2 = 0.999
ADAM_EPS = 1e-08
ADAM_WD = 0.01
ADAM_STEP = 10

LANES = 128
SUBLANES_F32 = 8
SUBLANES_BF16 = 16
VMEM_LIMIT_BYTES = 56 * 1024 * 1024

N_SHARDS = 4
N_DEVICES = 8
ANY = pl.BlockSpec(memory_space=pl.ANY)


def _tile(n, pref, mult):
    best = None
    for t in range(mult, min(n, pref) + 1, mult):
        if n % t == 0:
            best = t
    if best is None:
        raise ValueError(f"no tile for {n} (multiple of {mult}, at most {pref})")
    return best


def _params(*sem):
    return pltpu.CompilerParams(dimension_semantics=sem, vmem_limit_bytes=VMEM_LIMIT_BYTES)


class _Comm:
    def __init__(self, ins, outs, aliases, n_sems, copies):
        self.ins, self.outs, self.aliases, self.n_sems, self.copies = list(ins), list(outs), dict(aliases), n_sems, copies

    @staticmethod
    def join(parts):
        parts = [p for p in parts if p is not None]
        ins, outs, aliases, offs, n_sems = [], [], {}, [], 0
        for p in parts:
            offs.append((len(ins), len(outs), n_sems))
            aliases.update({len(ins) + i: len(outs) + o for i, o in p.aliases.items()})
            ins += p.ins
            outs += p.outs
            n_sems += p.n_sems

        def copies(in_refs, out_refs, sem):
            sends, recvs = [], []
            for p, (i0, o0, s0) in zip(parts, offs):
                s, r = p.copies(in_refs[i0:i0 + len(p.ins)], out_refs[o0:o0 + len(p.outs)],
                                lambda k, s0=s0: sem(s0 + k))
                sends += s
                recvs += r
            return sends, recvs

        return _Comm(ins, outs, aliases, n_sems, copies)


def _pallas(body, *, comm=None, name, out_shape, grid=(), in_specs=(), out_specs=(), scratch_shapes=(),
            compiler_params=None):
    if comm is None:
        return pl.pallas_call(body, name=name, out_shape=out_shape, grid=grid, in_specs=in_specs, out_specs=out_specs,
                              scratch_shapes=scratch_shapes, compiler_params=compiler_params)
    single = not isinstance(out_shape, (tuple, list))
    outs = (out_shape,) if single else tuple(out_shape)
    o_specs = (out_specs,) if single else tuple(out_specs)
    n_in, n_cin, n_out, n_cout, n_scr = len(in_specs), len(comm.ins), len(outs), len(comm.outs), len(scratch_shapes)

    def carried(*refs):
        base_in, c_in = refs[:n_in], refs[n_in:n_in + n_cin]
        o0 = n_in + n_cin
        base_out, c_out = refs[o0:o0 + n_out], refs[o0 + n_out:o0 + n_out + n_cout]
        s0 = o0 + n_out + n_cout
        base_scr, (send_sems, recv_sems) = refs[s0:s0 + n_scr], refs[s0 + n_scr:]
        sem = lambda k: (send_sems.at[k], recv_sems.at[k])
        first = functools.reduce(jnp.logical_and, [pl.program_id(a) == 0 for a in range(len(grid))])
        last = functools.reduce(jnp.logical_and, [pl.program_id(a) == n - 1 for a, n in enumerate(grid)])

        @pl.when(first)
        def _():
            for cp in comm.copies(c_in, c_out, sem)[0]:
                cp.start()

        body(*base_in, *base_out, *base_scr)

        @pl.when(last)
        def _():
            sends, recvs = comm.copies(c_in, c_out, sem)
            for cp in recvs:
                cp.wait_recv()
            for cp in sends:
                cp.wait_send()

    call = pl.pallas_call(
        carried, name=name, out_shape=outs + tuple(comm.outs), grid=grid,
        in_specs=list(in_specs) + [ANY] * n_cin, out_specs=o_specs + (ANY,) * n_cout,
        scratch_shapes=list(scratch_shapes) + [pltpu.SemaphoreType.DMA((comm.n_sems,))] * 2,
        input_output_aliases={n_in + i: n_out + o for i, o in comm.aliases.items()},
        compiler_params=_params(*(["arbitrary"] * len(grid))))

    def run(*args):
        res = call(*args, *comm.ins)
        base = res[0] if single else tuple(res[:n_out])
        return base, list(res[n_out:])

    return run


def _rmsnorm_fwd(x, g, name):
    T, D = x.shape
    tm = _tile(T, 512, SUBLANES_BF16)

    def body(x_ref, g_ref, o_ref):
        xf = x_ref[...]
        r = lax.rsqrt(jnp.mean(xf * xf, axis=-1, keepdims=True) + RMS_EPS)
        o_ref[...] = ((xf * r) * g_ref[...]).astype(o_ref.dtype)

    return pl.pallas_call(
        body, name=name, out_shape=jax.ShapeDtypeStruct((T, D), BF16), grid=(T // tm,),
        in_specs=[pl.BlockSpec((tm, D), lambda i: (i, 0)), pl.BlockSpec((1, D), lambda i: (0, 0))],
        out_specs=pl.BlockSpec((tm, D), lambda i: (i, 0)),
        compiler_params=_params("parallel"),
    )(x, g.reshape(1, D))


def _loss_head(a, b, res, g, target, name):
    T, K = a.shape
    D = b.shape[1]
    tm = _tile(T, 512, SUBLANES_BF16)

    def body(a_ref, b_ref, res_ref, g_ref, t_ref, dh_ref, acc_ref):
        @pl.when(pl.program_id(0) == 0)
        def _():
            acc_ref[...] = jnp.zeros_like(acc_ref)

        xf = jnp.dot(a_ref[...].astype(BF16), b_ref[...], preferred_element_type=F32) + res_ref[...]
        r = lax.rsqrt(jnp.mean(xf * xf, axis=-1, keepdims=True) + RMS_EPS)
        xhat = xf * r
        err = xhat * g_ref[...] - t_ref[...]
        dy = err * (1.0 / D)
        acc_ref[0:1, :] += jnp.sum(dy * xhat, axis=0, keepdims=True)
        acc_ref[1:2, :] += jnp.sum(err * err, axis=0, keepdims=True)
        t = dy * g_ref[...]
        dh_ref[...] = r * (t - xhat * jnp.mean(t * xhat, axis=-1, keepdims=True))

    row = pl.BlockSpec((tm, D), lambda i: (i, 0))
    dh, acc = pl.pallas_call(
        body, name=name,
        out_shape=(jax.ShapeDtypeStruct((T, D), F32), jax.ShapeDtypeStruct((SUBLANES_F32, D), F32)),
        grid=(T // tm,),
        in_specs=[pl.BlockSpec((tm, K), lambda i: (i, 0)), pl.BlockSpec((K, D), lambda i: (0, 0)), row,
                  pl.BlockSpec((1, D), lambda i: (0, 0)), row],
        out_specs=(row, pl.BlockSpec((SUBLANES_F32, D), lambda i: (0, 0))),
        compiler_params=_params("arbitrary"),
    )(a, b, res, g.reshape(1, D), target)
    return acc[1], dh, acc[0]


def _mm_nn(a, b, res, out_dtype, name, comm=None, norm_gains=()):
    T, K = a.shape
    N = b.shape[1]
    n_g = len(norm_gains)
    tm = _tile(T, 512 if n_g else 1024, SUBLANES_BF16)
    tn = _tile(N, 3072, LANES)
    assert not n_g or tn == N, "the fused rmsnorm needs whole rows in one tile"
    n_i = T // tm
    total = (N // tn) * n_i
    slots = 3

    def body(a_hbm, b_ref, *rest):
        n_scr = 4 if res is not None else 2
        scr, rest = rest[len(rest) - n_scr:], rest[:len(rest) - n_scr]
        a_buf, a_sem = scr[0], scr[1]
        step = pl.program_id(0) * n_i + pl.program_id(1)

        def fetch(s):
            start, col = (s % n_i) * tm, (s // n_i) * tn
            if not isinstance(s, int):
                start, col = pl.multiple_of(start, tm), pl.multiple_of(col, tn)
            rows = pl.ds(start, tm)
            copies = [pltpu.make_async_copy(a_hbm.at[rows, :], a_buf.at[s % slots], a_sem.at[s % slots])]
            if res is not None:
                copies.append(pltpu.make_async_copy(rest[0].at[rows, pl.ds(col, tn)], scr[2].at[s % slots],
                                                    scr[3].at[s % slots]))
            return copies

        @pl.when(step == 0)
        def _():
            for s in range(min(slots - 1, total)):
                for cp in fetch(s):
                    cp.start()

        @pl.when(step + slots - 1 < total)
        def _():
            for cp in fetch(step + slots - 1):
                cp.start()

        for cp in fetch(step):
            cp.wait()
        ins, outs = rest[:len(rest) - 1 - n_g], rest[len(rest) - 1 - n_g:]
        acc = jnp.dot(a_buf[step % slots].astype(BF16), b_ref[...], preferred_element_type=F32)
        if res is not None:
            acc = acc + scr[2][step % slots]
        outs[0][...] = acc.astype(outs[0].dtype)
        if n_g:
            y = acc * lax.rsqrt(jnp.mean(acc * acc, axis=-1, keepdims=True) + RMS_EPS)
            for g_ref, xn_ref in zip(ins[len(ins) - n_g:], outs[1:]):
                xn_ref[...] = (y * g_ref[...]).astype(xn_ref.dtype)

    tile = pl.BlockSpec((tm, tn), lambda j, i: (i, j))
    in_specs = [ANY, pl.BlockSpec((K, tn), lambda j, i: (0, j))]
    args = [a, b]
    scratch = [pltpu.VMEM((slots, tm, K), a.dtype), pltpu.SemaphoreType.DMA((slots,))]
    if res is not None:
        in_specs.append(ANY)
        args.append(res)
        scratch += [pltpu.VMEM((slots, tm, tn), res.dtype), pltpu.SemaphoreType.DMA((slots,))]
    for g in norm_gains:
        in_specs.append(pl.BlockSpec((1, N), lambda j, i: (0, 0)))
        args.append(g.reshape(1, N))
    out_shape = jax.ShapeDtypeStruct((T, N), out_dtype)
    if n_g:
        out_shape = (out_shape,) + (jax.ShapeDtypeStruct((T, N), BF16),) * n_g
    return _pallas(
        body, comm=comm, name=name, out_shape=out_shape, grid=(N // tn, n_i),
        in_specs=in_specs, out_specs=(tile,) * (1 + n_g) if n_g else tile, scratch_shapes=scratch,
        compiler_params=_params("arbitrary", "arbitrary"),
    )(*args)


def _mm_nt(dy, b, out_dtype, name, comm=None, norm=None, head_dot=None):
    dys = list(dy) if isinstance(dy, (list, tuple)) else [dy]
    T, n_each = dys[0].shape
    K = b.shape[0]
    tm = _tile(T, 2048 if norm is None and head_dot is None else 1024, SUBLANES_BF16)
    tk = _tile(K, 1536, LANES)
    tn = _tile(n_each, 2816 if norm is None else 1536, LANES)
    per = n_each // tn
    n_steps = per * len(dys)
    assert norm is None or tk == K, "the fused rmsnorm backward needs whole rows in one tile"

    def body(*refs):
        dy_refs, b_ref, acc_ref = refs[:len(dys)], refs[len(dys)], refs[-1]
        i, n = pl.program_id(0), pl.program_id(2)

        @pl.when(n == 0)
        def _():
            acc_ref[...] = jnp.zeros_like(acc_ref)

        for p, dy_ref in enumerate(dy_refs):
            @pl.when(jnp.logical_and(n >= p * per, n < (p + 1) * per))
            def _(dy_ref=dy_ref):
                acc_ref[...] += lax.dot_general(dy_ref[...].astype(BF16), b_ref[...], (((1,), (1,)), ((), ())),
                                                preferred_element_type=F32)

        if norm is None and head_dot is None:
            @pl.when(n == n_steps - 1)
            def _():
                refs[-2][...] = acc_ref[...].astype(refs[-2].dtype)
        elif norm is None:
            o_ref, out_ref, dot_ref = refs[len(dys) + 1:-1]

            @pl.when(n == n_steps - 1)
            def _():
                d = acc_ref[...]
                out_ref[...] = d.astype(out_ref.dtype)
                head0 = lax.broadcasted_iota(jnp.int32, (tm, LANES), 1) < HEAD_DIM
                for c0 in range(0, tk, LANES):
                    x = d[:, c0:c0 + LANES] * o_ref[:, c0:c0 + LANES]
                    d0 = jnp.sum(jnp.where(head0, x, 0.0), axis=-1, keepdims=True)
                    d1 = jnp.sum(jnp.where(head0, 0.0, x), axis=-1, keepdims=True)
                    dot_ref[:, c0:c0 + LANES] = jnp.where(head0, d0, d1)
        else:
            x_ref, g_ref, dres_ref, dx_ref, dg_ref = refs[len(dys) + 1:-1]

            @pl.when(jnp.logical_and(i == 0, n == 0))
            def _():
                dg_ref[...] = jnp.zeros_like(dg_ref)

            @pl.when(n == n_steps - 1)
            def _():
                xf = x_ref[...]
                r = lax.rsqrt(jnp.mean(xf * xf, axis=-1, keepdims=True) + RMS_EPS)
                xhat = xf * r
                d = acc_ref[...]
                dg_ref[0:1, :] += jnp.sum(d * xhat, axis=0, keepdims=True)
                t = d * g_ref[...]
                dx_ref[...] = dres_ref[...] + r * (t - xhat * jnp.mean(t * xhat, axis=-1, keepdims=True))

    in_specs = [pl.BlockSpec((tm, tn), lambda i, k, n, p=p: (i, jnp.clip(n - p * per, 0, per - 1))) for p in range(len(dys))]
    in_specs.append(pl.BlockSpec((tk, tn), lambda i, k, n: (k, n)))
    args = dys + [b]
    tile = pl.BlockSpec((tm, tk), lambda i, k, n: (i, k))
    if norm is None and head_dot is None:
        out_shape, out_specs = jax.ShapeDtypeStruct((T, K), out_dtype), tile
    elif norm is None:
        in_specs.append(tile)
        args.append(head_dot)
        out_shape = (jax.ShapeDtypeStruct((T, K), out_dtype), jax.ShapeDtypeStruct((T, K), F32))
        out_specs = (tile, tile)
    else:
        x, g, dres = norm
        in_specs += [tile, pl.BlockSpec((1, K), lambda i, k, n: (0, 0)), tile]
        args += [x, g.reshape(1, K), dres]
        out_shape = (jax.ShapeDtypeStruct((T, K), F32), jax.ShapeDtypeStruct((SUBLANES_F32, K), F32))
        out_specs = (tile, pl.BlockSpec((SUBLANES_F32, K), lambda i, k, n: (0, 0)))
    return _pallas(
        body, comm=comm, name=name, out_shape=out_shape, grid=(T // tm, K // tk, n_steps),
        in_specs=in_specs, out_specs=out_specs, scratch_shapes=[pltpu.VMEM((tm, tk), F32)],
        compiler_params=_params("parallel", "parallel", "arbitrary") if norm is None else _params(*["arbitrary"] * 3),
    )(*args)


def _mm_tn(a, dy, name):
    T, K = a.shape
    N = dy.shape[1]
    tt = _tile(T, 2048, SUBLANES_BF16)
    tk = _tile(K, 1536, LANES)
    tn = _tile(N, 1536, LANES)
    t_steps = T // tt

    def body(a_ref, dy_ref, o_ref, acc_ref):
        t = pl.program_id(2)

        @pl.when(t == 0)
        def _():
            acc_ref[...] = jnp.zeros_like(acc_ref)

        acc_ref[...] += lax.dot_general(a_ref[...].astype(BF16), dy_ref[...].astype(BF16),
                                        (((0,), (0,)), ((), ())), preferred_element_type=F32)

        @pl.when(t == t_steps - 1)
        def _():
            o_ref[...] = acc_ref[...].astype(o_ref.dtype)

    return pl.pallas_call(
        body, name=name, out_shape=jax.ShapeDtypeStruct((K, N), BF16), grid=(K // tk, N // tn, t_steps),
        in_specs=[pl.BlockSpec((tt, tk), lambda k, n, t: (t, k)), pl.BlockSpec((tt, tn), lambda k, n, t: (t, n))],
        out_specs=pl.BlockSpec((tk, tn), lambda k, n, t: (k, n)),
        scratch_shapes=[pltpu.VMEM((tk, tn), F32)],
        compiler_params=_params("parallel", "parallel", "arbitrary"),
    )(a, dy)


def _rows_before(halo, cur, k):
    h = halo.shape[0]
    return pltpu.roll(jnp.concatenate([halo, cur], axis=0), k, 0)[h:]


def _rows_after(cur, halo, k):
    n = cur.shape[0]
    total = n + halo.shape[0]
    return pltpu.roll(jnp.concatenate([cur, halo], axis=0), total - k, 0)[:n]


def _halo_specs(tm, width, n_rows):
    per = tm // SUBLANES_BF16
    last = n_rows // SUBLANES_BF16 - 1
    prev = pl.BlockSpec((SUBLANES_BF16, width), lambda i: (jnp.maximum(i * per - 1, 0), 0))
    nxt = pl.BlockSpec((SUBLANES_BF16, width), lambda i: (jnp.minimum((i + 1) * per, last), 0))
    return prev, nxt


def _gate_a_fwd(p, cw, seq, name, comm=None):
    T, D3 = p.shape
    D = D3 // 3
    tm = _tile(seq, 512, SUBLANES_BF16)
    cc = _tile(D, 256, LANES)
    prev, _ = _halo_specs(tm, D3, T)

    def body(p_ref, ph_ref, cw_ref, z_ref):
        at_start = (pl.program_id(0) * tm) % seq == 0
        for c0 in range(0, D, cc):
            b = p_ref[:, c0:c0 + cc].astype(F32)
            u = p_ref[:, D + c0:D + c0 + cc].astype(F32) * p_ref[:, 2 * D + c0:2 * D + c0 + cc].astype(F32)
            uh = ph_ref[:, D + c0:D + c0 + cc].astype(F32) * ph_ref[:, 2 * D + c0:2 * D + c0 + cc].astype(F32)
            uh = jnp.where(at_start, 0.0, uh)
            w = cw_ref[:, c0:c0 + cc]
            cv = _rows_before(uh, u, 2) * w[0:1] + _rows_before(uh, u, 1) * w[1:2] + u * w[2:3]
            z_ref[:, c0:c0 + cc] = (b * cv).astype(z_ref.dtype)

    return _pallas(
        body, comm=comm, name=name, out_shape=jax.ShapeDtypeStruct((T, D), BF16), grid=(T // tm,),
        in_specs=[pl.BlockSpec((tm, D3), lambda i: (i, 0)), prev, pl.BlockSpec((3, D), lambda i: (0, 0))],
        out_specs=pl.BlockSpec((tm, D), lambda i: (i, 0)),
        compiler_params=_params("parallel"),
    )(p, p, cw)


def _gate_a_bwd(p, cw, dz, seq, name, comm=None):
    T, D3 = p.shape
    D = D3 // 3
    tm = _tile(seq, 512, SUBLANES_BF16)
    cc = _tile(D, 256, LANES)
    p_prev, p_next = _halo_specs(tm, D3, T)
    _, dz_next = _halo_specs(tm, D, T)

    def body(p_ref, pp_ref, pn_ref, dz_ref, dzn_ref, cw_ref, dp_ref, dcw_ref):
        i = pl.program_id(0)

        @pl.when(i == 0)
        def _():
            dcw_ref[...] = jnp.zeros_like(dcw_ref)

        at_start = (i * tm) % seq == 0
        at_end = ((i + 1) * tm) % seq == 0
        for c0 in range(0, D, cc):
            cb, cc_, ch = slice(c0, c0 + cc), slice(D + c0, D + c0 + cc), slice(2 * D + c0, 2 * D + c0 + cc)
            b = p_ref[:, cb].astype(F32)
            c = p_ref[:, cc_].astype(F32)
            hh = p_ref[:, ch].astype(F32)
            u = c * hh
            uh = jnp.where(at_start, 0.0, pp_ref[:, cc_].astype(F32) * pp_ref[:, ch].astype(F32))
            w = cw_ref[:, cb]
            u1 = _rows_before(uh, u, 1)
            u2 = _rows_before(uh, u, 2)
            cv = u2 * w[0:1] + u1 * w[1:2] + u * w[2:3]
            dz_t = dz_ref[:, cb].astype(F32)
            dcv = dz_t * b
            dcvn = jnp.where(at_end, 0.0, dzn_ref[:, cb].astype(F32) * pn_ref[:, cb].astype(F32))
            du = dcv * w[2:3] + _rows_after(dcv, dcvn, 1) * w[1:2] + _rows_after(dcv, dcvn, 2) * w[0:1]
            dp_ref[:, cb] = (dz_t * cv).astype(dp_ref.dtype)
            dp_ref[:, cc_] = (du * hh).astype(dp_ref.dtype)
            dp_ref[:, ch] = (du * c).astype(dp_ref.dtype)
            dcw_ref[0:1, cb] += jnp.sum(dcv * u2, axis=0, keepdims=True)
            dcw_ref[1:2, cb] += jnp.sum(dcv * u1, axis=0, keepdims=True)
            dcw_ref[2:3, cb] += jnp.sum(dcv * u, axis=0, keepdims=True)

    res = _pallas(
        body, comm=comm, name=name,
        out_shape=(jax.ShapeDtypeStruct((T, D3), BF16), jax.ShapeDtypeStruct((SUBLANES_F32, D), F32)),
        grid=(T // tm,),
        in_specs=[pl.BlockSpec((tm, D3), lambda i: (i, 0)), p_prev, p_next,
                  pl.BlockSpec((tm, D), lambda i: (i, 0)), dz_next, pl.BlockSpec((3, D), lambda i: (0, 0))],
        out_specs=(pl.BlockSpec((tm, D3), lambda i: (i, 0)), pl.BlockSpec((SUBLANES_F32, D), lambda i: (0, 0))),
        compiler_params=_params("arbitrary"),
    )(p, p, p, dz, dz, cw)
    if comm is None:
        return res[0], res[1][0:3]
    return res[0][0], res[0][1][0:3], res[1]


def _ffn_gate_fwd(u, cw, cb, seq, name, comm=None):
    T, F2 = u.shape
    F = F2 // 2
    tm = _tile(seq, 512, SUBLANES_BF16)
    cc = _tile(F, 256, LANES)
    prev, _ = _halo_specs(tm, F2, T)

    def body(u_ref, uh_ref, cw_ref, cb_ref, a_ref, v_ref):
        at_start = (pl.program_id(0) * tm) % seq == 0

        def conv(c0):
            cols = slice(c0, c0 + cc)
            cur = u_ref[:, cols].astype(F32)
            halo = jnp.where(at_start, 0.0, uh_ref[:, cols].astype(F32))
            w = cw_ref[:, cols]
            return (_rows_before(halo, cur, 2) * w[0:1] + _rows_before(halo, cur, 1) * w[1:2] + cur * w[2:3]
                    + cb_ref[:, cols])

        for c0 in range(0, F, cc):
            g = conv(c0)
            up = conv(F + c0)
            a_ref[:, c0:c0 + cc] = ((g * jax.nn.sigmoid(g)) * up).astype(a_ref.dtype)
            v_ref[:, c0:c0 + cc] = g.astype(v_ref.dtype)
            v_ref[:, F + c0:F + c0 + cc] = up.astype(v_ref.dtype)

    return _pallas(
        body, comm=comm, name=name,
        out_shape=(jax.ShapeDtypeStruct((T, F), BF16), jax.ShapeDtypeStruct((T, F2), BF16)), grid=(T // tm,),
        in_specs=[pl.BlockSpec((tm, F2), lambda i: (i, 0)), prev,
                  pl.BlockSpec((3, F2), lambda i: (0, 0)), pl.BlockSpec((1, F2), lambda i: (0, 0))],
        out_specs=(pl.BlockSpec((tm, F), lambda i: (i, 0)), pl.BlockSpec((tm, F2), lambda i: (i, 0))),
        compiler_params=_params("parallel"),
    )(u, u, cw, cb.reshape(1, F2))


def _ffn_gate_bwd(u, v, cw, da, seq, name, comm=None):
    T, F2 = u.shape
    F = F2 // 2
    H = SUBLANES_BF16
    tm = _tile(seq, 512, H)
    cc = _tile(F, 256, LANES)
    _, v_next = _halo_specs(tm, F2, T)
    _, da_next = _halo_specs(tm, F, T)

    def body(u_ref, v_ref, vn_ref, da_ref, dan_ref, cw_ref, du_ref, acc_ref):
        i = pl.program_id(0)

        @pl.when(i == 0)
        def _():
            acc_ref[...] = jnp.zeros_like(acc_ref)

        at_end = ((i + 1) * tm) % seq == 0
        n = tm + H

        def rows_and_next(ref, nxt, cols):
            return jnp.concatenate([ref[:, cols].astype(F32), nxt[:, cols].astype(F32)], axis=0)

        def back(d, cols):
            w = cw_ref[:, cols]
            d0 = d[:tm]
            d1 = pltpu.roll(d, n - 1, 0)[:tm]
            d2 = pltpu.roll(d, n - 2, 0)[:tm]
            du_ref[:, cols] = (d0 * w[2:3] + d1 * w[1:2] + d2 * w[0:1]).astype(du_ref.dtype)
            ut = u_ref[:, cols].astype(F32)
            acc_ref[0:1, cols] += jnp.sum(d2 * ut, axis=0, keepdims=True)
            acc_ref[1:2, cols] += jnp.sum(d1 * ut, axis=0, keepdims=True)
            acc_ref[2:3, cols] += jnp.sum(d0 * ut, axis=0, keepdims=True)
            acc_ref[3:4, cols] += jnp.sum(d0, axis=0, keepdims=True)

        for c0 in range(0, F, cc):
            gc, uc = slice(c0, c0 + cc), slice(F + c0, F + c0 + cc)
            g = rows_and_next(v_ref, vn_ref, gc)
            up = rows_and_next(v_ref, vn_ref, uc)
            da_ext = jnp.concatenate([da_ref[:, gc].astype(F32),
                                      jnp.where(at_end, 0.0, dan_ref[:, gc].astype(F32))], axis=0)
            sg = jax.nn.sigmoid(g)
            back(da_ext * up * (sg * (1.0 + g * (1.0 - sg))), gc)
            back(da_ext * (g * sg), uc)

    res = _pallas(
        body, comm=comm, name=name,
        out_shape=(jax.ShapeDtypeStruct((T, F2), BF16), jax.ShapeDtypeStruct((SUBLANES_F32, F2), F32)),
        grid=(T // tm,),
        in_specs=[pl.BlockSpec((tm, F2), lambda i: (i, 0)), pl.BlockSpec((tm, F2), lambda i: (i, 0)), v_next,
                  pl.BlockSpec((tm, F), lambda i: (i, 0)), da_next, pl.BlockSpec((3, F2), lambda i: (0, 0))],
        out_specs=(pl.BlockSpec((tm, F2), lambda i: (i, 0)), pl.BlockSpec((SUBLANES_F32, F2), lambda i: (0, 0))),
        compiler_params=_params("arbitrary"),
    )(u, v, v, da, da, cw)
    (du, acc), landed = res if comm is not None else (res, None)
    return (du, acc[0:3], acc[3]) if comm is None else (du, acc[0:3], acc[3], landed)


def _bucket_map():
    P = ATT_BLOCK
    qi = np.arange(P, dtype=np.int64)[:, None]
    kc = np.arange(2 * P, dtype=np.int64)[None, :]
    delta = qi + P - kc
    maps = []
    max_exact = REL_BUCKETS // 2
    for window, dilation in DILATED_BRANCHES:
        band = (delta >= 0) & (delta <= window // dilation)
        n = np.maximum(delta * dilation, 0)
        nf = np.maximum(n, max_exact).astype(np.float32)
        large = max_exact + (np.log(nf / np.float32(max_exact)) / np.float32(math.log(REL_MAX_DISTANCE / max_exact))
                             * np.float32(REL_BUCKETS - max_exact)).astype(np.int32)
        large = np.minimum(large, REL_BUCKETS - 1)
        bucket = np.where(n < max_exact, n, large)
        maps.append(np.where(band, bucket, -1).astype(np.int32))
    return np.stack(maps)


def _bias_tables(rel_bias, bmap, name):
    n_pairs = rel_bias.shape[1] // 2
    nbr, P, P2 = bmap.shape

    def body(rb_ref, bm_ref, o_ref):
        pair = pl.program_id(0)
        in_seq = lax.broadcasted_iota(jnp.int32, (P, P2), 1) >= P
        for br in range(nbr):
            bm = bm_ref[br]
            for hh in range(2):
                acc = jnp.full((P, P2), MASKED_LOGIT, F32)
                for b in range(REL_BUCKETS):
                    acc = jnp.where(bm == b, rb_ref[b, 2 * pair + hh], acc)
                o_ref[br, 0, 0, hh * P:(hh + 1) * P, :] = acc
                o_ref[br, 0, 1, hh * P:(hh + 1) * P, :] = jnp.where(in_seq, acc, MASKED_LOGIT)

    return pl.pallas_call(
        body, name=name, out_shape=jax.ShapeDtypeStruct((nbr, n_pairs, 2, 2 * P, P2), F32), grid=(n_pairs,),
        in_specs=[pl.BlockSpec(memory_space=pltpu.SMEM), pl.BlockSpec((nbr, P, P2), lambda h: (0, 0, 0))],
        out_specs=pl.BlockSpec((nbr, 1, 2, 2 * P, P2), lambda h: (0, h, 0, 0, 0)),
        compiler_params=_params("parallel"),
    )(rel_bias, bmap)


def _bias_grad(dbias, bmap, name):
    nbr, n_pairs, _, P2 = dbias.shape
    P = P2 // 2

    def body(db_ref, bm_ref, o_ref):
        lane = lax.broadcasted_iota(jnp.int32, (1, LANES), 1)
        for hh in range(2):
            row = jnp.zeros((1, LANES), F32)
            for br in range(nbr):
                bm = bm_ref[br]
                d = db_ref[br, 0, hh * P:(hh + 1) * P, :]
                for b in range(REL_BUCKETS):
                    hit = jnp.sum(jnp.where(bm == b, d, 0.0), axis=1, keepdims=True)
                    row = row + jnp.where(lane == b, jnp.sum(hit, axis=0, keepdims=True), 0.0)
            o_ref[hh] = row

    return pl.pallas_call(
        body, name=name, out_shape=jax.ShapeDtypeStruct((2 * n_pairs, 1, LANES), F32), grid=(n_pairs,),
        in_specs=[pl.BlockSpec((nbr, 1, P2, P2), lambda h: (0, h, 0, 0)), pl.BlockSpec((nbr, P, P2), lambda h: (0, 0, 0))],
        out_specs=pl.BlockSpec((2, 1, LANES), lambda h: (h, 0, 0)),
        compiler_params=_params("parallel"),
    )(dbias, bmap)[:, 0, :]


def _rows(start, dilation):
    if dilation == 1:
        return pl.ds(pl.multiple_of(start, ATT_BLOCK), ATT_BLOCK)
    return pl.ds(start, ATT_BLOCK, stride=dilation)


def _for_each_block(seq, unroll, fn):
    P = ATT_BLOCK
    for br, (_, d) in enumerate(DILATED_BRANCHES):
        nb = seq // d // P
        u = unroll[br] if (unroll[br] % nb == 0 or nb % unroll[br] == 0) else 1
        step = d * P

        def some(i, carry, br=br, d=d, nb=nb, u=u, step=step):
            blocks = []
            if u % nb == 0:
                for k in range(u):
                    if k % nb == 0:
                        start = i * (u // nb) + k // nb
                        blocks.append((start, start, 1))
                    else:
                        blocks.append((blocks[-1][0] + step, blocks[-1][0], 0))
            else:
                r, j0 = (i * u) // nb, (i * u) % nb
                blocks.append((r + j0 * step, r + jnp.maximum(j0 - 1, 0) * step, jnp.where(j0 == 0, 1, 0)))
                for _ in range(1, u):
                    blocks.append((blocks[-1][0] + step, blocks[-1][0], 0))
            fn(br, d, blocks)
            return carry

        lax.fori_loop(0, nb * d // u, some, 0)


class _RowCache:
    def __init__(self, dilation):
        self.dilation, self.seen = dilation, {}

    def rows(self, ref, start):
        key = (id(ref), id(start))
        if key not in self.seen:
            self.seen[key] = ref[_rows(start, self.dilation), :].astype(BF16)
        return self.seen[key]

    def window(self, ref, start, prev):
        return jnp.concatenate([self.rows(ref, prev), self.rows(ref, start)], axis=0)


def _stack_heads(x, head0):
    return jnp.concatenate([jnp.where(head0, x, 0.0), jnp.where(head0, 0.0, x)], axis=0).astype(BF16)


def _attn_fwd(q, kv, bias, name):
    B, S, D = q.shape
    P = ATT_BLOCK
    n_pairs = D // LANES
    nbr = len(DILATED_BRANCHES)
    scale = HEAD_DIM ** -0.5

    def body(q_ref, k_ref, v_ref, bias_ref, o_ref, lse_ref, *stats):
        m_s, l_s, acc_s = stats[0:nbr], stats[nbr:2 * nbr], stats[2 * nbr:3 * nbr]
        head0 = lax.broadcasted_iota(jnp.int32, (P, LANES), 1) < HEAD_DIM

        def block(br, d, blocks):
            cache = _RowCache(d)
            s = [lax.dot_general(_stack_heads(q_ref[_rows(start, d), :] * scale, head0),
                                 cache.window(k_ref, start, prev), (((1,), (1,)), ((), ())),
                                 preferred_element_type=F32) + bias_ref[br, 0, first]
                 for start, prev, first in blocks]
            m = [jnp.max(x, axis=-1, keepdims=True) for x in s]
            p = [jnp.exp(x - y) for x, y in zip(s, m)]
            l = [jnp.sum(x, axis=-1, keepdims=True) for x in p]
            pv = [jnp.dot(x.astype(BF16), cache.window(v_ref, start, prev), preferred_element_type=F32)
                  for x, (start, prev, _) in zip(p, blocks)]
            for k, (start, _, _) in enumerate(blocks):
                rows = _rows(start, d)
                m_s[br][rows, :] = jnp.where(head0, m[k][:P], m[k][P:])
                l_s[br][rows, :] = jnp.where(head0, l[k][:P], l[k][P:])
                acc_s[br][rows, :] = jnp.where(head0, pv[k][:P], pv[k][P:])

        _for_each_block(S, ATTN_FWD_UNROLL, block)

        chunk = _tile(S, 256, SUBLANES_F32)

        def merge(i, carry):
            rows = pl.ds(pl.multiple_of(i * chunk, chunk), chunk)
            ms = [m_s[br][rows, :] for br in range(nbr)]
            m = functools.reduce(jnp.maximum, ms)
            l = jnp.zeros((chunk, LANES), F32)
            acc = jnp.zeros((chunk, LANES), F32)
            for br in range(nbr):
                w = jnp.exp(ms[br] - m)
                l = l + w * l_s[br][rows, :]
                acc = acc + w * acc_s[br][rows, :]
            o_ref[rows, :] = acc / l
            lse_ref[rows, :] = m + jnp.log(l)
            return carry

        lax.fori_loop(0, S // chunk, merge, 0)

    slab = lambda col0: pl.BlockSpec((None, S, LANES), lambda b, h: (b, 0, col0 + h))
    return pl.pallas_call(
        body, name=name,
        out_shape=(jax.ShapeDtypeStruct((B, S, D), F32), jax.ShapeDtypeStruct((B, S, D), F32)),
        grid=(B, n_pairs),
        in_specs=[slab(0), slab(0), slab(n_pairs),
                  pl.BlockSpec((nbr, 1, 2, 2 * P, 2 * P), lambda b, h: (0, h, 0, 0, 0))],
        out_specs=(slab(0), slab(0)),
        scratch_shapes=[pltpu.VMEM((S, LANES), F32)] * (3 * nbr),
        compiler_params=_params("parallel", "parallel"),
    )(q, kv, kv, bias)


def _attn_bwd(q, kv, delta, lse, do, bias, name, comm=None):
    B, S, D = q.shape
    P = ATT_BLOCK
    n_pairs = D // LANES
    nbr = len(DILATED_BRANCHES)
    scale = HEAD_DIM ** -0.5

    def body(q_ref, k_ref, v_ref, delta_s, lse_ref, do_ref, bias_ref, dq_ref, dk_ref, dv_ref, dbias_ref):
        head0 = lax.broadcasted_iota(jnp.int32, (P, LANES), 1) < HEAD_DIM

        @pl.when(pl.program_id(1) == 0)
        def _():
            dbias_ref[...] = jnp.zeros_like(dbias_ref)

        chunk = _tile(S, 512, SUBLANES_F32)

        def prepare(i, carry):
            rows = pl.ds(pl.multiple_of(i * chunk, chunk), chunk)
            zero = jnp.zeros((chunk, LANES), F32)
            dq_ref[rows, :] = zero
            dk_ref[rows, :] = zero
            dv_ref[rows, :] = zero
            return carry

        lax.fori_loop(0, S // chunk, prepare, 0)

        def per_head(x):
            return jnp.concatenate([x[:, 0:1], x[:, HEAD_DIM:HEAD_DIM + 1]], axis=0)

        nt = (((1,), (1,)), ((), ()))
        tn = (((0,), (0,)), ((), ()))

        def block(br, d, blocks):
            cache = _RowCache(d)
            q2 = [_stack_heads(q_ref[_rows(start, d), :] * scale, head0) for start, _, _ in blocks]
            do2 = [_stack_heads(do_ref[_rows(start, d), :], head0) for start, _, _ in blocks]
            kb = [cache.window(k_ref, start, prev) for start, prev, _ in blocks]
            vb = [cache.window(v_ref, start, prev) for start, prev, _ in blocks]
            s = [lax.dot_general(a, b, nt, preferred_element_type=F32) + bias_ref[br, 0, first]
                 for a, b, (_, _, first) in zip(q2, kb, blocks)]
            dp = [lax.dot_general(a, b, nt, preferred_element_type=F32) for a, b in zip(do2, vb)]
            p = [jnp.exp(x - per_head(lse_ref[_rows(start, d), :])) for x, (start, _, _) in zip(s, blocks)]
            ds = [x * (y - per_head(delta_s[_rows(start, d), :])) for x, y, (start, _, _) in zip(p, dp, blocks)]
            for x in ds:
                dbias_ref[br, 0] += x
            ds16 = [x.astype(BF16) for x in ds]
            dq2 = [jnp.dot(a, b, preferred_element_type=F32) for a, b in zip(ds16, kb)]
            dk = [lax.dot_general(a, b, tn, preferred_element_type=F32) for a, b in zip(ds16, q2)]
            dv = [lax.dot_general(a.astype(BF16), b, tn, preferred_element_type=F32) for a, b in zip(p, do2)]
            parts = {}
            for k, (start, prev, first) in enumerate(blocks):
                dq_ref[_rows(start, d), :] += jnp.where(head0, dq2[k][:P], dq2[k][P:]) * scale
                parts.setdefault(id(start), [start, []])[1].append((dk[k][P:], dv[k][P:]))
                if not (isinstance(first, int) and first == 1):
                    parts.setdefault(id(prev), [prev, []])[1].append((dk[k][:P], dv[k][:P]))
            for start, terms in parts.values():
                rows = _rows(start, d)
                dk_ref[rows, :] += functools.reduce(jnp.add, [t[0] for t in terms])
                dv_ref[rows, :] += functools.reduce(jnp.add, [t[1] for t in terms])

        _for_each_block(S, ATTN_BWD_UNROLL, block)

    slab = lambda col0: pl.BlockSpec((None, S, LANES), lambda h, b: (b, 0, col0 + h))
    tab = pl.BlockSpec((nbr, 1, 2, 2 * P, 2 * P), lambda h, b: (0, h, 0, 0, 0))
    dtab = pl.BlockSpec((nbr, 1, 2 * P, 2 * P), lambda h, b: (0, h, 0, 0))
    shp = jax.ShapeDtypeStruct((B, S, D), F32)
    return _pallas(
        body, comm=comm, name=name,
        out_shape=(shp, shp, shp, jax.ShapeDtypeStruct((nbr, n_pairs, 2 * P, 2 * P), F32)),
        grid=(n_pairs, B),
        in_specs=[slab(0), slab(0), slab(n_pairs), slab(0), slab(0), slab(0), tab],
        out_specs=(slab(0), slab(0), slab(0), dtab),
        compiler_params=_params("parallel", "arbitrary"),
    )(q, kv, kv, delta, lse, do, bias)


def _adamw(w, g, m, v, name):
    R, C = w.shape
    tr = _tile(R, 256, SUBLANES_F32) if R % SUBLANES_F32 == 0 else R
    tc = _tile(C, 2048, LANES) if C % LANES == 0 else C

    def body(w_ref, g_ref, m_ref, v_ref, d_ref, nm_ref, nv_ref):
        g_ = g_ref[...]
        m2 = ADAM_B1 * m_ref[...] + (1.0 - ADAM_B1) * g_
        v2 = ADAM_B2 * v_ref[...] + (1.0 - ADAM_B2) * (g_ * g_)
        m_hat = m2 / (1.0 - ADAM_B1 ** ADAM_STEP)
        v_hat = v2 / (1.0 - ADAM_B2 ** ADAM_STEP)
        d_ref[...] = -ADAM_LR * (m_hat / (jnp.sqrt(v_hat) + ADAM_EPS) + ADAM_WD * w_ref[...])
        nm_ref[...] = m2
        nv_ref[...] = v2

    blk = pl.BlockSpec((tr, tc), lambda i, j: (i, j))
    shp = jax.ShapeDtypeStruct((R, C), F32)
    return pl.pallas_call(
        body, name=name, out_shape=(shp, shp, shp), grid=(R // tr, C // tc),
        in_specs=[blk] * 4, out_specs=(blk,) * 3, compiler_params=_params("parallel", "parallel"),
    )(w, g, m, v)


def _sum_slots(slots, name):
    n, R, C = slots.shape
    tr = _tile(R, 256, SUBLANES_BF16) if R % SUBLANES_BF16 == 0 else R
    tc = _tile(C, 2048, LANES) if C % LANES == 0 else C

    def body(s_ref, o_ref):
        acc = s_ref[0].astype(F32)
        for k in range(1, n):
            acc = acc + s_ref[k].astype(F32)
        o_ref[...] = acc

    return pl.pallas_call(
        body, name=name, out_shape=jax.ShapeDtypeStruct((R, C), F32), grid=(R // tr, C // tc),
        in_specs=[pl.BlockSpec((n, tr, tc), lambda i, j: (0, i, j))],
        out_specs=pl.BlockSpec((tr, tc), lambda i, j: (i, j)),
        compiler_params=_params("parallel", "parallel"),
    )(slots)


def _my_place():
    return lax.axis_index("x"), lax.axis_index("y"), lax.axis_index("c")


def _other_chips(x, y):
    return [(1 - x, y), (x, 1 - y), (1 - x, 1 - y)]


def _piece(ref, blk, axis, shard, half):
    h0 = blk[0] // 2
    idx = []
    for dim, n in enumerate(blk):
        if dim == 0:
            start = half * h0 + (shard * n if axis == 0 else 0)
            idx.append(pl.ds(start, h0))
        elif dim == axis:
            idx.append(pl.ds(shard * n, n))
        else:
            idx.append(slice(None))
    return ref.at[tuple(idx)]


def _half_of(ref, blk, half):
    return ref.at[pl.ds(half * (blk[0] // 2), blk[0] // 2)]


def _gather_weights(shards, axes, names, small):
    n = len(shards)
    blks = [_shard_shape(s) for s in shards]
    task = _gather_ici_task(shards, axes, names)

    def stage1(*refs):
        small_in, outs, small_out = refs[n], refs[n + 1:2 * n + 1], refs[2 * n + 1]
        send_sems, recv_sems, local_sems = refs[2 * n + 2:]
        sem = lambda k: (send_sems.at[k], recv_sems.at[k])
        x, y, c = _my_place()
        me = 2 * x + y
        local = pltpu.make_async_copy(small_in, small_out.at[me], local_sems.at[0])
        local.start()
        sends, recvs = task.copies(None, outs, sem)
        for k, (px, py) in enumerate(_other_chips(x, y)):
            sends.append(_remote(small_in, small_out.at[me], sem(task.n_sems + k), (px, py, c)))
            recvs.append(_remote(small_in, small_out.at[2 * px + py], sem(task.n_sems + k), (px, py, c)))
        for cp in sends:
            cp.start()
        for cp in recvs:
            cp.wait_recv()
        for cp in sends:
            cp.wait_send()
        local.wait()

    res = pl.pallas_call(
        stage1, name="gather_weights_ici",
        out_shape=task.outs + [jax.ShapeDtypeStruct((N_SHARDS,) + small.shape, small.dtype)],
        in_specs=[ANY] * (n + 1), out_specs=[ANY] * (n + 1), input_output_aliases={a: a for a in range(n)},
        scratch_shapes=[pltpu.SemaphoreType.DMA((task.n_sems + 3,)), pltpu.SemaphoreType.DMA((task.n_sems + 3,)),
                        pltpu.SemaphoreType.DMA((1,))],
    )(*task.ins, small)
    full = _run_comm(_gather_d2d_task(list(res[:n]), blks, axes), "gather_weights_d2d")
    return full, res[n]


def _run_comm(comm, name):
    n_in, n_out = len(comm.ins), len(comm.outs)

    def body(*refs):
        send_sems, recv_sems = refs[n_in + n_out:]
        sends, recvs = comm.copies(refs[:n_in], refs[n_in:n_in + n_out], lambda k: (send_sems.at[k], recv_sems.at[k]))
        for cp in sends:
            cp.start()
        for cp in recvs:
            cp.wait_recv()
        for cp in sends:
            cp.wait_send()

    return list(pl.pallas_call(
        body, name=name, out_shape=comm.outs, in_specs=[ANY] * n_in, out_specs=[ANY] * n_out,
        input_output_aliases=comm.aliases, scratch_shapes=[pltpu.SemaphoreType.DMA((comm.n_sems,))] * 2,
    )(*comm.ins))


def _exchange_small(small, extra, name):
    n_in, n_out = len(extra.ins), len(extra.outs)

    def body(*refs):
        ex_in, small_in = refs[:n_in], refs[n_in]
        ex_out, small_out = refs[n_in + 1:n_in + 1 + n_out], refs[n_in + 1 + n_out]
        send_sems, recv_sems, local_sems = refs[n_in + n_out + 2:]
        sem = lambda k: (send_sems.at[k], recv_sems.at[k])
        x, y, c = _my_place()
        me = 4 * x + 2 * y + c
        local = pltpu.make_async_copy(small_in, small_out.at[me], local_sems.at[0])
        local.start()
        sends, recvs = extra.copies(ex_in, ex_out, sem)
        for rel in range(1, N_DEVICES):
            px, py, pc = x ^ ((rel >> 2) & 1), y ^ ((rel >> 1) & 1), c ^ (rel & 1)
            k = extra.n_sems + rel - 1
            sends.append(_remote(small_in, small_out.at[me], sem(k), (px, py, pc)))
            recvs.append(_remote(small_in, small_out.at[4 * px + 2 * py + pc], sem(k), (px, py, pc)))
        for cp in sends:
            cp.start()
        for cp in recvs:
            cp.wait_recv()
        for cp in sends:
            cp.wait_send()
        local.wait()

    n_sems = extra.n_sems + N_DEVICES - 1
    res = pl.pallas_call(
        body, name=name, out_shape=extra.outs + [jax.ShapeDtypeStruct((N_DEVICES,) + small.shape, small.dtype)],
        in_specs=[ANY] * (n_in + 1), out_specs=[ANY] * (n_out + 1), input_output_aliases=extra.aliases,
        scratch_shapes=[pltpu.SemaphoreType.DMA((n_sems,)), pltpu.SemaphoreType.DMA((n_sems,)),
                        pltpu.SemaphoreType.DMA((1,))],
    )(*extra.ins, small)
    return list(res[:n_out]), res[n_out]


def _remote(src, dst, sems, device):
    return pltpu.make_async_remote_copy(src_ref=src, dst_ref=dst, send_sem=sems[0], recv_sem=sems[1],
                                        device_id=device, device_id_type=pl.DeviceIdType.MESH)


def _sum_piece(dest, slots, grad, blk, axis, layer, n_layers, name):
    r, c = blk
    h0 = r // 2
    tr = _tile(h0, 256, SUBLANES_BF16)
    tc = _tile(c, 2048, LANES)
    place = jnp.stack([2 * lax.axis_index("x") + lax.axis_index("y"), lax.axis_index("c")]).astype(jnp.int32)

    def body(p_ref, s_ref, g_ref, *rest):
        acc = g_ref[...].astype(F32) + s_ref[0].astype(F32)
        for k in range(1, N_DEVICES - 1):
            acc = acc + s_ref[k].astype(F32)
        rest[-1][...] = acc

    def g_map(i, j, p):
        return (p[1] * (h0 // tr) + (p[0] * (r // tr) if axis == 0 else 0) + i, (p[0] * (c // tc) if axis == 1 else 0) + j)

    in_specs = [pl.BlockSpec((N_DEVICES - 1, tr, tc), lambda i, j, p: (0, i, j)), pl.BlockSpec((tr, tc), g_map)]
    args = [place, slots, grad]
    if dest is not None:
        in_specs.append(ANY)
        args.append(dest)
    return pl.pallas_call(
        body, name=name, out_shape=jax.ShapeDtypeStruct((n_layers, r, c), F32),
        grid_spec=pltpu.PrefetchScalarGridSpec(
            num_scalar_prefetch=1, grid=(h0 // tr, c // tc), in_specs=in_specs,
            out_specs=pl.BlockSpec((None, tr, tc), lambda i, j, p: (layer, p[1] * (h0 // tr) + i, j))),
        input_output_aliases={3: 0} if dest is not None else {},
        compiler_params=_params("parallel", "parallel"),
    )(*args)


def _swap_halves_task(blocks):
    n = len(blocks)
    layers = [(a, l) for a, b in enumerate(blocks) for l in range(b.shape[0])]

    def copies(in_refs, out_refs, sem):
        x, y, c = _my_place()
        sends, recvs = [], []
        for k, (a, l) in enumerate(layers):
            blk = blocks[a].shape[1:]
            mine = _half_of(out_refs[a].at[l], blk, c)
            sends.append(_remote(mine, mine, sem(k), (x, y, 1 - c)))
            recvs.append(_remote(mine, _half_of(out_refs[a].at[l], blk, 1 - c), sem(k), (x, y, 1 - c)))
        return sends, recvs

    return _Comm(blocks, [jax.ShapeDtypeStruct(b.shape, b.dtype) for b in blocks], {a: a for a in range(n)},
                 len(layers), copies)


def _shard_shape(shard):
    return shard[0].shape[1:] if isinstance(shard, tuple) else shard.shape


def _place_shard(shard, axis, name):
    r, c = _shard_shape(shard)
    tr = _tile(r, 512, SUBLANES_BF16)
    full = (r * N_SHARDS, c) if axis == 0 else (r, c * N_SHARDS)
    me2 = (2 * lax.axis_index("x") + lax.axis_index("y")).astype(jnp.int32).reshape(1)

    def body(me_ref, s_ref, o_ref):
        o_ref[...] = s_ref[...].astype(o_ref.dtype)

    if axis == 0:
        out_map = lambda i, me: (me[0] * (r // tr) + i, 0)
    else:
        out_map = lambda i, me: (i, me[0])
    if isinstance(shard, tuple):
        src, layer = shard
        in_spec = pl.BlockSpec((None, tr, c), lambda i, me: (layer, i, 0))
    else:
        src, in_spec = shard, pl.BlockSpec((tr, c), lambda i, me: (i, 0))
    return pl.pallas_call(
        body, name=name, out_shape=jax.ShapeDtypeStruct(full, BF16),
        grid_spec=pltpu.PrefetchScalarGridSpec(
            num_scalar_prefetch=1, grid=(r // tr,), in_specs=[in_spec], out_specs=pl.BlockSpec((tr, c), out_map)),
        compiler_params=_params("parallel"),
    )(me2, src)


def _gather_ici_task(shards, axes, names):
    n = len(shards)
    blks = [_shard_shape(s) for s in shards]
    bases = [_place_shard(s, ax, f"place_{nm}") for s, ax, nm in zip(shards, axes, names)]

    def copies(in_refs, out_refs, sem):
        x, y, c = _my_place()
        me = 2 * x + y
        sends, recvs = [], []
        for a in range(n):
            mine = _piece(out_refs[a], blks[a], axes[a], me, c)
            for k, (px, py) in enumerate(_other_chips(x, y)):
                sends.append(_remote(mine, mine, sem(3 * a + k), (px, py, c)))
                recvs.append(_remote(mine, _piece(out_refs[a], blks[a], axes[a], 2 * px + py, c), sem(3 * a + k),
                                     (px, py, c)))
        return sends, recvs

    return _Comm(bases, [jax.ShapeDtypeStruct(b.shape, b.dtype) for b in bases], {a: a for a in range(n)}, 3 * n, copies)


def _gather_d2d_task(partials, blks, axes):
    n = len(partials)

    def copies(in_refs, out_refs, sem):
        x, y, c = _my_place()
        sends, recvs = [], []
        for a in range(n):
            for k, (px, py) in enumerate(_other_chips(x, y)):
                mine = _piece(out_refs[a], blks[a], axes[a], 2 * px + py, c)
                theirs = _piece(out_refs[a], blks[a], axes[a], 2 * px + py, 1 - c)
                sends.append(_remote(mine, mine, sem(3 * a + k), (x, y, 1 - c)))
                recvs.append(_remote(mine, theirs, sem(3 * a + k), (x, y, 1 - c)))
        return sends, recvs

    return _Comm(partials, [jax.ShapeDtypeStruct(p.shape, p.dtype) for p in partials], {a: a for a in range(n)},
                 3 * n, copies)


def _scatter_task(grads, blks, axes):
    n = len(grads)

    def copies(in_refs, out_refs, sem):
        x, y, c = _my_place()
        sends, recvs = [], []
        for rel in range(1, N_DEVICES):
            px, py, pc = x ^ ((rel >> 2) & 1), y ^ ((rel >> 1) & 1), c ^ (rel & 1)
            for a in range(n):
                src = _piece(in_refs[a], blks[a], axes[a], 2 * px + py, pc)
                k = (N_DEVICES - 1) * a + rel - 1
                sends.append(_remote(src, out_refs[a].at[rel - 1], sem(k), (px, py, pc)))
                recvs.append(_remote(src, out_refs[a].at[rel - 1], sem(k), (px, py, pc)))
        return sends, recvs

    outs = [jax.ShapeDtypeStruct((N_DEVICES - 1, b[0] // 2) + tuple(b[1:]), g.dtype) for b, g in zip(blks, grads)]
    return _Comm(grads, outs, {}, (N_DEVICES - 1) * n, copies)


def _pack(arrays):
    flat = jnp.concatenate([a.reshape(-1).astype(F32) for a in arrays])
    pad = (-flat.shape[0]) % (SUBLANES_F32 * LANES)
    return jnp.pad(flat, (0, pad)).reshape(-1, LANES)


def _unpack(packed, shapes):
    flat = packed.reshape(-1)
    out, off = [], 0
    for s in shapes:
        n = int(np.prod(s))
        out.append(flat[off:off + n].reshape(s))
        off += n
    return out


def _local_step(x, target, W, shards, geom, small):
    W = dict(W)
    B, S, D = x.shape
    T = B * S
    x2 = x.reshape(T, D)
    tgt = target.reshape(T, D)
    bmap = jnp.asarray(_bucket_map())
    blk = lambda names: [geom[k][0] for k in names]
    axs = lambda names: [geom[k][1] for k in names]
    ici = lambda names: _gather_ici_task([shards[k] for k in names], axs(names), names)
    d2d = lambda names, partials: _gather_d2d_task(list(partials), blk(names), axs(names))
    big, slots = {}, {}

    def scatter(names):
        return _scatter_task([big[k] for k in names], blk(names), axs(names))

    def ffn_bwd(l, dout, h, saved, first, second):
        xn, u, v, a = saved
        da = _mm_nt(dout, W[f"w_down{l}"], BF16, f"ffn{l}_down_dx")
        big[f"w_down{l}"] = _mm_tn(a, dout, f"ffn{l}_down_dw")
        names = first + [f"w_down{l}"]
        du, g_cw, g_cb, landed = _ffn_gate_bwd(u, v, small["ffn_conv"][l], da, S, f"ffn{l}_gate_bwd", comm=scatter(names))
        slots.update(zip(names, landed))
        big[f"w_up{l}"] = _mm_tn(xn, du, f"ffn{l}_up_dw")
        norm = (h, small["ffn_norm"][l], dout)
        if second:
            (dh, g_norm), (slots[f"w_up{l}"],) = _mm_nt(du, W[f"w_up{l}"], None, f"ffn{l}_up_dx",
                                                        comm=scatter([f"w_up{l}"]), norm=norm)
        else:
            dh, g_norm = _mm_nt(du, W[f"w_up{l}"], None, f"ffn{l}_up_dx", norm=norm)
        return dh, g_cw, g_cb, g_norm[0]

    xn0 = _rmsnorm_fwd(x2, small["a_norm"][0], "a_norm")
    p, part = _mm_nn(xn0, W["w_in"], None, BF16, "a_in", comm=ici(["w_up0"]))
    z, (W["w_up0"],) = _gate_a_fwd(p, small["a_conv"][0], S, "a_gate", comm=d2d(["w_up0"], part))
    (h1, xn1), part = _mm_nn(z, W["w_out"], x2, F32, "a_out", comm=ici(["w_down0"]), norm_gains=[small["ffn_norm"][0]])
    attn, ffn1 = ["w_kv", "w_q", "w_o"], ["w_up1", "w_down1"]
    u0, landed = _mm_nn(xn1, W["w_up0"], None, BF16, "ffn0_up", comm=_Comm.join([d2d(["w_down0"], part), ici(attn)]))
    W["w_down0"] = landed[0]
    (a0, v0), landed = _ffn_gate_fwd(u0, small["ffn_conv"][0], small["ffn_conv_b"][0], S, "ffn0_gate",
                                     comm=_Comm.join([d2d(attn, landed[1:]), ici(ffn1)]))
    W.update(zip(attn, landed[:len(attn)]))
    (h2, kvn, xn3), landed = _mm_nn(a0, W["w_down0"], h1, F32, "ffn0_down", comm=d2d(ffn1, landed[len(attn):]),
                                    norm_gains=[small["kv_norm"], small["b_norm"][0]])
    W.update(zip(ffn1, landed))
    kv = _mm_nn(kvn, W["w_kv"], None, F32, "kv_proj")
    q = _mm_nn(xn3, W["w_q"], None, F32, "q_proj")
    bias = _bias_tables(small["rel_bias"], bmap, "rel_bias_tables")
    q3, kv3 = q.reshape(B, S, D), kv.reshape(B, S, 2 * D)
    o3, lse3 = _attn_fwd(q3, kv3, bias, "attn_fwd")
    o = o3.reshape(T, D)
    h3, xn4 = _mm_nn(o, W["w_o"], h2, F32, "o_proj", norm_gains=[small["ffn_norm"][1]])
    u1 = _mm_nn(xn4, W["w_up1"], None, BF16, "ffn1_up")
    a1, v1 = _ffn_gate_fwd(u1, small["ffn_conv"][1], small["ffn_conv_b"][1], S, "ffn1_gate")
    sq_err, dh4, g_final = _loss_head(a1, W["w_down1"], h3, small["final_norm"], tgt, "ffn1_down_loss")
    loss = 0.5 * jnp.sum(sq_err) / D

    dh3, g_cw1, g_cb1, g_fn1 = ffn_bwd(1, dh4, h3, (xn4, u1, v1, a1), [], False)
    do, delta = _mm_nt(dh3, W["w_o"], F32, "o_proj_dx", head_dot=o)
    big["w_o"] = _mm_tn(o, dh3, "o_proj_dw")
    (dq3, dk3, dv3, dbias), landed = _attn_bwd(q3, kv3, delta.reshape(B, S, D), lse3, do.reshape(B, S, D), bias, "attn_bwd",
                                               comm=scatter(["w_up1", "w_o"]))
    slots.update(zip(["w_up1", "w_o"], landed))
    g_rel = _bias_grad(dbias, bmap, "rel_bias_grad")[:, :REL_BUCKETS].T
    dq, dk, dv = dq3.reshape(T, D), dk3.reshape(T, D), dv3.reshape(T, D)
    dh2, g_bn = _mm_nt(dq, W["w_q"], None, "q_proj_dx", norm=(h2, small["b_norm"][0], dh3))
    big["w_q"] = _mm_tn(xn3, dq, "q_proj_dw")
    dh2, g_kvn = _mm_nt([dk, dv], W["w_kv"], None, "kv_proj_dx", norm=(h2, small["kv_norm"], dh2))
    big["w_kv"] = jnp.concatenate([_mm_tn(kvn, dk, "k_proj_dw"), _mm_tn(kvn, dv, "v_proj_dw")], axis=1)
    dh1, g_cw0, g_cb0, g_fn0 = ffn_bwd(0, dh2, h1, (xn1, u0, v0, a0), ["w_q", "w_kv"], True)
    dz = _mm_nt(dh1, W["w_out"], BF16, "a_out_dx")
    big["w_out"] = _mm_tn(z, dh1, "a_out_dw")
    dp, g_aconv, (slots["w_out"],) = _gate_a_bwd(p, small["a_conv"][0], dz, S, "a_gate_bwd", comm=scatter(["w_out"]))
    big["w_in"] = _mm_tn(xn0, dp, "a_in_dw")
    (dx, g_an), (slots["w_in"],) = _mm_nt(dp, W["w_in"], None, "a_in_dx", comm=scatter(["w_in"]),
                                          norm=(x2, small["a_norm"][0], dh1))
    g_bn, g_kvn, g_an = g_bn[0], g_kvn[0], g_an[0]

    small_g = {"a_norm": g_an[None], "a_conv": g_aconv[None], "kv_norm": g_kvn, "b_norm": g_bn[None],
               "rel_bias": g_rel, "ffn_norm": jnp.stack([g_fn0, g_fn1]), "ffn_conv": jnp.stack([g_cw0, g_cw1]),
               "ffn_conv_b": jnp.stack([g_cb0, g_cb1]), "final_norm": g_final}
    return loss, dx.reshape(B, S, D), big, slots, small_g


BIG = ("w_in", "w_out", "w_kv", "w_q", "w_o", "w_up0", "w_up1", "w_down0", "w_down1")
SMALL = ("a_norm", "a_conv", "kv_norm", "b_norm", "rel_bias", "ffn_norm", "ffn_conv", "ffn_conv_b", "final_norm")
SMALL_SHARDED = ("a_norm", "a_conv", "ffn_conv")
WEIGHT_ORDER = ("a_norm", "a_w_in", "a_conv", "a_w_out", "kv_norm", "w_kv", "b_norm", "b_w_q", "b_w_o", "rel_bias",
                "ffn_norm", "ffn_w_up", "ffn_conv", "ffn_conv_b", "ffn_w_down", "final_norm")
GRAD_OF = {"w_in": ("a_w_in", 0), "w_out": ("a_w_out", 0), "w_kv": ("w_kv", 0), "w_q": ("b_w_q", 0), "w_o": ("b_w_o", 0),
           "w_up0": ("ffn_w_up", 0), "w_up1": ("ffn_w_up", 1), "w_down0": ("ffn_w_down", 0), "w_down1": ("ffn_w_down", 1)}


def _as2d(a):
    return a.reshape(-1, a.shape[-1])


def kernel(x, a_norm, a_w_in, a_conv, a_w_out, kv_norm, w_kv, b_norm, b_w_q, b_w_o, rel_bias, ffn_norm, ffn_w_up, ffn_conv, ffn_conv_b, ffn_w_down, final_norm, loss_target, m_a_norm, m_a_w_in, m_a_conv, m_a_w_out, m_kv_norm, m_w_kv, m_b_norm, m_b_w_q, m_b_w_o, m_rel_bias, m_ffn_norm, m_ffn_w_up, m_ffn_conv, m_ffn_conv_b, m_ffn_w_down, m_final_norm, v_a_norm, v_a_w_in, v_a_conv, v_a_w_out, v_kv_norm, v_w_kv, v_b_norm, v_b_w_q, v_b_w_o, v_rel_bias, v_ffn_norm, v_ffn_w_up, v_ffn_conv, v_ffn_conv_b, v_ffn_w_down, v_final_norm):
    given = dict(a_norm=a_norm, a_w_in=a_w_in, a_conv=a_conv, a_w_out=a_w_out, kv_norm=kv_norm, w_kv=w_kv, b_norm=b_norm,
                 b_w_q=b_w_q, b_w_o=b_w_o, rel_bias=rel_bias, ffn_norm=ffn_norm, ffn_w_up=ffn_w_up, ffn_conv=ffn_conv,
                 ffn_conv_b=ffn_conv_b, ffn_w_down=ffn_w_down, final_norm=final_norm)
    mom_m = dict(a_norm=m_a_norm, a_w_in=m_a_w_in, a_conv=m_a_conv, a_w_out=m_a_w_out, kv_norm=m_kv_norm, w_kv=m_w_kv,
                 b_norm=m_b_norm, b_w_q=m_b_w_q, b_w_o=m_b_w_o, rel_bias=m_rel_bias, ffn_norm=m_ffn_norm,
                 ffn_w_up=m_ffn_w_up, ffn_conv=m_ffn_conv, ffn_conv_b=m_ffn_conv_b, ffn_w_down=m_ffn_w_down,
                 final_norm=m_final_norm)
    mom_v = dict(a_norm=v_a_norm, a_w_in=v_a_w_in, a_conv=v_a_conv, a_w_out=v_a_w_out, kv_norm=v_kv_norm, w_kv=v_w_kv,
                 b_norm=v_b_norm, b_w_q=v_b_w_q, b_w_o=v_b_w_o, rel_bias=v_rel_bias, ffn_norm=v_ffn_norm,
                 ffn_w_up=v_ffn_w_up, ffn_conv=v_ffn_conv, ffn_conv_b=v_ffn_conv_b, ffn_w_down=v_ffn_w_down,
                 final_norm=v_final_norm)

    shard = {"w_in": ((a_w_in, 0), 1), "w_out": ((a_w_out, 0), 0), "w_kv": (w_kv, 1), "w_q": ((b_w_q, 0), 0),
             "w_o": ((b_w_o, 0), 0), "w_up0": ((ffn_w_up, 0), 1), "w_up1": ((ffn_w_up, 1), 1),
             "w_down0": ((ffn_w_down, 0), 0), "w_down1": ((ffn_w_down, 1), 0)}

    small_sharded = [given[k] for k in SMALL_SHARDED]
    packed = _pack(small_sharded)
    first = ("w_in", "w_out")
    fulls, packed_all = _gather_weights([shard[k][0] for k in first], [shard[k][1] for k in first], first, packed)
    W = dict(zip(first, fulls))
    later = {k: shard[k][0] for k in BIG if k not in first}
    geom = {k: (_shard_shape(shard[k][0]), shard[k][1]) for k in BIG}
    small = {k: given[k] for k in SMALL}
    per_shard = [_unpack(packed_all[j], [a.shape for a in small_sharded]) for j in range(N_SHARDS)]
    for i, k in enumerate(SMALL_SHARDED):
        small[k] = jnp.concatenate([per_shard[j][i] for j in range(N_SHARDS)], axis=-1)

    loss, grad_x, big_g, slots, small_g = _local_step(x, loss_target, W, later, geom, small)

    small_shapes = [small_g[k].shape for k in SMALL] + [(1,)]
    layers_of = {}
    for k in BIG:
        layers_of.setdefault(GRAD_OF[k][0], []).append(k)
    blocks = {}
    for name, members in layers_of.items():
        dest = None
        for k in members:
            dest = _sum_piece(dest, slots[k], big_g[k], geom[k][0], geom[k][1], GRAD_OF[k][1], len(members), f"sum_{k}")
        blocks[name] = dest
    swapped, small_slots = _exchange_small(_pack([small_g[k] for k in SMALL] + [loss.reshape(1)]),
                                           _swap_halves_task(list(blocks.values())), "swap_halves_exchange_small")
    reduced = dict(zip(blocks, swapped))
    *small_sums, loss = _unpack(_sum_slots(small_slots, "sum_small"), small_shapes)
    loss = loss[0]
    small_red = dict(zip(SMALL, small_sums))
    j = 2 * lax.axis_index("x") + lax.axis_index("y")
    for k in SMALL_SHARDED:
        w = given[k].shape[-1]
        small_red[k] = lax.dynamic_slice_in_dim(small_red[k], j * w, w, axis=small_red[k].ndim - 1)

    grads, deltas, new_m, new_v = {}, {}, {}, {}
    for name in WEIGHT_ORDER:
        if name in reduced:
            g = reduced[name].reshape(given[name].shape)
            d, nm, nv = _adamw(_as2d(given[name]), _as2d(g), _as2d(mom_m[name]), _as2d(mom_v[name]), f"adamw_{name}")
            grads[name] = g
            deltas[name], new_m[name], new_v[name] = (t.reshape(given[name].shape) for t in (d, nm, nv))
    small_names = [n for n in WEIGHT_ORDER if n not in reduced]
    for n in small_names:
        grads[n] = small_red[n].reshape(given[n].shape)
    sw, sg, sm, sv = (_pack([d[n] for n in small_names]) for d in (given, grads, mom_m, mom_v))
    d, nm, nv = _adamw(sw, sg, sm, sv, "adamw_small")
    shapes = [given[n].shape for n in small_names]
    for n, a, b_, c_ in zip(small_names, _unpack(d, shapes), _unpack(nm, shapes), _unpack(nv, shapes)):
        deltas[n], new_m[n], new_v[n] = a, b_, c_

    return (loss, grad_x, *[grads[n] for n in WEIGHT_ORDER], *[deltas[n] for n in WEIGHT_ORDER],
            *[new_m[n] for n in WEIGHT_ORDER], *[new_v[n] for n in WEIGHT_ORDER])
```
